```python
import jax, jax.numpy as jnp
from jax import lax
import numpy as np

D_MODEL = 1024
BATCH = 16
SEQ = 256
DEPTH = 2
DEC_BATCH = 4
DEC_SEQ = 1024
PAST_LEN = 256

GRID_W = 64
N_HEADS = 16
N_KV = 4
HEAD_DIM = 64
D_ATT = N_HEADS * HEAD_DIM
D_RNN = D_MODEL
N_RG_BLK = 8
RG_BLK = D_RNN // N_RG_BLK
CONV_W = 4
RG_C = 8.0
N_EXPERTS = 16
N_GROUPS = 4
EXP_PER_GROUP = N_EXPERTS // N_GROUPS
TOP_K = 2
D_EXPERT = 512
Q_BLK = 128
ROPE_THETA = 10000.0
EPS = 1e-6
SPLITS = (D_RNN, 2 * D_RNN, 2 * D_RNN + D_ATT, 2 * D_RNN + D_ATT + N_KV * HEAD_DIM,
          2 * D_RNN + D_ATT + 2 * N_KV * HEAD_DIM, 2 * D_RNN + D_ATT + 2 * N_KV * HEAD_DIM + D_MODEL)
P_IN = 2 * D_RNN + D_ATT + 2 * N_KV * HEAD_DIM + 2 * D_MODEL

kernel_name = "hybrid_rglru_gqa_groupmoe_dit_step"


def rmsnorm(x, g):
    xf = x.astype(jnp.float32)
    y = xf * lax.rsqrt(jnp.mean(xf * xf, axis=-1, keepdims=True) + EPS)
    return (y * g.astype(jnp.float32)).astype(x.dtype)


def modulation(cvec, w, b):
    m = jax.nn.silu(cvec) @ w + b
    return [t[:, None, :] for t in jnp.split(m, 6, axis=-1)]


def conv_centred(x, w, b):
    s = x.shape[1]
    left = CONV_W // 2
    xp = jnp.pad(x, ((0, 0), (left, CONV_W - 1 - left), (0, 0)))
    y = b
    for j in range(CONV_W):
        y = y + xp[:, j:j + s, :] * w[j]
    return y


def _lin_combine(e1, e2):
    a1, b1 = e1
    a2, b2 = e2
    return a1 * a2, a2 * b1 + b2


def rg_lru_scan(x, h0, wa, ba, wx, bx, lam, reverse):
    b_, s, _ = x.shape
    xb = x.reshape(b_, s, N_RG_BLK, RG_BLK)
    r = jax.nn.sigmoid(jnp.einsum('bsnk,nkj->bsnj', xb, wa).reshape(b_, s, D_RNN) + ba)
    i = jax.nn.sigmoid(jnp.einsum('bsnk,nkj->bsnj', xb, wx).reshape(b_, s, D_RNN) + bx)
    log_a = (-RG_C * r.astype(jnp.float32)) * jax.nn.softplus(-lam.astype(jnp.float32))
    a = jnp.exp(log_a)
    inp = jnp.sqrt(-jnp.expm1(2.0 * log_a)) * (i * x).astype(jnp.float32)
    a_cum, b_cum = lax.associative_scan(_lin_combine, (a, inp), reverse=reverse, axis=1)
    return a_cum * h0.astype(jnp.float32)[:, None, :] + b_cum


def rope_2d(n, dtype):
    rows = n // GRID_W
    pos_row = jnp.repeat(jnp.arange(rows, dtype=jnp.float32), GRID_W)
    pos_col = jnp.tile(jnp.arange(GRID_W, dtype=jnp.float32), rows)
    half = HEAD_DIM // 2
    inv_freq = ROPE_THETA ** (-jnp.arange(0, half, 2, dtype=jnp.float32) / half)
    ang = jnp.concatenate([pos_row[:, None] * inv_freq, pos_col[:, None] * inv_freq], axis=-1)
    return jnp.cos(ang).astype(dtype), jnp.sin(ang).astype(dtype)


def apply_rope(x, cos, sin):
    half = HEAD_DIM // 2
    x1, x2 = x[..., :half], x[..., half:]
    return jnp.concatenate([x1 * cos - x2 * sin, x2 * cos + x1 * sin], axis=-1)


def block_attention(q, k, v):
    b_, h, sq, dh = q.shape
    g = h // N_KV
    nq = sq // Q_BLK
    qb = q.reshape(b_, N_KV, g, nq, Q_BLK, dh).transpose(3, 0, 1, 2, 4, 5)
    scale = HEAD_DIM ** -0.5

    def one_block(qblk):
        s = jnp.einsum('bkgqd,bksd->bkgqs', qblk, k).astype(jnp.float32) * scale
        p = jax.nn.softmax(s, axis=-1).astype(v.dtype)
        return jnp.einsum('bkgqs,bksd->bkgqd', p, v)

    o = lax.map(one_block, qb)
    return o.transpose(1, 2, 3, 0, 4, 5).reshape(b_, h, sq, dh)


def mixer(h, lp, rope_cs, past_k, past_v, h0_f, h0_b):
    b_, s, _ = h.shape
    proj = h @ lp['w_in']
    xr, gate, q, k, v, gr, ga = jnp.split(proj, SPLITS, axis=-1)
    xc = conv_centred(xr, lp['conv_w'], lp['conv_b'])
    hf = rg_lru_scan(xc, h0_f, lp['rg_wa'][0], lp['rg_ba'][0], lp['rg_wx'][0], lp['rg_bx'][0], lp['rg_lambda'][0], False)
    hb = rg_lru_scan(xc, h0_b, lp['rg_wa'][1], lp['rg_ba'][1], lp['rg_wx'][1], lp['rg_bx'][1], lp['rg_lambda'][1], True)
    y_rec = ((hf + hb) * jax.nn.gelu(gate.astype(jnp.float32))).astype(h.dtype)
    b_rec = y_rec @ lp['w_rec_out']
    q = rmsnorm(q.reshape(b_, s, N_HEADS, HEAD_DIM), lp['q_norm_g']).transpose(0, 2, 1, 3)
    k = rmsnorm(k.reshape(b_, s, N_KV, HEAD_DIM), lp['k_norm_g']).transpose(0, 2, 1, 3)
    v = v.reshape(b_, s, N_KV, HEAD_DIM).transpose(0, 2, 1, 3)
    if rope_cs is None:
        o = block_attention(q, k, v)
    else:
        cos, sin = rope_cs
        k_all = jnp.concatenate([past_k, apply_rope(k, cos, sin)], axis=2)
        v_all = jnp.concatenate([past_v, v], axis=2)
        o = block_attention(apply_rope(q, cos, sin), k_all, v_all)
    o = o.transpose(0, 2, 1, 3).reshape(b_, s, D_ATT)
    b_att = o @ lp['w_att_out']
    merged = jax.nn.sigmoid(gr) * b_rec + jax.nn.sigmoid(ga) * b_att
    out = merged @ lp['w_out']
    if rope_cs is None:
        state = jnp.stack([hf[:, -1], hb[:, 0]], axis=1).astype(h.dtype)
        return out, k, v, state
    return out


def moe(h, w_router, router_bias, w_gate_e, w_up_e, w_down_e):
    b_, s, d = h.shape
    t = h.reshape(b_ * s, d)
    probs = jax.nn.softmax((t @ w_router).astype(jnp.float32), axis=-1)
    sel = probs + router_bias.astype(jnp.float32)
    grp_score = lax.top_k(sel.reshape(-1, N_GROUPS, EXP_PER_GROUP), TOP_K)[0].sum(-1)
    best = jnp.argmax(grp_score, axis=-1)
    in_grp = (jnp.arange(N_EXPERTS) // EXP_PER_GROUP)[None, :] == best[:, None]
    _, idx = lax.top_k(jnp.where(in_grp, sel, -jnp.inf), TOP_K)
    w = jnp.take_along_axis(probs, idx, axis=-1)
    w = w / jnp.sum(w, axis=-1, keepdims=True)
    combine = jnp.einsum('tk,tke->te', w, jax.nn.one_hot(idx, N_EXPERTS, dtype=jnp.float32)).astype(h.dtype)
    g = jnp.einsum('td,edf->tef', t, w_gate_e)
    u = jnp.einsum('td,edf->tef', t, w_up_e)
    act = jax.nn.silu(g) * u * combine[:, :, None]
    y = jnp.einsum('tef,efd->td', act, w_down_e)
    return y.reshape(b_, s, d)


def layer(x, mods, lp, rope_cs, past_k, past_v, h0_f, h0_b, w_router, router_bias):
    sh1, sc1, g1, sh2, sc2, g2 = mods
    h = rmsnorm(x, lp['norm1_g']) * (1 + sc1) + sh1
    res = mixer(h, lp, rope_cs, past_k, past_v, h0_f, h0_b)
    if rope_cs is None:
        out, k, v, st = res
    else:
        out = res
    x = x + g1 * out
    h = rmsnorm(x, lp['norm2_g']) * (1 + sc2) + sh2
    x = x + g2 * moe(h, w_router, router_bias, lp['w_gate_e'], lp['w_up_e'], lp['w_down_e'])
    if rope_cs is None:
        return x, k, v, st
    return x


def setup_inputs(seed: int = 0) -> dict:
    key = jax.random.key(seed)
    ks = jax.random.split(key, 32)
    f32 = jnp.float32
    nrm = lambda k, shape, sc: jax.random.normal(k, shape, f32) * sc
    u = jax.random.uniform(ks[15], (DEPTH, 2, D_RNN), f32, 0.9, 0.999) ** (1.0 / RG_C)
    return {
        'x_prompt': nrm(ks[0], (BATCH, SEQ, D_MODEL), 1.0),
        'x_sample': nrm(ks[1], (DEC_BATCH, DEC_SEQ, D_MODEL), 1.0),
        'cache_k': nrm(ks[2], (DEC_BATCH, DEPTH, N_KV, PAST_LEN, HEAD_DIM), 1.0),
        'cache_v': nrm(ks[3], (DEC_BATCH, DEPTH, N_KV, PAST_LEN, HEAD_DIM), 1.0),
        'state_rec': nrm(ks[4], (DEC_BATCH, DEPTH, 2, D_RNN), 0.5),
        'c': nrm(ks[5], (DEC_BATCH, D_MODEL), 1.0),
        'c_ctx': nrm(ks[6], (D_MODEL,), 1.0),
        'w_mod': nrm(ks[7], (DEPTH, D_MODEL, 6 * D_MODEL), 0.5 * D_MODEL ** -0.5),
        'b_mod': nrm(ks[8], (DEPTH, 6 * D_MODEL), 0.01),
        'norm1_g': 1.0 + nrm(ks[9], (DEPTH, D_MODEL), 0.01),
        'norm2_g': 1.0 + nrm(ks[10], (DEPTH, D_MODEL), 0.01),
        'w_in': nrm(ks[11], (DEPTH, D_MODEL, P_IN), D_MODEL ** -0.5),
        'conv_w': nrm(ks[12], (DEPTH, CONV_W, D_RNN), CONV_W ** -0.5),
        'conv_b': nrm(ks[13], (DEPTH, D_RNN), 0.01),
        'rg_wa': nrm(ks[14], (DEPTH, 2, N_RG_BLK, RG_BLK, RG_BLK), RG_BLK ** -0.5),
        'rg_ba': nrm(ks[16], (DEPTH, 2, D_RNN), 0.01),
        'rg_wx': nrm(ks[17], (DEPTH, 2, N_RG_BLK, RG_BLK, RG_BLK), RG_BLK ** -0.5),
        'rg_bx': nrm(ks[18], (DEPTH, 2, D_RNN), 0.01),
        'rg_lambda': jnp.log(u / (1.0 - u)),
        'q_norm_g': 1.0 + nrm(ks[19], (DEPTH, HEAD_DIM), 0.01),
        'k_norm_g': 1.0 + nrm(ks[20], (DEPTH, HEAD_DIM), 0.01),
        'w_rec_out': nrm(ks[21], (DEPTH, D_RNN, D_MODEL), D_RNN ** -0.5),
        'w_att_out': nrm(ks[22], (DEPTH, D_ATT, D_MODEL), D_ATT ** -0.5),
        'w_out': nrm(ks[23], (DEPTH, D_MODEL, D_MODEL), D_MODEL ** -0.5),
        'w_router': nrm(ks[24], (D_MODEL, N_EXPERTS), D_MODEL ** -0.5),
        'router_bias': nrm(ks[25], (N_EXPERTS,), 0.01),
        'w_gate_e': nrm(ks[26], (DEPTH, N_EXPERTS, D_MODEL, D_EXPERT), D_MODEL ** -0.5),
        'w_up_e': nrm(ks[27], (DEPTH, N_EXPERTS, D_MODEL, D_EXPERT), D_MODEL ** -0.5),
        'w_down_e': nrm(ks[28], (DEPTH, N_EXPERTS, D_EXPERT, D_MODEL), D_EXPERT ** -0.5),
        'final_g': 1.0 + nrm(ks[29], (D_MODEL,), 0.01),
    }


def reference(x_prompt, x_sample, cache_k, cache_v, state_rec, c, c_ctx, w_mod, b_mod, norm1_g, norm2_g,
              w_in, conv_w, conv_b, rg_wa, rg_ba, rg_wx, rg_bx, rg_lambda, q_norm_g, k_norm_g,
              w_rec_out, w_att_out, w_out, w_router, router_bias, w_gate_e, w_up_e, w_down_e, final_g):
    rope_cs = rope_2d(x_sample.shape[1], x_sample.dtype)
    xp, xs = x_prompt, x_sample
    new_k, new_v, new_s = [], [], []
    for l in range(DEPTH):
        lp = {'norm1_g': norm1_g[l], 'norm2_g': norm2_g[l], 'w_in': w_in[l], 'conv_w': conv_w[l], 'conv_b': conv_b[l],
              'rg_wa': rg_wa[l], 'rg_ba': rg_ba[l], 'rg_wx': rg_wx[l], 'rg_bx': rg_bx[l], 'rg_lambda': rg_lambda[l],
              'q_norm_g': q_norm_g[l], 'k_norm_g': k_norm_g[l], 'w_rec_out': w_rec_out[l], 'w_att_out': w_att_out[l],
              'w_out': w_out[l], 'w_gate_e': w_gate_e[l], 'w_up_e': w_up_e[l], 'w_down_e': w_down_e[l]}
        mods_ctx = modulation(c_ctx[None, :], w_mod[l], b_mod[l])
        mods_lat = modulation(c, w_mod[l], b_mod[l])
        zeros = jnp.zeros((xp.shape[0], D_RNN), xp.dtype)
        xp, k_l, v_l, st_l = layer(xp, mods_ctx, lp, None, None, None, zeros, zeros, w_router, router_bias)
        new_k.append(k_l)
        new_v.append(v_l)
        new_s.append(st_l)
        xs = layer(xs, mods_lat, lp, rope_cs, cache_k[:, l], cache_v[:, l], state_rec[:, l, 0], state_rec[:, l, 1],
                   w_router, router_bias)
    y_prompt = rmsnorm(xp, final_g)
    y_sample = rmsnorm(xs, final_g)
    new_cache_k = jnp.stack(new_k, axis=1)
    new_cache_v = jnp.stack(new_v, axis=1)
    new_state_rec = jnp.stack(new_s, axis=1)
    return (y_prompt, y_sample, new_cache_k, new_cache_v, new_state_rec)
```

```python
import functools

import numpy as np
import jax
import jax.numpy as jnp
from jax import lax
from jax.experimental import pallas as pl
from jax.experimental.pallas import tpu as pltpu

F32 = jnp.float32
BF16 = jnp.bfloat16

D = 1024
BATCH = 16
SEQ = 256
DEPTH = 2
DEC_BATCH = 4
DEC_SEQ = 1024
PAST = 256
GRID_W = 64
N_HEADS = 16
N_KV = 4
HD = 64
N_RG_BLK = 8
RG_BLK = 128
RG_C = 8.0
N_EXP = 16
D_EXP = 512
ROPE_THETA = 10000.0
EPS = 1e-6
P_IN = 5632

T_CTX = BATCH * SEQ
T_LAT = DEC_BATCH * DEC_SEQ
T = T_CTX + T_LAT
SEG = 256
N_SEG = T // SEG
UNIT = 1024
N_UNIT = T // UNIT
CHUNK = UNIT // 8

VMEM_LIMIT = 56 * 1024 * 1024


def _cp(sem):
    return pltpu.CompilerParams(dimension_semantics=sem, vmem_limit_bytes=VMEM_LIMIT)


def _split(x):
    hi = x.astype(BF16)
    lo = (x - hi.astype(F32)).astype(BF16)
    return hi, lo


def _dot(a, b):
    return jnp.dot(a, b, preferred_element_type=F32)


def _dot_nt(a, b):
    return lax.dot_general(a, b, (((1,), (1,)), ((), ())), preferred_element_type=F32)


def _mods_kernel(c_ref, w_ref, b_ref, o_ref):
    c = c_ref[...]
    s = c * jax.nn.sigmoid(c)
    s_hi, s_lo = _split(s)
    w_hi, w_lo = _split(w_ref[...])
    o_ref[...] = _dot(s_hi, w_hi) + _dot(s_hi, w_lo) + _dot(s_lo, w_hi) + b_ref[...]


def _mods(cvecs, w_mod, b_mod):
    tn = 1536
    return pl.pallas_call(
        _mods_kernel,
        grid=(DEPTH, 6 * D // tn),
        in_specs=[
            pl.BlockSpec((8, D), lambda l, j: (0, 0)),
            pl.BlockSpec((None, D, tn), lambda l, j: (l, 0, j)),
            pl.BlockSpec((None, 1, tn), lambda l, j: (l, 0, j)),
        ],
        out_specs=pl.BlockSpec((None, 8, tn), lambda l, j: (l, 0, j)),
        out_shape=jax.ShapeDtypeStruct((DEPTH, 8, 6 * D), F32),
        compiler_params=_cp(("arbitrary", "arbitrary")),
        name="mods",
    )(cvecs, w_mod, b_mod.reshape(DEPTH, 1, 6 * D))


def _norm_mod(x, g, shift, scale):
    ms = jnp.mean(x * x, axis=-1, keepdims=True)
    return x * lax.rsqrt(ms + EPS) * g * (1.0 + scale) + shift


def _inproj_kernel(x_ref, mod_ref, g_ref, w_ref, o_ref, h_ref, *, tm):
    @pl.when(pl.program_id(1) == 0)
    def _():
        def seg(s, carry):
            r0 = pl.multiple_of(s * SEG, SEG)
            m = mod_ref[s]
            h = _norm_mod(x_ref[pl.ds(r0, SEG), :], g_ref[...], m[0:1, :], m[1:2, :])
            h_ref[pl.ds(r0, SEG), :] = h.astype(BF16)
            return carry
        lax.fori_loop(0, tm // SEG, seg, 0)

    o_ref[...] = _dot(h_ref[...], w_ref[...].astype(BF16))


def _inproj(x, modseg, norm_g, w_in, l):
    tm, tn = 1024, 512
    return pl.pallas_call(
        functools.partial(_inproj_kernel, tm=tm),
        grid=(T // tm, P_IN // tn),
        in_specs=[
            pl.BlockSpec((tm, D), lambda i, j: (i, 0)),
            pl.BlockSpec((None, tm // SEG, 8, D), lambda i, j: (l, i, 0, 0)),
            pl.BlockSpec((None, 1, D), lambda i, j: (l, 0, 0)),
            pl.BlockSpec((None, D, tn), lambda i, j: (l, 0, j)),
        ],
        out_specs=pl.BlockSpec((tm, tn), lambda i, j: (i, j)),
        out_shape=jax.ShapeDtypeStruct((T, P_IN), F32),
        scratch_shapes=[pltpu.VMEM((tm, D), BF16)],
        compiler_params=_cp(("arbitrary", "arbitrary")),
        name="inproj",
    )(x, modseg, norm_g.reshape(DEPTH, 1, D), w_in)


REC_CW = 512


def _rec_kernel(xr_ref, gate_ref, cw_ref, pv_ref, wg_ref, h0_ref,
                y_ref, stf_ref, stb_ref,
                af_ref, bf_ref, ab_ref, bb_ref):
    u = pl.program_id(0)
    is_ctx = u < (T_CTX // UNIT)
    seq_len = jnp.where(is_ctx, SEQ, DEC_SEQ)
    nblk = REC_CW // RG_BLK

    pv = pv_ref[...]
    cwts = cw_ref[...]
    conv_b = pv[6:7, :]

    def softplus_neg(lam):
        z = -lam
        return jnp.maximum(z, 0.0) + jnp.log1p(jnp.exp(-jnp.abs(z)))

    sp = (softplus_neg(pv[4:5, :]), softplus_neg(pv[5:6, :]))
    a_refs = (af_ref, ab_ref)
    b_refs = (bf_ref, bb_ref)

    def gates(ci, carry):
        base = pl.multiple_of(ci * CHUNK, CHUNK)
        lo = pl.multiple_of(jnp.maximum(base - 8, 0), 8)
        hi = pl.multiple_of(jnp.minimum(base + CHUNK, UNIT - 8), 8)
        main = xr_ref[pl.ds(base, CHUNK), :]
        win = jnp.concatenate([xr_ref[pl.ds(lo, 8), :], main, xr_ref[pl.ds(hi, 8), :]], axis=0)
        t = base + lax.broadcasted_iota(jnp.int32, (CHUNK, 1), 0)
        tl = jnp.bitwise_and(t, seq_len - 1)
        n_win = CHUNK + 16
        xm2 = jnp.where(tl >= 2, pltpu.roll(win, 2, 0)[8:8 + CHUNK], 0.0)
        xm1 = jnp.where(tl >= 1, pltpu.roll(win, 1, 0)[8:8 + CHUNK], 0.0)
        xp1 = jnp.where(tl <= seq_len - 2, pltpu.roll(win, n_win - 1, 0)[8:8 + CHUNK], 0.0)
        xc = conv_b + xm2 * cwts[0:1, :]
        xc = xc + xm1 * cwts[1:2, :]
        xc = xc + main * cwts[2:3, :]
        xc = xc + xp1 * cwts[3:4, :]
        for n in range(nblk):
            ls = slice(n * RG_BLK, (n + 1) * RG_BLK)
            xn = xc[:, ls]
            pre = _dot(xn.astype(BF16), wg_ref[n].astype(BF16))
            for d in range(2):
                r = jax.nn.sigmoid(pre[:, (2 * d) * RG_BLK:(2 * d + 1) * RG_BLK] + pv[2 * d:2 * d + 1, ls])
                i = jax.nn.sigmoid(pre[:, (2 * d + 1) * RG_BLK:(2 * d + 2) * RG_BLK] + pv[2 * d + 1:2 * d + 2, ls])
                log_a = (-RG_C * r) * sp[d][:, ls]
                a = jnp.exp(log_a)
                inp = jnp.sqrt(-jnp.tanh(log_a) * (a * a + 1.0)) * (i * xn)
                a_refs[d][n, pl.ds(base, CHUNK), :] = a
                b_refs[d][n, pl.ds(base, CHUNK), :] = inp
        return carry

    lax.fori_loop(0, 8, gates, 0)

    zeros = [jnp.zeros((8, RG_BLK), F32)] * nblk
    ones = [jnp.ones((8, RG_BLK), F32)] * nblk
    hf, pf, hb, pb = list(zeros), list(ones), list(zeros), list(ones)
    for r in range(CHUNK):
        rows_f = pl.ds(r, 8, stride=CHUNK)
        rows_b = pl.ds(CHUNK - 1 - r, 8, stride=CHUNK)
        for n in range(nblk):
            a = af_ref[n, rows_f, :]
            hf[n] = a * hf[n] + bf_ref[n, rows_f, :]
            pf[n] = a * pf[n]
            bf_ref[n, rows_f, :] = hf[n]
            af_ref[n, rows_f, :] = pf[n]
            a = ab_ref[n, rows_b, :]
            hb[n] = a * hb[n] + bb_ref[n, rows_b, :]
            pb[n] = a * pb[n]
            bb_ref[n, rows_b, :] = hb[n]
            ab_ref[n, rows_b, :] = pb[n]
    hf, pf, hb, pb = (jnp.concatenate(v, axis=-1) for v in (hf, pf, hb, pb))

    cps = jnp.where(is_ctx, SEQ // CHUNK, DEC_SEQ // CHUNK)
    h0f = h0_ref[0:1, :]
    h0b = h0_ref[1:2, :]
    cf = [h0f]
    for c in range(1, 8):
        chain = hf[c - 1:c, :] + pf[c - 1:c, :] * cf[c - 1]
        cf.append(jnp.where(jnp.bitwise_and(c, cps - 1) == 0, h0f, chain))
    cb = [None] * 8
    cb[7] = h0b
    for c in range(6, -1, -1):
        chain = hb[c + 1:c + 2, :] + pb[c + 1:c + 2, :] * cb[c + 1]
        cb[c] = jnp.where(jnp.bitwise_and(c, cps - 1) == cps - 1, h0b, chain)
    carry_f = jnp.concatenate(cf, axis=0)
    carry_b = jnp.concatenate(cb, axis=0)
    stf_ref[...] = hf + pf * carry_f
    stb_ref[...] = hb + pb * carry_b

    for ci in range(8):
        rows = pl.ds(ci * CHUNK, CHUNK)
        for n in range(nblk):
            ls = slice(n * RG_BLK, (n + 1) * RG_BLK)
            h_f = bf_ref[n, rows, :] + af_ref[n, rows, :] * carry_f[ci:ci + 1, ls]
            h_b = bb_ref[n, rows, :] + ab_ref[n, rows, :] * carry_b[ci:ci + 1, ls]
            g = gate_ref[rows, ls]
            y_ref[rows, ls] = ((h_f + h_b) * jax.nn.gelu(g, approximate=True)).astype(BF16)


def _rec(proj, conv_w, pvec, wg, h0, l):
    ncb = D // REC_CW
    return pl.pallas_call(
        _rec_kernel,
        grid=(N_UNIT, ncb),
        in_specs=[
            pl.BlockSpec((UNIT, REC_CW), lambda u, c: (u, c)),
            pl.BlockSpec((UNIT, REC_CW), lambda u, c: (u, ncb + c)),
            pl.BlockSpec((None, 4, REC_CW), lambda u, c: (l, 0, c)),
            pl.BlockSpec((None, 8, REC_CW), lambda u, c: (l, 0, c)),
            pl.BlockSpec((None, REC_CW // RG_BLK, RG_BLK, 4 * RG_BLK), lambda u, c: (l, c, 0, 0)),
            pl.BlockSpec((None, 2, REC_CW), lambda u, c: (u, 0, c)),
        ],
        out_specs=[
            pl.BlockSpec((UNIT, REC_CW), lambda u, c: (u, c)),
            pl.BlockSpec((None, 8, REC_CW), lambda u, c: (u, 0, c)),
            pl.BlockSpec((None, 8, REC_CW), lambda u, c: (u, 0, c)),
        ],
        out_shape=[
            jax.ShapeDtypeStruct((T, D), BF16),
            jax.ShapeDtypeStruct((N_UNIT, 8, D), F32),
            jax.ShapeDtypeStruct((N_UNIT, 8, D), F32),
        ],
        scratch_shapes=[pltpu.VMEM((REC_CW // RG_BLK, UNIT, RG_BLK), F32)] * 4,
        compiler_params=_cp(("arbitrary", "arbitrary")),
        name="rec",
    )(proj, proj, conv_w, pvec, wg, h0)


def _head_norm(x, g128, bd):
    hi, lo = _split(x * x)
    ms = _dot(hi, bd) + _dot(lo, bd)
    return x * lax.rsqrt(ms + EPS) * g128


def _rope(x, cos, sin_signed):
    lane = lax.broadcasted_iota(jnp.int32, x.shape, 1)
    first_half = jnp.bitwise_and(lane, HD - 1) < HD // 2
    partner = jnp.where(first_half, pltpu.roll(x, 2 * HD - HD // 2, 1), pltpu.roll(x, HD // 2, 1))
    return x * cos + partner * sin_signed


def _qkv_kernel(q_ref, k_ref, v_ref, qg_ref, kg_ref, cos_ref, sin_ref, bd_ref,
                qo_ref, ko_ref, vo_ref, *, rope):
    bd = bd_ref[...]
    scale = HD ** -0.5
    for j in range(N_HEADS // 2):
        x = _head_norm(q_ref[:, 2 * HD * j:2 * HD * (j + 1)], qg_ref[...], bd)
        if rope:
            x = _rope(x, cos_ref[...], sin_ref[...])
        x = x * scale
        qo_ref[2 * j] = x[:, :HD].astype(qo_ref.dtype)
        qo_ref[2 * j + 1] = x[:, HD:].astype(qo_ref.dtype)
    for j in range(N_KV // 2):
        x = _head_norm(k_ref[:, 2 * HD * j:2 * HD * (j + 1)], kg_ref[...], bd)
        if rope:
            x = _rope(x, cos_ref[...], sin_ref[...])
        ko_ref[2 * j] = x[:, :HD].astype(ko_ref.dtype)
        ko_ref[2 * j + 1] = x[:, HD:].astype(ko_ref.dtype)
        v = v_ref[:, 2 * HD * j:2 * HD * (j + 1)]
        vo_ref[2 * j] = v[:, :HD].astype(vo_ref.dtype)
        vo_ref[2 * j + 1] = v[:, HD:].astype(vo_ref.dtype)


def _qkv(proj, qg128, kg128, cos128, sin128, bd, l, latent):
    tm = SEG
    n = T_LAT // tm if latent else T_CTX // tm
    roff = T_CTX // tm if latent else 0
    per_seq = DEC_SEQ // tm
    if latent:
        kv_shape = (DEC_BATCH, N_KV, DEC_SEQ, HD)
        kv_spec = pl.BlockSpec((None, N_KV, tm, HD), lambda i: (i // per_seq, 0, i % per_seq, 0))
        kv_dtype = BF16
        tab_map = lambda i: (i % per_seq, 0)
    else:
        kv_shape = (BATCH, N_KV, SEQ, HD)
        kv_spec = pl.BlockSpec((None, N_KV, tm, HD), lambda i: (i, 0, 0, 0))
        kv_dtype = F32
        tab_map = lambda i: (0, 0)
    return pl.pallas_call(
        functools.partial(_qkv_kernel, rope=latent),
        grid=(n,),
        in_specs=[
            pl.BlockSpec((tm, D), lambda i: (roff + i, 2)),
            pl.BlockSpec((tm, N_KV * HD), lambda i: (roff + i, 3 * D // (N_KV * HD))),
            pl.BlockSpec((tm, N_KV * HD), lambda i: (roff + i, 3 * D // (N_KV * HD) + 1)),
            pl.BlockSpec((None, 1, 2 * HD), lambda i: (l, 0, 0)),
            pl.BlockSpec((None, 1, 2 * HD), lambda i: (l, 0, 0)),
            pl.BlockSpec((tm, 2 * HD), tab_map),
            pl.BlockSpec((tm, 2 * HD), tab_map),
            pl.BlockSpec((2 * HD, 2 * HD), lambda i: (0, 0)),
        ],
        out_specs=[
            pl.BlockSpec((N_HEADS, tm, HD), lambda i: (0, i, 0)),
            kv_spec,
            kv_spec,
        ],
        out_shape=[
            jax.ShapeDtypeStruct((N_HEADS, n * tm, HD), BF16),
            jax.ShapeDtypeStruct(kv_shape, kv_dtype),
            jax.ShapeDtypeStruct(kv_shape, kv_dtype),
        ],
        compiler_params=_cp(("arbitrary",)),
        name="qkv_lat" if latent else "qkv_ctx",
    )(proj, proj, proj, qg128, kg128, cos128, sin128, bd)


def _softmax_pv(q, k, v):
    s = _dot_nt(q, k)
    m = jnp.max(s, axis=-1, keepdims=True)
    p = jnp.exp(s - m)
    den = jnp.sum(p, axis=-1, keepdims=True)
    return _dot(p.astype(BF16), v) / den


def _heads_to_lanes(o, tq):
    g = N_HEADS // N_KV
    return jnp.concatenate([o[h * tq:(h + 1) * tq] for h in range(g)], axis=-1)


def _attn_ctx_kernel(q_ref, k_ref, v_ref, o_ref):
    g = N_HEADS // N_KV
    q = q_ref[...].reshape(g * SEQ, HD)
    o = _softmax_pv(q, k_ref[...].astype(BF16), v_ref[...].astype(BF16))
    o_ref[...] = _heads_to_lanes(o, SEQ).astype(BF16)


def _attn_ctx(qh, kc, vc):
    g = N_HEADS // N_KV
    return pl.pallas_call(
        _attn_ctx_kernel,
        grid=(BATCH, N_KV),
        in_specs=[
            pl.BlockSpec((g, SEQ, HD), lambda b, h: (h, b, 0)),
            pl.BlockSpec((None, None, SEQ, HD), lambda b, h: (b, h, 0, 0)),
            pl.BlockSpec((None, None, SEQ, HD), lambda b, h: (b, h, 0, 0)),
        ],
        out_specs=pl.BlockSpec((SEQ, g * HD), lambda b, h: (b, h)),
        out_shape=jax.ShapeDtypeStruct((T_CTX, D), BF16),
        compiler_params=_cp(("arbitrary", "arbitrary")),
        name="attn_ctx",
    )(qh, kc, vc)


ATT_TQ = 256


def _attn_lat_kernel(q_ref, pk_ref, pv_ref, k_ref, v_ref, o_ref):
    g = N_HEADS // N_KV
    q = q_ref[...].reshape(g * ATT_TQ, HD)
    k = jnp.concatenate([pk_ref[...].astype(BF16), k_ref[...]], axis=0)
    v = jnp.concatenate([pv_ref[...].astype(BF16), v_ref[...]], axis=0)
    o = _softmax_pv(q, k, v)
    o_ref[...] = _heads_to_lanes(o, ATT_TQ).astype(BF16)


def _attn_lat(qh, cache_k, cache_v, kr, vr, l):
    g = N_HEADS // N_KV
    nq = DEC_SEQ // ATT_TQ
    return pl.pallas_call(
        _attn_lat_kernel,
        grid=(DEC_BATCH, N_KV, nq),
        in_specs=[
            pl.BlockSpec((g, ATT_TQ, HD), lambda b, h, i: (h, b * nq + i, 0)),
            pl.BlockSpec((None, None, None, PAST, HD), lambda b, h, i: (b, l, h, 0, 0)),
            pl.BlockSpec((None, None, None, PAST, HD), lambda b, h, i: (b, l, h, 0, 0)),
            pl.BlockSpec((None, None, DEC_SEQ, HD), lambda b, h, i: (b, h, 0, 0)),
            pl.BlockSpec((None, None, DEC_SEQ, HD), lambda b, h, i: (b, h, 0, 0)),
        ],
        out_specs=pl.BlockSpec((ATT_TQ, g * HD), lambda b, h, i: (b * nq + i, h)),
        out_shape=jax.ShapeDtypeStruct((T_LAT, D), BF16),
        compiler_params=_cp(("arbitrary", "arbitrary", "arbitrary")),
        name="attn_lat",
    )(qh, cache_k, cache_v, kr, vr)


MERGE_TM = 512


def _route(lt, bias):
    rows = [lt[e:e + 1, :] for e in range(N_EXP)]
    m = rows[0]
    for e in range(1, N_EXP):
        m = jnp.maximum(m, rows[e])
    ex = [jnp.exp(r - m) for r in rows]
    z = ex[0]
    for e in range(1, N_EXP):
        z = z + ex[e]
    probs = [x / z for x in ex]
    sel = [probs[e] + bias[e:e + 1, :] for e in range(N_EXP)]

    def top2_sum(v):
        a, b = jnp.maximum(v[0], v[1]), jnp.minimum(v[0], v[1])
        c, d = jnp.maximum(v[2], v[3]), jnp.minimum(v[2], v[3])
        return jnp.maximum(a, c) + jnp.maximum(jnp.minimum(a, c), jnp.maximum(b, d))

    scores = [top2_sum(sel[4 * g:4 * g + 4]) for g in range(4)]
    best = jnp.zeros_like(scores[0], dtype=jnp.int32)
    best_s = scores[0]
    for g in range(1, 4):
        take = scores[g] > best_s
        best = jnp.where(take, g, best)
        best_s = jnp.where(take, scores[g], best_s)
    cs, cp = [], []
    for j in range(4):
        s_j, p_j = sel[j], probs[j]
        for g in range(1, 4):
            s_j = jnp.where(best == g, sel[4 * g + j], s_j)
            p_j = jnp.where(best == g, probs[4 * g + j], p_j)
        cs.append(s_j)
        cp.append(p_j)
    neg = jnp.full_like(cs[0], -jnp.inf)

    def argmax4(v):
        bi = jnp.zeros_like(best)
        bv = v[0]
        for j in range(1, 4):
            take = v[j] > bv
            bi = jnp.where(take, j, bi)
            bv = jnp.where(take, v[j], bv)
        return bi

    def pick(v, idx):
        out = v[0]
        for j in range(1, 4):
            out = jnp.where(idx == j, v[j], out)
        return out

    i1 = argmax4(cs)
    cs2 = [jnp.where(i1 == j, neg, cs[j]) for j in range(4)]
    i2 = argmax4(cs2)
    i2 = jnp.where((i2 == 0) & (i1 == 0), 1, i2)
    w1, w2 = pick(cp, i1), pick(cp, i2)
    den = w1 + w2
    return best * 4 + i1, best * 4 + i2, w1 / den, w2 / den


def _merge_kernel(yrec_ref, oatt_ref, gr0_ref, gr1_ref, ga0_ref, ga1_ref, x_ref, mod_ref, g2_ref,
                  wrec_ref, watt_ref, wout_ref, wrt_ref, rb_ref,
                  x1_ref, h2_ref, idx_ref, wts_ref, comb_ref,
                  wrec_s, watt_s, wout_s):
    @pl.when(pl.program_id(0) == 0)
    def _():
        wrec_s[...] = wrec_ref[...].astype(BF16)
        watt_s[...] = watt_ref[...].astype(BF16)
        wout_s[...] = wout_ref[...].astype(BF16)

    half = D // 2
    b_rec = _dot(yrec_ref[...], wrec_s[...])
    b_att = _dot(oatt_ref[...], watt_s[...])
    m0 = jax.nn.sigmoid(gr0_ref[...]) * b_rec[:, :half] + jax.nn.sigmoid(ga0_ref[...]) * b_att[:, :half]
    m1 = jax.nn.sigmoid(gr1_ref[...]) * b_rec[:, half:] + jax.nn.sigmoid(ga1_ref[...]) * b_att[:, half:]
    merged = jnp.concatenate([m0, m1], axis=-1).astype(BF16)
    out = _dot(merged, wout_s[...])

    hs = []
    for s in range(MERGE_TM // SEG):
        rows = slice(s * SEG, (s + 1) * SEG)
        m = mod_ref[s]
        x1 = x_ref[rows, :] + m[2:3, :] * out[rows, :]
        x1_ref[rows, :] = x1
        h2 = _norm_mod(x1, g2_ref[...], m[3:4, :], m[4:5, :])
        h2_ref[rows, :] = h2.astype(BF16)
        hs.append(h2)
    h2 = jnp.concatenate(hs, axis=0)

    h_hi, h_lo = _split(h2)
    w_hi, w_lo = _split(wrt_ref[...])
    lt = _dot_nt(w_hi, h_hi) + _dot_nt(w_hi, h_lo) + _dot_nt(w_lo, h_hi)
    e1, e2, w1, w2 = _route(lt, rb_ref[...])
    idx_ref[...] = jnp.concatenate([e1, e2], axis=0)
    wts_ref[...] = jnp.concatenate([w1, w2], axis=0)
    eid = lax.broadcasted_iota(jnp.int32, (N_EXP, MERGE_TM), 0)
    comb_ref[...] = jnp.where(eid == e1, w1, 0.0) + jnp.where(eid == e2, w2, 0.0)


def _merge(yrec, oatt, proj, x, modseg, norm2_g, w_rec_out, w_att_out, w_out, wrt, rbias, l):
    tm = MERGE_TM
    half = D // 2
    gcol = (3 * D + 2 * N_KV * HD) // half
    wspec = pl.BlockSpec((None, D, D), lambda i: (l, 0, 0))
    return pl.pallas_call(
        _merge_kernel,
        grid=(T // tm,),
        in_specs=[
            pl.BlockSpec((tm, D), lambda i: (i, 0)),
            pl.BlockSpec((tm, D), lambda i: (i, 0)),
            pl.BlockSpec((tm, half), lambda i: (i, gcol)),
            pl.BlockSpec((tm, half), lambda i: (i, gcol + 1)),
            pl.BlockSpec((tm, half), lambda i: (i, gcol + 2)),
            pl.BlockSpec((tm, half), lambda i: (i, gcol + 3)),
            pl.BlockSpec((tm, D), lambda i: (i, 0)),
            pl.BlockSpec((None, tm // SEG, 8, D), lambda i: (l, i, 0, 0)),
            pl.BlockSpec((None, 1, D), lambda i: (l, 0, 0)),
            wspec, wspec, wspec,
            pl.BlockSpec((N_EXP, D), lambda i: (0, 0)),
            pl.BlockSpec((N_EXP, 1), lambda i: (0, 0)),
        ],
        out_specs=[
            pl.BlockSpec((tm, D), lambda i: (i, 0)),
            pl.BlockSpec((tm, D), lambda i: (i, 0)),
            pl.BlockSpec((2, tm), lambda i: (0, i)),
            pl.BlockSpec((2, tm), lambda i: (0, i)),
            pl.BlockSpec((N_EXP, tm), lambda i: (0, i)),
        ],
        out_shape=[
            jax.ShapeDtypeStruct((T, D), F32),
            jax.ShapeDtypeStruct((T, D), BF16),
            jax.ShapeDtypeStruct((2, T), jnp.int32),
            jax.ShapeDtypeStruct((2, T), F32),
            jax.ShapeDtypeStruct((N_EXP, T), F32),
        ],
        scratch_shapes=[pltpu.VMEM((D, D), BF16)] * 3,
        compiler_params=_cp(("arbitrary",)),
        name="merge",
    )(yrec, oatt, proj, proj, proj, proj, x, modseg, norm2_g.reshape(DEPTH, 1, D),
      w_rec_out, w_att_out, w_out, wrt, rbias)


MOE_TM = 1024


def _moe_kernel(h_ref, comb_ref, wg_ref, wu_ref, wd_ref, x1_ref, mod_ref, o_ref, acc_ref):
    e = pl.program_id(1)

    @pl.when(e == 0)
    def _():
        acc_ref[...] = jnp.zeros_like(acc_ref)

    h = h_ref[...]
    g = _dot(h, wg_ref[...].astype(BF16))
    u = _dot(h, wu_ref[...].astype(BF16))
    act = (g * jax.nn.sigmoid(g)) * u * comb_ref[...]
    acc_ref[...] += _dot(act.astype(BF16), wd_ref[...].astype(BF16))

    @pl.when(e == N_EXP - 1)
    def _():
        for s in range(MOE_TM // SEG):
            rows = slice(s * SEG, (s + 1) * SEG)
            o_ref[rows, :] = x1_ref[rows, :] + mod_ref[s][5:6, :] * acc_ref[rows, :]


def _moe(h2, comb3, w_gate_e, w_up_e, w_down_e, x1, modseg, l):
    tm = MOE_TM
    return pl.pallas_call(
        _moe_kernel,
        grid=(T // tm, N_EXP),
        in_specs=[
            pl.BlockSpec((tm, D), lambda i, e: (i, 0)),
            pl.BlockSpec((None, tm, 1), lambda i, e: (e, i, 0)),
            pl.BlockSpec((None, None, D, D_EXP), lambda i, e: (l, e, 0, 0)),
            pl.BlockSpec((None, None, D, D_EXP), lambda i, e: (l, e, 0, 0)),
            pl.BlockSpec((None, None, D_EXP, D), lambda i, e: (l, e, 0, 0)),
            pl.BlockSpec((tm, D), lambda i, e: (i, 0)),
            pl.BlockSpec((None, tm // SEG, 8, D), lambda i, e: (l, i, 0, 0)),
        ],
        out_specs=pl.BlockSpec((tm, D), lambda i, e: (i, 0)),
        out_shape=jax.ShapeDtypeStruct((T, D), F32),
        scratch_shapes=[pltpu.VMEM((tm, D), F32)],
        compiler_params=_cp(("arbitrary", "arbitrary")),
        name="moe",
    )(h2, comb3, w_gate_e, w_up_e, w_down_e, x1, modseg)


def _final_kernel(x_ref, g_ref, o_ref):
    x = x_ref[...]
    ms = jnp.mean(x * x, axis=-1, keepdims=True)
    o_ref[...] = x * lax.rsqrt(ms + EPS) * g_ref[...]


def _final_norm(x, g):
    tm = 512
    return pl.pallas_call(
        _final_kernel,
        grid=(T // tm,),
        in_specs=[pl.BlockSpec((tm, D), lambda i: (i, 0)), pl.BlockSpec((1, D), lambda i: (0, 0))],
        out_specs=pl.BlockSpec((tm, D), lambda i: (i, 0)),
        out_shape=jax.ShapeDtypeStruct((T, D), F32),
        compiler_params=_cp(("arbitrary",)),
        name="final_norm",
    )(x, g.reshape(1, D))


def _rope_tables():
    n = DEC_SEQ
    pos_row = np.repeat(np.arange(n // GRID_W, dtype=np.float32), GRID_W)
    pos_col = np.tile(np.arange(GRID_W, dtype=np.float32), n // GRID_W)
    half = HD // 2
    inv_freq = jnp.asarray(ROPE_THETA, F32) ** (-jnp.arange(0, half, 2, dtype=F32) / half)
    ang = jnp.concatenate([jnp.asarray(pos_row)[:, None] * inv_freq,
                           jnp.asarray(pos_col)[:, None] * inv_freq], axis=-1)
    cos, sin = jnp.cos(ang), jnp.sin(ang)
    cos128 = jnp.tile(cos, (1, 4))
    sin128 = jnp.tile(jnp.concatenate([-sin, sin], axis=-1), (1, 2))
    return cos128, sin128


def _head_mean_matrix():
    idx = np.arange(2 * HD)
    same = (idx[:, None] // HD) == (idx[None, :] // HD)
    return jnp.asarray(same.astype(np.float32) / HD, BF16)


_SEG_ROWS = np.array([0] * (T_CTX // SEG) + [1 + b for b in range(DEC_BATCH) for _ in range(DEC_SEQ // SEG)])


def kernel(x_prompt, x_sample, cache_k, cache_v, state_rec, c, c_ctx, w_mod, b_mod, norm1_g, norm2_g, w_in, conv_w, conv_b, rg_wa, rg_ba, rg_wx, rg_bx, rg_lambda, q_norm_g, k_norm_g, w_rec_out, w_att_out, w_out, w_router, router_bias, w_gate_e, w_up_e, w_down_e, final_g):
    x = jnp.concatenate([x_prompt.reshape(T_CTX, D), x_sample.reshape(T_LAT, D)], axis=0)

    cvecs = jnp.concatenate([c_ctx[None, :], c, jnp.zeros((3, D), F32)], axis=0)
    mods = _mods(cvecs, w_mod, b_mod).reshape(DEPTH, 8, 6, D)
    modseg = jnp.pad(mods[:, _SEG_ROWS], ((0, 0), (0, 0), (0, 2), (0, 0)))

    cos128, sin128 = _rope_tables()
    bd = _head_mean_matrix()
    qg128 = jnp.tile(q_norm_g, (1, 2)).reshape(DEPTH, 1, 2 * HD)
    kg128 = jnp.tile(k_norm_g, (1, 2)).reshape(DEPTH, 1, 2 * HD)
    wg = jnp.concatenate([rg_wa[:, 0], rg_wx[:, 0], rg_wa[:, 1], rg_wx[:, 1]], axis=-1)
    pvec = jnp.stack([rg_ba[:, 0], rg_bx[:, 0], rg_ba[:, 1], rg_bx[:, 1],
                      rg_lambda[:, 0], rg_lambda[:, 1], conv_b, jnp.zeros_like(conv_b)], axis=1)
    wrt = w_router.T
    rbias = router_bias.reshape(N_EXP, 1)

    new_k, new_v, new_s = [], [], []
    for l in range(DEPTH):
        proj = _inproj(x, modseg, norm1_g, w_in, l)
        h0 = jnp.concatenate([jnp.zeros((T_CTX // UNIT, 2, D), F32), state_rec[:, l]], axis=0)
        yrec, stf, stb = _rec(proj, conv_w, pvec, wg, h0, l)
        qc, kc, vc = _qkv(proj, qg128, kg128, cos128, sin128, bd, l, latent=False)
        ql, kl, vl = _qkv(proj, qg128, kg128, cos128, sin128, bd, l, latent=True)
        o_ctx = _attn_ctx(qc, kc, vc)
        o_lat = _attn_lat(ql, cache_k, cache_v, kl, vl, l)
        oatt = jnp.concatenate([o_ctx, o_lat], axis=0)
        x1, h2, idx, wts, comb = _merge(yrec, oatt, proj, x, modseg, norm2_g,
                                        w_rec_out, w_att_out, w_out, wrt, rbias, l)
        x = _moe(h2, comb.reshape(N_EXP, T, 1), w_gate_e, w_up_e, w_down_e, x1, modseg, l)
        new_k.append(kc)
        new_v.append(vc)
        n_cu = T_CTX // UNIT
        spu = UNIT // SEQ
        hf_last = stf[:n_cu].reshape(n_cu, spu, 2, D)[:, :, 1].reshape(BATCH, D)
        hb_first = stb[:n_cu].reshape(n_cu, spu, 2, D)[:, :, 0].reshape(BATCH, D)
        new_s.append(jnp.stack([hf_last, hb_first], axis=1))

    y = _final_norm(x, final_g)
    y_prompt = y[:T_CTX].reshape(BATCH, SEQ, D)
    y_sample = y[T_CTX:].reshape(DEC_BATCH, DEC_SEQ, D)
    return (y_prompt, y_sample, jnp.stack(new_k, axis=1), jnp.stack(new_v, axis=1), jnp.stack(new_s, axis=1))
```

```python
import functools

import numpy as np
import jax
import jax.numpy as jnp
from jax import lax
from jax.experimental import pallas as pl
from jax.experimental.pallas import tpu as pltpu

F32 = jnp.float32
BF16 = jnp.bfloat16

D = 1024
BATCH = 16
SEQ = 256
DEPTH = 2
DEC_BATCH = 4
DEC_SEQ = 1024
PAST = 256
GRID_W = 64
N_HEADS = 16
N_KV = 4
HD = 64
N_RG_BLK = 8
RG_BLK = 128
RG_C = 8.0
N_EXP = 16
D_EXP = 512
ROPE_THETA = 10000.0
EPS = 1e-6
P_IN = 5632
TINY = float(np.finfo(np.float32).tiny)

T_CTX = BATCH * SEQ
T_LAT = DEC_BATCH * DEC_SEQ
T = T_CTX + T_LAT
SEG = 256
N_SEG = T // SEG
UNIT = 1024
N_UNIT = T // UNIT
CHUNK = UNIT // 8
CSTRIDE = CHUNK + 8

VMEM_LIMIT = 56 * 1024 * 1024


def _cp(sem):
    return pltpu.CompilerParams(dimension_semantics=sem, vmem_limit_bytes=VMEM_LIMIT)


def _split(x):
    hi = x.astype(BF16)
    lo = (x - hi.astype(F32)).astype(BF16)
    return hi, lo


def _sigmoid(x):
    return 0.5 * jnp.tanh(0.5 * x) + 0.5


def _dot(a, b):
    return jnp.dot(a, b, preferred_element_type=F32)


def _dot_nt(a, b):
    return lax.dot_general(a, b, (((1,), (1,)), ((), ())), preferred_element_type=F32)


def _mods_kernel(c_ref, w_ref, b_ref, o_ref):
    c = c_ref[...]
    s = c * jax.nn.sigmoid(c)
    s_hi, s_lo = _split(s)
    w_hi, w_lo = _split(w_ref[...])
    o_ref[...] = _dot(s_hi, w_hi) + _dot(s_hi, w_lo) + _dot(s_lo, w_hi) + b_ref[...]


def _mods(cvecs, w_mod, b_mod):
    tn = 1536
    return pl.pallas_call(
        _mods_kernel,
        grid=(DEPTH, 6 * D // tn),
        in_specs=[
            pl.BlockSpec((8, D), lambda l, j: (0, 0)),
            pl.BlockSpec((None, D, tn), lambda l, j: (l, 0, j)),
            pl.BlockSpec((None, 1, tn), lambda l, j: (l, 0, j)),
        ],
        out_specs=pl.BlockSpec((None, 8, tn), lambda l, j: (l, 0, j)),
        out_shape=jax.ShapeDtypeStruct((DEPTH, 8, 6 * D), F32),
        compiler_params=_cp(("arbitrary", "arbitrary")),
        name="mods",
    )(cvecs, w_mod, b_mod.reshape(DEPTH, 1, 6 * D))


def _norm_mod(x, g, shift, scale):
    ms = jnp.mean(x * x, axis=-1, keepdims=True)
    return x * lax.rsqrt(ms + EPS) * g * (1.0 + scale) + shift


def _inproj_kernel(x_ref, mod_ref, g_ref, w_ref, o_ref, h_ref, *, tm):
    @pl.when(pl.program_id(1) == 0)
    def _():
        def seg(s, carry):
            r0 = pl.multiple_of(s * SEG, SEG)
            m = mod_ref[s]
            h = _norm_mod(x_ref[pl.ds(r0, SEG), :], g_ref[...], m[0:1, :], m[1:2, :])
            h_ref[pl.ds(r0, SEG), :] = h.astype(BF16)
            return carry
        lax.fori_loop(0, tm // SEG, seg, 0)

    o_ref[...] = _dot(h_ref[...], w_ref[...].astype(BF16)).astype(BF16)


def _inproj(x, modseg, norm_g, w_in, l):
    tm, tn = 2048, 512
    return pl.pallas_call(
        functools.partial(_inproj_kernel, tm=tm),
        grid=(T // tm, P_IN // tn),
        in_specs=[
            pl.BlockSpec((tm, D), lambda i, j: (i, 0)),
            pl.BlockSpec((None, tm // SEG, 8, D), lambda i, j: (l, i, 0, 0)),
            pl.BlockSpec((None, 1, D), lambda i, j: (l, 0, 0)),
            pl.BlockSpec((None, D, tn), lambda i, j: (l, 0, j)),
        ],
        out_specs=pl.BlockSpec((tm, tn), lambda i, j: (i, j)),
        out_shape=jax.ShapeDtypeStruct((T, P_IN), BF16),
        scratch_shapes=[pltpu.VMEM((tm, D), BF16)],
        compiler_params=_cp(("arbitrary", "arbitrary")),
        name="inproj",
    )(x, modseg, norm_g.reshape(DEPTH, 1, D), w_in)


REC_CW = 512
HALO = 16


def _rec_kernel(xr_ref, gate_ref, cw_ref, pv_ref, wg_ref, h0_ref,
                y_ref, stf_ref, stb_ref,
                af_ref, bf_ref, ab_ref, bb_ref):
    u = pl.program_id(0)
    is_ctx = u < (T_CTX // UNIT)
    seq_len = jnp.where(is_ctx, SEQ, DEC_SEQ)
    nblk = REC_CW // RG_BLK

    pv = pv_ref[...]
    cwts = cw_ref[...]
    conv_b = pv[6:7, :]

    def softplus_neg(lam):
        z = -lam
        return jnp.maximum(z, 0.0) + jnp.log1p(jnp.exp(-jnp.abs(z)))

    c4s = tuple((-0.5 * RG_C) * softplus_neg(pv[4 + d:5 + d, :]) for d in range(2))
    pv_h = 0.5 * pv
    a_refs = (af_ref, ab_ref)
    b_refs = (bf_ref, bb_ref)

    def gates(ci, carry):
        base = pl.multiple_of(ci * CHUNK, CHUNK)
        lo = pl.multiple_of(jnp.maximum(base - HALO, 0), HALO)
        hi = pl.multiple_of(jnp.minimum(base + CHUNK, UNIT - HALO), HALO)
        main = xr_ref[pl.ds(base, CHUNK), :].astype(F32)
        win = jnp.concatenate([xr_ref[pl.ds(lo, HALO), :].astype(F32), main,
                               xr_ref[pl.ds(hi, HALO), :].astype(F32)], axis=0)
        t = base + lax.broadcasted_iota(jnp.int32, (CHUNK, 1), 0)
        tl = jnp.bitwise_and(t, seq_len - 1)
        n_win = CHUNK + 2 * HALO
        xm2 = jnp.where(tl >= 2, pltpu.roll(win, 2, 0)[HALO:HALO + CHUNK], 0.0)
        xm1 = jnp.where(tl >= 1, pltpu.roll(win, 1, 0)[HALO:HALO + CHUNK], 0.0)
        xp1 = jnp.where(tl <= seq_len - 2, pltpu.roll(win, n_win - 1, 0)[HALO:HALO + CHUNK], 0.0)
        xc = conv_b + xm2 * cwts[0:1, :]
        xc = xc + xm1 * cwts[1:2, :]
        xc = xc + main * cwts[2:3, :]
        xc = xc + xp1 * cwts[3:4, :]
        for n in range(nblk):
            ls = slice(n * RG_BLK, (n + 1) * RG_BLK)
            xn = xc[:, ls]
            pre_h = 0.5 * _dot(xn.astype(BF16), wg_ref[n].astype(BF16))
            for d in range(2):
                th_r = jnp.tanh(pre_h[:, (2 * d) * RG_BLK:(2 * d + 1) * RG_BLK] + pv_h[2 * d:2 * d + 1, ls])
                th_i = jnp.tanh(pre_h[:, (2 * d + 1) * RG_BLK:(2 * d + 2) * RG_BLK] + pv_h[2 * d + 1:2 * d + 2, ls])
                c4 = c4s[d][:, ls]
                log_a = c4 * th_r + c4
                i = 0.5 * th_i + 0.5
                a = jnp.exp(log_a)
                s = -jnp.tanh(log_a) * (a * a + 1.0)
                inp = (s * lax.rsqrt(jnp.maximum(s, TINY))) * (i * xn)
                sbase = pl.multiple_of(ci * CSTRIDE, 8)
                a_refs[d][n, pl.ds(sbase, CHUNK), :] = a
                b_refs[d][n, pl.ds(sbase, CHUNK), :] = inp
        return carry

    lax.fori_loop(0, 8, gates, 0)

    zeros = [jnp.zeros((8, RG_BLK), F32)] * nblk
    ones = [jnp.ones((8, RG_BLK), F32)] * nblk
    hf, pf, hb, pb = list(zeros), list(ones), list(zeros), list(ones)
    for r in range(CHUNK):
        rows_f = pl.ds(r, 8, stride=CSTRIDE)
        rows_b = pl.ds(CHUNK - 1 - r, 8, stride=CSTRIDE)
        for n in range(nblk):
            a = af_ref[n, rows_f, :]
            hf[n] = a * hf[n] + bf_ref[n, rows_f, :]
            pf[n] = a * pf[n]
            bf_ref[n, rows_f, :] = hf[n]
            af_ref[n, rows_f, :] = pf[n]
            a = ab_ref[n, rows_b, :]
            hb[n] = a * hb[n] + bb_ref[n, rows_b, :]
            pb[n] = a * pb[n]
            bb_ref[n, rows_b, :] = hb[n]
            ab_ref[n, rows_b, :] = pb[n]
    hf, pf, hb, pb = (jnp.concatenate(v, axis=-1) for v in (hf, pf, hb, pb))

    cps = jnp.where(is_ctx, SEQ // CHUNK, DEC_SEQ // CHUNK)
    h0f = h0_ref[0:1, :]
    h0b = h0_ref[1:2, :]
    cf = [h0f]
    for c in range(1, 8):
        chain = hf[c - 1:c, :] + pf[c - 1:c, :] * cf[c - 1]
        cf.append(jnp.where(jnp.bitwise_and(c, cps - 1) == 0, h0f, chain))
    cb = [None] * 8
    cb[7] = h0b
    for c in range(6, -1, -1):
        chain = hb[c + 1:c + 2, :] + pb[c + 1:c + 2, :] * cb[c + 1]
        cb[c] = jnp.where(jnp.bitwise_and(c, cps - 1) == cps - 1, h0b, chain)
    carry_f = jnp.concatenate(cf, axis=0)
    carry_b = jnp.concatenate(cb, axis=0)
    stf_ref[...] = hf + pf * carry_f
    stb_ref[...] = hb + pb * carry_b

    for ci in range(8):
        rows = pl.ds(ci * CHUNK, CHUNK)
        srows = pl.ds(ci * CSTRIDE, CHUNK)
        for n in range(nblk):
            ls = slice(n * RG_BLK, (n + 1) * RG_BLK)
            h_f = bf_ref[n, srows, :] + af_ref[n, srows, :] * carry_f[ci:ci + 1, ls]
            h_b = bb_ref[n, srows, :] + ab_ref[n, srows, :] * carry_b[ci:ci + 1, ls]
            g = gate_ref[rows, ls].astype(F32)
            y_ref[rows, ls] = ((h_f + h_b) * jax.nn.gelu(g, approximate=True)).astype(BF16)


def _rec(proj, conv_w, pvec, wg, h0, l):
    ncb = D // REC_CW
    return pl.pallas_call(
        _rec_kernel,
        grid=(N_UNIT, ncb),
        in_specs=[
            pl.BlockSpec((UNIT, REC_CW), lambda u, c: (u, c)),
            pl.BlockSpec((UNIT, REC_CW), lambda u, c: (u, ncb + c)),
            pl.BlockSpec((None, 4, REC_CW), lambda u, c: (l, 0, c)),
            pl.BlockSpec((None, 8, REC_CW), lambda u, c: (l, 0, c)),
            pl.BlockSpec((None, REC_CW // RG_BLK, RG_BLK, 4 * RG_BLK), lambda u, c: (l, c, 0, 0)),
            pl.BlockSpec((None, 2, REC_CW), lambda u, c: (u, 0, c)),
        ],
        out_specs=[
            pl.BlockSpec((UNIT, REC_CW), lambda u, c: (u, c)),
            pl.BlockSpec((None, 8, REC_CW), lambda u, c: (u, 0, c)),
            pl.BlockSpec((None, 8, REC_CW), lambda u, c: (u, 0, c)),
        ],
        out_shape=[
            jax.ShapeDtypeStruct((T, D), BF16),
            jax.ShapeDtypeStruct((N_UNIT, 8, D), F32),
            jax.ShapeDtypeStruct((N_UNIT, 8, D), F32),
        ],
        scratch_shapes=[pltpu.VMEM((REC_CW // RG_BLK, 8 * CSTRIDE, RG_BLK), F32)] * 4,
        compiler_params=_cp(("arbitrary", "arbitrary")),
        name="rec",
    )(proj, proj, conv_w, pvec, wg, h0)


def _head_norm(x, g128, bd):
    hi, lo = _split(x * x)
    ms = _dot(hi, bd) + _dot(lo, bd)
    return x * lax.rsqrt(ms + EPS) * g128


def _rope(x, cos, sin_signed):
    lane = lax.broadcasted_iota(jnp.int32, x.shape, 1)
    first_half = jnp.bitwise_and(lane, HD - 1) < HD // 2
    partner = jnp.where(first_half, pltpu.roll(x, 2 * HD - HD // 2, 1), pltpu.roll(x, HD // 2, 1))
    return x * cos + partner * sin_signed


def _qkv_kernel(q_ref, k_ref, v_ref, qg_ref, kg_ref, cos_ref, sin_ref, bd_ref,
                qo_ref, ko_ref, vo_ref, *, rope):
    bd = bd_ref[...]
    scale = HD ** -0.5 * float(np.log2(np.e))
    for j in range(N_HEADS // 2):
        x = _head_norm(q_ref[:, 2 * HD * j:2 * HD * (j + 1)].astype(F32), qg_ref[...], bd)
        if rope:
            x = _rope(x, cos_ref[...], sin_ref[...])
        x = x * scale
        qo_ref[2 * j] = x[:, :HD].astype(qo_ref.dtype)
        qo_ref[2 * j + 1] = x[:, HD:].astype(qo_ref.dtype)
    for j in range(N_KV // 2):
        x = _head_norm(k_ref[:, 2 * HD * j:2 * HD * (j + 1)].astype(F32), kg_ref[...], bd)
        if rope:
            x = _rope(x, cos_ref[...], sin_ref[...])
        ko_ref[2 * j] = x[:, :HD].astype(ko_ref.dtype)
        ko_ref[2 * j + 1] = x[:, HD:].astype(ko_ref.dtype)
        v = v_ref[:, 2 * HD * j:2 * HD * (j + 1)].astype(F32)
        vo_ref[2 * j] = v[:, :HD].astype(vo_ref.dtype)
        vo_ref[2 * j + 1] = v[:, HD:].astype(vo_ref.dtype)


def _qkv(proj, qg128, kg128, cos128, sin128, bd, l, latent):
    tm = SEG
    n = T_LAT // tm if latent else T_CTX // tm
    roff = T_CTX // tm if latent else 0
    per_seq = DEC_SEQ // tm
    if latent:
        kv_shape = (DEC_BATCH, N_KV, DEC_SEQ, HD)
        kv_spec = pl.BlockSpec((None, N_KV, tm, HD), lambda i: (i // per_seq, 0, i % per_seq, 0))
        kv_dtype = BF16
        tab_map = lambda i: (i % per_seq, 0)
    else:
        kv_shape = (BATCH, N_KV, SEQ, HD)
        kv_spec = pl.BlockSpec((None, N_KV, tm, HD), lambda i: (i, 0, 0, 0))
        kv_dtype = F32
        tab_map = lambda i: (0, 0)
    return pl.pallas_call(
        functools.partial(_qkv_kernel, rope=latent),
        grid=(n,),
        in_specs=[
            pl.BlockSpec((tm, D), lambda i: (roff + i, 2)),
            pl.BlockSpec((tm, N_KV * HD), lambda i: (roff + i, 3 * D // (N_KV * HD))),
            pl.BlockSpec((tm, N_KV * HD), lambda i: (roff + i, 3 * D // (N_KV * HD) + 1)),
            pl.BlockSpec((None, 1, 2 * HD), lambda i: (l, 0, 0)),
            pl.BlockSpec((None, 1, 2 * HD), lambda i: (l, 0, 0)),
            pl.BlockSpec((tm, 2 * HD), tab_map),
            pl.BlockSpec((tm, 2 * HD), tab_map),
            pl.BlockSpec((2 * HD, 2 * HD), lambda i: (0, 0)),
        ],
        out_specs=[
            pl.BlockSpec((N_HEADS, tm, HD), lambda i: (0, i, 0)),
            kv_spec,
            kv_spec,
        ],
        out_shape=[
            jax.ShapeDtypeStruct((N_HEADS, n * tm, HD), BF16),
            jax.ShapeDtypeStruct(kv_shape, kv_dtype),
            jax.ShapeDtypeStruct(kv_shape, kv_dtype),
        ],
        compiler_params=_cp(("arbitrary",)),
        name="qkv_lat" if latent else "qkv_ctx",
    )(proj, proj, proj, qg128, kg128, cos128, sin128, bd)


def _with_ones(v):
    return jnp.concatenate([v, jnp.ones_like(v)], axis=-1)


def _softmax_pv(q, k, v_ext):
    s = _dot_nt(q, k)
    m = jnp.max(s, axis=-1, keepdims=True)
    p = jnp.exp2(s - m).astype(BF16)
    r = _dot(p, v_ext)
    return r[:, :HD] / r[:, HD:HD + 1]


def _attend_heads(q_ref, k, v_ext, h0=0):
    return jnp.concatenate([_softmax_pv(q_ref[h0 + h], k, v_ext) for h in range(N_HEADS // N_KV)], axis=-1)


def _attn_ctx_kernel(q_ref, k_ref, v_ref, o_ref):
    g = N_HEADS // N_KV
    for kv in range(N_KV):
        o = _attend_heads(q_ref, k_ref[kv].astype(BF16), _with_ones(v_ref[kv].astype(BF16)), kv * g)
        o_ref[:, kv * g * HD:(kv + 1) * g * HD] = o.astype(BF16)


def _attn_ctx(qh, kc, vc):
    return pl.pallas_call(
        _attn_ctx_kernel,
        grid=(BATCH,),
        in_specs=[
            pl.BlockSpec((N_HEADS, SEQ, HD), lambda b: (0, b, 0)),
            pl.BlockSpec((None, N_KV, SEQ, HD), lambda b: (b, 0, 0, 0)),
            pl.BlockSpec((None, N_KV, SEQ, HD), lambda b: (b, 0, 0, 0)),
        ],
        out_specs=pl.BlockSpec((SEQ, D), lambda b: (b, 0)),
        out_shape=jax.ShapeDtypeStruct((T_CTX, D), BF16),
        compiler_params=_cp(("arbitrary",)),
        name="attn_ctx",
    )(qh, kc, vc)


ATT_TQ = 256


def _attn_lat_kernel(q_ref, pk_ref, pv_ref, k_ref, v_ref, o_ref):
    k = jnp.concatenate([pk_ref[...].astype(BF16), k_ref[...]], axis=0)
    v = jnp.concatenate([pv_ref[...].astype(BF16), v_ref[...]], axis=0)
    o_ref[...] = _attend_heads(q_ref, k, _with_ones(v)).astype(BF16)


def _attn_lat(qh, cache_k, cache_v, kr, vr, l):
    g = N_HEADS // N_KV
    nq = DEC_SEQ // ATT_TQ
    return pl.pallas_call(
        _attn_lat_kernel,
        grid=(DEC_BATCH, N_KV, nq),
        in_specs=[
            pl.BlockSpec((g, ATT_TQ, HD), lambda b, h, i: (h, b * nq + i, 0)),
            pl.BlockSpec((None, None, None, PAST, HD), lambda b, h, i: (b, l, h, 0, 0)),
            pl.BlockSpec((None, None, None, PAST, HD), lambda b, h, i: (b, l, h, 0, 0)),
            pl.BlockSpec((None, None, DEC_SEQ, HD), lambda b, h, i: (b, h, 0, 0)),
            pl.BlockSpec((None, None, DEC_SEQ, HD), lambda b, h, i: (b, h, 0, 0)),
        ],
        out_specs=pl.BlockSpec((ATT_TQ, g * HD), lambda b, h, i: (b * nq + i, h)),
        out_shape=jax.ShapeDtypeStruct((T_LAT, D), BF16),
        compiler_params=_cp(("arbitrary", "arbitrary", "arbitrary")),
        name="attn_lat",
    )(qh, cache_k, cache_v, kr, vr)


MERGE_TM = 512


def _route(lt, bias):
    rows = [lt[e:e + 1, :] for e in range(N_EXP)]
    m = rows[0]
    for e in range(1, N_EXP):
        m = jnp.maximum(m, rows[e])
    ex = [jnp.exp(r - m) for r in rows]
    z = ex[0]
    for e in range(1, N_EXP):
        z = z + ex[e]
    probs = [x / z for x in ex]
    sel = [probs[e] + bias[e:e + 1, :] for e in range(N_EXP)]

    def top2_sum(v):
        a, b = jnp.maximum(v[0], v[1]), jnp.minimum(v[0], v[1])
        c, d = jnp.maximum(v[2], v[3]), jnp.minimum(v[2], v[3])
        return jnp.maximum(a, c) + jnp.maximum(jnp.minimum(a, c), jnp.maximum(b, d))

    scores = [top2_sum(sel[4 * g:4 * g + 4]) for g in range(4)]
    best = jnp.zeros_like(scores[0], dtype=jnp.int32)
    best_s = scores[0]
    for g in range(1, 4):
        take = scores[g] > best_s
        best = jnp.where(take, g, best)
        best_s = jnp.where(take, scores[g], best_s)
    cs, cp = [], []
    for j in range(4):
        s_j, p_j = sel[j], probs[j]
        for g in range(1, 4):
            s_j = jnp.where(best == g, sel[4 * g + j], s_j)
            p_j = jnp.where(best == g, probs[4 * g + j], p_j)
        cs.append(s_j)
        cp.append(p_j)
    neg = jnp.full_like(cs[0], -jnp.inf)

    def argmax4(v):
        bi = jnp.zeros_like(best)
        bv = v[0]
        for j in range(1, 4):
            take = v[j] > bv
            bi = jnp.where(take, j, bi)
            bv = jnp.where(take, v[j], bv)
        return bi

    def pick(v, idx):
        out = v[0]
        for j in range(1, 4):
            out = jnp.where(idx == j, v[j], out)
        return out

    i1 = argmax4(cs)
    cs2 = [jnp.where(i1 == j, neg, cs[j]) for j in range(4)]
    i2 = argmax4(cs2)
    i2 = jnp.where((i2 == 0) & (i1 == 0), 1, i2)
    w1, w2 = pick(cp, i1), pick(cp, i2)
    den = w1 + w2
    return best * 4 + i1, best * 4 + i2, w1 / den, w2 / den


def _merge_kernel(yrec_ref, oatt_ref, gr0_ref, gr1_ref, ga0_ref, ga1_ref, x_ref, mod_ref, g2_ref,
                  wrec_ref, watt_ref, wout_ref, wrt_ref, rb_ref,
                  x1_ref, h2_ref, idx_ref, wts_ref, comb_ref,
                  wrec_s, watt_s, wout_s):
    @pl.when(pl.program_id(0) == 0)
    def _():
        wrec_s[...] = wrec_ref[...].astype(BF16)
        watt_s[...] = watt_ref[...].astype(BF16)
        wout_s[...] = wout_ref[...].astype(BF16)

    half = D // 2
    b_rec = _dot(yrec_ref[...], wrec_s[...])
    b_att = _dot(oatt_ref[...], watt_s[...])
    m0 = _sigmoid(gr0_ref[...].astype(F32)) * b_rec[:, :half] + _sigmoid(ga0_ref[...].astype(F32)) * b_att[:, :half]
    m1 = _sigmoid(gr1_ref[...].astype(F32)) * b_rec[:, half:] + _sigmoid(ga1_ref[...].astype(F32)) * b_att[:, half:]
    merged = jnp.concatenate([m0, m1], axis=-1).astype(BF16)
    out = _dot(merged, wout_s[...])

    hs = []
    for s in range(MERGE_TM // SEG):
        rows = slice(s * SEG, (s + 1) * SEG)
        m = mod_ref[s]
        x1 = x_ref[rows, :] + m[2:3, :] * out[rows, :]
        x1_ref[rows, :] = x1
        h2 = _norm_mod(x1, g2_ref[...], m[3:4, :], m[4:5, :])
        h2_ref[rows, :] = h2.astype(BF16)
        hs.append(h2)
    h2 = jnp.concatenate(hs, axis=0)

    h_hi, h_lo = _split(h2)
    w_hi, w_lo = _split(wrt_ref[...])
    lt = _dot_nt(w_hi, h_hi) + _dot_nt(w_hi, h_lo) + _dot_nt(w_lo, h_hi)
    e1, e2, w1, w2 = _route(lt, rb_ref[...])
    idx_ref[...] = jnp.concatenate([e1, e2], axis=0)
    wts_ref[...] = jnp.concatenate([w1, w2], axis=0)
    eid = lax.broadcasted_iota(jnp.int32, (N_EXP, MERGE_TM), 0)
    comb_ref[...] = jnp.where(eid == e1, w1, 0.0) + jnp.where(eid == e2, w2, 0.0)


def _merge(yrec, oatt, proj, x, modseg, norm2_g, w_rec_out, w_att_out, w_out, wrt, rbias, l):
    tm = MERGE_TM
    half = D // 2
    gcol = (3 * D + 2 * N_KV * HD) // half
    wspec = pl.BlockSpec((None, D, D), lambda i: (l, 0, 0))
    return pl.pallas_call(
        _merge_kernel,
        grid=(T // tm,),
        in_specs=[
            pl.BlockSpec((tm, D), lambda i: (i, 0)),
            pl.BlockSpec((tm, D), lambda i: (i, 0)),
            pl.BlockSpec((tm, half), lambda i: (i, gcol)),
            pl.BlockSpec((tm, half), lambda i: (i, gcol + 1)),
            pl.BlockSpec((tm, half), lambda i: (i, gcol + 2)),
            pl.BlockSpec((tm, half), lambda i: (i, gcol + 3)),
            pl.BlockSpec((tm, D), lambda i: (i, 0)),
            pl.BlockSpec((None, tm // SEG, 8, D), lambda i: (l, i, 0, 0)),
            pl.BlockSpec((None, 1, D), lambda i: (l, 0, 0)),
            wspec, wspec, wspec,
            pl.BlockSpec((N_EXP, D), lambda i: (0, 0)),
            pl.BlockSpec((N_EXP, 1), lambda i: (0, 0)),
        ],
        out_specs=[
            pl.BlockSpec((tm, D), lambda i: (i, 0)),
            pl.BlockSpec((tm, D), lambda i: (i, 0)),
            pl.BlockSpec((2, tm), lambda i: (0, i)),
            pl.BlockSpec((2, tm), lambda i: (0, i)),
            pl.BlockSpec((N_EXP, tm), lambda i: (0, i)),
        ],
        out_shape=[
            jax.ShapeDtypeStruct((T, D), F32),
            jax.ShapeDtypeStruct((T, D), BF16),
            jax.ShapeDtypeStruct((2, T), jnp.int32),
            jax.ShapeDtypeStruct((2, T), F32),
            jax.ShapeDtypeStruct((N_EXP, T), F32),
        ],
        scratch_shapes=[pltpu.VMEM((D, D), BF16)] * 3,
        compiler_params=_cp(("arbitrary",)),
        name="merge",
    )(yrec, oatt, proj, proj, proj, proj, x, modseg, norm2_g.reshape(DEPTH, 1, D),
      w_rec_out, w_att_out, w_out, wrt, rbias)


MOE_TM = 1024


def _moe_kernel(h_ref, comb_ref, wg_ref, wu_ref, wd_ref, x1_ref, mod_ref, o_ref, acc_ref):
    e = pl.program_id(1)

    @pl.when(e == 0)
    def _():
        acc_ref[...] = jnp.zeros_like(acc_ref)

    h = h_ref[...]
    g = _dot(h, wg_ref[...].astype(BF16))
    u = _dot(h, wu_ref[...].astype(BF16))
    act = (g * _sigmoid(g)) * u * comb_ref[...]
    acc_ref[...] += _dot(act.astype(BF16), wd_ref[...].astype(BF16))

    @pl.when(e == N_EXP - 1)
    def _():
        for s in range(MOE_TM // SEG):
            rows = slice(s * SEG, (s + 1) * SEG)
            o_ref[rows, :] = x1_ref[rows, :] + mod_ref[s][5:6, :] * acc_ref[rows, :]


def _moe(h2, comb3, w_gate_e, w_up_e, w_down_e, x1, modseg, l):
    tm = MOE_TM
    return pl.pallas_call(
        _moe_kernel,
        grid=(T // tm, N_EXP),
        in_specs=[
            pl.BlockSpec((tm, D), lambda i, e: (i, 0)),
            pl.BlockSpec((None, tm, 1), lambda i, e: (e, i, 0)),
            pl.BlockSpec((None, None, D, D_EXP), lambda i, e: (l, e, 0, 0)),
            pl.BlockSpec((None, None, D, D_EXP), lambda i, e: (l, e, 0, 0)),
            pl.BlockSpec((None, None, D_EXP, D), lambda i, e: (l, e, 0, 0)),
            pl.BlockSpec((tm, D), lambda i, e: (i, 0)),
            pl.BlockSpec((None, tm // SEG, 8, D), lambda i, e: (l, i, 0, 0)),
        ],
        out_specs=pl.BlockSpec((tm, D), lambda i, e: (i, 0)),
        out_shape=jax.ShapeDtypeStruct((T, D), F32),
        scratch_shapes=[pltpu.VMEM((tm, D), F32)],
        compiler_params=_cp(("arbitrary", "arbitrary")),
        name="moe",
    )(h2, comb3, w_gate_e, w_up_e, w_down_e, x1, modseg)


def _final_kernel(x_ref, g_ref, o_ref):
    x = x_ref[...]
    ms = jnp.mean(x * x, axis=-1, keepdims=True)
    o_ref[...] = x * lax.rsqrt(ms + EPS) * g_ref[...]


def _final_norm(x, g):
    tm = 512
    return pl.pallas_call(
        _final_kernel,
        grid=(T // tm,),
        in_specs=[pl.BlockSpec((tm, D), lambda i: (i, 0)), pl.BlockSpec((1, D), lambda i: (0, 0))],
        out_specs=pl.BlockSpec((tm, D), lambda i: (i, 0)),
        out_shape=jax.ShapeDtypeStruct((T, D), F32),
        compiler_params=_cp(("arbitrary",)),
        name="final_norm",
    )(x, g.reshape(1, D))


def _rope_tables():
    n = DEC_SEQ
    pos_row = np.repeat(np.arange(n // GRID_W, dtype=np.float32), GRID_W)
    pos_col = np.tile(np.arange(GRID_W, dtype=np.float32), n // GRID_W)
    half = HD // 2
    inv_freq = jnp.asarray(ROPE_THETA, F32) ** (-jnp.arange(0, half, 2, dtype=F32) / half)
    ang = jnp.concatenate([jnp.asarray(pos_row)[:, None] * inv_freq,
                           jnp.asarray(pos_col)[:, None] * inv_freq], axis=-1)
    cos, sin = jnp.cos(ang), jnp.sin(ang)
    cos128 = jnp.tile(cos, (1, 4))
    sin128 = jnp.tile(jnp.concatenate([-sin, sin], axis=-1), (1, 2))
    return cos128, sin128


def _head_mean_matrix():
    idx = np.arange(2 * HD)
    same = (idx[:, None] // HD) == (idx[None, :] // HD)
    return jnp.asarray(same.astype(np.float32) / HD, BF16)


_SEG_ROWS = np.array([0] * (T_CTX // SEG) + [1 + b for b in range(DEC_BATCH) for _ in range(DEC_SEQ // SEG)])


def kernel(x_prompt, x_sample, cache_k, cache_v, state_rec, c, c_ctx, w_mod, b_mod, norm1_g, norm2_g, w_in, conv_w, conv_b, rg_wa, rg_ba, rg_wx, rg_bx, rg_lambda, q_norm_g, k_norm_g, w_rec_out, w_att_out, w_out, w_router, router_bias, w_gate_e, w_up_e, w_down_e, final_g):
    x = jnp.concatenate([x_prompt.reshape(T_CTX, D), x_sample.reshape(T_LAT, D)], axis=0)

    cvecs = jnp.concatenate([c_ctx[None, :], c, jnp.zeros((3, D), F32)], axis=0)
    mods = _mods(cvecs, w_mod, b_mod).reshape(DEPTH, 8, 6, D)
    modseg = jnp.pad(mods[:, _SEG_ROWS], ((0, 0), (0, 0), (0, 2), (0, 0)))

    cos128, sin128 = _rope_tables()
    bd = _head_mean_matrix()
    qg128 = jnp.tile(q_norm_g, (1, 2)).reshape(DEPTH, 1, 2 * HD)
    kg128 = jnp.tile(k_norm_g, (1, 2)).reshape(DEPTH, 1, 2 * HD)
    wg = jnp.concatenate([rg_wa[:, 0], rg_wx[:, 0], rg_wa[:, 1], rg_wx[:, 1]], axis=-1)
    pvec = jnp.stack([rg_ba[:, 0], rg_bx[:, 0], rg_ba[:, 1], rg_bx[:, 1],
                      rg_lambda[:, 0], rg_lambda[:, 1], conv_b, jnp.zeros_like(conv_b)], axis=1)
    wrt = w_router.T
    rbias = router_bias.reshape(N_EXP, 1)

    new_k, new_v, new_s = [], [], []
    for l in range(DEPTH):
        proj = _inproj(x, modseg, norm1_g, w_in, l)
        h0 = jnp.concatenate([jnp.zeros((T_CTX // UNIT, 2, D), F32), state_rec[:, l]], axis=0)
        yrec, stf, stb = _rec(proj, conv_w, pvec, wg, h0, l)
        qc, kc, vc = _qkv(proj, qg128, kg128, cos128, sin128, bd, l, latent=False)
        ql, kl, vl = _qkv(proj, qg128, kg128, cos128, sin128, bd, l, latent=True)
        o_ctx = _attn_ctx(qc, kc, vc)
        o_lat = _attn_lat(ql, cache_k, cache_v, kl, vl, l)
        oatt = jnp.concatenate([o_ctx, o_lat], axis=0)
        x1, h2, idx, wts, comb = _merge(yrec, oatt, proj, x, modseg, norm2_g,
                                        w_rec_out, w_att_out, w_out, wrt, rbias, l)
        x = _moe(h2, comb.reshape(N_EXP, T, 1), w_gate_e, w_up_e, w_down_e, x1, modseg, l)
        new_k.append(kc)
        new_v.append(vc)
        n_cu = T_CTX // UNIT
        spu = UNIT // SEQ
        hf_last = stf[:n_cu].reshape(n_cu, spu, 2, D)[:, :, 1].reshape(BATCH, D)
        hb_first = stb[:n_cu].reshape(n_cu, spu, 2, D)[:, :, 0].reshape(BATCH, D)
        new_s.append(jnp.stack([hf_last, hb_first], axis=1))

    y = _final_norm(x, final_g)
    y_prompt = y[:T_CTX].reshape(BATCH, SEQ, D)
    y_sample = y[T_CTX:].reshape(DEC_BATCH, DEC_SEQ, D)
    return (y_prompt, y_sample, jnp.stack(new_k, axis=1), jnp.stack(new_v, axis=1), jnp.stack(new_s, axis=1))
```

```python
import functools

import numpy as np
import jax
import jax.numpy as jnp
from jax import lax
from jax.experimental import pallas as pl
from jax.experimental.pallas import tpu as pltpu

F32 = jnp.float32
BF16 = jnp.bfloat16

D = 1024
BATCH = 16
SEQ = 256
DEPTH = 2
DEC_BATCH = 4
DEC_SEQ = 1024
PAST = 256
GRID_W = 64
N_HEADS = 16
N_KV = 4
HD = 64
N_RG_BLK = 8
RG_BLK = 128
RG_C = 8.0
N_EXP = 16
D_EXP = 512
ROPE_THETA = 10000.0
EPS = 1e-6
P_IN = 5632
TINY = float(np.finfo(np.float32).tiny)

T_CTX = BATCH * SEQ
T_LAT = DEC_BATCH * DEC_SEQ
T = T_CTX + T_LAT
SEG = 256
N_SEG = T // SEG
UNIT = 1024
N_UNIT = T // UNIT
CHUNK = UNIT // 8
CSTRIDE = CHUNK + 8

VMEM_LIMIT = 56 * 1024 * 1024


def _cp(sem):
    return pltpu.CompilerParams(dimension_semantics=sem, vmem_limit_bytes=VMEM_LIMIT)


def _split(x):
    hi = x.astype(BF16)
    lo = (x - hi.astype(F32)).astype(BF16)
    return hi, lo


def _sigmoid(x):
    return 0.5 * jnp.tanh(0.5 * x) + 0.5


def _dot(a, b):
    return jnp.dot(a, b, preferred_element_type=F32)


def _dot_nt(a, b):
    return lax.dot_general(a, b, (((1,), (1,)), ((), ())), preferred_element_type=F32)


def _mods_kernel(c_ref, w_ref, b_ref, o_ref):
    c = c_ref[...]
    s = c * jax.nn.sigmoid(c)
    s_hi, s_lo = _split(s)
    w_hi, w_lo = _split(w_ref[...])
    o_ref[...] = _dot(s_hi, w_hi) + _dot(s_hi, w_lo) + _dot(s_lo, w_hi) + b_ref[...]


def _mods(cvecs, w_mod, b_mod):
    tn = 1536
    return pl.pallas_call(
        _mods_kernel,
        grid=(DEPTH, 6 * D // tn),
        in_specs=[
            pl.BlockSpec((8, D), lambda l, j: (0, 0)),
            pl.BlockSpec((None, D, tn), lambda l, j: (l, 0, j)),
            pl.BlockSpec((None, 1, tn), lambda l, j: (l, 0, j)),
        ],
        out_specs=pl.BlockSpec((None, 8, tn), lambda l, j: (l, 0, j)),
        out_shape=jax.ShapeDtypeStruct((DEPTH, 8, 6 * D), F32),
        compiler_params=_cp(("arbitrary", "arbitrary")),
        name="mods",
    )(cvecs, w_mod, b_mod.reshape(DEPTH, 1, 6 * D))


def _norm_mod(x, g, shift, scale):
    ms = jnp.mean(x * x, axis=-1, keepdims=True)
    return x * lax.rsqrt(ms + EPS) * g * (1.0 + scale) + shift


def _inproj_kernel(x_ref, mod_ref, g_ref, w_ref, o_ref, h_ref, *, tm):
    @pl.when(pl.program_id(1) == 0)
    def _():
        def seg(s, carry):
            r0 = pl.multiple_of(s * SEG, SEG)
            m = mod_ref[s]
            h = _norm_mod(x_ref[pl.ds(r0, SEG), :], g_ref[...], m[0:1, :], m[1:2, :])
            h_ref[pl.ds(r0, SEG), :] = h.astype(BF16)
            return carry
        lax.fori_loop(0, tm // SEG, seg, 0)

    o_ref[...] = _dot(h_ref[...], w_ref[...].astype(BF16)).astype(BF16)


def _inproj(x, modseg, norm_g, w_in, l):
    tm, tn = 2048, 512
    return pl.pallas_call(
        functools.partial(_inproj_kernel, tm=tm),
        grid=(T // tm, P_IN // tn),
        in_specs=[
            pl.BlockSpec((tm, D), lambda i, j: (i, 0)),
            pl.BlockSpec((None, tm // SEG, 8, D), lambda i, j: (l, i, 0, 0)),
            pl.BlockSpec((None, 1, D), lambda i, j: (l, 0, 0)),
            pl.BlockSpec((None, D, tn), lambda i, j: (l, 0, j)),
        ],
        out_specs=pl.BlockSpec((tm, tn), lambda i, j: (i, j)),
        out_shape=jax.ShapeDtypeStruct((T, P_IN), BF16),
        scratch_shapes=[pltpu.VMEM((tm, D), BF16)],
        compiler_params=_cp(("arbitrary", "arbitrary")),
        name="inproj",
    )(x, modseg, norm_g.reshape(DEPTH, 1, D), w_in)


REC_CW = 512
HALO = 16


def _rec_kernel(xr_ref, gate_ref, cw_ref, pv_ref, wg_ref, h0_ref,
                y_ref, stf_ref, stb_ref,
                af_ref, bf_ref, ab_ref, bb_ref):
    u = pl.program_id(0)
    is_ctx = u < (T_CTX // UNIT)
    seq_len = jnp.where(is_ctx, SEQ, DEC_SEQ)
    nblk = REC_CW // RG_BLK

    pv = pv_ref[...]
    cwts = cw_ref[...]
    conv_b = pv[6:7, :]

    def softplus_neg(lam):
        z = -lam
        return jnp.maximum(z, 0.0) + jnp.log1p(jnp.exp(-jnp.abs(z)))

    c4s = tuple((-0.5 * RG_C) * softplus_neg(pv[4 + d:5 + d, :]) for d in range(2))
    pv_h = 0.5 * pv
    a_refs = (af_ref, ab_ref)
    b_refs = (bf_ref, bb_ref)

    def gates(ci, carry):
        base = pl.multiple_of(ci * CHUNK, CHUNK)
        lo = pl.multiple_of(jnp.maximum(base - HALO, 0), HALO)
        hi = pl.multiple_of(jnp.minimum(base + CHUNK, UNIT - HALO), HALO)
        main = xr_ref[pl.ds(base, CHUNK), :].astype(F32)
        win = jnp.concatenate([xr_ref[pl.ds(lo, HALO), :].astype(F32), main,
                               xr_ref[pl.ds(hi, HALO), :].astype(F32)], axis=0)
        t = base + lax.broadcasted_iota(jnp.int32, (CHUNK, 1), 0)
        tl = jnp.bitwise_and(t, seq_len - 1)
        n_win = CHUNK + 2 * HALO
        xm2 = jnp.where(tl >= 2, pltpu.roll(win, 2, 0)[HALO:HALO + CHUNK], 0.0)
        xm1 = jnp.where(tl >= 1, pltpu.roll(win, 1, 0)[HALO:HALO + CHUNK], 0.0)
        xp1 = jnp.where(tl <= seq_len - 2, pltpu.roll(win, n_win - 1, 0)[HALO:HALO + CHUNK], 0.0)
        xc = conv_b + xm2 * cwts[0:1, :]
        xc = xc + xm1 * cwts[1:2, :]
        xc = xc + main * cwts[2:3, :]
        xc = xc + xp1 * cwts[3:4, :]
        for n in range(nblk):
            ls = slice(n * RG_BLK, (n + 1) * RG_BLK)
            xn = xc[:, ls]
            pre_h = 0.5 * _dot(xn.astype(BF16), wg_ref[n].astype(BF16))
            for d in range(2):
                th_r = jnp.tanh(pre_h[:, (2 * d) * RG_BLK:(2 * d + 1) * RG_BLK] + pv_h[2 * d:2 * d + 1, ls])
                th_i = jnp.tanh(pre_h[:, (2 * d + 1) * RG_BLK:(2 * d + 2) * RG_BLK] + pv_h[2 * d + 1:2 * d + 2, ls])
                c4 = c4s[d][:, ls]
                log_a = c4 * th_r + c4
                i = 0.5 * th_i + 0.5
                a = jnp.exp(log_a)
                s = -jnp.tanh(log_a) * (a * a + 1.0)
                inp = (s * lax.rsqrt(jnp.maximum(s, TINY))) * (i * xn)
                sbase = pl.multiple_of(ci * CSTRIDE, 8)
                a_refs[d][n, pl.ds(sbase, CHUNK), :] = a
                b_refs[d][n, pl.ds(sbase, CHUNK), :] = inp
        return carry

    lax.fori_loop(0, 8, gates, 0)

    zeros = [jnp.zeros((8, RG_BLK), F32)] * nblk
    ones = [jnp.ones((8, RG_BLK), F32)] * nblk
    hf, pf, hb, pb = list(zeros), list(ones), list(zeros), list(ones)
    for r in range(CHUNK):
        rows_f = pl.ds(r, 8, stride=CSTRIDE)
        rows_b = pl.ds(CHUNK - 1 - r, 8, stride=CSTRIDE)
        for n in range(nblk):
            a = af_ref[n, rows_f, :]
            hf[n] = a * hf[n] + bf_ref[n, rows_f, :]
            pf[n] = a * pf[n]
            bf_ref[n, rows_f, :] = hf[n]
            af_ref[n, rows_f, :] = pf[n]
            a = ab_ref[n, rows_b, :]
            hb[n] = a * hb[n] + bb_ref[n, rows_b, :]
            pb[n] = a * pb[n]
            bb_ref[n, rows_b, :] = hb[n]
            ab_ref[n, rows_b, :] = pb[n]
    hf, pf, hb, pb = (jnp.concatenate(v, axis=-1) for v in (hf, pf, hb, pb))

    cps = jnp.where(is_ctx, SEQ // CHUNK, DEC_SEQ // CHUNK)
    h0f = h0_ref[0:1, :]
    h0b = h0_ref[1:2, :]
    cf = [h0f]
    for c in range(1, 8):
        chain = hf[c - 1:c, :] + pf[c - 1:c, :] * cf[c - 1]
        cf.append(jnp.where(jnp.bitwise_and(c, cps - 1) == 0, h0f, chain))
    cb = [None] * 8
    cb[7] = h0b
    for c in range(6, -1, -1):
        chain = hb[c + 1:c + 2, :] + pb[c + 1:c + 2, :] * cb[c + 1]
        cb[c] = jnp.where(jnp.bitwise_and(c, cps - 1) == cps - 1, h0b, chain)
    carry_f = jnp.concatenate(cf, axis=0)
    carry_b = jnp.concatenate(cb, axis=0)
    stf_ref[...] = hf + pf * carry_f
    stb_ref[...] = hb + pb * carry_b

    for ci in range(8):
        rows = pl.ds(ci * CHUNK, CHUNK)
        srows = pl.ds(ci * CSTRIDE, CHUNK)
        for n in range(nblk):
            ls = slice(n * RG_BLK, (n + 1) * RG_BLK)
            h_f = bf_ref[n, srows, :] + af_ref[n, srows, :] * carry_f[ci:ci + 1, ls]
            h_b = bb_ref[n, srows, :] + ab_ref[n, srows, :] * carry_b[ci:ci + 1, ls]
            g = gate_ref[rows, ls].astype(F32)
            y_ref[rows, ls] = ((h_f + h_b) * jax.nn.gelu(g, approximate=True)).astype(BF16)


def _rec(proj, conv_w, pvec, wg, h0, l):
    ncb = D // REC_CW
    return pl.pallas_call(
        _rec_kernel,
        grid=(N_UNIT, ncb),
        in_specs=[
            pl.BlockSpec((UNIT, REC_CW), lambda u, c: (u, c)),
            pl.BlockSpec((UNIT, REC_CW), lambda u, c: (u, ncb + c)),
            pl.BlockSpec((None, 4, REC_CW), lambda u, c: (l, 0, c)),
            pl.BlockSpec((None, 8, REC_CW), lambda u, c: (l, 0, c)),
            pl.BlockSpec((None, REC_CW // RG_BLK, RG_BLK, 4 * RG_BLK), lambda u, c: (l, c, 0, 0)),
            pl.BlockSpec((None, 2, REC_CW), lambda u, c: (u, 0, c)),
        ],
        out_specs=[
            pl.BlockSpec((UNIT, REC_CW), lambda u, c: (u, c)),
            pl.BlockSpec((None, 8, REC_CW), lambda u, c: (u, 0, c)),
            pl.BlockSpec((None, 8, REC_CW), lambda u, c: (u, 0, c)),
        ],
        out_shape=[
            jax.ShapeDtypeStruct((T, D), BF16),
            jax.ShapeDtypeStruct((N_UNIT, 8, D), F32),
            jax.ShapeDtypeStruct((N_UNIT, 8, D), F32),
        ],
        scratch_shapes=[pltpu.VMEM((REC_CW // RG_BLK, 8 * CSTRIDE, RG_BLK), F32)] * 4,
        compiler_params=_cp(("arbitrary", "arbitrary")),
        name="rec",
    )(proj, proj, conv_w, pvec, wg, h0)


def _head_norm(x, g128, bd):
    hi, lo = _split(x * x)
    ms = _dot(hi, bd) + _dot(lo, bd)
    return x * lax.rsqrt(ms + EPS) * g128


def _rope(x, cos, sin_signed):
    lane = lax.broadcasted_iota(jnp.int32, x.shape, 1)
    first_half = jnp.bitwise_and(lane, HD - 1) < HD // 2
    partner = jnp.where(first_half, pltpu.roll(x, 2 * HD - HD // 2, 1), pltpu.roll(x, HD // 2, 1))
    return x * cos + partner * sin_signed


def _qkv_kernel(q_ref, k_ref, v_ref, qg_ref, kg_ref, cos_ref, sin_ref, bd_ref,
                qo_ref, ko_ref, vo_ref, *, rope):
    bd = bd_ref[...]
    scale = HD ** -0.5 * float(np.log2(np.e))
    for j in range(N_HEADS // 2):
        x = _head_norm(q_ref[:, 2 * HD * j:2 * HD * (j + 1)].astype(F32), qg_ref[...], bd)
        if rope:
            x = _rope(x, cos_ref[...], sin_ref[...])
        x = x * scale
        qo_ref[2 * j] = x[:, :HD].astype(qo_ref.dtype)
        qo_ref[2 * j + 1] = x[:, HD:].astype(qo_ref.dtype)
    for j in range(N_KV // 2):
        x = _head_norm(k_ref[:, 2 * HD * j:2 * HD * (j + 1)].astype(F32), kg_ref[...], bd)
        if rope:
            x = _rope(x, cos_ref[...], sin_ref[...])
        ko_ref[2 * j] = x[:, :HD].astype(ko_ref.dtype)
        ko_ref[2 * j + 1] = x[:, HD:].astype(ko_ref.dtype)
        v = v_ref[:, 2 * HD * j:2 * HD * (j + 1)].astype(F32)
        vo_ref[2 * j] = v[:, :HD].astype(vo_ref.dtype)
        vo_ref[2 * j + 1] = v[:, HD:].astype(vo_ref.dtype)


def _qkv(proj, qg128, kg128, cos128, sin128, bd, l, latent):
    tm = SEG
    n = T_LAT // tm if latent else T_CTX // tm
    roff = T_CTX // tm if latent else 0
    per_seq = DEC_SEQ // tm
    if latent:
        kv_shape = (DEC_BATCH, N_KV, DEC_SEQ, HD)
        kv_spec = pl.BlockSpec((None, N_KV, tm, HD), lambda i: (i // per_seq, 0, i % per_seq, 0))
        kv_dtype = BF16
        tab_map = lambda i: (i % per_seq, 0)
    else:
        kv_shape = (BATCH, N_KV, SEQ, HD)
        kv_spec = pl.BlockSpec((None, N_KV, tm, HD), lambda i: (i, 0, 0, 0))
        kv_dtype = F32
        tab_map = lambda i: (0, 0)
    return pl.pallas_call(
        functools.partial(_qkv_kernel, rope=latent),
        grid=(n,),
        in_specs=[
            pl.BlockSpec((tm, D), lambda i: (roff + i, 2)),
            pl.BlockSpec((tm, N_KV * HD), lambda i: (roff + i, 3 * D // (N_KV * HD))),
            pl.BlockSpec((tm, N_KV * HD), lambda i: (roff + i, 3 * D // (N_KV * HD) + 1)),
            pl.BlockSpec((None, 1, 2 * HD), lambda i: (l, 0, 0)),
            pl.BlockSpec((None, 1, 2 * HD), lambda i: (l, 0, 0)),
            pl.BlockSpec((tm, 2 * HD), tab_map),
            pl.BlockSpec((tm, 2 * HD), tab_map),
            pl.BlockSpec((2 * HD, 2 * HD), lambda i: (0, 0)),
        ],
        out_specs=[
            pl.BlockSpec((N_HEADS, tm, HD), lambda i: (0, i, 0)),
            kv_spec,
            kv_spec,
        ],
        out_shape=[
            jax.ShapeDtypeStruct((N_HEADS, n * tm, HD), BF16),
            jax.ShapeDtypeStruct(kv_shape, kv_dtype),
            jax.ShapeDtypeStruct(kv_shape, kv_dtype),
        ],
        compiler_params=_cp(("arbitrary",)),
        name="qkv_lat" if latent else "qkv_ctx",
    )(proj, proj, proj, qg128, kg128, cos128, sin128, bd)


def _with_ones(v):
    return jnp.concatenate([v, jnp.ones_like(v)], axis=-1)


def _softmax_pv(q, k, v_ext):
    s = _dot_nt(q, k)
    m = jnp.max(s, axis=-1, keepdims=True)
    p = jnp.exp2(s - m).astype(BF16)
    r = _dot(p, v_ext)
    return r[:, :HD] / r[:, HD:HD + 1]


def _attend_heads(q_ref, k, v_ext):
    return jnp.concatenate([_softmax_pv(q_ref[h], k, v_ext) for h in range(N_HEADS // N_KV)], axis=-1)


def _attn_ctx_kernel(q_ref, k_ref, v_ref, o_ref):
    o = _attend_heads(q_ref, k_ref[...].astype(BF16), _with_ones(v_ref[...].astype(BF16)))
    o_ref[...] = o.astype(BF16)


def _attn_ctx(qh, kc, vc):
    g = N_HEADS // N_KV
    return pl.pallas_call(
        _attn_ctx_kernel,
        grid=(BATCH, N_KV),
        in_specs=[
            pl.BlockSpec((g, SEQ, HD), lambda b, h: (h, b, 0)),
            pl.BlockSpec((None, None, SEQ, HD), lambda b, h: (b, h, 0, 0)),
            pl.BlockSpec((None, None, SEQ, HD), lambda b, h: (b, h, 0, 0)),
        ],
        out_specs=pl.BlockSpec((SEQ, g * HD), lambda b, h: (b, h)),
        out_shape=jax.ShapeDtypeStruct((T_CTX, D), BF16),
        compiler_params=_cp(("arbitrary", "arbitrary")),
        name="attn_ctx",
    )(qh, kc, vc)


ATT_TQ = 256


def _attn_lat_kernel(q_ref, pk_ref, pv_ref, k_ref, v_ref, o_ref):
    k = jnp.concatenate([pk_ref[...].astype(BF16), k_ref[...]], axis=0)
    v = jnp.concatenate([pv_ref[...].astype(BF16), v_ref[...]], axis=0)
    o_ref[...] = _attend_heads(q_ref, k, _with_ones(v)).astype(BF16)


def _attn_lat(qh, cache_k, cache_v, kr, vr, l):
    g = N_HEADS // N_KV
    nq = DEC_SEQ // ATT_TQ
    return pl.pallas_call(
        _attn_lat_kernel,
        grid=(DEC_BATCH, N_KV, nq),
        in_specs=[
            pl.BlockSpec((g, ATT_TQ, HD), lambda b, h, i: (h, b * nq + i, 0)),
            pl.BlockSpec((None, None, None, PAST, HD), lambda b, h, i: (b, l, h, 0, 0)),
            pl.BlockSpec((None, None, None, PAST, HD), lambda b, h, i: (b, l, h, 0, 0)),
            pl.BlockSpec((None, None, DEC_SEQ, HD), lambda b, h, i: (b, h, 0, 0)),
            pl.BlockSpec((None, None, DEC_SEQ, HD), lambda b, h, i: (b, h, 0, 0)),
        ],
        out_specs=pl.BlockSpec((ATT_TQ, g * HD), lambda b, h, i: (b * nq + i, h)),
        out_shape=jax.ShapeDtypeStruct((T_LAT, D), BF16),
        compiler_params=_cp(("arbitrary", "arbitrary", "arbitrary")),
        name="attn_lat",
    )(qh, cache_k, cache_v, kr, vr)


MERGE_TM = 512


def _route(lt, bias):
    rows = [lt[e:e + 1, :] for e in range(N_EXP)]
    m = rows[0]
    for e in range(1, N_EXP):
        m = jnp.maximum(m, rows[e])
    ex = [jnp.exp(r - m) for r in rows]
    z = ex[0]
    for e in range(1, N_EXP):
        z = z + ex[e]
    probs = [x / z for x in ex]
    sel = [probs[e] + bias[e:e + 1, :] for e in range(N_EXP)]

    def top2_sum(v):
        a, b = jnp.maximum(v[0], v[1]), jnp.minimum(v[0], v[1])
        c, d = jnp.maximum(v[2], v[3]), jnp.minimum(v[2], v[3])
        return jnp.maximum(a, c) + jnp.maximum(jnp.minimum(a, c), jnp.maximum(b, d))

    scores = [top2_sum(sel[4 * g:4 * g + 4]) for g in range(4)]
    best = jnp.zeros_like(scores[0], dtype=jnp.int32)
    best_s = scores[0]
    for g in range(1, 4):
        take = scores[g] > best_s
        best = jnp.where(take, g, best)
        best_s = jnp.where(take, scores[g], best_s)
    cs, cp = [], []
    for j in range(4):
        s_j, p_j = sel[j], probs[j]
        for g in range(1, 4):
            s_j = jnp.where(best == g, sel[4 * g + j], s_j)
            p_j = jnp.where(best == g, probs[4 * g + j], p_j)
        cs.append(s_j)
        cp.append(p_j)
    neg = jnp.full_like(cs[0], -jnp.inf)

    def argmax4(v):
        bi = jnp.zeros_like(best)
        bv = v[0]
        for j in range(1, 4):
            take = v[j] > bv
            bi = jnp.where(take, j, bi)
            bv = jnp.where(take, v[j], bv)
        return bi

    def pick(v, idx):
        out = v[0]
        for j in range(1, 4):
            out = jnp.where(idx == j, v[j], out)
        return out

    i1 = argmax4(cs)
    cs2 = [jnp.where(i1 == j, neg, cs[j]) for j in range(4)]
    i2 = argmax4(cs2)
    i2 = jnp.where((i2 == 0) & (i1 == 0), 1, i2)
    w1, w2 = pick(cp, i1), pick(cp, i2)
    den = w1 + w2
    return best * 4 + i1, best * 4 + i2, w1 / den, w2 / den


def _merge_kernel(yrec_ref, oatt_ref, gr0_ref, gr1_ref, ga0_ref, ga1_ref, x_ref, mod_ref, g2_ref,
                  wrec_ref, watt_ref, wout_ref, wrt_ref, rb_ref,
                  x1_ref, h2_ref, idx_ref, wts_ref,
                  wrec_s, watt_s, wout_s):
    @pl.when(pl.program_id(0) == 0)
    def _():
        wrec_s[...] = wrec_ref[...].astype(BF16)
        watt_s[...] = watt_ref[...].astype(BF16)
        wout_s[...] = wout_ref[...].astype(BF16)

    half = D // 2
    b_rec = _dot(yrec_ref[...], wrec_s[...])
    b_att = _dot(oatt_ref[...], watt_s[...])
    m0 = _sigmoid(gr0_ref[...].astype(F32)) * b_rec[:, :half] + _sigmoid(ga0_ref[...].astype(F32)) * b_att[:, :half]
    m1 = _sigmoid(gr1_ref[...].astype(F32)) * b_rec[:, half:] + _sigmoid(ga1_ref[...].astype(F32)) * b_att[:, half:]
    merged = jnp.concatenate([m0, m1], axis=-1).astype(BF16)
    out = _dot(merged, wout_s[...])

    hs = []
    for s in range(MERGE_TM // SEG):
        rows = slice(s * SEG, (s + 1) * SEG)
        m = mod_ref[s]
        x1 = x_ref[rows, :] + m[2:3, :] * out[rows, :]
        x1_ref[rows, :] = x1
        h2 = _norm_mod(x1, g2_ref[...], m[3:4, :], m[4:5, :])
        h2_ref[rows, :] = h2
        hs.append(h2)
    h2 = jnp.concatenate(hs, axis=0)

    h_hi, h_lo = _split(h2)
    w_hi, w_lo = _split(wrt_ref[...])
    lt = _dot_nt(w_hi, h_hi) + _dot_nt(w_hi, h_lo) + _dot_nt(w_lo, h_hi)
    e1, e2, w1, w2 = _route(lt, rb_ref[...])
    idx_ref[...] = jnp.concatenate([e1, e2], axis=0)
    wts_ref[...] = jnp.concatenate([w1, w2], axis=0)


def _merge(yrec, oatt, proj, x, modseg, norm2_g, w_rec_out, w_att_out, w_out, wrt, rbias, l):
    tm = MERGE_TM
    half = D // 2
    gcol = (3 * D + 2 * N_KV * HD) // half
    wspec = pl.BlockSpec((None, D, D), lambda i: (l, 0, 0))
    return pl.pallas_call(
        _merge_kernel,
        grid=(T // tm,),
        in_specs=[
            pl.BlockSpec((tm, D), lambda i: (i, 0)),
            pl.BlockSpec((tm, D), lambda i: (i, 0)),
            pl.BlockSpec((tm, half), lambda i: (i, gcol)),
            pl.BlockSpec((tm, half), lambda i: (i, gcol + 1)),
            pl.BlockSpec((tm, half), lambda i: (i, gcol + 2)),
            pl.BlockSpec((tm, half), lambda i: (i, gcol + 3)),
            pl.BlockSpec((tm, D), lambda i: (i, 0)),
            pl.BlockSpec((None, tm // SEG, 8, D), lambda i: (l, i, 0, 0)),
            pl.BlockSpec((None, 1, D), lambda i: (l, 0, 0)),
            wspec, wspec, wspec,
            pl.BlockSpec((N_EXP, D), lambda i: (0, 0)),
            pl.BlockSpec((N_EXP, 1), lambda i: (0, 0)),
        ],
        out_specs=[
            pl.BlockSpec((tm, D), lambda i: (i, 0)),
            pl.BlockSpec((tm, D), lambda i: (i, 0)),
            pl.BlockSpec((2, tm), lambda i: (0, i)),
            pl.BlockSpec((2, tm), lambda i: (0, i)),
        ],
        out_shape=[
            jax.ShapeDtypeStruct((T, D), F32),
            jax.ShapeDtypeStruct((T, D), F32),
            jax.ShapeDtypeStruct((2, T), jnp.int32),
            jax.ShapeDtypeStruct((2, T), F32),
        ],
        scratch_shapes=[pltpu.VMEM((D, D), BF16)] * 3,
        compiler_params=_cp(("arbitrary",)),
        name="merge",
    )(yrec, oatt, proj, proj, proj, proj, x, modseg, norm2_g.reshape(DEPTH, 1, D),
      w_rec_out, w_att_out, w_out, wrt, rbias)


MOE_TM = 256
MOE_NT = 2 * T // MOE_TM + N_EXP
MOE_ROWS = MOE_NT * MOE_TM
META_TILE_E, META_CNT, META_OFF, META_END, META_NT = 0, 1, 2, 3, 4


def _pos_kernel(idx_ref, pos_ref, meta_ref):
    shift = MOE_TM.bit_length() - 1
    idx = idx_ref[...]
    eid = lax.broadcasted_iota(jnp.int32, (N_EXP, T), 0)
    m0 = eid == idx[0:1, :]
    m1 = eid == idx[1:2, :]
    member = jnp.where(m0 | m1, 1.0, 0.0)
    cnt = jnp.sum(member, axis=1, keepdims=True).astype(jnp.int32)
    ntile = jnp.right_shift(cnt + (MOE_TM - 1), shift)
    offs, acc = [], jnp.zeros((1, 1), jnp.int32)
    for e in range(N_EXP):
        offs.append(acc)
        acc = acc + ntile[e:e + 1, :]
    off_t = jnp.concatenate(offs, axis=0)
    end_t = off_t + ntile

    blk = 256
    r_i = lax.broadcasted_iota(jnp.int32, (blk, blk), 0)
    c_i = lax.broadcasted_iota(jnp.int32, (blk, blk), 1)
    upper = jnp.where(r_i <= c_i, 1.0, 0.0).astype(BF16)
    run = (off_t * MOE_TM).astype(F32)
    for j in range(T // blk):
        ls = slice(j * blk, (j + 1) * blk)
        mb = member[:, ls]
        inc = _dot(mb.astype(BF16), upper)
        dest = run + inc - mb
        pos_ref[0:1, ls] = jnp.sum(jnp.where(m0[:, ls], dest, 0.0), axis=0, keepdims=True).astype(jnp.int32)
        pos_ref[1:2, ls] = jnp.sum(jnp.where(m1[:, ls], dest, 0.0), axis=0, keepdims=True).astype(jnp.int32)
        run = run + inc[:, blk - 1:blk]

    lane = lax.broadcasted_iota(jnp.int32, (1, 128), 1)
    zero = jnp.zeros((1, 128), jnp.int32)
    tile_e, cnt_row, off_row, end_row = zero, zero, zero, zero
    for e in range(N_EXP):
        tile_e = tile_e + jnp.where(lane >= end_t[e:e + 1, :], 1, 0)
        here = lane == e
        cnt_row = jnp.where(here, cnt[e:e + 1, :], cnt_row)
        off_row = jnp.where(here, off_t[e:e + 1, :] * MOE_TM, off_row)
        end_row = jnp.where(here, end_t[e:e + 1, :] * MOE_TM, end_row)
    tile_e = jnp.minimum(tile_e, N_EXP - 1)
    nt_row = zero + acc
    meta_ref[...] = jnp.concatenate([tile_e, cnt_row, off_row, end_row, nt_row, zero, zero, zero], axis=0)


def _route_pos(idx):
    return pl.pallas_call(
        _pos_kernel,
        grid=(1,),
        in_specs=[pl.BlockSpec((2, T), lambda i: (0, 0))],
        out_specs=[pl.BlockSpec((2, T), lambda i: (0, 0)), pl.BlockSpec((8, 128), lambda i: (0, 0))],
        out_shape=[jax.ShapeDtypeStruct((2, T), jnp.int32), jax.ShapeDtypeStruct((8, 128), jnp.int32)],
        compiler_params=_cp(("arbitrary",)),
        name="route_pos",
    )(idx)


DISP_TM = T // N_EXP


def _dispatch_kernel(meta_ref, pos_ref, h_hbm, z_hbm, xs_hbm, sem):
    i = pl.program_id(0)
    base = i * DISP_TM

    def row_copy(src, src_row, dst_row):
        return pltpu.make_async_copy(src.at[pl.ds(src_row, 1), :], xs_hbm.at[pl.ds(dst_row, 1), :], sem)

    def issue(r, c):
        row_copy(h_hbm, base + r, pos_ref[0, r]).start()
        row_copy(h_hbm, base + r, pos_ref[1, r]).start()
        return c

    lax.fori_loop(0, DISP_TM, issue, 0, unroll=8)

    pad0 = meta_ref[META_OFF, i] + meta_ref[META_CNT, i]
    npad = meta_ref[META_END, i] - pad0

    def zero_fill(p, c):
        row_copy(z_hbm, 0, p).start()
        return c

    lax.fori_loop(pad0, pad0 + npad, zero_fill, 0)

    pltpu.make_async_copy(h_hbm.at[pl.ds(0, 2 * DISP_TM), :], xs_hbm.at[pl.ds(0, 2 * DISP_TM), :], sem).wait()

    tail = meta_ref[META_NT, 0] + i

    def tail_copy():
        rows = pl.ds(pl.multiple_of(tail * MOE_TM, MOE_TM), MOE_TM)
        return pltpu.make_async_copy(z_hbm, xs_hbm.at[rows, :], sem)

    @pl.when(tail < MOE_NT)
    def _():
        tail_copy().start()

    def zero_wait(p, c):
        row_copy(z_hbm, 0, p).wait()
        return c

    lax.fori_loop(pad0, pad0 + npad, zero_wait, 0)

    @pl.when(tail < MOE_NT)
    def _():
        tail_copy().wait()


def _dispatch(meta, pos, h2, zrow):
    return pl.pallas_call(
        _dispatch_kernel,
        grid_spec=pltpu.PrefetchScalarGridSpec(
            num_scalar_prefetch=1,
            grid=(N_EXP,),
            in_specs=[
                pl.BlockSpec((2, DISP_TM), lambda i, meta: (0, i), memory_space=pltpu.SMEM),
                pl.BlockSpec(memory_space=pl.ANY),
                pl.BlockSpec(memory_space=pl.ANY),
            ],
            out_specs=pl.BlockSpec(memory_space=pl.ANY),
            scratch_shapes=[pltpu.SemaphoreType.DMA],
        ),
        out_shape=jax.ShapeDtypeStruct((MOE_ROWS, D), F32),
        compiler_params=_cp(("arbitrary",)),
        name="dispatch",
    )(meta, pos, h2, zrow)


def _experts_kernel(meta_ref, xs_ref, wg_ref, wu_ref, wd_ref, ys_ref, wg_s, wu_s, wd_s):
    j = pl.program_id(0)
    live = j < meta_ref[META_NT, 0]
    e = meta_ref[META_TILE_E, j]
    e_prev = meta_ref[META_TILE_E, jnp.maximum(j - 1, 0)]

    @pl.when(live & ((j == 0) | (e != e_prev)))
    def _():
        wg_s[...] = wg_ref[...].astype(BF16)
        wu_s[...] = wu_ref[...].astype(BF16)
        wd_s[...] = wd_ref[...].astype(BF16)

    @pl.when(live)
    def _():
        x = xs_ref[...].astype(BF16)
        g = _dot(x, wg_s[...])
        u = _dot(x, wu_s[...])
        act = (g * _sigmoid(g)) * u
        ys_ref[...] = _dot(act.astype(BF16), wd_s[...])

    @pl.when(jnp.logical_not(live))
    def _():
        ys_ref[...] = jnp.zeros_like(ys_ref)


def _experts(meta, xs, w_gate_e, w_up_e, w_down_e, l):
    def tile(j, meta):
        return jnp.minimum(j, meta[META_NT, 0] - 1)

    def wmap(j, meta):
        return (l, meta[META_TILE_E, tile(j, meta)], 0, 0)

    return pl.pallas_call(
        _experts_kernel,
        grid_spec=pltpu.PrefetchScalarGridSpec(
            num_scalar_prefetch=1,
            grid=(MOE_NT,),
            in_specs=[
                pl.BlockSpec((MOE_TM, D), lambda j, meta: (tile(j, meta), 0)),
                pl.BlockSpec((None, None, D, D_EXP), wmap),
                pl.BlockSpec((None, None, D, D_EXP), wmap),
                pl.BlockSpec((None, None, D_EXP, D), wmap),
            ],
            out_specs=pl.BlockSpec((MOE_TM, D), lambda j, meta: (j, 0)),
            scratch_shapes=[pltpu.VMEM((D, D_EXP), BF16), pltpu.VMEM((D, D_EXP), BF16),
                            pltpu.VMEM((D_EXP, D), BF16)],
        ),
        out_shape=jax.ShapeDtypeStruct((MOE_ROWS, D), F32),
        compiler_params=_cp(("arbitrary",)),
        name="experts",
    )(meta, xs, w_gate_e, w_up_e, w_down_e)


COMB_TM = SEG


def _combine_kernel(pos_ref, posn_ref, w_ref, x1_ref, mod_ref, fg_ref, ys_hbm, o_ref, buf, sem, *, final):
    i = pl.program_id(0)
    n = pl.num_programs(0)
    slot = lax.rem(i, 2)

    def issue(p_ref, s):
        def body(r, c):
            for k in range(2):
                pltpu.make_async_copy(ys_hbm.at[pl.ds(p_ref[k, r], 1), :],
                                      buf.at[s, k, pl.ds(r, 1), :], sem.at[s]).start()
            return c
        lax.fori_loop(0, COMB_TM, body, 0, unroll=8)

    @pl.when(i == 0)
    def _():
        issue(pos_ref, 0)

    @pl.when(i + 1 < n)
    def _():
        issue(posn_ref, 1 - slot)

    for k in range(2):
        pltpu.make_async_copy(ys_hbm.at[pl.ds(0, COMB_TM), :], buf.at[slot, k], sem.at[slot]).wait()

    w = w_ref[...]
    y = w[:, 0:1] * buf[slot, 0] + w[:, 1:2] * buf[slot, 1]
    x = x1_ref[...] + mod_ref[5:6, :] * y
    if final:
        ms = jnp.mean(x * x, axis=-1, keepdims=True)
        x = x * lax.rsqrt(ms + EPS) * fg_ref[...]
    o_ref[...] = x


def _combine(pos, wts_t, x1, modseg, final_g, ys, l, final):
    n = T // COMB_TM
    return pl.pallas_call(
        functools.partial(_combine_kernel, final=final),
        grid=(n,),
        in_specs=[
            pl.BlockSpec((2, COMB_TM), lambda i: (0, i), memory_space=pltpu.SMEM),
            pl.BlockSpec((2, COMB_TM), lambda i: (0, jnp.minimum(i + 1, n - 1)), memory_space=pltpu.SMEM),
            pl.BlockSpec((COMB_TM, 2), lambda i: (i, 0)),
            pl.BlockSpec((COMB_TM, D), lambda i: (i, 0)),
            pl.BlockSpec((None, None, 8, D), lambda i: (l, i, 0, 0)),
            pl.BlockSpec((1, D), lambda i: (0, 0)),
            pl.BlockSpec(memory_space=pl.ANY),
        ],
        out_specs=pl.BlockSpec((COMB_TM, D), lambda i: (i, 0)),
        out_shape=jax.ShapeDtypeStruct((T, D), F32),
        scratch_shapes=[pltpu.VMEM((2, 2, COMB_TM, D), F32), pltpu.SemaphoreType.DMA((2,))],
        compiler_params=_cp(("arbitrary",)),
        name="combine",
    )(pos, pos, wts_t, x1, modseg, final_g.reshape(1, D), ys)


def _rope_tables():
    n = DEC_SEQ
    pos_row = np.repeat(np.arange(n // GRID_W, dtype=np.float32), GRID_W)
    pos_col = np.tile(np.arange(GRID_W, dtype=np.float32), n // GRID_W)
    half = HD // 2
    inv_freq = jnp.asarray(ROPE_THETA, F32) ** (-jnp.arange(0, half, 2, dtype=F32) / half)
    ang = jnp.concatenate([jnp.asarray(pos_row)[:, None] * inv_freq,
                           jnp.asarray(pos_col)[:, None] * inv_freq], axis=-1)
    cos, sin = jnp.cos(ang), jnp.sin(ang)
    cos128 = jnp.tile(cos, (1, 4))
    sin128 = jnp.tile(jnp.concatenate([-sin, sin], axis=-1), (1, 2))
    return cos128, sin128


def _head_mean_matrix():
    idx = np.arange(2 * HD)
    same = (idx[:, None] // HD) == (idx[None, :] // HD)
    return jnp.asarray(same.astype(np.float32) / HD, BF16)


_SEG_ROWS = np.array([0] * (T_CTX // SEG) + [1 + b for b in range(DEC_BATCH) for _ in range(DEC_SEQ // SEG)])


def kernel(x_prompt, x_sample, cache_k, cache_v, state_rec, c, c_ctx, w_mod, b_mod, norm1_g, norm2_g, w_in, conv_w, conv_b, rg_wa, rg_ba, rg_wx, rg_bx, rg_lambda, q_norm_g, k_norm_g, w_rec_out, w_att_out, w_out, w_router, router_bias, w_gate_e, w_up_e, w_down_e, final_g):
    x = jnp.concatenate([x_prompt.reshape(T_CTX, D), x_sample.reshape(T_LAT, D)], axis=0)

    cvecs = jnp.concatenate([c_ctx[None, :], c, jnp.zeros((3, D), F32)], axis=0)
    mods = _mods(cvecs, w_mod, b_mod).reshape(DEPTH, 8, 6, D)
    modseg = jnp.pad(mods[:, _SEG_ROWS], ((0, 0), (0, 0), (0, 2), (0, 0)))

    cos128, sin128 = _rope_tables()
    bd = _head_mean_matrix()
    qg128 = jnp.tile(q_norm_g, (1, 2)).reshape(DEPTH, 1, 2 * HD)
    kg128 = jnp.tile(k_norm_g, (1, 2)).reshape(DEPTH, 1, 2 * HD)
    wg = jnp.concatenate([rg_wa[:, 0], rg_wx[:, 0], rg_wa[:, 1], rg_wx[:, 1]], axis=-1)
    pvec = jnp.stack([rg_ba[:, 0], rg_bx[:, 0], rg_ba[:, 1], rg_bx[:, 1],
                      rg_lambda[:, 0], rg_lambda[:, 1], conv_b, jnp.zeros_like(conv_b)], axis=1)
    wrt = w_router.T
    rbias = router_bias.reshape(N_EXP, 1)
    zrow = jnp.zeros((MOE_TM, D), F32)

    new_k, new_v, new_s = [], [], []
    for l in range(DEPTH):
        proj = _inproj(x, modseg, norm1_g, w_in, l)
        h0 = jnp.concatenate([jnp.zeros((T_CTX // UNIT, 2, D), F32), state_rec[:, l]], axis=0)
        yrec, stf, stb = _rec(proj, conv_w, pvec, wg, h0, l)
        qc, kc, vc = _qkv(proj, qg128, kg128, cos128, sin128, bd, l, latent=False)
        ql, kl, vl = _qkv(proj, qg128, kg128, cos128, sin128, bd, l, latent=True)
        o_ctx = _attn_ctx(qc, kc, vc)
        o_lat = _attn_lat(ql, cache_k, cache_v, kl, vl, l)
        oatt = jnp.concatenate([o_ctx, o_lat], axis=0)
        x1, h2, idx, wts = _merge(yrec, oatt, proj, x, modseg, norm2_g,
                                  w_rec_out, w_att_out, w_out, wrt, rbias, l)
        pos, meta = _route_pos(idx)
        xs = _dispatch(meta, pos, h2, zrow)
        ys = _experts(meta, xs, w_gate_e, w_up_e, w_down_e, l)
        x = _combine(pos, wts.T, x1, modseg, final_g, ys, l, final=(l == DEPTH - 1))
        new_k.append(kc)
        new_v.append(vc)
        n_cu = T_CTX // UNIT
        spu = UNIT // SEQ
        hf_last = stf[:n_cu].reshape(n_cu, spu, 2, D)[:, :, 1].reshape(BATCH, D)
        hb_first = stb[:n_cu].reshape(n_cu, spu, 2, D)[:, :, 0].reshape(BATCH, D)
        new_s.append(jnp.stack([hf_last, hb_first], axis=1))

    y_prompt = x[:T_CTX].reshape(BATCH, SEQ, D)
    y_sample = x[T_CTX:].reshape(DEC_BATCH, DEC_SEQ, D)
    return (y_prompt, y_sample, jnp.stack(new_k, axis=1), jnp.stack(new_v, axis=1), jnp.stack(new_s, axis=1))
```

```python
import functools

import numpy as np
import jax
import jax.numpy as jnp
from jax import lax
from jax.experimental import pallas as pl
from jax.experimental.pallas import tpu as pltpu

F32 = jnp.float32
BF16 = jnp.bfloat16

D = 1024
BATCH = 16
SEQ = 256
DEPTH = 2
DEC_BATCH = 4
DEC_SEQ = 1024
PAST = 256
GRID_W = 64
N_HEADS = 16
N_KV = 4
HD = 64
N_RG_BLK = 8
RG_BLK = 128
RG_C = 8.0
N_EXP = 16
D_EXP = 512
ROPE_THETA = 10000.0
EPS = 1e-6
P_IN = 5632
TINY = float(np.finfo(np.float32).tiny)

T_CTX = BATCH * SEQ
T_LAT = DEC_BATCH * DEC_SEQ
T = T_CTX + T_LAT
SEG = 256
N_SEG = T // SEG
UNIT = 1024
N_UNIT = T // UNIT
CHUNK = UNIT // 8
CSTRIDE = CHUNK + 8

VMEM_LIMIT = 56 * 1024 * 1024


def _cp(sem):
    return pltpu.CompilerParams(dimension_semantics=sem, vmem_limit_bytes=VMEM_LIMIT)


def _split(x):
    hi = x.astype(BF16)
    lo = (x - hi.astype(F32)).astype(BF16)
    return hi, lo


def _sigmoid(x):
    return 0.5 * jnp.tanh(0.5 * x) + 0.5


def _dot(a, b):
    return jnp.dot(a, b, preferred_element_type=F32)


def _dot_nt(a, b):
    return lax.dot_general(a, b, (((1,), (1,)), ((), ())), preferred_element_type=F32)


def _mods_kernel(c_ref, w_ref, b_ref, o_ref):
    c = c_ref[...]
    s = c * jax.nn.sigmoid(c)
    s_hi, s_lo = _split(s)
    w_hi, w_lo = _split(w_ref[...])
    o_ref[...] = _dot(s_hi, w_hi) + _dot(s_hi, w_lo) + _dot(s_lo, w_hi) + b_ref[...]


def _mods(cvecs, w_mod, b_mod):
    tn = 1536
    return pl.pallas_call(
        _mods_kernel,
        grid=(DEPTH, 6 * D // tn),
        in_specs=[
            pl.BlockSpec((8, D), lambda l, j: (0, 0)),
            pl.BlockSpec((None, D, tn), lambda l, j: (l, 0, j)),
            pl.BlockSpec((None, 1, tn), lambda l, j: (l, 0, j)),
        ],
        out_specs=pl.BlockSpec((None, 8, tn), lambda l, j: (l, 0, j)),
        out_shape=jax.ShapeDtypeStruct((DEPTH, 8, 6 * D), F32),
        compiler_params=_cp(("arbitrary", "arbitrary")),
        name="mods",
    )(cvecs, w_mod, b_mod.reshape(DEPTH, 1, 6 * D))


def _norm_mod(x, g, shift, scale):
    ms = jnp.mean(x * x, axis=-1, keepdims=True)
    return x * lax.rsqrt(ms + EPS) * g * (1.0 + scale) + shift


def _inproj_kernel(x_ref, mod_ref, g_ref, w_ref, o_ref, h_ref, *, tm):
    @pl.when(pl.program_id(1) == 0)
    def _():
        def seg(s, carry):
            r0 = pl.multiple_of(s * SEG, SEG)
            m = mod_ref[s]
            h = _norm_mod(x_ref[pl.ds(r0, SEG), :], g_ref[...], m[0:1, :], m[1:2, :])
            h_ref[pl.ds(r0, SEG), :] = h.astype(BF16)
            return carry
        lax.fori_loop(0, tm // SEG, seg, 0)

    o_ref[...] = _dot(h_ref[...], w_ref[...].astype(BF16)).astype(BF16)


def _inproj(x, modseg, norm_g, w_in, l):
    tm, tn = 2048, 512
    return pl.pallas_call(
        functools.partial(_inproj_kernel, tm=tm),
        grid=(T // tm, P_IN // tn),
        in_specs=[
            pl.BlockSpec((tm, D), lambda i, j: (i, 0)),
            pl.BlockSpec((None, tm // SEG, 8, D), lambda i, j: (l, i, 0, 0)),
            pl.BlockSpec((None, 1, D), lambda i, j: (l, 0, 0)),
            pl.BlockSpec((None, D, tn), lambda i, j: (l, 0, j)),
        ],
        out_specs=pl.BlockSpec((tm, tn), lambda i, j: (i, j)),
        out_shape=jax.ShapeDtypeStruct((T, P_IN), BF16),
        scratch_shapes=[pltpu.VMEM((tm, D), BF16)],
        compiler_params=_cp(("arbitrary", "arbitrary")),
        name="inproj",
    )(x, modseg, norm_g.reshape(DEPTH, 1, D), w_in)


REC_CW = 512
HALO = 16


def _rec_kernel(xr_ref, gate_ref, cw_ref, pv_ref, wg_ref, h0_ref,
                y_ref, stf_ref, stb_ref,
                af_ref, bf_ref, ab_ref, bb_ref):
    u = pl.program_id(0)
    is_ctx = u < (T_CTX // UNIT)
    seq_len = jnp.where(is_ctx, SEQ, DEC_SEQ)
    nblk = REC_CW // RG_BLK

    pv = pv_ref[...]
    cwts = cw_ref[...]
    conv_b = pv[6:7, :]

    def softplus_neg(lam):
        z = -lam
        return jnp.maximum(z, 0.0) + jnp.log1p(jnp.exp(-jnp.abs(z)))

    c4s = tuple((-0.5 * RG_C) * softplus_neg(pv[4 + d:5 + d, :]) for d in range(2))
    pv_h = 0.5 * pv
    a_refs = (af_ref, ab_ref)
    b_refs = (bf_ref, bb_ref)

    def gates(ci, carry):
        base = pl.multiple_of(ci * CHUNK, CHUNK)
        lo = pl.multiple_of(jnp.maximum(base - HALO, 0), HALO)
        hi = pl.multiple_of(jnp.minimum(base + CHUNK, UNIT - HALO), HALO)
        main = xr_ref[pl.ds(base, CHUNK), :].astype(F32)
        win = jnp.concatenate([xr_ref[pl.ds(lo, HALO), :].astype(F32), main,
                               xr_ref[pl.ds(hi, HALO), :].astype(F32)], axis=0)
        t = base + lax.broadcasted_iota(jnp.int32, (CHUNK, 1), 0)
        tl = jnp.bitwise_and(t, seq_len - 1)
        n_win = CHUNK + 2 * HALO
        xm2 = jnp.where(tl >= 2, pltpu.roll(win, 2, 0)[HALO:HALO + CHUNK], 0.0)
        xm1 = jnp.where(tl >= 1, pltpu.roll(win, 1, 0)[HALO:HALO + CHUNK], 0.0)
        xp1 = jnp.where(tl <= seq_len - 2, pltpu.roll(win, n_win - 1, 0)[HALO:HALO + CHUNK], 0.0)
        xc = conv_b + xm2 * cwts[0:1, :]
        xc = xc + xm1 * cwts[1:2, :]
        xc = xc + main * cwts[2:3, :]
        xc = xc + xp1 * cwts[3:4, :]
        for n in range(nblk):
            ls = slice(n * RG_BLK, (n + 1) * RG_BLK)
            xn = xc[:, ls]
            pre_h = 0.5 * _dot(xn.astype(BF16), wg_ref[n].astype(BF16))
            for d in range(2):
                th_r = jnp.tanh(pre_h[:, (2 * d) * RG_BLK:(2 * d + 1) * RG_BLK] + pv_h[2 * d:2 * d + 1, ls])
                th_i = jnp.tanh(pre_h[:, (2 * d + 1) * RG_BLK:(2 * d + 2) * RG_BLK] + pv_h[2 * d + 1:2 * d + 2, ls])
                c4 = c4s[d][:, ls]
                log_a = c4 * th_r + c4
                i = 0.5 * th_i + 0.5
                a = jnp.exp(log_a)
                s = -jnp.tanh(log_a) * (a * a + 1.0)
                inp = (s * lax.rsqrt(jnp.maximum(s, TINY))) * (i * xn)
                sbase = pl.multiple_of(ci * CSTRIDE, 8)
                a_refs[d][n, pl.ds(sbase, CHUNK), :] = a
                b_refs[d][n, pl.ds(sbase, CHUNK), :] = inp
        return carry

    lax.fori_loop(0, 8, gates, 0)

    zeros = [jnp.zeros((8, RG_BLK), F32)] * nblk
    ones = [jnp.ones((8, RG_BLK), F32)] * nblk
    hf, pf, hb, pb = list(zeros), list(ones), list(zeros), list(ones)
    for r in range(CHUNK):
        rows_f = pl.ds(r, 8, stride=CSTRIDE)
        rows_b = pl.ds(CHUNK - 1 - r, 8, stride=CSTRIDE)
        for n in range(nblk):
            a = af_ref[n, rows_f, :]
            hf[n] = a * hf[n] + bf_ref[n, rows_f, :]
            pf[n] = a * pf[n]
            bf_ref[n, rows_f, :] = hf[n]
            af_ref[n, rows_f, :] = pf[n]
            a = ab_ref[n, rows_b, :]
            hb[n] = a * hb[n] + bb_ref[n, rows_b, :]
            pb[n] = a * pb[n]
            bb_ref[n, rows_b, :] = hb[n]
            ab_ref[n, rows_b, :] = pb[n]
    hf, pf, hb, pb = (jnp.concatenate(v, axis=-1) for v in (hf, pf, hb, pb))

    cps = jnp.where(is_ctx, SEQ // CHUNK, DEC_SEQ // CHUNK)
    h0f = h0_ref[0:1, :]
    h0b = h0_ref[1:2, :]
    cf = [h0f]
    for c in range(1, 8):
        chain = hf[c - 1:c, :] + pf[c - 1:c, :] * cf[c - 1]
        cf.append(jnp.where(jnp.bitwise_and(c, cps - 1) == 0, h0f, chain))
    cb = [None] * 8
    cb[7] = h0b
    for c in range(6, -1, -1):
        chain = hb[c + 1:c + 2, :] + pb[c + 1:c + 2, :] * cb[c + 1]
        cb[c] = jnp.where(jnp.bitwise_and(c, cps - 1) == cps - 1, h0b, chain)
    carry_f = jnp.concatenate(cf, axis=0)
    carry_b = jnp.concatenate(cb, axis=0)
    stf_ref[...] = hf + pf * carry_f
    stb_ref[...] = hb + pb * carry_b

    for ci in range(8):
        rows = pl.ds(ci * CHUNK, CHUNK)
        srows = pl.ds(ci * CSTRIDE, CHUNK)
        for n in range(nblk):
            ls = slice(n * RG_BLK, (n + 1) * RG_BLK)
            h_f = bf_ref[n, srows, :] + af_ref[n, srows, :] * carry_f[ci:ci + 1, ls]
            h_b = bb_ref[n, srows, :] + ab_ref[n, srows, :] * carry_b[ci:ci + 1, ls]
            g = gate_ref[rows, ls].astype(F32)
            y_ref[rows, ls] = ((h_f + h_b) * jax.nn.gelu(g, approximate=True)).astype(BF16)


def _rec(proj, conv_w, pvec, wg, h0, l):
    ncb = D // REC_CW
    return pl.pallas_call(
        _rec_kernel,
        grid=(N_UNIT, ncb),
        in_specs=[
            pl.BlockSpec((UNIT, REC_CW), lambda u, c: (u, c)),
            pl.BlockSpec((UNIT, REC_CW), lambda u, c: (u, ncb + c)),
            pl.BlockSpec((None, 4, REC_CW), lambda u, c: (l, 0, c)),
            pl.BlockSpec((None, 8, REC_CW), lambda u, c: (l, 0, c)),
            pl.BlockSpec((None, REC_CW // RG_BLK, RG_BLK, 4 * RG_BLK), lambda u, c: (l, c, 0, 0)),
            pl.BlockSpec((None, 2, REC_CW), lambda u, c: (u, 0, c)),
        ],
        out_specs=[
            pl.BlockSpec((UNIT, REC_CW), lambda u, c: (u, c)),
            pl.BlockSpec((None, 8, REC_CW), lambda u, c: (u, 0, c)),
            pl.BlockSpec((None, 8, REC_CW), lambda u, c: (u, 0, c)),
        ],
        out_shape=[
            jax.ShapeDtypeStruct((T, D), BF16),
            jax.ShapeDtypeStruct((N_UNIT, 8, D), F32),
            jax.ShapeDtypeStruct((N_UNIT, 8, D), F32),
        ],
        scratch_shapes=[pltpu.VMEM((REC_CW // RG_BLK, 8 * CSTRIDE, RG_BLK), F32)] * 4,
        compiler_params=_cp(("arbitrary", "arbitrary")),
        name="rec",
    )(proj, proj, conv_w, pvec, wg, h0)


def _head_norm(x, g128, bd):
    hi, lo = _split(x * x)
    ms = _dot(hi, bd) + _dot(lo, bd)
    return x * lax.rsqrt(ms + EPS) * g128


def _rope(x, cos, sin_signed):
    lane = lax.broadcasted_iota(jnp.int32, x.shape, 1)
    first_half = jnp.bitwise_and(lane, HD - 1) < HD // 2
    partner = jnp.where(first_half, pltpu.roll(x, 2 * HD - HD // 2, 1), pltpu.roll(x, HD // 2, 1))
    return x * cos + partner * sin_signed


def _qkv_kernel(q_ref, k_ref, v_ref, qg_ref, kg_ref, cos_ref, sin_ref, bd_ref,
                qo_ref, ko_ref, vo_ref, *, rope):
    bd = bd_ref[...]
    scale = HD ** -0.5 * float(np.log2(np.e))
    for j in range(N_HEADS // 2):
        x = _head_norm(q_ref[:, 2 * HD * j:2 * HD * (j + 1)].astype(F32), qg_ref[...], bd)
        if rope:
            x = _rope(x, cos_ref[...], sin_ref[...])
        x = x * scale
        qo_ref[2 * j] = x[:, :HD].astype(qo_ref.dtype)
        qo_ref[2 * j + 1] = x[:, HD:].astype(qo_ref.dtype)
    for j in range(N_KV // 2):
        x = _head_norm(k_ref[:, 2 * HD * j:2 * HD * (j + 1)].astype(F32), kg_ref[...], bd)
        if rope:
            x = _rope(x, cos_ref[...], sin_ref[...])
        ko_ref[2 * j] = x[:, :HD].astype(ko_ref.dtype)
        ko_ref[2 * j + 1] = x[:, HD:].astype(ko_ref.dtype)
        v = v_ref[:, 2 * HD * j:2 * HD * (j + 1)].astype(F32)
        vo_ref[2 * j] = v[:, :HD].astype(vo_ref.dtype)
        vo_ref[2 * j + 1] = v[:, HD:].astype(vo_ref.dtype)


def _qkv(proj, qg128, kg128, cos128, sin128, bd, l, latent):
    tm = SEG
    n = T_LAT // tm if latent else T_CTX // tm
    roff = T_CTX // tm if latent else 0
    per_seq = DEC_SEQ // tm
    if latent:
        kv_shape = (DEC_BATCH, N_KV, DEC_SEQ, HD)
        kv_spec = pl.BlockSpec((None, N_KV, tm, HD), lambda i: (i // per_seq, 0, i % per_seq, 0))
        kv_dtype = BF16
        tab_map = lambda i: (i % per_seq, 0)
    else:
        kv_shape = (BATCH, N_KV, SEQ, HD)
        kv_spec = pl.BlockSpec((None, N_KV, tm, HD), lambda i: (i, 0, 0, 0))
        kv_dtype = F32
        tab_map = lambda i: (0, 0)
    return pl.pallas_call(
        functools.partial(_qkv_kernel, rope=latent),
        grid=(n,),
        in_specs=[
            pl.BlockSpec((tm, D), lambda i: (roff + i, 2)),
            pl.BlockSpec((tm, N_KV * HD), lambda i: (roff + i, 3 * D // (N_KV * HD))),
            pl.BlockSpec((tm, N_KV * HD), lambda i: (roff + i, 3 * D // (N_KV * HD) + 1)),
            pl.BlockSpec((None, 1, 2 * HD), lambda i: (l, 0, 0)),
            pl.BlockSpec((None, 1, 2 * HD), lambda i: (l, 0, 0)),
            pl.BlockSpec((tm, 2 * HD), tab_map),
            pl.BlockSpec((tm, 2 * HD), tab_map),
            pl.BlockSpec((2 * HD, 2 * HD), lambda i: (0, 0)),
        ],
        out_specs=[
            pl.BlockSpec((N_HEADS, tm, HD), lambda i: (0, i, 0)),
            kv_spec,
            kv_spec,
        ],
        out_shape=[
            jax.ShapeDtypeStruct((N_HEADS, n * tm, HD), BF16),
            jax.ShapeDtypeStruct(kv_shape, kv_dtype),
            jax.ShapeDtypeStruct(kv_shape, kv_dtype),
        ],
        compiler_params=_cp(("arbitrary",)),
        name="qkv_lat" if latent else "qkv_ctx",
    )(proj, proj, proj, qg128, kg128, cos128, sin128, bd)


def _with_ones(v):
    return jnp.concatenate([v, jnp.ones_like(v)], axis=-1)


def _softmax_pv(q, k, v_ext):
    s = _dot_nt(q, k)
    m = jnp.max(s, axis=-1, keepdims=True)
    p = jnp.exp2(s - m).astype(BF16)
    r = _dot(p, v_ext)
    return r[:, :HD] / r[:, HD:HD + 1]


def _attend_heads(q_ref, k, v_ext):
    return jnp.concatenate([_softmax_pv(q_ref[h], k, v_ext) for h in range(N_HEADS // N_KV)], axis=-1)


def _attn_ctx_kernel(q_ref, k_ref, v_ref, o_ref):
    o = _attend_heads(q_ref, k_ref[...].astype(BF16), _with_ones(v_ref[...].astype(BF16)))
    o_ref[...] = o.astype(BF16)


def _attn_ctx(qh, kc, vc):
    g = N_HEADS // N_KV
    return pl.pallas_call(
        _attn_ctx_kernel,
        grid=(BATCH, N_KV),
        in_specs=[
            pl.BlockSpec((g, SEQ, HD), lambda b, h: (h, b, 0)),
            pl.BlockSpec((None, None, SEQ, HD), lambda b, h: (b, h, 0, 0)),
            pl.BlockSpec((None, None, SEQ, HD), lambda b, h: (b, h, 0, 0)),
        ],
        out_specs=pl.BlockSpec((SEQ, g * HD), lambda b, h: (b, h)),
        out_shape=jax.ShapeDtypeStruct((T_CTX, D), BF16),
        compiler_params=_cp(("arbitrary", "arbitrary")),
        name="attn_ctx",
    )(qh, kc, vc)


ATT_TQ = 256


def _attn_lat_kernel(q_ref, pk_ref, pv_ref, k_ref, v_ref, o_ref):
    k = jnp.concatenate([pk_ref[...].astype(BF16), k_ref[...]], axis=0)
    v = jnp.concatenate([pv_ref[...].astype(BF16), v_ref[...]], axis=0)
    o_ref[...] = _attend_heads(q_ref, k, _with_ones(v)).astype(BF16)


def _attn_lat(qh, cache_k, cache_v, kr, vr, l):
    g = N_HEADS // N_KV
    nq = DEC_SEQ // ATT_TQ
    return pl.pallas_call(
        _attn_lat_kernel,
        grid=(DEC_BATCH, N_KV, nq),
        in_specs=[
            pl.BlockSpec((g, ATT_TQ, HD), lambda b, h, i: (h, b * nq + i, 0)),
            pl.BlockSpec((None, None, None, PAST, HD), lambda b, h, i: (b, l, h, 0, 0)),
            pl.BlockSpec((None, None, None, PAST, HD), lambda b, h, i: (b, l, h, 0, 0)),
            pl.BlockSpec((None, None, DEC_SEQ, HD), lambda b, h, i: (b, h, 0, 0)),
            pl.BlockSpec((None, None, DEC_SEQ, HD), lambda b, h, i: (b, h, 0, 0)),
        ],
        out_specs=pl.BlockSpec((ATT_TQ, g * HD), lambda b, h, i: (b * nq + i, h)),
        out_shape=jax.ShapeDtypeStruct((T_LAT, D), BF16),
        compiler_params=_cp(("arbitrary", "arbitrary", "arbitrary")),
        name="attn_lat",
    )(qh, cache_k, cache_v, kr, vr)


MERGE_TM = 512


def _route(lt, bias):
    rows = [lt[e:e + 1, :] for e in range(N_EXP)]
    m = rows[0]
    for e in range(1, N_EXP):
        m = jnp.maximum(m, rows[e])
    ex = [jnp.exp(r - m) for r in rows]
    z = ex[0]
    for e in range(1, N_EXP):
        z = z + ex[e]
    probs = [x / z for x in ex]
    sel = [probs[e] + bias[e:e + 1, :] for e in range(N_EXP)]

    def top2_sum(v):
        a, b = jnp.maximum(v[0], v[1]), jnp.minimum(v[0], v[1])
        c, d = jnp.maximum(v[2], v[3]), jnp.minimum(v[2], v[3])
        return jnp.maximum(a, c) + jnp.maximum(jnp.minimum(a, c), jnp.maximum(b, d))

    scores = [top2_sum(sel[4 * g:4 * g + 4]) for g in range(4)]
    best = jnp.zeros_like(scores[0], dtype=jnp.int32)
    best_s = scores[0]
    for g in range(1, 4):
        take = scores[g] > best_s
        best = jnp.where(take, g, best)
        best_s = jnp.where(take, scores[g], best_s)
    cs, cp = [], []
    for j in range(4):
        s_j, p_j = sel[j], probs[j]
        for g in range(1, 4):
            s_j = jnp.where(best == g, sel[4 * g + j], s_j)
            p_j = jnp.where(best == g, probs[4 * g + j], p_j)
        cs.append(s_j)
        cp.append(p_j)
    neg = jnp.full_like(cs[0], -jnp.inf)

    def argmax4(v):
        bi = jnp.zeros_like(best)
        bv = v[0]
        for j in range(1, 4):
            take = v[j] > bv
            bi = jnp.where(take, j, bi)
            bv = jnp.where(take, v[j], bv)
        return bi

    def pick(v, idx):
        out = v[0]
        for j in range(1, 4):
            out = jnp.where(idx == j, v[j], out)
        return out

    i1 = argmax4(cs)
    cs2 = [jnp.where(i1 == j, neg, cs[j]) for j in range(4)]
    i2 = argmax4(cs2)
    i2 = jnp.where((i2 == 0) & (i1 == 0), 1, i2)
    w1, w2 = pick(cp, i1), pick(cp, i2)
    den = w1 + w2
    return best * 4 + i1, best * 4 + i2, w1 / den, w2 / den


def _merge_kernel(yrec_ref, oatt_ref, gr0_ref, gr1_ref, ga0_ref, ga1_ref, x_ref, mod_ref, g2_ref,
                  wrec_ref, watt_ref, wout_ref, wrt_ref, rb_ref,
                  x1_ref, h2_ref, idx_ref, wts_ref,
                  wrec_s, watt_s, wout_s):
    @pl.when(pl.program_id(0) == 0)
    def _():
        wrec_s[...] = wrec_ref[...].astype(BF16)
        watt_s[...] = watt_ref[...].astype(BF16)
        wout_s[...] = wout_ref[...].astype(BF16)

    half = D // 2
    b_rec = _dot(yrec_ref[...], wrec_s[...])
    b_att = _dot(oatt_ref[...], watt_s[...])
    m0 = _sigmoid(gr0_ref[...].astype(F32)) * b_rec[:, :half] + _sigmoid(ga0_ref[...].astype(F32)) * b_att[:, :half]
    m1 = _sigmoid(gr1_ref[...].astype(F32)) * b_rec[:, half:] + _sigmoid(ga1_ref[...].astype(F32)) * b_att[:, half:]
    merged = jnp.concatenate([m0, m1], axis=-1).astype(BF16)
    out = _dot(merged, wout_s[...])

    hs = []
    for s in range(MERGE_TM // SEG):
        rows = slice(s * SEG, (s + 1) * SEG)
        m = mod_ref[s]
        x1 = x_ref[rows, :] + m[2:3, :] * out[rows, :]
        x1_ref[rows, :] = x1
        h2 = _norm_mod(x1, g2_ref[...], m[3:4, :], m[4:5, :])
        h2_ref[rows, :] = h2
        hs.append(h2)
    h2 = jnp.concatenate(hs, axis=0)

    h_hi, h_lo = _split(h2)
    w_hi, w_lo = _split(wrt_ref[...])
    lt = _dot_nt(w_hi, h_hi) + _dot_nt(w_hi, h_lo) + _dot_nt(w_lo, h_hi)
    e1, e2, w1, w2 = _route(lt, rb_ref[...])
    idx_ref[...] = jnp.concatenate([e1, e2], axis=0)
    wts_ref[...] = jnp.concatenate([w1, w2], axis=0)


def _merge(yrec, oatt, proj, x, modseg, norm2_g, w_rec_out, w_att_out, w_out, wrt, rbias, l):
    tm = MERGE_TM
    half = D // 2
    gcol = (3 * D + 2 * N_KV * HD) // half
    wspec = pl.BlockSpec((None, D, D), lambda i: (l, 0, 0))
    return pl.pallas_call(
        _merge_kernel,
        grid=(T // tm,),
        in_specs=[
            pl.BlockSpec((tm, D), lambda i: (i, 0)),
            pl.BlockSpec((tm, D), lambda i: (i, 0)),
            pl.BlockSpec((tm, half), lambda i: (i, gcol)),
            pl.BlockSpec((tm, half), lambda i: (i, gcol + 1)),
            pl.BlockSpec((tm, half), lambda i: (i, gcol + 2)),
            pl.BlockSpec((tm, half), lambda i: (i, gcol + 3)),
            pl.BlockSpec((tm, D), lambda i: (i, 0)),
            pl.BlockSpec((None, tm // SEG, 8, D), lambda i: (l, i, 0, 0)),
            pl.BlockSpec((None, 1, D), lambda i: (l, 0, 0)),
            wspec, wspec, wspec,
            pl.BlockSpec((N_EXP, D), lambda i: (0, 0)),
            pl.BlockSpec((N_EXP, 1), lambda i: (0, 0)),
        ],
        out_specs=[
            pl.BlockSpec((tm, D), lambda i: (i, 0)),
            pl.BlockSpec((tm, D), lambda i: (i, 0)),
            pl.BlockSpec((2, tm), lambda i: (0, i)),
            pl.BlockSpec((2, tm), lambda i: (0, i)),
        ],
        out_shape=[
            jax.ShapeDtypeStruct((T, D), F32),
            jax.ShapeDtypeStruct((T, D), F32),
            jax.ShapeDtypeStruct((2, T), jnp.int32),
            jax.ShapeDtypeStruct((2, T), F32),
        ],
        scratch_shapes=[pltpu.VMEM((D, D), BF16)] * 3,
        compiler_params=_cp(("arbitrary",)),
        name="merge",
    )(yrec, oatt, proj, proj, proj, proj, x, modseg, norm2_g.reshape(DEPTH, 1, D),
      w_rec_out, w_att_out, w_out, wrt, rbias)


MOE_TM = 256
MOE_NT = 2 * T // MOE_TM + N_EXP
MOE_ROWS = MOE_NT * MOE_TM
META_TILE_E, META_CNT, META_OFF, META_END, META_NT = 0, 1, 2, 3, 4


def _pos_kernel(idx_ref, pos_ref, meta_ref):
    shift = MOE_TM.bit_length() - 1
    idx = idx_ref[...]
    eid = lax.broadcasted_iota(jnp.int32, (N_EXP, T), 0)
    m0 = eid == idx[0:1, :]
    m1 = eid == idx[1:2, :]
    member = jnp.where(m0 | m1, 1.0, 0.0)
    cnt = jnp.sum(member, axis=1, keepdims=True).astype(jnp.int32)
    ntile = jnp.right_shift(cnt + (MOE_TM - 1), shift)
    offs, acc = [], jnp.zeros((1, 1), jnp.int32)
    for e in range(N_EXP):
        offs.append(acc)
        acc = acc + ntile[e:e + 1, :]
    off_t = jnp.concatenate(offs, axis=0)
    end_t = off_t + ntile

    blk = 256
    r_i = lax.broadcasted_iota(jnp.int32, (blk, blk), 0)
    c_i = lax.broadcasted_iota(jnp.int32, (blk, blk), 1)
    upper = jnp.where(r_i <= c_i, 1.0, 0.0).astype(BF16)
    run = (off_t * MOE_TM).astype(F32)
    for j in range(T // blk):
        ls = slice(j * blk, (j + 1) * blk)
        mb = member[:, ls]
        inc = _dot(mb.astype(BF16), upper)
        dest = run + inc - mb
        pos_ref[0:1, ls] = jnp.sum(jnp.where(m0[:, ls], dest, 0.0), axis=0, keepdims=True).astype(jnp.int32)
        pos_ref[1:2, ls] = jnp.sum(jnp.where(m1[:, ls], dest, 0.0), axis=0, keepdims=True).astype(jnp.int32)
        run = run + inc[:, blk - 1:blk]

    lane = lax.broadcasted_iota(jnp.int32, (1, 128), 1)
    zero = jnp.zeros((1, 128), jnp.int32)
    tile_e, cnt_row, off_row, end_row = zero, zero, zero, zero
    for e in range(N_EXP):
        tile_e = tile_e + jnp.where(lane >= end_t[e:e + 1, :], 1, 0)
        here = lane == e
        cnt_row = jnp.where(here, cnt[e:e + 1, :], cnt_row)
        off_row = jnp.where(here, off_t[e:e + 1, :] * MOE_TM, off_row)
        end_row = jnp.where(here, end_t[e:e + 1, :] * MOE_TM, end_row)
    tile_e = jnp.minimum(tile_e, N_EXP - 1)
    nt_row = zero + acc
    meta_ref[...] = jnp.concatenate([tile_e, cnt_row, off_row, end_row, nt_row, zero, zero, zero], axis=0)


def _route_pos(idx):
    return pl.pallas_call(
        _pos_kernel,
        grid=(1,),
        in_specs=[pl.BlockSpec((2, T), lambda i: (0, 0))],
        out_specs=[pl.BlockSpec((2, T), lambda i: (0, 0)), pl.BlockSpec((8, 128), lambda i: (0, 0))],
        out_shape=[jax.ShapeDtypeStruct((2, T), jnp.int32), jax.ShapeDtypeStruct((8, 128), jnp.int32)],
        compiler_params=_cp(("arbitrary",)),
        name="route_pos",
    )(idx)


DISP_TM = T // N_EXP


def _dispatch_kernel(meta_ref, pos_ref, h_ref, z_hbm, xs_hbm, sem):
    i = pl.program_id(0)

    def row_copy(src, src_row, dst_row):
        return pltpu.make_async_copy(src.at[pl.ds(src_row, 1), :], xs_hbm.at[pl.ds(dst_row, 1), :], sem)

    def issue(r, c):
        row_copy(h_ref, r, pos_ref[0, r]).start(priority=0)
        row_copy(h_ref, r, pos_ref[1, r]).start(priority=1)
        return c

    lax.fori_loop(0, DISP_TM, issue, 0, unroll=8)

    pad0 = meta_ref[META_OFF, i] + meta_ref[META_CNT, i]
    npad = meta_ref[META_END, i] - pad0

    def zero_fill(p, c):
        row_copy(z_hbm, 0, p).start()
        return c

    lax.fori_loop(pad0, pad0 + npad, zero_fill, 0)

    for _ in range(2):
        pltpu.make_async_copy(h_ref, xs_hbm.at[pl.ds(0, DISP_TM), :], sem).wait()

    tail = meta_ref[META_NT, 0] + i

    def tail_copy():
        rows = pl.ds(pl.multiple_of(tail * MOE_TM, MOE_TM), MOE_TM)
        return pltpu.make_async_copy(z_hbm, xs_hbm.at[rows, :], sem)

    @pl.when(tail < MOE_NT)
    def _():
        tail_copy().start()

    def zero_wait(p, c):
        row_copy(z_hbm, 0, p).wait()
        return c

    lax.fori_loop(pad0, pad0 + npad, zero_wait, 0)

    @pl.when(tail < MOE_NT)
    def _():
        tail_copy().wait()


def _dispatch(meta, pos, h2, zrow):
    return pl.pallas_call(
        _dispatch_kernel,
        grid_spec=pltpu.PrefetchScalarGridSpec(
            num_scalar_prefetch=1,
            grid=(N_EXP,),
            in_specs=[
                pl.BlockSpec((2, DISP_TM), lambda i, meta: (0, i), memory_space=pltpu.SMEM),
                pl.BlockSpec((DISP_TM, D), lambda i, meta: (i, 0)),
                pl.BlockSpec((MOE_TM, D), lambda i, meta: (0, 0)),
            ],
            out_specs=pl.BlockSpec(memory_space=pl.ANY),
            scratch_shapes=[pltpu.SemaphoreType.DMA],
        ),
        out_shape=jax.ShapeDtypeStruct((MOE_ROWS, D), F32),
        compiler_params=_cp(("arbitrary",)),
        name="dispatch",
    )(meta, pos, h2, zrow)


def _experts_kernel(meta_ref, xs_ref, wg_ref, wu_ref, wd_ref, ys_ref, wg_s, wu_s, wd_s):
    j = pl.program_id(0)
    live = j < meta_ref[META_NT, 0]
    e = meta_ref[META_TILE_E, j]
    e_prev = meta_ref[META_TILE_E, jnp.maximum(j - 1, 0)]

    @pl.when(live & ((j == 0) | (e != e_prev)))
    def _():
        wg_s[...] = wg_ref[...].astype(BF16)
        wu_s[...] = wu_ref[...].astype(BF16)
        wd_s[...] = wd_ref[...].astype(BF16)

    @pl.when(live)
    def _():
        x = xs_ref[...].astype(BF16)
        g = _dot(x, wg_s[...])
        u = _dot(x, wu_s[...])
        act = (g * _sigmoid(g)) * u
        ys_ref[...] = _dot(act.astype(BF16), wd_s[...])

    @pl.when(jnp.logical_not(live))
    def _():
        ys_ref[...] = jnp.zeros_like(ys_ref)


def _experts(meta, xs, w_gate_e, w_up_e, w_down_e, l):
    def tile(j, meta):
        return jnp.minimum(j, meta[META_NT, 0] - 1)

    def wmap(j, meta):
        return (l, meta[META_TILE_E, tile(j, meta)], 0, 0)

    return pl.pallas_call(
        _experts_kernel,
        grid_spec=pltpu.PrefetchScalarGridSpec(
            num_scalar_prefetch=1,
            grid=(MOE_NT,),
            in_specs=[
                pl.BlockSpec((MOE_TM, D), lambda j, meta: (tile(j, meta), 0)),
                pl.BlockSpec((None, None, D, D_EXP), wmap),
                pl.BlockSpec((None, None, D, D_EXP), wmap),
                pl.BlockSpec((None, None, D_EXP, D), wmap),
            ],
            out_specs=pl.BlockSpec((MOE_TM, D), lambda j, meta: (j, 0)),
            scratch_shapes=[pltpu.VMEM((D, D_EXP), BF16), pltpu.VMEM((D, D_EXP), BF16),
                            pltpu.VMEM((D_EXP, D), BF16)],
        ),
        out_shape=jax.ShapeDtypeStruct((MOE_ROWS, D), F32),
        compiler_params=_cp(("arbitrary",)),
        name="experts",
    )(meta, xs, w_gate_e, w_up_e, w_down_e)


COMB_TM = SEG


def _combine_kernel(pos_ref, posn_ref, w_ref, x1_ref, mod_ref, fg_ref, ys_hbm, o_ref, buf, sem, *, final):
    i = pl.program_id(0)
    n = pl.num_programs(0)
    slot = lax.rem(i, 2)

    def issue(p_ref, s):
        def body(r, c):
            for k in range(2):
                pltpu.make_async_copy(ys_hbm.at[pl.ds(p_ref[k, r], 1), :],
                                      buf.at[s, k, pl.ds(r, 1), :], sem.at[s]).start(priority=k)
            return c
        lax.fori_loop(0, COMB_TM, body, 0, unroll=8)

    @pl.when(i == 0)
    def _():
        issue(pos_ref, 0)

    @pl.when(i + 1 < n)
    def _():
        issue(posn_ref, 1 - slot)

    for k in range(2):
        pltpu.make_async_copy(ys_hbm.at[pl.ds(0, COMB_TM), :], buf.at[slot, k], sem.at[slot]).wait()

    w = w_ref[...]
    y = w[:, 0:1] * buf[slot, 0] + w[:, 1:2] * buf[slot, 1]
    x = x1_ref[...] + mod_ref[5:6, :] * y
    if final:
        ms = jnp.mean(x * x, axis=-1, keepdims=True)
        x = x * lax.rsqrt(ms + EPS) * fg_ref[...]
    o_ref[...] = x


def _combine(pos, wts_t, x1, modseg, final_g, ys, l, final):
    n = T // COMB_TM
    return pl.pallas_call(
        functools.partial(_combine_kernel, final=final),
        grid=(n,),
        in_specs=[
            pl.BlockSpec((2, COMB_TM), lambda i: (0, i), memory_space=pltpu.SMEM),
            pl.BlockSpec((2, COMB_TM), lambda i: (0, jnp.minimum(i + 1, n - 1)), memory_space=pltpu.SMEM),
            pl.BlockSpec((COMB_TM, 2), lambda i: (i, 0)),
            pl.BlockSpec((COMB_TM, D), lambda i: (i, 0)),
            pl.BlockSpec((None, None, 8, D), lambda i: (l, i, 0, 0)),
            pl.BlockSpec((1, D), lambda i: (0, 0)),
            pl.BlockSpec(memory_space=pl.ANY),
        ],
        out_specs=pl.BlockSpec((COMB_TM, D), lambda i: (i, 0)),
        out_shape=jax.ShapeDtypeStruct((T, D), F32),
        scratch_shapes=[pltpu.VMEM((2, 2, COMB_TM, D), F32), pltpu.SemaphoreType.DMA((2,))],
        compiler_params=_cp(("arbitrary",)),
        name="combine",
    )(pos, pos, wts_t, x1, modseg, final_g.reshape(1, D), ys)


def _rope_tables():
    n = DEC_SEQ
    pos_row = np.repeat(np.arange(n // GRID_W, dtype=np.float32), GRID_W)
    pos_col = np.tile(np.arange(GRID_W, dtype=np.float32), n // GRID_W)
    half = HD // 2
    inv_freq = jnp.asarray(ROPE_THETA, F32) ** (-jnp.arange(0, half, 2, dtype=F32) / half)
    ang = jnp.concatenate([jnp.asarray(pos_row)[:, None] * inv_freq,
                           jnp.asarray(pos_col)[:, None] * inv_freq], axis=-1)
    cos, sin = jnp.cos(ang), jnp.sin(ang)
    cos128 = jnp.tile(cos, (1, 4))
    sin128 = jnp.tile(jnp.concatenate([-sin, sin], axis=-1), (1, 2))
    return cos128, sin128


def _head_mean_matrix():
    idx = np.arange(2 * HD)
    same = (idx[:, None] // HD) == (idx[None, :] // HD)
    return jnp.asarray(same.astype(np.float32) / HD, BF16)


_SEG_ROWS = np.array([0] * (T_CTX // SEG) + [1 + b for b in range(DEC_BATCH) for _ in range(DEC_SEQ // SEG)])


def kernel(x_prompt, x_sample, cache_k, cache_v, state_rec, c, c_ctx, w_mod, b_mod, norm1_g, norm2_g, w_in, conv_w, conv_b, rg_wa, rg_ba, rg_wx, rg_bx, rg_lambda, q_norm_g, k_norm_g, w_rec_out, w_att_out, w_out, w_router, router_bias, w_gate_e, w_up_e, w_down_e, final_g):
    x = jnp.concatenate([x_prompt.reshape(T_CTX, D), x_sample.reshape(T_LAT, D)], axis=0)

    cvecs = jnp.concatenate([c_ctx[None, :], c, jnp.zeros((3, D), F32)], axis=0)
    mods = _mods(cvecs, w_mod, b_mod).reshape(DEPTH, 8, 6, D)
    modseg = jnp.pad(mods[:, _SEG_ROWS], ((0, 0), (0, 0), (0, 2), (0, 0)))

    cos128, sin128 = _rope_tables()
    bd = _head_mean_matrix()
    qg128 = jnp.tile(q_norm_g, (1, 2)).reshape(DEPTH, 1, 2 * HD)
    kg128 = jnp.tile(k_norm_g, (1, 2)).reshape(DEPTH, 1, 2 * HD)
    wg = jnp.concatenate([rg_wa[:, 0], rg_wx[:, 0], rg_wa[:, 1], rg_wx[:, 1]], axis=-1)
    pvec = jnp.stack([rg_ba[:, 0], rg_bx[:, 0], rg_ba[:, 1], rg_bx[:, 1],
                      rg_lambda[:, 0], rg_lambda[:, 1], conv_b, jnp.zeros_like(conv_b)], axis=1)
    wrt = w_router.T
    rbias = router_bias.reshape(N_EXP, 1)
    zrow = jnp.zeros((MOE_TM, D), F32)

    new_k, new_v, new_s = [], [], []
    for l in range(DEPTH):
        proj = _inproj(x, modseg, norm1_g, w_in, l)
        h0 = jnp.concatenate([jnp.zeros((T_CTX // UNIT, 2, D), F32), state_rec[:, l]], axis=0)
        yrec, stf, stb = _rec(proj, conv_w, pvec, wg, h0, l)
        qc, kc, vc = _qkv(proj, qg128, kg128, cos128, sin128, bd, l, latent=False)
        ql, kl, vl = _qkv(proj, qg128, kg128, cos128, sin128, bd, l, latent=True)
        o_ctx = _attn_ctx(qc, kc, vc)
        o_lat = _attn_lat(ql, cache_k, cache_v, kl, vl, l)
        oatt = jnp.concatenate([o_ctx, o_lat], axis=0)
        x1, h2, idx, wts = _merge(yrec, oatt, proj, x, modseg, norm2_g,
                                  w_rec_out, w_att_out, w_out, wrt, rbias, l)
        pos, meta = _route_pos(idx)
        xs = _dispatch(meta, pos, h2, zrow)
        ys = _experts(meta, xs, w_gate_e, w_up_e, w_down_e, l)
        x = _combine(pos, wts.T, x1, modseg, final_g, ys, l, final=(l == DEPTH - 1))
        new_k.append(kc)
        new_v.append(vc)
        n_cu = T_CTX // UNIT
        spu = UNIT // SEQ
        hf_last = stf[:n_cu].reshape(n_cu, spu, 2, D)[:, :, 1].reshape(BATCH, D)
        hb_first = stb[:n_cu].reshape(n_cu, spu, 2, D)[:, :, 0].reshape(BATCH, D)
        new_s.append(jnp.stack([hf_last, hb_first], axis=1))

    y_prompt = x[:T_CTX].reshape(BATCH, SEQ, D)
    y_sample = x[T_CTX:].reshape(DEC_BATCH, DEC_SEQ, D)
    return (y_prompt, y_sample, jnp.stack(new_k, axis=1), jnp.stack(new_v, axis=1), jnp.stack(new_s, axis=1))
```

```python
import functools

import numpy as np
import jax
import jax.numpy as jnp
from jax import lax
from jax.experimental import pallas as pl
from jax.experimental.pallas import tpu as pltpu

F32 = jnp.float32
BF16 = jnp.bfloat16

D = 1024
BATCH = 16
SEQ = 256
DEPTH = 2
DEC_BATCH = 4
DEC_SEQ = 1024
PAST = 256
GRID_W = 64
N_HEADS = 16
N_KV = 4
HD = 64
N_RG_BLK = 8
RG_BLK = 128
RG_C = 8.0
N_EXP = 16
D_EXP = 512
ROPE_THETA = 10000.0
EPS = 1e-6
P_IN = 5632
TINY = float(np.finfo(np.float32).tiny)
NEG_LOG2E = -float(np.log2(np.e))

T_CTX = BATCH * SEQ
T_LAT = DEC_BATCH * DEC_SEQ
T = T_CTX + T_LAT
SEG = 256
N_SEG = T // SEG
UNIT = 1024
N_UNIT = T // UNIT
CHUNK = UNIT // 8
CSTRIDE = CHUNK + 8

VMEM_LIMIT = 56 * 1024 * 1024


def _cp(sem):
    return pltpu.CompilerParams(dimension_semantics=sem, vmem_limit_bytes=VMEM_LIMIT)


def _split(x):
    hi = x.astype(BF16)
    lo = (x - hi.astype(F32)).astype(BF16)
    return hi, lo


def _sigmoid(x):
    return 0.5 * jnp.tanh(0.5 * x) + 0.5


def _dot(a, b):
    return jnp.dot(a, b, preferred_element_type=F32)


def _dot_nt(a, b):
    return lax.dot_general(a, b, (((1,), (1,)), ((), ())), preferred_element_type=F32)


def _mods_kernel(c_ref, w_ref, b_ref, o_ref):
    c = c_ref[...]
    s = c * jax.nn.sigmoid(c)
    s_hi, s_lo = _split(s)
    w_hi, w_lo = _split(w_ref[...])
    o_ref[...] = _dot(s_hi, w_hi) + _dot(s_hi, w_lo) + _dot(s_lo, w_hi) + b_ref[...]


def _mods(cvecs, w_mod, b_mod):
    tn = 1536
    return pl.pallas_call(
        _mods_kernel,
        grid=(DEPTH, 6 * D // tn),
        in_specs=[
            pl.BlockSpec((8, D), lambda l, j: (0, 0)),
            pl.BlockSpec((None, D, tn), lambda l, j: (l, 0, j)),
            pl.BlockSpec((None, 1, tn), lambda l, j: (l, 0, j)),
        ],
        out_specs=pl.BlockSpec((None, 8, tn), lambda l, j: (l, 0, j)),
        out_shape=jax.ShapeDtypeStruct((DEPTH, 8, 6 * D), F32),
        compiler_params=_cp(("arbitrary", "arbitrary")),
        name="mods",
    )(cvecs, w_mod, b_mod.reshape(DEPTH, 1, 6 * D))


def _norm_mod(x, g, shift, scale):
    ms = jnp.mean(x * x, axis=-1, keepdims=True)
    return x * lax.rsqrt(ms + EPS) * g * (1.0 + scale) + shift


def _two_part_specs(tm, n_ctx):
    return [pl.BlockSpec((tm, D), lambda i, *_: (jnp.minimum(i, n_ctx - 1), 0)),
            pl.BlockSpec((tm, D), lambda i, *_: (jnp.maximum(i - n_ctx, 0), 0))]


def _inproj_kernel(xa_ref, xb_ref, mod_ref, g_ref, w_ref, o_ref, h_ref, *, tm):
    def prologue(x_ref):
        def seg(s, carry):
            r0 = pl.multiple_of(s * SEG, SEG)
            m = mod_ref[s]
            h = _norm_mod(x_ref[pl.ds(r0, SEG), :], g_ref[...], m[0:1, :], m[1:2, :])
            h_ref[pl.ds(r0, SEG), :] = h.astype(BF16)
            return carry
        lax.fori_loop(0, tm // SEG, seg, 0)

    first = pl.program_id(1) == 0
    is_ctx = pl.program_id(0) < T_CTX // tm

    @pl.when(first & is_ctx)
    def _():
        prologue(xa_ref)

    @pl.when(first & jnp.logical_not(is_ctx))
    def _():
        prologue(xb_ref)

    o_ref[...] = _dot(h_ref[...], w_ref[...].astype(BF16)).astype(BF16)


def _inproj(xa, xb, modseg, norm_g, w_in, l):
    tm, tn = 2048, 512
    return pl.pallas_call(
        functools.partial(_inproj_kernel, tm=tm),
        grid=(T // tm, P_IN // tn),
        in_specs=_two_part_specs(tm, T_CTX // tm) + [
            pl.BlockSpec((None, tm // SEG, 8, D), lambda i, j: (l, i, 0, 0)),
            pl.BlockSpec((None, 1, D), lambda i, j: (l, 0, 0)),
            pl.BlockSpec((None, D, tn), lambda i, j: (l, 0, j)),
        ],
        out_specs=pl.BlockSpec((tm, tn), lambda i, j: (i, j)),
        out_shape=jax.ShapeDtypeStruct((T, P_IN), BF16),
        scratch_shapes=[pltpu.VMEM((tm, D), BF16)],
        compiler_params=_cp(("arbitrary", "arbitrary")),
        name="inproj",
    )(xa, xb, modseg, norm_g.reshape(DEPTH, 1, D), w_in)


REC_CW = 512
HALO = 16


def _rec_kernel(xr_ref, gate_ref, cw_ref, pv_ref, wg_ref, h0_ref,
                y_ref, stf_ref, stb_ref,
                af_ref, bf_ref, ab_ref, bb_ref, wgh_ref):
    u = pl.program_id(0)
    is_ctx = u < (T_CTX // UNIT)
    seq_len = jnp.where(is_ctx, SEQ, DEC_SEQ)
    nblk = REC_CW // RG_BLK

    pv = pv_ref[...]
    cwts = cw_ref[...]
    conv_b = pv[6:7, :]

    def softplus_neg(lam):
        z = -lam
        return jnp.maximum(z, 0.0) + jnp.log1p(jnp.exp(-jnp.abs(z)))

    c4s = tuple((0.5 * RG_C) * softplus_neg(pv[4 + d:5 + d, :]) for d in range(2))
    pv_h = 0.5 * pv
    for n in range(nblk):
        wgh_ref[n] = (0.5 * wg_ref[n]).astype(BF16)
    a_refs = (af_ref, ab_ref)
    b_refs = (bf_ref, bb_ref)

    def gates(ci, carry):
        base = pl.multiple_of(ci * CHUNK, CHUNK)
        lo = pl.multiple_of(jnp.maximum(base - HALO, 0), HALO)
        hi = pl.multiple_of(jnp.minimum(base + CHUNK, UNIT - HALO), HALO)
        main = xr_ref[pl.ds(base, CHUNK), :].astype(F32)
        win = jnp.concatenate([xr_ref[pl.ds(lo, HALO), :].astype(F32), main,
                               xr_ref[pl.ds(hi, HALO), :].astype(F32)], axis=0)
        n_win = CHUNK + 2 * HALO
        row8 = lax.broadcasted_iota(jnp.int32, (8, 1), 0)
        tl_head = jnp.bitwise_and(base + row8, seq_len - 1)
        tl_tail = jnp.bitwise_and(base + (CHUNK - 8) + row8, seq_len - 1)

        def tap(shift, head_ok=None, tail_ok=None):
            x = pltpu.roll(win, shift, 0)[HALO:HALO + CHUNK]
            if head_ok is not None:
                return jnp.concatenate([jnp.where(head_ok, x[:8], 0.0), x[8:]], axis=0)
            return jnp.concatenate([x[:CHUNK - 8], jnp.where(tail_ok, x[CHUNK - 8:], 0.0)], axis=0)

        xm2 = tap(2, head_ok=tl_head >= 2)
        xm1 = tap(1, head_ok=tl_head >= 1)
        xp1 = tap(n_win - 1, tail_ok=tl_tail <= seq_len - 2)
        xc = conv_b + xm2 * cwts[0:1, :]
        xc = xc + xm1 * cwts[1:2, :]
        xc = xc + main * cwts[2:3, :]
        xc = xc + xp1 * cwts[3:4, :]
        for n in range(nblk):
            ls = slice(n * RG_BLK, (n + 1) * RG_BLK)
            xn = xc[:, ls]
            hx = 0.5 * xn
            pre_h = _dot(xn.astype(BF16), wgh_ref[n])
            for d in range(2):
                th_r = jnp.tanh(pre_h[:, (2 * d) * RG_BLK:(2 * d + 1) * RG_BLK] + pv_h[2 * d:2 * d + 1, ls])
                th_i = jnp.tanh(pre_h[:, (2 * d + 1) * RG_BLK:(2 * d + 2) * RG_BLK] + pv_h[2 * d + 1:2 * d + 2, ls])
                c4 = c4s[d][:, ls]
                nla = c4 * th_r + c4
                a = jnp.exp2(nla * NEG_LOG2E)
                s = jnp.tanh(nla) * (a * a + 1.0)
                inp = (s * lax.rsqrt(jnp.maximum(s, TINY))) * (hx * th_i + hx)
                sbase = pl.multiple_of(ci * CSTRIDE, 8)
                a_refs[d][n, pl.ds(sbase, CHUNK), :] = a
                b_refs[d][n, pl.ds(sbase, CHUNK), :] = inp
        return carry

    lax.fori_loop(0, 8, gates, 0)

    zeros = [jnp.zeros((8, RG_BLK), F32)] * nblk
    ones = [jnp.ones((8, RG_BLK), F32)] * nblk
    hf, pf, hb, pb = list(zeros), list(ones), list(zeros), list(ones)
    for r in range(CHUNK):
        rows_f = pl.ds(r, 8, stride=CSTRIDE)
        rows_b = pl.ds(CHUNK - 1 - r, 8, stride=CSTRIDE)
        for n in range(nblk):
            a = af_ref[n, rows_f, :]
            hf[n] = a * hf[n] + bf_ref[n, rows_f, :]
            pf[n] = a * pf[n]
            bf_ref[n, rows_f, :] = hf[n]
            af_ref[n, rows_f, :] = pf[n]
            a = ab_ref[n, rows_b, :]
            hb[n] = a * hb[n] + bb_ref[n, rows_b, :]
            pb[n] = a * pb[n]
            bb_ref[n, rows_b, :] = hb[n]
            ab_ref[n, rows_b, :] = pb[n]
    hf, pf, hb, pb = (jnp.concatenate(v, axis=-1) for v in (hf, pf, hb, pb))

    cps = jnp.where(is_ctx, SEQ // CHUNK, DEC_SEQ // CHUNK)
    h0f = h0_ref[0:1, :]
    h0b = h0_ref[1:2, :]
    cf = [h0f]
    for c in range(1, 8):
        chain = hf[c - 1:c, :] + pf[c - 1:c, :] * cf[c - 1]
        cf.append(jnp.where(jnp.bitwise_and(c, cps - 1) == 0, h0f, chain))
    cb = [None] * 8
    cb[7] = h0b
    for c in range(6, -1, -1):
        chain = hb[c + 1:c + 2, :] + pb[c + 1:c + 2, :] * cb[c + 1]
        cb[c] = jnp.where(jnp.bitwise_and(c, cps - 1) == cps - 1, h0b, chain)
    carry_f = jnp.concatenate(cf, axis=0)
    carry_b = jnp.concatenate(cb, axis=0)
    stf_ref[...] = hf + pf * carry_f
    stb_ref[...] = hb + pb * carry_b

    for ci in range(8):
        rows = pl.ds(ci * CHUNK, CHUNK)
        srows = pl.ds(ci * CSTRIDE, CHUNK)
        for n in range(nblk):
            ls = slice(n * RG_BLK, (n + 1) * RG_BLK)
            h_f = bf_ref[n, srows, :] + af_ref[n, srows, :] * carry_f[ci:ci + 1, ls]
            h_b = bb_ref[n, srows, :] + ab_ref[n, srows, :] * carry_b[ci:ci + 1, ls]
            g = gate_ref[rows, ls].astype(F32)
            y_ref[rows, ls] = ((h_f + h_b) * jax.nn.gelu(g, approximate=True)).astype(BF16)


def _rec(proj, conv_w, pvec, wg, h0, l):
    ncb = D // REC_CW
    return pl.pallas_call(
        _rec_kernel,
        grid=(N_UNIT, ncb),
        in_specs=[
            pl.BlockSpec((UNIT, REC_CW), lambda u, c: (u, c)),
            pl.BlockSpec((UNIT, REC_CW), lambda u, c: (u, ncb + c)),
            pl.BlockSpec((None, 4, REC_CW), lambda u, c: (l, 0, c)),
            pl.BlockSpec((None, 8, REC_CW), lambda u, c: (l, 0, c)),
            pl.BlockSpec((None, REC_CW // RG_BLK, RG_BLK, 4 * RG_BLK), lambda u, c: (l, c, 0, 0)),
            pl.BlockSpec((None, 2, REC_CW), lambda u, c: (u, 0, c)),
        ],
        out_specs=[
            pl.BlockSpec((UNIT, REC_CW), lambda u, c: (u, c)),
            pl.BlockSpec((None, 8, REC_CW), lambda u, c: (u, 0, c)),
            pl.BlockSpec((None, 8, REC_CW), lambda u, c: (u, 0, c)),
        ],
        out_shape=[
            jax.ShapeDtypeStruct((T, D), BF16),
            jax.ShapeDtypeStruct((N_UNIT, 8, D), F32),
            jax.ShapeDtypeStruct((N_UNIT, 8, D), F32),
        ],
        scratch_shapes=[pltpu.VMEM((REC_CW // RG_BLK, 8 * CSTRIDE, RG_BLK), F32)] * 4
        + [pltpu.VMEM((REC_CW // RG_BLK, RG_BLK, 4 * RG_BLK), BF16)],
        compiler_params=_cp(("arbitrary", "arbitrary")),
        name="rec",
    )(proj, proj, conv_w, pvec, wg, h0)


def _head_norm(x, g128, bd):
    hi, lo = _split(x * x)
    ms = _dot(hi, bd) + _dot(lo, bd)
    return x * lax.rsqrt(ms + EPS) * g128


def _rope(x, cos, sin_signed):
    lane = lax.broadcasted_iota(jnp.int32, x.shape, 1)
    first_half = jnp.bitwise_and(lane, HD - 1) < HD // 2
    partner = jnp.where(first_half, pltpu.roll(x, 2 * HD - HD // 2, 1), pltpu.roll(x, HD // 2, 1))
    return x * cos + partner * sin_signed


def _qkv_kernel(q_ref, k_ref, v_ref, qg_ref, kg_ref, cos_ref, sin_ref, bd_ref,
                qo_ref, ko_ref, vo_ref, *, rope):
    bd = bd_ref[...]
    scale = HD ** -0.5 * float(np.log2(np.e))
    for j in range(N_HEADS // 2):
        x = _head_norm(q_ref[:, 2 * HD * j:2 * HD * (j + 1)].astype(F32), qg_ref[...], bd)
        if rope:
            x = _rope(x, cos_ref[...], sin_ref[...])
        x = x * scale
        qo_ref[2 * j] = x[:, :HD].astype(qo_ref.dtype)
        qo_ref[2 * j + 1] = x[:, HD:].astype(qo_ref.dtype)
    for j in range(N_KV // 2):
        x = _head_norm(k_ref[:, 2 * HD * j:2 * HD * (j + 1)].astype(F32), kg_ref[...], bd)
        if rope:
            x = _rope(x, cos_ref[...], sin_ref[...])
        ko_ref[2 * j] = x[:, :HD].astype(ko_ref.dtype)
        ko_ref[2 * j + 1] = x[:, HD:].astype(ko_ref.dtype)
        v = v_ref[:, 2 * HD * j:2 * HD * (j + 1)].astype(F32)
        vo_ref[2 * j] = v[:, :HD].astype(vo_ref.dtype)
        vo_ref[2 * j + 1] = v[:, HD:].astype(vo_ref.dtype)


def _qkv(proj, qg128, kg128, cos128, sin128, bd, l, latent):
    tm = SEG
    n = T_LAT // tm if latent else T_CTX // tm
    roff = T_CTX // tm if latent else 0
    per_seq = DEC_SEQ // tm
    if latent:
        kv_shape = (DEC_BATCH, N_KV, DEC_SEQ, HD)
        kv_spec = pl.BlockSpec((None, N_KV, tm, HD), lambda i: (i // per_seq, 0, i % per_seq, 0))
        kv_dtype = BF16
        tab_map = lambda i: (i % per_seq, 0)
    else:
        kv_shape = (BATCH, N_KV, SEQ, HD)
        kv_spec = pl.BlockSpec((None, N_KV, tm, HD), lambda i: (i, 0, 0, 0))
        kv_dtype = F32
        tab_map = lambda i: (0, 0)
    return pl.pallas_call(
        functools.partial(_qkv_kernel, rope=latent),
        grid=(n,),
        in_specs=[
            pl.BlockSpec((tm, D), lambda i: (roff + i, 2)),
            pl.BlockSpec((tm, N_KV * HD), lambda i: (roff + i, 3 * D // (N_KV * HD))),
            pl.BlockSpec((tm, N_KV * HD), lambda i: (roff + i, 3 * D // (N_KV * HD) + 1)),
            pl.BlockSpec((None, 1, 2 * HD), lambda i: (l, 0, 0)),
            pl.BlockSpec((None, 1, 2 * HD), lambda i: (l, 0, 0)),
            pl.BlockSpec((tm, 2 * HD), tab_map),
            pl.BlockSpec((tm, 2 * HD), tab_map),
            pl.BlockSpec((2 * HD, 2 * HD), lambda i: (0, 0)),
        ],
        out_specs=[
            pl.BlockSpec((N_HEADS, tm, HD), lambda i: (0, i, 0)),
            kv_spec,
            kv_spec,
        ],
        out_shape=[
            jax.ShapeDtypeStruct((N_HEADS, n * tm, HD), BF16),
            jax.ShapeDtypeStruct(kv_shape, kv_dtype),
            jax.ShapeDtypeStruct(kv_shape, kv_dtype),
        ],
        compiler_params=_cp(("arbitrary",)),
        name="qkv_lat" if latent else "qkv_ctx",
    )(proj, proj, proj, qg128, kg128, cos128, sin128, bd)


def _with_ones(v):
    return jnp.concatenate([v, jnp.ones_like(v)], axis=-1)


def _softmax_pv(q, k, v_ext):
    s = _dot_nt(q, k)
    m = jnp.max(s, axis=-1, keepdims=True)
    p = jnp.exp2(s - m).astype(BF16)
    r = _dot(p, v_ext)
    return r[:, :HD] / r[:, HD:HD + 1]


def _attend_heads(q_ref, k, v_ext):
    return jnp.concatenate([_softmax_pv(q_ref[h], k, v_ext) for h in range(N_HEADS // N_KV)], axis=-1)


def _attn_ctx_kernel(q_ref, k_ref, v_ref, o_ref):
    o = _attend_heads(q_ref, k_ref[...].astype(BF16), _with_ones(v_ref[...].astype(BF16)))
    o_ref[...] = o.astype(BF16)


def _attn_ctx(qh, kc, vc):
    g = N_HEADS // N_KV
    return pl.pallas_call(
        _attn_ctx_kernel,
        grid=(BATCH, N_KV),
        in_specs=[
            pl.BlockSpec((g, SEQ, HD), lambda b, h: (h, b, 0)),
            pl.BlockSpec((None, None, SEQ, HD), lambda b, h: (b, h, 0, 0)),
            pl.BlockSpec((None, None, SEQ, HD), lambda b, h: (b, h, 0, 0)),
        ],
        out_specs=pl.BlockSpec((SEQ, g * HD), lambda b, h: (b, h)),
        out_shape=jax.ShapeDtypeStruct((T_CTX, D), BF16),
        compiler_params=_cp(("arbitrary", "arbitrary")),
        name="attn_ctx",
    )(qh, kc, vc)


ATT_TQ = 256


def _attn_lat_kernel(q_ref, pk_ref, pv_ref, k_ref, v_ref, o_ref, k_s, v_s):
    k_s[0:PAST, :] = pk_ref[...].astype(BF16)
    k_s[PAST:, :] = k_ref[...]
    v_s[0:PAST, :] = _with_ones(pv_ref[...].astype(BF16))
    v_s[PAST:, :] = _with_ones(v_ref[...])

    def q_tile(qi, carry):
        rows = pl.ds(pl.multiple_of(qi * ATT_TQ, ATT_TQ), ATT_TQ)
        o = jnp.concatenate([_softmax_pv(q_ref[h, rows, :], k_s[...], v_s[...])
                             for h in range(N_HEADS // N_KV)], axis=-1)
        o_ref[rows, :] = o.astype(BF16)
        return carry

    lax.fori_loop(0, DEC_SEQ // ATT_TQ, q_tile, 0)


def _attn_lat(qh, cache_k, cache_v, kr, vr, l):
    g = N_HEADS // N_KV
    return pl.pallas_call(
        _attn_lat_kernel,
        grid=(DEC_BATCH, N_KV),
        in_specs=[
            pl.BlockSpec((g, DEC_SEQ, HD), lambda b, h: (h, b, 0)),
            pl.BlockSpec((None, None, None, PAST, HD), lambda b, h: (b, l, h, 0, 0)),
            pl.BlockSpec((None, None, None, PAST, HD), lambda b, h: (b, l, h, 0, 0)),
            pl.BlockSpec((None, None, DEC_SEQ, HD), lambda b, h: (b, h, 0, 0)),
            pl.BlockSpec((None, None, DEC_SEQ, HD), lambda b, h: (b, h, 0, 0)),
        ],
        out_specs=pl.BlockSpec((DEC_SEQ, g * HD), lambda b, h: (b, h)),
        out_shape=jax.ShapeDtypeStruct((T_LAT, D), BF16),
        scratch_shapes=[pltpu.VMEM((PAST + DEC_SEQ, HD), BF16), pltpu.VMEM((PAST + DEC_SEQ, 2 * HD), BF16)],
        compiler_params=_cp(("arbitrary", "arbitrary")),
        name="attn_lat",
    )(qh, cache_k, cache_v, kr, vr)


MERGE_TM = 512


def _route(lt, bias):
    rows = [lt[e:e + 1, :] for e in range(N_EXP)]
    m = rows[0]
    for e in range(1, N_EXP):
        m = jnp.maximum(m, rows[e])
    ex = [jnp.exp(r - m) for r in rows]
    z = ex[0]
    for e in range(1, N_EXP):
        z = z + ex[e]
    probs = [x / z for x in ex]
    sel = [probs[e] + bias[e:e + 1, :] for e in range(N_EXP)]

    def top2_sum(v):
        a, b = jnp.maximum(v[0], v[1]), jnp.minimum(v[0], v[1])
        c, d = jnp.maximum(v[2], v[3]), jnp.minimum(v[2], v[3])
        return jnp.maximum(a, c) + jnp.maximum(jnp.minimum(a, c), jnp.maximum(b, d))

    scores = [top2_sum(sel[4 * g:4 * g + 4]) for g in range(4)]
    best = jnp.zeros_like(scores[0], dtype=jnp.int32)
    best_s = scores[0]
    for g in range(1, 4):
        take = scores[g] > best_s
        best = jnp.where(take, g, best)
        best_s = jnp.where(take, scores[g], best_s)
    cs, cp = [], []
    for j in range(4):
        s_j, p_j = sel[j], probs[j]
        for g in range(1, 4):
            s_j = jnp.where(best == g, sel[4 * g + j], s_j)
            p_j = jnp.where(best == g, probs[4 * g + j], p_j)
        cs.append(s_j)
        cp.append(p_j)
    neg = jnp.full_like(cs[0], -jnp.inf)

    def argmax4(v):
        bi = jnp.zeros_like(best)
        bv = v[0]
        for j in range(1, 4):
            take = v[j] > bv
            bi = jnp.where(take, j, bi)
            bv = jnp.where(take, v[j], bv)
        return bi

    def pick(v, idx):
        out = v[0]
        for j in range(1, 4):
            out = jnp.where(idx == j, v[j], out)
        return out

    i1 = argmax4(cs)
    cs2 = [jnp.where(i1 == j, neg, cs[j]) for j in range(4)]
    i2 = argmax4(cs2)
    i2 = jnp.where((i2 == 0) & (i1 == 0), 1, i2)
    w1, w2 = pick(cp, i1), pick(cp, i2)
    den = w1 + w2
    return best * 4 + i1, best * 4 + i2, w1 / den, w2 / den


def _merge_kernel(yrec_ref, oa_ref, ob_ref, gr0_ref, gr1_ref, ga0_ref, ga1_ref, xa_ref, xb_ref, mod_ref, g2_ref,
                  wrec_ref, watt_ref, wout_ref, wrt_ref, rb_ref,
                  x1_ref, h2_ref, idx_ref, wts_ref,
                  wrec_s, watt_s, wout_s):
    @pl.when(pl.program_id(0) == 0)
    def _():
        wrec_s[...] = wrec_ref[...].astype(BF16)
        watt_s[...] = watt_ref[...].astype(BF16)
        wout_s[...] = wout_ref[...].astype(BF16)

    is_ctx = pl.program_id(0) < T_CTX // MERGE_TM
    args = (yrec_ref, gr0_ref, gr1_ref, ga0_ref, ga1_ref, mod_ref, g2_ref, wrt_ref, rb_ref,
            x1_ref, h2_ref, idx_ref, wts_ref, wrec_s, watt_s, wout_s)

    @pl.when(is_ctx)
    def _():
        _merge_body(oa_ref, xa_ref, *args)

    @pl.when(jnp.logical_not(is_ctx))
    def _():
        _merge_body(ob_ref, xb_ref, *args)


def _merge_body(oatt_ref, x_ref, yrec_ref, gr0_ref, gr1_ref, ga0_ref, ga1_ref, mod_ref, g2_ref, wrt_ref, rb_ref,
                x1_ref, h2_ref, idx_ref, wts_ref, wrec_s, watt_s, wout_s):
    half = D // 2
    b_rec = _dot(yrec_ref[...], wrec_s[...])
    b_att = _dot(oatt_ref[...], watt_s[...])
    m0 = _sigmoid(gr0_ref[...].astype(F32)) * b_rec[:, :half] + _sigmoid(ga0_ref[...].astype(F32)) * b_att[:, :half]
    m1 = _sigmoid(gr1_ref[...].astype(F32)) * b_rec[:, half:] + _sigmoid(ga1_ref[...].astype(F32)) * b_att[:, half:]
    merged = jnp.concatenate([m0, m1], axis=-1).astype(BF16)
    out = _dot(merged, wout_s[...])

    hs = []
    for s in range(MERGE_TM // SEG):
        rows = slice(s * SEG, (s + 1) * SEG)
        m = mod_ref[s]
        x1 = x_ref[rows, :] + m[2:3, :] * out[rows, :]
        x1_ref[rows, :] = x1
        h2 = _norm_mod(x1, g2_ref[...], m[3:4, :], m[4:5, :])
        h2_ref[rows, :] = h2
        hs.append(h2)
    h2 = jnp.concatenate(hs, axis=0)

    h_hi, h_lo = _split(h2)
    w_hi, w_lo = _split(wrt_ref[...])
    lt = _dot_nt(w_hi, h_hi) + _dot_nt(w_hi, h_lo) + _dot_nt(w_lo, h_hi)
    e1, e2, w1, w2 = _route(lt, rb_ref[...])
    idx_ref[...] = jnp.concatenate([e1, e2], axis=0)
    wts_ref[...] = jnp.concatenate([w1, w2], axis=0)


def _merge(yrec, o_ctx, o_lat, proj, xa, xb, modseg, norm2_g, w_rec_out, w_att_out, w_out, wrt, rbias, l):
    tm = MERGE_TM
    half = D // 2
    gcol = (3 * D + 2 * N_KV * HD) // half
    wspec = pl.BlockSpec((None, D, D), lambda i: (l, 0, 0))
    return pl.pallas_call(
        _merge_kernel,
        grid=(T // tm,),
        in_specs=[pl.BlockSpec((tm, D), lambda i: (i, 0))] + _two_part_specs(tm, T_CTX // tm) + [
            pl.BlockSpec((tm, half), lambda i: (i, gcol)),
            pl.BlockSpec((tm, half), lambda i: (i, gcol + 1)),
            pl.BlockSpec((tm, half), lambda i: (i, gcol + 2)),
            pl.BlockSpec((tm, half), lambda i: (i, gcol + 3)),
        ] + _two_part_specs(tm, T_CTX // tm) + [
            pl.BlockSpec((None, tm // SEG, 8, D), lambda i: (l, i, 0, 0)),
            pl.BlockSpec((None, 1, D), lambda i: (l, 0, 0)),
            wspec, wspec, wspec,
            pl.BlockSpec((N_EXP, D), lambda i: (0, 0)),
            pl.BlockSpec((N_EXP, 1), lambda i: (0, 0)),
        ],
        out_specs=[
            pl.BlockSpec((tm, D), lambda i: (i, 0)),
            pl.BlockSpec((tm, D), lambda i: (i, 0)),
            pl.BlockSpec((2, tm), lambda i: (0, i)),
            pl.BlockSpec((2, tm), lambda i: (0, i)),
        ],
        out_shape=[
            jax.ShapeDtypeStruct((T, D), F32),
            jax.ShapeDtypeStruct((T, D), F32),
            jax.ShapeDtypeStruct((2, T), jnp.int32),
            jax.ShapeDtypeStruct((2, T), F32),
        ],
        scratch_shapes=[pltpu.VMEM((D, D), BF16)] * 3,
        compiler_params=_cp(("arbitrary",)),
        name="merge",
    )(yrec, o_ctx, o_lat, proj, proj, proj, proj, xa, xb, modseg, norm2_g.reshape(DEPTH, 1, D),
      w_rec_out, w_att_out, w_out, wrt, rbias)


MOE_TM = 256
MOE_NT = 2 * T // MOE_TM + N_EXP
MOE_ROWS = MOE_NT * MOE_TM
META_TILE_E, META_CNT, META_OFF, META_END, META_NT = 0, 1, 2, 3, 4


def _pos_kernel(idx_ref, pos_ref, meta_ref):
    shift = MOE_TM.bit_length() - 1
    idx = idx_ref[...]
    eid = lax.broadcasted_iota(jnp.int32, (N_EXP, T), 0)
    m0 = eid == idx[0:1, :]
    m1 = eid == idx[1:2, :]
    member = jnp.where(m0 | m1, 1.0, 0.0)
    cnt = jnp.sum(member, axis=1, keepdims=True).astype(jnp.int32)
    ntile = jnp.right_shift(cnt + (MOE_TM - 1), shift)
    offs, acc = [], jnp.zeros((1, 1), jnp.int32)
    for e in range(N_EXP):
        offs.append(acc)
        acc = acc + ntile[e:e + 1, :]
    off_t = jnp.concatenate(offs, axis=0)
    end_t = off_t + ntile

    blk = 256
    r_i = lax.broadcasted_iota(jnp.int32, (blk, blk), 0)
    c_i = lax.broadcasted_iota(jnp.int32, (blk, blk), 1)
    upper = jnp.where(r_i <= c_i, 1.0, 0.0).astype(BF16)
    run = (off_t * MOE_TM).astype(F32)
    for j in range(T // blk):
        ls = slice(j * blk, (j + 1) * blk)
        mb = member[:, ls]
        inc = _dot(mb.astype(BF16), upper)
        dest = run + inc - mb
        pos_ref[0:1, ls] = jnp.sum(jnp.where(m0[:, ls], dest, 0.0), axis=0, keepdims=True).astype(jnp.int32)
        pos_ref[1:2, ls] = jnp.sum(jnp.where(m1[:, ls], dest, 0.0), axis=0, keepdims=True).astype(jnp.int32)
        run = run + inc[:, blk - 1:blk]

    lane = lax.broadcasted_iota(jnp.int32, (1, 128), 1)
    zero = jnp.zeros((1, 128), jnp.int32)
    tile_e, cnt_row, off_row, end_row = zero, zero, zero, zero
    for e in range(N_EXP):
        tile_e = tile_e + jnp.where(lane >= end_t[e:e + 1, :], 1, 0)
        here = lane == e
        cnt_row = jnp.where(here, cnt[e:e + 1, :], cnt_row)
        off_row = jnp.where(here, off_t[e:e + 1, :] * MOE_TM, off_row)
        end_row = jnp.where(here, end_t[e:e + 1, :] * MOE_TM, end_row)
    tile_e = jnp.minimum(tile_e, N_EXP - 1)
    nt_row = zero + acc
    meta_ref[...] = jnp.concatenate([tile_e, cnt_row, off_row, end_row, nt_row, zero, zero, zero], axis=0)


def _route_pos(idx):
    return pl.pallas_call(
        _pos_kernel,
        grid=(1,),
        in_specs=[pl.BlockSpec((2, T), lambda i: (0, 0))],
        out_specs=[pl.BlockSpec((2, T), lambda i: (0, 0)), pl.BlockSpec((8, 128), lambda i: (0, 0))],
        out_shape=[jax.ShapeDtypeStruct((2, T), jnp.int32), jax.ShapeDtypeStruct((8, 128), jnp.int32)],
        compiler_params=_cp(("arbitrary",)),
        name="route_pos",
    )(idx)


DISP_TM = T // N_EXP


def _dispatch_kernel(meta_ref, pos_ref, h_ref, z_hbm, xs_hbm, sem):
    i = pl.program_id(0)

    def row_copy(src, src_row, dst_row):
        return pltpu.make_async_copy(src.at[pl.ds(src_row, 1), :], xs_hbm.at[pl.ds(dst_row, 1), :], sem)

    def issue(r, c):
        row_copy(h_ref, r, pos_ref[0, r]).start(priority=0)
        row_copy(h_ref, r, pos_ref[1, r]).start(priority=1)
        return c

    lax.fori_loop(0, DISP_TM, issue, 0, unroll=8)

    pad0 = meta_ref[META_OFF, i] + meta_ref[META_CNT, i]
    npad = meta_ref[META_END, i] - pad0

    def zero_fill(p, c):
        row_copy(z_hbm, 0, p).start()
        return c

    lax.fori_loop(pad0, pad0 + npad, zero_fill, 0)

    for _ in range(2):
        pltpu.make_async_copy(h_ref, xs_hbm.at[pl.ds(0, DISP_TM), :], sem).wait()

    tail = meta_ref[META_NT, 0] + i

    def tail_copy():
        rows = pl.ds(pl.multiple_of(tail * MOE_TM, MOE_TM), MOE_TM)
        return pltpu.make_async_copy(z_hbm, xs_hbm.at[rows, :], sem)

    @pl.when(tail < MOE_NT)
    def _():
        tail_copy().start()

    def zero_wait(p, c):
        row_copy(z_hbm, 0, p).wait()
        return c

    lax.fori_loop(pad0, pad0 + npad, zero_wait, 0)

    @pl.when(tail < MOE_NT)
    def _():
        tail_copy().wait()


def _dispatch(meta, pos, h2, zrow):
    return pl.pallas_call(
        _dispatch_kernel,
        grid_spec=pltpu.PrefetchScalarGridSpec(
            num_scalar_prefetch=1,
            grid=(N_EXP,),
            in_specs=[
                pl.BlockSpec((2, DISP_TM), lambda i, meta: (0, i), memory_space=pltpu.SMEM),
                pl.BlockSpec((DISP_TM, D), lambda i, meta: (i, 0)),
                pl.BlockSpec((MOE_TM, D), lambda i, meta: (0, 0)),
            ],
            out_specs=pl.BlockSpec(memory_space=pl.ANY),
            scratch_shapes=[pltpu.SemaphoreType.DMA],
        ),
        out_shape=jax.ShapeDtypeStruct((MOE_ROWS, D), F32),
        compiler_params=_cp(("arbitrary",)),
        name="dispatch",
    )(meta, pos, h2, zrow)


def _experts_kernel(meta_ref, xs_ref, wg_ref, wu_ref, wd_ref, ys_ref, wg_s, wu_s, wd_s):
    j = pl.program_id(0)
    live = j < meta_ref[META_NT, 0]
    e = meta_ref[META_TILE_E, j]
    e_prev = meta_ref[META_TILE_E, jnp.maximum(j - 1, 0)]

    @pl.when(live & ((j == 0) | (e != e_prev)))
    def _():
        wg_s[...] = wg_ref[...].astype(BF16)
        wu_s[...] = wu_ref[...].astype(BF16)
        wd_s[...] = wd_ref[...].astype(BF16)

    @pl.when(live)
    def _():
        x = xs_ref[...].astype(BF16)
        g = _dot(x, wg_s[...])
        u = _dot(x, wu_s[...])
        act = (g * _sigmoid(g)) * u
        ys_ref[...] = _dot(act.astype(BF16), wd_s[...])

    @pl.when(jnp.logical_not(live))
    def _():
        ys_ref[...] = jnp.zeros_like(ys_ref)


def _experts(meta, xs, w_gate_e, w_up_e, w_down_e, l):
    def tile(j, meta):
        return jnp.minimum(j, meta[META_NT, 0] - 1)

    def wmap(j, meta):
        return (l, meta[META_TILE_E, tile(j, meta)], 0, 0)

    return pl.pallas_call(
        _experts_kernel,
        grid_spec=pltpu.PrefetchScalarGridSpec(
            num_scalar_prefetch=1,
            grid=(MOE_NT,),
            in_specs=[
                pl.BlockSpec((MOE_TM, D), lambda j, meta: (tile(j, meta), 0)),
                pl.BlockSpec((None, None, D, D_EXP), wmap),
                pl.BlockSpec((None, None, D, D_EXP), wmap),
                pl.BlockSpec((None, None, D_EXP, D), wmap),
            ],
            out_specs=pl.BlockSpec((MOE_TM, D), lambda j, meta: (j, 0)),
            scratch_shapes=[pltpu.VMEM((D, D_EXP), BF16), pltpu.VMEM((D, D_EXP), BF16),
                            pltpu.VMEM((D_EXP, D), BF16)],
        ),
        out_shape=jax.ShapeDtypeStruct((MOE_ROWS, D), F32),
        compiler_params=_cp(("arbitrary",)),
        name="experts",
    )(meta, xs, w_gate_e, w_up_e, w_down_e)


COMB_TM = SEG


def _combine_kernel(pos_ref, posn_ref, w_ref, x1_ref, mod_ref, fg_ref, ys_hbm, oa_ref, ob_ref, buf, sem, *, final):
    i = pl.program_id(0)
    n = pl.num_programs(0)
    slot = lax.rem(i, 2)

    def issue(p_ref, s):
        def body(r, c):
            for k in range(2):
                pltpu.make_async_copy(ys_hbm.at[pl.ds(p_ref[k, r], 1), :],
                                      buf.at[s, k, pl.ds(r, 1), :], sem.at[s]).start(priority=k)
            return c
        lax.fori_loop(0, COMB_TM, body, 0, unroll=8)

    @pl.when(i == 0)
    def _():
        issue(pos_ref, 0)

    @pl.when(i + 1 < n)
    def _():
        issue(posn_ref, 1 - slot)

    for k in range(2):
        pltpu.make_async_copy(ys_hbm.at[pl.ds(0, COMB_TM), :], buf.at[slot, k], sem.at[slot]).wait()

    w = w_ref[...]
    y = w[:, 0:1] * buf[slot, 0] + w[:, 1:2] * buf[slot, 1]
    x = x1_ref[...] + mod_ref[5:6, :] * y
    if final:
        ms = jnp.mean(x * x, axis=-1, keepdims=True)
        x = x * lax.rsqrt(ms + EPS) * fg_ref[...]
    @pl.when(i < T_CTX // COMB_TM)
    def _():
        oa_ref[...] = x

    @pl.when(i >= T_CTX // COMB_TM)
    def _():
        ob_ref[...] = x


def _combine(pos, wts_t, x1, modseg, final_g, ys, l, final):
    n = T // COMB_TM
    return pl.pallas_call(
        functools.partial(_combine_kernel, final=final),
        grid=(n,),
        in_specs=[
            pl.BlockSpec((2, COMB_TM), lambda i: (0, i), memory_space=pltpu.SMEM),
            pl.BlockSpec((2, COMB_TM), lambda i: (0, jnp.minimum(i + 1, n - 1)), memory_space=pltpu.SMEM),
            pl.BlockSpec((COMB_TM, 2), lambda i: (i, 0)),
            pl.BlockSpec((COMB_TM, D), lambda i: (i, 0)),
            pl.BlockSpec((None, None, 8, D), lambda i: (l, i, 0, 0)),
            pl.BlockSpec((1, D), lambda i: (0, 0)),
            pl.BlockSpec(memory_space=pl.ANY),
        ],
        out_specs=_two_part_specs(COMB_TM, T_CTX // COMB_TM),
        out_shape=[jax.ShapeDtypeStruct((T_CTX, D), F32), jax.ShapeDtypeStruct((T_LAT, D), F32)],
        scratch_shapes=[pltpu.VMEM((2, 2, COMB_TM, D), F32), pltpu.SemaphoreType.DMA((2,))],
        compiler_params=_cp(("arbitrary",)),
        name="combine",
    )(pos, pos, wts_t, x1, modseg, final_g.reshape(1, D), ys)


def _rope_tables():
    n = DEC_SEQ
    pos_row = np.repeat(np.arange(n // GRID_W, dtype=np.float32), GRID_W)
    pos_col = np.tile(np.arange(GRID_W, dtype=np.float32), n // GRID_W)
    half = HD // 2
    inv_freq = jnp.asarray(ROPE_THETA, F32) ** (-jnp.arange(0, half, 2, dtype=F32) / half)
    ang = jnp.concatenate([jnp.asarray(pos_row)[:, None] * inv_freq,
                           jnp.asarray(pos_col)[:, None] * inv_freq], axis=-1)
    cos, sin = jnp.cos(ang), jnp.sin(ang)
    cos128 = jnp.tile(cos, (1, 4))
    sin128 = jnp.tile(jnp.concatenate([-sin, sin], axis=-1), (1, 2))
    return cos128, sin128


def _head_mean_matrix():
    idx = np.arange(2 * HD)
    same = (idx[:, None] // HD) == (idx[None, :] // HD)
    return jnp.asarray(same.astype(np.float32) / HD, BF16)


_SEG_ROWS = np.array([0] * (T_CTX // SEG) + [1 + b for b in range(DEC_BATCH) for _ in range(DEC_SEQ // SEG)])


def kernel(x_prompt, x_sample, cache_k, cache_v, state_rec, c, c_ctx, w_mod, b_mod, norm1_g, norm2_g, w_in, conv_w, conv_b, rg_wa, rg_ba, rg_wx, rg_bx, rg_lambda, q_norm_g, k_norm_g, w_rec_out, w_att_out, w_out, w_router, router_bias, w_gate_e, w_up_e, w_down_e, final_g):
    xa, xb = x_prompt.reshape(T_CTX, D), x_sample.reshape(T_LAT, D)

    cvecs = jnp.concatenate([c_ctx[None, :], c, jnp.zeros((3, D), F32)], axis=0)
    mods = _mods(cvecs, w_mod, b_mod).reshape(DEPTH, 8, 6, D)
    modseg = jnp.pad(mods[:, _SEG_ROWS], ((0, 0), (0, 0), (0, 2), (0, 0)))

    cos128, sin128 = _rope_tables()
    bd = _head_mean_matrix()
    qg128 = jnp.tile(q_norm_g, (1, 2)).reshape(DEPTH, 1, 2 * HD)
    kg128 = jnp.tile(k_norm_g, (1, 2)).reshape(DEPTH, 1, 2 * HD)
    wg = jnp.concatenate([rg_wa[:, 0], rg_wx[:, 0], rg_wa[:, 1], rg_wx[:, 1]], axis=-1)
    pvec = jnp.stack([rg_ba[:, 0], rg_bx[:, 0], rg_ba[:, 1], rg_bx[:, 1],
                      rg_lambda[:, 0], rg_lambda[:, 1], conv_b, jnp.zeros_like(conv_b)], axis=1)
    wrt = w_router.T
    rbias = router_bias.reshape(N_EXP, 1)
    zrow = jnp.zeros((MOE_TM, D), F32)

    new_k, new_v, new_s = [], [], []
    for l in range(DEPTH):
        proj = _inproj(xa, xb, modseg, norm1_g, w_in, l)
        h0 = jnp.concatenate([jnp.zeros((T_CTX // UNIT, 2, D), F32), state_rec[:, l]], axis=0)
        yrec, stf, stb = _rec(proj, conv_w, pvec, wg, h0, l)
        qc, kc, vc = _qkv(proj, qg128, kg128, cos128, sin128, bd, l, latent=False)
        ql, kl, vl = _qkv(proj, qg128, kg128, cos128, sin128, bd, l, latent=True)
        o_ctx = _attn_ctx(qc, kc, vc)
        o_lat = _attn_lat(ql, cache_k, cache_v, kl, vl, l)
        x1, h2, idx, wts = _merge(yrec, o_ctx, o_lat, proj, xa, xb, modseg, norm2_g,
                                  w_rec_out, w_att_out, w_out, wrt, rbias, l)
        pos, meta = _route_pos(idx)
        xs = _dispatch(meta, pos, h2, zrow)
        ys = _experts(meta, xs, w_gate_e, w_up_e, w_down_e, l)
        xa, xb = _combine(pos, wts.T, x1, modseg, final_g, ys, l, final=(l == DEPTH - 1))
        new_k.append(kc)
        new_v.append(vc)
        n_cu = T_CTX // UNIT
        spu = UNIT // SEQ
        hf_last = stf[:n_cu].reshape(n_cu, spu, 2, D)[:, :, 1].reshape(BATCH, D)
        hb_first = stb[:n_cu].reshape(n_cu, spu, 2, D)[:, :, 0].reshape(BATCH, D)
        new_s.append(jnp.stack([hf_last, hb_first], axis=1))

    y_prompt = xa.reshape(BATCH, SEQ, D)
    y_sample = xb.reshape(DEC_BATCH, DEC_SEQ, D)
    return (y_prompt, y_sample, jnp.stack(new_k, axis=1), jnp.stack(new_v, axis=1), jnp.stack(new_s, axis=1))
```

```python
import functools

import numpy as np
import jax
import jax.numpy as jnp
from jax import lax
from jax.experimental import pallas as pl
from jax.experimental.pallas import tpu as pltpu

F32 = jnp.float32
BF16 = jnp.bfloat16

D = 1024
BATCH = 16
SEQ = 256
DEPTH = 2
DEC_BATCH = 4
DEC_SEQ = 1024
PAST = 256
GRID_W = 64
N_HEADS = 16
N_KV = 4
HD = 64
N_RG_BLK = 8
RG_BLK = 128
RG_C = 8.0
N_EXP = 16
D_EXP = 512
ROPE_THETA = 10000.0
EPS = 1e-6
P_IN = 5632
TINY = float(np.finfo(np.float32).tiny)
NEG_LOG2E = -float(np.log2(np.e))

T_CTX = BATCH * SEQ
T_LAT = DEC_BATCH * DEC_SEQ
T = T_CTX + T_LAT
SEG = 256
N_SEG = T // SEG
UNIT = 1024
N_UNIT = T // UNIT
CHUNK = UNIT // 8
CSTRIDE = CHUNK + 8

VMEM_LIMIT = 56 * 1024 * 1024


def _cp(sem):
    return pltpu.CompilerParams(dimension_semantics=sem, vmem_limit_bytes=VMEM_LIMIT)


def _split(x):
    hi = x.astype(BF16)
    lo = (x - hi.astype(F32)).astype(BF16)
    return hi, lo


def _sigmoid(x):
    return 0.5 * jnp.tanh(0.5 * x) + 0.5


def _dot(a, b):
    return jnp.dot(a, b, preferred_element_type=F32)


def _dot_nt(a, b):
    return lax.dot_general(a, b, (((1,), (1,)), ((), ())), preferred_element_type=F32)


def _mods_kernel(c_ref, w_ref, b_ref, o_ref):
    c = c_ref[...]
    s = c * jax.nn.sigmoid(c)
    s_hi, s_lo = _split(s)
    w_hi, w_lo = _split(w_ref[...])
    o_ref[...] = _dot(s_hi, w_hi) + _dot(s_hi, w_lo) + _dot(s_lo, w_hi) + b_ref[...]


def _mods(cvecs, w_mod, b_mod):
    tn = 1536
    return pl.pallas_call(
        _mods_kernel,
        grid=(DEPTH, 6 * D // tn),
        in_specs=[
            pl.BlockSpec((8, D), lambda l, j: (0, 0)),
            pl.BlockSpec((None, D, tn), lambda l, j: (l, 0, j)),
            pl.BlockSpec((None, 1, tn), lambda l, j: (l, 0, j)),
        ],
        out_specs=pl.BlockSpec((None, 8, tn), lambda l, j: (l, 0, j)),
        out_shape=jax.ShapeDtypeStruct((DEPTH, 8, 6 * D), F32),
        compiler_params=_cp(("arbitrary", "arbitrary")),
        name="mods",
    )(cvecs, w_mod, b_mod.reshape(DEPTH, 1, 6 * D))


def _norm_mod(x, g, shift, scale):
    ms = jnp.mean(x * x, axis=-1, keepdims=True)
    return x * lax.rsqrt(ms + EPS) * g * (1.0 + scale) + shift


def _two_part_specs(tm, n_ctx):
    return [pl.BlockSpec((tm, D), lambda i, *_: (jnp.minimum(i, n_ctx - 1), 0)),
            pl.BlockSpec((tm, D), lambda i, *_: (jnp.maximum(i - n_ctx, 0), 0))]


def _inproj_kernel(xa_ref, xb_ref, mod_ref, g_ref, w_ref, o_ref, h_ref, *, tm):
    def prologue(x_ref):
        def seg(s, carry):
            r0 = pl.multiple_of(s * SEG, SEG)
            m = mod_ref[s]
            h = _norm_mod(x_ref[pl.ds(r0, SEG), :], g_ref[...], m[0:1, :], m[1:2, :])
            h_ref[pl.ds(r0, SEG), :] = h.astype(BF16)
            return carry
        lax.fori_loop(0, tm // SEG, seg, 0)

    first = pl.program_id(1) == 0
    is_ctx = pl.program_id(0) < T_CTX // tm

    @pl.when(first & is_ctx)
    def _():
        prologue(xa_ref)

    @pl.when(first & jnp.logical_not(is_ctx))
    def _():
        prologue(xb_ref)

    o_ref[...] = _dot(h_ref[...], w_ref[...].astype(BF16)).astype(BF16)


def _inproj(xa, xb, modseg, norm_g, w_in, l):
    tm, tn = 2048, 512
    return pl.pallas_call(
        functools.partial(_inproj_kernel, tm=tm),
        grid=(T // tm, P_IN // tn),
        in_specs=_two_part_specs(tm, T_CTX // tm) + [
            pl.BlockSpec((None, tm // SEG, 8, D), lambda i, j: (l, i, 0, 0)),
            pl.BlockSpec((None, 1, D), lambda i, j: (l, 0, 0)),
            pl.BlockSpec((None, D, tn), lambda i, j: (l, 0, j)),
        ],
        out_specs=pl.BlockSpec((tm, tn), lambda i, j: (i, j)),
        out_shape=jax.ShapeDtypeStruct((T, P_IN), BF16),
        scratch_shapes=[pltpu.VMEM((tm, D), BF16)],
        compiler_params=_cp(("arbitrary", "arbitrary")),
        name="inproj",
    )(xa, xb, modseg, norm_g.reshape(DEPTH, 1, D), w_in)


REC_CW = 512
HALO = 16


def _rec_kernel(xr_ref, gate_ref, cw_ref, pv_ref, wg_ref, h0_ref,
                y_ref, stf_ref, stb_ref,
                af_ref, bf_ref, ab_ref, bb_ref, wgh_ref):
    u = pl.program_id(0)
    is_ctx = u < (T_CTX // UNIT)
    seq_len = jnp.where(is_ctx, SEQ, DEC_SEQ)
    nblk = REC_CW // RG_BLK

    pv = pv_ref[...]
    cwts = cw_ref[...]
    conv_b = pv[6:7, :]

    def softplus_neg(lam):
        z = -lam
        return jnp.maximum(z, 0.0) + jnp.log1p(jnp.exp(-jnp.abs(z)))

    c4s = tuple((0.5 * RG_C) * softplus_neg(pv[4 + d:5 + d, :]) for d in range(2))
    pv_h = 0.5 * pv
    for n in range(nblk):
        wgh_ref[n] = (0.5 * wg_ref[n]).astype(BF16)
    a_refs = (af_ref, ab_ref)
    b_refs = (bf_ref, bb_ref)

    def gates(ci, carry):
        base = pl.multiple_of(ci * CHUNK, CHUNK)
        lo = pl.multiple_of(jnp.maximum(base - HALO, 0), HALO)
        hi = pl.multiple_of(jnp.minimum(base + CHUNK, UNIT - HALO), HALO)
        main = xr_ref[pl.ds(base, CHUNK), :].astype(F32)
        win = jnp.concatenate([xr_ref[pl.ds(lo, HALO), :].astype(F32), main,
                               xr_ref[pl.ds(hi, HALO), :].astype(F32)], axis=0)
        n_win = CHUNK + 2 * HALO
        row8 = lax.broadcasted_iota(jnp.int32, (8, 1), 0)
        tl_head = jnp.bitwise_and(base + row8, seq_len - 1)
        tl_tail = jnp.bitwise_and(base + (CHUNK - 8) + row8, seq_len - 1)

        def tap(shift, head_ok=None, tail_ok=None):
            x = pltpu.roll(win, shift, 0)[HALO:HALO + CHUNK]
            if head_ok is not None:
                return jnp.concatenate([jnp.where(head_ok, x[:8], 0.0), x[8:]], axis=0)
            return jnp.concatenate([x[:CHUNK - 8], jnp.where(tail_ok, x[CHUNK - 8:], 0.0)], axis=0)

        xm2 = tap(2, head_ok=tl_head >= 2)
        xm1 = tap(1, head_ok=tl_head >= 1)
        xp1 = tap(n_win - 1, tail_ok=tl_tail <= seq_len - 2)
        xc = conv_b + xm2 * cwts[0:1, :]
        xc = xc + xm1 * cwts[1:2, :]
        xc = xc + main * cwts[2:3, :]
        xc = xc + xp1 * cwts[3:4, :]
        for n in range(nblk):
            ls = slice(n * RG_BLK, (n + 1) * RG_BLK)
            xn = xc[:, ls]
            hx = 0.5 * xn
            pre_h = _dot(xn.astype(BF16), wgh_ref[n])
            for d in range(2):
                th_r = jnp.tanh(pre_h[:, (2 * d) * RG_BLK:(2 * d + 1) * RG_BLK] + pv_h[2 * d:2 * d + 1, ls])
                th_i = jnp.tanh(pre_h[:, (2 * d + 1) * RG_BLK:(2 * d + 2) * RG_BLK] + pv_h[2 * d + 1:2 * d + 2, ls])
                c4 = c4s[d][:, ls]
                nla = c4 * th_r + c4
                a = jnp.exp2(nla * NEG_LOG2E)
                s = jnp.tanh(nla) * (a * a + 1.0)
                inp = (s * lax.rsqrt(jnp.maximum(s, TINY))) * (hx * th_i + hx)
                sbase = pl.multiple_of(ci * CSTRIDE, 8)
                a_refs[d][n, pl.ds(sbase, CHUNK), :] = a
                b_refs[d][n, pl.ds(sbase, CHUNK), :] = inp
        return carry

    lax.fori_loop(0, 8, gates, 0)

    zeros = [jnp.zeros((8, RG_BLK), F32)] * nblk
    ones = [jnp.ones((8, RG_BLK), F32)] * nblk
    hf, pf, hb, pb = list(zeros), list(ones), list(zeros), list(ones)
    for r in range(CHUNK):
        rows_f = pl.ds(r, 8, stride=CSTRIDE)
        rows_b = pl.ds(CHUNK - 1 - r, 8, stride=CSTRIDE)
        for n in range(nblk):
            a = af_ref[n, rows_f, :]
            hf[n] = a * hf[n] + bf_ref[n, rows_f, :]
            pf[n] = a * pf[n]
            bf_ref[n, rows_f, :] = hf[n]
            af_ref[n, rows_f, :] = pf[n]
            a = ab_ref[n, rows_b, :]
            hb[n] = a * hb[n] + bb_ref[n, rows_b, :]
            pb[n] = a * pb[n]
            bb_ref[n, rows_b, :] = hb[n]
            ab_ref[n, rows_b, :] = pb[n]
    hf, pf, hb, pb = (jnp.concatenate(v, axis=-1) for v in (hf, pf, hb, pb))

    cps = jnp.where(is_ctx, SEQ // CHUNK, DEC_SEQ // CHUNK)
    h0f = h0_ref[0:1, :]
    h0b = h0_ref[1:2, :]
    cf = [h0f]
    for c in range(1, 8):
        chain = hf[c - 1:c, :] + pf[c - 1:c, :] * cf[c - 1]
        cf.append(jnp.where(jnp.bitwise_and(c, cps - 1) == 0, h0f, chain))
    cb = [None] * 8
    cb[7] = h0b
    for c in range(6, -1, -1):
        chain = hb[c + 1:c + 2, :] + pb[c + 1:c + 2, :] * cb[c + 1]
        cb[c] = jnp.where(jnp.bitwise_and(c, cps - 1) == cps - 1, h0b, chain)
    carry_f = jnp.concatenate(cf, axis=0)
    carry_b = jnp.concatenate(cb, axis=0)
    stf_ref[...] = hf + pf * carry_f
    stb_ref[...] = hb + pb * carry_b

    for ci in range(8):
        rows = pl.ds(ci * CHUNK, CHUNK)
        srows = pl.ds(ci * CSTRIDE, CHUNK)
        for n in range(nblk):
            ls = slice(n * RG_BLK, (n + 1) * RG_BLK)
            h_f = bf_ref[n, srows, :] + af_ref[n, srows, :] * carry_f[ci:ci + 1, ls]
            h_b = bb_ref[n, srows, :] + ab_ref[n, srows, :] * carry_b[ci:ci + 1, ls]
            g = gate_ref[rows, ls].astype(F32)
            y_ref[rows, ls] = ((h_f + h_b) * jax.nn.gelu(g, approximate=True)).astype(BF16)


def _rec(proj, conv_w, pvec, wg, h0, l):
    ncb = D // REC_CW
    return pl.pallas_call(
        _rec_kernel,
        grid=(N_UNIT, ncb),
        in_specs=[
            pl.BlockSpec((UNIT, REC_CW), lambda u, c: (u, c)),
            pl.BlockSpec((UNIT, REC_CW), lambda u, c: (u, ncb + c)),
            pl.BlockSpec((None, 4, REC_CW), lambda u, c: (l, 0, c)),
            pl.BlockSpec((None, 8, REC_CW), lambda u, c: (l, 0, c)),
            pl.BlockSpec((None, REC_CW // RG_BLK, RG_BLK, 4 * RG_BLK), lambda u, c: (l, c, 0, 0)),
            pl.BlockSpec((None, 2, REC_CW), lambda u, c: (u, 0, c)),
        ],
        out_specs=[
            pl.BlockSpec((UNIT, REC_CW), lambda u, c: (u, c)),
            pl.BlockSpec((None, 8, REC_CW), lambda u, c: (u, 0, c)),
            pl.BlockSpec((None, 8, REC_CW), lambda u, c: (u, 0, c)),
        ],
        out_shape=[
            jax.ShapeDtypeStruct((T, D), BF16),
            jax.ShapeDtypeStruct((N_UNIT, 8, D), F32),
            jax.ShapeDtypeStruct((N_UNIT, 8, D), F32),
        ],
        scratch_shapes=[pltpu.VMEM((REC_CW // RG_BLK, 8 * CSTRIDE, RG_BLK), F32)] * 4
        + [pltpu.VMEM((REC_CW // RG_BLK, RG_BLK, 4 * RG_BLK), BF16)],
        compiler_params=_cp(("arbitrary", "arbitrary")),
        name="rec",
    )(proj, proj, conv_w, pvec, wg, h0)


def _head_norm(x, g128, bd):
    hi, lo = _split(x * x)
    ms = _dot(hi, bd) + _dot(lo, bd)
    return x * lax.rsqrt(ms + EPS) * g128


def _rope(x, cos, sin_signed):
    lane = lax.broadcasted_iota(jnp.int32, x.shape, 1)
    first_half = jnp.bitwise_and(lane, HD - 1) < HD // 2
    partner = jnp.where(first_half, pltpu.roll(x, 2 * HD - HD // 2, 1), pltpu.roll(x, HD // 2, 1))
    return x * cos + partner * sin_signed


def _qkv_kernel(q_ref, k_ref, v_ref, qg_ref, kg_ref, cos_ref, sin_ref, bd_ref,
                qo_ref, ko_ref, vo_ref, *, rope):
    bd = bd_ref[...]
    scale = HD ** -0.5 * float(np.log2(np.e))
    for j in range(N_HEADS // 2):
        x = _head_norm(q_ref[:, 2 * HD * j:2 * HD * (j + 1)].astype(F32), qg_ref[...], bd)
        if rope:
            x = _rope(x, cos_ref[...], sin_ref[...])
        x = x * scale
        qo_ref[2 * j] = x[:, :HD].astype(qo_ref.dtype)
        qo_ref[2 * j + 1] = x[:, HD:].astype(qo_ref.dtype)
    for j in range(N_KV // 2):
        x = _head_norm(k_ref[:, 2 * HD * j:2 * HD * (j + 1)].astype(F32), kg_ref[...], bd)
        if rope:
            x = _rope(x, cos_ref[...], sin_ref[...])
        ko_ref[2 * j] = x[:, :HD].astype(ko_ref.dtype)
        ko_ref[2 * j + 1] = x[:, HD:].astype(ko_ref.dtype)
        v = v_ref[:, 2 * HD * j:2 * HD * (j + 1)].astype(F32)
        vo_ref[2 * j] = v[:, :HD].astype(vo_ref.dtype)
        vo_ref[2 * j + 1] = v[:, HD:].astype(vo_ref.dtype)


def _qkv(proj, qg128, kg128, cos128, sin128, bd, l, latent):
    tm = SEG
    n = T_LAT // tm if latent else T_CTX // tm
    roff = T_CTX // tm if latent else 0
    per_seq = DEC_SEQ // tm
    if latent:
        kv_shape = (DEC_BATCH, N_KV, DEC_SEQ, HD)
        kv_spec = pl.BlockSpec((None, N_KV, tm, HD), lambda i: (i // per_seq, 0, i % per_seq, 0))
        kv_dtype = BF16
        tab_map = lambda i: (i % per_seq, 0)
    else:
        kv_shape = (BATCH, N_KV, SEQ, HD)
        kv_spec = pl.BlockSpec((None, N_KV, tm, HD), lambda i: (i, 0, 0, 0))
        kv_dtype = F32
        tab_map = lambda i: (0, 0)
    return pl.pallas_call(
        functools.partial(_qkv_kernel, rope=latent),
        grid=(n,),
        in_specs=[
            pl.BlockSpec((tm, D), lambda i: (roff + i, 2)),
            pl.BlockSpec((tm, N_KV * HD), lambda i: (roff + i, 3 * D // (N_KV * HD))),
            pl.BlockSpec((tm, N_KV * HD), lambda i: (roff + i, 3 * D // (N_KV * HD) + 1)),
            pl.BlockSpec((None, 1, 2 * HD), lambda i: (l, 0, 0)),
            pl.BlockSpec((None, 1, 2 * HD), lambda i: (l, 0, 0)),
            pl.BlockSpec((tm, 2 * HD), tab_map),
            pl.BlockSpec((tm, 2 * HD), tab_map),
            pl.BlockSpec((2 * HD, 2 * HD), lambda i: (0, 0)),
        ],
        out_specs=[
            pl.BlockSpec((N_HEADS, tm, HD), lambda i: (0, i, 0)),
            kv_spec,
            kv_spec,
        ],
        out_shape=[
            jax.ShapeDtypeStruct((N_HEADS, n * tm, HD), BF16),
            jax.ShapeDtypeStruct(kv_shape, kv_dtype),
            jax.ShapeDtypeStruct(kv_shape, kv_dtype),
        ],
        compiler_params=_cp(("arbitrary",)),
        name="qkv_lat" if latent else "qkv_ctx",
    )(proj, proj, proj, qg128, kg128, cos128, sin128, bd)


def _with_ones(v):
    return jnp.concatenate([v, jnp.ones_like(v)], axis=-1)


def _softmax_pv(q, k, v_ext):
    s = _dot_nt(q, k)
    m = jnp.max(s, axis=-1, keepdims=True)
    p = jnp.exp2(s - m).astype(BF16)
    r = _dot(p, v_ext)
    return r[:, :HD] / r[:, HD:HD + 1]


def _attend_heads(q_ref, k, v_ext):
    return jnp.concatenate([_softmax_pv(q_ref[h], k, v_ext) for h in range(N_HEADS // N_KV)], axis=-1)


def _attn_ctx_kernel(q_ref, k_ref, v_ref, o_ref):
    o = _attend_heads(q_ref, k_ref[...].astype(BF16), _with_ones(v_ref[...].astype(BF16)))
    o_ref[...] = o.astype(BF16)


def _attn_ctx(qh, kc, vc):
    g = N_HEADS // N_KV
    return pl.pallas_call(
        _attn_ctx_kernel,
        grid=(BATCH, N_KV),
        in_specs=[
            pl.BlockSpec((g, SEQ, HD), lambda b, h: (h, b, 0)),
            pl.BlockSpec((None, None, SEQ, HD), lambda b, h: (b, h, 0, 0)),
            pl.BlockSpec((None, None, SEQ, HD), lambda b, h: (b, h, 0, 0)),
        ],
        out_specs=pl.BlockSpec((SEQ, g * HD), lambda b, h: (b, h)),
        out_shape=jax.ShapeDtypeStruct((T_CTX, D), BF16),
        compiler_params=_cp(("arbitrary", "arbitrary")),
        name="attn_ctx",
    )(qh, kc, vc)


ATT_TQ = 256


def _attn_lat_kernel(q_ref, pk_ref, pv_ref, k_ref, v_ref, o_ref, k_s, v_s):
    k_s[0:PAST, :] = pk_ref[...].astype(BF16)
    k_s[PAST:, :] = k_ref[...]
    v_s[0:PAST, :] = _with_ones(pv_ref[...].astype(BF16))
    v_s[PAST:, :] = _with_ones(v_ref[...])

    def q_tile(qi, carry):
        rows = pl.ds(pl.multiple_of(qi * ATT_TQ, ATT_TQ), ATT_TQ)
        o = jnp.concatenate([_softmax_pv(q_ref[h, rows, :], k_s[...], v_s[...])
                             for h in range(N_HEADS // N_KV)], axis=-1)
        o_ref[rows, :] = o.astype(BF16)
        return carry

    lax.fori_loop(0, DEC_SEQ // ATT_TQ, q_tile, 0)


def _attn_lat(qh, cache_k, cache_v, kr, vr, l):
    g = N_HEADS // N_KV
    return pl.pallas_call(
        _attn_lat_kernel,
        grid=(DEC_BATCH, N_KV),
        in_specs=[
            pl.BlockSpec((g, DEC_SEQ, HD), lambda b, h: (h, b, 0)),
            pl.BlockSpec((None, None, None, PAST, HD), lambda b, h: (b, l, h, 0, 0)),
            pl.BlockSpec((None, None, None, PAST, HD), lambda b, h: (b, l, h, 0, 0)),
            pl.BlockSpec((None, None, DEC_SEQ, HD), lambda b, h: (b, h, 0, 0)),
            pl.BlockSpec((None, None, DEC_SEQ, HD), lambda b, h: (b, h, 0, 0)),
        ],
        out_specs=pl.BlockSpec((DEC_SEQ, g * HD), lambda b, h: (b, h)),
        out_shape=jax.ShapeDtypeStruct((T_LAT, D), BF16),
        scratch_shapes=[pltpu.VMEM((PAST + DEC_SEQ, HD), BF16), pltpu.VMEM((PAST + DEC_SEQ, 2 * HD), BF16)],
        compiler_params=_cp(("arbitrary", "arbitrary")),
        name="attn_lat",
    )(qh, cache_k, cache_v, kr, vr)


MERGE_TM = 512


def _route(lt, bias):
    rows = [lt[e:e + 1, :] for e in range(N_EXP)]
    m = rows[0]
    for e in range(1, N_EXP):
        m = jnp.maximum(m, rows[e])
    ex = [jnp.exp(r - m) for r in rows]
    z = ex[0]
    for e in range(1, N_EXP):
        z = z + ex[e]
    probs = [x / z for x in ex]
    sel = [probs[e] + bias[e:e + 1, :] for e in range(N_EXP)]

    def top2_sum(v):
        a, b = jnp.maximum(v[0], v[1]), jnp.minimum(v[0], v[1])
        c, d = jnp.maximum(v[2], v[3]), jnp.minimum(v[2], v[3])
        return jnp.maximum(a, c) + jnp.maximum(jnp.minimum(a, c), jnp.maximum(b, d))

    scores = [top2_sum(sel[4 * g:4 * g + 4]) for g in range(4)]
    best = jnp.zeros_like(scores[0], dtype=jnp.int32)
    best_s = scores[0]
    for g in range(1, 4):
        take = scores[g] > best_s
        best = jnp.where(take, g, best)
        best_s = jnp.where(take, scores[g], best_s)
    cs, cp = [], []
    for j in range(4):
        s_j, p_j = sel[j], probs[j]
        for g in range(1, 4):
            s_j = jnp.where(best == g, sel[4 * g + j], s_j)
            p_j = jnp.where(best == g, probs[4 * g + j], p_j)
        cs.append(s_j)
        cp.append(p_j)
    neg = jnp.full_like(cs[0], -jnp.inf)

    def argmax4(v):
        bi = jnp.zeros_like(best)
        bv = v[0]
        for j in range(1, 4):
            take = v[j] > bv
            bi = jnp.where(take, j, bi)
            bv = jnp.where(take, v[j], bv)
        return bi

    def pick(v, idx):
        out = v[0]
        for j in range(1, 4):
            out = jnp.where(idx == j, v[j], out)
        return out

    i1 = argmax4(cs)
    cs2 = [jnp.where(i1 == j, neg, cs[j]) for j in range(4)]
    i2 = argmax4(cs2)
    i2 = jnp.where((i2 == 0) & (i1 == 0), 1, i2)
    w1, w2 = pick(cp, i1), pick(cp, i2)
    den = w1 + w2
    return best * 4 + i1, best * 4 + i2, w1 / den, w2 / den


def _merge_kernel(yrec_ref, oa_ref, ob_ref, gr0_ref, gr1_ref, ga0_ref, ga1_ref, xa_ref, xb_ref, mod_ref, g2_ref,
                  wrec_ref, watt_ref, wout_ref, wrt_ref, rb_ref,
                  x1_ref, h2_ref, idx_ref, wts_ref,
                  wrec_s, watt_s, wout_s):
    @pl.when(pl.program_id(0) == 0)
    def _():
        wrec_s[...] = wrec_ref[...].astype(BF16)
        watt_s[...] = watt_ref[...].astype(BF16)
        wout_s[...] = wout_ref[...].astype(BF16)

    is_ctx = pl.program_id(0) < T_CTX // MERGE_TM
    args = (yrec_ref, gr0_ref, gr1_ref, ga0_ref, ga1_ref, mod_ref, g2_ref, wrt_ref, rb_ref,
            x1_ref, h2_ref, idx_ref, wts_ref, wrec_s, watt_s, wout_s)

    @pl.when(is_ctx)
    def _():
        _merge_body(oa_ref, xa_ref, *args)

    @pl.when(jnp.logical_not(is_ctx))
    def _():
        _merge_body(ob_ref, xb_ref, *args)


def _merge_body(oatt_ref, x_ref, yrec_ref, gr0_ref, gr1_ref, ga0_ref, ga1_ref, mod_ref, g2_ref, wrt_ref, rb_ref,
                x1_ref, h2_ref, idx_ref, wts_ref, wrec_s, watt_s, wout_s):
    half = D // 2
    b_rec = _dot(yrec_ref[...], wrec_s[...])
    b_att = _dot(oatt_ref[...], watt_s[...])
    m0 = _sigmoid(gr0_ref[...].astype(F32)) * b_rec[:, :half] + _sigmoid(ga0_ref[...].astype(F32)) * b_att[:, :half]
    m1 = _sigmoid(gr1_ref[...].astype(F32)) * b_rec[:, half:] + _sigmoid(ga1_ref[...].astype(F32)) * b_att[:, half:]
    merged = jnp.concatenate([m0, m1], axis=-1).astype(BF16)
    out = _dot(merged, wout_s[...])

    hs = []
    for s in range(MERGE_TM // SEG):
        rows = slice(s * SEG, (s + 1) * SEG)
        m = mod_ref[s]
        x1 = x_ref[rows, :] + m[2:3, :] * out[rows, :]
        x1_ref[rows, :] = x1
        h2 = _norm_mod(x1, g2_ref[...], m[3:4, :], m[4:5, :])
        h2_ref[rows, :] = h2
        hs.append(h2)
    h2 = jnp.concatenate(hs, axis=0)

    h_hi, h_lo = _split(h2)
    w_hi, w_lo = _split(wrt_ref[...])
    lt = _dot_nt(w_hi, h_hi) + _dot_nt(w_hi, h_lo) + _dot_nt(w_lo, h_hi)
    e1, e2, w1, w2 = _route(lt, rb_ref[...])
    idx_ref[...] = jnp.concatenate([e1, e2], axis=0)
    wts_ref[...] = jnp.concatenate([w1, w2], axis=0)


def _merge(yrec, o_ctx, o_lat, proj, xa, xb, modseg, norm2_g, w_rec_out, w_att_out, w_out, wrt, rbias, l):
    tm = MERGE_TM
    half = D // 2
    gcol = (3 * D + 2 * N_KV * HD) // half
    wspec = pl.BlockSpec((None, D, D), lambda i: (l, 0, 0))
    return pl.pallas_call(
        _merge_kernel,
        grid=(T // tm,),
        in_specs=[pl.BlockSpec((tm, D), lambda i: (i, 0))] + _two_part_specs(tm, T_CTX // tm) + [
            pl.BlockSpec((tm, half), lambda i: (i, gcol)),
            pl.BlockSpec((tm, half), lambda i: (i, gcol + 1)),
            pl.BlockSpec((tm, half), lambda i: (i, gcol + 2)),
            pl.BlockSpec((tm, half), lambda i: (i, gcol + 3)),
        ] + _two_part_specs(tm, T_CTX // tm) + [
            pl.BlockSpec((None, tm // SEG, 8, D), lambda i: (l, i, 0, 0)),
            pl.BlockSpec((None, 1, D), lambda i: (l, 0, 0)),
            wspec, wspec, wspec,
            pl.BlockSpec((N_EXP, D), lambda i: (0, 0)),
            pl.BlockSpec((N_EXP, 1), lambda i: (0, 0)),
        ],
        out_specs=[
            pl.BlockSpec((tm, D), lambda i: (i, 0)),
            pl.BlockSpec((tm, D), lambda i: (i, 0)),
            pl.BlockSpec((2, tm), lambda i: (0, i)),
            pl.BlockSpec((2, tm), lambda i: (0, i)),
        ],
        out_shape=[
            jax.ShapeDtypeStruct((T, D), F32),
            jax.ShapeDtypeStruct((T, D), F32),
            jax.ShapeDtypeStruct((2, T), jnp.int32),
            jax.ShapeDtypeStruct((2, T), F32),
        ],
        scratch_shapes=[pltpu.VMEM((D, D), BF16)] * 3,
        compiler_params=_cp(("arbitrary",)),
        name="merge",
    )(yrec, o_ctx, o_lat, proj, proj, proj, proj, xa, xb, modseg, norm2_g.reshape(DEPTH, 1, D),
      w_rec_out, w_att_out, w_out, wrt, rbias)


MOE_TM = 256
MOE_NT = 2 * T // MOE_TM + N_EXP
MOE_ROWS = MOE_NT * MOE_TM
META_TILE_E, META_CNT, META_OFF, META_END, META_NT = 0, 1, 2, 3, 4


def _pos_kernel(idx_ref, pos_ref, meta_ref):
    shift = MOE_TM.bit_length() - 1
    idx = idx_ref[...]
    eid = lax.broadcasted_iota(jnp.int32, (N_EXP, T), 0)
    m0 = eid == idx[0:1, :]
    m1 = eid == idx[1:2, :]
    member = jnp.where(m0 | m1, 1.0, 0.0)
    cnt = jnp.sum(member, axis=1, keepdims=True).astype(jnp.int32)
    ntile = jnp.right_shift(cnt + (MOE_TM - 1), shift)
    offs, acc = [], jnp.zeros((1, 1), jnp.int32)
    for e in range(N_EXP):
        offs.append(acc)
        acc = acc + ntile[e:e + 1, :]
    off_t = jnp.concatenate(offs, axis=0)
    end_t = off_t + ntile

    blk = 256
    r_i = lax.broadcasted_iota(jnp.int32, (blk, blk), 0)
    c_i = lax.broadcasted_iota(jnp.int32, (blk, blk), 1)
    upper = jnp.where(r_i <= c_i, 1.0, 0.0).astype(BF16)
    run = (off_t * MOE_TM).astype(F32)
    for j in range(T // blk):
        ls = slice(j * blk, (j + 1) * blk)
        mb = member[:, ls]
        inc = _dot(mb.astype(BF16), upper)
        dest = run + inc - mb
        pos_ref[0:1, ls] = jnp.sum(jnp.where(m0[:, ls], dest, 0.0), axis=0, keepdims=True).astype(jnp.int32)
        pos_ref[1:2, ls] = jnp.sum(jnp.where(m1[:, ls], dest, 0.0), axis=0, keepdims=True).astype(jnp.int32)
        run = run + inc[:, blk - 1:blk]

    lane = lax.broadcasted_iota(jnp.int32, (1, 128), 1)
    zero = jnp.zeros((1, 128), jnp.int32)
    tile_e, cnt_row, off_row, end_row = zero, zero, zero, zero
    for e in range(N_EXP):
        tile_e = tile_e + jnp.where(lane >= end_t[e:e + 1, :], 1, 0)
        here = lane == e
        cnt_row = jnp.where(here, cnt[e:e + 1, :], cnt_row)
        off_row = jnp.where(here, off_t[e:e + 1, :] * MOE_TM, off_row)
        end_row = jnp.where(here, end_t[e:e + 1, :] * MOE_TM, end_row)
    tile_e = jnp.minimum(tile_e, N_EXP - 1)
    nt_row = zero + acc
    meta_ref[...] = jnp.concatenate([tile_e, cnt_row, off_row, end_row, nt_row, zero, zero, zero], axis=0)


def _route_pos(idx):
    return pl.pallas_call(
        _pos_kernel,
        grid=(1,),
        in_specs=[pl.BlockSpec((2, T), lambda i: (0, 0))],
        out_specs=[pl.BlockSpec((2, T), lambda i: (0, 0)), pl.BlockSpec((8, 128), lambda i: (0, 0))],
        out_shape=[jax.ShapeDtypeStruct((2, T), jnp.int32), jax.ShapeDtypeStruct((8, 128), jnp.int32)],
        compiler_params=_cp(("arbitrary",)),
        name="route_pos",
    )(idx)


DISP_TM = 256


def _dispatch_kernel(meta_ref, pos_ref, h_ref, z_hbm, xs_hbm, sem):
    i = pl.program_id(0)

    def row_copy(src, src_row, dst_row):
        return pltpu.make_async_copy(src.at[pl.ds(src_row, 1), :], xs_hbm.at[pl.ds(dst_row, 1), :], sem)

    for r in range(DISP_TM):
        row_copy(h_ref, r, pos_ref[0, r]).start(priority=0)
        row_copy(h_ref, r, pos_ref[1, r]).start(priority=1)

    e = jnp.minimum(i, N_EXP - 1)
    pad0 = meta_ref[META_OFF, e] + meta_ref[META_CNT, e]
    npad = jnp.where(i < N_EXP, meta_ref[META_END, e] - pad0, 0)

    def zero_fill(p, c):
        row_copy(z_hbm, 0, p).start()
        return c

    lax.fori_loop(pad0, pad0 + npad, zero_fill, 0)

    for _ in range(2):
        pltpu.make_async_copy(h_ref, xs_hbm.at[pl.ds(0, DISP_TM), :], sem).wait()

    tail = meta_ref[META_NT, 0] + i
    has_tail = (i < N_EXP) & (tail < MOE_NT)

    def tail_copy():
        rows = pl.ds(pl.multiple_of(tail * MOE_TM, MOE_TM), MOE_TM)
        return pltpu.make_async_copy(z_hbm, xs_hbm.at[rows, :], sem)

    @pl.when(has_tail)
    def _():
        tail_copy().start()

    def zero_wait(p, c):
        row_copy(z_hbm, 0, p).wait()
        return c

    lax.fori_loop(pad0, pad0 + npad, zero_wait, 0)

    @pl.when(has_tail)
    def _():
        tail_copy().wait()


def _dispatch(meta, pos, h2, zrow):
    return pl.pallas_call(
        _dispatch_kernel,
        grid_spec=pltpu.PrefetchScalarGridSpec(
            num_scalar_prefetch=1,
            grid=(T // DISP_TM,),
            in_specs=[
                pl.BlockSpec((2, DISP_TM), lambda i, meta: (0, i), memory_space=pltpu.SMEM),
                pl.BlockSpec((DISP_TM, D), lambda i, meta: (i, 0)),
                pl.BlockSpec((MOE_TM, D), lambda i, meta: (0, 0)),
            ],
            out_specs=pl.BlockSpec(memory_space=pl.ANY),
            scratch_shapes=[pltpu.SemaphoreType.DMA],
        ),
        out_shape=jax.ShapeDtypeStruct((MOE_ROWS, D), F32),
        compiler_params=_cp(("arbitrary",)),
        name="dispatch",
    )(meta, pos, h2, zrow)


def _experts_kernel(meta_ref, xs_ref, wg_ref, wu_ref, wd_ref, ys_ref, wg_s, wu_s, wd_s):
    j = pl.program_id(0)
    live = j < meta_ref[META_NT, 0]
    e = meta_ref[META_TILE_E, j]
    e_prev = meta_ref[META_TILE_E, jnp.maximum(j - 1, 0)]

    @pl.when(live & ((j == 0) | (e != e_prev)))
    def _():
        wg_s[...] = wg_ref[...].astype(BF16)
        wu_s[...] = wu_ref[...].astype(BF16)
        wd_s[...] = wd_ref[...].astype(BF16)

    @pl.when(live)
    def _():
        x = xs_ref[...].astype(BF16)
        g = _dot(x, wg_s[...])
        u = _dot(x, wu_s[...])
        act = (g * _sigmoid(g)) * u
        ys_ref[...] = _dot(act.astype(BF16), wd_s[...])

    @pl.when(jnp.logical_not(live))
    def _():
        ys_ref[...] = jnp.zeros_like(ys_ref)


def _experts(meta, xs, w_gate_e, w_up_e, w_down_e, l):
    def tile(j, meta):
        return jnp.minimum(j, meta[META_NT, 0] - 1)

    def wmap(j, meta):
        return (l, meta[META_TILE_E, tile(j, meta)], 0, 0)

    return pl.pallas_call(
        _experts_kernel,
        grid_spec=pltpu.PrefetchScalarGridSpec(
            num_scalar_prefetch=1,
            grid=(MOE_NT,),
            in_specs=[
                pl.BlockSpec((MOE_TM, D), lambda j, meta: (tile(j, meta), 0)),
                pl.BlockSpec((None, None, D, D_EXP), wmap),
                pl.BlockSpec((None, None, D, D_EXP), wmap),
                pl.BlockSpec((None, None, D_EXP, D), wmap),
            ],
            out_specs=pl.BlockSpec((MOE_TM, D), lambda j, meta: (j, 0)),
            scratch_shapes=[pltpu.VMEM((D, D_EXP), BF16), pltpu.VMEM((D, D_EXP), BF16),
                            pltpu.VMEM((D_EXP, D), BF16)],
        ),
        out_shape=jax.ShapeDtypeStruct((MOE_ROWS, D), F32),
        compiler_params=_cp(("arbitrary",)),
        name="experts",
    )(meta, xs, w_gate_e, w_up_e, w_down_e)


COMB_TM = SEG


def _combine_kernel(pos_ref, w_ref, x1_ref, mod_ref, fg_ref, ys_hbm, oa_ref, ob_ref, buf, sem, *, final):
    i = pl.program_id(0)
    n = pl.num_programs(0) - 1
    n_ctx = T_CTX // COMB_TM

    for s in range(2):
        @pl.when((i < n) & (lax.rem(i, 2) == s))
        def _():
            for r in range(COMB_TM):
                for k in range(2):
                    pltpu.make_async_copy(ys_hbm.at[pl.ds(pos_ref[k, r], 1), :],
                                          buf.at[s, k, pl.ds(r, 1), :], sem.at[s]).start(priority=k)

    @pl.when(i > 0)
    def _():
        slot = lax.rem(i - 1, 2)
        for k in range(2):
            pltpu.make_async_copy(ys_hbm.at[pl.ds(0, COMB_TM), :], buf.at[slot, k], sem.at[slot]).wait()
        w = w_ref[...]
        y = w[:, 0:1] * buf[slot, 0] + w[:, 1:2] * buf[slot, 1]
        x = x1_ref[...] + mod_ref[5:6, :] * y
        if final:
            ms = jnp.mean(x * x, axis=-1, keepdims=True)
            x = x * lax.rsqrt(ms + EPS) * fg_ref[...]

        @pl.when(i - 1 < n_ctx)
        def _():
            oa_ref[...] = x

        @pl.when(i - 1 >= n_ctx)
        def _():
            ob_ref[...] = x


def _combine(pos, wts_t, x1, modseg, final_g, ys, l, final):
    n = T // COMB_TM
    n_ctx = T_CTX // COMB_TM

    def done(i):
        return jnp.maximum(i - 1, 0)

    return pl.pallas_call(
        functools.partial(_combine_kernel, final=final),
        grid=(n + 1,),
        in_specs=[
            pl.BlockSpec((2, COMB_TM), lambda i: (0, jnp.minimum(i, n - 1)), memory_space=pltpu.SMEM),
            pl.BlockSpec((COMB_TM, 2), lambda i: (done(i), 0)),
            pl.BlockSpec((COMB_TM, D), lambda i: (done(i), 0)),
            pl.BlockSpec((None, None, 8, D), lambda i: (l, done(i), 0, 0)),
            pl.BlockSpec((1, D), lambda i: (0, 0)),
            pl.BlockSpec(memory_space=pl.ANY),
        ],
        out_specs=[pl.BlockSpec((COMB_TM, D), lambda i: (jnp.minimum(done(i), n_ctx - 1), 0)),
                   pl.BlockSpec((COMB_TM, D), lambda i: (jnp.maximum(done(i) - n_ctx, 0), 0))],
        out_shape=[jax.ShapeDtypeStruct((T_CTX, D), F32), jax.ShapeDtypeStruct((T_LAT, D), F32)],
        scratch_shapes=[pltpu.VMEM((2, 2, COMB_TM, D), F32), pltpu.SemaphoreType.DMA((2,))],
        compiler_params=_cp(("arbitrary",)),
        name="combine",
    )(pos, wts_t, x1, modseg, final_g.reshape(1, D), ys)


def _rope_tables():
    n = DEC_SEQ
    pos_row = np.repeat(np.arange(n // GRID_W, dtype=np.float32), GRID_W)
    pos_col = np.tile(np.arange(GRID_W, dtype=np.float32), n // GRID_W)
    half = HD // 2
    inv_freq = jnp.asarray(ROPE_THETA, F32) ** (-jnp.arange(0, half, 2, dtype=F32) / half)
    ang = jnp.concatenate([jnp.asarray(pos_row)[:, None] * inv_freq,
                           jnp.asarray(pos_col)[:, None] * inv_freq], axis=-1)
    cos, sin = jnp.cos(ang), jnp.sin(ang)
    cos128 = jnp.tile(cos, (1, 4))
    sin128 = jnp.tile(jnp.concatenate([-sin, sin], axis=-1), (1, 2))
    return cos128, sin128


def _head_mean_matrix():
    idx = np.arange(2 * HD)
    same = (idx[:, None] // HD) == (idx[None, :] // HD)
    return jnp.asarray(same.astype(np.float32) / HD, BF16)


_SEG_ROWS = np.array([0] * (T_CTX // SEG) + [1 + b for b in range(DEC_BATCH) for _ in range(DEC_SEQ // SEG)])


def kernel(x_prompt, x_sample, cache_k, cache_v, state_rec, c, c_ctx, w_mod, b_mod, norm1_g, norm2_g, w_in, conv_w, conv_b, rg_wa, rg_ba, rg_wx, rg_bx, rg_lambda, q_norm_g, k_norm_g, w_rec_out, w_att_out, w_out, w_router, router_bias, w_gate_e, w_up_e, w_down_e, final_g):
    xa, xb = x_prompt.reshape(T_CTX, D), x_sample.reshape(T_LAT, D)

    cvecs = jnp.concatenate([c_ctx[None, :], c, jnp.zeros((3, D), F32)], axis=0)
    mods = _mods(cvecs, w_mod, b_mod).reshape(DEPTH, 8, 6, D)
    modseg = jnp.pad(mods[:, _SEG_ROWS], ((0, 0), (0, 0), (0, 2), (0, 0)))

    cos128, sin128 = _rope_tables()
    bd = _head_mean_matrix()
    qg128 = jnp.tile(q_norm_g, (1, 2)).reshape(DEPTH, 1, 2 * HD)
    kg128 = jnp.tile(k_norm_g, (1, 2)).reshape(DEPTH, 1, 2 * HD)
    wg = jnp.concatenate([rg_wa[:, 0], rg_wx[:, 0], rg_wa[:, 1], rg_wx[:, 1]], axis=-1)
    pvec = jnp.stack([rg_ba[:, 0], rg_bx[:, 0], rg_ba[:, 1], rg_bx[:, 1],
                      rg_lambda[:, 0], rg_lambda[:, 1], conv_b, jnp.zeros_like(conv_b)], axis=1)
    wrt = w_router.T
    rbias = router_bias.reshape(N_EXP, 1)
    zrow = jnp.zeros((MOE_TM, D), F32)

    new_k, new_v, new_s = [], [], []
    for l in range(DEPTH):
        proj = _inproj(xa, xb, modseg, norm1_g, w_in, l)
        h0 = jnp.concatenate([jnp.zeros((T_CTX // UNIT, 2, D), F32), state_rec[:, l]], axis=0)
        yrec, stf, stb = _rec(proj, conv_w, pvec, wg, h0, l)
        qc, kc, vc = _qkv(proj, qg128, kg128, cos128, sin128, bd, l, latent=False)
        ql, kl, vl = _qkv(proj, qg128, kg128, cos128, sin128, bd, l, latent=True)
        o_ctx = _attn_ctx(qc, kc, vc)
        o_lat = _attn_lat(ql, cache_k, cache_v, kl, vl, l)
        x1, h2, idx, wts = _merge(yrec, o_ctx, o_lat, proj, xa, xb, modseg, norm2_g,
                                  w_rec_out, w_att_out, w_out, wrt, rbias, l)
        pos, meta = _route_pos(idx)
        xs = _dispatch(meta, pos, h2, zrow)
        ys = _experts(meta, xs, w_gate_e, w_up_e, w_down_e, l)
        xa, xb = _combine(pos, wts.T, x1, modseg, final_g, ys, l, final=(l == DEPTH - 1))
        new_k.append(kc)
        new_v.append(vc)
        n_cu = T_CTX // UNIT
        spu = UNIT // SEQ
        hf_last = stf[:n_cu].reshape(n_cu, spu, 2, D)[:, :, 1].reshape(BATCH, D)
        hb_first = stb[:n_cu].reshape(n_cu, spu, 2, D)[:, :, 0].reshape(BATCH, D)
        new_s.append(jnp.stack([hf_last, hb_first], axis=1))

    y_prompt = xa.reshape(BATCH, SEQ, D)
    y_sample = xb.reshape(DEC_BATCH, DEC_SEQ, D)
    return (y_prompt, y_sample, jnp.stack(new_k, axis=1), jnp.stack(new_v, axis=1), jnp.stack(new_s, axis=1))
```

```python
import functools

import numpy as np
import jax
import jax.numpy as jnp
from jax import lax
from jax.experimental import pallas as pl
from jax.experimental.pallas import tpu as pltpu

F32 = jnp.float32
BF16 = jnp.bfloat16

D = 1024
BATCH = 16
SEQ = 256
DEPTH = 2
DEC_BATCH = 4
DEC_SEQ = 1024
PAST = 256
GRID_W = 64
N_HEADS = 16
N_KV = 4
HD = 64
N_RG_BLK = 8
RG_BLK = 128
RG_C = 8.0
N_EXP = 16
D_EXP = 512
ROPE_THETA = 10000.0
EPS = 1e-6
P_IN = 5632
TINY = float(np.finfo(np.float32).tiny)
NEG_LOG2E = -float(np.log2(np.e))

T_CTX = BATCH * SEQ
T_LAT = DEC_BATCH * DEC_SEQ
T = T_CTX + T_LAT
SEG = 256
N_SEG = T // SEG
UNIT = 1024
N_UNIT = T // UNIT
CHUNK = UNIT // 8
CSTRIDE = CHUNK + 8

VMEM_LIMIT = 56 * 1024 * 1024


def _cp(sem):
    return pltpu.CompilerParams(dimension_semantics=sem, vmem_limit_bytes=VMEM_LIMIT)


def _split(x):
    hi = x.astype(BF16)
    lo = (x - hi.astype(F32)).astype(BF16)
    return hi, lo


def _sigmoid(x):
    return 0.5 * jnp.tanh(0.5 * x) + 0.5


def _dot(a, b):
    return jnp.dot(a, b, preferred_element_type=F32)


def _dot_nt(a, b):
    return lax.dot_general(a, b, (((1,), (1,)), ((), ())), preferred_element_type=F32)


def _mods_kernel(c_ref, w_ref, b_ref, o_ref):
    c = c_ref[...]
    s = c * jax.nn.sigmoid(c)
    s_hi, s_lo = _split(s)
    w_hi, w_lo = _split(w_ref[...])
    o_ref[...] = _dot(s_hi, w_hi) + _dot(s_hi, w_lo) + _dot(s_lo, w_hi) + b_ref[...]


def _mods(cvecs, w_mod, b_mod):
    tn = 1536
    return pl.pallas_call(
        _mods_kernel,
        grid=(DEPTH, 6 * D // tn),
        in_specs=[
            pl.BlockSpec((8, D), lambda l, j: (0, 0)),
            pl.BlockSpec((None, D, tn), lambda l, j: (l, 0, j)),
            pl.BlockSpec((None, 1, tn), lambda l, j: (l, 0, j)),
        ],
        out_specs=pl.BlockSpec((None, 8, tn), lambda l, j: (l, 0, j)),
        out_shape=jax.ShapeDtypeStruct((DEPTH, 8, 6 * D), F32),
        compiler_params=_cp(("arbitrary", "arbitrary")),
        name="mods",
    )(cvecs, w_mod, b_mod.reshape(DEPTH, 1, 6 * D))


def _norm_mod(x, g, shift, scale):
    ms = jnp.mean(x * x, axis=-1, keepdims=True)
    return x * lax.rsqrt(ms + EPS) * g * (1.0 + scale) + shift


def _two_part_specs(tm, n_ctx):
    return [pl.BlockSpec((tm, D), lambda i, *_: (jnp.minimum(i, n_ctx - 1), 0)),
            pl.BlockSpec((tm, D), lambda i, *_: (jnp.maximum(i - n_ctx, 0), 0))]


def _inproj_kernel(xa_ref, xb_ref, mod_ref, g_ref, w_ref, o_ref, h_ref, *, tm):
    def prologue(x_ref):
        def seg(s, carry):
            r0 = pl.multiple_of(s * SEG, SEG)
            m = mod_ref[s]
            h = _norm_mod(x_ref[pl.ds(r0, SEG), :], g_ref[...], m[0:1, :], m[1:2, :])
            h_ref[pl.ds(r0, SEG), :] = h.astype(BF16)
            return carry
        lax.fori_loop(0, tm // SEG, seg, 0)

    first = pl.program_id(1) == 0
    is_ctx = pl.program_id(0) < T_CTX // tm

    @pl.when(first & is_ctx)
    def _():
        prologue(xa_ref)

    @pl.when(first & jnp.logical_not(is_ctx))
    def _():
        prologue(xb_ref)

    o_ref[...] = _dot(h_ref[...], w_ref[...].astype(BF16)).astype(BF16)


def _inproj(xa, xb, modseg, norm_g, w_in, l):
    tm, tn = 2048, 512
    return pl.pallas_call(
        functools.partial(_inproj_kernel, tm=tm),
        grid=(T // tm, P_IN // tn),
        in_specs=_two_part_specs(tm, T_CTX // tm) + [
            pl.BlockSpec((None, tm // SEG, 8, D), lambda i, j: (l, i, 0, 0)),
            pl.BlockSpec((None, 1, D), lambda i, j: (l, 0, 0)),
            pl.BlockSpec((None, D, tn), lambda i, j: (l, 0, j)),
        ],
        out_specs=pl.BlockSpec((tm, tn), lambda i, j: (i, j)),
        out_shape=jax.ShapeDtypeStruct((T, P_IN), BF16),
        scratch_shapes=[pltpu.VMEM((tm, D), BF16)],
        compiler_params=_cp(("arbitrary", "arbitrary")),
        name="inproj",
    )(xa, xb, modseg, norm_g.reshape(DEPTH, 1, D), w_in)


REC_CW = 512
HALO = 16


def _rec_kernel(xr_ref, gate_ref, cw_ref, pv_ref, wg_ref, h0_ref,
                y_ref, stf_ref, stb_ref,
                af_ref, bf_ref, ab_ref, bb_ref, wgh_ref):
    u = pl.program_id(0)
    is_ctx = u < (T_CTX // UNIT)
    seq_len = jnp.where(is_ctx, SEQ, DEC_SEQ)
    nblk = REC_CW // RG_BLK

    pv = pv_ref[...]
    cwts = cw_ref[...]
    conv_b = pv[6:7, :]

    def softplus_neg(lam):
        z = -lam
        return jnp.maximum(z, 0.0) + jnp.log1p(jnp.exp(-jnp.abs(z)))

    c4s = tuple((0.5 * RG_C) * softplus_neg(pv[4 + d:5 + d, :]) for d in range(2))
    pv_h = 0.5 * pv
    for n in range(nblk):
        wgh_ref[n] = (0.5 * wg_ref[n]).astype(BF16)
    a_refs = (af_ref, ab_ref)
    b_refs = (bf_ref, bb_ref)

    def gates(ci, carry):
        base = pl.multiple_of(ci * CHUNK, CHUNK)
        lo = pl.multiple_of(jnp.maximum(base - HALO, 0), HALO)
        hi = pl.multiple_of(jnp.minimum(base + CHUNK, UNIT - HALO), HALO)
        main = xr_ref[pl.ds(base, CHUNK), :].astype(F32)
        win = jnp.concatenate([xr_ref[pl.ds(lo, HALO), :].astype(F32), main,
                               xr_ref[pl.ds(hi, HALO), :].astype(F32)], axis=0)
        n_win = CHUNK + 2 * HALO
        row8 = lax.broadcasted_iota(jnp.int32, (8, 1), 0)
        tl_head = jnp.bitwise_and(base + row8, seq_len - 1)
        tl_tail = jnp.bitwise_and(base + (CHUNK - 8) + row8, seq_len - 1)

        def tap(shift, head_ok=None, tail_ok=None):
            x = pltpu.roll(win, shift, 0)[HALO:HALO + CHUNK]
            if head_ok is not None:
                return jnp.concatenate([jnp.where(head_ok, x[:8], 0.0), x[8:]], axis=0)
            return jnp.concatenate([x[:CHUNK - 8], jnp.where(tail_ok, x[CHUNK - 8:], 0.0)], axis=0)

        xm2 = tap(2, head_ok=tl_head >= 2)
        xm1 = tap(1, head_ok=tl_head >= 1)
        xp1 = tap(n_win - 1, tail_ok=tl_tail <= seq_len - 2)
        xc = conv_b + xm2 * cwts[0:1, :]
        xc = xc + xm1 * cwts[1:2, :]
        xc = xc + main * cwts[2:3, :]
        xc = xc + xp1 * cwts[3:4, :]
        for n in range(nblk):
            ls = slice(n * RG_BLK, (n + 1) * RG_BLK)
            xn = xc[:, ls]
            hx = 0.5 * xn
            pre_h = _dot(xn.astype(BF16), wgh_ref[n])
            for d in range(2):
                th_r = jnp.tanh(pre_h[:, (2 * d) * RG_BLK:(2 * d + 1) * RG_BLK] + pv_h[2 * d:2 * d + 1, ls])
                th_i = jnp.tanh(pre_h[:, (2 * d + 1) * RG_BLK:(2 * d + 2) * RG_BLK] + pv_h[2 * d + 1:2 * d + 2, ls])
                c4 = c4s[d][:, ls]
                nla = c4 * th_r + c4
                a = jnp.exp2(nla * NEG_LOG2E)
                s = jnp.tanh(nla) * (a * a + 1.0)
                inp = (s * lax.rsqrt(jnp.maximum(s, TINY))) * (hx * th_i + hx)
                sbase = pl.multiple_of(ci * CSTRIDE, 8)
                a_refs[d][n, pl.ds(sbase, CHUNK), :] = a
                b_refs[d][n, pl.ds(sbase, CHUNK), :] = inp
        return carry

    lax.fori_loop(0, 8, gates, 0)

    zeros = [jnp.zeros((8, RG_BLK), F32)] * nblk
    ones = [jnp.ones((8, RG_BLK), F32)] * nblk
    hf, pf, hb, pb = list(zeros), list(ones), list(zeros), list(ones)
    for r in range(CHUNK):
        rows_f = pl.ds(r, 8, stride=CSTRIDE)
        rows_b = pl.ds(CHUNK - 1 - r, 8, stride=CSTRIDE)
        for n in range(nblk):
            a = af_ref[n, rows_f, :]
            hf[n] = a * hf[n] + bf_ref[n, rows_f, :]
            pf[n] = a * pf[n]
            bf_ref[n, rows_f, :] = hf[n]
            af_ref[n, rows_f, :] = pf[n]
            a = ab_ref[n, rows_b, :]
            hb[n] = a * hb[n] + bb_ref[n, rows_b, :]
            pb[n] = a * pb[n]
            bb_ref[n, rows_b, :] = hb[n]
            ab_ref[n, rows_b, :] = pb[n]
    hf, pf, hb, pb = (jnp.concatenate(v, axis=-1) for v in (hf, pf, hb, pb))

    cps = jnp.where(is_ctx, SEQ // CHUNK, DEC_SEQ // CHUNK)
    h0f = h0_ref[0:1, :]
    h0b = h0_ref[1:2, :]
    cf = [h0f]
    for c in range(1, 8):
        chain = hf[c - 1:c, :] + pf[c - 1:c, :] * cf[c - 1]
        cf.append(jnp.where(jnp.bitwise_and(c, cps - 1) == 0, h0f, chain))
    cb = [None] * 8
    cb[7] = h0b
    for c in range(6, -1, -1):
        chain = hb[c + 1:c + 2, :] + pb[c + 1:c + 2, :] * cb[c + 1]
        cb[c] = jnp.where(jnp.bitwise_and(c, cps - 1) == cps - 1, h0b, chain)
    carry_f = jnp.concatenate(cf, axis=0)
    carry_b = jnp.concatenate(cb, axis=0)
    stf_ref[...] = hf + pf * carry_f
    stb_ref[...] = hb + pb * carry_b

    for ci in range(8):
        rows = pl.ds(ci * CHUNK, CHUNK)
        srows = pl.ds(ci * CSTRIDE, CHUNK)
        for n in range(nblk):
            ls = slice(n * RG_BLK, (n + 1) * RG_BLK)
            h_f = bf_ref[n, srows, :] + af_ref[n, srows, :] * carry_f[ci:ci + 1, ls]
            h_b = bb_ref[n, srows, :] + ab_ref[n, srows, :] * carry_b[ci:ci + 1, ls]
            g = gate_ref[rows, ls].astype(F32)
            y_ref[rows, ls] = ((h_f + h_b) * jax.nn.gelu(g, approximate=True)).astype(BF16)


def _rec(proj, conv_w, pvec, wg, h0, l):
    ncb = D // REC_CW
    return pl.pallas_call(
        _rec_kernel,
        grid=(N_UNIT, ncb),
        in_specs=[
            pl.BlockSpec((UNIT, REC_CW), lambda u, c: (u, c)),
            pl.BlockSpec((UNIT, REC_CW), lambda u, c: (u, ncb + c)),
            pl.BlockSpec((None, 4, REC_CW), lambda u, c: (l, 0, c)),
            pl.BlockSpec((None, 8, REC_CW), lambda u, c: (l, 0, c)),
            pl.BlockSpec((None, REC_CW // RG_BLK, RG_BLK, 4 * RG_BLK), lambda u, c: (l, c, 0, 0)),
            pl.BlockSpec((None, 2, REC_CW), lambda u, c: (u, 0, c)),
        ],
        out_specs=[
            pl.BlockSpec((UNIT, REC_CW), lambda u, c: (u, c)),
            pl.BlockSpec((None, 8, REC_CW), lambda u, c: (u, 0, c)),
            pl.BlockSpec((None, 8, REC_CW), lambda u, c: (u, 0, c)),
        ],
        out_shape=[
            jax.ShapeDtypeStruct((T, D), BF16),
            jax.ShapeDtypeStruct((N_UNIT, 8, D), F32),
            jax.ShapeDtypeStruct((N_UNIT, 8, D), F32),
        ],
        scratch_shapes=[pltpu.VMEM((REC_CW // RG_BLK, 8 * CSTRIDE, RG_BLK), F32)] * 4
        + [pltpu.VMEM((REC_CW // RG_BLK, RG_BLK, 4 * RG_BLK), BF16)],
        compiler_params=_cp(("arbitrary", "arbitrary")),
        name="rec",
    )(proj, proj, conv_w, pvec, wg, h0)


def _head_norm(x, g128, bd):
    hi, lo = _split(x * x)
    ms = _dot(hi, bd) + _dot(lo, bd)
    return x * lax.rsqrt(ms + EPS) * g128


def _rope(x, cos, sin_signed):
    lane = lax.broadcasted_iota(jnp.int32, x.shape, 1)
    first_half = jnp.bitwise_and(lane, HD - 1) < HD // 2
    partner = jnp.where(first_half, pltpu.roll(x, 2 * HD - HD // 2, 1), pltpu.roll(x, HD // 2, 1))
    return x * cos + partner * sin_signed


def _qkv_kernel(q_ref, k_ref, v_ref, qg_ref, kg_ref, cos_ref, sin_ref, bd_ref,
                qo_ref, ko_ref, vo_ref, *, rope):
    bd = bd_ref[...]
    scale = HD ** -0.5 * float(np.log2(np.e))
    for j in range(N_HEADS // 2):
        x = _head_norm(q_ref[:, 2 * HD * j:2 * HD * (j + 1)].astype(F32), qg_ref[...], bd)
        if rope:
            x = _rope(x, cos_ref[...], sin_ref[...])
        x = x * scale
        qo_ref[2 * j] = x[:, :HD].astype(qo_ref.dtype)
        qo_ref[2 * j + 1] = x[:, HD:].astype(qo_ref.dtype)
    for j in range(N_KV // 2):
        x = _head_norm(k_ref[:, 2 * HD * j:2 * HD * (j + 1)].astype(F32), kg_ref[...], bd)
        if rope:
            x = _rope(x, cos_ref[...], sin_ref[...])
        ko_ref[2 * j] = x[:, :HD].astype(ko_ref.dtype)
        ko_ref[2 * j + 1] = x[:, HD:].astype(ko_ref.dtype)
        v = v_ref[:, 2 * HD * j:2 * HD * (j + 1)].astype(F32)
        vo_ref[2 * j] = v[:, :HD].astype(vo_ref.dtype)
        vo_ref[2 * j + 1] = v[:, HD:].astype(vo_ref.dtype)


def _qkv(proj, qg128, kg128, cos128, sin128, bd, l, latent):
    tm = SEG
    n = T_LAT // tm if latent else T_CTX // tm
    roff = T_CTX // tm if latent else 0
    per_seq = DEC_SEQ // tm
    if latent:
        kv_shape = (DEC_BATCH, N_KV, DEC_SEQ, HD)
        kv_spec = pl.BlockSpec((None, N_KV, tm, HD), lambda i: (i // per_seq, 0, i % per_seq, 0))
        kv_dtype = BF16
        tab_map = lambda i: (i % per_seq, 0)
    else:
        kv_shape = (BATCH, N_KV, SEQ, HD)
        kv_spec = pl.BlockSpec((None, N_KV, tm, HD), lambda i: (i, 0, 0, 0))
        kv_dtype = F32
        tab_map = lambda i: (0, 0)
    return pl.pallas_call(
        functools.partial(_qkv_kernel, rope=latent),
        grid=(n,),
        in_specs=[
            pl.BlockSpec((tm, D), lambda i: (roff + i, 2)),
            pl.BlockSpec((tm, N_KV * HD), lambda i: (roff + i, 3 * D // (N_KV * HD))),
            pl.BlockSpec((tm, N_KV * HD), lambda i: (roff + i, 3 * D // (N_KV * HD) + 1)),
            pl.BlockSpec((None, 1, 2 * HD), lambda i: (l, 0, 0)),
            pl.BlockSpec((None, 1, 2 * HD), lambda i: (l, 0, 0)),
            pl.BlockSpec((tm, 2 * HD), tab_map),
            pl.BlockSpec((tm, 2 * HD), tab_map),
            pl.BlockSpec((2 * HD, 2 * HD), lambda i: (0, 0)),
        ],
        out_specs=[
            pl.BlockSpec((N_HEADS, tm, HD), lambda i: (0, i, 0)),
            kv_spec,
            kv_spec,
        ],
        out_shape=[
            jax.ShapeDtypeStruct((N_HEADS, n * tm, HD), BF16),
            jax.ShapeDtypeStruct(kv_shape, kv_dtype),
            jax.ShapeDtypeStruct(kv_shape, kv_dtype),
        ],
        compiler_params=_cp(("arbitrary",)),
        name="qkv_lat" if latent else "qkv_ctx",
    )(proj, proj, proj, qg128, kg128, cos128, sin128, bd)


def _with_ones(v):
    return jnp.concatenate([v, jnp.ones_like(v)], axis=-1)


def _softmax_pv(q, k, v_ext):
    s = _dot_nt(q, k)
    m = jnp.max(s, axis=-1, keepdims=True)
    p = jnp.exp2(s - m).astype(BF16)
    r = _dot(p, v_ext)
    return r[:, :HD] / r[:, HD:HD + 1]


def _attend_heads(q_ref, k, v_ext):
    return jnp.concatenate([_softmax_pv(q_ref[h], k, v_ext) for h in range(N_HEADS // N_KV)], axis=-1)


def _attn_ctx_kernel(q_ref, k_ref, v_ref, o_ref):
    o = _attend_heads(q_ref, k_ref[...].astype(BF16), _with_ones(v_ref[...].astype(BF16)))
    o_ref[...] = o.astype(BF16)


def _attn_ctx(qh, kc, vc):
    g = N_HEADS // N_KV
    return pl.pallas_call(
        _attn_ctx_kernel,
        grid=(BATCH, N_KV),
        in_specs=[
            pl.BlockSpec((g, SEQ, HD), lambda b, h: (h, b, 0)),
            pl.BlockSpec((None, None, SEQ, HD), lambda b, h: (b, h, 0, 0)),
            pl.BlockSpec((None, None, SEQ, HD), lambda b, h: (b, h, 0, 0)),
        ],
        out_specs=pl.BlockSpec((SEQ, g * HD), lambda b, h: (b, h)),
        out_shape=jax.ShapeDtypeStruct((T_CTX, D), BF16),
        compiler_params=_cp(("arbitrary", "arbitrary")),
        name="attn_ctx",
    )(qh, kc, vc)


ATT_TQ = 256


def _attn_lat_kernel(q_ref, pk_ref, pv_ref, k_ref, v_ref, o_ref, k_s, v_s):
    k_s[0:PAST, :] = pk_ref[...].astype(BF16)
    k_s[PAST:, :] = k_ref[...]
    v_s[0:PAST, :] = _with_ones(pv_ref[...].astype(BF16))
    v_s[PAST:, :] = _with_ones(v_ref[...])

    def q_tile(qi, carry):
        rows = pl.ds(pl.multiple_of(qi * ATT_TQ, ATT_TQ), ATT_TQ)
        o = jnp.concatenate([_softmax_pv(q_ref[h, rows, :], k_s[...], v_s[...])
                             for h in range(N_HEADS // N_KV)], axis=-1)
        o_ref[rows, :] = o.astype(BF16)
        return carry

    lax.fori_loop(0, DEC_SEQ // ATT_TQ, q_tile, 0)


def _attn_lat(qh, cache_k, cache_v, kr, vr, l):
    g = N_HEADS // N_KV
    return pl.pallas_call(
        _attn_lat_kernel,
        grid=(DEC_BATCH, N_KV),
        in_specs=[
            pl.BlockSpec((g, DEC_SEQ, HD), lambda b, h: (h, b, 0)),
            pl.BlockSpec((None, None, None, PAST, HD), lambda b, h: (b, l, h, 0, 0)),
            pl.BlockSpec((None, None, None, PAST, HD), lambda b, h: (b, l, h, 0, 0)),
            pl.BlockSpec((None, None, DEC_SEQ, HD), lambda b, h: (b, h, 0, 0)),
            pl.BlockSpec((None, None, DEC_SEQ, HD), lambda b, h: (b, h, 0, 0)),
        ],
        out_specs=pl.BlockSpec((DEC_SEQ, g * HD), lambda b, h: (b, h)),
        out_shape=jax.ShapeDtypeStruct((T_LAT, D), BF16),
        scratch_shapes=[pltpu.VMEM((PAST + DEC_SEQ, HD), BF16), pltpu.VMEM((PAST + DEC_SEQ, 2 * HD), BF16)],
        compiler_params=_cp(("arbitrary", "arbitrary")),
        name="attn_lat",
    )(qh, cache_k, cache_v, kr, vr)


MERGE_TM = 512


def _route(lt, bias):
    rows = [lt[e:e + 1, :] for e in range(N_EXP)]
    m = rows[0]
    for e in range(1, N_EXP):
        m = jnp.maximum(m, rows[e])
    ex = [jnp.exp(r - m) for r in rows]
    z = ex[0]
    for e in range(1, N_EXP):
        z = z + ex[e]
    probs = [x / z for x in ex]
    sel = [probs[e] + bias[e:e + 1, :] for e in range(N_EXP)]

    def top2_sum(v):
        a, b = jnp.maximum(v[0], v[1]), jnp.minimum(v[0], v[1])
        c, d = jnp.maximum(v[2], v[3]), jnp.minimum(v[2], v[3])
        return jnp.maximum(a, c) + jnp.maximum(jnp.minimum(a, c), jnp.maximum(b, d))

    scores = [top2_sum(sel[4 * g:4 * g + 4]) for g in range(4)]
    best = jnp.zeros_like(scores[0], dtype=jnp.int32)
    best_s = scores[0]
    for g in range(1, 4):
        take = scores[g] > best_s
        best = jnp.where(take, g, best)
        best_s = jnp.where(take, scores[g], best_s)
    cs, cp = [], []
    for j in range(4):
        s_j, p_j = sel[j], probs[j]
        for g in range(1, 4):
            s_j = jnp.where(best == g, sel[4 * g + j], s_j)
            p_j = jnp.where(best == g, probs[4 * g + j], p_j)
        cs.append(s_j)
        cp.append(p_j)
    neg = jnp.full_like(cs[0], -jnp.inf)

    def argmax4(v):
        bi = jnp.zeros_like(best)
        bv = v[0]
        for j in range(1, 4):
            take = v[j] > bv
            bi = jnp.where(take, j, bi)
            bv = jnp.where(take, v[j], bv)
        return bi

    def pick(v, idx):
        out = v[0]
        for j in range(1, 4):
            out = jnp.where(idx == j, v[j], out)
        return out

    i1 = argmax4(cs)
    cs2 = [jnp.where(i1 == j, neg, cs[j]) for j in range(4)]
    i2 = argmax4(cs2)
    i2 = jnp.where((i2 == 0) & (i1 == 0), 1, i2)
    w1, w2 = pick(cp, i1), pick(cp, i2)
    den = w1 + w2
    return best * 4 + i1, best * 4 + i2, w1 / den, w2 / den


def _merge_kernel(yrec_ref, oa_ref, ob_ref, gr0_ref, gr1_ref, ga0_ref, ga1_ref, xa_ref, xb_ref, mod_ref, g2_ref,
                  wrec_ref, watt_ref, wout_ref, wrt_ref, rb_ref,
                  x1_ref, h2_ref, idx_ref, wts_ref,
                  wrec_s, watt_s, wout_s):
    @pl.when(pl.program_id(0) == 0)
    def _():
        wrec_s[...] = wrec_ref[...].astype(BF16)
        watt_s[...] = watt_ref[...].astype(BF16)
        wout_s[...] = wout_ref[...].astype(BF16)

    is_ctx = pl.program_id(0) < T_CTX // MERGE_TM
    args = (yrec_ref, gr0_ref, gr1_ref, ga0_ref, ga1_ref, mod_ref, g2_ref, wrt_ref, rb_ref,
            x1_ref, h2_ref, idx_ref, wts_ref, wrec_s, watt_s, wout_s)

    @pl.when(is_ctx)
    def _():
        _merge_body(oa_ref, xa_ref, *args)

    @pl.when(jnp.logical_not(is_ctx))
    def _():
        _merge_body(ob_ref, xb_ref, *args)


def _merge_body(oatt_ref, x_ref, yrec_ref, gr0_ref, gr1_ref, ga0_ref, ga1_ref, mod_ref, g2_ref, wrt_ref, rb_ref,
                x1_ref, h2_ref, idx_ref, wts_ref, wrec_s, watt_s, wout_s):
    half = D // 2
    b_rec = _dot(yrec_ref[...], wrec_s[...])
    b_att = _dot(oatt_ref[...], watt_s[...])
    m0 = _sigmoid(gr0_ref[...].astype(F32)) * b_rec[:, :half] + _sigmoid(ga0_ref[...].astype(F32)) * b_att[:, :half]
    m1 = _sigmoid(gr1_ref[...].astype(F32)) * b_rec[:, half:] + _sigmoid(ga1_ref[...].astype(F32)) * b_att[:, half:]
    merged = jnp.concatenate([m0, m1], axis=-1).astype(BF16)
    out = _dot(merged, wout_s[...])

    hs = []
    for s in range(MERGE_TM // SEG):
        rows = slice(s * SEG, (s + 1) * SEG)
        m = mod_ref[s]
        x1 = x_ref[rows, :] + m[2:3, :] * out[rows, :]
        x1_ref[rows, :] = x1
        h2 = _norm_mod(x1, g2_ref[...], m[3:4, :], m[4:5, :])
        h2_ref[rows, :] = h2
        hs.append(h2)
    h2 = jnp.concatenate(hs, axis=0)

    h_hi, h_lo = _split(h2)
    w_hi, w_lo = _split(wrt_ref[...])
    lt = _dot_nt(w_hi, h_hi) + _dot_nt(w_hi, h_lo) + _dot_nt(w_lo, h_hi)
    e1, e2, w1, w2 = _route(lt, rb_ref[...])
    idx_ref[...] = jnp.concatenate([e1, e2], axis=0)
    wts_ref[...] = jnp.concatenate([w1, w2], axis=0)


def _merge(yrec, o_ctx, o_lat, proj, xa, xb, modseg, norm2_g, w_rec_out, w_att_out, w_out, wrt, rbias, l):
    tm = MERGE_TM
    half = D // 2
    gcol = (3 * D + 2 * N_KV * HD) // half
    wspec = pl.BlockSpec((None, D, D), lambda i: (l, 0, 0))
    return pl.pallas_call(
        _merge_kernel,
        grid=(T // tm,),
        in_specs=[pl.BlockSpec((tm, D), lambda i: (i, 0))] + _two_part_specs(tm, T_CTX // tm) + [
            pl.BlockSpec((tm, half), lambda i: (i, gcol)),
            pl.BlockSpec((tm, half), lambda i: (i, gcol + 1)),
            pl.BlockSpec((tm, half), lambda i: (i, gcol + 2)),
            pl.BlockSpec((tm, half), lambda i: (i, gcol + 3)),
        ] + _two_part_specs(tm, T_CTX // tm) + [
            pl.BlockSpec((None, tm // SEG, 8, D), lambda i: (l, i, 0, 0)),
            pl.BlockSpec((None, 1, D), lambda i: (l, 0, 0)),
            wspec, wspec, wspec,
            pl.BlockSpec((N_EXP, D), lambda i: (0, 0)),
            pl.BlockSpec((N_EXP, 1), lambda i: (0, 0)),
        ],
        out_specs=[
            pl.BlockSpec((tm, D), lambda i: (i, 0)),
            pl.BlockSpec((tm, D), lambda i: (i, 0)),
            pl.BlockSpec((2, tm), lambda i: (0, i)),
            pl.BlockSpec((2, tm), lambda i: (0, i)),
        ],
        out_shape=[
            jax.ShapeDtypeStruct((T, D), F32),
            jax.ShapeDtypeStruct((T, D), F32),
            jax.ShapeDtypeStruct((2, T), jnp.int32),
            jax.ShapeDtypeStruct((2, T), F32),
        ],
        scratch_shapes=[pltpu.VMEM((D, D), BF16)] * 3,
        compiler_params=_cp(("arbitrary",)),
        name="merge",
    )(yrec, o_ctx, o_lat, proj, proj, proj, proj, xa, xb, modseg, norm2_g.reshape(DEPTH, 1, D),
      w_rec_out, w_att_out, w_out, wrt, rbias)


MOE_TM = 256
MOE_NT = 2 * T // MOE_TM + N_EXP
MOE_ROWS = MOE_NT * MOE_TM
META_TILE_E, META_CNT, META_OFF, META_END, META_NT, META_NEXT_E = 0, 1, 2, 3, 4, 5


def _pos_kernel(idx_ref, pos_ref, meta_ref):
    shift = MOE_TM.bit_length() - 1
    idx = idx_ref[...]
    eid = lax.broadcasted_iota(jnp.int32, (N_EXP, T), 0)
    m0 = eid == idx[0:1, :]
    m1 = eid == idx[1:2, :]
    member = jnp.where(m0 | m1, 1.0, 0.0)
    cnt = jnp.sum(member, axis=1, keepdims=True).astype(jnp.int32)
    ntile = jnp.right_shift(cnt + (MOE_TM - 1), shift)
    offs, acc = [], jnp.zeros((1, 1), jnp.int32)
    for e in range(N_EXP):
        offs.append(acc)
        acc = acc + ntile[e:e + 1, :]
    off_t = jnp.concatenate(offs, axis=0)
    end_t = off_t + ntile

    blk = 256
    r_i = lax.broadcasted_iota(jnp.int32, (blk, blk), 0)
    c_i = lax.broadcasted_iota(jnp.int32, (blk, blk), 1)
    upper = jnp.where(r_i <= c_i, 1.0, 0.0).astype(BF16)
    run = (off_t * MOE_TM).astype(F32)
    for j in range(T // blk):
        ls = slice(j * blk, (j + 1) * blk)
        mb = member[:, ls]
        inc = _dot(mb.astype(BF16), upper)
        dest = run + inc - mb
        pos_ref[0:1, ls] = jnp.sum(jnp.where(m0[:, ls], dest, 0.0), axis=0, keepdims=True).astype(jnp.int32)
        pos_ref[1:2, ls] = jnp.sum(jnp.where(m1[:, ls], dest, 0.0), axis=0, keepdims=True).astype(jnp.int32)
        run = run + inc[:, blk - 1:blk]

    lane = lax.broadcasted_iota(jnp.int32, (1, 128), 1)
    zero = jnp.zeros((1, 128), jnp.int32)
    tile_e, cnt_row, off_row, end_row = zero, zero, zero, zero
    for e in range(N_EXP):
        tile_e = tile_e + jnp.where(lane >= end_t[e:e + 1, :], 1, 0)
        here = lane == e
        cnt_row = jnp.where(here, cnt[e:e + 1, :], cnt_row)
        off_row = jnp.where(here, off_t[e:e + 1, :] * MOE_TM, off_row)
        end_row = jnp.where(here, end_t[e:e + 1, :] * MOE_TM, end_row)
    tile_e = jnp.minimum(tile_e, N_EXP - 1)
    nt_row = zero + acc
    next_row = zero
    nxt = jnp.full((1, 1), -1, jnp.int32)
    for e in reversed(range(N_EXP)):
        next_row = jnp.where(lane == e, nxt, next_row)
        nxt = jnp.where(cnt[e:e + 1, :] > 0, e, nxt)
    meta_ref[...] = jnp.concatenate([tile_e, cnt_row, off_row, end_row, nt_row, next_row, zero, zero], axis=0)


def _route_pos(idx):
    return pl.pallas_call(
        _pos_kernel,
        grid=(1,),
        in_specs=[pl.BlockSpec((2, T), lambda i: (0, 0))],
        out_specs=[pl.BlockSpec((2, T), lambda i: (0, 0)), pl.BlockSpec((8, 128), lambda i: (0, 0))],
        out_shape=[jax.ShapeDtypeStruct((2, T), jnp.int32), jax.ShapeDtypeStruct((8, 128), jnp.int32)],
        compiler_params=_cp(("arbitrary",)),
        name="route_pos",
    )(idx)


DISP_TM = 256


def _dispatch_kernel(meta_ref, pos_ref, h_ref, z_hbm, xs_hbm, sem):
    i = pl.program_id(0)

    def row_copy(src, src_row, dst_row):
        return pltpu.make_async_copy(src.at[pl.ds(src_row, 1), :], xs_hbm.at[pl.ds(dst_row, 1), :], sem)

    for r in range(DISP_TM):
        row_copy(h_ref, r, pos_ref[0, r]).start(priority=0)
        row_copy(h_ref, r, pos_ref[1, r]).start(priority=1)

    e = jnp.minimum(i, N_EXP - 1)
    pad0 = meta_ref[META_OFF, e] + meta_ref[META_CNT, e]
    npad = jnp.where(i < N_EXP, meta_ref[META_END, e] - pad0, 0)

    def zero_fill(p, c):
        row_copy(z_hbm, 0, p).start()
        return c

    lax.fori_loop(pad0, pad0 + npad, zero_fill, 0)

    for _ in range(2):
        pltpu.make_async_copy(h_ref, xs_hbm.at[pl.ds(0, DISP_TM), :], sem).wait()

    tail = meta_ref[META_NT, 0] + i
    has_tail = (i < N_EXP) & (tail < MOE_NT)

    def tail_copy():
        rows = pl.ds(pl.multiple_of(tail * MOE_TM, MOE_TM), MOE_TM)
        return pltpu.make_async_copy(z_hbm, xs_hbm.at[rows, :], sem)

    @pl.when(has_tail)
    def _():
        tail_copy().start()

    def zero_wait(p, c):
        row_copy(z_hbm, 0, p).wait()
        return c

    lax.fori_loop(pad0, pad0 + npad, zero_wait, 0)

    @pl.when(has_tail)
    def _():
        tail_copy().wait()


def _dispatch(meta, pos, h2, zrow):
    return pl.pallas_call(
        _dispatch_kernel,
        grid_spec=pltpu.PrefetchScalarGridSpec(
            num_scalar_prefetch=1,
            grid=(T // DISP_TM,),
            in_specs=[
                pl.BlockSpec((2, DISP_TM), lambda i, meta: (0, i), memory_space=pltpu.SMEM),
                pl.BlockSpec((DISP_TM, D), lambda i, meta: (i, 0)),
                pl.BlockSpec((MOE_TM, D), lambda i, meta: (0, 0)),
            ],
            out_specs=pl.BlockSpec(memory_space=pl.ANY),
            scratch_shapes=[pltpu.SemaphoreType.DMA],
        ),
        out_shape=jax.ShapeDtypeStruct((MOE_ROWS, D), F32),
        compiler_params=_cp(("arbitrary",)),
        name="dispatch",
    )(meta, pos, h2, zrow)


def _experts_kernel(meta_ref, xs_ref, wg_hbm, wu_hbm, wd_hbm, ys_ref,
                    wg_f, wu_f, wd_f, wg_s, wu_s, wd_s, sem, *, l):
    j = pl.program_id(0)
    live = j < meta_ref[META_NT, 0]
    e = meta_ref[META_TILE_E, j]
    e_prev = meta_ref[META_TILE_E, jnp.maximum(j - 1, 0)]

    def fetch(ex):
        return (pltpu.make_async_copy(wg_hbm.at[l, ex], wg_f, sem.at[0]),
                pltpu.make_async_copy(wu_hbm.at[l, ex], wu_f, sem.at[1]),
                pltpu.make_async_copy(wd_hbm.at[l, ex], wd_f, sem.at[2]))

    @pl.when(j == 0)
    def _():
        for c in fetch(e):
            c.start()

    @pl.when(live & ((j == 0) | (e != e_prev)))
    def _():
        for c, dst, src in zip(fetch(e), (wg_s, wu_s, wd_s), (wg_f, wu_f, wd_f)):
            c.wait()
            dst[...] = src[...].astype(BF16)
        nxt = meta_ref[META_NEXT_E, e]

        @pl.when(nxt >= 0)
        def _():
            for c in fetch(nxt):
                c.start()

    @pl.when(live)
    def _():
        x = xs_ref[...].astype(BF16)
        g = _dot(x, wg_s[...])
        u = _dot(x, wu_s[...])
        act = (g * _sigmoid(g)) * u
        ys_ref[...] = _dot(act.astype(BF16), wd_s[...])

    @pl.when(jnp.logical_not(live))
    def _():
        ys_ref[...] = jnp.zeros_like(ys_ref)


def _experts(meta, xs, w_gate_e, w_up_e, w_down_e, l):
    def tile(j, meta):
        return jnp.minimum(j, meta[META_NT, 0] - 1)

    return pl.pallas_call(
        functools.partial(_experts_kernel, l=l),
        grid_spec=pltpu.PrefetchScalarGridSpec(
            num_scalar_prefetch=1,
            grid=(MOE_NT,),
            in_specs=[
                pl.BlockSpec((MOE_TM, D), lambda j, meta: (tile(j, meta), 0)),
                pl.BlockSpec(memory_space=pl.ANY),
                pl.BlockSpec(memory_space=pl.ANY),
                pl.BlockSpec(memory_space=pl.ANY),
            ],
            out_specs=pl.BlockSpec((MOE_TM, D), lambda j, meta: (j, 0)),
            scratch_shapes=[pltpu.VMEM((D, D_EXP), F32), pltpu.VMEM((D, D_EXP), F32), pltpu.VMEM((D_EXP, D), F32),
                            pltpu.VMEM((D, D_EXP), BF16), pltpu.VMEM((D, D_EXP), BF16), pltpu.VMEM((D_EXP, D), BF16),
                            pltpu.SemaphoreType.DMA((3,))],
        ),
        out_shape=jax.ShapeDtypeStruct((MOE_ROWS, D), F32),
        compiler_params=_cp(("arbitrary",)),
        name="experts",
    )(meta, xs, w_gate_e, w_up_e, w_down_e)


COMB_TM = SEG


def _combine_kernel(pos_ref, w_ref, x1_ref, mod_ref, fg_ref, ys_hbm, oa_ref, ob_ref, buf, sem, *, final):
    i = pl.program_id(0)
    n = pl.num_programs(0) - 1
    n_ctx = T_CTX // COMB_TM

    for s in range(2):
        @pl.when((i < n) & (lax.rem(i, 2) == s))
        def _():
            for r in range(COMB_TM):
                for k in range(2):
                    pltpu.make_async_copy(ys_hbm.at[pl.ds(pos_ref[k, r], 1), :],
                                          buf.at[s, k, pl.ds(r, 1), :], sem.at[s]).start(priority=k)

    @pl.when(i > 0)
    def _():
        slot = lax.rem(i - 1, 2)
        for k in range(2):
            pltpu.make_async_copy(ys_hbm.at[pl.ds(0, COMB_TM), :], buf.at[slot, k], sem.at[slot]).wait()
        w = w_ref[...]
        y = w[:, 0:1] * buf[slot, 0] + w[:, 1:2] * buf[slot, 1]
        x = x1_ref[...] + mod_ref[5:6, :] * y
        if final:
            ms = jnp.mean(x * x, axis=-1, keepdims=True)
            x = x * lax.rsqrt(ms + EPS) * fg_ref[...]

        @pl.when(i - 1 < n_ctx)
        def _():
            oa_ref[...] = x

        @pl.when(i - 1 >= n_ctx)
        def _():
            ob_ref[...] = x


def _combine(pos, wts_t, x1, modseg, final_g, ys, l, final):
    n = T // COMB_TM
    n_ctx = T_CTX // COMB_TM

    def done(i):
        return jnp.maximum(i - 1, 0)

    return pl.pallas_call(
        functools.partial(_combine_kernel, final=final),
        grid=(n + 1,),
        in_specs=[
            pl.BlockSpec((2, COMB_TM), lambda i: (0, jnp.minimum(i, n - 1)), memory_space=pltpu.SMEM),
            pl.BlockSpec((COMB_TM, 2), lambda i: (done(i), 0)),
            pl.BlockSpec((COMB_TM, D), lambda i: (done(i), 0)),
            pl.BlockSpec((None, None, 8, D), lambda i: (l, done(i), 0, 0)),
            pl.BlockSpec((1, D), lambda i: (0, 0)),
            pl.BlockSpec(memory_space=pl.ANY),
        ],
        out_specs=[pl.BlockSpec((COMB_TM, D), lambda i: (jnp.minimum(done(i), n_ctx - 1), 0)),
                   pl.BlockSpec((COMB_TM, D), lambda i: (jnp.maximum(done(i) - n_ctx, 0), 0))],
        out_shape=[jax.ShapeDtypeStruct((T_CTX, D), F32), jax.ShapeDtypeStruct((T_LAT, D), F32)],
        scratch_shapes=[pltpu.VMEM((2, 2, COMB_TM, D), F32), pltpu.SemaphoreType.DMA((2,))],
        compiler_params=_cp(("arbitrary",)),
        name="combine",
    )(pos, wts_t, x1, modseg, final_g.reshape(1, D), ys)


def _rope_tables():
    n = DEC_SEQ
    pos_row = np.repeat(np.arange(n // GRID_W, dtype=np.float32), GRID_W)
    pos_col = np.tile(np.arange(GRID_W, dtype=np.float32), n // GRID_W)
    half = HD // 2
    inv_freq = jnp.asarray(ROPE_THETA, F32) ** (-jnp.arange(0, half, 2, dtype=F32) / half)
    ang = jnp.concatenate([jnp.asarray(pos_row)[:, None] * inv_freq,
                           jnp.asarray(pos_col)[:, None] * inv_freq], axis=-1)
    cos, sin = jnp.cos(ang), jnp.sin(ang)
    cos128 = jnp.tile(cos, (1, 4))
    sin128 = jnp.tile(jnp.concatenate([-sin, sin], axis=-1), (1, 2))
    return cos128, sin128


def _head_mean_matrix():
    idx = np.arange(2 * HD)
    same = (idx[:, None] // HD) == (idx[None, :] // HD)
    return jnp.asarray(same.astype(np.float32) / HD, BF16)


_SEG_ROWS = np.array([0] * (T_CTX // SEG) + [1 + b for b in range(DEC_BATCH) for _ in range(DEC_SEQ // SEG)])


def kernel(x_prompt, x_sample, cache_k, cache_v, state_rec, c, c_ctx, w_mod, b_mod, norm1_g, norm2_g, w_in, conv_w, conv_b, rg_wa, rg_ba, rg_wx, rg_bx, rg_lambda, q_norm_g, k_norm_g, w_rec_out, w_att_out, w_out, w_router, router_bias, w_gate_e, w_up_e, w_down_e, final_g):
    xa, xb = x_prompt.reshape(T_CTX, D), x_sample.reshape(T_LAT, D)

    cvecs = jnp.concatenate([c_ctx[None, :], c, jnp.zeros((3, D), F32)], axis=0)
    mods = _mods(cvecs, w_mod, b_mod).reshape(DEPTH, 8, 6, D)
    modseg = jnp.pad(mods[:, _SEG_ROWS], ((0, 0), (0, 0), (0, 2), (0, 0)))

    cos128, sin128 = _rope_tables()
    bd = _head_mean_matrix()
    qg128 = jnp.tile(q_norm_g, (1, 2)).reshape(DEPTH, 1, 2 * HD)
    kg128 = jnp.tile(k_norm_g, (1, 2)).reshape(DEPTH, 1, 2 * HD)
    wg = jnp.concatenate([rg_wa[:, 0], rg_wx[:, 0], rg_wa[:, 1], rg_wx[:, 1]], axis=-1)
    pvec = jnp.stack([rg_ba[:, 0], rg_bx[:, 0], rg_ba[:, 1], rg_bx[:, 1],
                      rg_lambda[:, 0], rg_lambda[:, 1], conv_b, jnp.zeros_like(conv_b)], axis=1)
    wrt = w_router.T
    rbias = router_bias.reshape(N_EXP, 1)
    zrow = jnp.zeros((MOE_TM, D), F32)

    new_k, new_v, new_s = [], [], []
    for l in range(DEPTH):
        proj = _inproj(xa, xb, modseg, norm1_g, w_in, l)
        h0 = jnp.concatenate([jnp.zeros((T_CTX // UNIT, 2, D), F32), state_rec[:, l]], axis=0)
        yrec, stf, stb = _rec(proj, conv_w, pvec, wg, h0, l)
        qc, kc, vc = _qkv(proj, qg128, kg128, cos128, sin128, bd, l, latent=False)
        ql, kl, vl = _qkv(proj, qg128, kg128, cos128, sin128, bd, l, latent=True)
        o_ctx = _attn_ctx(qc, kc, vc)
        o_lat = _attn_lat(ql, cache_k, cache_v, kl, vl, l)
        x1, h2, idx, wts = _merge(yrec, o_ctx, o_lat, proj, xa, xb, modseg, norm2_g,
                                  w_rec_out, w_att_out, w_out, wrt, rbias, l)
        pos, meta = _route_pos(idx)
        xs = _dispatch(meta, pos, h2, zrow)
        ys = _experts(meta, xs, w_gate_e, w_up_e, w_down_e, l)
        xa, xb = _combine(pos, wts.T, x1, modseg, final_g, ys, l, final=(l == DEPTH - 1))
        new_k.append(kc)
        new_v.append(vc)
        n_cu = T_CTX // UNIT
        spu = UNIT // SEQ
        hf_last = stf[:n_cu].reshape(n_cu, spu, 2, D)[:, :, 1].reshape(BATCH, D)
        hb_first = stb[:n_cu].reshape(n_cu, spu, 2, D)[:, :, 0].reshape(BATCH, D)
        new_s.append(jnp.stack([hf_last, hb_first], axis=1))

    y_prompt = xa.reshape(BATCH, SEQ, D)
    y_sample = xb.reshape(DEC_BATCH, DEC_SEQ, D)
    return (y_prompt, y_sample, jnp.stack(new_k, axis=1), jnp.stack(new_v, axis=1), jnp.stack(new_s, axis=1))
```

```python
import functools

import numpy as np
import jax
import jax.numpy as jnp
from jax import lax
from jax.experimental import pallas as pl
from jax.experimental.pallas import tpu as pltpu

F32 = jnp.float32
BF16 = jnp.bfloat16

D = 1024
BATCH = 16
SEQ = 256
DEPTH = 2
DEC_BATCH = 4
DEC_SEQ = 1024
PAST = 256
GRID_W = 64
N_HEADS = 16
N_KV = 4
HD = 64
N_RG_BLK = 8
RG_BLK = 128
RG_C = 8.0
N_EXP = 16
D_EXP = 512
ROPE_THETA = 10000.0
EPS = 1e-6
P_IN = 5632
TINY = float(np.finfo(np.float32).tiny)
NEG_LOG2E = -float(np.log2(np.e))

T_CTX = BATCH * SEQ
T_LAT = DEC_BATCH * DEC_SEQ
T = T_CTX + T_LAT
SEG = 256
N_SEG = T // SEG
UNIT = 1024
N_UNIT = T // UNIT
CHUNK = UNIT // 8
CSTRIDE = CHUNK + 8

VMEM_LIMIT = 56 * 1024 * 1024


def _cp(sem):
    return pltpu.CompilerParams(dimension_semantics=sem, vmem_limit_bytes=VMEM_LIMIT)


def _split(x):
    hi = x.astype(BF16)
    lo = (x - hi.astype(F32)).astype(BF16)
    return hi, lo


def _sigmoid(x):
    return 0.5 * jnp.tanh(0.5 * x) + 0.5


def _dot(a, b):
    return jnp.dot(a, b, preferred_element_type=F32)


def _dot_nt(a, b):
    return lax.dot_general(a, b, (((1,), (1,)), ((), ())), preferred_element_type=F32)


def _mods_kernel(c_ref, w_ref, b_ref, o_ref):
    c = c_ref[...]
    s = c * jax.nn.sigmoid(c)
    s_hi, s_lo = _split(s)
    w_hi, w_lo = _split(w_ref[...])
    o_ref[...] = _dot(s_hi, w_hi) + _dot(s_hi, w_lo) + _dot(s_lo, w_hi) + b_ref[...]


def _mods(cvecs, w_mod, b_mod):
    tn = 1536
    return pl.pallas_call(
        _mods_kernel,
        grid=(DEPTH, 6 * D // tn),
        in_specs=[
            pl.BlockSpec((8, D), lambda l, j: (0, 0)),
            pl.BlockSpec((None, D, tn), lambda l, j: (l, 0, j)),
            pl.BlockSpec((None, 1, tn), lambda l, j: (l, 0, j)),
        ],
        out_specs=pl.BlockSpec((None, 8, tn), lambda l, j: (l, 0, j)),
        out_shape=jax.ShapeDtypeStruct((DEPTH, 8, 6 * D), F32),
        compiler_params=_cp(("arbitrary", "arbitrary")),
        name="mods",
    )(cvecs, w_mod, b_mod.reshape(DEPTH, 1, 6 * D))


def _norm_mod(x, g, shift, scale):
    ms = jnp.mean(x * x, axis=-1, keepdims=True)
    return x * lax.rsqrt(ms + EPS) * g * (1.0 + scale) + shift


def _two_part_specs(tm, n_ctx):
    return [pl.BlockSpec((tm, D), lambda i, *_: (jnp.minimum(i, n_ctx - 1), 0)),
            pl.BlockSpec((tm, D), lambda i, *_: (jnp.maximum(i - n_ctx, 0), 0))]


def _inproj_kernel(xa_ref, xb_ref, mod_ref, g_ref, w_ref, o_ref, h_ref, *, tm):
    def prologue(x_ref):
        def seg(s, carry):
            r0 = pl.multiple_of(s * SEG, SEG)
            m = mod_ref[s]
            h = _norm_mod(x_ref[pl.ds(r0, SEG), :], g_ref[...], m[0:1, :], m[1:2, :])
            h_ref[pl.ds(r0, SEG), :] = h.astype(BF16)
            return carry
        lax.fori_loop(0, tm // SEG, seg, 0)

    first = pl.program_id(1) == 0
    is_ctx = pl.program_id(0) < T_CTX // tm

    @pl.when(first & is_ctx)
    def _():
        prologue(xa_ref)

    @pl.when(first & jnp.logical_not(is_ctx))
    def _():
        prologue(xb_ref)

    o_ref[...] = _dot(h_ref[...], w_ref[...].astype(BF16)).astype(BF16)


def _inproj(xa, xb, modseg, norm_g, w_in, l):
    tm, tn = 2048, 512
    return pl.pallas_call(
        functools.partial(_inproj_kernel, tm=tm),
        grid=(T // tm, P_IN // tn),
        in_specs=_two_part_specs(tm, T_CTX // tm) + [
            pl.BlockSpec((None, tm // SEG, 8, D), lambda i, j: (l, i, 0, 0)),
            pl.BlockSpec((None, 1, D), lambda i, j: (l, 0, 0)),
            pl.BlockSpec((None, D, tn), lambda i, j: (l, 0, j)),
        ],
        out_specs=pl.BlockSpec((tm, tn), lambda i, j: (i, j)),
        out_shape=jax.ShapeDtypeStruct((T, P_IN), BF16),
        scratch_shapes=[pltpu.VMEM((tm, D), BF16)],
        compiler_params=_cp(("arbitrary", "arbitrary")),
        name="inproj",
    )(xa, xb, modseg, norm_g.reshape(DEPTH, 1, D), w_in)


REC_CW = 512
HALO = 16


def _rec_kernel(xr_ref, gate_ref, cw_ref, pv_ref, wg_ref, h0_ref,
                y_ref, stf_ref, stb_ref,
                af_ref, bf_ref, ab_ref, bb_ref, wgh_ref):
    u = pl.program_id(0)
    is_ctx = u < (T_CTX // UNIT)
    seq_len = jnp.where(is_ctx, SEQ, DEC_SEQ)
    nblk = REC_CW // RG_BLK

    pv = pv_ref[...]
    cwts = cw_ref[...]
    conv_b = pv[6:7, :]

    def softplus_neg(lam):
        z = -lam
        return jnp.maximum(z, 0.0) + jnp.log1p(jnp.exp(-jnp.abs(z)))

    c4s = tuple((0.5 * RG_C) * softplus_neg(pv[4 + d:5 + d, :]) for d in range(2))
    pv_h = 0.5 * pv
    for n in range(nblk):
        wgh_ref[n] = (0.5 * wg_ref[n]).astype(BF16)
    a_refs = (af_ref, ab_ref)
    b_refs = (bf_ref, bb_ref)

    def gates(ci, carry):
        base = pl.multiple_of(ci * CHUNK, CHUNK)
        lo = pl.multiple_of(jnp.maximum(base - HALO, 0), HALO)
        hi = pl.multiple_of(jnp.minimum(base + CHUNK, UNIT - HALO), HALO)
        main = xr_ref[pl.ds(base, CHUNK), :].astype(F32)
        win = jnp.concatenate([xr_ref[pl.ds(lo, HALO), :].astype(F32), main,
                               xr_ref[pl.ds(hi, HALO), :].astype(F32)], axis=0)
        n_win = CHUNK + 2 * HALO
        row8 = lax.broadcasted_iota(jnp.int32, (8, 1), 0)
        tl_head = jnp.bitwise_and(base + row8, seq_len - 1)
        tl_tail = jnp.bitwise_and(base + (CHUNK - 8) + row8, seq_len - 1)

        def tap(shift, head_ok=None, tail_ok=None):
            x = pltpu.roll(win, shift, 0)[HALO:HALO + CHUNK]
            if head_ok is not None:
                return jnp.concatenate([jnp.where(head_ok, x[:8], 0.0), x[8:]], axis=0)
            return jnp.concatenate([x[:CHUNK - 8], jnp.where(tail_ok, x[CHUNK - 8:], 0.0)], axis=0)

        xm2 = tap(2, head_ok=tl_head >= 2)
        xm1 = tap(1, head_ok=tl_head >= 1)
        xp1 = tap(n_win - 1, tail_ok=tl_tail <= seq_len - 2)
        xc = conv_b + xm2 * cwts[0:1, :]
        xc = xc + xm1 * cwts[1:2, :]
        xc = xc + main * cwts[2:3, :]
        xc = xc + xp1 * cwts[3:4, :]
        for n in range(nblk):
            ls = slice(n * RG_BLK, (n + 1) * RG_BLK)
            xn = xc[:, ls]
            hx = 0.5 * xn
            pre_h = _dot(xn.astype(BF16), wgh_ref[n])
            for d in range(2):
                th_r = jnp.tanh(pre_h[:, (2 * d) * RG_BLK:(2 * d + 1) * RG_BLK] + pv_h[2 * d:2 * d + 1, ls])
                th_i = jnp.tanh(pre_h[:, (2 * d + 1) * RG_BLK:(2 * d + 2) * RG_BLK] + pv_h[2 * d + 1:2 * d + 2, ls])
                c4 = c4s[d][:, ls]
                nla = c4 * th_r + c4
                a = jnp.exp2(nla * NEG_LOG2E)
                s = jnp.tanh(nla) * (a * a + 1.0)
                inp = (s * lax.rsqrt(jnp.maximum(s, TINY))) * (hx * th_i + hx)
                sbase = pl.multiple_of(ci * CSTRIDE, 8)
                a_refs[d][n, pl.ds(sbase, CHUNK), :] = a
                b_refs[d][n, pl.ds(sbase, CHUNK), :] = inp
        return carry

    lax.fori_loop(0, 8, gates, 0)

    zeros = [jnp.zeros((8, RG_BLK), F32)] * nblk
    ones = [jnp.ones((8, RG_BLK), F32)] * nblk
    hf, pf, hb, pb = list(zeros), list(ones), list(zeros), list(ones)
    for r in range(CHUNK):
        rows_f = pl.ds(r, 8, stride=CSTRIDE)
        rows_b = pl.ds(CHUNK - 1 - r, 8, stride=CSTRIDE)
        for n in range(nblk):
            a = af_ref[n, rows_f, :]
            hf[n] = a * hf[n] + bf_ref[n, rows_f, :]
            pf[n] = a * pf[n]
            bf_ref[n, rows_f, :] = hf[n]
            af_ref[n, rows_f, :] = pf[n]
            a = ab_ref[n, rows_b, :]
            hb[n] = a * hb[n] + bb_ref[n, rows_b, :]
            pb[n] = a * pb[n]
            bb_ref[n, rows_b, :] = hb[n]
            ab_ref[n, rows_b, :] = pb[n]
    hf, pf, hb, pb = (jnp.concatenate(v, axis=-1) for v in (hf, pf, hb, pb))

    cps = jnp.where(is_ctx, SEQ // CHUNK, DEC_SEQ // CHUNK)
    h0f = h0_ref[0:1, :]
    h0b = h0_ref[1:2, :]
    cf = [h0f]
    for c in range(1, 8):
        chain = hf[c - 1:c, :] + pf[c - 1:c, :] * cf[c - 1]
        cf.append(jnp.where(jnp.bitwise_and(c, cps - 1) == 0, h0f, chain))
    cb = [None] * 8
    cb[7] = h0b
    for c in range(6, -1, -1):
        chain = hb[c + 1:c + 2, :] + pb[c + 1:c + 2, :] * cb[c + 1]
        cb[c] = jnp.where(jnp.bitwise_and(c, cps - 1) == cps - 1, h0b, chain)
    carry_f = jnp.concatenate(cf, axis=0)
    carry_b = jnp.concatenate(cb, axis=0)
    stf_ref[...] = hf + pf * carry_f
    stb_ref[...] = hb + pb * carry_b

    for ci in range(8):
        rows = pl.ds(ci * CHUNK, CHUNK)
        srows = pl.ds(ci * CSTRIDE, CHUNK)
        for n in range(nblk):
            ls = slice(n * RG_BLK, (n + 1) * RG_BLK)
            h_f = bf_ref[n, srows, :] + af_ref[n, srows, :] * carry_f[ci:ci + 1, ls]
            h_b = bb_ref[n, srows, :] + ab_ref[n, srows, :] * carry_b[ci:ci + 1, ls]
            g = gate_ref[rows, ls].astype(F32)
            y_ref[rows, ls] = ((h_f + h_b) * jax.nn.gelu(g, approximate=True)).astype(BF16)


def _rec(proj, conv_w, pvec, wg, h0, l):
    ncb = D // REC_CW
    return pl.pallas_call(
        _rec_kernel,
        grid=(N_UNIT, ncb),
        in_specs=[
            pl.BlockSpec((UNIT, REC_CW), lambda u, c: (u, c)),
            pl.BlockSpec((UNIT, REC_CW), lambda u, c: (u, ncb + c)),
            pl.BlockSpec((None, 4, REC_CW), lambda u, c: (l, 0, c)),
            pl.BlockSpec((None, 8, REC_CW), lambda u, c: (l, 0, c)),
            pl.BlockSpec((None, REC_CW // RG_BLK, RG_BLK, 4 * RG_BLK), lambda u, c: (l, c, 0, 0)),
            pl.BlockSpec((None, 2, REC_CW), lambda u, c: (u, 0, c)),
        ],
        out_specs=[
            pl.BlockSpec((UNIT, REC_CW), lambda u, c: (u, c)),
            pl.BlockSpec((None, 8, REC_CW), lambda u, c: (u, 0, c)),
            pl.BlockSpec((None, 8, REC_CW), lambda u, c: (u, 0, c)),
        ],
        out_shape=[
            jax.ShapeDtypeStruct((T, D), BF16),
            jax.ShapeDtypeStruct((N_UNIT, 8, D), F32),
            jax.ShapeDtypeStruct((N_UNIT, 8, D), F32),
        ],
        scratch_shapes=[pltpu.VMEM((REC_CW // RG_BLK, 8 * CSTRIDE, RG_BLK), F32)] * 4
        + [pltpu.VMEM((REC_CW // RG_BLK, RG_BLK, 4 * RG_BLK), BF16)],
        compiler_params=_cp(("arbitrary", "arbitrary")),
        name="rec",
    )(proj, proj, conv_w, pvec, wg, h0)


def _head_norm(x, g128, bd):
    hi, lo = _split(x * x)
    ms = _dot(hi, bd) + _dot(lo, bd)
    return x * lax.rsqrt(ms + EPS) * g128


def _rope(x, cos, sin_signed):
    lane = lax.broadcasted_iota(jnp.int32, x.shape, 1)
    first_half = jnp.bitwise_and(lane, HD - 1) < HD // 2
    partner = jnp.where(first_half, pltpu.roll(x, 2 * HD - HD // 2, 1), pltpu.roll(x, HD // 2, 1))
    return x * cos + partner * sin_signed


def _qkv_kernel(q_ref, k_ref, v_ref, qg_ref, kg_ref, cos_ref, sin_ref, bd_ref,
                qo_ref, ko_ref, vo_ref, *, rope):
    bd = bd_ref[...]
    scale = HD ** -0.5 * float(np.log2(np.e))
    for j in range(N_HEADS // 2):
        x = _head_norm(q_ref[:, 2 * HD * j:2 * HD * (j + 1)].astype(F32), qg_ref[...], bd)
        if rope:
            x = _rope(x, cos_ref[...], sin_ref[...])
        x = x * scale
        qo_ref[2 * j] = x[:, :HD].astype(qo_ref.dtype)
        qo_ref[2 * j + 1] = x[:, HD:].astype(qo_ref.dtype)
    for j in range(N_KV // 2):
        x = _head_norm(k_ref[:, 2 * HD * j:2 * HD * (j + 1)].astype(F32), kg_ref[...], bd)
        if rope:
            x = _rope(x, cos_ref[...], sin_ref[...])
        ko_ref[2 * j] = x[:, :HD].astype(ko_ref.dtype)
        ko_ref[2 * j + 1] = x[:, HD:].astype(ko_ref.dtype)
        v = v_ref[:, 2 * HD * j:2 * HD * (j + 1)].astype(F32)
        vo_ref[2 * j] = v[:, :HD].astype(vo_ref.dtype)
        vo_ref[2 * j + 1] = v[:, HD:].astype(vo_ref.dtype)


def _qkv(proj, qg128, kg128, cos128, sin128, bd, l, latent):
    tm = SEG
    n = T_LAT // tm if latent else T_CTX // tm
    roff = T_CTX // tm if latent else 0
    per_seq = DEC_SEQ // tm
    if latent:
        kv_shape = (DEC_BATCH, N_KV, DEC_SEQ, HD)
        kv_spec = pl.BlockSpec((None, N_KV, tm, HD), lambda i: (i // per_seq, 0, i % per_seq, 0))
        kv_dtype = BF16
        tab_map = lambda i: (i % per_seq, 0)
    else:
        kv_shape = (BATCH, N_KV, SEQ, HD)
        kv_spec = pl.BlockSpec((None, N_KV, tm, HD), lambda i: (i, 0, 0, 0))
        kv_dtype = F32
        tab_map = lambda i: (0, 0)
    return pl.pallas_call(
        functools.partial(_qkv_kernel, rope=latent),
        grid=(n,),
        in_specs=[
            pl.BlockSpec((tm, D), lambda i: (roff + i, 2)),
            pl.BlockSpec((tm, N_KV * HD), lambda i: (roff + i, 3 * D // (N_KV * HD))),
            pl.BlockSpec((tm, N_KV * HD), lambda i: (roff + i, 3 * D // (N_KV * HD) + 1)),
            pl.BlockSpec((None, 1, 2 * HD), lambda i: (l, 0, 0)),
            pl.BlockSpec((None, 1, 2 * HD), lambda i: (l, 0, 0)),
            pl.BlockSpec((tm, 2 * HD), tab_map),
            pl.BlockSpec((tm, 2 * HD), tab_map),
            pl.BlockSpec((2 * HD, 2 * HD), lambda i: (0, 0)),
        ],
        out_specs=[
            pl.BlockSpec((N_HEADS, tm, HD), lambda i: (0, i, 0)),
            kv_spec,
            kv_spec,
        ],
        out_shape=[
            jax.ShapeDtypeStruct((N_HEADS, n * tm, HD), BF16),
            jax.ShapeDtypeStruct(kv_shape, kv_dtype),
            jax.ShapeDtypeStruct(kv_shape, kv_dtype),
        ],
        compiler_params=_cp(("arbitrary",)),
        name="qkv_lat" if latent else "qkv_ctx",
    )(proj, proj, proj, qg128, kg128, cos128, sin128, bd)


def _with_ones(v):
    return jnp.concatenate([v, jnp.ones_like(v)], axis=-1)


def _softmax_pv(q, k, v_ext):
    s = _dot_nt(q, k)
    m = jnp.max(s, axis=-1, keepdims=True)
    p = jnp.exp2(s - m).astype(BF16)
    r = _dot(p, v_ext)
    return r[:, :HD] / r[:, HD:HD + 1]


def _attend_heads(q_ref, k, v_ext):
    return jnp.concatenate([_softmax_pv(q_ref[h], k, v_ext) for h in range(N_HEADS // N_KV)], axis=-1)


def _attn_ctx_kernel(q_ref, k_ref, v_ref, o_ref):
    g = N_HEADS // N_KV
    q = q_ref[...].reshape(g * SEQ, HD)
    o = _softmax_pv(q, k_ref[...].astype(BF16), _with_ones(v_ref[...].astype(BF16)))
    o = jnp.concatenate([o[h * SEQ:(h + 1) * SEQ] for h in range(g)], axis=-1)
    o_ref[...] = o.astype(BF16)


def _attn_ctx(qh, kc, vc):
    g = N_HEADS // N_KV
    return pl.pallas_call(
        _attn_ctx_kernel,
        grid=(BATCH, N_KV),
        in_specs=[
            pl.BlockSpec((g, SEQ, HD), lambda b, h: (h, b, 0)),
            pl.BlockSpec((None, None, SEQ, HD), lambda b, h: (b, h, 0, 0)),
            pl.BlockSpec((None, None, SEQ, HD), lambda b, h: (b, h, 0, 0)),
        ],
        out_specs=pl.BlockSpec((SEQ, g * HD), lambda b, h: (b, h)),
        out_shape=jax.ShapeDtypeStruct((T_CTX, D), BF16),
        compiler_params=_cp(("arbitrary", "arbitrary")),
        name="attn_ctx",
    )(qh, kc, vc)


ATT_TQ = 1024


def _attn_lat_kernel(q_ref, pk_ref, pv_ref, k_ref, v_ref, o_ref, k_s, v_s):
    k_s[0:PAST, :] = pk_ref[...].astype(BF16)
    k_s[PAST:, :] = k_ref[...]
    v_s[0:PAST, :] = _with_ones(pv_ref[...].astype(BF16))
    v_s[PAST:, :] = _with_ones(v_ref[...])

    def q_tile(qi, carry):
        rows = pl.ds(pl.multiple_of(qi * ATT_TQ, ATT_TQ), ATT_TQ)
        o = jnp.concatenate([_softmax_pv(q_ref[h, rows, :], k_s[...], v_s[...])
                             for h in range(N_HEADS // N_KV)], axis=-1)
        o_ref[rows, :] = o.astype(BF16)
        return carry

    lax.fori_loop(0, DEC_SEQ // ATT_TQ, q_tile, 0)


def _attn_lat(qh, cache_k, cache_v, kr, vr, l):
    g = N_HEADS // N_KV
    return pl.pallas_call(
        _attn_lat_kernel,
        grid=(DEC_BATCH, N_KV),
        in_specs=[
            pl.BlockSpec((g, DEC_SEQ, HD), lambda b, h: (h, b, 0)),
            pl.BlockSpec((None, None, None, PAST, HD), lambda b, h: (b, l, h, 0, 0)),
            pl.BlockSpec((None, None, None, PAST, HD), lambda b, h: (b, l, h, 0, 0)),
            pl.BlockSpec((None, None, DEC_SEQ, HD), lambda b, h: (b, h, 0, 0)),
            pl.BlockSpec((None, None, DEC_SEQ, HD), lambda b, h: (b, h, 0, 0)),
        ],
        out_specs=pl.BlockSpec((DEC_SEQ, g * HD), lambda b, h: (b, h)),
        out_shape=jax.ShapeDtypeStruct((T_LAT, D), BF16),
        scratch_shapes=[pltpu.VMEM((PAST + DEC_SEQ, HD), BF16), pltpu.VMEM((PAST + DEC_SEQ, 2 * HD), BF16)],
        compiler_params=_cp(("arbitrary", "arbitrary")),
        name="attn_lat",
    )(qh, cache_k, cache_v, kr, vr)


MERGE_TM = 512


def _route(lt, bias):
    rows = [lt[e:e + 1, :] for e in range(N_EXP)]
    m = rows[0]
    for e in range(1, N_EXP):
        m = jnp.maximum(m, rows[e])
    ex = [jnp.exp(r - m) for r in rows]
    z = ex[0]
    for e in range(1, N_EXP):
        z = z + ex[e]
    probs = [x / z for x in ex]
    sel = [probs[e] + bias[e:e + 1, :] for e in range(N_EXP)]

    def top2_sum(v):
        a, b = jnp.maximum(v[0], v[1]), jnp.minimum(v[0], v[1])
        c, d = jnp.maximum(v[2], v[3]), jnp.minimum(v[2], v[3])
        return jnp.maximum(a, c) + jnp.maximum(jnp.minimum(a, c), jnp.maximum(b, d))

    scores = [top2_sum(sel[4 * g:4 * g + 4]) for g in range(4)]
    best = jnp.zeros_like(scores[0], dtype=jnp.int32)
    best_s = scores[0]
    for g in range(1, 4):
        take = scores[g] > best_s
        best = jnp.where(take, g, best)
        best_s = jnp.where(take, scores[g], best_s)
    cs, cp = [], []
    for j in range(4):
        s_j, p_j = sel[j], probs[j]
        for g in range(1, 4):
            s_j = jnp.where(best == g, sel[4 * g + j], s_j)
            p_j = jnp.where(best == g, probs[4 * g + j], p_j)
        cs.append(s_j)
        cp.append(p_j)
    neg = jnp.full_like(cs[0], -jnp.inf)

    def argmax4(v):
        bi = jnp.zeros_like(best)
        bv = v[0]
        for j in range(1, 4):
            take = v[j] > bv
            bi = jnp.where(take, j, bi)
            bv = jnp.where(take, v[j], bv)
        return bi

    def pick(v, idx):
        out = v[0]
        for j in range(1, 4):
            out = jnp.where(idx == j, v[j], out)
        return out

    i1 = argmax4(cs)
    cs2 = [jnp.where(i1 == j, neg, cs[j]) for j in range(4)]
    i2 = argmax4(cs2)
    i2 = jnp.where((i2 == 0) & (i1 == 0), 1, i2)
    w1, w2 = pick(cp, i1), pick(cp, i2)
    den = w1 + w2
    return best * 4 + i1, best * 4 + i2, w1 / den, w2 / den


def _merge_kernel(yrec_ref, oa_ref, ob_ref, gr0_ref, gr1_ref, ga0_ref, ga1_ref, xa_ref, xb_ref, mod_ref, g2_ref,
                  wrec_ref, watt_ref, wout_ref, wrt_ref, rb_ref,
                  x1_ref, h2_ref, idx_ref, wts_ref,
                  wrec_s, watt_s, wout_s):
    @pl.when(pl.program_id(0) == 0)
    def _():
        wrec_s[...] = wrec_ref[...].astype(BF16)
        watt_s[...] = watt_ref[...].astype(BF16)
        wout_s[...] = wout_ref[...].astype(BF16)

    is_ctx = pl.program_id(0) < T_CTX // MERGE_TM
    args = (yrec_ref, gr0_ref, gr1_ref, ga0_ref, ga1_ref, mod_ref, g2_ref, wrt_ref, rb_ref,
            x1_ref, h2_ref, idx_ref, wts_ref, wrec_s, watt_s, wout_s)

    @pl.when(is_ctx)
    def _():
        _merge_body(oa_ref, xa_ref, *args)

    @pl.when(jnp.logical_not(is_ctx))
    def _():
        _merge_body(ob_ref, xb_ref, *args)


def _merge_body(oatt_ref, x_ref, yrec_ref, gr0_ref, gr1_ref, ga0_ref, ga1_ref, mod_ref, g2_ref, wrt_ref, rb_ref,
                x1_ref, h2_ref, idx_ref, wts_ref, wrec_s, watt_s, wout_s):
    half = D // 2
    b_rec = _dot(yrec_ref[...], wrec_s[...])
    b_att = _dot(oatt_ref[...], watt_s[...])
    m0 = _sigmoid(gr0_ref[...].astype(F32)) * b_rec[:, :half] + _sigmoid(ga0_ref[...].astype(F32)) * b_att[:, :half]
    m1 = _sigmoid(gr1_ref[...].astype(F32)) * b_rec[:, half:] + _sigmoid(ga1_ref[...].astype(F32)) * b_att[:, half:]
    merged = jnp.concatenate([m0, m1], axis=-1).astype(BF16)
    out = _dot(merged, wout_s[...])

    hs = []
    for s in range(MERGE_TM // SEG):
        rows = slice(s * SEG, (s + 1) * SEG)
        m = mod_ref[s]
        x1 = x_ref[rows, :] + m[2:3, :] * out[rows, :]
        x1_ref[rows, :] = x1
        h2 = _norm_mod(x1, g2_ref[...], m[3:4, :], m[4:5, :])
        h2_ref[rows, :] = h2
        hs.append(h2)
    h2 = jnp.concatenate(hs, axis=0)

    h_hi, h_lo = _split(h2)
    w_hi, w_lo = _split(wrt_ref[...])
    lt = _dot_nt(w_hi, h_hi) + _dot_nt(w_hi, h_lo) + _dot_nt(w_lo, h_hi)
    e1, e2, w1, w2 = _route(lt, rb_ref[...])
    idx_ref[...] = jnp.concatenate([e1, e2], axis=0)
    wts_ref[...] = jnp.concatenate([w1, w2], axis=0)


def _merge(yrec, o_ctx, o_lat, proj, xa, xb, modseg, norm2_g, w_rec_out, w_att_out, w_out, wrt, rbias, l):
    tm = MERGE_TM
    half = D // 2
    gcol = (3 * D + 2 * N_KV * HD) // half
    wspec = pl.BlockSpec((None, D, D), lambda i: (l, 0, 0))
    return pl.pallas_call(
        _merge_kernel,
        grid=(T // tm,),
        in_specs=[pl.BlockSpec((tm, D), lambda i: (i, 0))] + _two_part_specs(tm, T_CTX // tm) + [
            pl.BlockSpec((tm, half), lambda i: (i, gcol)),
            pl.BlockSpec((tm, half), lambda i: (i, gcol + 1)),
            pl.BlockSpec((tm, half), lambda i: (i, gcol + 2)),
            pl.BlockSpec((tm, half), lambda i: (i, gcol + 3)),
        ] + _two_part_specs(tm, T_CTX // tm) + [
            pl.BlockSpec((None, tm // SEG, 8, D), lambda i: (l, i, 0, 0)),
            pl.BlockSpec((None, 1, D), lambda i: (l, 0, 0)),
            wspec, wspec, wspec,
            pl.BlockSpec((N_EXP, D), lambda i: (0, 0)),
            pl.BlockSpec((N_EXP, 1), lambda i: (0, 0)),
        ],
        out_specs=[
            pl.BlockSpec((tm, D), lambda i: (i, 0)),
            pl.BlockSpec((tm, D), lambda i: (i, 0)),
            pl.BlockSpec((2, tm), lambda i: (0, i)),
            pl.BlockSpec((2, tm), lambda i: (0, i)),
        ],
        out_shape=[
            jax.ShapeDtypeStruct((T, D), F32),
            jax.ShapeDtypeStruct((T, D), F32),
            jax.ShapeDtypeStruct((2, T), jnp.int32),
            jax.ShapeDtypeStruct((2, T), F32),
        ],
        scratch_shapes=[pltpu.VMEM((D, D), BF16)] * 3,
        compiler_params=_cp(("arbitrary",)),
        name="merge",
    )(yrec, o_ctx, o_lat, proj, proj, proj, proj, xa, xb, modseg, norm2_g.reshape(DEPTH, 1, D),
      w_rec_out, w_att_out, w_out, wrt, rbias)


MOE_TM = 256
MOE_NT = 2 * T // MOE_TM + N_EXP
MOE_ROWS = MOE_NT * MOE_TM
META_TILE_E, META_CNT, META_OFF, META_END, META_NT, META_NEXT_E = 0, 1, 2, 3, 4, 5


def _pos_kernel(idx_ref, pos_ref, meta_ref):
    shift = MOE_TM.bit_length() - 1
    idx = idx_ref[...]
    eid = lax.broadcasted_iota(jnp.int32, (N_EXP, T), 0)
    m0 = eid == idx[0:1, :]
    m1 = eid == idx[1:2, :]
    member = jnp.where(m0 | m1, 1.0, 0.0)
    cnt = jnp.sum(member, axis=1, keepdims=True).astype(jnp.int32)
    ntile = jnp.right_shift(cnt + (MOE_TM - 1), shift)
    offs, acc = [], jnp.zeros((1, 1), jnp.int32)
    for e in range(N_EXP):
        offs.append(acc)
        acc = acc + ntile[e:e + 1, :]
    off_t = jnp.concatenate(offs, axis=0)
    end_t = off_t + ntile

    blk = 256
    r_i = lax.broadcasted_iota(jnp.int32, (blk, blk), 0)
    c_i = lax.broadcasted_iota(jnp.int32, (blk, blk), 1)
    upper = jnp.where(r_i <= c_i, 1.0, 0.0).astype(BF16)
    run = (off_t * MOE_TM).astype(F32)
    for j in range(T // blk):
        ls = slice(j * blk, (j + 1) * blk)
        mb = member[:, ls]
        inc = _dot(mb.astype(BF16), upper)
        dest = run + inc - mb
        pos_ref[0:1, ls] = jnp.sum(jnp.where(m0[:, ls], dest, 0.0), axis=0, keepdims=True).astype(jnp.int32)
        pos_ref[1:2, ls] = jnp.sum(jnp.where(m1[:, ls], dest, 0.0), axis=0, keepdims=True).astype(jnp.int32)
        run = run + inc[:, blk - 1:blk]

    lane = lax.broadcasted_iota(jnp.int32, (1, 128), 1)
    zero = jnp.zeros((1, 128), jnp.int32)
    tile_e, cnt_row, off_row, end_row = zero, zero, zero, zero
    for e in range(N_EXP):
        tile_e = tile_e + jnp.where(lane >= end_t[e:e + 1, :], 1, 0)
        here = lane == e
        cnt_row = jnp.where(here, cnt[e:e + 1, :], cnt_row)
        off_row = jnp.where(here, off_t[e:e + 1, :] * MOE_TM, off_row)
        end_row = jnp.where(here, end_t[e:e + 1, :] * MOE_TM, end_row)
    tile_e = jnp.minimum(tile_e, N_EXP - 1)
    nt_row = zero + acc
    next_row = zero
    nxt = jnp.full((1, 1), -1, jnp.int32)
    for e in reversed(range(N_EXP)):
        next_row = jnp.where(lane == e, nxt, next_row)
        nxt = jnp.where(cnt[e:e + 1, :] > 0, e, nxt)
    meta_ref[...] = jnp.concatenate([tile_e, cnt_row, off_row, end_row, nt_row, next_row, zero, zero], axis=0)


def _route_pos(idx):
    return pl.pallas_call(
        _pos_kernel,
        grid=(1,),
        in_specs=[pl.BlockSpec((2, T), lambda i: (0, 0))],
        out_specs=[pl.BlockSpec((2, T), lambda i: (0, 0)), pl.BlockSpec((8, 128), lambda i: (0, 0))],
        out_shape=[jax.ShapeDtypeStruct((2, T), jnp.int32), jax.ShapeDtypeStruct((8, 128), jnp.int32)],
        compiler_params=_cp(("arbitrary",)),
        name="route_pos",
    )(idx)


DISP_TM = 256


def _dispatch_kernel(meta_ref, pos_ref, h_ref, z_hbm, xs_hbm, sem):
    i = pl.program_id(0)

    def row_copy(src, src_row, dst_row):
        return pltpu.make_async_copy(src.at[pl.ds(src_row, 1), :], xs_hbm.at[pl.ds(dst_row, 1), :], sem)

    for r in range(DISP_TM):
        row_copy(h_ref, r, pos_ref[0, r]).start(priority=0)
        row_copy(h_ref, r, pos_ref[1, r]).start(priority=1)

    e = jnp.minimum(i, N_EXP - 1)
    pad0 = meta_ref[META_OFF, e] + meta_ref[META_CNT, e]
    npad = jnp.where(i < N_EXP, meta_ref[META_END, e] - pad0, 0)

    def zero_fill(p, c):
        row_copy(z_hbm, 0, p).start()
        return c

    lax.fori_loop(pad0, pad0 + npad, zero_fill, 0)

    for _ in range(2):
        pltpu.make_async_copy(h_ref, xs_hbm.at[pl.ds(0, DISP_TM), :], sem).wait()

    tail = meta_ref[META_NT, 0] + i
    has_tail = (i < N_EXP) & (tail < MOE_NT)

    def tail_copy():
        rows = pl.ds(pl.multiple_of(tail * MOE_TM, MOE_TM), MOE_TM)
        return pltpu.make_async_copy(z_hbm, xs_hbm.at[rows, :], sem)

    @pl.when(has_tail)
    def _():
        tail_copy().start()

    def zero_wait(p, c):
        row_copy(z_hbm, 0, p).wait()
        return c

    lax.fori_loop(pad0, pad0 + npad, zero_wait, 0)

    @pl.when(has_tail)
    def _():
        tail_copy().wait()


def _dispatch(meta, pos, h2, zrow):
    return pl.pallas_call(
        _dispatch_kernel,
        grid_spec=pltpu.PrefetchScalarGridSpec(
            num_scalar_prefetch=1,
            grid=(T // DISP_TM,),
            in_specs=[
                pl.BlockSpec((2, DISP_TM), lambda i, meta: (0, i), memory_space=pltpu.SMEM),
                pl.BlockSpec((DISP_TM, D), lambda i, meta: (i, 0)),
                pl.BlockSpec((MOE_TM, D), lambda i, meta: (0, 0)),
            ],
            out_specs=pl.BlockSpec(memory_space=pl.ANY),
            scratch_shapes=[pltpu.SemaphoreType.DMA],
        ),
        out_shape=jax.ShapeDtypeStruct((MOE_ROWS, D), F32),
        compiler_params=_cp(("arbitrary",)),
        name="dispatch",
    )(meta, pos, h2, zrow)


def _experts_kernel(meta_ref, xs_ref, wg_hbm, wu_hbm, wd_hbm, ys_ref,
                    wg_f, wu_f, wd_f, wg_s, wu_s, wd_s, sem, *, l):
    j = pl.program_id(0)
    live = j < meta_ref[META_NT, 0]
    e = meta_ref[META_TILE_E, j]
    e_prev = meta_ref[META_TILE_E, jnp.maximum(j - 1, 0)]

    def fetch(ex):
        return (pltpu.make_async_copy(wg_hbm.at[l, ex], wg_f, sem.at[0]),
                pltpu.make_async_copy(wu_hbm.at[l, ex], wu_f, sem.at[1]),
                pltpu.make_async_copy(wd_hbm.at[l, ex], wd_f, sem.at[2]))

    @pl.when(j == 0)
    def _():
        for c in fetch(e):
            c.start()

    @pl.when(live & ((j == 0) | (e != e_prev)))
    def _():
        for c, dst, src in zip(fetch(e), (wg_s, wu_s, wd_s), (wg_f, wu_f, wd_f)):
            c.wait()
            dst[...] = src[...].astype(BF16)
        nxt = meta_ref[META_NEXT_E, e]

        @pl.when(nxt >= 0)
        def _():
            for c in fetch(nxt):
                c.start()

    @pl.when(live)
    def _():
        x = xs_ref[...].astype(BF16)
        g = _dot(x, wg_s[...])
        u = _dot(x, wu_s[...])
        act = (g * _sigmoid(g)) * u
        ys_ref[...] = _dot(act.astype(BF16), wd_s[...])

    @pl.when(jnp.logical_not(live))
    def _():
        ys_ref[...] = jnp.zeros_like(ys_ref)


def _experts(meta, xs, w_gate_e, w_up_e, w_down_e, l):
    def tile(j, meta):
        return jnp.minimum(j, meta[META_NT, 0] - 1)

    return pl.pallas_call(
        functools.partial(_experts_kernel, l=l),
        grid_spec=pltpu.PrefetchScalarGridSpec(
            num_scalar_prefetch=1,
            grid=(MOE_NT,),
            in_specs=[
                pl.BlockSpec((MOE_TM, D), lambda j, meta: (tile(j, meta), 0)),
                pl.BlockSpec(memory_space=pl.ANY),
                pl.BlockSpec(memory_space=pl.ANY),
                pl.BlockSpec(memory_space=pl.ANY),
            ],
            out_specs=pl.BlockSpec((MOE_TM, D), lambda j, meta: (j, 0)),
            scratch_shapes=[pltpu.VMEM((D, D_EXP), F32), pltpu.VMEM((D, D_EXP), F32), pltpu.VMEM((D_EXP, D), F32),
                            pltpu.VMEM((D, D_EXP), BF16), pltpu.VMEM((D, D_EXP), BF16), pltpu.VMEM((D_EXP, D), BF16),
                            pltpu.SemaphoreType.DMA((3,))],
        ),
        out_shape=jax.ShapeDtypeStruct((MOE_ROWS, D), F32),
        compiler_params=_cp(("arbitrary",)),
        name="experts",
    )(meta, xs, w_gate_e, w_up_e, w_down_e)


COMB_TM = SEG


def _combine_kernel(pos_ref, w_ref, x1_ref, mod_ref, fg_ref, ys_hbm, oa_ref, ob_ref, buf, sem, *, final):
    i = pl.program_id(0)
    n = pl.num_programs(0) - 1
    n_ctx = T_CTX // COMB_TM

    for s in range(2):
        @pl.when((i < n) & (lax.rem(i, 2) == s))
        def _():
            for r in range(COMB_TM):
                for k in range(2):
                    pltpu.make_async_copy(ys_hbm.at[pl.ds(pos_ref[k, r], 1), :],
                                          buf.at[s, k, pl.ds(r, 1), :], sem.at[s]).start(priority=k)

    @pl.when(i > 0)
    def _():
        slot = lax.rem(i - 1, 2)
        for k in range(2):
            pltpu.make_async_copy(ys_hbm.at[pl.ds(0, COMB_TM), :], buf.at[slot, k], sem.at[slot]).wait()
        w = w_ref[...]
        y = w[:, 0:1] * buf[slot, 0] + w[:, 1:2] * buf[slot, 1]
        x = x1_ref[...] + mod_ref[5:6, :] * y
        if final:
            ms = jnp.mean(x * x, axis=-1, keepdims=True)
            x = x * lax.rsqrt(ms + EPS) * fg_ref[...]

        @pl.when(i - 1 < n_ctx)
        def _():
            oa_ref[...] = x

        @pl.when(i - 1 >= n_ctx)
        def _():
            ob_ref[...] = x


def _combine(pos, wts_t, x1, modseg, final_g, ys, l, final):
    n = T // COMB_TM
    n_ctx = T_CTX // COMB_TM

    def done(i):
        return jnp.maximum(i - 1, 0)

    return pl.pallas_call(
        functools.partial(_combine_kernel, final=final),
        grid=(n + 1,),
        in_specs=[
            pl.BlockSpec((2, COMB_TM), lambda i: (0, jnp.minimum(i, n - 1)), memory_space=pltpu.SMEM),
            pl.BlockSpec((COMB_TM, 2), lambda i: (done(i), 0)),
            pl.BlockSpec((COMB_TM, D), lambda i: (done(i), 0)),
            pl.BlockSpec((None, None, 8, D), lambda i: (l, done(i), 0, 0)),
            pl.BlockSpec((1, D), lambda i: (0, 0)),
            pl.BlockSpec(memory_space=pl.ANY),
        ],
        out_specs=[pl.BlockSpec((COMB_TM, D), lambda i: (jnp.minimum(done(i), n_ctx - 1), 0)),
                   pl.BlockSpec((COMB_TM, D), lambda i: (jnp.maximum(done(i) - n_ctx, 0), 0))],
        out_shape=[jax.ShapeDtypeStruct((T_CTX, D), F32), jax.ShapeDtypeStruct((T_LAT, D), F32)],
        scratch_shapes=[pltpu.VMEM((2, 2, COMB_TM, D), F32), pltpu.SemaphoreType.DMA((2,))],
        compiler_params=_cp(("arbitrary",)),
        name="combine",
    )(pos, wts_t, x1, modseg, final_g.reshape(1, D), ys)


def _rope_tables():
    n = DEC_SEQ
    pos_row = np.repeat(np.arange(n // GRID_W, dtype=np.float32), GRID_W)
    pos_col = np.tile(np.arange(GRID_W, dtype=np.float32), n // GRID_W)
    half = HD // 2
    inv_freq = jnp.asarray(ROPE_THETA, F32) ** (-jnp.arange(0, half, 2, dtype=F32) / half)
    ang = jnp.concatenate([jnp.asarray(pos_row)[:, None] * inv_freq,
                           jnp.asarray(pos_col)[:, None] * inv_freq], axis=-1)
    cos, sin = jnp.cos(ang), jnp.sin(ang)
    cos128 = jnp.tile(cos, (1, 4))
    sin128 = jnp.tile(jnp.concatenate([-sin, sin], axis=-1), (1, 2))
    return cos128, sin128


def _head_mean_matrix():
    idx = np.arange(2 * HD)
    same = (idx[:, None] // HD) == (idx[None, :] // HD)
    return jnp.asarray(same.astype(np.float32) / HD, BF16)


_SEG_ROWS = np.array([0] * (T_CTX // SEG) + [1 + b for b in range(DEC_BATCH) for _ in range(DEC_SEQ // SEG)])


def kernel(x_prompt, x_sample, cache_k, cache_v, state_rec, c, c_ctx, w_mod, b_mod, norm1_g, norm2_g, w_in, conv_w, conv_b, rg_wa, rg_ba, rg_wx, rg_bx, rg_lambda, q_norm_g, k_norm_g, w_rec_out, w_att_out, w_out, w_router, router_bias, w_gate_e, w_up_e, w_down_e, final_g):
    xa, xb = x_prompt.reshape(T_CTX, D), x_sample.reshape(T_LAT, D)

    cvecs = jnp.concatenate([c_ctx[None, :], c, jnp.zeros((3, D), F32)], axis=0)
    mods = _mods(cvecs, w_mod, b_mod).reshape(DEPTH, 8, 6, D)
    modseg = jnp.pad(mods[:, _SEG_ROWS], ((0, 0), (0, 0), (0, 2), (0, 0)))

    cos128, sin128 = _rope_tables()
    bd = _head_mean_matrix()
    qg128 = jnp.tile(q_norm_g, (1, 2)).reshape(DEPTH, 1, 2 * HD)
    kg128 = jnp.tile(k_norm_g, (1, 2)).reshape(DEPTH, 1, 2 * HD)
    wg = jnp.concatenate([rg_wa[:, 0], rg_wx[:, 0], rg_wa[:, 1], rg_wx[:, 1]], axis=-1)
    pvec = jnp.stack([rg_ba[:, 0], rg_bx[:, 0], rg_ba[:, 1], rg_bx[:, 1],
                      rg_lambda[:, 0], rg_lambda[:, 1], conv_b, jnp.zeros_like(conv_b)], axis=1)
    wrt = w_router.T
    rbias = router_bias.reshape(N_EXP, 1)
    zrow = jnp.zeros((MOE_TM, D), F32)

    new_k, new_v, new_s = [], [], []
    for l in range(DEPTH):
        proj = _inproj(xa, xb, modseg, norm1_g, w_in, l)
        h0 = jnp.concatenate([jnp.zeros((T_CTX // UNIT, 2, D), F32), state_rec[:, l]], axis=0)
        yrec, stf, stb = _rec(proj, conv_w, pvec, wg, h0, l)
        qc, kc, vc = _qkv(proj, qg128, kg128, cos128, sin128, bd, l, latent=False)
        ql, kl, vl = _qkv(proj, qg128, kg128, cos128, sin128, bd, l, latent=True)
        o_ctx = _attn_ctx(qc, kc, vc)
        o_lat = _attn_lat(ql, cache_k, cache_v, kl, vl, l)
        x1, h2, idx, wts = _merge(yrec, o_ctx, o_lat, proj, xa, xb, modseg, norm2_g,
                                  w_rec_out, w_att_out, w_out, wrt, rbias, l)
        pos, meta = _route_pos(idx)
        xs = _dispatch(meta, pos, h2, zrow)
        ys = _experts(meta, xs, w_gate_e, w_up_e, w_down_e, l)
        xa, xb = _combine(pos, wts.T, x1, modseg, final_g, ys, l, final=(l == DEPTH - 1))
        new_k.append(kc)
        new_v.append(vc)
        n_cu = T_CTX // UNIT
        spu = UNIT // SEQ
        hf_last = stf[:n_cu].reshape(n_cu, spu, 2, D)[:, :, 1].reshape(BATCH, D)
        hb_first = stb[:n_cu].reshape(n_cu, spu, 2, D)[:, :, 0].reshape(BATCH, D)
        new_s.append(jnp.stack([hf_last, hb_first], axis=1))

    y_prompt = xa.reshape(BATCH, SEQ, D)
    y_sample = xb.reshape(DEC_BATCH, DEC_SEQ, D)
    return (y_prompt, y_sample, jnp.stack(new_k, axis=1), jnp.stack(new_v, axis=1), jnp.stack(new_s, axis=1))
```

```python
import functools

import numpy as np
import jax
import jax.numpy as jnp
from jax import lax
from jax.experimental import pallas as pl
from jax.experimental.pallas import tpu as pltpu

F32 = jnp.float32
BF16 = jnp.bfloat16

D = 1024
BATCH = 16
SEQ = 256
DEPTH = 2
DEC_BATCH = 4
DEC_SEQ = 1024
PAST = 256
GRID_W = 64
N_HEADS = 16
N_KV = 4
HD = 64
N_RG_BLK = 8
RG_BLK = 128
RG_C = 8.0
N_EXP = 16
D_EXP = 512
ROPE_THETA = 10000.0
EPS = 1e-6
P_IN = 5632
TINY = float(np.finfo(np.float32).tiny)
NEG_LOG2E = -float(np.log2(np.e))

T_CTX = BATCH * SEQ
T_LAT = DEC_BATCH * DEC_SEQ
T = T_CTX + T_LAT
SEG = 256
N_SEG = T // SEG
UNIT = 1024
N_UNIT = T // UNIT
CHUNK = UNIT // 8
CSTRIDE = CHUNK + 8

VMEM_LIMIT = 56 * 1024 * 1024


def _cp(sem):
    return pltpu.CompilerParams(dimension_semantics=sem, vmem_limit_bytes=VMEM_LIMIT)


def _split(x):
    hi = x.astype(BF16)
    lo = (x - hi.astype(F32)).astype(BF16)
    return hi, lo


def _sigmoid(x):
    return 0.5 * jnp.tanh(0.5 * x) + 0.5


NCH = D // 128


def _store_row_tiles(ref, x):
    n = x.shape[0]
    for c in range(NCH):
        ref[pl.ds(c, n, stride=NCH), :] = x[:, c * 128:(c + 1) * 128]


def _load_row_tiles(ref, n):
    return jnp.concatenate([ref[pl.ds(c, n, stride=NCH), :] for c in range(NCH)], axis=-1)


def _dot(a, b):
    return jnp.dot(a, b, preferred_element_type=F32)


def _dot_nt(a, b):
    return lax.dot_general(a, b, (((1,), (1,)), ((), ())), preferred_element_type=F32)


def _mods_kernel(c_ref, w_ref, b_ref, o_ref):
    c = c_ref[...]
    s = c * jax.nn.sigmoid(c)
    s_hi, s_lo = _split(s)
    w_hi, w_lo = _split(w_ref[...])
    o_ref[...] = _dot(s_hi, w_hi) + _dot(s_hi, w_lo) + _dot(s_lo, w_hi) + b_ref[...]


def _mods(cvecs, w_mod, b_mod):
    tn = 1536
    return pl.pallas_call(
        _mods_kernel,
        grid=(DEPTH, 6 * D // tn),
        in_specs=[
            pl.BlockSpec((8, D), lambda l, j: (0, 0)),
            pl.BlockSpec((None, D, tn), lambda l, j: (l, 0, j)),
            pl.BlockSpec((None, 1, tn), lambda l, j: (l, 0, j)),
        ],
        out_specs=pl.BlockSpec((None, 8, tn), lambda l, j: (l, 0, j)),
        out_shape=jax.ShapeDtypeStruct((DEPTH, 8, 6 * D), F32),
        compiler_params=_cp(("arbitrary", "arbitrary")),
        name="mods",
    )(cvecs, w_mod, b_mod.reshape(DEPTH, 1, 6 * D))


def _norm_mod(x, g, shift, scale):
    ms = jnp.mean(x * x, axis=-1, keepdims=True)
    return x * lax.rsqrt(ms + EPS) * g * (1.0 + scale) + shift


def _two_part_specs(tm, n_ctx):
    return [pl.BlockSpec((tm, D), lambda i, *_: (jnp.minimum(i, n_ctx - 1), 0)),
            pl.BlockSpec((tm, D), lambda i, *_: (jnp.maximum(i - n_ctx, 0), 0))]


def _inproj_kernel(xa_ref, xb_ref, mod_ref, g_ref, w_ref, o_ref, h_ref, *, tm):
    def prologue(x_ref):
        def seg(s, carry):
            r0 = pl.multiple_of(s * SEG, SEG)
            m = mod_ref[s]
            h = _norm_mod(x_ref[pl.ds(r0, SEG), :], g_ref[...], m[0:1, :], m[1:2, :])
            h_ref[pl.ds(r0, SEG), :] = h.astype(BF16)
            return carry
        lax.fori_loop(0, tm // SEG, seg, 0)

    first = pl.program_id(1) == 0
    is_ctx = pl.program_id(0) < T_CTX // tm

    @pl.when(first & is_ctx)
    def _():
        prologue(xa_ref)

    @pl.when(first & jnp.logical_not(is_ctx))
    def _():
        prologue(xb_ref)

    o_ref[...] = _dot(h_ref[...], w_ref[...].astype(BF16)).astype(BF16)


def _inproj(xa, xb, modseg, norm_g, w_in, l):
    tm, tn = 2048, 512
    return pl.pallas_call(
        functools.partial(_inproj_kernel, tm=tm),
        grid=(T // tm, P_IN // tn),
        in_specs=_two_part_specs(tm, T_CTX // tm) + [
            pl.BlockSpec((None, tm // SEG, 8, D), lambda i, j: (l, i, 0, 0)),
            pl.BlockSpec((None, 1, D), lambda i, j: (l, 0, 0)),
            pl.BlockSpec((None, D, tn), lambda i, j: (l, 0, j)),
        ],
        out_specs=pl.BlockSpec((tm, tn), lambda i, j: (i, j)),
        out_shape=jax.ShapeDtypeStruct((T, P_IN), BF16),
        scratch_shapes=[pltpu.VMEM((tm, D), BF16)],
        compiler_params=_cp(("arbitrary", "arbitrary")),
        name="inproj",
    )(xa, xb, modseg, norm_g.reshape(DEPTH, 1, D), w_in)


REC_CW = 512
HALO = 16


def _rec_kernel(xr_ref, gate_ref, cw_ref, pv_ref, wg_ref, h0_ref,
                y_ref, stf_ref, stb_ref,
                af_ref, bf_ref, ab_ref, bb_ref, wgh_ref):
    u = pl.program_id(0)
    is_ctx = u < (T_CTX // UNIT)
    seq_len = jnp.where(is_ctx, SEQ, DEC_SEQ)
    nblk = REC_CW // RG_BLK

    pv = pv_ref[...]
    cwts = cw_ref[...]
    conv_b = pv[6:7, :]

    def softplus_neg(lam):
        z = -lam
        return jnp.maximum(z, 0.0) + jnp.log1p(jnp.exp(-jnp.abs(z)))

    c4s = tuple((0.5 * RG_C) * softplus_neg(pv[4 + d:5 + d, :]) for d in range(2))
    pv_h = 0.5 * pv
    for n in range(nblk):
        wgh_ref[n] = (0.5 * wg_ref[n]).astype(BF16)
    a_refs = (af_ref, ab_ref)
    b_refs = (bf_ref, bb_ref)

    def gates(ci, carry):
        base = pl.multiple_of(ci * CHUNK, CHUNK)
        lo = pl.multiple_of(jnp.maximum(base - HALO, 0), HALO)
        hi = pl.multiple_of(jnp.minimum(base + CHUNK, UNIT - HALO), HALO)
        main = xr_ref[pl.ds(base, CHUNK), :].astype(F32)
        win = jnp.concatenate([xr_ref[pl.ds(lo, HALO), :].astype(F32), main,
                               xr_ref[pl.ds(hi, HALO), :].astype(F32)], axis=0)
        n_win = CHUNK + 2 * HALO
        row8 = lax.broadcasted_iota(jnp.int32, (8, 1), 0)
        tl_head = jnp.bitwise_and(base + row8, seq_len - 1)
        tl_tail = jnp.bitwise_and(base + (CHUNK - 8) + row8, seq_len - 1)

        def tap(shift, head_ok=None, tail_ok=None):
            x = pltpu.roll(win, shift, 0)[HALO:HALO + CHUNK]
            if head_ok is not None:
                return jnp.concatenate([jnp.where(head_ok, x[:8], 0.0), x[8:]], axis=0)
            return jnp.concatenate([x[:CHUNK - 8], jnp.where(tail_ok, x[CHUNK - 8:], 0.0)], axis=0)

        xm2 = tap(2, head_ok=tl_head >= 2)
        xm1 = tap(1, head_ok=tl_head >= 1)
        xp1 = tap(n_win - 1, tail_ok=tl_tail <= seq_len - 2)
        xc = conv_b + xm2 * cwts[0:1, :]
        xc = xc + xm1 * cwts[1:2, :]
        xc = xc + main * cwts[2:3, :]
        xc = xc + xp1 * cwts[3:4, :]
        for n in range(nblk):
            ls = slice(n * RG_BLK, (n + 1) * RG_BLK)
            xn = xc[:, ls]
            hx = 0.5 * xn
            pre_h = _dot(xn.astype(BF16), wgh_ref[n])
            for d in range(2):
                th_r = jnp.tanh(pre_h[:, (2 * d) * RG_BLK:(2 * d + 1) * RG_BLK] + pv_h[2 * d:2 * d + 1, ls])
                th_i = jnp.tanh(pre_h[:, (2 * d + 1) * RG_BLK:(2 * d + 2) * RG_BLK] + pv_h[2 * d + 1:2 * d + 2, ls])
                c4 = c4s[d][:, ls]
                nla = c4 * th_r + c4
                a = jnp.exp2(nla * NEG_LOG2E)
                s = jnp.tanh(nla) * (a * a + 1.0)
                inp = (s * lax.rsqrt(jnp.maximum(s, TINY))) * (hx * th_i + hx)
                sbase = pl.multiple_of(ci * CSTRIDE, 8)
                a_refs[d][n, pl.ds(sbase, CHUNK), :] = a
                b_refs[d][n, pl.ds(sbase, CHUNK), :] = inp
        return carry

    lax.fori_loop(0, 8, gates, 0)

    zeros = [jnp.zeros((8, RG_BLK), F32)] * nblk
    ones = [jnp.ones((8, RG_BLK), F32)] * nblk
    hf, pf, hb, pb = list(zeros), list(ones), list(zeros), list(ones)
    for r in range(CHUNK):
        rows_f = pl.ds(r, 8, stride=CSTRIDE)
        rows_b = pl.ds(CHUNK - 1 - r, 8, stride=CSTRIDE)
        for n in range(nblk):
            a = af_ref[n, rows_f, :]
            hf[n] = a * hf[n] + bf_ref[n, rows_f, :]
            pf[n] = a * pf[n]
            bf_ref[n, rows_f, :] = hf[n]
            af_ref[n, rows_f, :] = pf[n]
            a = ab_ref[n, rows_b, :]
            hb[n] = a * hb[n] + bb_ref[n, rows_b, :]
            pb[n] = a * pb[n]
            bb_ref[n, rows_b, :] = hb[n]
            ab_ref[n, rows_b, :] = pb[n]
    hf, pf, hb, pb = (jnp.concatenate(v, axis=-1) for v in (hf, pf, hb, pb))

    cps = jnp.where(is_ctx, SEQ // CHUNK, DEC_SEQ // CHUNK)
    h0f = h0_ref[0:1, :]
    h0b = h0_ref[1:2, :]
    cf = [h0f]
    for c in range(1, 8):
        chain = hf[c - 1:c, :] + pf[c - 1:c, :] * cf[c - 1]
        cf.append(jnp.where(jnp.bitwise_and(c, cps - 1) == 0, h0f, chain))
    cb = [None] * 8
    cb[7] = h0b
    for c in range(6, -1, -1):
        chain = hb[c + 1:c + 2, :] + pb[c + 1:c + 2, :] * cb[c + 1]
        cb[c] = jnp.where(jnp.bitwise_and(c, cps - 1) == cps - 1, h0b, chain)
    carry_f = jnp.concatenate(cf, axis=0)
    carry_b = jnp.concatenate(cb, axis=0)
    stf_ref[...] = hf + pf * carry_f
    stb_ref[...] = hb + pb * carry_b

    for ci in range(8):
        rows = pl.ds(ci * CHUNK, CHUNK)
        srows = pl.ds(ci * CSTRIDE, CHUNK)
        for n in range(nblk):
            ls = slice(n * RG_BLK, (n + 1) * RG_BLK)
            h_f = bf_ref[n, srows, :] + af_ref[n, srows, :] * carry_f[ci:ci + 1, ls]
            h_b = bb_ref[n, srows, :] + ab_ref[n, srows, :] * carry_b[ci:ci + 1, ls]
            g = gate_ref[rows, ls].astype(F32)
            y_ref[rows, ls] = ((h_f + h_b) * jax.nn.gelu(g, approximate=True)).astype(BF16)


def _rec(proj, conv_w, pvec, wg, h0, l):
    ncb = D // REC_CW
    return pl.pallas_call(
        _rec_kernel,
        grid=(N_UNIT, ncb),
        in_specs=[
            pl.BlockSpec((UNIT, REC_CW), lambda u, c: (u, c)),
            pl.BlockSpec((UNIT, REC_CW), lambda u, c: (u, ncb + c)),
            pl.BlockSpec((None, 4, REC_CW), lambda u, c: (l, 0, c)),
            pl.BlockSpec((None, 8, REC_CW), lambda u, c: (l, 0, c)),
            pl.BlockSpec((None, REC_CW // RG_BLK, RG_BLK, 4 * RG_BLK), lambda u, c: (l, c, 0, 0)),
            pl.BlockSpec((None, 2, REC_CW), lambda u, c: (u, 0, c)),
        ],
        out_specs=[
            pl.BlockSpec((UNIT, REC_CW), lambda u, c: (u, c)),
            pl.BlockSpec((None, 8, REC_CW), lambda u, c: (u, 0, c)),
            pl.BlockSpec((None, 8, REC_CW), lambda u, c: (u, 0, c)),
        ],
        out_shape=[
            jax.ShapeDtypeStruct((T, D), BF16),
            jax.ShapeDtypeStruct((N_UNIT, 8, D), F32),
            jax.ShapeDtypeStruct((N_UNIT, 8, D), F32),
        ],
        scratch_shapes=[pltpu.VMEM((REC_CW // RG_BLK, 8 * CSTRIDE, RG_BLK), F32)] * 4
        + [pltpu.VMEM((REC_CW // RG_BLK, RG_BLK, 4 * RG_BLK), BF16)],
        compiler_params=_cp(("arbitrary", "arbitrary")),
        name="rec",
    )(proj, proj, conv_w, pvec, wg, h0)


def _head_norm(x, g128, bd):
    hi, lo = _split(x * x)
    ms = _dot(hi, bd) + _dot(lo, bd)
    return x * lax.rsqrt(ms + EPS) * g128


def _rope(x, cos, sin_signed):
    lane = lax.broadcasted_iota(jnp.int32, x.shape, 1)
    first_half = jnp.bitwise_and(lane, HD - 1) < HD // 2
    partner = jnp.where(first_half, pltpu.roll(x, 2 * HD - HD // 2, 1), pltpu.roll(x, HD // 2, 1))
    return x * cos + partner * sin_signed


def _qkv_kernel(q_ref, k_ref, v_ref, qg_ref, kg_ref, cos_ref, sin_ref, bd_ref,
                qo_ref, ko_ref, vo_ref, *, rope):
    bd = bd_ref[...]
    scale = HD ** -0.5 * float(np.log2(np.e))
    for j in range(N_HEADS // 2):
        x = _head_norm(q_ref[:, 2 * HD * j:2 * HD * (j + 1)].astype(F32), qg_ref[...], bd)
        if rope:
            x = _rope(x, cos_ref[...], sin_ref[...])
        x = x * scale
        qo_ref[2 * j] = x[:, :HD].astype(qo_ref.dtype)
        qo_ref[2 * j + 1] = x[:, HD:].astype(qo_ref.dtype)
    for j in range(N_KV // 2):
        x = _head_norm(k_ref[:, 2 * HD * j:2 * HD * (j + 1)].astype(F32), kg_ref[...], bd)
        if rope:
            x = _rope(x, cos_ref[...], sin_ref[...])
        ko_ref[2 * j] = x[:, :HD].astype(ko_ref.dtype)
        ko_ref[2 * j + 1] = x[:, HD:].astype(ko_ref.dtype)
        v = v_ref[:, 2 * HD * j:2 * HD * (j + 1)].astype(F32)
        vo_ref[2 * j] = v[:, :HD].astype(vo_ref.dtype)
        vo_ref[2 * j + 1] = v[:, HD:].astype(vo_ref.dtype)


def _qkv(proj, qg128, kg128, cos128, sin128, bd, l, latent):
    tm = SEG
    n = T_LAT // tm if latent else T_CTX // tm
    roff = T_CTX // tm if latent else 0
    per_seq = DEC_SEQ // tm
    if latent:
        kv_shape = (DEC_BATCH, N_KV, DEC_SEQ, HD)
        kv_spec = pl.BlockSpec((None, N_KV, tm, HD), lambda i: (i // per_seq, 0, i % per_seq, 0))
        kv_dtype = BF16
        tab_map = lambda i: (i % per_seq, 0)
    else:
        kv_shape = (BATCH, N_KV, SEQ, HD)
        kv_spec = pl.BlockSpec((None, N_KV, tm, HD), lambda i: (i, 0, 0, 0))
        kv_dtype = F32
        tab_map = lambda i: (0, 0)
    return pl.pallas_call(
        functools.partial(_qkv_kernel, rope=latent),
        grid=(n,),
        in_specs=[
            pl.BlockSpec((tm, D), lambda i: (roff + i, 2)),
            pl.BlockSpec((tm, N_KV * HD), lambda i: (roff + i, 3 * D // (N_KV * HD))),
            pl.BlockSpec((tm, N_KV * HD), lambda i: (roff + i, 3 * D // (N_KV * HD) + 1)),
            pl.BlockSpec((None, 1, 2 * HD), lambda i: (l, 0, 0)),
            pl.BlockSpec((None, 1, 2 * HD), lambda i: (l, 0, 0)),
            pl.BlockSpec((tm, 2 * HD), tab_map),
            pl.BlockSpec((tm, 2 * HD), tab_map),
            pl.BlockSpec((2 * HD, 2 * HD), lambda i: (0, 0)),
        ],
        out_specs=[
            pl.BlockSpec((N_HEADS, tm, HD), lambda i: (0, i, 0)),
            kv_spec,
            kv_spec,
        ],
        out_shape=[
            jax.ShapeDtypeStruct((N_HEADS, n * tm, HD), BF16),
            jax.ShapeDtypeStruct(kv_shape, kv_dtype),
            jax.ShapeDtypeStruct(kv_shape, kv_dtype),
        ],
        compiler_params=_cp(("arbitrary",)),
        name="qkv_lat" if latent else "qkv_ctx",
    )(proj, proj, proj, qg128, kg128, cos128, sin128, bd)


def _with_ones(v):
    return jnp.concatenate([v, jnp.ones_like(v)], axis=-1)


def _softmax_pv(q, k, v_ext):
    s = _dot_nt(q, k)
    m = jnp.max(s, axis=-1, keepdims=True)
    p = jnp.exp2(s - m).astype(BF16)
    r = _dot(p, v_ext)
    return r[:, :HD] / r[:, HD:HD + 1]


def _attend_heads(q_ref, k, v_ext):
    return jnp.concatenate([_softmax_pv(q_ref[h], k, v_ext) for h in range(N_HEADS // N_KV)], axis=-1)


def _attn_ctx_kernel(q_ref, k_ref, v_ref, o_ref):
    g = N_HEADS // N_KV
    q = q_ref[...].reshape(g * SEQ, HD)
    o = _softmax_pv(q, k_ref[...].astype(BF16), _with_ones(v_ref[...].astype(BF16)))
    o = jnp.concatenate([o[h * SEQ:(h + 1) * SEQ] for h in range(g)], axis=-1)
    o_ref[...] = o.astype(BF16)


def _attn_ctx(qh, kc, vc):
    g = N_HEADS // N_KV
    return pl.pallas_call(
        _attn_ctx_kernel,
        grid=(BATCH, N_KV),
        in_specs=[
            pl.BlockSpec((g, SEQ, HD), lambda b, h: (h, b, 0)),
            pl.BlockSpec((None, None, SEQ, HD), lambda b, h: (b, h, 0, 0)),
            pl.BlockSpec((None, None, SEQ, HD), lambda b, h: (b, h, 0, 0)),
        ],
        out_specs=pl.BlockSpec((SEQ, g * HD), lambda b, h: (b, h)),
        out_shape=jax.ShapeDtypeStruct((T_CTX, D), BF16),
        compiler_params=_cp(("arbitrary", "arbitrary")),
        name="attn_ctx",
    )(qh, kc, vc)


ATT_TQ = 1024


def _attn_lat_kernel(q_ref, pk_ref, pv_ref, k_ref, v_ref, o_ref, k_s, v_s):
    k_s[0:PAST, :] = pk_ref[...].astype(BF16)
    k_s[PAST:, :] = k_ref[...]
    v_s[0:PAST, :] = _with_ones(pv_ref[...].astype(BF16))
    v_s[PAST:, :] = _with_ones(v_ref[...])

    def q_tile(qi, carry):
        rows = pl.ds(pl.multiple_of(qi * ATT_TQ, ATT_TQ), ATT_TQ)
        o = jnp.concatenate([_softmax_pv(q_ref[h, rows, :], k_s[...], v_s[...])
                             for h in range(N_HEADS // N_KV)], axis=-1)
        o_ref[rows, :] = o.astype(BF16)
        return carry

    lax.fori_loop(0, DEC_SEQ // ATT_TQ, q_tile, 0)


def _attn_lat(qh, cache_k, cache_v, kr, vr, l):
    g = N_HEADS // N_KV
    return pl.pallas_call(
        _attn_lat_kernel,
        grid=(DEC_BATCH, N_KV),
        in_specs=[
            pl.BlockSpec((g, DEC_SEQ, HD), lambda b, h: (h, b, 0)),
            pl.BlockSpec((None, None, None, PAST, HD), lambda b, h: (b, l, h, 0, 0)),
            pl.BlockSpec((None, None, None, PAST, HD), lambda b, h: (b, l, h, 0, 0)),
            pl.BlockSpec((None, None, DEC_SEQ, HD), lambda b, h: (b, h, 0, 0)),
            pl.BlockSpec((None, None, DEC_SEQ, HD), lambda b, h: (b, h, 0, 0)),
        ],
        out_specs=pl.BlockSpec((DEC_SEQ, g * HD), lambda b, h: (b, h)),
        out_shape=jax.ShapeDtypeStruct((T_LAT, D), BF16),
        scratch_shapes=[pltpu.VMEM((PAST + DEC_SEQ, HD), BF16), pltpu.VMEM((PAST + DEC_SEQ, 2 * HD), BF16)],
        compiler_params=_cp(("arbitrary", "arbitrary")),
        name="attn_lat",
    )(qh, cache_k, cache_v, kr, vr)


MERGE_TM = 512


def _route(lt, bias):
    rows = [lt[e:e + 1, :] for e in range(N_EXP)]
    m = rows[0]
    for e in range(1, N_EXP):
        m = jnp.maximum(m, rows[e])
    ex = [jnp.exp(r - m) for r in rows]
    z = ex[0]
    for e in range(1, N_EXP):
        z = z + ex[e]
    probs = [x / z for x in ex]
    sel = [probs[e] + bias[e:e + 1, :] for e in range(N_EXP)]

    def top2_sum(v):
        a, b = jnp.maximum(v[0], v[1]), jnp.minimum(v[0], v[1])
        c, d = jnp.maximum(v[2], v[3]), jnp.minimum(v[2], v[3])
        return jnp.maximum(a, c) + jnp.maximum(jnp.minimum(a, c), jnp.maximum(b, d))

    scores = [top2_sum(sel[4 * g:4 * g + 4]) for g in range(4)]
    best = jnp.zeros_like(scores[0], dtype=jnp.int32)
    best_s = scores[0]
    for g in range(1, 4):
        take = scores[g] > best_s
        best = jnp.where(take, g, best)
        best_s = jnp.where(take, scores[g], best_s)
    cs, cp = [], []
    for j in range(4):
        s_j, p_j = sel[j], probs[j]
        for g in range(1, 4):
            s_j = jnp.where(best == g, sel[4 * g + j], s_j)
            p_j = jnp.where(best == g, probs[4 * g + j], p_j)
        cs.append(s_j)
        cp.append(p_j)
    neg = jnp.full_like(cs[0], -jnp.inf)

    def argmax4(v):
        bi = jnp.zeros_like(best)
        bv = v[0]
        for j in range(1, 4):
            take = v[j] > bv
            bi = jnp.where(take, j, bi)
            bv = jnp.where(take, v[j], bv)
        return bi

    def pick(v, idx):
        out = v[0]
        for j in range(1, 4):
            out = jnp.where(idx == j, v[j], out)
        return out

    i1 = argmax4(cs)
    cs2 = [jnp.where(i1 == j, neg, cs[j]) for j in range(4)]
    i2 = argmax4(cs2)
    i2 = jnp.where((i2 == 0) & (i1 == 0), 1, i2)
    w1, w2 = pick(cp, i1), pick(cp, i2)
    den = w1 + w2
    return best * 4 + i1, best * 4 + i2, w1 / den, w2 / den


def _merge_kernel(yrec_ref, oa_ref, ob_ref, gr0_ref, gr1_ref, ga0_ref, ga1_ref, xa_ref, xb_ref, mod_ref, g2_ref,
                  wrec_ref, watt_ref, wout_ref, wrt_ref, rb_ref,
                  x1_ref, h2_ref, idx_ref, wts_ref,
                  wrec_s, watt_s, wout_s):
    @pl.when(pl.program_id(0) == 0)
    def _():
        wrec_s[...] = wrec_ref[...].astype(BF16)
        watt_s[...] = watt_ref[...].astype(BF16)
        wout_s[...] = wout_ref[...].astype(BF16)

    is_ctx = pl.program_id(0) < T_CTX // MERGE_TM
    args = (yrec_ref, gr0_ref, gr1_ref, ga0_ref, ga1_ref, mod_ref, g2_ref, wrt_ref, rb_ref,
            x1_ref, h2_ref, idx_ref, wts_ref, wrec_s, watt_s, wout_s)

    @pl.when(is_ctx)
    def _():
        _merge_body(oa_ref, xa_ref, *args)

    @pl.when(jnp.logical_not(is_ctx))
    def _():
        _merge_body(ob_ref, xb_ref, *args)


def _merge_body(oatt_ref, x_ref, yrec_ref, gr0_ref, gr1_ref, ga0_ref, ga1_ref, mod_ref, g2_ref, wrt_ref, rb_ref,
                x1_ref, h2_ref, idx_ref, wts_ref, wrec_s, watt_s, wout_s):
    half = D // 2
    b_rec = _dot(yrec_ref[...], wrec_s[...])
    b_att = _dot(oatt_ref[...], watt_s[...])
    m0 = _sigmoid(gr0_ref[...].astype(F32)) * b_rec[:, :half] + _sigmoid(ga0_ref[...].astype(F32)) * b_att[:, :half]
    m1 = _sigmoid(gr1_ref[...].astype(F32)) * b_rec[:, half:] + _sigmoid(ga1_ref[...].astype(F32)) * b_att[:, half:]
    merged = jnp.concatenate([m0, m1], axis=-1).astype(BF16)
    out = _dot(merged, wout_s[...])

    hs = []
    for s in range(MERGE_TM // SEG):
        rows = slice(s * SEG, (s + 1) * SEG)
        m = mod_ref[s]
        x1 = x_ref[rows, :] + m[2:3, :] * out[rows, :]
        x1_ref[rows, :] = x1
        h2 = _norm_mod(x1, g2_ref[...], m[3:4, :], m[4:5, :])
        hs.append(h2)
    h2 = jnp.concatenate(hs, axis=0)
    _store_row_tiles(h2_ref, h2)

    h_hi, h_lo = _split(h2)
    w_hi, w_lo = _split(wrt_ref[...])
    lt = _dot_nt(w_hi, h_hi) + _dot_nt(w_hi, h_lo) + _dot_nt(w_lo, h_hi)
    e1, e2, w1, w2 = _route(lt, rb_ref[...])
    idx_ref[...] = jnp.concatenate([e1, e2], axis=0)
    wts_ref[...] = jnp.concatenate([w1, w2], axis=0)


def _merge(yrec, o_ctx, o_lat, proj, xa, xb, modseg, norm2_g, w_rec_out, w_att_out, w_out, wrt, rbias, l):
    tm = MERGE_TM
    half = D // 2
    gcol = (3 * D + 2 * N_KV * HD) // half
    wspec = pl.BlockSpec((None, D, D), lambda i: (l, 0, 0))
    return pl.pallas_call(
        _merge_kernel,
        grid=(T // tm,),
        in_specs=[pl.BlockSpec((tm, D), lambda i: (i, 0))] + _two_part_specs(tm, T_CTX // tm) + [
            pl.BlockSpec((tm, half), lambda i: (i, gcol)),
            pl.BlockSpec((tm, half), lambda i: (i, gcol + 1)),
            pl.BlockSpec((tm, half), lambda i: (i, gcol + 2)),
            pl.BlockSpec((tm, half), lambda i: (i, gcol + 3)),
        ] + _two_part_specs(tm, T_CTX // tm) + [
            pl.BlockSpec((None, tm // SEG, 8, D), lambda i: (l, i, 0, 0)),
            pl.BlockSpec((None, 1, D), lambda i: (l, 0, 0)),
            wspec, wspec, wspec,
            pl.BlockSpec((N_EXP, D), lambda i: (0, 0)),
            pl.BlockSpec((N_EXP, 1), lambda i: (0, 0)),
        ],
        out_specs=[
            pl.BlockSpec((tm, D), lambda i: (i, 0)),
            pl.BlockSpec((tm * NCH, 128), lambda i: (i, 0)),
            pl.BlockSpec((2, tm), lambda i: (0, i)),
            pl.BlockSpec((2, tm), lambda i: (0, i)),
        ],
        out_shape=[
            jax.ShapeDtypeStruct((T, D), F32),
            jax.ShapeDtypeStruct((T * NCH, 128), F32),
            jax.ShapeDtypeStruct((2, T), jnp.int32),
            jax.ShapeDtypeStruct((2, T), F32),
        ],
        scratch_shapes=[pltpu.VMEM((D, D), BF16)] * 3,
        compiler_params=_cp(("arbitrary",)),
        name="merge",
    )(yrec, o_ctx, o_lat, proj, proj, proj, proj, xa, xb, modseg, norm2_g.reshape(DEPTH, 1, D),
      w_rec_out, w_att_out, w_out, wrt, rbias)


MOE_TM = 256
MOE_NT = 2 * T // MOE_TM + N_EXP
MOE_ROWS = MOE_NT * MOE_TM
META_TILE_E, META_CNT, META_OFF, META_END, META_NT, META_NEXT_E = 0, 1, 2, 3, 4, 5


def _pos_kernel(idx_ref, pos_ref, meta_ref):
    shift = MOE_TM.bit_length() - 1
    idx = idx_ref[...]
    eid = lax.broadcasted_iota(jnp.int32, (N_EXP, T), 0)
    m0 = eid == idx[0:1, :]
    m1 = eid == idx[1:2, :]
    member = jnp.where(m0 | m1, 1.0, 0.0)
    cnt = jnp.sum(member, axis=1, keepdims=True).astype(jnp.int32)
    ntile = jnp.right_shift(cnt + (MOE_TM - 1), shift)
    offs, acc = [], jnp.zeros((1, 1), jnp.int32)
    for e in range(N_EXP):
        offs.append(acc)
        acc = acc + ntile[e:e + 1, :]
    off_t = jnp.concatenate(offs, axis=0)
    end_t = off_t + ntile

    blk = 256
    r_i = lax.broadcasted_iota(jnp.int32, (blk, blk), 0)
    c_i = lax.broadcasted_iota(jnp.int32, (blk, blk), 1)
    upper = jnp.where(r_i <= c_i, 1.0, 0.0).astype(BF16)
    run = (off_t * MOE_TM).astype(F32)
    for j in range(T // blk):
        ls = slice(j * blk, (j + 1) * blk)
        mb = member[:, ls]
        inc = _dot(mb.astype(BF16), upper)
        dest = run + inc - mb
        pos_ref[0:1, ls] = jnp.sum(jnp.where(m0[:, ls], dest, 0.0), axis=0, keepdims=True).astype(jnp.int32)
        pos_ref[1:2, ls] = jnp.sum(jnp.where(m1[:, ls], dest, 0.0), axis=0, keepdims=True).astype(jnp.int32)
        run = run + inc[:, blk - 1:blk]

    lane = lax.broadcasted_iota(jnp.int32, (1, 128), 1)
    zero = jnp.zeros((1, 128), jnp.int32)
    tile_e, cnt_row, off_row, end_row = zero, zero, zero, zero
    for e in range(N_EXP):
        tile_e = tile_e + jnp.where(lane >= end_t[e:e + 1, :], 1, 0)
        here = lane == e
        cnt_row = jnp.where(here, cnt[e:e + 1, :], cnt_row)
        off_row = jnp.where(here, off_t[e:e + 1, :] * MOE_TM, off_row)
        end_row = jnp.where(here, end_t[e:e + 1, :] * MOE_TM, end_row)
    tile_e = jnp.minimum(tile_e, N_EXP - 1)
    nt_row = zero + acc
    next_row = zero
    nxt = jnp.full((1, 1), -1, jnp.int32)
    for e in reversed(range(N_EXP)):
        next_row = jnp.where(lane == e, nxt, next_row)
        nxt = jnp.where(cnt[e:e + 1, :] > 0, e, nxt)
    meta_ref[...] = jnp.concatenate([tile_e, cnt_row, off_row, end_row, nt_row, next_row, zero, zero], axis=0)


def _route_pos(idx):
    return pl.pallas_call(
        _pos_kernel,
        grid=(1,),
        in_specs=[pl.BlockSpec((2, T), lambda i: (0, 0))],
        out_specs=[pl.BlockSpec((2, T), lambda i: (0, 0)), pl.BlockSpec((8, 128), lambda i: (0, 0))],
        out_shape=[jax.ShapeDtypeStruct((2, T), jnp.int32), jax.ShapeDtypeStruct((8, 128), jnp.int32)],
        compiler_params=_cp(("arbitrary",)),
        name="route_pos",
    )(idx)


DISP_TM = 256


def _dispatch_kernel(meta_ref, pos_ref, h_ref, z_hbm, xs_hbm, sem):
    i = pl.program_id(0)

    def row_copy(src, src_row, dst_row):
        return pltpu.make_async_copy(src.at[pl.ds(src_row * NCH, NCH), :],
                                     xs_hbm.at[pl.ds(pl.multiple_of(dst_row * NCH, NCH), NCH), :], sem)

    for r in range(DISP_TM):
        row_copy(h_ref, r, pos_ref[0, r]).start(priority=0)
        row_copy(h_ref, r, pos_ref[1, r]).start(priority=1)

    e = jnp.minimum(i, N_EXP - 1)
    pad0 = meta_ref[META_OFF, e] + meta_ref[META_CNT, e]
    npad = jnp.where(i < N_EXP, meta_ref[META_END, e] - pad0, 0)

    def zero_fill(p, c):
        row_copy(z_hbm, 0, p).start()
        return c

    lax.fori_loop(pad0, pad0 + npad, zero_fill, 0)

    for _ in range(2):
        pltpu.make_async_copy(h_ref, xs_hbm.at[pl.ds(0, DISP_TM * NCH), :], sem).wait()

    tail = meta_ref[META_NT, 0] + i
    has_tail = (i < N_EXP) & (tail < MOE_NT)

    def tail_copy():
        rows = pl.ds(pl.multiple_of(tail * (MOE_TM * NCH), MOE_TM * NCH), MOE_TM * NCH)
        return pltpu.make_async_copy(z_hbm, xs_hbm.at[rows, :], sem)

    @pl.when(has_tail)
    def _():
        tail_copy().start()

    def zero_wait(p, c):
        row_copy(z_hbm, 0, p).wait()
        return c

    lax.fori_loop(pad0, pad0 + npad, zero_wait, 0)

    @pl.when(has_tail)
    def _():
        tail_copy().wait()


def _dispatch(meta, pos, h2, zrow):
    return pl.pallas_call(
        _dispatch_kernel,
        grid_spec=pltpu.PrefetchScalarGridSpec(
            num_scalar_prefetch=1,
            grid=(T // DISP_TM,),
            in_specs=[
                pl.BlockSpec((2, DISP_TM), lambda i, meta: (0, i), memory_space=pltpu.SMEM),
                pl.BlockSpec((DISP_TM * NCH, 128), lambda i, meta: (i, 0)),
                pl.BlockSpec((MOE_TM * NCH, 128), lambda i, meta: (0, 0)),
            ],
            out_specs=pl.BlockSpec(memory_space=pl.ANY),
            scratch_shapes=[pltpu.SemaphoreType.DMA],
        ),
        out_shape=jax.ShapeDtypeStruct((MOE_ROWS * NCH, 128), F32),
        compiler_params=_cp(("arbitrary",)),
        name="dispatch",
    )(meta, pos, h2, zrow)


def _experts_kernel(meta_ref, xs_ref, wg_hbm, wu_hbm, wd_hbm, ys_ref,
                    wg_f, wu_f, wd_f, wg_s, wu_s, wd_s, sem, *, l):
    j = pl.program_id(0)
    live = j < meta_ref[META_NT, 0]
    e = meta_ref[META_TILE_E, j]
    e_prev = meta_ref[META_TILE_E, jnp.maximum(j - 1, 0)]

    def fetch(ex):
        return (pltpu.make_async_copy(wg_hbm.at[l, ex], wg_f, sem.at[0]),
                pltpu.make_async_copy(wu_hbm.at[l, ex], wu_f, sem.at[1]),
                pltpu.make_async_copy(wd_hbm.at[l, ex], wd_f, sem.at[2]))

    @pl.when(j == 0)
    def _():
        for c in fetch(e):
            c.start()

    @pl.when(live & ((j == 0) | (e != e_prev)))
    def _():
        for c, dst, src in zip(fetch(e), (wg_s, wu_s, wd_s), (wg_f, wu_f, wd_f)):
            c.wait()
            dst[...] = src[...].astype(BF16)
        nxt = meta_ref[META_NEXT_E, e]

        @pl.when(nxt >= 0)
        def _():
            for c in fetch(nxt):
                c.start()

    @pl.when(live)
    def _():
        x = _load_row_tiles(xs_ref, MOE_TM).astype(BF16)
        g = _dot(x, wg_s[...])
        u = _dot(x, wu_s[...])
        act = (g * _sigmoid(g)) * u
        _store_row_tiles(ys_ref, _dot(act.astype(BF16), wd_s[...]))

    @pl.when(jnp.logical_not(live))
    def _():
        ys_ref[...] = jnp.zeros_like(ys_ref)


def _experts(meta, xs, w_gate_e, w_up_e, w_down_e, l):
    def tile(j, meta):
        return jnp.minimum(j, meta[META_NT, 0] - 1)

    return pl.pallas_call(
        functools.partial(_experts_kernel, l=l),
        grid_spec=pltpu.PrefetchScalarGridSpec(
            num_scalar_prefetch=1,
            grid=(MOE_NT,),
            in_specs=[
                pl.BlockSpec((MOE_TM * NCH, 128), lambda j, meta: (tile(j, meta), 0)),
                pl.BlockSpec(memory_space=pl.ANY),
                pl.BlockSpec(memory_space=pl.ANY),
                pl.BlockSpec(memory_space=pl.ANY),
            ],
            out_specs=pl.BlockSpec((MOE_TM * NCH, 128), lambda j, meta: (j, 0)),
            scratch_shapes=[pltpu.VMEM((D, D_EXP), F32), pltpu.VMEM((D, D_EXP), F32), pltpu.VMEM((D_EXP, D), F32),
                            pltpu.VMEM((D, D_EXP), BF16), pltpu.VMEM((D, D_EXP), BF16), pltpu.VMEM((D_EXP, D), BF16),
                            pltpu.SemaphoreType.DMA((3,))],
        ),
        out_shape=jax.ShapeDtypeStruct((MOE_ROWS * NCH, 128), F32),
        compiler_params=_cp(("arbitrary",)),
        name="experts",
    )(meta, xs, w_gate_e, w_up_e, w_down_e)


COMB_TM = SEG


def _combine_kernel(pos_ref, w_ref, x1_ref, mod_ref, fg_ref, ys_hbm, oa_ref, ob_ref, buf, y_s, sem, *, final):
    i = pl.program_id(0)
    n = pl.num_programs(0) - 1
    n_ctx = T_CTX // COMB_TM

    for s in range(2):
        @pl.when((i < n) & (lax.rem(i, 2) == s))
        def _():
            for r in range(COMB_TM):
                for k in range(2):
                    src = pl.ds(pl.multiple_of(pos_ref[k, r] * NCH, NCH), NCH)
                    pltpu.make_async_copy(ys_hbm.at[src, :], buf.at[s, k, pl.ds(r * NCH, NCH), :],
                                          sem.at[s]).start(priority=k)

    for slot in range(2):
        @pl.when((i > 0) & (lax.rem(i - 1, 2) == slot))
        def _():
            for k in range(2):
                pltpu.make_async_copy(ys_hbm.at[pl.ds(0, COMB_TM * NCH), :], buf.at[slot, k], sem.at[slot]).wait()
            w = w_ref[...]
            y = (w[:, 0:1] * _load_row_tiles(buf.at[slot, 0], COMB_TM)
                 + w[:, 1:2] * _load_row_tiles(buf.at[slot, 1], COMB_TM))
            y_s[...] = y

    @pl.when(i > 0)
    def _():
        x = x1_ref[...] + mod_ref[5:6, :] * y_s[...]
        if final:
            ms = jnp.mean(x * x, axis=-1, keepdims=True)
            x = x * lax.rsqrt(ms + EPS) * fg_ref[...]

        @pl.when(i - 1 < n_ctx)
        def _():
            oa_ref[...] = x

        @pl.when(i - 1 >= n_ctx)
        def _():
            ob_ref[...] = x


def _combine(pos, wts_t, x1, modseg, final_g, ys, l, final):
    n = T // COMB_TM
    n_ctx = T_CTX // COMB_TM

    def done(i):
        return jnp.maximum(i - 1, 0)

    return pl.pallas_call(
        functools.partial(_combine_kernel, final=final),
        grid=(n + 1,),
        in_specs=[
            pl.BlockSpec((2, COMB_TM), lambda i: (0, jnp.minimum(i, n - 1)), memory_space=pltpu.SMEM),
            pl.BlockSpec((COMB_TM, 2), lambda i: (done(i), 0)),
            pl.BlockSpec((COMB_TM, D), lambda i: (done(i), 0)),
            pl.BlockSpec((None, None, 8, D), lambda i: (l, done(i), 0, 0)),
            pl.BlockSpec((1, D), lambda i: (0, 0)),
            pl.BlockSpec(memory_space=pl.ANY),
        ],
        out_specs=[pl.BlockSpec((COMB_TM, D), lambda i: (jnp.minimum(done(i), n_ctx - 1), 0)),
                   pl.BlockSpec((COMB_TM, D), lambda i: (jnp.maximum(done(i) - n_ctx, 0), 0))],
        out_shape=[jax.ShapeDtypeStruct((T_CTX, D), F32), jax.ShapeDtypeStruct((T_LAT, D), F32)],
        scratch_shapes=[pltpu.VMEM((2, 2, COMB_TM * NCH, 128), F32), pltpu.VMEM((COMB_TM, D), F32),
                        pltpu.SemaphoreType.DMA((2,))],
        compiler_params=_cp(("arbitrary",)),
        name="combine",
    )(pos, wts_t, x1, modseg, final_g.reshape(1, D), ys)


def _rope_tables():
    n = DEC_SEQ
    pos_row = np.repeat(np.arange(n // GRID_W, dtype=np.float32), GRID_W)
    pos_col = np.tile(np.arange(GRID_W, dtype=np.float32), n // GRID_W)
    half = HD // 2
    inv_freq = jnp.asarray(ROPE_THETA, F32) ** (-jnp.arange(0, half, 2, dtype=F32) / half)
    ang = jnp.concatenate([jnp.asarray(pos_row)[:, None] * inv_freq,
                           jnp.asarray(pos_col)[:, None] * inv_freq], axis=-1)
    cos, sin = jnp.cos(ang), jnp.sin(ang)
    cos128 = jnp.tile(cos, (1, 4))
    sin128 = jnp.tile(jnp.concatenate([-sin, sin], axis=-1), (1, 2))
    return cos128, sin128


def _head_mean_matrix():
    idx = np.arange(2 * HD)
    same = (idx[:, None] // HD) == (idx[None, :] // HD)
    return jnp.asarray(same.astype(np.float32) / HD, BF16)


_SEG_ROWS = np.array([0] * (T_CTX // SEG) + [1 + b for b in range(DEC_BATCH) for _ in range(DEC_SEQ // SEG)])


def kernel(x_prompt, x_sample, cache_k, cache_v, state_rec, c, c_ctx, w_mod, b_mod, norm1_g, norm2_g, w_in, conv_w, conv_b, rg_wa, rg_ba, rg_wx, rg_bx, rg_lambda, q_norm_g, k_norm_g, w_rec_out, w_att_out, w_out, w_router, router_bias, w_gate_e, w_up_e, w_down_e, final_g):
    xa, xb = x_prompt.reshape(T_CTX, D), x_sample.reshape(T_LAT, D)

    cvecs = jnp.concatenate([c_ctx[None, :], c, jnp.zeros((3, D), F32)], axis=0)
    mods = _mods(cvecs, w_mod, b_mod).reshape(DEPTH, 8, 6, D)
    modseg = jnp.pad(mods[:, _SEG_ROWS], ((0, 0), (0, 0), (0, 2), (0, 0)))

    cos128, sin128 = _rope_tables()
    bd = _head_mean_matrix()
    qg128 = jnp.tile(q_norm_g, (1, 2)).reshape(DEPTH, 1, 2 * HD)
    kg128 = jnp.tile(k_norm_g, (1, 2)).reshape(DEPTH, 1, 2 * HD)
    wg = jnp.concatenate([rg_wa[:, 0], rg_wx[:, 0], rg_wa[:, 1], rg_wx[:, 1]], axis=-1)
    pvec = jnp.stack([rg_ba[:, 0], rg_bx[:, 0], rg_ba[:, 1], rg_bx[:, 1],
                      rg_lambda[:, 0], rg_lambda[:, 1], conv_b, jnp.zeros_like(conv_b)], axis=1)
    wrt = w_router.T
    rbias = router_bias.reshape(N_EXP, 1)
    zrow = jnp.zeros((MOE_TM * NCH, 128), F32)

    new_k, new_v, new_s = [], [], []
    for l in range(DEPTH):
        proj = _inproj(xa, xb, modseg, norm1_g, w_in, l)
        h0 = jnp.concatenate([jnp.zeros((T_CTX // UNIT, 2, D), F32), state_rec[:, l]], axis=0)
        yrec, stf, stb = _rec(proj, conv_w, pvec, wg, h0, l)
        qc, kc, vc = _qkv(proj, qg128, kg128, cos128, sin128, bd, l, latent=False)
        ql, kl, vl = _qkv(proj, qg128, kg128, cos128, sin128, bd, l, latent=True)
        o_ctx = _attn_ctx(qc, kc, vc)
        o_lat = _attn_lat(ql, cache_k, cache_v, kl, vl, l)
        x1, h2, idx, wts = _merge(yrec, o_ctx, o_lat, proj, xa, xb, modseg, norm2_g,
                                  w_rec_out, w_att_out, w_out, wrt, rbias, l)
        pos, meta = _route_pos(idx)
        xs = _dispatch(meta, pos, h2, zrow)
        ys = _experts(meta, xs, w_gate_e, w_up_e, w_down_e, l)
        xa, xb = _combine(pos, wts.T, x1, modseg, final_g, ys, l, final=(l == DEPTH - 1))
        new_k.append(kc)
        new_v.append(vc)
        n_cu = T_CTX // UNIT
        spu = UNIT // SEQ
        hf_last = stf[:n_cu].reshape(n_cu, spu, 2, D)[:, :, 1].reshape(BATCH, D)
        hb_first = stb[:n_cu].reshape(n_cu, spu, 2, D)[:, :, 0].reshape(BATCH, D)
        new_s.append(jnp.stack([hf_last, hb_first], axis=1))

    y_prompt = xa.reshape(BATCH, SEQ, D)
    y_sample = xb.reshape(DEC_BATCH, DEC_SEQ, D)
    return (y_prompt, y_sample, jnp.stack(new_k, axis=1), jnp.stack(new_v, axis=1), jnp.stack(new_s, axis=1))
```

```python
import functools

import numpy as np
import jax
import jax.numpy as jnp
from jax import lax
from jax.experimental import pallas as pl
from jax.experimental.pallas import tpu as pltpu

F32 = jnp.float32
BF16 = jnp.bfloat16

D = 1024
BATCH = 16
SEQ = 256
DEPTH = 2
DEC_BATCH = 4
DEC_SEQ = 1024
PAST = 256
GRID_W = 64
N_HEADS = 16
N_KV = 4
HD = 64
N_RG_BLK = 8
RG_BLK = 128
RG_C = 8.0
N_EXP = 16
D_EXP = 512
ROPE_THETA = 10000.0
EPS = 1e-6
P_IN = 5632
TINY = float(np.finfo(np.float32).tiny)
NEG_LOG2E = -float(np.log2(np.e))

T_CTX = BATCH * SEQ
T_LAT = DEC_BATCH * DEC_SEQ
T = T_CTX + T_LAT
SEG = 256
N_SEG = T // SEG
UNIT = 1024
N_UNIT = T // UNIT
CHUNK = UNIT // 8
CSTRIDE = CHUNK + 8

VMEM_LIMIT = 56 * 1024 * 1024


def _cp(sem):
    return pltpu.CompilerParams(dimension_semantics=sem, vmem_limit_bytes=VMEM_LIMIT)


def _split(x):
    hi = x.astype(BF16)
    lo = (x - hi.astype(F32)).astype(BF16)
    return hi, lo


def _sigmoid(x):
    return 0.5 * jnp.tanh(0.5 * x) + 0.5


NCH = D // 128


def _store_row_tiles(ref, x):
    n = x.shape[0]
    for c in range(NCH):
        ref[pl.ds(c, n, stride=NCH), :] = x[:, c * 128:(c + 1) * 128]


def _load_row_tiles(ref, n):
    return jnp.concatenate([ref[pl.ds(c, n, stride=NCH), :] for c in range(NCH)], axis=-1)


def _dot(a, b):
    return jnp.dot(a, b, preferred_element_type=F32)


def _dot_nt(a, b):
    return lax.dot_general(a, b, (((1,), (1,)), ((), ())), preferred_element_type=F32)


def _mods_kernel(c_ref, w_ref, b_ref, o_ref):
    c = c_ref[...]
    s = c * jax.nn.sigmoid(c)
    s_hi, s_lo = _split(s)
    w_hi, w_lo = _split(w_ref[...])
    o_ref[...] = _dot(s_hi, w_hi) + _dot(s_hi, w_lo) + _dot(s_lo, w_hi) + b_ref[...]


def _mods(cvecs, w_mod, b_mod):
    tn = 1536
    return pl.pallas_call(
        _mods_kernel,
        grid=(DEPTH, 6 * D // tn),
        in_specs=[
            pl.BlockSpec((8, D), lambda l, j: (0, 0)),
            pl.BlockSpec((None, D, tn), lambda l, j: (l, 0, j)),
            pl.BlockSpec((None, 1, tn), lambda l, j: (l, 0, j)),
        ],
        out_specs=pl.BlockSpec((None, 8, tn), lambda l, j: (l, 0, j)),
        out_shape=jax.ShapeDtypeStruct((DEPTH, 8, 6 * D), F32),
        compiler_params=_cp(("arbitrary", "arbitrary")),
        name="mods",
    )(cvecs, w_mod, b_mod.reshape(DEPTH, 1, 6 * D))


def _norm_mod(x, g, shift, scale):
    ms = jnp.mean(x * x, axis=-1, keepdims=True)
    return x * lax.rsqrt(ms + EPS) * g * (1.0 + scale) + shift


def _two_part_specs(tm, n_ctx):
    return [pl.BlockSpec((tm, D), lambda i, *_: (jnp.minimum(i, n_ctx - 1), 0)),
            pl.BlockSpec((tm, D), lambda i, *_: (jnp.maximum(i - n_ctx, 0), 0))]


def _inproj_kernel(xa_ref, xb_ref, mod_ref, g_ref, w_ref, o_ref, h_ref, *, tm):
    def prologue(x_ref):
        def seg(s, carry):
            r0 = pl.multiple_of(s * SEG, SEG)
            m = mod_ref[s]
            h = _norm_mod(x_ref[pl.ds(r0, SEG), :], g_ref[...], m[0:1, :], m[1:2, :])
            h_ref[pl.ds(r0, SEG), :] = h.astype(BF16)
            return carry
        lax.fori_loop(0, tm // SEG, seg, 0)

    first = pl.program_id(1) == 0
    is_ctx = pl.program_id(0) < T_CTX // tm

    @pl.when(first & is_ctx)
    def _():
        prologue(xa_ref)

    @pl.when(first & jnp.logical_not(is_ctx))
    def _():
        prologue(xb_ref)

    o_ref[...] = _dot(h_ref[...], w_ref[...].astype(BF16)).astype(BF16)


def _inproj(xa, xb, modseg, norm_g, w_in, l):
    tm, tn = 2048, 512
    return pl.pallas_call(
        functools.partial(_inproj_kernel, tm=tm),
        grid=(T // tm, P_IN // tn),
        in_specs=_two_part_specs(tm, T_CTX // tm) + [
            pl.BlockSpec((None, tm // SEG, 8, D), lambda i, j: (l, i, 0, 0)),
            pl.BlockSpec((None, 1, D), lambda i, j: (l, 0, 0)),
            pl.BlockSpec((None, D, tn), lambda i, j: (l, 0, j)),
        ],
        out_specs=pl.BlockSpec((tm, tn), lambda i, j: (i, j)),
        out_shape=jax.ShapeDtypeStruct((T, P_IN), BF16),
        scratch_shapes=[pltpu.VMEM((tm, D), BF16)],
        compiler_params=_cp(("arbitrary", "arbitrary")),
        name="inproj",
    )(xa, xb, modseg, norm_g.reshape(DEPTH, 1, D), w_in)


REC_CW = 512
PAD_F = 16
PAD_B = 8


def _rec_kernel(xr_ref, gate_ref, cw_ref, pv_ref, wg_ref, h0_ref,
                y_ref, stf_ref, stb_ref,
                xs_ref, af_ref, bf_ref, ab_ref, bb_ref, nat_ref, wgh_ref):
    u = pl.program_id(0)
    is_ctx = u < (T_CTX // UNIT)
    cps = jnp.where(is_ctx, SEQ // CHUNK, DEC_SEQ // CHUNK)
    nblk = REC_CW // RG_BLK

    def lanes(n):
        return slice(n * RG_BLK, (n + 1) * RG_BLK)

    def tile(r):
        return slice(8 * r, 8 * r + 8)

    for c in range(8):
        for n in range(nblk):
            nat_ref[n, c * CSTRIDE:c * CSTRIDE + CHUNK, :] = xr_ref[c * CHUNK:(c + 1) * CHUNK, lanes(n)].astype(F32)
    for r in range(CHUNK):
        for n in range(nblk):
            xs_ref[PAD_F + 8 * r:PAD_F + 8 * r + 8, lanes(n)] = nat_ref[n, pl.ds(r, 8, stride=CSTRIDE), :]
    chunk_id = lax.broadcasted_iota(jnp.int32, (8, 1), 0)
    seq_start = jnp.bitwise_and(chunk_id, cps - 1) == 0
    seq_end = jnp.bitwise_and(chunk_id, cps - 1) == cps - 1
    for j, r in ((0, CHUNK - 2), (1, CHUNK - 1)):
        prev_chunk = pltpu.roll(xs_ref[PAD_F + 8 * r:PAD_F + 8 * r + 8, :], 1, 0)
        xs_ref[tile(j), :] = jnp.where(seq_start, 0.0, prev_chunk)
    next_chunk = pltpu.roll(xs_ref[PAD_F:PAD_F + 8, :], 7, 0)
    xs_ref[PAD_F + UNIT:PAD_F + UNIT + PAD_B, :] = jnp.where(seq_end, 0.0, next_chunk)

    pv = pv_ref[...]
    cwts = cw_ref[...]
    conv_b = pv[6:7, :]

    def softplus_neg(lam):
        z = -lam
        return jnp.maximum(z, 0.0) + jnp.log1p(jnp.exp(-jnp.abs(z)))

    c4s = tuple((0.5 * RG_C) * softplus_neg(pv[4 + d:5 + d, :]) for d in range(2))
    pv_h = 0.5 * pv
    for n in range(nblk):
        wgh_ref[n] = (0.5 * wg_ref[n]).astype(BF16)
    a_refs = (af_ref, ab_ref)
    b_refs = (bf_ref, bb_ref)

    def gates(g, carry):
        base = pl.multiple_of(g * CHUNK, CHUNK)

        def tap(d):
            return xs_ref[pl.ds(pl.multiple_of(base + PAD_F + 8 * d, 8), CHUNK), :]

        xc = conv_b + tap(-2) * cwts[0:1, :]
        xc = xc + tap(-1) * cwts[1:2, :]
        xc = xc + tap(0) * cwts[2:3, :]
        xc = xc + tap(1) * cwts[3:4, :]
        for n in range(nblk):
            ls = lanes(n)
            xn = xc[:, ls]
            hx = 0.5 * xn
            pre_h = _dot(xn.astype(BF16), wgh_ref[n])
            for d in range(2):
                th_r = jnp.tanh(pre_h[:, (2 * d) * RG_BLK:(2 * d + 1) * RG_BLK] + pv_h[2 * d:2 * d + 1, ls])
                th_i = jnp.tanh(pre_h[:, (2 * d + 1) * RG_BLK:(2 * d + 2) * RG_BLK] + pv_h[2 * d + 1:2 * d + 2, ls])
                c4 = c4s[d][:, ls]
                nla = c4 * th_r + c4
                a = jnp.exp2(nla * NEG_LOG2E)
                s = jnp.tanh(nla) * (a * a + 1.0)
                inp = (s * lax.rsqrt(jnp.maximum(s, TINY))) * (hx * th_i + hx)
                a_refs[d][pl.ds(base, CHUNK), ls] = a
                b_refs[d][pl.ds(base, CHUNK), ls] = inp
        return carry

    lax.fori_loop(0, 8, gates, 0)

    hf = hb = jnp.zeros((8, REC_CW), F32)
    pf = pb = jnp.ones((8, REC_CW), F32)
    for r in range(CHUNK):
        rf, rb = tile(r), tile(CHUNK - 1 - r)
        a = af_ref[rf, :]
        hf = a * hf + bf_ref[rf, :]
        pf = a * pf
        bf_ref[rf, :] = hf
        af_ref[rf, :] = pf
        a = ab_ref[rb, :]
        hb = a * hb + bb_ref[rb, :]
        pb = a * pb
        bb_ref[rb, :] = hb
        ab_ref[rb, :] = pb

    h0f = h0_ref[0:1, :]
    h0b = h0_ref[1:2, :]
    cf = [h0f]
    for c in range(1, 8):
        chain = hf[c - 1:c, :] + pf[c - 1:c, :] * cf[c - 1]
        cf.append(jnp.where(jnp.bitwise_and(c, cps - 1) == 0, h0f, chain))
    cb = [None] * 8
    cb[7] = h0b
    for c in range(6, -1, -1):
        chain = hb[c + 1:c + 2, :] + pb[c + 1:c + 2, :] * cb[c + 1]
        cb[c] = jnp.where(jnp.bitwise_and(c, cps - 1) == cps - 1, h0b, chain)
    carry_f = jnp.concatenate(cf, axis=0)
    carry_b = jnp.concatenate(cb, axis=0)
    stf_ref[...] = hf + pf * carry_f
    stb_ref[...] = hb + pb * carry_b

    for r in range(CHUNK):
        h = (bf_ref[tile(r), :] + af_ref[tile(r), :] * carry_f) + (bb_ref[tile(r), :] + ab_ref[tile(r), :] * carry_b)
        for n in range(nblk):
            nat_ref[n, pl.ds(r, 8, stride=CSTRIDE), :] = h[:, lanes(n)]

    for c in range(8):
        rows = slice(c * CHUNK, (c + 1) * CHUNK)
        for n in range(nblk):
            g = gate_ref[rows, lanes(n)].astype(F32)
            h = nat_ref[n, c * CSTRIDE:c * CSTRIDE + CHUNK, :]
            y_ref[rows, lanes(n)] = (h * jax.nn.gelu(g, approximate=True)).astype(BF16)


def _rec(proj, conv_w, pvec, wg, h0, l):
    ncb = D // REC_CW
    return pl.pallas_call(
        _rec_kernel,
        grid=(N_UNIT, ncb),
        in_specs=[
            pl.BlockSpec((UNIT, REC_CW), lambda u, c: (u, c)),
            pl.BlockSpec((UNIT, REC_CW), lambda u, c: (u, ncb + c)),
            pl.BlockSpec((None, 4, REC_CW), lambda u, c: (l, 0, c)),
            pl.BlockSpec((None, 8, REC_CW), lambda u, c: (l, 0, c)),
            pl.BlockSpec((None, REC_CW // RG_BLK, RG_BLK, 4 * RG_BLK), lambda u, c: (l, c, 0, 0)),
            pl.BlockSpec((None, 2, REC_CW), lambda u, c: (u, 0, c)),
        ],
        out_specs=[
            pl.BlockSpec((UNIT, REC_CW), lambda u, c: (u, c)),
            pl.BlockSpec((None, 8, REC_CW), lambda u, c: (u, 0, c)),
            pl.BlockSpec((None, 8, REC_CW), lambda u, c: (u, 0, c)),
        ],
        out_shape=[
            jax.ShapeDtypeStruct((T, D), BF16),
            jax.ShapeDtypeStruct((N_UNIT, 8, D), F32),
            jax.ShapeDtypeStruct((N_UNIT, 8, D), F32),
        ],
        scratch_shapes=[pltpu.VMEM((PAD_F + UNIT + PAD_B, REC_CW), F32)]
        + [pltpu.VMEM((UNIT, REC_CW), F32)] * 4
        + [pltpu.VMEM((REC_CW // RG_BLK, 8 * CSTRIDE, RG_BLK), F32),
           pltpu.VMEM((REC_CW // RG_BLK, RG_BLK, 4 * RG_BLK), BF16)],
        compiler_params=_cp(("arbitrary", "arbitrary")),
        name="rec",
    )(proj, proj, conv_w, pvec, wg, h0)


def _head_norm(x, g128, bd):
    hi, lo = _split(x * x)
    ms = _dot(hi, bd) + _dot(lo, bd)
    return x * lax.rsqrt(ms + EPS) * g128


def _rope(x, cos, sin_signed):
    lane = lax.broadcasted_iota(jnp.int32, x.shape, 1)
    first_half = jnp.bitwise_and(lane, HD - 1) < HD // 2
    partner = jnp.where(first_half, pltpu.roll(x, 2 * HD - HD // 2, 1), pltpu.roll(x, HD // 2, 1))
    return x * cos + partner * sin_signed


def _qkv_kernel(q_ref, k_ref, v_ref, qg_ref, kg_ref, cos_ref, sin_ref, bd_ref,
                qo_ref, ko_ref, vo_ref, *, rope):
    bd = bd_ref[...]
    scale = HD ** -0.5 * float(np.log2(np.e))
    for j in range(N_HEADS // 2):
        x = _head_norm(q_ref[:, 2 * HD * j:2 * HD * (j + 1)].astype(F32), qg_ref[...], bd)
        if rope:
            x = _rope(x, cos_ref[...], sin_ref[...])
        x = x * scale
        qo_ref[2 * j] = x[:, :HD].astype(qo_ref.dtype)
        qo_ref[2 * j + 1] = x[:, HD:].astype(qo_ref.dtype)
    for j in range(N_KV // 2):
        x = _head_norm(k_ref[:, 2 * HD * j:2 * HD * (j + 1)].astype(F32), kg_ref[...], bd)
        if rope:
            x = _rope(x, cos_ref[...], sin_ref[...])
        ko_ref[2 * j] = x[:, :HD].astype(ko_ref.dtype)
        ko_ref[2 * j + 1] = x[:, HD:].astype(ko_ref.dtype)
        v = v_ref[:, 2 * HD * j:2 * HD * (j + 1)].astype(F32)
        vo_ref[2 * j] = v[:, :HD].astype(vo_ref.dtype)
        vo_ref[2 * j + 1] = v[:, HD:].astype(vo_ref.dtype)


def _qkv(proj, qg128, kg128, cos128, sin128, bd, l, latent):
    tm = SEG
    n = T_LAT // tm if latent else T_CTX // tm
    roff = T_CTX // tm if latent else 0
    per_seq = DEC_SEQ // tm
    if latent:
        kv_shape = (DEC_BATCH, N_KV, DEC_SEQ, HD)
        kv_spec = pl.BlockSpec((None, N_KV, tm, HD), lambda i: (i // per_seq, 0, i % per_seq, 0))
        kv_dtype = BF16
        tab_map = lambda i: (i % per_seq, 0)
    else:
        kv_shape = (BATCH, N_KV, SEQ, HD)
        kv_spec = pl.BlockSpec((None, N_KV, tm, HD), lambda i: (i, 0, 0, 0))
        kv_dtype = F32
        tab_map = lambda i: (0, 0)
    return pl.pallas_call(
        functools.partial(_qkv_kernel, rope=latent),
        grid=(n,),
        in_specs=[
            pl.BlockSpec((tm, D), lambda i: (roff + i, 2)),
            pl.BlockSpec((tm, N_KV * HD), lambda i: (roff + i, 3 * D // (N_KV * HD))),
            pl.BlockSpec((tm, N_KV * HD), lambda i: (roff + i, 3 * D // (N_KV * HD) + 1)),
            pl.BlockSpec((None, 1, 2 * HD), lambda i: (l, 0, 0)),
            pl.BlockSpec((None, 1, 2 * HD), lambda i: (l, 0, 0)),
            pl.BlockSpec((tm, 2 * HD), tab_map),
            pl.BlockSpec((tm, 2 * HD), tab_map),
            pl.BlockSpec((2 * HD, 2 * HD), lambda i: (0, 0)),
        ],
        out_specs=[
            pl.BlockSpec((N_HEADS, tm, HD), lambda i: (0, i, 0)),
            kv_spec,
            kv_spec,
        ],
        out_shape=[
            jax.ShapeDtypeStruct((N_HEADS, n * tm, HD), BF16),
            jax.ShapeDtypeStruct(kv_shape, kv_dtype),
            jax.ShapeDtypeStruct(kv_shape, kv_dtype),
        ],
        compiler_params=_cp(("arbitrary",)),
        name="qkv_lat" if latent else "qkv_ctx",
    )(proj, proj, proj, qg128, kg128, cos128, sin128, bd)


def _with_ones(v):
    return jnp.concatenate([v, jnp.ones_like(v)], axis=-1)


def _softmax_pv(q, k, v_ext):
    s = _dot_nt(q, k)
    m = jnp.max(s, axis=-1, keepdims=True)
    p = jnp.exp2(s - m).astype(BF16)
    r = _dot(p, v_ext)
    return r[:, :HD] / r[:, HD:HD + 1]


def _attend_heads(q_ref, k, v_ext):
    return jnp.concatenate([_softmax_pv(q_ref[h], k, v_ext) for h in range(N_HEADS // N_KV)], axis=-1)


def _attn_ctx_kernel(q_ref, k_ref, v_ref, o_ref):
    g = N_HEADS // N_KV
    q = q_ref[...].reshape(g * SEQ, HD)
    o = _softmax_pv(q, k_ref[...].astype(BF16), _with_ones(v_ref[...].astype(BF16)))
    o = jnp.concatenate([o[h * SEQ:(h + 1) * SEQ] for h in range(g)], axis=-1)
    o_ref[...] = o.astype(BF16)


def _attn_ctx(qh, kc, vc):
    g = N_HEADS // N_KV
    return pl.pallas_call(
        _attn_ctx_kernel,
        grid=(BATCH, N_KV),
        in_specs=[
            pl.BlockSpec((g, SEQ, HD), lambda b, h: (h, b, 0)),
            pl.BlockSpec((None, None, SEQ, HD), lambda b, h: (b, h, 0, 0)),
            pl.BlockSpec((None, None, SEQ, HD), lambda b, h: (b, h, 0, 0)),
        ],
        out_specs=pl.BlockSpec((SEQ, g * HD), lambda b, h: (b, h)),
        out_shape=jax.ShapeDtypeStruct((T_CTX, D), BF16),
        compiler_params=_cp(("arbitrary", "arbitrary")),
        name="attn_ctx",
    )(qh, kc, vc)


ATT_TQ = 1024


def _attn_lat_kernel(q_ref, pk_ref, pv_ref, k_ref, v_ref, o_ref, k_s, v_s):
    k_s[0:PAST, :] = pk_ref[...].astype(BF16)
    k_s[PAST:, :] = k_ref[...]
    v_s[0:PAST, :] = _with_ones(pv_ref[...].astype(BF16))
    v_s[PAST:, :] = _with_ones(v_ref[...])

    def q_tile(qi, carry):
        rows = pl.ds(pl.multiple_of(qi * ATT_TQ, ATT_TQ), ATT_TQ)
        o = jnp.concatenate([_softmax_pv(q_ref[h, rows, :], k_s[...], v_s[...])
                             for h in range(N_HEADS // N_KV)], axis=-1)
        o_ref[rows, :] = o.astype(BF16)
        return carry

    lax.fori_loop(0, DEC_SEQ // ATT_TQ, q_tile, 0)


def _attn_lat(qh, cache_k, cache_v, kr, vr, l):
    g = N_HEADS // N_KV
    return pl.pallas_call(
        _attn_lat_kernel,
        grid=(DEC_BATCH, N_KV),
        in_specs=[
            pl.BlockSpec((g, DEC_SEQ, HD), lambda b, h: (h, b, 0)),
            pl.BlockSpec((None, None, None, PAST, HD), lambda b, h: (b, l, h, 0, 0)),
            pl.BlockSpec((None, None, None, PAST, HD), lambda b, h: (b, l, h, 0, 0)),
            pl.BlockSpec((None, None, DEC_SEQ, HD), lambda b, h: (b, h, 0, 0)),
            pl.BlockSpec((None, None, DEC_SEQ, HD), lambda b, h: (b, h, 0, 0)),
        ],
        out_specs=pl.BlockSpec((DEC_SEQ, g * HD), lambda b, h: (b, h)),
        out_shape=jax.ShapeDtypeStruct((T_LAT, D), BF16),
        scratch_shapes=[pltpu.VMEM((PAST + DEC_SEQ, HD), BF16), pltpu.VMEM((PAST + DEC_SEQ, 2 * HD), BF16)],
        compiler_params=_cp(("arbitrary", "arbitrary")),
        name="attn_lat",
    )(qh, cache_k, cache_v, kr, vr)


MERGE_TM = 512


def _route(lt, bias):
    rows = [lt[e:e + 1, :] for e in range(N_EXP)]
    m = rows[0]
    for e in range(1, N_EXP):
        m = jnp.maximum(m, rows[e])
    ex = [jnp.exp(r - m) for r in rows]
    z = ex[0]
    for e in range(1, N_EXP):
        z = z + ex[e]
    probs = [x / z for x in ex]
    sel = [probs[e] + bias[e:e + 1, :] for e in range(N_EXP)]

    def top2_sum(v):
        a, b = jnp.maximum(v[0], v[1]), jnp.minimum(v[0], v[1])
        c, d = jnp.maximum(v[2], v[3]), jnp.minimum(v[2], v[3])
        return jnp.maximum(a, c) + jnp.maximum(jnp.minimum(a, c), jnp.maximum(b, d))

    scores = [top2_sum(sel[4 * g:4 * g + 4]) for g in range(4)]
    best = jnp.zeros_like(scores[0], dtype=jnp.int32)
    best_s = scores[0]
    for g in range(1, 4):
        take = scores[g] > best_s
        best = jnp.where(take, g, best)
        best_s = jnp.where(take, scores[g], best_s)
    cs, cp = [], []
    for j in range(4):
        s_j, p_j = sel[j], probs[j]
        for g in range(1, 4):
            s_j = jnp.where(best == g, sel[4 * g + j], s_j)
            p_j = jnp.where(best == g, probs[4 * g + j], p_j)
        cs.append(s_j)
        cp.append(p_j)
    neg = jnp.full_like(cs[0], -jnp.inf)

    def argmax4(v):
        bi = jnp.zeros_like(best)
        bv = v[0]
        for j in range(1, 4):
            take = v[j] > bv
            bi = jnp.where(take, j, bi)
            bv = jnp.where(take, v[j], bv)
        return bi

    def pick(v, idx):
        out = v[0]
        for j in range(1, 4):
            out = jnp.where(idx == j, v[j], out)
        return out

    i1 = argmax4(cs)
    cs2 = [jnp.where(i1 == j, neg, cs[j]) for j in range(4)]
    i2 = argmax4(cs2)
    i2 = jnp.where((i2 == 0) & (i1 == 0), 1, i2)
    w1, w2 = pick(cp, i1), pick(cp, i2)
    den = w1 + w2
    return best * 4 + i1, best * 4 + i2, w1 / den, w2 / den


def _merge_kernel(yrec_ref, oa_ref, ob_ref, gr0_ref, gr1_ref, ga0_ref, ga1_ref, xa_ref, xb_ref, mod_ref, g2_ref,
                  wrec_ref, watt_ref, wout_ref, wrt_ref, rb_ref,
                  x1_ref, h2_ref, idx_ref, wts_ref,
                  wrec_s, watt_s, wout_s):
    @pl.when(pl.program_id(0) == 0)
    def _():
        wrec_s[...] = wrec_ref[...].astype(BF16)
        watt_s[...] = watt_ref[...].astype(BF16)
        wout_s[...] = wout_ref[...].astype(BF16)

    is_ctx = pl.program_id(0) < T_CTX // MERGE_TM
    args = (yrec_ref, gr0_ref, gr1_ref, ga0_ref, ga1_ref, mod_ref, g2_ref, wrt_ref, rb_ref,
            x1_ref, h2_ref, idx_ref, wts_ref, wrec_s, watt_s, wout_s)

    @pl.when(is_ctx)
    def _():
        _merge_body(oa_ref, xa_ref, *args)

    @pl.when(jnp.logical_not(is_ctx))
    def _():
        _merge_body(ob_ref, xb_ref, *args)


def _merge_body(oatt_ref, x_ref, yrec_ref, gr0_ref, gr1_ref, ga0_ref, ga1_ref, mod_ref, g2_ref, wrt_ref, rb_ref,
                x1_ref, h2_ref, idx_ref, wts_ref, wrec_s, watt_s, wout_s):
    half = D // 2
    b_rec = _dot(yrec_ref[...], wrec_s[...])
    b_att = _dot(oatt_ref[...], watt_s[...])
    m0 = _sigmoid(gr0_ref[...].astype(F32)) * b_rec[:, :half] + _sigmoid(ga0_ref[...].astype(F32)) * b_att[:, :half]
    m1 = _sigmoid(gr1_ref[...].astype(F32)) * b_rec[:, half:] + _sigmoid(ga1_ref[...].astype(F32)) * b_att[:, half:]
    merged = jnp.concatenate([m0, m1], axis=-1).astype(BF16)
    out = _dot(merged, wout_s[...])

    hs = []
    for s in range(MERGE_TM // SEG):
        rows = slice(s * SEG, (s + 1) * SEG)
        m = mod_ref[s]
        x1 = x_ref[rows, :] + m[2:3, :] * out[rows, :]
        x1_ref[rows, :] = x1
        h2 = _norm_mod(x1, g2_ref[...], m[3:4, :], m[4:5, :])
        hs.append(h2)
    h2 = jnp.concatenate(hs, axis=0)
    _store_row_tiles(h2_ref, h2)

    h_hi, h_lo = _split(h2)
    w_hi, w_lo = _split(wrt_ref[...])
    lt = _dot_nt(w_hi, h_hi) + _dot_nt(w_hi, h_lo) + _dot_nt(w_lo, h_hi)
    e1, e2, w1, w2 = _route(lt, rb_ref[...])
    idx_ref[...] = jnp.concatenate([e1, e2], axis=0)
    wts_ref[...] = jnp.concatenate([w1, w2], axis=0)


def _merge(yrec, o_ctx, o_lat, proj, xa, xb, modseg, norm2_g, w_rec_out, w_att_out, w_out, wrt, rbias, l):
    tm = MERGE_TM
    half = D // 2
    gcol = (3 * D + 2 * N_KV * HD) // half
    wspec = pl.BlockSpec((None, D, D), lambda i: (l, 0, 0))
    return pl.pallas_call(
        _merge_kernel,
        grid=(T // tm,),
        in_specs=[pl.BlockSpec((tm, D), lambda i: (i, 0))] + _two_part_specs(tm, T_CTX // tm) + [
            pl.BlockSpec((tm, half), lambda i: (i, gcol)),
            pl.BlockSpec((tm, half), lambda i: (i, gcol + 1)),
            pl.BlockSpec((tm, half), lambda i: (i, gcol + 2)),
            pl.BlockSpec((tm, half), lambda i: (i, gcol + 3)),
        ] + _two_part_specs(tm, T_CTX // tm) + [
            pl.BlockSpec((None, tm // SEG, 8, D), lambda i: (l, i, 0, 0)),
            pl.BlockSpec((None, 1, D), lambda i: (l, 0, 0)),
            wspec, wspec, wspec,
            pl.BlockSpec((N_EXP, D), lambda i: (0, 0)),
            pl.BlockSpec((N_EXP, 1), lambda i: (0, 0)),
        ],
        out_specs=[
            pl.BlockSpec((tm, D), lambda i: (i, 0)),
            pl.BlockSpec((tm * NCH, 128), lambda i: (i, 0)),
            pl.BlockSpec((2, tm), lambda i: (0, i)),
            pl.BlockSpec((2, tm), lambda i: (0, i)),
        ],
        out_shape=[
            jax.ShapeDtypeStruct((T, D), F32),
            jax.ShapeDtypeStruct((T * NCH, 128), F32),
            jax.ShapeDtypeStruct((2, T), jnp.int32),
            jax.ShapeDtypeStruct((2, T), F32),
        ],
        scratch_shapes=[pltpu.VMEM((D, D), BF16)] * 3,
        compiler_params=_cp(("arbitrary",)),
        name="merge",
    )(yrec, o_ctx, o_lat, proj, proj, proj, proj, xa, xb, modseg, norm2_g.reshape(DEPTH, 1, D),
      w_rec_out, w_att_out, w_out, wrt, rbias)


MOE_TM = 256
MOE_NT = 2 * T // MOE_TM + N_EXP
MOE_ROWS = MOE_NT * MOE_TM
META_TILE_E, META_CNT, META_OFF, META_END, META_NT, META_NEXT_E = 0, 1, 2, 3, 4, 5


def _pos_kernel(idx_ref, pos_ref, meta_ref):
    shift = MOE_TM.bit_length() - 1
    idx = idx_ref[...]
    eid = lax.broadcasted_iota(jnp.int32, (N_EXP, T), 0)
    m0 = eid == idx[0:1, :]
    m1 = eid == idx[1:2, :]
    member = jnp.where(m0 | m1, 1.0, 0.0)
    cnt = jnp.sum(member, axis=1, keepdims=True).astype(jnp.int32)
    ntile = jnp.right_shift(cnt + (MOE_TM - 1), shift)
    offs, acc = [], jnp.zeros((1, 1), jnp.int32)
    for e in range(N_EXP):
        offs.append(acc)
        acc = acc + ntile[e:e + 1, :]
    off_t = jnp.concatenate(offs, axis=0)
    end_t = off_t + ntile

    blk = 256
    r_i = lax.broadcasted_iota(jnp.int32, (blk, blk), 0)
    c_i = lax.broadcasted_iota(jnp.int32, (blk, blk), 1)
    upper = jnp.where(r_i <= c_i, 1.0, 0.0).astype(BF16)
    run = (off_t * MOE_TM).astype(F32)
    for j in range(T // blk):
        ls = slice(j * blk, (j + 1) * blk)
        mb = member[:, ls]
        inc = _dot(mb.astype(BF16), upper)
        dest = run + inc - mb
        pos_ref[0:1, ls] = jnp.sum(jnp.where(m0[:, ls], dest, 0.0), axis=0, keepdims=True).astype(jnp.int32)
        pos_ref[1:2, ls] = jnp.sum(jnp.where(m1[:, ls], dest, 0.0), axis=0, keepdims=True).astype(jnp.int32)
        run = run + inc[:, blk - 1:blk]

    lane = lax.broadcasted_iota(jnp.int32, (1, 128), 1)
    zero = jnp.zeros((1, 128), jnp.int32)
    tile_e, cnt_row, off_row, end_row = zero, zero, zero, zero
    for e in range(N_EXP):
        tile_e = tile_e + jnp.where(lane >= end_t[e:e + 1, :], 1, 0)
        here = lane == e
        cnt_row = jnp.where(here, cnt[e:e + 1, :], cnt_row)
        off_row = jnp.where(here, off_t[e:e + 1, :] * MOE_TM, off_row)
        end_row = jnp.where(here, end_t[e:e + 1, :] * MOE_TM, end_row)
    tile_e = jnp.minimum(tile_e, N_EXP - 1)
    nt_row = zero + acc
    next_row = zero
    nxt = jnp.full((1, 1), -1, jnp.int32)
    for e in reversed(range(N_EXP)):
        next_row = jnp.where(lane == e, nxt, next_row)
        nxt = jnp.where(cnt[e:e + 1, :] > 0, e, nxt)
    meta_ref[...] = jnp.concatenate([tile_e, cnt_row, off_row, end_row, nt_row, next_row, zero, zero], axis=0)


def _route_pos(idx):
    return pl.pallas_call(
        _pos_kernel,
        grid=(1,),
        in_specs=[pl.BlockSpec((2, T), lambda i: (0, 0))],
        out_specs=[pl.BlockSpec((2, T), lambda i: (0, 0)), pl.BlockSpec((8, 128), lambda i: (0, 0))],
        out_shape=[jax.ShapeDtypeStruct((2, T), jnp.int32), jax.ShapeDtypeStruct((8, 128), jnp.int32)],
        compiler_params=_cp(("arbitrary",)),
        name="route_pos",
    )(idx)


DISP_TM = 256


def _dispatch_kernel(meta_ref, pos_ref, h_ref, z_hbm, xs_hbm, sem):
    i = pl.program_id(0)

    def row_copy(src, src_row, dst_row):
        return pltpu.make_async_copy(src.at[pl.ds(src_row * NCH, NCH), :],
                                     xs_hbm.at[pl.ds(pl.multiple_of(dst_row * NCH, NCH), NCH), :], sem)

    for r in range(DISP_TM):
        row_copy(h_ref, r, pos_ref[0, r]).start(priority=0)
        row_copy(h_ref, r, pos_ref[1, r]).start(priority=1)

    e = jnp.minimum(i, N_EXP - 1)
    pad0 = meta_ref[META_OFF, e] + meta_ref[META_CNT, e]
    npad = jnp.where(i < N_EXP, meta_ref[META_END, e] - pad0, 0)

    def zero_fill(p, c):
        row_copy(z_hbm, 0, p).start()
        return c

    lax.fori_loop(pad0, pad0 + npad, zero_fill, 0)

    for _ in range(2):
        pltpu.make_async_copy(h_ref, xs_hbm.at[pl.ds(0, DISP_TM * NCH), :], sem).wait()

    tail = meta_ref[META_NT, 0] + i
    has_tail = (i < N_EXP) & (tail < MOE_NT)

    def tail_copy():
        rows = pl.ds(pl.multiple_of(tail * (MOE_TM * NCH), MOE_TM * NCH), MOE_TM * NCH)
        return pltpu.make_async_copy(z_hbm, xs_hbm.at[rows, :], sem)

    @pl.when(has_tail)
    def _():
        tail_copy().start()

    def zero_wait(p, c):
        row_copy(z_hbm, 0, p).wait()
        return c

    lax.fori_loop(pad0, pad0 + npad, zero_wait, 0)

    @pl.when(has_tail)
    def _():
        tail_copy().wait()


def _dispatch(meta, pos, h2, zrow):
    return pl.pallas_call(
        _dispatch_kernel,
        grid_spec=pltpu.PrefetchScalarGridSpec(
            num_scalar_prefetch=1,
            grid=(T // DISP_TM,),
            in_specs=[
                pl.BlockSpec((2, DISP_TM), lambda i, meta: (0, i), memory_space=pltpu.SMEM),
                pl.BlockSpec((DISP_TM * NCH, 128), lambda i, meta: (i, 0)),
                pl.BlockSpec((MOE_TM * NCH, 128), lambda i, meta: (0, 0)),
            ],
            out_specs=pl.BlockSpec(memory_space=pl.ANY),
            scratch_shapes=[pltpu.SemaphoreType.DMA],
        ),
        out_shape=jax.ShapeDtypeStruct((MOE_ROWS * NCH, 128), F32),
        compiler_params=_cp(("arbitrary",)),
        name="dispatch",
    )(meta, pos, h2, zrow)


def _experts_kernel(meta_ref, xs_ref, wg_hbm, wu_hbm, wd_hbm, ys_ref,
                    wg_f, wu_f, wd_f, wg_s, wu_s, wd_s, sem, *, l):
    j = pl.program_id(0)
    live = j < meta_ref[META_NT, 0]
    e = meta_ref[META_TILE_E, j]
    e_prev = meta_ref[META_TILE_E, jnp.maximum(j - 1, 0)]

    def fetch(ex):
        return (pltpu.make_async_copy(wg_hbm.at[l, ex], wg_f, sem.at[0]),
                pltpu.make_async_copy(wu_hbm.at[l, ex], wu_f, sem.at[1]),
                pltpu.make_async_copy(wd_hbm.at[l, ex], wd_f, sem.at[2]))

    @pl.when(j == 0)
    def _():
        for c in fetch(e):
            c.start()

    @pl.when(live & ((j == 0) | (e != e_prev)))
    def _():
        for c, dst, src in zip(fetch(e), (wg_s, wu_s, wd_s), (wg_f, wu_f, wd_f)):
            c.wait()
            dst[...] = src[...].astype(BF16)
        nxt = meta_ref[META_NEXT_E, e]

        @pl.when(nxt >= 0)
        def _():
            for c in fetch(nxt):
                c.start()

    @pl.when(live)
    def _():
        x = _load_row_tiles(xs_ref, MOE_TM).astype(BF16)
        g = _dot(x, wg_s[...])
        u = _dot(x, wu_s[...])
        act = (g * _sigmoid(g)) * u
        _store_row_tiles(ys_ref, _dot(act.astype(BF16), wd_s[...]))

    @pl.when(jnp.logical_not(live))
    def _():
        ys_ref[...] = jnp.zeros_like(ys_ref)


def _experts(meta, xs, w_gate_e, w_up_e, w_down_e, l):
    def tile(j, meta):
        return jnp.minimum(j, meta[META_NT, 0] - 1)

    return pl.pallas_call(
        functools.partial(_experts_kernel, l=l),
        grid_spec=pltpu.PrefetchScalarGridSpec(
            num_scalar_prefetch=1,
            grid=(MOE_NT,),
            in_specs=[
                pl.BlockSpec((MOE_TM * NCH, 128), lambda j, meta: (tile(j, meta), 0)),
                pl.BlockSpec(memory_space=pl.ANY),
                pl.BlockSpec(memory_space=pl.ANY),
                pl.BlockSpec(memory_space=pl.ANY),
            ],
            out_specs=pl.BlockSpec((MOE_TM * NCH, 128), lambda j, meta: (j, 0)),
            scratch_shapes=[pltpu.VMEM((D, D_EXP), F32), pltpu.VMEM((D, D_EXP), F32), pltpu.VMEM((D_EXP, D), F32),
                            pltpu.VMEM((D, D_EXP), BF16), pltpu.VMEM((D, D_EXP), BF16), pltpu.VMEM((D_EXP, D), BF16),
                            pltpu.SemaphoreType.DMA((3,))],
        ),
        out_shape=jax.ShapeDtypeStruct((MOE_ROWS * NCH, 128), F32),
        compiler_params=_cp(("arbitrary",)),
        name="experts",
    )(meta, xs, w_gate_e, w_up_e, w_down_e)


COMB_TM = SEG


def _combine_kernel(pos_ref, w_ref, x1_ref, mod_ref, fg_ref, ys_hbm, oa_ref, ob_ref, buf, y_s, sem, *, final):
    i = pl.program_id(0)
    n = pl.num_programs(0) - 1
    n_ctx = T_CTX // COMB_TM

    for s in range(2):
        @pl.when((i < n) & (lax.rem(i, 2) == s))
        def _():
            for r in range(COMB_TM):
                for k in range(2):
                    src = pl.ds(pl.multiple_of(pos_ref[k, r] * NCH, NCH), NCH)
                    pltpu.make_async_copy(ys_hbm.at[src, :], buf.at[s, k, pl.ds(r * NCH, NCH), :],
                                          sem.at[s]).start(priority=k)

    for slot in range(2):
        @pl.when((i > 0) & (lax.rem(i - 1, 2) == slot))
        def _():
            for k in range(2):
                pltpu.make_async_copy(ys_hbm.at[pl.ds(0, COMB_TM * NCH), :], buf.at[slot, k], sem.at[slot]).wait()
            w = w_ref[...]
            y = (w[:, 0:1] * _load_row_tiles(buf.at[slot, 0], COMB_TM)
                 + w[:, 1:2] * _load_row_tiles(buf.at[slot, 1], COMB_TM))
            y_s[...] = y

    @pl.when(i > 0)
    def _():
        x = x1_ref[...] + mod_ref[5:6, :] * y_s[...]
        if final:
            ms = jnp.mean(x * x, axis=-1, keepdims=True)
            x = x * lax.rsqrt(ms + EPS) * fg_ref[...]

        @pl.when(i - 1 < n_ctx)
        def _():
            oa_ref[...] = x

        @pl.when(i - 1 >= n_ctx)
        def _():
            ob_ref[...] = x


def _combine(pos, wts_t, x1, modseg, final_g, ys, l, final):
    n = T // COMB_TM
    n_ctx = T_CTX // COMB_TM

    def done(i):
        return jnp.maximum(i - 1, 0)

    return pl.pallas_call(
        functools.partial(_combine_kernel, final=final),
        grid=(n + 1,),
        in_specs=[
            pl.BlockSpec((2, COMB_TM), lambda i: (0, jnp.minimum(i, n - 1)), memory_space=pltpu.SMEM),
            pl.BlockSpec((COMB_TM, 2), lambda i: (done(i), 0)),
            pl.BlockSpec((COMB_TM, D), lambda i: (done(i), 0)),
            pl.BlockSpec((None, None, 8, D), lambda i: (l, done(i), 0, 0)),
            pl.BlockSpec((1, D), lambda i: (0, 0)),
            pl.BlockSpec(memory_space=pl.ANY),
        ],
        out_specs=[pl.BlockSpec((COMB_TM, D), lambda i: (jnp.minimum(done(i), n_ctx - 1), 0)),
                   pl.BlockSpec((COMB_TM, D), lambda i: (jnp.maximum(done(i) - n_ctx, 0), 0))],
        out_shape=[jax.ShapeDtypeStruct((T_CTX, D), F32), jax.ShapeDtypeStruct((T_LAT, D), F32)],
        scratch_shapes=[pltpu.VMEM((2, 2, COMB_TM * NCH, 128), F32), pltpu.VMEM((COMB_TM, D), F32),
                        pltpu.SemaphoreType.DMA((2,))],
        compiler_params=_cp(("arbitrary",)),
        name="combine",
    )(pos, wts_t, x1, modseg, final_g.reshape(1, D), ys)


def _rope_tables():
    n = DEC_SEQ
    pos_row = np.repeat(np.arange(n // GRID_W, dtype=np.float32), GRID_W)
    pos_col = np.tile(np.arange(GRID_W, dtype=np.float32), n // GRID_W)
    half = HD // 2
    inv_freq = jnp.asarray(ROPE_THETA, F32) ** (-jnp.arange(0, half, 2, dtype=F32) / half)
    ang = jnp.concatenate([jnp.asarray(pos_row)[:, None] * inv_freq,
                           jnp.asarray(pos_col)[:, None] * inv_freq], axis=-1)
    cos, sin = jnp.cos(ang), jnp.sin(ang)
    cos128 = jnp.tile(cos, (1, 4))
    sin128 = jnp.tile(jnp.concatenate([-sin, sin], axis=-1), (1, 2))
    return cos128, sin128


def _head_mean_matrix():
    idx = np.arange(2 * HD)
    same = (idx[:, None] // HD) == (idx[None, :] // HD)
    return jnp.asarray(same.astype(np.float32) / HD, BF16)


_SEG_ROWS = np.array([0] * (T_CTX // SEG) + [1 + b for b in range(DEC_BATCH) for _ in range(DEC_SEQ // SEG)])


def kernel(x_prompt, x_sample, cache_k, cache_v, state_rec, c, c_ctx, w_mod, b_mod, norm1_g, norm2_g, w_in, conv_w, conv_b, rg_wa, rg_ba, rg_wx, rg_bx, rg_lambda, q_norm_g, k_norm_g, w_rec_out, w_att_out, w_out, w_router, router_bias, w_gate_e, w_up_e, w_down_e, final_g):
    xa, xb = x_prompt.reshape(T_CTX, D), x_sample.reshape(T_LAT, D)

    cvecs = jnp.concatenate([c_ctx[None, :], c, jnp.zeros((3, D), F32)], axis=0)
    mods = _mods(cvecs, w_mod, b_mod).reshape(DEPTH, 8, 6, D)
    modseg = jnp.pad(mods[:, _SEG_ROWS], ((0, 0), (0, 0), (0, 2), (0, 0)))

    cos128, sin128 = _rope_tables()
    bd = _head_mean_matrix()
    qg128 = jnp.tile(q_norm_g, (1, 2)).reshape(DEPTH, 1, 2 * HD)
    kg128 = jnp.tile(k_norm_g, (1, 2)).reshape(DEPTH, 1, 2 * HD)
    wg = jnp.concatenate([rg_wa[:, 0], rg_wx[:, 0], rg_wa[:, 1], rg_wx[:, 1]], axis=-1)
    pvec = jnp.stack([rg_ba[:, 0], rg_bx[:, 0], rg_ba[:, 1], rg_bx[:, 1],
                      rg_lambda[:, 0], rg_lambda[:, 1], conv_b, jnp.zeros_like(conv_b)], axis=1)
    wrt = w_router.T
    rbias = router_bias.reshape(N_EXP, 1)
    zrow = jnp.zeros((MOE_TM * NCH, 128), F32)

    new_k, new_v, new_s = [], [], []
    for l in range(DEPTH):
        proj = _inproj(xa, xb, modseg, norm1_g, w_in, l)
        h0 = jnp.concatenate([jnp.zeros((T_CTX // UNIT, 2, D), F32), state_rec[:, l]], axis=0)
        yrec, stf, stb = _rec(proj, conv_w, pvec, wg, h0, l)
        qc, kc, vc = _qkv(proj, qg128, kg128, cos128, sin128, bd, l, latent=False)
        ql, kl, vl = _qkv(proj, qg128, kg128, cos128, sin128, bd, l, latent=True)
        o_ctx = _attn_ctx(qc, kc, vc)
        o_lat = _attn_lat(ql, cache_k, cache_v, kl, vl, l)
        x1, h2, idx, wts = _merge(yrec, o_ctx, o_lat, proj, xa, xb, modseg, norm2_g,
                                  w_rec_out, w_att_out, w_out, wrt, rbias, l)
        pos, meta = _route_pos(idx)
        xs = _dispatch(meta, pos, h2, zrow)
        ys = _experts(meta, xs, w_gate_e, w_up_e, w_down_e, l)
        xa, xb = _combine(pos, wts.T, x1, modseg, final_g, ys, l, final=(l == DEPTH - 1))
        new_k.append(kc)
        new_v.append(vc)
        n_cu = T_CTX // UNIT
        spu = UNIT // SEQ
        hf_last = stf[:n_cu].reshape(n_cu, spu, 2, D)[:, :, 1].reshape(BATCH, D)
        hb_first = stb[:n_cu].reshape(n_cu, spu, 2, D)[:, :, 0].reshape(BATCH, D)
        new_s.append(jnp.stack([hf_last, hb_first], axis=1))

    y_prompt = xa.reshape(BATCH, SEQ, D)
    y_sample = xb.reshape(DEC_BATCH, DEC_SEQ, D)
    return (y_prompt, y_sample, jnp.stack(new_k, axis=1), jnp.stack(new_v, axis=1), jnp.stack(new_s, axis=1))
```

```python
import functools

import numpy as np
import jax
import jax.numpy as jnp
from jax import lax
from jax.experimental import pallas as pl
from jax.experimental.pallas import tpu as pltpu

F32 = jnp.float32
BF16 = jnp.bfloat16

D = 1024
BATCH = 16
SEQ = 256
DEPTH = 2
DEC_BATCH = 4
DEC_SEQ = 1024
PAST = 256
GRID_W = 64
N_HEADS = 16
N_KV = 4
HD = 64
N_RG_BLK = 8
RG_BLK = 128
RG_C = 8.0
N_EXP = 16
D_EXP = 512
ROPE_THETA = 10000.0
EPS = 1e-6
P_IN = 5632
TINY = float(np.finfo(np.float32).tiny)
NEG_LOG2E = -float(np.log2(np.e))

T_CTX = BATCH * SEQ
T_LAT = DEC_BATCH * DEC_SEQ
T = T_CTX + T_LAT
SEG = 256
N_SEG = T // SEG
UNIT = 1024
N_UNIT = T // UNIT
CHUNK = UNIT // 8
CSTRIDE = CHUNK + 8

VMEM_LIMIT = 56 * 1024 * 1024


def _cp(sem):
    return pltpu.CompilerParams(dimension_semantics=sem, vmem_limit_bytes=VMEM_LIMIT)


def _split(x):
    hi = x.astype(BF16)
    lo = (x - hi.astype(F32)).astype(BF16)
    return hi, lo


def _sigmoid(x):
    return 0.5 * jnp.tanh(0.5 * x) + 0.5


NCH = D // 128


def _store_row_tiles(ref, x):
    n = x.shape[0]
    for c in range(NCH):
        ref[pl.ds(c, n, stride=NCH), :] = x[:, c * 128:(c + 1) * 128]


def _load_row_tiles(ref, n):
    return jnp.concatenate([ref[pl.ds(c, n, stride=NCH), :] for c in range(NCH)], axis=-1)


def _dot(a, b):
    return jnp.dot(a, b, preferred_element_type=F32)


def _dot_nt(a, b):
    return lax.dot_general(a, b, (((1,), (1,)), ((), ())), preferred_element_type=F32)


def _mods_kernel(c_ref, w_ref, b_ref, o_ref):
    c = c_ref[...]
    s = c * jax.nn.sigmoid(c)
    s_hi, s_lo = _split(s)
    w_hi, w_lo = _split(w_ref[...])
    o_ref[...] = _dot(s_hi, w_hi) + _dot(s_hi, w_lo) + _dot(s_lo, w_hi) + b_ref[...]


def _mods(cvecs, w_mod, b_mod):
    tn = 1536
    return pl.pallas_call(
        _mods_kernel,
        grid=(DEPTH, 6 * D // tn),
        in_specs=[
            pl.BlockSpec((8, D), lambda l, j: (0, 0)),
            pl.BlockSpec((None, D, tn), lambda l, j: (l, 0, j)),
            pl.BlockSpec((None, 1, tn), lambda l, j: (l, 0, j)),
        ],
        out_specs=pl.BlockSpec((None, 8, tn), lambda l, j: (l, 0, j)),
        out_shape=jax.ShapeDtypeStruct((DEPTH, 8, 6 * D), F32),
        compiler_params=_cp(("arbitrary", "arbitrary")),
        name="mods",
    )(cvecs, w_mod, b_mod.reshape(DEPTH, 1, 6 * D))


def _norm_mod(x, g, shift, scale):
    ms = jnp.mean(x * x, axis=-1, keepdims=True)
    return x * lax.rsqrt(ms + EPS) * g * (1.0 + scale) + shift


def _two_part_specs(tm, n_ctx):
    return [pl.BlockSpec((tm, D), lambda i, *_: (jnp.minimum(i, n_ctx - 1), 0)),
            pl.BlockSpec((tm, D), lambda i, *_: (jnp.maximum(i - n_ctx, 0), 0))]


def _inproj_kernel(xa_ref, xb_ref, mod_ref, g_ref, w_ref, o_ref, h_ref, *, tm):
    def prologue(x_ref):
        def seg(s, carry):
            r0 = pl.multiple_of(s * SEG, SEG)
            m = mod_ref[s]
            h = _norm_mod(x_ref[pl.ds(r0, SEG), :], g_ref[...], m[0:1, :], m[1:2, :])
            h_ref[pl.ds(r0, SEG), :] = h.astype(BF16)
            return carry
        lax.fori_loop(0, tm // SEG, seg, 0)

    first = pl.program_id(1) == 0
    is_ctx = pl.program_id(0) < T_CTX // tm

    @pl.when(first & is_ctx)
    def _():
        prologue(xa_ref)

    @pl.when(first & jnp.logical_not(is_ctx))
    def _():
        prologue(xb_ref)

    o_ref[...] = _dot(h_ref[...], w_ref[...].astype(BF16)).astype(BF16)


def _inproj(xa, xb, modseg, norm_g, w_in, l):
    tm, tn = 2048, 512
    return pl.pallas_call(
        functools.partial(_inproj_kernel, tm=tm),
        grid=(T // tm, P_IN // tn),
        in_specs=_two_part_specs(tm, T_CTX // tm) + [
            pl.BlockSpec((None, tm // SEG, 8, D), lambda i, j: (l, i, 0, 0)),
            pl.BlockSpec((None, 1, D), lambda i, j: (l, 0, 0)),
            pl.BlockSpec((None, D, tn), lambda i, j: (l, 0, j)),
        ],
        out_specs=pl.BlockSpec((tm, tn), lambda i, j: (i, j)),
        out_shape=jax.ShapeDtypeStruct((T, P_IN), BF16),
        scratch_shapes=[pltpu.VMEM((tm, D), BF16)],
        compiler_params=_cp(("arbitrary", "arbitrary")),
        name="inproj",
    )(xa, xb, modseg, norm_g.reshape(DEPTH, 1, D), w_in)


REC_CW = 512
PAD_F = 16
PAD_B = 8


def _rec_kernel(xr_ref, gate_ref, cw_ref, pv_ref, wg_ref, h0_ref,
                y_ref, stf_ref, stb_ref,
                xs_ref, af_ref, bf_ref, ab_ref, bb_ref, nat_ref, wgh_ref):
    u = pl.program_id(0)
    is_ctx = u < (T_CTX // UNIT)
    cps = jnp.where(is_ctx, SEQ // CHUNK, DEC_SEQ // CHUNK)
    nblk = REC_CW // RG_BLK

    def lanes(n):
        return slice(n * RG_BLK, (n + 1) * RG_BLK)

    def tile(r):
        return slice(8 * r, 8 * r + 8)

    for c in range(8):
        for n in range(nblk):
            nat_ref[n, c * CSTRIDE:c * CSTRIDE + CHUNK, :] = xr_ref[c * CHUNK:(c + 1) * CHUNK, lanes(n)].astype(F32)
    for r in range(CHUNK):
        for n in range(nblk):
            xs_ref[PAD_F + 8 * r:PAD_F + 8 * r + 8, lanes(n)] = nat_ref[n, pl.ds(r, 8, stride=CSTRIDE), :]
    chunk_id = lax.broadcasted_iota(jnp.int32, (8, 1), 0)
    seq_start = jnp.bitwise_and(chunk_id, cps - 1) == 0
    seq_end = jnp.bitwise_and(chunk_id, cps - 1) == cps - 1
    for j, r in ((0, CHUNK - 2), (1, CHUNK - 1)):
        prev_chunk = pltpu.roll(xs_ref[PAD_F + 8 * r:PAD_F + 8 * r + 8, :], 1, 0)
        xs_ref[tile(j), :] = jnp.where(seq_start, 0.0, prev_chunk)
    next_chunk = pltpu.roll(xs_ref[PAD_F:PAD_F + 8, :], 7, 0)
    xs_ref[PAD_F + UNIT:PAD_F + UNIT + PAD_B, :] = jnp.where(seq_end, 0.0, next_chunk)

    pv = pv_ref[...]
    cwts = cw_ref[...]
    conv_b = pv[6:7, :]

    def softplus_neg(lam):
        z = -lam
        return jnp.maximum(z, 0.0) + jnp.log1p(jnp.exp(-jnp.abs(z)))

    c4s = tuple((0.5 * RG_C) * softplus_neg(pv[4 + d:5 + d, :]) for d in range(2))
    pv_h = 0.5 * pv
    for n in range(nblk):
        wgh_ref[n] = (0.5 * wg_ref[n]).astype(BF16)
    a_refs = (af_ref, ab_ref)
    b_refs = (bf_ref, bb_ref)

    def gates(g, carry):
        base = pl.multiple_of(g * CHUNK, CHUNK)

        def tap(d):
            return xs_ref[pl.ds(pl.multiple_of(base + PAD_F + 8 * d, 8), CHUNK), :]

        xc = conv_b + tap(-2) * cwts[0:1, :]
        xc = xc + tap(-1) * cwts[1:2, :]
        xc = xc + tap(0) * cwts[2:3, :]
        xc = xc + tap(1) * cwts[3:4, :]
        for n in range(nblk):
            ls = lanes(n)
            xn = xc[:, ls]
            hx = 0.5 * xn
            pre_h = _dot(xn.astype(BF16), wgh_ref[n])
            for d in range(2):
                th_r = jnp.tanh(pre_h[:, (2 * d) * RG_BLK:(2 * d + 1) * RG_BLK] + pv_h[2 * d:2 * d + 1, ls])
                th_i = jnp.tanh(pre_h[:, (2 * d + 1) * RG_BLK:(2 * d + 2) * RG_BLK] + pv_h[2 * d + 1:2 * d + 2, ls])
                c4 = c4s[d][:, ls]
                nla = c4 * th_r + c4
                a = jnp.exp2(nla * NEG_LOG2E)
                s = jnp.tanh(nla) * (a * a + 1.0)
                inp = (s * lax.rsqrt(jnp.maximum(s, TINY))) * (hx * th_i + hx)
                a_refs[d][pl.ds(base, CHUNK), ls] = a
                b_refs[d][pl.ds(base, CHUNK), ls] = inp
        return carry

    lax.fori_loop(0, 8, gates, 0)

    hf = hb = jnp.zeros((8, REC_CW), F32)
    pf = pb = jnp.ones((8, REC_CW), F32)
    for r in range(CHUNK):
        rf, rb = tile(r), tile(CHUNK - 1 - r)
        a = af_ref[rf, :]
        hf = a * hf + bf_ref[rf, :]
        pf = a * pf
        bf_ref[rf, :] = hf
        af_ref[rf, :] = pf
        a = ab_ref[rb, :]
        hb = a * hb + bb_ref[rb, :]
        pb = a * pb
        bb_ref[rb, :] = hb
        ab_ref[rb, :] = pb

    h0f = h0_ref[0:1, :]
    h0b = h0_ref[1:2, :]
    cf = [h0f]
    for c in range(1, 8):
        chain = hf[c - 1:c, :] + pf[c - 1:c, :] * cf[c - 1]
        cf.append(jnp.where(jnp.bitwise_and(c, cps - 1) == 0, h0f, chain))
    cb = [None] * 8
    cb[7] = h0b
    for c in range(6, -1, -1):
        chain = hb[c + 1:c + 2, :] + pb[c + 1:c + 2, :] * cb[c + 1]
        cb[c] = jnp.where(jnp.bitwise_and(c, cps - 1) == cps - 1, h0b, chain)
    carry_f = jnp.concatenate(cf, axis=0)
    carry_b = jnp.concatenate(cb, axis=0)
    stf_ref[...] = hf + pf * carry_f
    stb_ref[...] = hb + pb * carry_b

    for r in range(CHUNK):
        h = (bf_ref[tile(r), :] + af_ref[tile(r), :] * carry_f) + (bb_ref[tile(r), :] + ab_ref[tile(r), :] * carry_b)
        for n in range(nblk):
            nat_ref[n, pl.ds(r, 8, stride=CSTRIDE), :] = h[:, lanes(n)]

    for c in range(8):
        rows = slice(c * CHUNK, (c + 1) * CHUNK)
        for n in range(nblk):
            g = gate_ref[rows, lanes(n)].astype(F32)
            h = nat_ref[n, c * CSTRIDE:c * CSTRIDE + CHUNK, :]
            y_ref[rows, lanes(n)] = (h * jax.nn.gelu(g, approximate=True)).astype(BF16)


def _rec(proj, conv_w, pvec, wg, h0, l):
    ncb = D // REC_CW
    return pl.pallas_call(
        _rec_kernel,
        grid=(N_UNIT, ncb),
        in_specs=[
            pl.BlockSpec((UNIT, REC_CW), lambda u, c: (u, c)),
            pl.BlockSpec((UNIT, REC_CW), lambda u, c: (u, ncb + c)),
            pl.BlockSpec((None, 4, REC_CW), lambda u, c: (l, 0, c)),
            pl.BlockSpec((None, 8, REC_CW), lambda u, c: (l, 0, c)),
            pl.BlockSpec((None, REC_CW // RG_BLK, RG_BLK, 4 * RG_BLK), lambda u, c: (l, c, 0, 0)),
            pl.BlockSpec((None, 2, REC_CW), lambda u, c: (u, 0, c)),
        ],
        out_specs=[
            pl.BlockSpec((UNIT, REC_CW), lambda u, c: (u, c)),
            pl.BlockSpec((None, 8, REC_CW), lambda u, c: (u, 0, c)),
            pl.BlockSpec((None, 8, REC_CW), lambda u, c: (u, 0, c)),
        ],
        out_shape=[
            jax.ShapeDtypeStruct((T, D), BF16),
            jax.ShapeDtypeStruct((N_UNIT, 8, D), F32),
            jax.ShapeDtypeStruct((N_UNIT, 8, D), F32),
        ],
        scratch_shapes=[pltpu.VMEM((PAD_F + UNIT + PAD_B, REC_CW), F32)]
        + [pltpu.VMEM((UNIT, REC_CW), F32)] * 4
        + [pltpu.VMEM((REC_CW // RG_BLK, 8 * CSTRIDE, RG_BLK), F32),
           pltpu.VMEM((REC_CW // RG_BLK, RG_BLK, 4 * RG_BLK), BF16)],
        compiler_params=_cp(("arbitrary", "arbitrary")),
        name="rec",
    )(proj, proj, conv_w, pvec, wg, h0)


def _head_norm(x, g128, bd):
    hi, lo = _split(x * x)
    ms = _dot(hi, bd) + _dot(lo, bd)
    return x * lax.rsqrt(ms + EPS) * g128


def _rope(x, cos, sin_signed):
    lane = lax.broadcasted_iota(jnp.int32, x.shape, 1)
    first_half = jnp.bitwise_and(lane, HD - 1) < HD // 2
    partner = jnp.where(first_half, pltpu.roll(x, 2 * HD - HD // 2, 1), pltpu.roll(x, HD // 2, 1))
    return x * cos + partner * sin_signed


def _qkv_kernel(q_ref, k_ref, v_ref, qg_ref, kg_ref, cos_ref, sin_ref, bd_ref, *rest, rope, slab):
    qo_ref, ko_ref, vo_ref = rest[-3:]
    if slab is not None:
        if slab > 0:
            pk_ref, pv_ref = rest[:2]
            for p in range(slab):
                ko_ref[p] = pk_ref[p]
                vo_ref[p] = pv_ref[p]
        ko_ref, vo_ref = ko_ref.at[slab], vo_ref.at[slab]
    bd = bd_ref[...]
    scale = HD ** -0.5 * float(np.log2(np.e))
    for j in range(N_HEADS // 2):
        x = _head_norm(q_ref[:, 2 * HD * j:2 * HD * (j + 1)].astype(F32), qg_ref[...], bd)
        if rope:
            x = _rope(x, cos_ref[...], sin_ref[...])
        x = x * scale
        qo_ref[2 * j] = x[:, :HD].astype(qo_ref.dtype)
        qo_ref[2 * j + 1] = x[:, HD:].astype(qo_ref.dtype)
    for j in range(N_KV // 2):
        x = _head_norm(k_ref[:, 2 * HD * j:2 * HD * (j + 1)].astype(F32), kg_ref[...], bd)
        if rope:
            x = _rope(x, cos_ref[...], sin_ref[...])
        ko_ref[2 * j] = x[:, :HD].astype(ko_ref.dtype)
        ko_ref[2 * j + 1] = x[:, HD:].astype(ko_ref.dtype)
        v = v_ref[:, 2 * HD * j:2 * HD * (j + 1)].astype(F32)
        vo_ref[2 * j] = v[:, :HD].astype(vo_ref.dtype)
        vo_ref[2 * j + 1] = v[:, HD:].astype(vo_ref.dtype)


def _qkv(proj, qg128, kg128, cos128, sin128, bd, l, latent, prev_caches=()):
    tm = SEG
    n = T_LAT // tm if latent else T_CTX // tm
    roff = T_CTX // tm if latent else 0
    per_seq = DEC_SEQ // tm
    prev_specs = []
    if latent:
        kv_shape = (DEC_BATCH, N_KV, DEC_SEQ, HD)
        kv_spec = pl.BlockSpec((None, N_KV, tm, HD), lambda i: (i // per_seq, 0, i % per_seq, 0))
        kv_dtype = BF16
        tab_map = lambda i: (i % per_seq, 0)
    else:
        kv_shape = (BATCH, l + 1, N_KV, SEQ, HD)
        kv_spec = pl.BlockSpec((None, l + 1, N_KV, tm, HD), lambda i: (i, 0, 0, 0, 0))
        kv_dtype = F32
        tab_map = lambda i: (0, 0)
        if l > 0:
            prev_specs = [pl.BlockSpec((None, l, N_KV, tm, HD), lambda i: (i, 0, 0, 0, 0))] * 2
    return pl.pallas_call(
        functools.partial(_qkv_kernel, rope=latent, slab=None if latent else l),
        grid=(n,),
        in_specs=[
            pl.BlockSpec((tm, D), lambda i: (roff + i, 2)),
            pl.BlockSpec((tm, N_KV * HD), lambda i: (roff + i, 3 * D // (N_KV * HD))),
            pl.BlockSpec((tm, N_KV * HD), lambda i: (roff + i, 3 * D // (N_KV * HD) + 1)),
            pl.BlockSpec((None, 1, 2 * HD), lambda i: (l, 0, 0)),
            pl.BlockSpec((None, 1, 2 * HD), lambda i: (l, 0, 0)),
            pl.BlockSpec((tm, 2 * HD), tab_map),
            pl.BlockSpec((tm, 2 * HD), tab_map),
            pl.BlockSpec((2 * HD, 2 * HD), lambda i: (0, 0)),
        ] + prev_specs,
        out_specs=[
            pl.BlockSpec((N_HEADS, tm, HD), lambda i: (0, i, 0)),
            kv_spec,
            kv_spec,
        ],
        out_shape=[
            jax.ShapeDtypeStruct((N_HEADS, n * tm, HD), BF16),
            jax.ShapeDtypeStruct(kv_shape, kv_dtype),
            jax.ShapeDtypeStruct(kv_shape, kv_dtype),
        ],
        compiler_params=_cp(("arbitrary",)),
        name="qkv_lat" if latent else "qkv_ctx",
    )(proj, proj, proj, qg128, kg128, cos128, sin128, bd, *prev_caches)


def _with_ones(v):
    return jnp.concatenate([v, jnp.ones_like(v)], axis=-1)


def _softmax_pv(q, k, v_ext):
    s = _dot_nt(q, k)
    m = jnp.max(s, axis=-1, keepdims=True)
    p = jnp.exp2(s - m).astype(BF16)
    r = _dot(p, v_ext)
    return r[:, :HD] / r[:, HD:HD + 1]


def _attend_heads(q_ref, k, v_ext):
    return jnp.concatenate([_softmax_pv(q_ref[h], k, v_ext) for h in range(N_HEADS // N_KV)], axis=-1)


def _attn_ctx_kernel(q_ref, k_ref, v_ref, o_ref):
    g = N_HEADS // N_KV
    for kv in range(CTX_KV_PER_STEP):
        q = q_ref[kv * g:(kv + 1) * g].reshape(g * SEQ, HD)
        o = _softmax_pv(q, k_ref[kv].astype(BF16), _with_ones(v_ref[kv].astype(BF16)))
        o = jnp.concatenate([o[h * SEQ:(h + 1) * SEQ] for h in range(g)], axis=-1)
        o_ref[:, kv * g * HD:(kv + 1) * g * HD] = o.astype(BF16)


CTX_KV_PER_STEP = 1


def _attn_ctx(qh, kc, vc, l):
    g = N_HEADS // N_KV * CTX_KV_PER_STEP
    return pl.pallas_call(
        _attn_ctx_kernel,
        grid=(BATCH, N_KV // CTX_KV_PER_STEP),
        in_specs=[
            pl.BlockSpec((g, SEQ, HD), lambda b, h: (h, b, 0)),
            pl.BlockSpec((None, None, CTX_KV_PER_STEP, SEQ, HD), lambda b, h: (b, l, h, 0, 0)),
            pl.BlockSpec((None, None, CTX_KV_PER_STEP, SEQ, HD), lambda b, h: (b, l, h, 0, 0)),
        ],
        out_specs=pl.BlockSpec((SEQ, g * HD), lambda b, h: (b, h)),
        out_shape=jax.ShapeDtypeStruct((T_CTX, D), BF16),
        compiler_params=_cp(("arbitrary", "arbitrary")),
        name="attn_ctx",
    )(qh, kc, vc)


ATT_TQ = 1024


def _attn_lat_kernel(q_ref, pk_ref, pv_ref, k_ref, v_ref, o_ref, k_s, v_s):
    k_s[0:PAST, :] = pk_ref[...].astype(BF16)
    k_s[PAST:, :] = k_ref[...]
    v_s[0:PAST, :] = _with_ones(pv_ref[...].astype(BF16))
    v_s[PAST:, :] = _with_ones(v_ref[...])

    def q_tile(qi, carry):
        rows = pl.ds(pl.multiple_of(qi * ATT_TQ, ATT_TQ), ATT_TQ)
        o = jnp.concatenate([_softmax_pv(q_ref[h, rows, :], k_s[...], v_s[...])
                             for h in range(N_HEADS // N_KV)], axis=-1)
        o_ref[rows, :] = o.astype(BF16)
        return carry

    lax.fori_loop(0, DEC_SEQ // ATT_TQ, q_tile, 0)


def _attn_lat(qh, cache_k, cache_v, kr, vr, l):
    g = N_HEADS // N_KV
    return pl.pallas_call(
        _attn_lat_kernel,
        grid=(DEC_BATCH, N_KV),
        in_specs=[
            pl.BlockSpec((g, DEC_SEQ, HD), lambda b, h: (h, b, 0)),
            pl.BlockSpec((None, None, None, PAST, HD), lambda b, h: (b, l, h, 0, 0)),
            pl.BlockSpec((None, None, None, PAST, HD), lambda b, h: (b, l, h, 0, 0)),
            pl.BlockSpec((None, None, DEC_SEQ, HD), lambda b, h: (b, h, 0, 0)),
            pl.BlockSpec((None, None, DEC_SEQ, HD), lambda b, h: (b, h, 0, 0)),
        ],
        out_specs=pl.BlockSpec((DEC_SEQ, g * HD), lambda b, h: (b, h)),
        out_shape=jax.ShapeDtypeStruct((T_LAT, D), BF16),
        scratch_shapes=[pltpu.VMEM((PAST + DEC_SEQ, HD), BF16), pltpu.VMEM((PAST + DEC_SEQ, 2 * HD), BF16)],
        compiler_params=_cp(("arbitrary", "arbitrary")),
        name="attn_lat",
    )(qh, cache_k, cache_v, kr, vr)


MERGE_TM = 512


def _route(lt, bias):
    rows = [lt[e:e + 1, :] for e in range(N_EXP)]
    m = rows[0]
    for e in range(1, N_EXP):
        m = jnp.maximum(m, rows[e])
    ex = [jnp.exp(r - m) for r in rows]
    z = ex[0]
    for e in range(1, N_EXP):
        z = z + ex[e]
    probs = [x / z for x in ex]
    sel = [probs[e] + bias[e:e + 1, :] for e in range(N_EXP)]

    def top2_sum(v):
        a, b = jnp.maximum(v[0], v[1]), jnp.minimum(v[0], v[1])
        c, d = jnp.maximum(v[2], v[3]), jnp.minimum(v[2], v[3])
        return jnp.maximum(a, c) + jnp.maximum(jnp.minimum(a, c), jnp.maximum(b, d))

    scores = [top2_sum(sel[4 * g:4 * g + 4]) for g in range(4)]
    best = jnp.zeros_like(scores[0], dtype=jnp.int32)
    best_s = scores[0]
    for g in range(1, 4):
        take = scores[g] > best_s
        best = jnp.where(take, g, best)
        best_s = jnp.where(take, scores[g], best_s)
    cs, cp = [], []
    for j in range(4):
        s_j, p_j = sel[j], probs[j]
        for g in range(1, 4):
            s_j = jnp.where(best == g, sel[4 * g + j], s_j)
            p_j = jnp.where(best == g, probs[4 * g + j], p_j)
        cs.append(s_j)
        cp.append(p_j)
    neg = jnp.full_like(cs[0], -jnp.inf)

    def argmax4(v):
        bi = jnp.zeros_like(best)
        bv = v[0]
        for j in range(1, 4):
            take = v[j] > bv
            bi = jnp.where(take, j, bi)
            bv = jnp.where(take, v[j], bv)
        return bi

    def pick(v, idx):
        out = v[0]
        for j in range(1, 4):
            out = jnp.where(idx == j, v[j], out)
        return out

    i1 = argmax4(cs)
    cs2 = [jnp.where(i1 == j, neg, cs[j]) for j in range(4)]
    i2 = argmax4(cs2)
    i2 = jnp.where((i2 == 0) & (i1 == 0), 1, i2)
    w1, w2 = pick(cp, i1), pick(cp, i2)
    den = w1 + w2
    return best * 4 + i1, best * 4 + i2, w1 / den, w2 / den


def _merge_kernel(yrec_ref, oa_ref, ob_ref, gr0_ref, gr1_ref, ga0_ref, ga1_ref, xa_ref, xb_ref, mod_ref, g2_ref,
                  wrec_ref, watt_ref, wout_ref, wrt_ref, rb_ref,
                  x1_ref, h2_ref, idx_ref, wts_ref,
                  wrec_s, watt_s, wout_s):
    @pl.when(pl.program_id(0) == 0)
    def _():
        wrec_s[...] = wrec_ref[...].astype(BF16)
        watt_s[...] = watt_ref[...].astype(BF16)
        wout_s[...] = wout_ref[...].astype(BF16)

    is_ctx = pl.program_id(0) < T_CTX // MERGE_TM
    args = (yrec_ref, gr0_ref, gr1_ref, ga0_ref, ga1_ref, mod_ref, g2_ref, wrt_ref, rb_ref,
            x1_ref, h2_ref, idx_ref, wts_ref, wrec_s, watt_s, wout_s)

    @pl.when(is_ctx)
    def _():
        _merge_body(oa_ref, xa_ref, *args)

    @pl.when(jnp.logical_not(is_ctx))
    def _():
        _merge_body(ob_ref, xb_ref, *args)


def _merge_body(oatt_ref, x_ref, yrec_ref, gr0_ref, gr1_ref, ga0_ref, ga1_ref, mod_ref, g2_ref, wrt_ref, rb_ref,
                x1_ref, h2_ref, idx_ref, wts_ref, wrec_s, watt_s, wout_s):
    half = D // 2
    b_rec = _dot(yrec_ref[...], wrec_s[...])
    b_att = _dot(oatt_ref[...], watt_s[...])
    m0 = _sigmoid(gr0_ref[...].astype(F32)) * b_rec[:, :half] + _sigmoid(ga0_ref[...].astype(F32)) * b_att[:, :half]
    m1 = _sigmoid(gr1_ref[...].astype(F32)) * b_rec[:, half:] + _sigmoid(ga1_ref[...].astype(F32)) * b_att[:, half:]
    merged = jnp.concatenate([m0, m1], axis=-1).astype(BF16)
    out = _dot(merged, wout_s[...])

    hs = []
    for s in range(MERGE_TM // SEG):
        rows = slice(s * SEG, (s + 1) * SEG)
        m = mod_ref[s]
        x1 = x_ref[rows, :] + m[2:3, :] * out[rows, :]
        x1_ref[rows, :] = x1
        h2 = _norm_mod(x1, g2_ref[...], m[3:4, :], m[4:5, :])
        hs.append(h2)
    h2 = jnp.concatenate(hs, axis=0)
    _store_row_tiles(h2_ref, h2)

    h_hi, h_lo = _split(h2)
    w_hi, w_lo = _split(wrt_ref[...])
    lt = _dot_nt(w_hi, h_hi) + _dot_nt(w_hi, h_lo) + _dot_nt(w_lo, h_hi)
    e1, e2, w1, w2 = _route(lt, rb_ref[...])
    idx_ref[...] = jnp.concatenate([e1, e2], axis=0)
    wts_ref[...] = jnp.concatenate([w1, w2], axis=0)


def _merge(yrec, o_ctx, o_lat, proj, xa, xb, modseg, norm2_g, w_rec_out, w_att_out, w_out, wrt, rbias, l):
    tm = MERGE_TM
    half = D // 2
    gcol = (3 * D + 2 * N_KV * HD) // half
    wspec = pl.BlockSpec((None, D, D), lambda i: (l, 0, 0))
    return pl.pallas_call(
        _merge_kernel,
        grid=(T // tm,),
        in_specs=[pl.BlockSpec((tm, D), lambda i: (i, 0))] + _two_part_specs(tm, T_CTX // tm) + [
            pl.BlockSpec((tm, half), lambda i: (i, gcol)),
            pl.BlockSpec((tm, half), lambda i: (i, gcol + 1)),
            pl.BlockSpec((tm, half), lambda i: (i, gcol + 2)),
            pl.BlockSpec((tm, half), lambda i: (i, gcol + 3)),
        ] + _two_part_specs(tm, T_CTX // tm) + [
            pl.BlockSpec((None, tm // SEG, 8, D), lambda i: (l, i, 0, 0)),
            pl.BlockSpec((None, 1, D), lambda i: (l, 0, 0)),
            wspec, wspec, wspec,
            pl.BlockSpec((N_EXP, D), lambda i: (0, 0)),
            pl.BlockSpec((N_EXP, 1), lambda i: (0, 0)),
        ],
        out_specs=[
            pl.BlockSpec((tm, D), lambda i: (i, 0)),
            pl.BlockSpec((tm * NCH, 128), lambda i: (i, 0)),
            pl.BlockSpec((2, tm), lambda i: (0, i)),
            pl.BlockSpec((2, tm), lambda i: (0, i)),
        ],
        out_shape=[
            jax.ShapeDtypeStruct((T, D), F32),
            jax.ShapeDtypeStruct((T * NCH, 128), F32),
            jax.ShapeDtypeStruct((2, T), jnp.int32),
            jax.ShapeDtypeStruct((2, T), F32),
        ],
        scratch_shapes=[pltpu.VMEM((D, D), BF16)] * 3,
        compiler_params=_cp(("arbitrary",)),
        name="merge",
    )(yrec, o_ctx, o_lat, proj, proj, proj, proj, xa, xb, modseg, norm2_g.reshape(DEPTH, 1, D),
      w_rec_out, w_att_out, w_out, wrt, rbias)


MOE_TM = 256
MOE_NT = 2 * T // MOE_TM + N_EXP
MOE_ROWS = MOE_NT * MOE_TM
META_TILE_E, META_CNT, META_OFF, META_END, META_NT, META_NEXT_E = 0, 1, 2, 3, 4, 5


def _pos_kernel(idx_ref, pos_ref, meta_ref):
    shift = MOE_TM.bit_length() - 1
    idx = idx_ref[...]
    eid = lax.broadcasted_iota(jnp.int32, (N_EXP, T), 0)
    m0 = eid == idx[0:1, :]
    m1 = eid == idx[1:2, :]
    member = jnp.where(m0 | m1, 1.0, 0.0)
    cnt = jnp.sum(member, axis=1, keepdims=True).astype(jnp.int32)
    ntile = jnp.right_shift(cnt + (MOE_TM - 1), shift)
    offs, acc = [], jnp.zeros((1, 1), jnp.int32)
    for e in range(N_EXP):
        offs.append(acc)
        acc = acc + ntile[e:e + 1, :]
    off_t = jnp.concatenate(offs, axis=0)
    end_t = off_t + ntile

    blk = 256
    r_i = lax.broadcasted_iota(jnp.int32, (blk, blk), 0)
    c_i = lax.broadcasted_iota(jnp.int32, (blk, blk), 1)
    upper = jnp.where(r_i <= c_i, 1.0, 0.0).astype(BF16)
    run = (off_t * MOE_TM).astype(F32)
    for j in range(T // blk):
        ls = slice(j * blk, (j + 1) * blk)
        mb = member[:, ls]
        inc = _dot(mb.astype(BF16), upper)
        dest = run + inc - mb
        pos_ref[0:1, ls] = jnp.sum(jnp.where(m0[:, ls], dest, 0.0), axis=0, keepdims=True).astype(jnp.int32)
        pos_ref[1:2, ls] = jnp.sum(jnp.where(m1[:, ls], dest, 0.0), axis=0, keepdims=True).astype(jnp.int32)
        run = run + inc[:, blk - 1:blk]

    lane = lax.broadcasted_iota(jnp.int32, (1, 128), 1)
    zero = jnp.zeros((1, 128), jnp.int32)
    tile_e, cnt_row, off_row, end_row = zero, zero, zero, zero
    for e in range(N_EXP):
        tile_e = tile_e + jnp.where(lane >= end_t[e:e + 1, :], 1, 0)
        here = lane == e
        cnt_row = jnp.where(here, cnt[e:e + 1, :], cnt_row)
        off_row = jnp.where(here, off_t[e:e + 1, :] * MOE_TM, off_row)
        end_row = jnp.where(here, end_t[e:e + 1, :] * MOE_TM, end_row)
    tile_e = jnp.minimum(tile_e, N_EXP - 1)
    nt_row = zero + acc
    next_row = zero
    nxt = jnp.full((1, 1), -1, jnp.int32)
    for e in reversed(range(N_EXP)):
        next_row = jnp.where(lane == e, nxt, next_row)
        nxt = jnp.where(cnt[e:e + 1, :] > 0, e, nxt)
    meta_ref[...] = jnp.concatenate([tile_e, cnt_row, off_row, end_row, nt_row, next_row, zero, zero], axis=0)


def _route_pos(idx):
    return pl.pallas_call(
        _pos_kernel,
        grid=(1,),
        in_specs=[pl.BlockSpec((2, T), lambda i: (0, 0))],
        out_specs=[pl.BlockSpec((2, T), lambda i: (0, 0)), pl.BlockSpec((8, 128), lambda i: (0, 0))],
        out_shape=[jax.ShapeDtypeStruct((2, T), jnp.int32), jax.ShapeDtypeStruct((8, 128), jnp.int32)],
        compiler_params=_cp(("arbitrary",)),
        name="route_pos",
    )(idx)


DISP_TM = 256


def _dispatch_kernel(meta_ref, pos_ref, h_ref, z_hbm, xs_hbm, sem):
    i = pl.program_id(0)

    def row_copy(src, src_row, dst_row):
        return pltpu.make_async_copy(src.at[pl.ds(src_row * NCH, NCH), :],
                                     xs_hbm.at[pl.ds(pl.multiple_of(dst_row * NCH, NCH), NCH), :], sem)

    for r in range(DISP_TM):
        row_copy(h_ref, r, pos_ref[0, r]).start(priority=0)
        row_copy(h_ref, r, pos_ref[1, r]).start(priority=1)

    e = jnp.minimum(i, N_EXP - 1)
    pad0 = meta_ref[META_OFF, e] + meta_ref[META_CNT, e]
    npad = jnp.where(i < N_EXP, meta_ref[META_END, e] - pad0, 0)

    def pad_copies(act):
        s = pad0
        for bit in reversed(range(MOE_TM.bit_length() - 1)):
            size = 1 << bit
            part = jnp.bitwise_and(npad, size)

            @pl.when(part != 0)
            def _():
                dst = pl.ds(pl.multiple_of(s * NCH, NCH), size * NCH)
                act(pltpu.make_async_copy(z_hbm.at[pl.ds(0, size * NCH), :], xs_hbm.at[dst, :], sem))

            s = s + part

    pad_copies(lambda c: c.start())

    for _ in range(2):
        pltpu.make_async_copy(h_ref, xs_hbm.at[pl.ds(0, DISP_TM * NCH), :], sem).wait()

    tail = meta_ref[META_NT, 0] + i
    has_tail = (i < N_EXP) & (tail < MOE_NT)

    def tail_copy():
        rows = pl.ds(pl.multiple_of(tail * (MOE_TM * NCH), MOE_TM * NCH), MOE_TM * NCH)
        return pltpu.make_async_copy(z_hbm, xs_hbm.at[rows, :], sem)

    @pl.when(has_tail)
    def _():
        tail_copy().start()

    pad_copies(lambda c: c.wait())

    @pl.when(has_tail)
    def _():
        tail_copy().wait()


def _dispatch(meta, pos, h2, zrow):
    return pl.pallas_call(
        _dispatch_kernel,
        grid_spec=pltpu.PrefetchScalarGridSpec(
            num_scalar_prefetch=1,
            grid=(T // DISP_TM,),
            in_specs=[
                pl.BlockSpec((2, DISP_TM), lambda i, meta: (0, i), memory_space=pltpu.SMEM),
                pl.BlockSpec((DISP_TM * NCH, 128), lambda i, meta: (i, 0)),
                pl.BlockSpec((MOE_TM * NCH, 128), lambda i, meta: (0, 0)),
            ],
            out_specs=pl.BlockSpec(memory_space=pl.ANY),
            scratch_shapes=[pltpu.SemaphoreType.DMA],
        ),
        out_shape=jax.ShapeDtypeStruct((MOE_ROWS * NCH, 128), F32),
        compiler_params=_cp(("arbitrary",)),
        name="dispatch",
    )(meta, pos, h2, zrow)


def _experts_kernel(meta_ref, xs_ref, wg_hbm, wu_hbm, wd_hbm, ys_ref,
                    wg_f, wu_f, wd_f, wg_s, wu_s, wd_s, sem, *, l):
    j = pl.program_id(0)
    live = j < meta_ref[META_NT, 0]
    e = meta_ref[META_TILE_E, j]
    e_prev = meta_ref[META_TILE_E, jnp.maximum(j - 1, 0)]

    def fetch(ex):
        return (pltpu.make_async_copy(wg_hbm.at[l, ex], wg_f, sem.at[0]),
                pltpu.make_async_copy(wu_hbm.at[l, ex], wu_f, sem.at[1]),
                pltpu.make_async_copy(wd_hbm.at[l, ex], wd_f, sem.at[2]))

    @pl.when(j == 0)
    def _():
        for c in fetch(e):
            c.start()

    @pl.when(live & ((j == 0) | (e != e_prev)))
    def _():
        for c, dst, src in zip(fetch(e), (wg_s, wu_s, wd_s), (wg_f, wu_f, wd_f)):
            c.wait()
            dst[...] = src[...].astype(BF16)
        nxt = meta_ref[META_NEXT_E, e]

        @pl.when(nxt >= 0)
        def _():
            for c in fetch(nxt):
                c.start()

    @pl.when(live)
    def _():
        x = _load_row_tiles(xs_ref, MOE_TM).astype(BF16)
        g = _dot(x, wg_s[...])
        u = _dot(x, wu_s[...])
        act = (g * _sigmoid(g)) * u
        _store_row_tiles(ys_ref, _dot(act.astype(BF16), wd_s[...]))

    @pl.when(jnp.logical_not(live))
    def _():
        ys_ref[...] = jnp.zeros_like(ys_ref)


def _experts(meta, xs, w_gate_e, w_up_e, w_down_e, l):
    def tile(j, meta):
        return jnp.minimum(j, meta[META_NT, 0] - 1)

    return pl.pallas_call(
        functools.partial(_experts_kernel, l=l),
        grid_spec=pltpu.PrefetchScalarGridSpec(
            num_scalar_prefetch=1,
            grid=(MOE_NT,),
            in_specs=[
                pl.BlockSpec((MOE_TM * NCH, 128), lambda j, meta: (tile(j, meta), 0)),
                pl.BlockSpec(memory_space=pl.ANY),
                pl.BlockSpec(memory_space=pl.ANY),
                pl.BlockSpec(memory_space=pl.ANY),
            ],
            out_specs=pl.BlockSpec((MOE_TM * NCH, 128), lambda j, meta: (j, 0)),
            scratch_shapes=[pltpu.VMEM((D, D_EXP), F32), pltpu.VMEM((D, D_EXP), F32), pltpu.VMEM((D_EXP, D), F32),
                            pltpu.VMEM((D, D_EXP), BF16), pltpu.VMEM((D, D_EXP), BF16), pltpu.VMEM((D_EXP, D), BF16),
                            pltpu.SemaphoreType.DMA((3,))],
        ),
        out_shape=jax.ShapeDtypeStruct((MOE_ROWS * NCH, 128), F32),
        compiler_params=_cp(("arbitrary",)),
        name="experts",
    )(meta, xs, w_gate_e, w_up_e, w_down_e)


COMB_TM = SEG


def _combine_kernel(pos_ref, w_ref, x1_ref, mod_ref, fg_ref, ys_hbm, oa_ref, ob_ref, buf, y_s, sem, *, final):
    i = pl.program_id(0)
    n = pl.num_programs(0) - 1
    n_ctx = T_CTX // COMB_TM

    for s in range(2):
        @pl.when((i < n) & (lax.rem(i, 2) == s))
        def _():
            for r in range(COMB_TM):
                for k in range(2):
                    src = pl.ds(pl.multiple_of(pos_ref[k, r] * NCH, NCH), NCH)
                    pltpu.make_async_copy(ys_hbm.at[src, :], buf.at[s, k, pl.ds(r * NCH, NCH), :],
                                          sem.at[s]).start(priority=k)

    for slot in range(2):
        @pl.when((i > 0) & (lax.rem(i - 1, 2) == slot))
        def _():
            for k in range(2):
                pltpu.make_async_copy(ys_hbm.at[pl.ds(0, COMB_TM * NCH), :], buf.at[slot, k], sem.at[slot]).wait()
            w = w_ref[...]
            y = (w[:, 0:1] * _load_row_tiles(buf.at[slot, 0], COMB_TM)
                 + w[:, 1:2] * _load_row_tiles(buf.at[slot, 1], COMB_TM))
            y_s[...] = y

    @pl.when(i > 0)
    def _():
        x = x1_ref[...] + mod_ref[5:6, :] * y_s[...]
        if final:
            ms = jnp.mean(x * x, axis=-1, keepdims=True)
            x = x * lax.rsqrt(ms + EPS) * fg_ref[...]

        @pl.when(i - 1 < n_ctx)
        def _():
            oa_ref[...] = x

        @pl.when(i - 1 >= n_ctx)
        def _():
            ob_ref[...] = x


def _combine(pos, wts_t, x1, modseg, final_g, ys, l, final):
    n = T // COMB_TM
    n_ctx = T_CTX // COMB_TM

    def done(i):
        return jnp.maximum(i - 1, 0)

    return pl.pallas_call(
        functools.partial(_combine_kernel, final=final),
        grid=(n + 1,),
        in_specs=[
            pl.BlockSpec((2, COMB_TM), lambda i: (0, jnp.minimum(i, n - 1)), memory_space=pltpu.SMEM),
            pl.BlockSpec((COMB_TM, 2), lambda i: (done(i), 0)),
            pl.BlockSpec((COMB_TM, D), lambda i: (done(i), 0)),
            pl.BlockSpec((None, None, 8, D), lambda i: (l, done(i), 0, 0)),
            pl.BlockSpec((1, D), lambda i: (0, 0)),
            pl.BlockSpec(memory_space=pl.ANY),
        ],
        out_specs=[pl.BlockSpec((COMB_TM, D), lambda i: (jnp.minimum(done(i), n_ctx - 1), 0)),
                   pl.BlockSpec((COMB_TM, D), lambda i: (jnp.maximum(done(i) - n_ctx, 0), 0))],
        out_shape=[jax.ShapeDtypeStruct((T_CTX, D), F32), jax.ShapeDtypeStruct((T_LAT, D), F32)],
        scratch_shapes=[pltpu.VMEM((2, 2, COMB_TM * NCH, 128), F32), pltpu.VMEM((COMB_TM, D), F32),
                        pltpu.SemaphoreType.DMA((2,))],
        compiler_params=_cp(("arbitrary",)),
        name="combine",
    )(pos, wts_t, x1, modseg, final_g.reshape(1, D), ys)


def _rope_tables():
    n = DEC_SEQ
    pos_row = np.repeat(np.arange(n // GRID_W, dtype=np.float32), GRID_W)
    pos_col = np.tile(np.arange(GRID_W, dtype=np.float32), n // GRID_W)
    half = HD // 2
    inv_freq = jnp.asarray(ROPE_THETA, F32) ** (-jnp.arange(0, half, 2, dtype=F32) / half)
    ang = jnp.concatenate([jnp.asarray(pos_row)[:, None] * inv_freq,
                           jnp.asarray(pos_col)[:, None] * inv_freq], axis=-1)
    cos, sin = jnp.cos(ang), jnp.sin(ang)
    cos128 = jnp.tile(cos, (1, 4))
    sin128 = jnp.tile(jnp.concatenate([-sin, sin], axis=-1), (1, 2))
    return cos128, sin128


def _head_mean_matrix():
    idx = np.arange(2 * HD)
    same = (idx[:, None] // HD) == (idx[None, :] // HD)
    return jnp.asarray(same.astype(np.float32) / HD, BF16)


_SEG_ROWS = np.array([0] * (T_CTX // SEG) + [1 + b for b in range(DEC_BATCH) for _ in range(DEC_SEQ // SEG)])


def kernel(x_prompt, x_sample, cache_k, cache_v, state_rec, c, c_ctx, w_mod, b_mod, norm1_g, norm2_g, w_in, conv_w, conv_b, rg_wa, rg_ba, rg_wx, rg_bx, rg_lambda, q_norm_g, k_norm_g, w_rec_out, w_att_out, w_out, w_router, router_bias, w_gate_e, w_up_e, w_down_e, final_g):
    xa, xb = x_prompt.reshape(T_CTX, D), x_sample.reshape(T_LAT, D)

    cvecs = jnp.concatenate([c_ctx[None, :], c, jnp.zeros((3, D), F32)], axis=0)
    mods = _mods(cvecs, w_mod, b_mod).reshape(DEPTH, 8, 6, D)
    modseg = jnp.pad(mods[:, _SEG_ROWS], ((0, 0), (0, 0), (0, 2), (0, 0)))

    cos128, sin128 = _rope_tables()
    bd = _head_mean_matrix()
    qg128 = jnp.tile(q_norm_g, (1, 2)).reshape(DEPTH, 1, 2 * HD)
    kg128 = jnp.tile(k_norm_g, (1, 2)).reshape(DEPTH, 1, 2 * HD)
    wg = jnp.concatenate([rg_wa[:, 0], rg_wx[:, 0], rg_wa[:, 1], rg_wx[:, 1]], axis=-1)
    pvec = jnp.stack([rg_ba[:, 0], rg_bx[:, 0], rg_ba[:, 1], rg_bx[:, 1],
                      rg_lambda[:, 0], rg_lambda[:, 1], conv_b, jnp.zeros_like(conv_b)], axis=1)
    wrt = w_router.T
    rbias = router_bias.reshape(N_EXP, 1)
    zrow = jnp.zeros((MOE_TM * NCH, 128), F32)

    caches, new_s = (), []
    for l in range(DEPTH):
        proj = _inproj(xa, xb, modseg, norm1_g, w_in, l)
        h0 = jnp.concatenate([jnp.zeros((T_CTX // UNIT, 2, D), F32), state_rec[:, l]], axis=0)
        yrec, stf, stb = _rec(proj, conv_w, pvec, wg, h0, l)
        qc, kc, vc = _qkv(proj, qg128, kg128, cos128, sin128, bd, l, latent=False, prev_caches=caches)
        caches = (kc, vc)
        ql, kl, vl = _qkv(proj, qg128, kg128, cos128, sin128, bd, l, latent=True)
        o_ctx = _attn_ctx(qc, kc, vc, l)
        o_lat = _attn_lat(ql, cache_k, cache_v, kl, vl, l)
        x1, h2, idx, wts = _merge(yrec, o_ctx, o_lat, proj, xa, xb, modseg, norm2_g,
                                  w_rec_out, w_att_out, w_out, wrt, rbias, l)
        pos, meta = _route_pos(idx)
        xs = _dispatch(meta, pos, h2, zrow)
        ys = _experts(meta, xs, w_gate_e, w_up_e, w_down_e, l)
        xa, xb = _combine(pos, wts.T, x1, modseg, final_g, ys, l, final=(l == DEPTH - 1))
        n_cu = T_CTX // UNIT
        spu = UNIT // SEQ
        hf_last = stf[:n_cu].reshape(n_cu, spu, 2, D)[:, :, 1].reshape(BATCH, D)
        hb_first = stb[:n_cu].reshape(n_cu, spu, 2, D)[:, :, 0].reshape(BATCH, D)
        new_s.append(jnp.stack([hf_last, hb_first], axis=1))

    y_prompt = xa.reshape(BATCH, SEQ, D)
    y_sample = xb.reshape(DEC_BATCH, DEC_SEQ, D)
    return (y_prompt, y_sample, caches[0], caches[1], jnp.stack(new_s, axis=1))
```

```python
import functools

import numpy as np
import jax
import jax.numpy as jnp
from jax import lax
from jax.experimental import pallas as pl
from jax.experimental.pallas import tpu as pltpu

F32 = jnp.float32
BF16 = jnp.bfloat16

D = 1024
BATCH = 16
SEQ = 256
DEPTH = 2
DEC_BATCH = 4
DEC_SEQ = 1024
PAST = 256
GRID_W = 64
N_HEADS = 16
N_KV = 4
HD = 64
N_RG_BLK = 8
RG_BLK = 128
RG_C = 8.0
N_EXP = 16
D_EXP = 512
ROPE_THETA = 10000.0
EPS = 1e-6
P_IN = 5632
TINY = float(np.finfo(np.float32).tiny)
NEG_LOG2E = -float(np.log2(np.e))

T_CTX = BATCH * SEQ
T_LAT = DEC_BATCH * DEC_SEQ
T = T_CTX + T_LAT
SEG = 256
N_SEG = T // SEG
UNIT = 1024
N_UNIT = T // UNIT
CHUNK = UNIT // 8
CSTRIDE = CHUNK + 8

VMEM_LIMIT = 56 * 1024 * 1024


def _cp(sem):
    return pltpu.CompilerParams(dimension_semantics=sem, vmem_limit_bytes=VMEM_LIMIT)


def _split(x):
    hi = x.astype(BF16)
    lo = (x - hi.astype(F32)).astype(BF16)
    return hi, lo


def _sigmoid(x):
    return 0.5 * jnp.tanh(0.5 * x) + 0.5


NCH = D // 128


def _store_row_tiles(ref, x):
    n = x.shape[0]
    for c in range(NCH):
        ref[pl.ds(c, n, stride=NCH), :] = x[:, c * 128:(c + 1) * 128]


def _load_row_tiles(ref, n):
    return jnp.concatenate([ref[pl.ds(c, n, stride=NCH), :] for c in range(NCH)], axis=-1)


def _dot(a, b):
    return jnp.dot(a, b, preferred_element_type=F32)


def _dot_nt(a, b):
    return lax.dot_general(a, b, (((1,), (1,)), ((), ())), preferred_element_type=F32)


def _mods_kernel(c_ref, w_ref, b_ref, o_ref):
    c = c_ref[...]
    s = c * jax.nn.sigmoid(c)
    s_hi, s_lo = _split(s)
    w_hi, w_lo = _split(w_ref[...])
    o_ref[...] = _dot(s_hi, w_hi) + _dot(s_hi, w_lo) + _dot(s_lo, w_hi) + b_ref[...]


def _mods(cvecs, w_mod, b_mod):
    tn = 1536
    return pl.pallas_call(
        _mods_kernel,
        grid=(DEPTH, 6 * D // tn),
        in_specs=[
            pl.BlockSpec((8, D), lambda l, j: (0, 0)),
            pl.BlockSpec((None, D, tn), lambda l, j: (l, 0, j)),
            pl.BlockSpec((None, 1, tn), lambda l, j: (l, 0, j)),
        ],
        out_specs=pl.BlockSpec((None, 8, tn), lambda l, j: (l, 0, j)),
        out_shape=jax.ShapeDtypeStruct((DEPTH, 8, 6 * D), F32),
        compiler_params=_cp(("arbitrary", "arbitrary")),
        name="mods",
    )(cvecs, w_mod, b_mod.reshape(DEPTH, 1, 6 * D))


def _norm_mod(x, g, shift, scale):
    ms = jnp.mean(x * x, axis=-1, keepdims=True)
    return x * lax.rsqrt(ms + EPS) * g * (1.0 + scale) + shift


def _two_part_specs(tm, n_ctx):
    return [pl.BlockSpec((tm, D), lambda i, *_: (jnp.minimum(i, n_ctx - 1), 0)),
            pl.BlockSpec((tm, D), lambda i, *_: (jnp.maximum(i - n_ctx, 0), 0))]


def _inproj_kernel(xa_ref, xb_ref, mod_ref, g_ref, w_ref, o_ref, h_ref, *, tm):
    def prologue(x_ref):
        def seg(s, carry):
            r0 = pl.multiple_of(s * SEG, SEG)
            m = mod_ref[s]
            h = _norm_mod(x_ref[pl.ds(r0, SEG), :], g_ref[...], m[0:1, :], m[1:2, :])
            h_ref[pl.ds(r0, SEG), :] = h.astype(BF16)
            return carry
        lax.fori_loop(0, tm // SEG, seg, 0)

    first = pl.program_id(1) == 0
    is_ctx = pl.program_id(0) < T_CTX // tm

    @pl.when(first & is_ctx)
    def _():
        prologue(xa_ref)

    @pl.when(first & jnp.logical_not(is_ctx))
    def _():
        prologue(xb_ref)

    o_ref[...] = _dot(h_ref[...], w_ref[...].astype(BF16)).astype(BF16)


def _inproj(xa, xb, modseg, norm_g, w_in, l):
    tm, tn = 2048, 512
    return pl.pallas_call(
        functools.partial(_inproj_kernel, tm=tm),
        grid=(T // tm, P_IN // tn),
        in_specs=_two_part_specs(tm, T_CTX // tm) + [
            pl.BlockSpec((None, tm // SEG, 8, D), lambda i, j: (l, i, 0, 0)),
            pl.BlockSpec((None, 1, D), lambda i, j: (l, 0, 0)),
            pl.BlockSpec((None, D, tn), lambda i, j: (l, 0, j)),
        ],
        out_specs=pl.BlockSpec((tm, tn), lambda i, j: (i, j)),
        out_shape=jax.ShapeDtypeStruct((T, P_IN), BF16),
        scratch_shapes=[pltpu.VMEM((tm, D), BF16)],
        compiler_params=_cp(("arbitrary", "arbitrary")),
        name="inproj",
    )(xa, xb, modseg, norm_g.reshape(DEPTH, 1, D), w_in)


REC_CW = 512
PAD_F = 16
PAD_B = 8
GATE_ROWS = 256


def _rec_kernel(xr_ref, gate_ref, cw_ref, pv_ref, wg_ref, h0_ref,
                y_ref, stf_ref, stb_ref,
                xs_ref, af_ref, bf_ref, ab_ref, bb_ref, nat_ref, wgh_ref):
    u = pl.program_id(0)
    is_ctx = u < (T_CTX // UNIT)
    cps = jnp.where(is_ctx, SEQ // CHUNK, DEC_SEQ // CHUNK)
    nblk = REC_CW // RG_BLK

    def lanes(n):
        return slice(n * RG_BLK, (n + 1) * RG_BLK)

    def tile(r):
        return slice(8 * r, 8 * r + 8)

    for c in range(8):
        for n in range(nblk):
            nat_ref[n, c * CSTRIDE:c * CSTRIDE + CHUNK, :] = xr_ref[c * CHUNK:(c + 1) * CHUNK, lanes(n)].astype(F32)
    for r in range(CHUNK):
        for n in range(nblk):
            xs_ref[PAD_F + 8 * r:PAD_F + 8 * r + 8, lanes(n)] = nat_ref[n, pl.ds(r, 8, stride=CSTRIDE), :]
    chunk_id = lax.broadcasted_iota(jnp.int32, (8, 1), 0)
    seq_start = jnp.bitwise_and(chunk_id, cps - 1) == 0
    seq_end = jnp.bitwise_and(chunk_id, cps - 1) == cps - 1
    for j, r in ((0, CHUNK - 2), (1, CHUNK - 1)):
        prev_chunk = pltpu.roll(xs_ref[PAD_F + 8 * r:PAD_F + 8 * r + 8, :], 1, 0)
        xs_ref[tile(j), :] = jnp.where(seq_start, 0.0, prev_chunk)
    next_chunk = pltpu.roll(xs_ref[PAD_F:PAD_F + 8, :], 7, 0)
    xs_ref[PAD_F + UNIT:PAD_F + UNIT + PAD_B, :] = jnp.where(seq_end, 0.0, next_chunk)

    pv = pv_ref[...]
    cwts = cw_ref[...]
    conv_b = pv[6:7, :]

    def softplus_neg(lam):
        z = -lam
        return jnp.maximum(z, 0.0) + jnp.log1p(jnp.exp(-jnp.abs(z)))

    c4s = tuple((0.5 * RG_C) * softplus_neg(pv[4 + d:5 + d, :]) for d in range(2))
    pv_h = 0.5 * pv
    for n in range(nblk):
        wgh_ref[n] = (0.5 * wg_ref[n]).astype(BF16)
    a_refs = (af_ref, ab_ref)
    b_refs = (bf_ref, bb_ref)

    def gates(g, carry):
        base = pl.multiple_of(g * GATE_ROWS, GATE_ROWS)

        def tap(d):
            return xs_ref[pl.ds(pl.multiple_of(base + PAD_F + 8 * d, 8), GATE_ROWS), :]

        xc = conv_b + tap(-2) * cwts[0:1, :]
        xc = xc + tap(-1) * cwts[1:2, :]
        xc = xc + tap(0) * cwts[2:3, :]
        xc = xc + tap(1) * cwts[3:4, :]
        for n in range(nblk):
            ls = lanes(n)
            xn = xc[:, ls]
            hx = 0.5 * xn
            pre_h = _dot(xn.astype(BF16), wgh_ref[n])
            for d in range(2):
                th_r = jnp.tanh(pre_h[:, (2 * d) * RG_BLK:(2 * d + 1) * RG_BLK] + pv_h[2 * d:2 * d + 1, ls])
                th_i = jnp.tanh(pre_h[:, (2 * d + 1) * RG_BLK:(2 * d + 2) * RG_BLK] + pv_h[2 * d + 1:2 * d + 2, ls])
                c4 = c4s[d][:, ls]
                nla = c4 * th_r + c4
                a = jnp.exp2(nla * NEG_LOG2E)
                s = jnp.tanh(nla) * (a * a + 1.0)
                inp = (s * lax.rsqrt(jnp.maximum(s, TINY))) * (hx * th_i + hx)
                a_refs[d][pl.ds(base, GATE_ROWS), ls] = a
                b_refs[d][pl.ds(base, GATE_ROWS), ls] = inp
        return carry

    lax.fori_loop(0, UNIT // GATE_ROWS, gates, 0)

    hf = hb = jnp.zeros((8, REC_CW), F32)
    pf = pb = jnp.ones((8, REC_CW), F32)
    for r in range(CHUNK):
        rf, rb = tile(r), tile(CHUNK - 1 - r)
        a = af_ref[rf, :]
        hf = a * hf + bf_ref[rf, :]
        pf = a * pf
        bf_ref[rf, :] = hf
        af_ref[rf, :] = pf
        a = ab_ref[rb, :]
        hb = a * hb + bb_ref[rb, :]
        pb = a * pb
        bb_ref[rb, :] = hb
        ab_ref[rb, :] = pb

    h0f = h0_ref[0:1, :]
    h0b = h0_ref[1:2, :]
    cf = [h0f]
    for c in range(1, 8):
        chain = hf[c - 1:c, :] + pf[c - 1:c, :] * cf[c - 1]
        cf.append(jnp.where(jnp.bitwise_and(c, cps - 1) == 0, h0f, chain))
    cb = [None] * 8
    cb[7] = h0b
    for c in range(6, -1, -1):
        chain = hb[c + 1:c + 2, :] + pb[c + 1:c + 2, :] * cb[c + 1]
        cb[c] = jnp.where(jnp.bitwise_and(c, cps - 1) == cps - 1, h0b, chain)
    carry_f = jnp.concatenate(cf, axis=0)
    carry_b = jnp.concatenate(cb, axis=0)
    stf_ref[...] = hf + pf * carry_f
    stb_ref[...] = hb + pb * carry_b

    for r in range(CHUNK):
        h = (bf_ref[tile(r), :] + af_ref[tile(r), :] * carry_f) + (bb_ref[tile(r), :] + ab_ref[tile(r), :] * carry_b)
        for n in range(nblk):
            nat_ref[n, pl.ds(r, 8, stride=CSTRIDE), :] = h[:, lanes(n)]

    for c in range(8):
        rows = slice(c * CHUNK, (c + 1) * CHUNK)
        for n in range(nblk):
            g = gate_ref[rows, lanes(n)].astype(F32)
            h = nat_ref[n, c * CSTRIDE:c * CSTRIDE + CHUNK, :]
            y_ref[rows, lanes(n)] = (h * jax.nn.gelu(g, approximate=True)).astype(BF16)


def _rec(proj, conv_w, pvec, wg, h0, l):
    ncb = D // REC_CW
    return pl.pallas_call(
        _rec_kernel,
        grid=(N_UNIT, ncb),
        in_specs=[
            pl.BlockSpec((UNIT, REC_CW), lambda u, c: (u, c)),
            pl.BlockSpec((UNIT, REC_CW), lambda u, c: (u, ncb + c)),
            pl.BlockSpec((None, 4, REC_CW), lambda u, c: (l, 0, c)),
            pl.BlockSpec((None, 8, REC_CW), lambda u, c: (l, 0, c)),
            pl.BlockSpec((None, REC_CW // RG_BLK, RG_BLK, 4 * RG_BLK), lambda u, c: (l, c, 0, 0)),
            pl.BlockSpec((None, 2, REC_CW), lambda u, c: (u, 0, c)),
        ],
        out_specs=[
            pl.BlockSpec((UNIT, REC_CW), lambda u, c: (u, c)),
            pl.BlockSpec((None, 8, REC_CW), lambda u, c: (u, 0, c)),
            pl.BlockSpec((None, 8, REC_CW), lambda u, c: (u, 0, c)),
        ],
        out_shape=[
            jax.ShapeDtypeStruct((T, D), BF16),
            jax.ShapeDtypeStruct((N_UNIT, 8, D), F32),
            jax.ShapeDtypeStruct((N_UNIT, 8, D), F32),
        ],
        scratch_shapes=[pltpu.VMEM((PAD_F + UNIT + PAD_B, REC_CW), F32)]
        + [pltpu.VMEM((UNIT, REC_CW), F32)] * 4
        + [pltpu.VMEM((REC_CW // RG_BLK, 8 * CSTRIDE, RG_BLK), F32),
           pltpu.VMEM((REC_CW // RG_BLK, RG_BLK, 4 * RG_BLK), BF16)],
        compiler_params=_cp(("arbitrary", "arbitrary")),
        name="rec",
    )(proj, proj, conv_w, pvec, wg, h0)


def _head_norm(x, g128, bd):
    hi, lo = _split(x * x)
    ms = _dot(hi, bd) + _dot(lo, bd)
    return x * lax.rsqrt(ms + EPS) * g128


def _rope(x, cos, sin_signed):
    lane = lax.broadcasted_iota(jnp.int32, x.shape, 1)
    first_half = jnp.bitwise_and(lane, HD - 1) < HD // 2
    partner = jnp.where(first_half, pltpu.roll(x, 2 * HD - HD // 2, 1), pltpu.roll(x, HD // 2, 1))
    return x * cos + partner * sin_signed


def _qkv_kernel(q_ref, k_ref, v_ref, qg_ref, kg_ref, cos_ref, sin_ref, bd_ref, *rest, rope, slab):
    qo_ref, ko_ref, vo_ref = rest[-3:]
    if slab is not None:
        if slab > 0:
            pk_ref, pv_ref = rest[:2]
            for p in range(slab):
                ko_ref[p] = pk_ref[p]
                vo_ref[p] = pv_ref[p]
        ko_ref, vo_ref = ko_ref.at[slab], vo_ref.at[slab]
    bd = bd_ref[...]
    scale = HD ** -0.5 * float(np.log2(np.e))
    for j in range(N_HEADS // 2):
        x = _head_norm(q_ref[:, 2 * HD * j:2 * HD * (j + 1)].astype(F32), qg_ref[...], bd)
        if rope:
            x = _rope(x, cos_ref[...], sin_ref[...])
        x = x * scale
        qo_ref[2 * j] = x[:, :HD].astype(qo_ref.dtype)
        qo_ref[2 * j + 1] = x[:, HD:].astype(qo_ref.dtype)
    for j in range(N_KV // 2):
        x = _head_norm(k_ref[:, 2 * HD * j:2 * HD * (j + 1)].astype(F32), kg_ref[...], bd)
        if rope:
            x = _rope(x, cos_ref[...], sin_ref[...])
        ko_ref[2 * j] = x[:, :HD].astype(ko_ref.dtype)
        ko_ref[2 * j + 1] = x[:, HD:].astype(ko_ref.dtype)
        v = v_ref[:, 2 * HD * j:2 * HD * (j + 1)].astype(F32)
        vo_ref[2 * j] = v[:, :HD].astype(vo_ref.dtype)
        vo_ref[2 * j + 1] = v[:, HD:].astype(vo_ref.dtype)


def _qkv(proj, qg128, kg128, cos128, sin128, bd, l, latent, prev_caches=()):
    tm = SEG
    n = T_LAT // tm if latent else T_CTX // tm
    roff = T_CTX // tm if latent else 0
    per_seq = DEC_SEQ // tm
    prev_specs = []
    if latent:
        kv_shape = (DEC_BATCH, N_KV, DEC_SEQ, HD)
        kv_spec = pl.BlockSpec((None, N_KV, tm, HD), lambda i: (i // per_seq, 0, i % per_seq, 0))
        kv_dtype = BF16
        tab_map = lambda i: (i % per_seq, 0)
    else:
        kv_shape = (BATCH, l + 1, N_KV, SEQ, HD)
        kv_spec = pl.BlockSpec((None, l + 1, N_KV, tm, HD), lambda i: (i, 0, 0, 0, 0))
        kv_dtype = F32
        tab_map = lambda i: (0, 0)
        if l > 0:
            prev_specs = [pl.BlockSpec((None, l, N_KV, tm, HD), lambda i: (i, 0, 0, 0, 0))] * 2
    return pl.pallas_call(
        functools.partial(_qkv_kernel, rope=latent, slab=None if latent else l),
        grid=(n,),
        in_specs=[
            pl.BlockSpec((tm, D), lambda i: (roff + i, 2)),
            pl.BlockSpec((tm, N_KV * HD), lambda i: (roff + i, 3 * D // (N_KV * HD))),
            pl.BlockSpec((tm, N_KV * HD), lambda i: (roff + i, 3 * D // (N_KV * HD) + 1)),
            pl.BlockSpec((None, 1, 2 * HD), lambda i: (l, 0, 0)),
            pl.BlockSpec((None, 1, 2 * HD), lambda i: (l, 0, 0)),
            pl.BlockSpec((tm, 2 * HD), tab_map),
            pl.BlockSpec((tm, 2 * HD), tab_map),
            pl.BlockSpec((2 * HD, 2 * HD), lambda i: (0, 0)),
        ] + prev_specs,
        out_specs=[
            pl.BlockSpec((N_HEADS, tm, HD), lambda i: (0, i, 0)),
            kv_spec,
            kv_spec,
        ],
        out_shape=[
            jax.ShapeDtypeStruct((N_HEADS, n * tm, HD), BF16),
            jax.ShapeDtypeStruct(kv_shape, kv_dtype),
            jax.ShapeDtypeStruct(kv_shape, kv_dtype),
        ],
        compiler_params=_cp(("arbitrary",)),
        name="qkv_lat" if latent else "qkv_ctx",
    )(proj, proj, proj, qg128, kg128, cos128, sin128, bd, *prev_caches)


def _with_ones(v):
    return jnp.concatenate([v, jnp.ones_like(v)], axis=-1)


def _softmax_pv(q, k, v_ext):
    s = _dot_nt(q, k)
    m = jnp.max(s, axis=-1, keepdims=True)
    p = jnp.exp2(s - m).astype(BF16)
    r = _dot(p, v_ext)
    return r[:, :HD] / r[:, HD:HD + 1]


def _attend_heads(q_ref, k, v_ext):
    return jnp.concatenate([_softmax_pv(q_ref[h], k, v_ext) for h in range(N_HEADS // N_KV)], axis=-1)


def _attn_ctx_kernel(q_ref, k_ref, v_ref, o_ref):
    g = N_HEADS // N_KV
    for kv in range(CTX_KV_PER_STEP):
        q = q_ref[kv * g:(kv + 1) * g].reshape(g * SEQ, HD)
        o = _softmax_pv(q, k_ref[kv].astype(BF16), _with_ones(v_ref[kv].astype(BF16)))
        o = jnp.concatenate([o[h * SEQ:(h + 1) * SEQ] for h in range(g)], axis=-1)
        o_ref[:, kv * g * HD:(kv + 1) * g * HD] = o.astype(BF16)


CTX_KV_PER_STEP = 1


def _attn_ctx(qh, kc, vc, l):
    g = N_HEADS // N_KV * CTX_KV_PER_STEP
    return pl.pallas_call(
        _attn_ctx_kernel,
        grid=(BATCH, N_KV // CTX_KV_PER_STEP),
        in_specs=[
            pl.BlockSpec((g, SEQ, HD), lambda b, h: (h, b, 0)),
            pl.BlockSpec((None, None, CTX_KV_PER_STEP, SEQ, HD), lambda b, h: (b, l, h, 0, 0)),
            pl.BlockSpec((None, None, CTX_KV_PER_STEP, SEQ, HD), lambda b, h: (b, l, h, 0, 0)),
        ],
        out_specs=pl.BlockSpec((SEQ, g * HD), lambda b, h: (b, h)),
        out_shape=jax.ShapeDtypeStruct((T_CTX, D), BF16),
        compiler_params=_cp(("arbitrary", "arbitrary")),
        name="attn_ctx",
    )(qh, kc, vc)


ATT_TQ = 1024


def _attn_lat_kernel(q_ref, pk_ref, pv_ref, k_ref, v_ref, o_ref, k_s, v_s):
    k_s[0:PAST, :] = pk_ref[...].astype(BF16)
    k_s[PAST:, :] = k_ref[...]
    v_s[0:PAST, :] = _with_ones(pv_ref[...].astype(BF16))
    v_s[PAST:, :] = _with_ones(v_ref[...])

    def q_tile(qi, carry):
        rows = pl.ds(pl.multiple_of(qi * ATT_TQ, ATT_TQ), ATT_TQ)
        o = jnp.concatenate([_softmax_pv(q_ref[h, rows, :], k_s[...], v_s[...])
                             for h in range(N_HEADS // N_KV)], axis=-1)
        o_ref[rows, :] = o.astype(BF16)
        return carry

    lax.fori_loop(0, DEC_SEQ // ATT_TQ, q_tile, 0)


def _attn_lat(qh, cache_k, cache_v, kr, vr, l):
    g = N_HEADS // N_KV
    return pl.pallas_call(
        _attn_lat_kernel,
        grid=(DEC_BATCH, N_KV),
        in_specs=[
            pl.BlockSpec((g, DEC_SEQ, HD), lambda b, h: (h, b, 0)),
            pl.BlockSpec((None, None, None, PAST, HD), lambda b, h: (b, l, h, 0, 0)),
            pl.BlockSpec((None, None, None, PAST, HD), lambda b, h: (b, l, h, 0, 0)),
            pl.BlockSpec((None, None, DEC_SEQ, HD), lambda b, h: (b, h, 0, 0)),
            pl.BlockSpec((None, None, DEC_SEQ, HD), lambda b, h: (b, h, 0, 0)),
        ],
        out_specs=pl.BlockSpec((DEC_SEQ, g * HD), lambda b, h: (b, h)),
        out_shape=jax.ShapeDtypeStruct((T_LAT, D), BF16),
        scratch_shapes=[pltpu.VMEM((PAST + DEC_SEQ, HD), BF16), pltpu.VMEM((PAST + DEC_SEQ, 2 * HD), BF16)],
        compiler_params=_cp(("arbitrary", "arbitrary")),
        name="attn_lat",
    )(qh, cache_k, cache_v, kr, vr)


MERGE_TM = 512


def _route(lt, bias):
    rows = [lt[e:e + 1, :] for e in range(N_EXP)]
    m = rows[0]
    for e in range(1, N_EXP):
        m = jnp.maximum(m, rows[e])
    ex = [jnp.exp(r - m) for r in rows]
    z = ex[0]
    for e in range(1, N_EXP):
        z = z + ex[e]
    probs = [x / z for x in ex]
    sel = [probs[e] + bias[e:e + 1, :] for e in range(N_EXP)]

    def top2_sum(v):
        a, b = jnp.maximum(v[0], v[1]), jnp.minimum(v[0], v[1])
        c, d = jnp.maximum(v[2], v[3]), jnp.minimum(v[2], v[3])
        return jnp.maximum(a, c) + jnp.maximum(jnp.minimum(a, c), jnp.maximum(b, d))

    scores = [top2_sum(sel[4 * g:4 * g + 4]) for g in range(4)]
    best = jnp.zeros_like(scores[0], dtype=jnp.int32)
    best_s = scores[0]
    for g in range(1, 4):
        take = scores[g] > best_s
        best = jnp.where(take, g, best)
        best_s = jnp.where(take, scores[g], best_s)
    cs, cp = [], []
    for j in range(4):
        s_j, p_j = sel[j], probs[j]
        for g in range(1, 4):
            s_j = jnp.where(best == g, sel[4 * g + j], s_j)
            p_j = jnp.where(best == g, probs[4 * g + j], p_j)
        cs.append(s_j)
        cp.append(p_j)
    neg = jnp.full_like(cs[0], -jnp.inf)

    def argmax4(v):
        bi = jnp.zeros_like(best)
        bv = v[0]
        for j in range(1, 4):
            take = v[j] > bv
            bi = jnp.where(take, j, bi)
            bv = jnp.where(take, v[j], bv)
        return bi

    def pick(v, idx):
        out = v[0]
        for j in range(1, 4):
            out = jnp.where(idx == j, v[j], out)
        return out

    i1 = argmax4(cs)
    cs2 = [jnp.where(i1 == j, neg, cs[j]) for j in range(4)]
    i2 = argmax4(cs2)
    i2 = jnp.where((i2 == 0) & (i1 == 0), 1, i2)
    w1, w2 = pick(cp, i1), pick(cp, i2)
    den = w1 + w2
    return best * 4 + i1, best * 4 + i2, w1 / den, w2 / den


def _merge_kernel(yrec_ref, oa_ref, ob_ref, gr0_ref, gr1_ref, ga0_ref, ga1_ref, xa_ref, xb_ref, mod_ref, g2_ref,
                  wrec_ref, watt_ref, wout_ref, wrt_ref, rb_ref,
                  x1_ref, h2_ref, idx_ref, wts_ref,
                  wrec_s, watt_s, wout_s):
    @pl.when(pl.program_id(0) == 0)
    def _():
        wrec_s[...] = wrec_ref[...].astype(BF16)
        watt_s[...] = watt_ref[...].astype(BF16)
        wout_s[...] = wout_ref[...].astype(BF16)

    is_ctx = pl.program_id(0) < T_CTX // MERGE_TM
    args = (yrec_ref, gr0_ref, gr1_ref, ga0_ref, ga1_ref, mod_ref, g2_ref, wrt_ref, rb_ref,
            x1_ref, h2_ref, idx_ref, wts_ref, wrec_s, watt_s, wout_s)

    @pl.when(is_ctx)
    def _():
        _merge_body(oa_ref, xa_ref, *args)

    @pl.when(jnp.logical_not(is_ctx))
    def _():
        _merge_body(ob_ref, xb_ref, *args)


def _merge_body(oatt_ref, x_ref, yrec_ref, gr0_ref, gr1_ref, ga0_ref, ga1_ref, mod_ref, g2_ref, wrt_ref, rb_ref,
                x1_ref, h2_ref, idx_ref, wts_ref, wrec_s, watt_s, wout_s):
    half = D // 2
    b_rec = _dot(yrec_ref[...], wrec_s[...])
    b_att = _dot(oatt_ref[...], watt_s[...])
    m0 = _sigmoid(gr0_ref[...].astype(F32)) * b_rec[:, :half] + _sigmoid(ga0_ref[...].astype(F32)) * b_att[:, :half]
    m1 = _sigmoid(gr1_ref[...].astype(F32)) * b_rec[:, half:] + _sigmoid(ga1_ref[...].astype(F32)) * b_att[:, half:]
    merged = jnp.concatenate([m0, m1], axis=-1).astype(BF16)
    out = _dot(merged, wout_s[...])

    hs = []
    for s in range(MERGE_TM // SEG):
        rows = slice(s * SEG, (s + 1) * SEG)
        m = mod_ref[s]
        x1 = x_ref[rows, :] + m[2:3, :] * out[rows, :]
        x1_ref[rows, :] = x1
        h2 = _norm_mod(x1, g2_ref[...], m[3:4, :], m[4:5, :])
        hs.append(h2)
    h2 = jnp.concatenate(hs, axis=0)
    _store_row_tiles(h2_ref, h2)

    h_hi, h_lo = _split(h2)
    w_hi, w_lo = _split(wrt_ref[...])
    lt = _dot_nt(w_hi, h_hi) + _dot_nt(w_hi, h_lo) + _dot_nt(w_lo, h_hi)
    e1, e2, w1, w2 = _route(lt, rb_ref[...])
    idx_ref[...] = jnp.concatenate([e1, e2], axis=0)
    wts_ref[...] = jnp.concatenate([w1, w2], axis=0)


def _merge(yrec, o_ctx, o_lat, proj, xa, xb, modseg, norm2_g, w_rec_out, w_att_out, w_out, wrt, rbias, l):
    tm = MERGE_TM
    half = D // 2
    gcol = (3 * D + 2 * N_KV * HD) // half
    wspec = pl.BlockSpec((None, D, D), lambda i: (l, 0, 0))
    return pl.pallas_call(
        _merge_kernel,
        grid=(T // tm,),
        in_specs=[pl.BlockSpec((tm, D), lambda i: (i, 0))] + _two_part_specs(tm, T_CTX // tm) + [
            pl.BlockSpec((tm, half), lambda i: (i, gcol)),
            pl.BlockSpec((tm, half), lambda i: (i, gcol + 1)),
            pl.BlockSpec((tm, half), lambda i: (i, gcol + 2)),
            pl.BlockSpec((tm, half), lambda i: (i, gcol + 3)),
        ] + _two_part_specs(tm, T_CTX // tm) + [
            pl.BlockSpec((None, tm // SEG, 8, D), lambda i: (l, i, 0, 0)),
            pl.BlockSpec((None, 1, D), lambda i: (l, 0, 0)),
            wspec, wspec, wspec,
            pl.BlockSpec((N_EXP, D), lambda i: (0, 0)),
            pl.BlockSpec((N_EXP, 1), lambda i: (0, 0)),
        ],
        out_specs=[
            pl.BlockSpec((tm, D), lambda i: (i, 0)),
            pl.BlockSpec((tm * NCH, 128), lambda i: (i, 0)),
            pl.BlockSpec((2, tm), lambda i: (0, i)),
            pl.BlockSpec((2, tm), lambda i: (0, i)),
        ],
        out_shape=[
            jax.ShapeDtypeStruct((T, D), F32),
            jax.ShapeDtypeStruct((T * NCH, 128), F32),
            jax.ShapeDtypeStruct((2, T), jnp.int32),
            jax.ShapeDtypeStruct((2, T), F32),
        ],
        scratch_shapes=[pltpu.VMEM((D, D), BF16)] * 3,
        compiler_params=_cp(("arbitrary",)),
        name="merge",
    )(yrec, o_ctx, o_lat, proj, proj, proj, proj, xa, xb, modseg, norm2_g.reshape(DEPTH, 1, D),
      w_rec_out, w_att_out, w_out, wrt, rbias)


MOE_TM = 512
MOE_NT = 2 * T // MOE_TM + N_EXP
MOE_ROWS = MOE_NT * MOE_TM
META_TILE_E, META_CNT, META_OFF, META_END, META_NT, META_NEXT_E = 0, 1, 2, 3, 4, 5


def _pos_kernel(idx_ref, pos_ref, meta_ref):
    shift = MOE_TM.bit_length() - 1
    idx = idx_ref[...]
    eid = lax.broadcasted_iota(jnp.int32, (N_EXP, T), 0)
    m0 = eid == idx[0:1, :]
    m1 = eid == idx[1:2, :]
    member = jnp.where(m0 | m1, 1.0, 0.0)
    cnt = jnp.sum(member, axis=1, keepdims=True).astype(jnp.int32)
    ntile = jnp.right_shift(cnt + (MOE_TM - 1), shift)
    offs, acc = [], jnp.zeros((1, 1), jnp.int32)
    for e in range(N_EXP):
        offs.append(acc)
        acc = acc + ntile[e:e + 1, :]
    off_t = jnp.concatenate(offs, axis=0)
    end_t = off_t + ntile

    blk = 256
    r_i = lax.broadcasted_iota(jnp.int32, (blk, blk), 0)
    c_i = lax.broadcasted_iota(jnp.int32, (blk, blk), 1)
    upper = jnp.where(r_i <= c_i, 1.0, 0.0).astype(BF16)
    run = (off_t * MOE_TM).astype(F32)
    for j in range(T // blk):
        ls = slice(j * blk, (j + 1) * blk)
        mb = member[:, ls]
        inc = _dot(mb.astype(BF16), upper)
        dest = run + inc - mb
        pos_ref[0:1, ls] = jnp.sum(jnp.where(m0[:, ls], dest, 0.0), axis=0, keepdims=True).astype(jnp.int32)
        pos_ref[1:2, ls] = jnp.sum(jnp.where(m1[:, ls], dest, 0.0), axis=0, keepdims=True).astype(jnp.int32)
        run = run + inc[:, blk - 1:blk]

    lane = lax.broadcasted_iota(jnp.int32, (1, 128), 1)
    zero = jnp.zeros((1, 128), jnp.int32)
    tile_e, cnt_row, off_row, end_row = zero, zero, zero, zero
    for e in range(N_EXP):
        tile_e = tile_e + jnp.where(lane >= end_t[e:e + 1, :], 1, 0)
        here = lane == e
        cnt_row = jnp.where(here, cnt[e:e + 1, :], cnt_row)
        off_row = jnp.where(here, off_t[e:e + 1, :] * MOE_TM, off_row)
        end_row = jnp.where(here, end_t[e:e + 1, :] * MOE_TM, end_row)
    tile_e = jnp.minimum(tile_e, N_EXP - 1)
    nt_row = zero + acc
    next_row = zero
    nxt = jnp.full((1, 1), -1, jnp.int32)
    for e in reversed(range(N_EXP)):
        next_row = jnp.where(lane == e, nxt, next_row)
        nxt = jnp.where(cnt[e:e + 1, :] > 0, e, nxt)
    meta_ref[...] = jnp.concatenate([tile_e, cnt_row, off_row, end_row, nt_row, next_row, zero, zero], axis=0)


def _route_pos(idx):
    return pl.pallas_call(
        _pos_kernel,
        grid=(1,),
        in_specs=[pl.BlockSpec((2, T), lambda i: (0, 0))],
        out_specs=[pl.BlockSpec((2, T), lambda i: (0, 0)), pl.BlockSpec((8, 128), lambda i: (0, 0))],
        out_shape=[jax.ShapeDtypeStruct((2, T), jnp.int32), jax.ShapeDtypeStruct((8, 128), jnp.int32)],
        compiler_params=_cp(("arbitrary",)),
        name="route_pos",
    )(idx)


DISP_TM = 256


def _dispatch_kernel(meta_ref, pos_ref, h_ref, z_hbm, xs_hbm, sem):
    i = pl.program_id(0)

    def row_copy(src, src_row, dst_row):
        return pltpu.make_async_copy(src.at[pl.ds(src_row * NCH, NCH), :],
                                     xs_hbm.at[pl.ds(pl.multiple_of(dst_row * NCH, NCH), NCH), :], sem)

    for r in range(DISP_TM):
        row_copy(h_ref, r, pos_ref[0, r]).start(priority=0)
        row_copy(h_ref, r, pos_ref[1, r]).start(priority=1)

    e = jnp.minimum(i, N_EXP - 1)
    pad0 = meta_ref[META_OFF, e] + meta_ref[META_CNT, e]
    npad = jnp.where(i < N_EXP, meta_ref[META_END, e] - pad0, 0)

    def pad_copies(act):
        s = pad0
        for bit in reversed(range(MOE_TM.bit_length() - 1)):
            size = 1 << bit
            part = jnp.bitwise_and(npad, size)

            @pl.when(part != 0)
            def _():
                dst = pl.ds(pl.multiple_of(s * NCH, NCH), size * NCH)
                act(pltpu.make_async_copy(z_hbm.at[pl.ds(0, size * NCH), :], xs_hbm.at[dst, :], sem))

            s = s + part

    pad_copies(lambda c: c.start())

    for _ in range(2):
        pltpu.make_async_copy(h_ref, xs_hbm.at[pl.ds(0, DISP_TM * NCH), :], sem).wait()

    tail = meta_ref[META_NT, 0] + i
    has_tail = (i < N_EXP) & (tail < MOE_NT)

    def tail_copy():
        rows = pl.ds(pl.multiple_of(tail * (MOE_TM * NCH), MOE_TM * NCH), MOE_TM * NCH)
        return pltpu.make_async_copy(z_hbm, xs_hbm.at[rows, :], sem)

    @pl.when(has_tail)
    def _():
        tail_copy().start()

    pad_copies(lambda c: c.wait())

    @pl.when(has_tail)
    def _():
        tail_copy().wait()


def _dispatch(meta, pos, h2, zrow):
    return pl.pallas_call(
        _dispatch_kernel,
        grid_spec=pltpu.PrefetchScalarGridSpec(
            num_scalar_prefetch=1,
            grid=(T // DISP_TM,),
            in_specs=[
                pl.BlockSpec((2, DISP_TM), lambda i, meta: (0, i), memory_space=pltpu.SMEM),
                pl.BlockSpec((DISP_TM * NCH, 128), lambda i, meta: (i, 0)),
                pl.BlockSpec((MOE_TM * NCH, 128), lambda i, meta: (0, 0)),
            ],
            out_specs=pl.BlockSpec(memory_space=pl.ANY),
            scratch_shapes=[pltpu.SemaphoreType.DMA],
        ),
        out_shape=jax.ShapeDtypeStruct((MOE_ROWS * NCH, 128), F32),
        compiler_params=_cp(("arbitrary",)),
        name="dispatch",
    )(meta, pos, h2, zrow)


def _experts_kernel(meta_ref, xs_ref, wg_hbm, wu_hbm, wd_hbm, ys_ref,
                    wg_f, wu_f, wd_f, wg_s, wu_s, wd_s, sem, *, l):
    j = pl.program_id(0)
    live = j < meta_ref[META_NT, 0]
    e = meta_ref[META_TILE_E, j]
    e_prev = meta_ref[META_TILE_E, jnp.maximum(j - 1, 0)]

    def fetch(ex):
        return (pltpu.make_async_copy(wg_hbm.at[l, ex], wg_f, sem.at[0]),
                pltpu.make_async_copy(wu_hbm.at[l, ex], wu_f, sem.at[1]),
                pltpu.make_async_copy(wd_hbm.at[l, ex], wd_f, sem.at[2]))

    @pl.when(j == 0)
    def _():
        for c in fetch(e):
            c.start()

    @pl.when(live & ((j == 0) | (e != e_prev)))
    def _():
        for c, dst, src in zip(fetch(e), (wg_s, wu_s, wd_s), (wg_f, wu_f, wd_f)):
            c.wait()
            dst[...] = src[...].astype(BF16)
        nxt = meta_ref[META_NEXT_E, e]

        @pl.when(nxt >= 0)
        def _():
            for c in fetch(nxt):
                c.start()

    @pl.when(live)
    def _():
        x = _load_row_tiles(xs_ref, MOE_TM).astype(BF16)
        g = _dot(x, wg_s[...])
        u = _dot(x, wu_s[...])
        act = (g * _sigmoid(g)) * u
        _store_row_tiles(ys_ref, _dot(act.astype(BF16), wd_s[...]))

    @pl.when(jnp.logical_not(live))
    def _():
        ys_ref[...] = jnp.zeros_like(ys_ref)


def _experts(meta, xs, w_gate_e, w_up_e, w_down_e, l):
    def tile(j, meta):
        return jnp.minimum(j, meta[META_NT, 0] - 1)

    return pl.pallas_call(
        functools.partial(_experts_kernel, l=l),
        grid_spec=pltpu.PrefetchScalarGridSpec(
            num_scalar_prefetch=1,
            grid=(MOE_NT,),
            in_specs=[
                pl.BlockSpec((MOE_TM * NCH, 128), lambda j, meta: (tile(j, meta), 0)),
                pl.BlockSpec(memory_space=pl.ANY),
                pl.BlockSpec(memory_space=pl.ANY),
                pl.BlockSpec(memory_space=pl.ANY),
            ],
            out_specs=pl.BlockSpec((MOE_TM * NCH, 128), lambda j, meta: (j, 0)),
            scratch_shapes=[pltpu.VMEM((D, D_EXP), F32), pltpu.VMEM((D, D_EXP), F32), pltpu.VMEM((D_EXP, D), F32),
                            pltpu.VMEM((D, D_EXP), BF16), pltpu.VMEM((D, D_EXP), BF16), pltpu.VMEM((D_EXP, D), BF16),
                            pltpu.SemaphoreType.DMA((3,))],
        ),
        out_shape=jax.ShapeDtypeStruct((MOE_ROWS * NCH, 128), F32),
        compiler_params=_cp(("arbitrary",)),
        name="experts",
    )(meta, xs, w_gate_e, w_up_e, w_down_e)


COMB_TM = SEG


def _combine_kernel(pos_ref, w_ref, x1_ref, mod_ref, fg_ref, ys_hbm, oa_ref, ob_ref, buf, y_s, sem, *, final):
    i = pl.program_id(0)
    n = pl.num_programs(0) - 1
    n_ctx = T_CTX // COMB_TM

    for s in range(2):
        @pl.when((i < n) & (lax.rem(i, 2) == s))
        def _():
            for r in range(COMB_TM):
                for k in range(2):
                    src = pl.ds(pl.multiple_of(pos_ref[k, r] * NCH, NCH), NCH)
                    pltpu.make_async_copy(ys_hbm.at[src, :], buf.at[s, k, pl.ds(r * NCH, NCH), :],
                                          sem.at[s]).start(priority=k)

    for slot in range(2):
        @pl.when((i > 0) & (lax.rem(i - 1, 2) == slot))
        def _():
            for k in range(2):
                pltpu.make_async_copy(ys_hbm.at[pl.ds(0, COMB_TM * NCH), :], buf.at[slot, k], sem.at[slot]).wait()
            w = w_ref[...]
            y = (w[:, 0:1] * _load_row_tiles(buf.at[slot, 0], COMB_TM)
                 + w[:, 1:2] * _load_row_tiles(buf.at[slot, 1], COMB_TM))
            y_s[...] = y

    @pl.when(i > 0)
    def _():
        x = x1_ref[...] + mod_ref[5:6, :] * y_s[...]
        if final:
            ms = jnp.mean(x * x, axis=-1, keepdims=True)
            x = x * lax.rsqrt(ms + EPS) * fg_ref[...]

        @pl.when(i - 1 < n_ctx)
        def _():
            oa_ref[...] = x

        @pl.when(i - 1 >= n_ctx)
        def _():
            ob_ref[...] = x


def _combine(pos, wts_t, x1, modseg, final_g, ys, l, final):
    n = T // COMB_TM
    n_ctx = T_CTX // COMB_TM

    def done(i):
        return jnp.maximum(i - 1, 0)

    return pl.pallas_call(
        functools.partial(_combine_kernel, final=final),
        grid=(n + 1,),
        in_specs=[
            pl.BlockSpec((2, COMB_TM), lambda i: (0, jnp.minimum(i, n - 1)), memory_space=pltpu.SMEM),
            pl.BlockSpec((COMB_TM, 2), lambda i: (done(i), 0)),
            pl.BlockSpec((COMB_TM, D), lambda i: (done(i), 0)),
            pl.BlockSpec((None, None, 8, D), lambda i: (l, done(i), 0, 0)),
            pl.BlockSpec((1, D), lambda i: (0, 0)),
            pl.BlockSpec(memory_space=pl.ANY),
        ],
        out_specs=[pl.BlockSpec((COMB_TM, D), lambda i: (jnp.minimum(done(i), n_ctx - 1), 0)),
                   pl.BlockSpec((COMB_TM, D), lambda i: (jnp.maximum(done(i) - n_ctx, 0), 0))],
        out_shape=[jax.ShapeDtypeStruct((T_CTX, D), F32), jax.ShapeDtypeStruct((T_LAT, D), F32)],
        scratch_shapes=[pltpu.VMEM((2, 2, COMB_TM * NCH, 128), F32), pltpu.VMEM((COMB_TM, D), F32),
                        pltpu.SemaphoreType.DMA((2,))],
        compiler_params=_cp(("arbitrary",)),
        name="combine",
    )(pos, wts_t, x1, modseg, final_g.reshape(1, D), ys)


def _rope_tables():
    n = DEC_SEQ
    pos_row = np.repeat(np.arange(n // GRID_W, dtype=np.float32), GRID_W)
    pos_col = np.tile(np.arange(GRID_W, dtype=np.float32), n // GRID_W)
    half = HD // 2
    inv_freq = jnp.asarray(ROPE_THETA, F32) ** (-jnp.arange(0, half, 2, dtype=F32) / half)
    ang = jnp.concatenate([jnp.asarray(pos_row)[:, None] * inv_freq,
                           jnp.asarray(pos_col)[:, None] * inv_freq], axis=-1)
    cos, sin = jnp.cos(ang), jnp.sin(ang)
    cos128 = jnp.tile(cos, (1, 4))
    sin128 = jnp.tile(jnp.concatenate([-sin, sin], axis=-1), (1, 2))
    return cos128, sin128


def _head_mean_matrix():
    idx = np.arange(2 * HD)
    same = (idx[:, None] // HD) == (idx[None, :] // HD)
    return jnp.asarray(same.astype(np.float32) / HD, BF16)


_SEG_ROWS = np.array([0] * (T_CTX // SEG) + [1 + b for b in range(DEC_BATCH) for _ in range(DEC_SEQ // SEG)])


def kernel(x_prompt, x_sample, cache_k, cache_v, state_rec, c, c_ctx, w_mod, b_mod, norm1_g, norm2_g, w_in, conv_w, conv_b, rg_wa, rg_ba, rg_wx, rg_bx, rg_lambda, q_norm_g, k_norm_g, w_rec_out, w_att_out, w_out, w_router, router_bias, w_gate_e, w_up_e, w_down_e, final_g):
    xa, xb = x_prompt.reshape(T_CTX, D), x_sample.reshape(T_LAT, D)

    cvecs = jnp.concatenate([c_ctx[None, :], c, jnp.zeros((3, D), F32)], axis=0)
    mods = _mods(cvecs, w_mod, b_mod).reshape(DEPTH, 8, 6, D)
    modseg = jnp.pad(mods[:, _SEG_ROWS], ((0, 0), (0, 0), (0, 2), (0, 0)))

    cos128, sin128 = _rope_tables()
    bd = _head_mean_matrix()
    qg128 = jnp.tile(q_norm_g, (1, 2)).reshape(DEPTH, 1, 2 * HD)
    kg128 = jnp.tile(k_norm_g, (1, 2)).reshape(DEPTH, 1, 2 * HD)
    wg = jnp.concatenate([rg_wa[:, 0], rg_wx[:, 0], rg_wa[:, 1], rg_wx[:, 1]], axis=-1)
    pvec = jnp.stack([rg_ba[:, 0], rg_bx[:, 0], rg_ba[:, 1], rg_bx[:, 1],
                      rg_lambda[:, 0], rg_lambda[:, 1], conv_b, jnp.zeros_like(conv_b)], axis=1)
    wrt = w_router.T
    rbias = router_bias.reshape(N_EXP, 1)
    zrow = jnp.zeros((MOE_TM * NCH, 128), F32)

    caches, new_s = (), []
    for l in range(DEPTH):
        proj = _inproj(xa, xb, modseg, norm1_g, w_in, l)
        h0 = jnp.concatenate([jnp.zeros((T_CTX // UNIT, 2, D), F32), state_rec[:, l]], axis=0)
        yrec, stf, stb = _rec(proj, conv_w, pvec, wg, h0, l)
        qc, kc, vc = _qkv(proj, qg128, kg128, cos128, sin128, bd, l, latent=False, prev_caches=caches)
        caches = (kc, vc)
        ql, kl, vl = _qkv(proj, qg128, kg128, cos128, sin128, bd, l, latent=True)
        o_ctx = _attn_ctx(qc, kc, vc, l)
        o_lat = _attn_lat(ql, cache_k, cache_v, kl, vl, l)
        x1, h2, idx, wts = _merge(yrec, o_ctx, o_lat, proj, xa, xb, modseg, norm2_g,
                                  w_rec_out, w_att_out, w_out, wrt, rbias, l)
        pos, meta = _route_pos(idx)
        xs = _dispatch(meta, pos, h2, zrow)
        ys = _experts(meta, xs, w_gate_e, w_up_e, w_down_e, l)
        xa, xb = _combine(pos, wts.T, x1, modseg, final_g, ys, l, final=(l == DEPTH - 1))
        n_cu = T_CTX // UNIT
        spu = UNIT // SEQ
        hf_last = stf[:n_cu].reshape(n_cu, spu, 2, D)[:, :, 1].reshape(BATCH, D)
        hb_first = stb[:n_cu].reshape(n_cu, spu, 2, D)[:, :, 0].reshape(BATCH, D)
        new_s.append(jnp.stack([hf_last, hb_first], axis=1))

    y_prompt = xa.reshape(BATCH, SEQ, D)
    y_sample = xb.reshape(DEC_BATCH, DEC_SEQ, D)
    return (y_prompt, y_sample, caches[0], caches[1], jnp.stack(new_s, axis=1))
```

```python
import functools

import numpy as np
import jax
import jax.numpy as jnp
from jax import lax
from jax.experimental import pallas as pl
from jax.experimental.pallas import tpu as pltpu

F32 = jnp.float32
BF16 = jnp.bfloat16

D = 1024
BATCH = 16
SEQ = 256
DEPTH = 2
DEC_BATCH = 4
DEC_SEQ = 1024
PAST = 256
GRID_W = 64
N_HEADS = 16
N_KV = 4
HD = 64
N_RG_BLK = 8
RG_BLK = 128
RG_C = 8.0
N_EXP = 16
D_EXP = 512
ROPE_THETA = 10000.0
EPS = 1e-6
P_IN = 5632
TINY = float(np.finfo(np.float32).tiny)
NEG_LOG2E = -float(np.log2(np.e))

T_CTX = BATCH * SEQ
T_LAT = DEC_BATCH * DEC_SEQ
T = T_CTX + T_LAT
SEG = 256
N_SEG = T // SEG
UNIT = 1024
N_UNIT = T // UNIT
CHUNK = UNIT // 8
CSTRIDE = CHUNK + 8

VMEM_LIMIT = 56 * 1024 * 1024


def _cp(sem):
    return pltpu.CompilerParams(dimension_semantics=sem, vmem_limit_bytes=VMEM_LIMIT)


def _split(x):
    hi = x.astype(BF16)
    lo = (x - hi.astype(F32)).astype(BF16)
    return hi, lo


def _sigmoid(x):
    return 0.5 * jnp.tanh(0.5 * x) + 0.5


NCH = D // 128


def _store_row_tiles(ref, x):
    n = x.shape[0]
    for c in range(NCH):
        ref[pl.ds(c, n, stride=NCH), :] = x[:, c * 128:(c + 1) * 128]


def _load_row_tiles(ref, n):
    return jnp.concatenate([ref[pl.ds(c, n, stride=NCH), :] for c in range(NCH)], axis=-1)


def _dot(a, b):
    return jnp.dot(a, b, preferred_element_type=F32)


def _dot_nt(a, b):
    return lax.dot_general(a, b, (((1,), (1,)), ((), ())), preferred_element_type=F32)


def _mods_kernel(c_ref, w_ref, b_ref, o_ref):
    c = c_ref[...]
    s = c * jax.nn.sigmoid(c)
    s_hi, s_lo = _split(s)
    w_hi, w_lo = _split(w_ref[...])
    o_ref[...] = _dot(s_hi, w_hi) + _dot(s_hi, w_lo) + _dot(s_lo, w_hi) + b_ref[...]


def _mods(cvecs, w_mod, b_mod):
    tn = 1536
    return pl.pallas_call(
        _mods_kernel,
        grid=(DEPTH, 6 * D // tn),
        in_specs=[
            pl.BlockSpec((8, D), lambda l, j: (0, 0)),
            pl.BlockSpec((None, D, tn), lambda l, j: (l, 0, j)),
            pl.BlockSpec((None, 1, tn), lambda l, j: (l, 0, j)),
        ],
        out_specs=pl.BlockSpec((None, 8, tn), lambda l, j: (l, 0, j)),
        out_shape=jax.ShapeDtypeStruct((DEPTH, 8, 6 * D), F32),
        compiler_params=_cp(("arbitrary", "arbitrary")),
        name="mods",
    )(cvecs, w_mod, b_mod.reshape(DEPTH, 1, 6 * D))


def _norm_mod(x, g, shift, scale):
    ms = jnp.mean(x * x, axis=-1, keepdims=True)
    return x * lax.rsqrt(ms + EPS) * g * (1.0 + scale) + shift


def _two_part_specs(tm, n_ctx):
    return [pl.BlockSpec((tm, D), lambda i, *_: (jnp.minimum(i, n_ctx - 1), 0)),
            pl.BlockSpec((tm, D), lambda i, *_: (jnp.maximum(i - n_ctx, 0), 0))]


def _inproj_kernel(xa_ref, xb_ref, mod_ref, g_ref, w_ref, o_ref, h_ref, *, tm):
    def prologue(x_ref):
        def seg(s, carry):
            r0 = pl.multiple_of(s * SEG, SEG)
            m = mod_ref[s]
            h = _norm_mod(x_ref[pl.ds(r0, SEG), :], g_ref[...], m[0:1, :], m[1:2, :])
            h_ref[pl.ds(r0, SEG), :] = h.astype(BF16)
            return carry
        lax.fori_loop(0, tm // SEG, seg, 0)

    first = pl.program_id(1) == 0
    is_ctx = pl.program_id(0) < T_CTX // tm

    @pl.when(first & is_ctx)
    def _():
        prologue(xa_ref)

    @pl.when(first & jnp.logical_not(is_ctx))
    def _():
        prologue(xb_ref)

    o_ref[...] = _dot(h_ref[...], w_ref[...].astype(BF16)).astype(BF16)


def _inproj(xa, xb, modseg, norm_g, w_in, l):
    tm, tn = 2048, 512
    return pl.pallas_call(
        functools.partial(_inproj_kernel, tm=tm),
        grid=(T // tm, P_IN // tn),
        in_specs=_two_part_specs(tm, T_CTX // tm) + [
            pl.BlockSpec((None, tm // SEG, 8, D), lambda i, j: (l, i, 0, 0)),
            pl.BlockSpec((None, 1, D), lambda i, j: (l, 0, 0)),
            pl.BlockSpec((None, D, tn), lambda i, j: (l, 0, j)),
        ],
        out_specs=pl.BlockSpec((tm, tn), lambda i, j: (i, j)),
        out_shape=jax.ShapeDtypeStruct((T, P_IN), BF16),
        scratch_shapes=[pltpu.VMEM((tm, D), BF16)],
        compiler_params=_cp(("arbitrary", "arbitrary")),
        name="inproj",
    )(xa, xb, modseg, norm_g.reshape(DEPTH, 1, D), w_in)


REC_CW = 512
PAD_F = 16
PAD_B = 8
GATE_ROWS = 256


def _rec_kernel(xr_ref, gate_ref, cw_ref, pv_ref, wg_ref, h0_ref,
                y_ref, stf_ref, stb_ref,
                xs_ref, af_ref, bf_ref, ab_ref, bb_ref, nat_ref, wgh_ref):
    u = pl.program_id(0)
    is_ctx = u < (T_CTX // UNIT)
    cps = jnp.where(is_ctx, SEQ // CHUNK, DEC_SEQ // CHUNK)
    nblk = REC_CW // RG_BLK

    def lanes(n):
        return slice(n * RG_BLK, (n + 1) * RG_BLK)

    def tile(r):
        return slice(8 * r, 8 * r + 8)

    for c in range(8):
        for n in range(nblk):
            nat_ref[n, c * CSTRIDE:c * CSTRIDE + CHUNK, :] = xr_ref[c * CHUNK:(c + 1) * CHUNK, lanes(n)].astype(F32)
    for r in range(CHUNK):
        for n in range(nblk):
            xs_ref[PAD_F + 8 * r:PAD_F + 8 * r + 8, lanes(n)] = nat_ref[n, pl.ds(r, 8, stride=CSTRIDE), :]
    chunk_id = lax.broadcasted_iota(jnp.int32, (8, 1), 0)
    seq_start = jnp.bitwise_and(chunk_id, cps - 1) == 0
    seq_end = jnp.bitwise_and(chunk_id, cps - 1) == cps - 1
    for j, r in ((0, CHUNK - 2), (1, CHUNK - 1)):
        prev_chunk = pltpu.roll(xs_ref[PAD_F + 8 * r:PAD_F + 8 * r + 8, :], 1, 0)
        xs_ref[tile(j), :] = jnp.where(seq_start, 0.0, prev_chunk)
    next_chunk = pltpu.roll(xs_ref[PAD_F:PAD_F + 8, :], 7, 0)
    xs_ref[PAD_F + UNIT:PAD_F + UNIT + PAD_B, :] = jnp.where(seq_end, 0.0, next_chunk)

    pv = pv_ref[...]
    cwts = cw_ref[...]
    conv_b = pv[6:7, :]

    def softplus_neg(lam):
        z = -lam
        return jnp.maximum(z, 0.0) + jnp.log1p(jnp.exp(-jnp.abs(z)))

    c4s = tuple((0.5 * RG_C) * softplus_neg(pv[4 + d:5 + d, :]) for d in range(2))
    pv_h = 0.5 * pv
    for n in range(nblk):
        wgh_ref[n] = (0.5 * wg_ref[n]).astype(BF16)
    a_refs = (af_ref, ab_ref)
    b_refs = (bf_ref, bb_ref)

    def gates(g, carry):
        base = pl.multiple_of(g * GATE_ROWS, GATE_ROWS)

        def tap(d):
            return xs_ref[pl.ds(pl.multiple_of(base + PAD_F + 8 * d, 8), GATE_ROWS), :]

        xc = conv_b + tap(-2) * cwts[0:1, :]
        xc = xc + tap(-1) * cwts[1:2, :]
        xc = xc + tap(0) * cwts[2:3, :]
        xc = xc + tap(1) * cwts[3:4, :]
        for n in range(nblk):
            ls = lanes(n)
            xn = xc[:, ls]
            hx = 0.5 * xn
            pre_h = _dot(xn.astype(BF16), wgh_ref[n])
            for d in range(2):
                th_r = jnp.tanh(pre_h[:, (2 * d) * RG_BLK:(2 * d + 1) * RG_BLK] + pv_h[2 * d:2 * d + 1, ls])
                th_i = jnp.tanh(pre_h[:, (2 * d + 1) * RG_BLK:(2 * d + 2) * RG_BLK] + pv_h[2 * d + 1:2 * d + 2, ls])
                c4 = c4s[d][:, ls]
                nla = c4 * th_r + c4
                a = jnp.exp2(nla * NEG_LOG2E)
                s = jnp.tanh(nla) * (a * a + 1.0)
                inp = (s * lax.rsqrt(jnp.maximum(s, TINY))) * (hx * th_i + hx)
                a_refs[d][pl.ds(base, GATE_ROWS), ls] = a
                b_refs[d][pl.ds(base, GATE_ROWS), ls] = inp
        return carry

    lax.fori_loop(0, UNIT // GATE_ROWS, gates, 0)

    hf = hb = jnp.zeros((8, REC_CW), F32)
    pf = pb = jnp.ones((8, REC_CW), F32)
    for r in range(CHUNK):
        rf, rb = tile(r), tile(CHUNK - 1 - r)
        a = af_ref[rf, :]
        hf = a * hf + bf_ref[rf, :]
        pf = a * pf
        bf_ref[rf, :] = hf
        af_ref[rf, :] = pf
        a = ab_ref[rb, :]
        hb = a * hb + bb_ref[rb, :]
        pb = a * pb
        bb_ref[rb, :] = hb
        ab_ref[rb, :] = pb

    h0f = h0_ref[0:1, :]
    h0b = h0_ref[1:2, :]
    cf = [h0f]
    for c in range(1, 8):
        chain = hf[c - 1:c, :] + pf[c - 1:c, :] * cf[c - 1]
        cf.append(jnp.where(jnp.bitwise_and(c, cps - 1) == 0, h0f, chain))
    cb = [None] * 8
    cb[7] = h0b
    for c in range(6, -1, -1):
        chain = hb[c + 1:c + 2, :] + pb[c + 1:c + 2, :] * cb[c + 1]
        cb[c] = jnp.where(jnp.bitwise_and(c, cps - 1) == cps - 1, h0b, chain)
    carry_f = jnp.concatenate(cf, axis=0)
    carry_b = jnp.concatenate(cb, axis=0)
    stf_ref[...] = hf + pf * carry_f
    stb_ref[...] = hb + pb * carry_b

    for r in range(CHUNK):
        h = (bf_ref[tile(r), :] + af_ref[tile(r), :] * carry_f) + (bb_ref[tile(r), :] + ab_ref[tile(r), :] * carry_b)
        for n in range(nblk):
            nat_ref[n, pl.ds(r, 8, stride=CSTRIDE), :] = h[:, lanes(n)]

    for c in range(8):
        rows = slice(c * CHUNK, (c + 1) * CHUNK)
        for n in range(nblk):
            g = gate_ref[rows, lanes(n)].astype(F32)
            h = nat_ref[n, c * CSTRIDE:c * CSTRIDE + CHUNK, :]
            y_ref[rows, lanes(n)] = (h * jax.nn.gelu(g, approximate=True)).astype(BF16)


def _rec(proj, conv_w, pvec, wg, h0, l):
    ncb = D // REC_CW
    return pl.pallas_call(
        _rec_kernel,
        grid=(N_UNIT, ncb),
        in_specs=[
            pl.BlockSpec((UNIT, REC_CW), lambda u, c: (u, c)),
            pl.BlockSpec((UNIT, REC_CW), lambda u, c: (u, ncb + c)),
            pl.BlockSpec((None, 4, REC_CW), lambda u, c: (l, 0, c)),
            pl.BlockSpec((None, 8, REC_CW), lambda u, c: (l, 0, c)),
            pl.BlockSpec((None, REC_CW // RG_BLK, RG_BLK, 4 * RG_BLK), lambda u, c: (l, c, 0, 0)),
            pl.BlockSpec((None, 2, REC_CW), lambda u, c: (u, 0, c)),
        ],
        out_specs=[
            pl.BlockSpec((UNIT, REC_CW), lambda u, c: (u, c)),
            pl.BlockSpec((None, 8, REC_CW), lambda u, c: (u, 0, c)),
            pl.BlockSpec((None, 8, REC_CW), lambda u, c: (u, 0, c)),
        ],
        out_shape=[
            jax.ShapeDtypeStruct((T, D), BF16),
            jax.ShapeDtypeStruct((N_UNIT, 8, D), F32),
            jax.ShapeDtypeStruct((N_UNIT, 8, D), F32),
        ],
        scratch_shapes=[pltpu.VMEM((PAD_F + UNIT + PAD_B, REC_CW), F32)]
        + [pltpu.VMEM((UNIT, REC_CW), F32)] * 4
        + [pltpu.VMEM((REC_CW // RG_BLK, 8 * CSTRIDE, RG_BLK), F32),
           pltpu.VMEM((REC_CW // RG_BLK, RG_BLK, 4 * RG_BLK), BF16)],
        compiler_params=_cp(("arbitrary", "arbitrary")),
        name="rec",
    )(proj, proj, conv_w, pvec, wg, h0)


def _head_norm(x, g128, bd):
    hi, lo = _split(x * x)
    ms = _dot(hi, bd) + _dot(lo, bd)
    return x * lax.rsqrt(ms + EPS) * g128


def _rope(x, cos, sin_signed):
    lane = lax.broadcasted_iota(jnp.int32, x.shape, 1)
    first_half = jnp.bitwise_and(lane, HD - 1) < HD // 2
    partner = jnp.where(first_half, pltpu.roll(x, 2 * HD - HD // 2, 1), pltpu.roll(x, HD // 2, 1))
    return x * cos + partner * sin_signed


def _with_ones(v):
    return jnp.concatenate([v, jnp.ones_like(v)], axis=-1)


def _softmax_pv(q, k, v_ext):
    s = _dot_nt(q, k)
    m = jnp.max(s, axis=-1, keepdims=True)
    p = jnp.exp2(s - m).astype(BF16)
    r = _dot(p, v_ext)
    return r[:, :HD] / r[:, HD:HD + 1]


def _attend_heads(q_ref, k, v_ext):
    return jnp.concatenate([_softmax_pv(q_ref[h], k, v_ext) for h in range(N_HEADS // N_KV)], axis=-1)


def _attn_ctx_kernel(q_ref, k_ref, v_ref, qg_ref, kg_ref, bd_ref, *rest, slab):
    o_ref, ko_ref, vo_ref = rest[-3:]
    for p in range(slab):
        ko_ref[p] = rest[0][p]
        vo_ref[p] = rest[1][p]
    g = N_HEADS // N_KV
    bd = bd_ref[...]
    odd = lax.rem(pl.program_id(1), 2) == 1
    scale = HD ** -0.5 * float(np.log2(np.e))

    kx = _head_norm(k_ref[...].astype(F32), kg_ref[...], bd)
    vx = v_ref[...].astype(F32)
    k_new = jnp.where(odd, kx[:, HD:], kx[:, :HD])
    v_new = jnp.where(odd, vx[:, HD:], vx[:, :HD])
    ko_ref[slab] = k_new
    vo_ref[slab] = v_new
    heads = []
    for j in range(g // 2):
        x = _head_norm(q_ref[:, 2 * HD * j:2 * HD * (j + 1)].astype(F32), qg_ref[...], bd) * scale
        heads += [x[:, :HD].astype(BF16), x[:, HD:].astype(BF16)]
    o = _softmax_pv(jnp.concatenate(heads, axis=0), k_new.astype(BF16), _with_ones(v_new.astype(BF16)))
    o_ref[...] = jnp.concatenate([o[h * SEQ:(h + 1) * SEQ] for h in range(g)], axis=-1).astype(BF16)


def _attn_ctx(proj, qg128, kg128, bd, l, prev_caches=()):
    g = N_HEADS // N_KV
    qcol = 2 * D // (g * HD)
    kcol = 3 * D // (2 * HD)
    vcol = kcol + N_KV // 2
    cache_spec = pl.BlockSpec((None, l + 1, None, SEQ, HD), lambda b, h: (b, 0, h, 0, 0))
    prev_specs = [pl.BlockSpec((None, l, None, SEQ, HD), lambda b, h: (b, 0, h, 0, 0))] * 2 if l else []
    cache_shape = jax.ShapeDtypeStruct((BATCH, l + 1, N_KV, SEQ, HD), F32)
    return pl.pallas_call(
        functools.partial(_attn_ctx_kernel, slab=l),
        grid=(BATCH, N_KV),
        in_specs=[
            pl.BlockSpec((SEQ, g * HD), lambda b, h: (b, qcol + h)),
            pl.BlockSpec((SEQ, 2 * HD), lambda b, h: (b, kcol + h // 2)),
            pl.BlockSpec((SEQ, 2 * HD), lambda b, h: (b, vcol + h // 2)),
            pl.BlockSpec((None, 1, 2 * HD), lambda b, h: (l, 0, 0)),
            pl.BlockSpec((None, 1, 2 * HD), lambda b, h: (l, 0, 0)),
            pl.BlockSpec((2 * HD, 2 * HD), lambda b, h: (0, 0)),
        ] + prev_specs,
        out_specs=[pl.BlockSpec((SEQ, g * HD), lambda b, h: (b, h)), cache_spec, cache_spec],
        out_shape=[jax.ShapeDtypeStruct((T_CTX, D), BF16), cache_shape, cache_shape],
        compiler_params=_cp(("arbitrary", "arbitrary")),
        name="attn_ctx",
    )(proj, proj, proj, qg128, kg128, bd, *prev_caches)


def _attn_lat_kernel(q_ref, k_ref, v_ref, qg_ref, kg_ref, cos_ref, sin_ref, bd_ref, pk_ref, pv_ref,
                     o_ref, q_s, k_s, v_s):
    g = N_HEADS // N_KV
    bd, cos, sin = bd_ref[...], cos_ref[...], sin_ref[...]
    odd = lax.rem(pl.program_id(1), 2) == 1
    scale = HD ** -0.5 * float(np.log2(np.e))

    kx = _rope(_head_norm(k_ref[...].astype(F32), kg_ref[...], bd), cos, sin)
    vx = v_ref[...]
    k_s[0:PAST, :] = pk_ref[...].astype(BF16)
    k_s[PAST:, :] = jnp.where(odd, kx[:, HD:], kx[:, :HD]).astype(BF16)
    v_s[0:PAST, :] = _with_ones(pv_ref[...].astype(BF16))
    v_s[PAST:, :] = _with_ones(jnp.where(odd, vx[:, HD:], vx[:, :HD]))
    for j in range(g // 2):
        x = _head_norm(q_ref[:, 2 * HD * j:2 * HD * (j + 1)].astype(F32), qg_ref[...], bd)
        x = _rope(x, cos, sin) * scale
        q_s[2 * j] = x[:, :HD].astype(BF16)
        q_s[2 * j + 1] = x[:, HD:].astype(BF16)

    o_ref[...] = _attend_heads(q_s, k_s[...], v_s[...]).astype(BF16)


def _attn_lat(proj, cache_k, cache_v, qg128, kg128, cos128, sin128, bd, l):
    g = N_HEADS // N_KV
    row0 = T_CTX // DEC_SEQ
    qcol = 2 * D // (g * HD)
    kcol = 3 * D // (2 * HD)
    vcol = kcol + N_KV // 2
    return pl.pallas_call(
        _attn_lat_kernel,
        grid=(DEC_BATCH, N_KV),
        in_specs=[
            pl.BlockSpec((DEC_SEQ, g * HD), lambda b, h: (row0 + b, qcol + h)),
            pl.BlockSpec((DEC_SEQ, 2 * HD), lambda b, h: (row0 + b, kcol + h // 2)),
            pl.BlockSpec((DEC_SEQ, 2 * HD), lambda b, h: (row0 + b, vcol + h // 2)),
            pl.BlockSpec((None, 1, 2 * HD), lambda b, h: (l, 0, 0)),
            pl.BlockSpec((None, 1, 2 * HD), lambda b, h: (l, 0, 0)),
            pl.BlockSpec((DEC_SEQ, 2 * HD), lambda b, h: (0, 0)),
            pl.BlockSpec((DEC_SEQ, 2 * HD), lambda b, h: (0, 0)),
            pl.BlockSpec((2 * HD, 2 * HD), lambda b, h: (0, 0)),
            pl.BlockSpec((None, None, None, PAST, HD), lambda b, h: (b, l, h, 0, 0)),
            pl.BlockSpec((None, None, None, PAST, HD), lambda b, h: (b, l, h, 0, 0)),
        ],
        out_specs=pl.BlockSpec((DEC_SEQ, g * HD), lambda b, h: (b, h)),
        out_shape=jax.ShapeDtypeStruct((T_LAT, D), BF16),
        scratch_shapes=[pltpu.VMEM((g, DEC_SEQ, HD), BF16), pltpu.VMEM((PAST + DEC_SEQ, HD), BF16),
                        pltpu.VMEM((PAST + DEC_SEQ, 2 * HD), BF16)],
        compiler_params=_cp(("arbitrary", "arbitrary")),
        name="attn_lat",
    )(proj, proj, proj, qg128, kg128, cos128, sin128, bd, cache_k, cache_v)


MERGE_TM = 512


def _route(lt, bias):
    rows = [lt[e:e + 1, :] for e in range(N_EXP)]
    m = rows[0]
    for e in range(1, N_EXP):
        m = jnp.maximum(m, rows[e])
    ex = [jnp.exp(r - m) for r in rows]
    z = ex[0]
    for e in range(1, N_EXP):
        z = z + ex[e]
    probs = [x / z for x in ex]
    sel = [probs[e] + bias[e:e + 1, :] for e in range(N_EXP)]

    def top2_sum(v):
        a, b = jnp.maximum(v[0], v[1]), jnp.minimum(v[0], v[1])
        c, d = jnp.maximum(v[2], v[3]), jnp.minimum(v[2], v[3])
        return jnp.maximum(a, c) + jnp.maximum(jnp.minimum(a, c), jnp.maximum(b, d))

    scores = [top2_sum(sel[4 * g:4 * g + 4]) for g in range(4)]
    best = jnp.zeros_like(scores[0], dtype=jnp.int32)
    best_s = scores[0]
    for g in range(1, 4):
        take = scores[g] > best_s
        best = jnp.where(take, g, best)
        best_s = jnp.where(take, scores[g], best_s)
    cs, cp = [], []
    for j in range(4):
        s_j, p_j = sel[j], probs[j]
        for g in range(1, 4):
            s_j = jnp.where(best == g, sel[4 * g + j], s_j)
            p_j = jnp.where(best == g, probs[4 * g + j], p_j)
        cs.append(s_j)
        cp.append(p_j)
    neg = jnp.full_like(cs[0], -jnp.inf)

    def argmax4(v):
        bi = jnp.zeros_like(best)
        bv = v[0]
        for j in range(1, 4):
            take = v[j] > bv
            bi = jnp.where(take, j, bi)
            bv = jnp.where(take, v[j], bv)
        return bi

    def pick(v, idx):
        out = v[0]
        for j in range(1, 4):
            out = jnp.where(idx == j, v[j], out)
        return out

    i1 = argmax4(cs)
    cs2 = [jnp.where(i1 == j, neg, cs[j]) for j in range(4)]
    i2 = argmax4(cs2)
    i2 = jnp.where((i2 == 0) & (i1 == 0), 1, i2)
    w1, w2 = pick(cp, i1), pick(cp, i2)
    den = w1 + w2
    return best * 4 + i1, best * 4 + i2, w1 / den, w2 / den


def _merge_kernel(yrec_ref, oa_ref, ob_ref, gr0_ref, gr1_ref, ga0_ref, ga1_ref, xa_ref, xb_ref, mod_ref, g2_ref,
                  wrec_ref, watt_ref, wout_ref, wrt_ref, rb_ref,
                  x1_ref, h2_ref, idx_ref, wts_ref,
                  wrec_s, watt_s, wout_s):
    @pl.when(pl.program_id(0) == 0)
    def _():
        wrec_s[...] = wrec_ref[...].astype(BF16)
        watt_s[...] = watt_ref[...].astype(BF16)
        wout_s[...] = wout_ref[...].astype(BF16)

    is_ctx = pl.program_id(0) < T_CTX // MERGE_TM
    args = (yrec_ref, gr0_ref, gr1_ref, ga0_ref, ga1_ref, mod_ref, g2_ref, wrt_ref, rb_ref,
            x1_ref, h2_ref, idx_ref, wts_ref, wrec_s, watt_s, wout_s)

    @pl.when(is_ctx)
    def _():
        _merge_body(oa_ref, xa_ref, *args)

    @pl.when(jnp.logical_not(is_ctx))
    def _():
        _merge_body(ob_ref, xb_ref, *args)


def _merge_body(oatt_ref, x_ref, yrec_ref, gr0_ref, gr1_ref, ga0_ref, ga1_ref, mod_ref, g2_ref, wrt_ref, rb_ref,
                x1_ref, h2_ref, idx_ref, wts_ref, wrec_s, watt_s, wout_s):
    half = D // 2
    b_rec = _dot(yrec_ref[...], wrec_s[...])
    b_att = _dot(oatt_ref[...], watt_s[...])
    m0 = _sigmoid(gr0_ref[...].astype(F32)) * b_rec[:, :half] + _sigmoid(ga0_ref[...].astype(F32)) * b_att[:, :half]
    m1 = _sigmoid(gr1_ref[...].astype(F32)) * b_rec[:, half:] + _sigmoid(ga1_ref[...].astype(F32)) * b_att[:, half:]
    merged = jnp.concatenate([m0, m1], axis=-1).astype(BF16)
    out = _dot(merged, wout_s[...])

    hs = []
    for s in range(MERGE_TM // SEG):
        rows = slice(s * SEG, (s + 1) * SEG)
        m = mod_ref[s]
        x1 = x_ref[rows, :] + m[2:3, :] * out[rows, :]
        x1_ref[rows, :] = x1
        h2 = _norm_mod(x1, g2_ref[...], m[3:4, :], m[4:5, :])
        hs.append(h2)
    h2 = jnp.concatenate(hs, axis=0)
    _store_row_tiles(h2_ref, h2)

    h_hi, h_lo = _split(h2)
    w_hi, w_lo = _split(wrt_ref[...])
    lt = _dot_nt(w_hi, h_hi) + _dot_nt(w_hi, h_lo) + _dot_nt(w_lo, h_hi)
    e1, e2, w1, w2 = _route(lt, rb_ref[...])
    idx_ref[...] = jnp.concatenate([e1, e2], axis=0)
    wts_ref[...] = jnp.concatenate([w1, w2], axis=0)


def _merge(yrec, o_ctx, o_lat, proj, xa, xb, modseg, norm2_g, w_rec_out, w_att_out, w_out, wrt, rbias, l):
    tm = MERGE_TM
    half = D // 2
    gcol = (3 * D + 2 * N_KV * HD) // half
    wspec = pl.BlockSpec((None, D, D), lambda i: (l, 0, 0))
    return pl.pallas_call(
        _merge_kernel,
        grid=(T // tm,),
        in_specs=[pl.BlockSpec((tm, D), lambda i: (i, 0))] + _two_part_specs(tm, T_CTX // tm) + [
            pl.BlockSpec((tm, half), lambda i: (i, gcol)),
            pl.BlockSpec((tm, half), lambda i: (i, gcol + 1)),
            pl.BlockSpec((tm, half), lambda i: (i, gcol + 2)),
            pl.BlockSpec((tm, half), lambda i: (i, gcol + 3)),
        ] + _two_part_specs(tm, T_CTX // tm) + [
            pl.BlockSpec((None, tm // SEG, 8, D), lambda i: (l, i, 0, 0)),
            pl.BlockSpec((None, 1, D), lambda i: (l, 0, 0)),
            wspec, wspec, wspec,
            pl.BlockSpec((N_EXP, D), lambda i: (0, 0)),
            pl.BlockSpec((N_EXP, 1), lambda i: (0, 0)),
        ],
        out_specs=[
            pl.BlockSpec((tm, D), lambda i: (i, 0)),
            pl.BlockSpec((tm * NCH, 128), lambda i: (i, 0)),
            pl.BlockSpec((2, tm), lambda i: (0, i)),
            pl.BlockSpec((2, tm), lambda i: (0, i)),
        ],
        out_shape=[
            jax.ShapeDtypeStruct((T, D), F32),
            jax.ShapeDtypeStruct((T * NCH, 128), F32),
            jax.ShapeDtypeStruct((2, T), jnp.int32),
            jax.ShapeDtypeStruct((2, T), F32),
        ],
        scratch_shapes=[pltpu.VMEM((D, D), BF16)] * 3,
        compiler_params=_cp(("arbitrary",)),
        name="merge",
    )(yrec, o_ctx, o_lat, proj, proj, proj, proj, xa, xb, modseg, norm2_g.reshape(DEPTH, 1, D),
      w_rec_out, w_att_out, w_out, wrt, rbias)


MOE_TM = 512
MOE_NT = 2 * T // MOE_TM + N_EXP
MOE_ROWS = MOE_NT * MOE_TM
META_TILE_E, META_CNT, META_OFF, META_END, META_NT, META_NEXT_E = 0, 1, 2, 3, 4, 5


def _pos_kernel(idx_ref, pos_ref, meta_ref):
    shift = MOE_TM.bit_length() - 1
    idx = idx_ref[...]
    eid = lax.broadcasted_iota(jnp.int32, (N_EXP, T), 0)
    m0 = eid == idx[0:1, :]
    m1 = eid == idx[1:2, :]
    member = jnp.where(m0 | m1, 1.0, 0.0)
    cnt = jnp.sum(member, axis=1, keepdims=True).astype(jnp.int32)
    ntile = jnp.right_shift(cnt + (MOE_TM - 1), shift)
    offs, acc = [], jnp.zeros((1, 1), jnp.int32)
    for e in range(N_EXP):
        offs.append(acc)
        acc = acc + ntile[e:e + 1, :]
    off_t = jnp.concatenate(offs, axis=0)
    end_t = off_t + ntile

    blk = 256
    r_i = lax.broadcasted_iota(jnp.int32, (blk, blk), 0)
    c_i = lax.broadcasted_iota(jnp.int32, (blk, blk), 1)
    upper = jnp.where(r_i <= c_i, 1.0, 0.0).astype(BF16)
    run = (off_t * MOE_TM).astype(F32)
    for j in range(T // blk):
        ls = slice(j * blk, (j + 1) * blk)
        mb = member[:, ls]
        inc = _dot(mb.astype(BF16), upper)
        dest = run + inc - mb
        pos_ref[0:1, ls] = jnp.sum(jnp.where(m0[:, ls], dest, 0.0), axis=0, keepdims=True).astype(jnp.int32)
        pos_ref[1:2, ls] = jnp.sum(jnp.where(m1[:, ls], dest, 0.0), axis=0, keepdims=True).astype(jnp.int32)
        run = run + inc[:, blk - 1:blk]

    lane = lax.broadcasted_iota(jnp.int32, (1, 128), 1)
    zero = jnp.zeros((1, 128), jnp.int32)
    tile_e, cnt_row, off_row, end_row = zero, zero, zero, zero
    for e in range(N_EXP):
        tile_e = tile_e + jnp.where(lane >= end_t[e:e + 1, :], 1, 0)
        here = lane == e
        cnt_row = jnp.where(here, cnt[e:e + 1, :], cnt_row)
        off_row = jnp.where(here, off_t[e:e + 1, :] * MOE_TM, off_row)
        end_row = jnp.where(here, end_t[e:e + 1, :] * MOE_TM, end_row)
    tile_e = jnp.minimum(tile_e, N_EXP - 1)
    nt_row = zero + acc
    next_row = zero
    nxt = jnp.full((1, 1), -1, jnp.int32)
    for e in reversed(range(N_EXP)):
        next_row = jnp.where(lane == e, nxt, next_row)
        nxt = jnp.where(cnt[e:e + 1, :] > 0, e, nxt)
    meta_ref[...] = jnp.concatenate([tile_e, cnt_row, off_row, end_row, nt_row, next_row, zero, zero], axis=0)


def _route_pos(idx):
    return pl.pallas_call(
        _pos_kernel,
        grid=(1,),
        in_specs=[pl.BlockSpec((2, T), lambda i: (0, 0))],
        out_specs=[pl.BlockSpec((2, T), lambda i: (0, 0)), pl.BlockSpec((8, 128), lambda i: (0, 0))],
        out_shape=[jax.ShapeDtypeStruct((2, T), jnp.int32), jax.ShapeDtypeStruct((8, 128), jnp.int32)],
        compiler_params=_cp(("arbitrary",)),
        name="route_pos",
    )(idx)


DISP_TM = 256


def _dispatch_kernel(meta_ref, pos_ref, h_ref, z_hbm, xs_hbm, sem):
    i = pl.program_id(0)

    def row_copy(src, src_row, dst_row):
        return pltpu.make_async_copy(src.at[pl.ds(src_row * NCH, NCH), :],
                                     xs_hbm.at[pl.ds(pl.multiple_of(dst_row * NCH, NCH), NCH), :], sem)

    for r in range(DISP_TM):
        row_copy(h_ref, r, pos_ref[0, r]).start(priority=0)
        row_copy(h_ref, r, pos_ref[1, r]).start(priority=1)

    e = jnp.minimum(i, N_EXP - 1)
    pad0 = meta_ref[META_OFF, e] + meta_ref[META_CNT, e]
    npad = jnp.where(i < N_EXP, meta_ref[META_END, e] - pad0, 0)

    def pad_copies(act):
        s = pad0
        for bit in reversed(range(MOE_TM.bit_length() - 1)):
            size = 1 << bit
            part = jnp.bitwise_and(npad, size)

            @pl.when(part != 0)
            def _():
                dst = pl.ds(pl.multiple_of(s * NCH, NCH), size * NCH)
                act(pltpu.make_async_copy(z_hbm.at[pl.ds(0, size * NCH), :], xs_hbm.at[dst, :], sem))

            s = s + part

    pad_copies(lambda c: c.start())

    for _ in range(2):
        pltpu.make_async_copy(h_ref, xs_hbm.at[pl.ds(0, DISP_TM * NCH), :], sem).wait()

    tail = meta_ref[META_NT, 0] + i
    has_tail = (i < N_EXP) & (tail < MOE_NT)

    def tail_copy():
        rows = pl.ds(pl.multiple_of(tail * (MOE_TM * NCH), MOE_TM * NCH), MOE_TM * NCH)
        return pltpu.make_async_copy(z_hbm, xs_hbm.at[rows, :], sem)

    @pl.when(has_tail)
    def _():
        tail_copy().start()

    pad_copies(lambda c: c.wait())

    @pl.when(has_tail)
    def _():
        tail_copy().wait()


def _dispatch(meta, pos, h2, zrow):
    return pl.pallas_call(
        _dispatch_kernel,
        grid_spec=pltpu.PrefetchScalarGridSpec(
            num_scalar_prefetch=1,
            grid=(T // DISP_TM,),
            in_specs=[
                pl.BlockSpec((2, DISP_TM), lambda i, meta: (0, i), memory_space=pltpu.SMEM),
                pl.BlockSpec((DISP_TM * NCH, 128), lambda i, meta: (i, 0)),
                pl.BlockSpec((MOE_TM * NCH, 128), lambda i, meta: (0, 0)),
            ],
            out_specs=pl.BlockSpec(memory_space=pl.ANY),
            scratch_shapes=[pltpu.SemaphoreType.DMA],
        ),
        out_shape=jax.ShapeDtypeStruct((MOE_ROWS * NCH, 128), F32),
        compiler_params=_cp(("arbitrary",)),
        name="dispatch",
    )(meta, pos, h2, zrow)


def _experts_kernel(meta_ref, xs_ref, wg_hbm, wu_hbm, wd_hbm, ys_ref,
                    wg_f, wu_f, wd_f, wg_s, wu_s, wd_s, sem, *, l):
    j = pl.program_id(0)
    live = j < meta_ref[META_NT, 0]
    e = meta_ref[META_TILE_E, j]
    e_prev = meta_ref[META_TILE_E, jnp.maximum(j - 1, 0)]

    def fetch(ex):
        return (pltpu.make_async_copy(wg_hbm.at[l, ex], wg_f, sem.at[0]),
                pltpu.make_async_copy(wu_hbm.at[l, ex], wu_f, sem.at[1]),
                pltpu.make_async_copy(wd_hbm.at[l, ex], wd_f, sem.at[2]))

    @pl.when(j == 0)
    def _():
        for c in fetch(e):
            c.start()

    @pl.when(live & ((j == 0) | (e != e_prev)))
    def _():
        for c, dst, src in zip(fetch(e), (wg_s, wu_s, wd_s), (wg_f, wu_f, wd_f)):
            c.wait()
            dst[...] = src[...].astype(BF16)
        nxt = meta_ref[META_NEXT_E, e]

        @pl.when(nxt >= 0)
        def _():
            for c in fetch(nxt):
                c.start()

    @pl.when(live)
    def _():
        x = _load_row_tiles(xs_ref, MOE_TM).astype(BF16)
        g = _dot(x, wg_s[...])
        u = _dot(x, wu_s[...])
        act = (g * _sigmoid(g)) * u
        _store_row_tiles(ys_ref, _dot(act.astype(BF16), wd_s[...]))

    @pl.when(jnp.logical_not(live))
    def _():
        ys_ref[...] = jnp.zeros_like(ys_ref)


def _experts(meta, xs, w_gate_e, w_up_e, w_down_e, l):
    def tile(j, meta):
        return jnp.minimum(j, meta[META_NT, 0] - 1)

    return pl.pallas_call(
        functools.partial(_experts_kernel, l=l),
        grid_spec=pltpu.PrefetchScalarGridSpec(
            num_scalar_prefetch=1,
            grid=(MOE_NT,),
            in_specs=[
                pl.BlockSpec((MOE_TM * NCH, 128), lambda j, meta: (tile(j, meta), 0)),
                pl.BlockSpec(memory_space=pl.ANY),
                pl.BlockSpec(memory_space=pl.ANY),
                pl.BlockSpec(memory_space=pl.ANY),
            ],
            out_specs=pl.BlockSpec((MOE_TM * NCH, 128), lambda j, meta: (j, 0)),
            scratch_shapes=[pltpu.VMEM((D, D_EXP), F32), pltpu.VMEM((D, D_EXP), F32), pltpu.VMEM((D_EXP, D), F32),
                            pltpu.VMEM((D, D_EXP), BF16), pltpu.VMEM((D, D_EXP), BF16), pltpu.VMEM((D_EXP, D), BF16),
                            pltpu.SemaphoreType.DMA((3,))],
        ),
        out_shape=jax.ShapeDtypeStruct((MOE_ROWS * NCH, 128), F32),
        compiler_params=_cp(("arbitrary",)),
        name="experts",
    )(meta, xs, w_gate_e, w_up_e, w_down_e)


COMB_TM = SEG


def _combine_kernel(pos_ref, w_ref, x1_ref, mod_ref, fg_ref, ys_hbm, oa_ref, ob_ref, buf, y_s, sem, *, final):
    i = pl.program_id(0)
    n = pl.num_programs(0) - 1
    n_ctx = T_CTX // COMB_TM

    for s in range(2):
        @pl.when((i < n) & (lax.rem(i, 2) == s))
        def _():
            for r in range(COMB_TM):
                for k in range(2):
                    src = pl.ds(pl.multiple_of(pos_ref[k, r] * NCH, NCH), NCH)
                    pltpu.make_async_copy(ys_hbm.at[src, :], buf.at[s, k, pl.ds(r * NCH, NCH), :],
                                          sem.at[s]).start(priority=k)

    for slot in range(2):
        @pl.when((i > 0) & (lax.rem(i - 1, 2) == slot))
        def _():
            for k in range(2):
                pltpu.make_async_copy(ys_hbm.at[pl.ds(0, COMB_TM * NCH), :], buf.at[slot, k], sem.at[slot]).wait()
            w = w_ref[...]
            y = (w[:, 0:1] * _load_row_tiles(buf.at[slot, 0], COMB_TM)
                 + w[:, 1:2] * _load_row_tiles(buf.at[slot, 1], COMB_TM))
            y_s[...] = y

    @pl.when(i > 0)
    def _():
        x = x1_ref[...] + mod_ref[5:6, :] * y_s[...]
        if final:
            ms = jnp.mean(x * x, axis=-1, keepdims=True)
            x = x * lax.rsqrt(ms + EPS) * fg_ref[...]

        @pl.when(i - 1 < n_ctx)
        def _():
            oa_ref[...] = x

        @pl.when(i - 1 >= n_ctx)
        def _():
            ob_ref[...] = x


def _combine(pos, wts_t, x1, modseg, final_g, ys, l, final):
    n = T // COMB_TM
    n_ctx = T_CTX // COMB_TM

    def done(i):
        return jnp.maximum(i - 1, 0)

    return pl.pallas_call(
        functools.partial(_combine_kernel, final=final),
        grid=(n + 1,),
        in_specs=[
            pl.BlockSpec((2, COMB_TM), lambda i: (0, jnp.minimum(i, n - 1)), memory_space=pltpu.SMEM),
            pl.BlockSpec((COMB_TM, 2), lambda i: (done(i), 0)),
            pl.BlockSpec((COMB_TM, D), lambda i: (done(i), 0)),
            pl.BlockSpec((None, None, 8, D), lambda i: (l, done(i), 0, 0)),
            pl.BlockSpec((1, D), lambda i: (0, 0)),
            pl.BlockSpec(memory_space=pl.ANY),
        ],
        out_specs=[pl.BlockSpec((COMB_TM, D), lambda i: (jnp.minimum(done(i), n_ctx - 1), 0)),
                   pl.BlockSpec((COMB_TM, D), lambda i: (jnp.maximum(done(i) - n_ctx, 0), 0))],
        out_shape=[jax.ShapeDtypeStruct((T_CTX, D), F32), jax.ShapeDtypeStruct((T_LAT, D), F32)],
        scratch_shapes=[pltpu.VMEM((2, 2, COMB_TM * NCH, 128), F32), pltpu.VMEM((COMB_TM, D), F32),
                        pltpu.SemaphoreType.DMA((2,))],
        compiler_params=_cp(("arbitrary",)),
        name="combine",
    )(pos, wts_t, x1, modseg, final_g.reshape(1, D), ys)


def _rope_tables():
    n = DEC_SEQ
    pos_row = np.repeat(np.arange(n // GRID_W, dtype=np.float32), GRID_W)
    pos_col = np.tile(np.arange(GRID_W, dtype=np.float32), n // GRID_W)
    half = HD // 2
    inv_freq = jnp.asarray(ROPE_THETA, F32) ** (-jnp.arange(0, half, 2, dtype=F32) / half)
    ang = jnp.concatenate([jnp.asarray(pos_row)[:, None] * inv_freq,
                           jnp.asarray(pos_col)[:, None] * inv_freq], axis=-1)
    cos, sin = jnp.cos(ang), jnp.sin(ang)
    cos128 = jnp.tile(cos, (1, 4))
    sin128 = jnp.tile(jnp.concatenate([-sin, sin], axis=-1), (1, 2))
    return cos128, sin128


def _head_mean_matrix():
    idx = np.arange(2 * HD)
    same = (idx[:, None] // HD) == (idx[None, :] // HD)
    return jnp.asarray(same.astype(np.float32) / HD, BF16)


_SEG_ROWS = np.array([0] * (T_CTX // SEG) + [1 + b for b in range(DEC_BATCH) for _ in range(DEC_SEQ // SEG)])


def kernel(x_prompt, x_sample, cache_k, cache_v, state_rec, c, c_ctx, w_mod, b_mod, norm1_g, norm2_g, w_in, conv_w, conv_b, rg_wa, rg_ba, rg_wx, rg_bx, rg_lambda, q_norm_g, k_norm_g, w_rec_out, w_att_out, w_out, w_router, router_bias, w_gate_e, w_up_e, w_down_e, final_g):
    xa, xb = x_prompt.reshape(T_CTX, D), x_sample.reshape(T_LAT, D)

    cvecs = jnp.concatenate([c_ctx[None, :], c, jnp.zeros((3, D), F32)], axis=0)
    mods = _mods(cvecs, w_mod, b_mod).reshape(DEPTH, 8, 6, D)
    modseg = jnp.pad(mods[:, _SEG_ROWS], ((0, 0), (0, 0), (0, 2), (0, 0)))

    cos128, sin128 = _rope_tables()
    bd = _head_mean_matrix()
    qg128 = jnp.tile(q_norm_g, (1, 2)).reshape(DEPTH, 1, 2 * HD)
    kg128 = jnp.tile(k_norm_g, (1, 2)).reshape(DEPTH, 1, 2 * HD)
    wg = jnp.concatenate([rg_wa[:, 0], rg_wx[:, 0], rg_wa[:, 1], rg_wx[:, 1]], axis=-1)
    pvec = jnp.stack([rg_ba[:, 0], rg_bx[:, 0], rg_ba[:, 1], rg_bx[:, 1],
                      rg_lambda[:, 0], rg_lambda[:, 1], conv_b, jnp.zeros_like(conv_b)], axis=1)
    wrt = w_router.T
    rbias = router_bias.reshape(N_EXP, 1)
    zrow = jnp.zeros((MOE_TM * NCH, 128), F32)

    caches, new_s = (), []
    for l in range(DEPTH):
        proj = _inproj(xa, xb, modseg, norm1_g, w_in, l)
        h0 = jnp.concatenate([jnp.zeros((T_CTX // UNIT, 2, D), F32), state_rec[:, l]], axis=0)
        yrec, stf, stb = _rec(proj, conv_w, pvec, wg, h0, l)
        o_ctx, kc, vc = _attn_ctx(proj, qg128, kg128, bd, l, prev_caches=caches)
        caches = (kc, vc)
        o_lat = _attn_lat(proj, cache_k, cache_v, qg128, kg128, cos128, sin128, bd, l)
        x1, h2, idx, wts = _merge(yrec, o_ctx, o_lat, proj, xa, xb, modseg, norm2_g,
                                  w_rec_out, w_att_out, w_out, wrt, rbias, l)
        pos, meta = _route_pos(idx)
        xs = _dispatch(meta, pos, h2, zrow)
        ys = _experts(meta, xs, w_gate_e, w_up_e, w_down_e, l)
        xa, xb = _combine(pos, wts.T, x1, modseg, final_g, ys, l, final=(l == DEPTH - 1))
        n_cu = T_CTX // UNIT
        spu = UNIT // SEQ
        hf_last = stf[:n_cu].reshape(n_cu, spu, 2, D)[:, :, 1].reshape(BATCH, D)
        hb_first = stb[:n_cu].reshape(n_cu, spu, 2, D)[:, :, 0].reshape(BATCH, D)
        new_s.append(jnp.stack([hf_last, hb_first], axis=1))

    y_prompt = xa.reshape(BATCH, SEQ, D)
    y_sample = xb.reshape(DEC_BATCH, DEC_SEQ, D)
    return (y_prompt, y_sample, caches[0], caches[1], jnp.stack(new_s, axis=1))
```

```python
import functools

import numpy as np
import jax
import jax.numpy as jnp
from jax import lax
from jax.experimental import pallas as pl
from jax.experimental.pallas import tpu as pltpu

F32 = jnp.float32
BF16 = jnp.bfloat16

D = 1024
BATCH = 16
SEQ = 256
DEPTH = 2
DEC_BATCH = 4
DEC_SEQ = 1024
PAST = 256
GRID_W = 64
N_HEADS = 16
N_KV = 4
HD = 64
RG_BLK = 128
RG_C = 8.0
N_EXP = 16
D_EXP = 512
ROPE_THETA = 10000.0
EPS = 1e-6
P_IN = 5632
TINY = float(np.finfo(np.float32).tiny)
NEG_LOG2E = -float(np.log2(np.e))

T_CTX = BATCH * SEQ
T_LAT = DEC_BATCH * DEC_SEQ
T = T_CTX + T_LAT
SEG = 256
UNIT = 1024
N_UNIT = T // UNIT
LANES = 128
SUBLANES = 8
CHUNK = UNIT // SUBLANES
CSTRIDE = CHUNK + SUBLANES

VMEM_LIMIT = 56 * 1024 * 1024


def _cp(sem):
    return pltpu.CompilerParams(dimension_semantics=sem, vmem_limit_bytes=VMEM_LIMIT)


def _split(x):
    hi = x.astype(BF16)
    lo = (x - hi.astype(F32)).astype(BF16)
    return hi, lo


def _sigmoid(x):
    return 0.5 * jnp.tanh(0.5 * x) + 0.5


NCH = D // LANES


def _store_row_tiles(ref, x):
    n = x.shape[0]
    for c in range(NCH):
        ref[pl.ds(c, n, stride=NCH), :] = x[:, c * 128:(c + 1) * 128]


def _load_row_tiles(ref, n):
    return jnp.concatenate([ref[pl.ds(c, n, stride=NCH), :] for c in range(NCH)], axis=-1)


def _dot(a, b):
    return jnp.dot(a, b, preferred_element_type=F32)


def _dot_nt(a, b):
    return lax.dot_general(a, b, (((1,), (1,)), ((), ())), preferred_element_type=F32)


def _mods_kernel(c_ref, w_ref, b_ref, o_ref):
    c = c_ref[...]
    s = c * jax.nn.sigmoid(c)
    s_hi, s_lo = _split(s)
    w_hi, w_lo = _split(w_ref[...])
    o_ref[...] = _dot(s_hi, w_hi) + _dot(s_hi, w_lo) + _dot(s_lo, w_hi) + b_ref[...]


def _mods(cvecs, w_mod, b_mod):
    tn = 1536
    return pl.pallas_call(
        _mods_kernel,
        grid=(DEPTH, 6 * D // tn),
        in_specs=[
            pl.BlockSpec((8, D), lambda l, j: (0, 0)),
            pl.BlockSpec((None, D, tn), lambda l, j: (l, 0, j)),
            pl.BlockSpec((None, 1, tn), lambda l, j: (l, 0, j)),
        ],
        out_specs=pl.BlockSpec((None, 8, tn), lambda l, j: (l, 0, j)),
        out_shape=jax.ShapeDtypeStruct((DEPTH, 8, 6 * D), F32),
        compiler_params=_cp(("arbitrary", "arbitrary")),
        name="mods",
    )(cvecs, w_mod, b_mod.reshape(DEPTH, 1, 6 * D))


def _norm_mod(x, g, shift, scale):
    ms = jnp.mean(x * x, axis=-1, keepdims=True)
    return x * lax.rsqrt(ms + EPS) * g * (1.0 + scale) + shift


def _two_part_specs(tm, n_ctx):
    return [pl.BlockSpec((tm, D), lambda i, *_: (jnp.minimum(i, n_ctx - 1), 0)),
            pl.BlockSpec((tm, D), lambda i, *_: (jnp.maximum(i - n_ctx, 0), 0))]


def _inproj_kernel(xa_ref, xb_ref, mod_ref, g_ref, w_ref, o_ref, h_ref, *, tm):
    def prologue(x_ref):
        def seg(s, carry):
            r0 = pl.multiple_of(s * SEG, SEG)
            m = mod_ref[s]
            h = _norm_mod(x_ref[pl.ds(r0, SEG), :], g_ref[...], m[0:1, :], m[1:2, :])
            h_ref[pl.ds(r0, SEG), :] = h.astype(BF16)
            return carry
        lax.fori_loop(0, tm // SEG, seg, 0)

    first = pl.program_id(1) == 0
    is_ctx = pl.program_id(0) < T_CTX // tm

    @pl.when(first & is_ctx)
    def _():
        prologue(xa_ref)

    @pl.when(first & jnp.logical_not(is_ctx))
    def _():
        prologue(xb_ref)

    o_ref[...] = _dot(h_ref[...], w_ref[...].astype(BF16)).astype(BF16)


def _inproj(xa, xb, modseg, norm_g, w_in, l):
    tm, tn = 2048, 512
    return pl.pallas_call(
        functools.partial(_inproj_kernel, tm=tm),
        grid=(T // tm, P_IN // tn),
        in_specs=_two_part_specs(tm, T_CTX // tm) + [
            pl.BlockSpec((None, tm // SEG, 8, D), lambda i, j: (l, i, 0, 0)),
            pl.BlockSpec((None, 1, D), lambda i, j: (l, 0, 0)),
            pl.BlockSpec((None, D, tn), lambda i, j: (l, 0, j)),
        ],
        out_specs=pl.BlockSpec((tm, tn), lambda i, j: (i, j)),
        out_shape=jax.ShapeDtypeStruct((T, P_IN), BF16),
        scratch_shapes=[pltpu.VMEM((tm, D), BF16)],
        compiler_params=_cp(("arbitrary", "arbitrary")),
        name="inproj",
    )(xa, xb, modseg, norm_g.reshape(DEPTH, 1, D), w_in)


REC_CW = 512
PAD_F = 16
PAD_B = 8
GATE_ROWS = 512


def _rec_kernel(xr_ref, gate_ref, cw_ref, pv_ref, wg_ref, h0_ref,
                y_ref, stf_ref, stb_ref,
                xs_ref, af_ref, bf_ref, ab_ref, bb_ref, nat_ref, wgh_ref):
    u = pl.program_id(0)
    is_ctx = u < (T_CTX // UNIT)
    cps = jnp.where(is_ctx, SEQ // CHUNK, DEC_SEQ // CHUNK)
    nblk = REC_CW // RG_BLK

    def lanes(n):
        return slice(n * RG_BLK, (n + 1) * RG_BLK)

    def tile(r):
        return slice(8 * r, 8 * r + 8)

    for c in range(8):
        for n in range(nblk):
            nat_ref[n, c * CSTRIDE:c * CSTRIDE + CHUNK, :] = xr_ref[c * CHUNK:(c + 1) * CHUNK, lanes(n)].astype(F32)
    for r in range(CHUNK):
        for n in range(nblk):
            xs_ref[PAD_F + 8 * r:PAD_F + 8 * r + 8, lanes(n)] = nat_ref[n, pl.ds(r, 8, stride=CSTRIDE), :]
    chunk_id = lax.broadcasted_iota(jnp.int32, (8, 1), 0)
    seq_start = jnp.bitwise_and(chunk_id, cps - 1) == 0
    seq_end = jnp.bitwise_and(chunk_id, cps - 1) == cps - 1
    for j, r in ((0, CHUNK - 2), (1, CHUNK - 1)):
        prev_chunk = pltpu.roll(xs_ref[PAD_F + 8 * r:PAD_F + 8 * r + 8, :], 1, 0)
        xs_ref[tile(j), :] = jnp.where(seq_start, 0.0, prev_chunk)
    next_chunk = pltpu.roll(xs_ref[PAD_F:PAD_F + 8, :], 7, 0)
    xs_ref[PAD_F + UNIT:PAD_F + UNIT + PAD_B, :] = jnp.where(seq_end, 0.0, next_chunk)

    pv = pv_ref[...]
    cwts = cw_ref[...]
    conv_b = pv[6:7, :]

    def softplus_neg(lam):
        z = -lam
        return jnp.maximum(z, 0.0) + jnp.log1p(jnp.exp(-jnp.abs(z)))

    c4s = tuple((0.5 * RG_C) * softplus_neg(pv[4 + d:5 + d, :]) for d in range(2))
    pv_h = 0.5 * pv
    for n in range(nblk):
        wgh_ref[n] = (0.5 * wg_ref[n]).astype(BF16)
    a_refs = (af_ref, ab_ref)
    b_refs = (bf_ref, bb_ref)

    def gates(g, carry):
        base = pl.multiple_of(g * GATE_ROWS, GATE_ROWS)

        def tap(d):
            return xs_ref[pl.ds(pl.multiple_of(base + PAD_F + 8 * d, 8), GATE_ROWS), :]

        xc = conv_b + tap(-2) * cwts[0:1, :]
        xc = xc + tap(-1) * cwts[1:2, :]
        xc = xc + tap(0) * cwts[2:3, :]
        xc = xc + tap(1) * cwts[3:4, :]
        for n in range(nblk):
            ls = lanes(n)
            xn = xc[:, ls]
            hx = 0.5 * xn
            pre_h = _dot(xn.astype(BF16), wgh_ref[n])
            for d in range(2):
                th_r = jnp.tanh(pre_h[:, (2 * d) * RG_BLK:(2 * d + 1) * RG_BLK] + pv_h[2 * d:2 * d + 1, ls])
                th_i = jnp.tanh(pre_h[:, (2 * d + 1) * RG_BLK:(2 * d + 2) * RG_BLK] + pv_h[2 * d + 1:2 * d + 2, ls])
                c4 = c4s[d][:, ls]
                nla = c4 * th_r + c4
                a = jnp.exp2(nla * NEG_LOG2E)
                s = jnp.tanh(nla) * (a * a + 1.0)
                inp = (s * lax.rsqrt(jnp.maximum(s, TINY))) * (hx * th_i + hx)
                a_refs[d][pl.ds(base, GATE_ROWS), ls] = a
                b_refs[d][pl.ds(base, GATE_ROWS), ls] = inp
        return carry

    lax.fori_loop(0, UNIT // GATE_ROWS, gates, 0)

    hf = hb = jnp.zeros((8, REC_CW), F32)
    pf = pb = jnp.ones((8, REC_CW), F32)
    for r in range(CHUNK):
        rf, rb = tile(r), tile(CHUNK - 1 - r)
        a = af_ref[rf, :]
        hf = a * hf + bf_ref[rf, :]
        pf = a * pf
        bf_ref[rf, :] = hf
        af_ref[rf, :] = pf
        a = ab_ref[rb, :]
        hb = a * hb + bb_ref[rb, :]
        pb = a * pb
        bb_ref[rb, :] = hb
        ab_ref[rb, :] = pb

    h0f = h0_ref[0:1, :]
    h0b = h0_ref[1:2, :]
    cf = [h0f]
    for c in range(1, 8):
        chain = hf[c - 1:c, :] + pf[c - 1:c, :] * cf[c - 1]
        cf.append(jnp.where(jnp.bitwise_and(c, cps - 1) == 0, h0f, chain))
    cb = [None] * 8
    cb[7] = h0b
    for c in range(6, -1, -1):
        chain = hb[c + 1:c + 2, :] + pb[c + 1:c + 2, :] * cb[c + 1]
        cb[c] = jnp.where(jnp.bitwise_and(c, cps - 1) == cps - 1, h0b, chain)
    carry_f = jnp.concatenate(cf, axis=0)
    carry_b = jnp.concatenate(cb, axis=0)
    stf_ref[...] = hf + pf * carry_f
    stb_ref[...] = hb + pb * carry_b

    for r in range(CHUNK):
        h = (bf_ref[tile(r), :] + af_ref[tile(r), :] * carry_f) + (bb_ref[tile(r), :] + ab_ref[tile(r), :] * carry_b)
        for n in range(nblk):
            nat_ref[n, pl.ds(r, 8, stride=CSTRIDE), :] = h[:, lanes(n)]

    for c in range(8):
        rows = slice(c * CHUNK, (c + 1) * CHUNK)
        for n in range(nblk):
            g = gate_ref[rows, lanes(n)].astype(F32)
            h = nat_ref[n, c * CSTRIDE:c * CSTRIDE + CHUNK, :]
            y_ref[rows, lanes(n)] = (h * jax.nn.gelu(g, approximate=True)).astype(BF16)


def _rec(proj, conv_w, pvec, wg, h0, l):
    ncb = D // REC_CW
    return pl.pallas_call(
        _rec_kernel,
        grid=(N_UNIT, ncb),
        in_specs=[
            pl.BlockSpec((UNIT, REC_CW), lambda u, c: (u, c)),
            pl.BlockSpec((UNIT, REC_CW), lambda u, c: (u, ncb + c)),
            pl.BlockSpec((None, 4, REC_CW), lambda u, c: (l, 0, c)),
            pl.BlockSpec((None, 8, REC_CW), lambda u, c: (l, 0, c)),
            pl.BlockSpec((None, REC_CW // RG_BLK, RG_BLK, 4 * RG_BLK), lambda u, c: (l, c, 0, 0)),
            pl.BlockSpec((None, 2, REC_CW), lambda u, c: (u, 0, c)),
        ],
        out_specs=[
            pl.BlockSpec((UNIT, REC_CW), lambda u, c: (u, c)),
            pl.BlockSpec((None, 8, REC_CW), lambda u, c: (u, 0, c)),
            pl.BlockSpec((None, 8, REC_CW), lambda u, c: (u, 0, c)),
        ],
        out_shape=[
            jax.ShapeDtypeStruct((T, D), BF16),
            jax.ShapeDtypeStruct((N_UNIT, 8, D), F32),
            jax.ShapeDtypeStruct((N_UNIT, 8, D), F32),
        ],
        scratch_shapes=[pltpu.VMEM((PAD_F + UNIT + PAD_B, REC_CW), F32)]
        + [pltpu.VMEM((UNIT, REC_CW), F32)] * 4
        + [pltpu.VMEM((REC_CW // RG_BLK, 8 * CSTRIDE, RG_BLK), F32),
           pltpu.VMEM((REC_CW // RG_BLK, RG_BLK, 4 * RG_BLK), BF16)],
        compiler_params=_cp(("arbitrary", "arbitrary")),
        name="rec",
    )(proj, proj, conv_w, pvec, wg, h0)


def _head_norm(x, g128, bd):
    hi, lo = _split(x * x)
    ms = _dot(hi, bd) + _dot(lo, bd)
    return x * lax.rsqrt(ms + EPS) * g128


def _rope(x, cos, sin_signed):
    lane = lax.broadcasted_iota(jnp.int32, x.shape, 1)
    first_half = jnp.bitwise_and(lane, HD - 1) < HD // 2
    partner = jnp.where(first_half, pltpu.roll(x, 2 * HD - HD // 2, 1), pltpu.roll(x, HD // 2, 1))
    return x * cos + partner * sin_signed


def _with_ones(v):
    return jnp.concatenate([v, jnp.ones_like(v)], axis=-1)


def _softmax_pv(q, k, v_ext):
    s = _dot_nt(q, k)
    m = jnp.max(s, axis=-1, keepdims=True)
    p = jnp.exp2(s - m).astype(BF16)
    r = _dot(p, v_ext)
    return r[:, :HD] / r[:, HD:HD + 1]


def _attend_heads(q_ref, k, v_ext):
    return jnp.concatenate([_softmax_pv(q_ref[h], k, v_ext) for h in range(N_HEADS // N_KV)], axis=-1)


def _attn_ctx_kernel(q_ref, k_ref, v_ref, qg_ref, kg_ref, bd_ref, *rest, slab):
    o_ref, ko_ref, vo_ref = rest[-3:]
    for p in range(slab):
        ko_ref[p] = rest[0][p]
        vo_ref[p] = rest[1][p]
    g = N_HEADS // N_KV
    bd = bd_ref[...]
    odd = lax.rem(pl.program_id(1), 2) == 1
    scale = HD ** -0.5 * float(np.log2(np.e))

    kx = _head_norm(k_ref[...].astype(F32), kg_ref[...], bd)
    vx = v_ref[...].astype(F32)
    k_new = jnp.where(odd, kx[:, HD:], kx[:, :HD])
    v_new = jnp.where(odd, vx[:, HD:], vx[:, :HD])
    ko_ref[slab] = k_new
    vo_ref[slab] = v_new
    heads = []
    for j in range(g // 2):
        x = _head_norm(q_ref[:, 2 * HD * j:2 * HD * (j + 1)].astype(F32), qg_ref[...], bd) * scale
        heads += [x[:, :HD].astype(BF16), x[:, HD:].astype(BF16)]
    o = _softmax_pv(jnp.concatenate(heads, axis=0), k_new.astype(BF16), _with_ones(v_new.astype(BF16)))
    o_ref[...] = jnp.concatenate([o[h * SEQ:(h + 1) * SEQ] for h in range(g)], axis=-1).astype(BF16)


def _attn_ctx(proj, qg128, kg128, bd, l, prev_caches=()):
    g = N_HEADS // N_KV
    qcol = 2 * D // (g * HD)
    kcol = 3 * D // (2 * HD)
    vcol = kcol + N_KV // 2
    cache_spec = pl.BlockSpec((None, l + 1, None, SEQ, HD), lambda b, h: (b, 0, h, 0, 0))
    prev_specs = [pl.BlockSpec((None, l, None, SEQ, HD), lambda b, h: (b, 0, h, 0, 0))] * 2 if l else []
    cache_shape = jax.ShapeDtypeStruct((BATCH, l + 1, N_KV, SEQ, HD), F32)
    return pl.pallas_call(
        functools.partial(_attn_ctx_kernel, slab=l),
        grid=(BATCH, N_KV),
        in_specs=[
            pl.BlockSpec((SEQ, g * HD), lambda b, h: (b, qcol + h)),
            pl.BlockSpec((SEQ, 2 * HD), lambda b, h: (b, kcol + h // 2)),
            pl.BlockSpec((SEQ, 2 * HD), lambda b, h: (b, vcol + h // 2)),
            pl.BlockSpec((None, 1, 2 * HD), lambda b, h: (l, 0, 0)),
            pl.BlockSpec((None, 1, 2 * HD), lambda b, h: (l, 0, 0)),
            pl.BlockSpec((2 * HD, 2 * HD), lambda b, h: (0, 0)),
        ] + prev_specs,
        out_specs=[pl.BlockSpec((SEQ, g * HD), lambda b, h: (b, h)), cache_spec, cache_spec],
        out_shape=[jax.ShapeDtypeStruct((T_CTX, D), BF16), cache_shape, cache_shape],
        compiler_params=_cp(("arbitrary", "arbitrary")),
        name="attn_ctx",
    )(proj, proj, proj, qg128, kg128, bd, *prev_caches)


def _attn_lat_kernel(q_ref, k_ref, v_ref, qg_ref, kg_ref, cos_ref, sin_ref, bd_ref, pk_ref, pv_ref,
                     o_ref, q_s, k_s, v_s):
    g = N_HEADS // N_KV
    bd, cos, sin = bd_ref[...], cos_ref[...], sin_ref[...]
    odd = lax.rem(pl.program_id(1), 2) == 1
    scale = HD ** -0.5 * float(np.log2(np.e))

    kx = _rope(_head_norm(k_ref[...].astype(F32), kg_ref[...], bd), cos, sin)
    vx = v_ref[...]
    k_s[0:PAST, :] = pk_ref[...].astype(BF16)
    k_s[PAST:, :] = jnp.where(odd, kx[:, HD:], kx[:, :HD]).astype(BF16)
    v_s[0:PAST, :] = _with_ones(pv_ref[...].astype(BF16))
    v_s[PAST:, :] = _with_ones(jnp.where(odd, vx[:, HD:], vx[:, :HD]))
    for j in range(g // 2):
        x = _head_norm(q_ref[:, 2 * HD * j:2 * HD * (j + 1)].astype(F32), qg_ref[...], bd)
        x = _rope(x, cos, sin) * scale
        q_s[2 * j] = x[:, :HD].astype(BF16)
        q_s[2 * j + 1] = x[:, HD:].astype(BF16)

    o_ref[...] = _attend_heads(q_s, k_s[...], v_s[...]).astype(BF16)


def _attn_lat(proj, cache_k, cache_v, qg128, kg128, cos128, sin128, bd, l):
    g = N_HEADS // N_KV
    row0 = T_CTX // DEC_SEQ
    qcol = 2 * D // (g * HD)
    kcol = 3 * D // (2 * HD)
    vcol = kcol + N_KV // 2
    return pl.pallas_call(
        _attn_lat_kernel,
        grid=(DEC_BATCH, N_KV),
        in_specs=[
            pl.BlockSpec((DEC_SEQ, g * HD), lambda b, h: (row0 + b, qcol + h)),
            pl.BlockSpec((DEC_SEQ, 2 * HD), lambda b, h: (row0 + b, kcol + h // 2)),
            pl.BlockSpec((DEC_SEQ, 2 * HD), lambda b, h: (row0 + b, vcol + h // 2)),
            pl.BlockSpec((None, 1, 2 * HD), lambda b, h: (l, 0, 0)),
            pl.BlockSpec((None, 1, 2 * HD), lambda b, h: (l, 0, 0)),
            pl.BlockSpec((DEC_SEQ, 2 * HD), lambda b, h: (0, 0)),
            pl.BlockSpec((DEC_SEQ, 2 * HD), lambda b, h: (0, 0)),
            pl.BlockSpec((2 * HD, 2 * HD), lambda b, h: (0, 0)),
            pl.BlockSpec((None, None, None, PAST, HD), lambda b, h: (b, l, h, 0, 0)),
            pl.BlockSpec((None, None, None, PAST, HD), lambda b, h: (b, l, h, 0, 0)),
        ],
        out_specs=pl.BlockSpec((DEC_SEQ, g * HD), lambda b, h: (b, h)),
        out_shape=jax.ShapeDtypeStruct((T_LAT, D), BF16),
        scratch_shapes=[pltpu.VMEM((g, DEC_SEQ, HD), BF16), pltpu.VMEM((PAST + DEC_SEQ, HD), BF16),
                        pltpu.VMEM((PAST + DEC_SEQ, 2 * HD), BF16)],
        compiler_params=_cp(("arbitrary", "arbitrary")),
        name="attn_lat",
    )(proj, proj, proj, qg128, kg128, cos128, sin128, bd, cache_k, cache_v)


MERGE_TM = 512


def _route(lt, bias):
    rows = [lt[e:e + 1, :] for e in range(N_EXP)]
    m = rows[0]
    for e in range(1, N_EXP):
        m = jnp.maximum(m, rows[e])
    ex = [jnp.exp(r - m) for r in rows]
    z = ex[0]
    for e in range(1, N_EXP):
        z = z + ex[e]
    probs = [x / z for x in ex]
    sel = [probs[e] + bias[e:e + 1, :] for e in range(N_EXP)]

    def top2_sum(v):
        a, b = jnp.maximum(v[0], v[1]), jnp.minimum(v[0], v[1])
        c, d = jnp.maximum(v[2], v[3]), jnp.minimum(v[2], v[3])
        return jnp.maximum(a, c) + jnp.maximum(jnp.minimum(a, c), jnp.maximum(b, d))

    scores = [top2_sum(sel[4 * g:4 * g + 4]) for g in range(4)]
    best = jnp.zeros_like(scores[0], dtype=jnp.int32)
    best_s = scores[0]
    for g in range(1, 4):
        take = scores[g] > best_s
        best = jnp.where(take, g, best)
        best_s = jnp.where(take, scores[g], best_s)
    cs, cp = [], []
    for j in range(4):
        s_j, p_j = sel[j], probs[j]
        for g in range(1, 4):
            s_j = jnp.where(best == g, sel[4 * g + j], s_j)
            p_j = jnp.where(best == g, probs[4 * g + j], p_j)
        cs.append(s_j)
        cp.append(p_j)
    neg = jnp.full_like(cs[0], -jnp.inf)

    def argmax4(v):
        bi = jnp.zeros_like(best)
        bv = v[0]
        for j in range(1, 4):
            take = v[j] > bv
            bi = jnp.where(take, j, bi)
            bv = jnp.where(take, v[j], bv)
        return bi

    def pick(v, idx):
        out = v[0]
        for j in range(1, 4):
            out = jnp.where(idx == j, v[j], out)
        return out

    i1 = argmax4(cs)
    cs2 = [jnp.where(i1 == j, neg, cs[j]) for j in range(4)]
    i2 = argmax4(cs2)
    i2 = jnp.where((i2 == 0) & (i1 == 0), 1, i2)
    w1, w2 = pick(cp, i1), pick(cp, i2)
    den = w1 + w2
    return best * 4 + i1, best * 4 + i2, w1 / den, w2 / den


def _merge_kernel(yrec_ref, oa_ref, ob_ref, gr0_ref, gr1_ref, ga0_ref, ga1_ref, xa_ref, xb_ref, mod_ref, g2_ref,
                  wrec_ref, watt_ref, wout_ref, wrt_ref, rb_ref,
                  x1_ref, h2_ref, idx_ref, wts_ref,
                  wrec_s, watt_s, wout_s):
    @pl.when(pl.program_id(0) == 0)
    def _():
        wrec_s[...] = wrec_ref[...].astype(BF16)
        watt_s[...] = watt_ref[...].astype(BF16)
        wout_s[...] = wout_ref[...].astype(BF16)

    is_ctx = pl.program_id(0) < T_CTX // MERGE_TM
    args = (yrec_ref, gr0_ref, gr1_ref, ga0_ref, ga1_ref, mod_ref, g2_ref, wrt_ref, rb_ref,
            x1_ref, h2_ref, idx_ref, wts_ref, wrec_s, watt_s, wout_s)

    @pl.when(is_ctx)
    def _():
        _merge_body(oa_ref, xa_ref, *args)

    @pl.when(jnp.logical_not(is_ctx))
    def _():
        _merge_body(ob_ref, xb_ref, *args)


def _merge_body(oatt_ref, x_ref, yrec_ref, gr0_ref, gr1_ref, ga0_ref, ga1_ref, mod_ref, g2_ref, wrt_ref, rb_ref,
                x1_ref, h2_ref, idx_ref, wts_ref, wrec_s, watt_s, wout_s):
    half = D // 2
    b_rec = _dot(yrec_ref[...], wrec_s[...])
    b_att = _dot(oatt_ref[...], watt_s[...])
    m0 = _sigmoid(gr0_ref[...].astype(F32)) * b_rec[:, :half] + _sigmoid(ga0_ref[...].astype(F32)) * b_att[:, :half]
    m1 = _sigmoid(gr1_ref[...].astype(F32)) * b_rec[:, half:] + _sigmoid(ga1_ref[...].astype(F32)) * b_att[:, half:]
    merged = jnp.concatenate([m0, m1], axis=-1).astype(BF16)
    out = _dot(merged, wout_s[...])

    hs = []
    for s in range(MERGE_TM // SEG):
        rows = slice(s * SEG, (s + 1) * SEG)
        m = mod_ref[s]
        x1 = x_ref[rows, :] + m[2:3, :] * out[rows, :]
        x1_ref[rows, :] = x1
        h2 = _norm_mod(x1, g2_ref[...], m[3:4, :], m[4:5, :])
        hs.append(h2)
    h2 = jnp.concatenate(hs, axis=0)
    _store_row_tiles(h2_ref, h2)

    h_hi, h_lo = _split(h2)
    w_hi, w_lo = _split(wrt_ref[...])
    lt = _dot_nt(w_hi, h_hi) + _dot_nt(w_hi, h_lo) + _dot_nt(w_lo, h_hi)
    e1, e2, w1, w2 = _route(lt, rb_ref[...])
    idx_ref[...] = jnp.concatenate([e1, e2], axis=0)
    wts_ref[...] = jnp.concatenate([w1, w2], axis=0)


def _merge(yrec, o_ctx, o_lat, proj, xa, xb, modseg, norm2_g, w_rec_out, w_att_out, w_out, wrt, rbias, l):
    tm = MERGE_TM
    half = D // 2
    gcol = (3 * D + 2 * N_KV * HD) // half
    wspec = pl.BlockSpec((None, D, D), lambda i: (l, 0, 0))
    return pl.pallas_call(
        _merge_kernel,
        grid=(T // tm,),
        in_specs=[pl.BlockSpec((tm, D), lambda i: (i, 0))] + _two_part_specs(tm, T_CTX // tm) + [
            pl.BlockSpec((tm, half), lambda i: (i, gcol)),
            pl.BlockSpec((tm, half), lambda i: (i, gcol + 1)),
            pl.BlockSpec((tm, half), lambda i: (i, gcol + 2)),
            pl.BlockSpec((tm, half), lambda i: (i, gcol + 3)),
        ] + _two_part_specs(tm, T_CTX // tm) + [
            pl.BlockSpec((None, tm // SEG, 8, D), lambda i: (l, i, 0, 0)),
            pl.BlockSpec((None, 1, D), lambda i: (l, 0, 0)),
            wspec, wspec, wspec,
            pl.BlockSpec((N_EXP, D), lambda i: (0, 0)),
            pl.BlockSpec((N_EXP, 1), lambda i: (0, 0)),
        ],
        out_specs=[
            pl.BlockSpec((tm, D), lambda i: (i, 0)),
            pl.BlockSpec((tm * NCH, 128), lambda i: (i, 0)),
            pl.BlockSpec((2, tm), lambda i: (0, i)),
            pl.BlockSpec((2, tm), lambda i: (0, i)),
        ],
        out_shape=[
            jax.ShapeDtypeStruct((T, D), F32),
            jax.ShapeDtypeStruct((T * NCH, 128), F32),
            jax.ShapeDtypeStruct((2, T), jnp.int32),
            jax.ShapeDtypeStruct((2, T), F32),
        ],
        scratch_shapes=[pltpu.VMEM((D, D), BF16)] * 3,
        compiler_params=_cp(("arbitrary",)),
        name="merge",
    )(yrec, o_ctx, o_lat, proj, proj, proj, proj, xa, xb, modseg, norm2_g.reshape(DEPTH, 1, D),
      w_rec_out, w_att_out, w_out, wrt, rbias)


MOE_TM = 512
MOE_NT = 2 * T // MOE_TM + N_EXP
MOE_ROWS = MOE_NT * MOE_TM
META_TILE_E, META_CNT, META_OFF, META_END, META_NT, META_NEXT_E = 0, 1, 2, 3, 4, 5


def _pos_kernel(idx_ref, pos_ref, meta_ref):
    shift = MOE_TM.bit_length() - 1
    idx = idx_ref[...]
    eid = lax.broadcasted_iota(jnp.int32, (N_EXP, T), 0)
    m0 = eid == idx[0:1, :]
    m1 = eid == idx[1:2, :]
    member = jnp.where(m0 | m1, 1.0, 0.0)
    cnt = jnp.sum(member, axis=1, keepdims=True).astype(jnp.int32)
    ntile = jnp.right_shift(cnt + (MOE_TM - 1), shift)
    offs, acc = [], jnp.zeros((1, 1), jnp.int32)
    for e in range(N_EXP):
        offs.append(acc)
        acc = acc + ntile[e:e + 1, :]
    off_t = jnp.concatenate(offs, axis=0)
    end_t = off_t + ntile

    blk = 256
    r_i = lax.broadcasted_iota(jnp.int32, (blk, blk), 0)
    c_i = lax.broadcasted_iota(jnp.int32, (blk, blk), 1)
    upper = jnp.where(r_i <= c_i, 1.0, 0.0).astype(BF16)
    run = (off_t * MOE_TM).astype(F32)
    for j in range(T // blk):
        ls = slice(j * blk, (j + 1) * blk)
        mb = member[:, ls]
        inc = _dot(mb.astype(BF16), upper)
        dest = run + inc - mb
        pos_ref[0:1, ls] = jnp.sum(jnp.where(m0[:, ls], dest, 0.0), axis=0, keepdims=True).astype(jnp.int32)
        pos_ref[1:2, ls] = jnp.sum(jnp.where(m1[:, ls], dest, 0.0), axis=0, keepdims=True).astype(jnp.int32)
        run = run + inc[:, blk - 1:blk]

    lane = lax.broadcasted_iota(jnp.int32, (1, 128), 1)
    zero = jnp.zeros((1, 128), jnp.int32)
    tile_e, cnt_row, off_row, end_row = zero, zero, zero, zero
    for e in range(N_EXP):
        tile_e = tile_e + jnp.where(lane >= end_t[e:e + 1, :], 1, 0)
        here = lane == e
        cnt_row = jnp.where(here, cnt[e:e + 1, :], cnt_row)
        off_row = jnp.where(here, off_t[e:e + 1, :] * MOE_TM, off_row)
        end_row = jnp.where(here, end_t[e:e + 1, :] * MOE_TM, end_row)
    tile_e = jnp.minimum(tile_e, N_EXP - 1)
    nt_row = zero + acc
    next_row = zero
    nxt = jnp.full((1, 1), -1, jnp.int32)
    for e in reversed(range(N_EXP)):
        next_row = jnp.where(lane == e, nxt, next_row)
        nxt = jnp.where(cnt[e:e + 1, :] > 0, e, nxt)
    meta_ref[...] = jnp.concatenate([tile_e, cnt_row, off_row, end_row, nt_row, next_row, zero, zero], axis=0)


def _route_pos(idx):
    return pl.pallas_call(
        _pos_kernel,
        grid=(1,),
        in_specs=[pl.BlockSpec((2, T), lambda i: (0, 0))],
        out_specs=[pl.BlockSpec((2, T), lambda i: (0, 0)), pl.BlockSpec((8, 128), lambda i: (0, 0))],
        out_shape=[jax.ShapeDtypeStruct((2, T), jnp.int32), jax.ShapeDtypeStruct((8, 128), jnp.int32)],
        compiler_params=_cp(("arbitrary",)),
        name="route_pos",
    )(idx)


DISP_TM = 256


def _dispatch_kernel(meta_ref, pos_ref, h_ref, z_hbm, xs_hbm, sem):
    i = pl.program_id(0)

    def row_copy(src, src_row, dst_row):
        return pltpu.make_async_copy(src.at[pl.ds(src_row * NCH, NCH), :],
                                     xs_hbm.at[pl.ds(pl.multiple_of(dst_row * NCH, NCH), NCH), :], sem)

    for r in range(DISP_TM):
        row_copy(h_ref, r, pos_ref[0, r]).start(priority=0)
        row_copy(h_ref, r, pos_ref[1, r]).start(priority=1)

    e = jnp.minimum(i, N_EXP - 1)
    pad0 = meta_ref[META_OFF, e] + meta_ref[META_CNT, e]
    npad = jnp.where(i < N_EXP, meta_ref[META_END, e] - pad0, 0)

    def pad_copies(act):
        s = pad0
        for bit in reversed(range(MOE_TM.bit_length() - 1)):
            size = 1 << bit
            part = jnp.bitwise_and(npad, size)

            @pl.when(part != 0)
            def _():
                dst = pl.ds(pl.multiple_of(s * NCH, NCH), size * NCH)
                act(pltpu.make_async_copy(z_hbm.at[pl.ds(0, size * NCH), :], xs_hbm.at[dst, :], sem))

            s = s + part

    pad_copies(lambda c: c.start())

    for _ in range(2):
        pltpu.make_async_copy(h_ref, xs_hbm.at[pl.ds(0, DISP_TM * NCH), :], sem).wait()

    tail = meta_ref[META_NT, 0] + i
    has_tail = (i < N_EXP) & (tail < MOE_NT)

    def tail_copy():
        rows = pl.ds(pl.multiple_of(tail * (MOE_TM * NCH), MOE_TM * NCH), MOE_TM * NCH)
        return pltpu.make_async_copy(z_hbm, xs_hbm.at[rows, :], sem)

    @pl.when(has_tail)
    def _():
        tail_copy().start()

    pad_copies(lambda c: c.wait())

    @pl.when(has_tail)
    def _():
        tail_copy().wait()


def _dispatch(meta, pos, h2, zrow):
    return pl.pallas_call(
        _dispatch_kernel,
        grid_spec=pltpu.PrefetchScalarGridSpec(
            num_scalar_prefetch=1,
            grid=(T // DISP_TM,),
            in_specs=[
                pl.BlockSpec((2, DISP_TM), lambda i, meta: (0, i), memory_space=pltpu.SMEM),
                pl.BlockSpec((DISP_TM * NCH, 128), lambda i, meta: (i, 0)),
                pl.BlockSpec((MOE_TM * NCH, 128), lambda i, meta: (0, 0)),
            ],
            out_specs=pl.BlockSpec(memory_space=pl.ANY),
            scratch_shapes=[pltpu.SemaphoreType.DMA],
        ),
        out_shape=jax.ShapeDtypeStruct((MOE_ROWS * NCH, 128), F32),
        compiler_params=_cp(("arbitrary",)),
        name="dispatch",
    )(meta, pos, h2, zrow)


def _experts_kernel(meta_ref, xs_ref, wg_hbm, wu_hbm, wd_hbm, ys_ref,
                    wg_f, wu_f, wd_f, wg_s, wu_s, wd_s, sem, *, l):
    j = pl.program_id(0)
    live = j < meta_ref[META_NT, 0]
    e = meta_ref[META_TILE_E, j]
    e_prev = meta_ref[META_TILE_E, jnp.maximum(j - 1, 0)]

    def fetch(ex):
        return (pltpu.make_async_copy(wg_hbm.at[l, ex], wg_f, sem.at[0]),
                pltpu.make_async_copy(wu_hbm.at[l, ex], wu_f, sem.at[1]),
                pltpu.make_async_copy(wd_hbm.at[l, ex], wd_f, sem.at[2]))

    @pl.when(j == 0)
    def _():
        for c in fetch(e):
            c.start()

    @pl.when(live & ((j == 0) | (e != e_prev)))
    def _():
        for c, dst, src in zip(fetch(e), (wg_s, wu_s, wd_s), (wg_f, wu_f, wd_f)):
            c.wait()
            dst[...] = src[...].astype(BF16)
        nxt = meta_ref[META_NEXT_E, e]

        @pl.when(nxt >= 0)
        def _():
            for c in fetch(nxt):
                c.start()

    @pl.when(live)
    def _():
        x = _load_row_tiles(xs_ref, MOE_TM).astype(BF16)
        g = _dot(x, wg_s[...])
        u = _dot(x, wu_s[...])
        act = (g * _sigmoid(g)) * u
        _store_row_tiles(ys_ref, _dot(act.astype(BF16), wd_s[...]))

    @pl.when(jnp.logical_not(live))
    def _():
        ys_ref[...] = jnp.zeros_like(ys_ref)


def _experts(meta, xs, w_gate_e, w_up_e, w_down_e, l):
    def tile(j, meta):
        return jnp.minimum(j, meta[META_NT, 0] - 1)

    return pl.pallas_call(
        functools.partial(_experts_kernel, l=l),
        grid_spec=pltpu.PrefetchScalarGridSpec(
            num_scalar_prefetch=1,
            grid=(MOE_NT,),
            in_specs=[
                pl.BlockSpec((MOE_TM * NCH, 128), lambda j, meta: (tile(j, meta), 0)),
                pl.BlockSpec(memory_space=pl.ANY),
                pl.BlockSpec(memory_space=pl.ANY),
                pl.BlockSpec(memory_space=pl.ANY),
            ],
            out_specs=pl.BlockSpec((MOE_TM * NCH, 128), lambda j, meta: (j, 0)),
            scratch_shapes=[pltpu.VMEM((D, D_EXP), F32), pltpu.VMEM((D, D_EXP), F32), pltpu.VMEM((D_EXP, D), F32),
                            pltpu.VMEM((D, D_EXP), BF16), pltpu.VMEM((D, D_EXP), BF16), pltpu.VMEM((D_EXP, D), BF16),
                            pltpu.SemaphoreType.DMA((3,))],
        ),
        out_shape=jax.ShapeDtypeStruct((MOE_ROWS * NCH, 128), F32),
        compiler_params=_cp(("arbitrary",)),
        name="experts",
    )(meta, xs, w_gate_e, w_up_e, w_down_e)


COMB_TM = SEG


def _combine_kernel(pos_ref, w_ref, x1_ref, mod_ref, fg_ref, ys_hbm, oa_ref, ob_ref, buf, y_s, sem, *, final):
    i = pl.program_id(0)
    n = pl.num_programs(0) - 1
    n_ctx = T_CTX // COMB_TM

    for s in range(2):
        @pl.when((i < n) & (lax.rem(i, 2) == s))
        def _():
            for r in range(COMB_TM):
                for k in range(2):
                    src = pl.ds(pl.multiple_of(pos_ref[k, r] * NCH, NCH), NCH)
                    pltpu.make_async_copy(ys_hbm.at[src, :], buf.at[s, k, pl.ds(r * NCH, NCH), :],
                                          sem.at[s]).start(priority=k)

    for slot in range(2):
        @pl.when((i > 0) & (lax.rem(i - 1, 2) == slot))
        def _():
            for k in range(2):
                pltpu.make_async_copy(ys_hbm.at[pl.ds(0, COMB_TM * NCH), :], buf.at[slot, k], sem.at[slot]).wait()
            w = w_ref[...]
            y = (w[:, 0:1] * _load_row_tiles(buf.at[slot, 0], COMB_TM)
                 + w[:, 1:2] * _load_row_tiles(buf.at[slot, 1], COMB_TM))
            y_s[...] = y

    @pl.when(i > 0)
    def _():
        x = x1_ref[...] + mod_ref[5:6, :] * y_s[...]
        if final:
            ms = jnp.mean(x * x, axis=-1, keepdims=True)
            x = x * lax.rsqrt(ms + EPS) * fg_ref[...]

        @pl.when(i - 1 < n_ctx)
        def _():
            oa_ref[...] = x

        @pl.when(i - 1 >= n_ctx)
        def _():
            ob_ref[...] = x


def _combine(pos, wts_t, x1, modseg, final_g, ys, l, final):
    n = T // COMB_TM
    n_ctx = T_CTX // COMB_TM

    def done(i):
        return jnp.maximum(i - 1, 0)

    return pl.pallas_call(
        functools.partial(_combine_kernel, final=final),
        grid=(n + 1,),
        in_specs=[
            pl.BlockSpec((2, COMB_TM), lambda i: (0, jnp.minimum(i, n - 1)), memory_space=pltpu.SMEM),
            pl.BlockSpec((COMB_TM, 2), lambda i: (done(i), 0)),
            pl.BlockSpec((COMB_TM, D), lambda i: (done(i), 0)),
            pl.BlockSpec((None, None, 8, D), lambda i: (l, done(i), 0, 0)),
            pl.BlockSpec((1, D), lambda i: (0, 0)),
            pl.BlockSpec(memory_space=pl.ANY),
        ],
        out_specs=[pl.BlockSpec((COMB_TM, D), lambda i: (jnp.minimum(done(i), n_ctx - 1), 0)),
                   pl.BlockSpec((COMB_TM, D), lambda i: (jnp.maximum(done(i) - n_ctx, 0), 0))],
        out_shape=[jax.ShapeDtypeStruct((T_CTX, D), F32), jax.ShapeDtypeStruct((T_LAT, D), F32)],
        scratch_shapes=[pltpu.VMEM((2, 2, COMB_TM * NCH, 128), F32), pltpu.VMEM((COMB_TM, D), F32),
                        pltpu.SemaphoreType.DMA((2,))],
        compiler_params=_cp(("arbitrary",)),
        name="combine",
    )(pos, wts_t, x1, modseg, final_g.reshape(1, D), ys)


def _rope_tables():
    n = DEC_SEQ
    pos_row = np.repeat(np.arange(n // GRID_W, dtype=np.float32), GRID_W)
    pos_col = np.tile(np.arange(GRID_W, dtype=np.float32), n // GRID_W)
    half = HD // 2
    inv_freq = jnp.asarray(ROPE_THETA, F32) ** (-jnp.arange(0, half, 2, dtype=F32) / half)
    ang = jnp.concatenate([jnp.asarray(pos_row)[:, None] * inv_freq,
                           jnp.asarray(pos_col)[:, None] * inv_freq], axis=-1)
    cos, sin = jnp.cos(ang), jnp.sin(ang)
    cos128 = jnp.tile(cos, (1, 4))
    sin128 = jnp.tile(jnp.concatenate([-sin, sin], axis=-1), (1, 2))
    return cos128, sin128


def _head_mean_matrix():
    idx = np.arange(2 * HD)
    same = (idx[:, None] // HD) == (idx[None, :] // HD)
    return jnp.asarray(same.astype(np.float32) / HD, BF16)


_SEG_ROWS = np.array([0] * (T_CTX // SEG) + [1 + b for b in range(DEC_BATCH) for _ in range(DEC_SEQ // SEG)])


def kernel(x_prompt, x_sample, cache_k, cache_v, state_rec, c, c_ctx, w_mod, b_mod, norm1_g, norm2_g, w_in, conv_w, conv_b, rg_wa, rg_ba, rg_wx, rg_bx, rg_lambda, q_norm_g, k_norm_g, w_rec_out, w_att_out, w_out, w_router, router_bias, w_gate_e, w_up_e, w_down_e, final_g):
    xa, xb = x_prompt.reshape(T_CTX, D), x_sample.reshape(T_LAT, D)

    cvecs = jnp.concatenate([c_ctx[None, :], c, jnp.zeros((3, D), F32)], axis=0)
    mods = _mods(cvecs, w_mod, b_mod).reshape(DEPTH, 8, 6, D)
    modseg = jnp.pad(mods[:, _SEG_ROWS], ((0, 0), (0, 0), (0, 2), (0, 0)))

    cos128, sin128 = _rope_tables()
    bd = _head_mean_matrix()
    qg128 = jnp.tile(q_norm_g, (1, 2)).reshape(DEPTH, 1, 2 * HD)
    kg128 = jnp.tile(k_norm_g, (1, 2)).reshape(DEPTH, 1, 2 * HD)
    wg = jnp.concatenate([rg_wa[:, 0], rg_wx[:, 0], rg_wa[:, 1], rg_wx[:, 1]], axis=-1)
    pvec = jnp.stack([rg_ba[:, 0], rg_bx[:, 0], rg_ba[:, 1], rg_bx[:, 1],
                      rg_lambda[:, 0], rg_lambda[:, 1], conv_b, jnp.zeros_like(conv_b)], axis=1)
    wrt = w_router.T
    rbias = router_bias.reshape(N_EXP, 1)
    zrow = jnp.zeros((MOE_TM * NCH, 128), F32)

    caches, new_s = (), []
    for l in range(DEPTH):
        proj = _inproj(xa, xb, modseg, norm1_g, w_in, l)
        h0 = jnp.concatenate([jnp.zeros((T_CTX // UNIT, 2, D), F32), state_rec[:, l]], axis=0)
        yrec, stf, stb = _rec(proj, conv_w, pvec, wg, h0, l)
        o_ctx, kc, vc = _attn_ctx(proj, qg128, kg128, bd, l, prev_caches=caches)
        caches = (kc, vc)
        o_lat = _attn_lat(proj, cache_k, cache_v, qg128, kg128, cos128, sin128, bd, l)
        x1, h2, idx, wts = _merge(yrec, o_ctx, o_lat, proj, xa, xb, modseg, norm2_g,
                                  w_rec_out, w_att_out, w_out, wrt, rbias, l)
        pos, meta = _route_pos(idx)
        xs = _dispatch(meta, pos, h2, zrow)
        ys = _experts(meta, xs, w_gate_e, w_up_e, w_down_e, l)
        xa, xb = _combine(pos, wts.T, x1, modseg, final_g, ys, l, final=(l == DEPTH - 1))
        n_cu = T_CTX // UNIT
        spu = UNIT // SEQ
        hf_last = stf[:n_cu].reshape(n_cu, spu, 2, D)[:, :, 1].reshape(BATCH, D)
        hb_first = stb[:n_cu].reshape(n_cu, spu, 2, D)[:, :, 0].reshape(BATCH, D)
        new_s.append(jnp.stack([hf_last, hb_first], axis=1))

    y_prompt = xa.reshape(BATCH, SEQ, D)
    y_sample = xb.reshape(DEC_BATCH, DEC_SEQ, D)
    return (y_prompt, y_sample, caches[0], caches[1], jnp.stack(new_s, axis=1))
```

```python
import functools

import numpy as np
import jax
import jax.numpy as jnp
from jax import lax
from jax.experimental import pallas as pl
from jax.experimental.pallas import tpu as pltpu

F32 = jnp.float32
BF16 = jnp.bfloat16

D = 1024
BATCH = 16
SEQ = 256
DEPTH = 2
DEC_BATCH = 4
DEC_SEQ = 1024
PAST = 256
GRID_W = 64
N_HEADS = 16
N_KV = 4
HD = 64
RG_BLK = 128
RG_C = 8.0
N_EXP = 16
D_EXP = 512
ROPE_THETA = 10000.0
EPS = 1e-6
P_IN = 5632
TINY = float(np.finfo(np.float32).tiny)
NEG_LOG2E = -float(np.log2(np.e))

T_CTX = BATCH * SEQ
T_LAT = DEC_BATCH * DEC_SEQ
T = T_CTX + T_LAT
SEG = 256
UNIT = 1024
N_UNIT = T // UNIT
LANES = 128
SUBLANES = 8
CHUNK = UNIT // SUBLANES
CSTRIDE = CHUNK + SUBLANES

VMEM_LIMIT = 56 * 1024 * 1024


def _cp(sem):
    return pltpu.CompilerParams(dimension_semantics=sem, vmem_limit_bytes=VMEM_LIMIT)


def _split(x):
    hi = x.astype(BF16)
    lo = (x - hi.astype(F32)).astype(BF16)
    return hi, lo


def _sigmoid(x):
    return 0.5 * jnp.tanh(0.5 * x) + 0.5


NCH = D // LANES


def _store_row_tiles(ref, x):
    n = x.shape[0]
    for c in range(NCH):
        ref[pl.ds(c, n, stride=NCH), :] = x[:, c * 128:(c + 1) * 128]


def _load_row_tiles(ref, n):
    return jnp.concatenate([ref[pl.ds(c, n, stride=NCH), :] for c in range(NCH)], axis=-1)


def _dot(a, b):
    return jnp.dot(a, b, preferred_element_type=F32)


def _dot_nt(a, b):
    return lax.dot_general(a, b, (((1,), (1,)), ((), ())), preferred_element_type=F32)


def _mods_kernel(c_ref, w_ref, b_ref, o_ref):
    c = c_ref[...]
    s = c * jax.nn.sigmoid(c)
    s_hi, s_lo = _split(s)
    w_hi, w_lo = _split(w_ref[...])
    o_ref[...] = _dot(s_hi, w_hi) + _dot(s_hi, w_lo) + _dot(s_lo, w_hi) + b_ref[...]


def _mods(cvecs, w_mod, b_mod):
    tn = 1536
    return pl.pallas_call(
        _mods_kernel,
        grid=(DEPTH, 6 * D // tn),
        in_specs=[
            pl.BlockSpec((8, D), lambda l, j: (0, 0)),
            pl.BlockSpec((None, D, tn), lambda l, j: (l, 0, j)),
            pl.BlockSpec((None, 1, tn), lambda l, j: (l, 0, j)),
        ],
        out_specs=pl.BlockSpec((None, 8, tn), lambda l, j: (l, 0, j)),
        out_shape=jax.ShapeDtypeStruct((DEPTH, 8, 6 * D), F32),
        compiler_params=_cp(("arbitrary", "arbitrary")),
        name="mods",
    )(cvecs, w_mod, b_mod.reshape(DEPTH, 1, 6 * D))


def _norm_mod(x, g, shift, scale):
    ms = jnp.mean(x * x, axis=-1, keepdims=True)
    return x * lax.rsqrt(ms + EPS) * g * (1.0 + scale) + shift


def _two_part_specs(tm, n_ctx):
    return [pl.BlockSpec((tm, D), lambda i, *_: (jnp.minimum(i, n_ctx - 1), 0)),
            pl.BlockSpec((tm, D), lambda i, *_: (jnp.maximum(i - n_ctx, 0), 0))]


def _inproj_kernel(xa_ref, xb_ref, mod_ref, g_ref, w_ref, o_ref, h_ref, *, tm):
    def prologue(x_ref):
        def seg(s, carry):
            r0 = pl.multiple_of(s * SEG, SEG)
            m = mod_ref[s]
            h = _norm_mod(x_ref[pl.ds(r0, SEG), :], g_ref[...], m[0:1, :], m[1:2, :])
            h_ref[pl.ds(r0, SEG), :] = h.astype(BF16)
            return carry
        lax.fori_loop(0, tm // SEG, seg, 0)

    first = pl.program_id(1) == 0
    is_ctx = pl.program_id(0) < T_CTX // tm

    @pl.when(first & is_ctx)
    def _():
        prologue(xa_ref)

    @pl.when(first & jnp.logical_not(is_ctx))
    def _():
        prologue(xb_ref)

    o_ref[...] = _dot(h_ref[...], w_ref[...].astype(BF16)).astype(BF16)


def _inproj(xa, xb, modseg, norm_g, w_in, l):
    tm, tn = 2048, 512
    return pl.pallas_call(
        functools.partial(_inproj_kernel, tm=tm),
        grid=(T // tm, P_IN // tn),
        in_specs=_two_part_specs(tm, T_CTX // tm) + [
            pl.BlockSpec((None, tm // SEG, 8, D), lambda i, j: (l, i, 0, 0)),
            pl.BlockSpec((None, 1, D), lambda i, j: (l, 0, 0)),
            pl.BlockSpec((None, D, tn), lambda i, j: (l, 0, j)),
        ],
        out_specs=pl.BlockSpec((tm, tn), lambda i, j: (i, j)),
        out_shape=jax.ShapeDtypeStruct((T, P_IN), BF16),
        scratch_shapes=[pltpu.VMEM((tm, D), BF16)],
        compiler_params=_cp(("arbitrary", "arbitrary")),
        name="inproj",
    )(xa, xb, modseg, norm_g.reshape(DEPTH, 1, D), w_in)


REC_CW = 512
PAD_F = 16
PAD_B = 8
GATE_ROWS = 512


def _rec_kernel(xr_ref, gate_ref, cw_ref, pv_ref, wg_ref, h0_ref,
                y_ref, stf_ref, stb_ref,
                xs_ref, af_ref, bf_ref, ab_ref, bb_ref, nat_ref, wgh_ref):
    u = pl.program_id(0)
    is_ctx = u < (T_CTX // UNIT)
    cps = jnp.where(is_ctx, SEQ // CHUNK, DEC_SEQ // CHUNK)
    nblk = REC_CW // RG_BLK

    def lanes(n):
        return slice(n * RG_BLK, (n + 1) * RG_BLK)

    def tile(r):
        return slice(8 * r, 8 * r + 8)

    for c in range(8):
        for n in range(nblk):
            nat_ref[n, c * CSTRIDE:c * CSTRIDE + CHUNK, :] = xr_ref[c * CHUNK:(c + 1) * CHUNK, lanes(n)].astype(F32)
    for r in range(CHUNK):
        for n in range(nblk):
            xs_ref[PAD_F + 8 * r:PAD_F + 8 * r + 8, lanes(n)] = nat_ref[n, pl.ds(r, 8, stride=CSTRIDE), :]
    chunk_id = lax.broadcasted_iota(jnp.int32, (8, 1), 0)
    seq_start = jnp.bitwise_and(chunk_id, cps - 1) == 0
    seq_end = jnp.bitwise_and(chunk_id, cps - 1) == cps - 1
    for j, r in ((0, CHUNK - 2), (1, CHUNK - 1)):
        prev_chunk = pltpu.roll(xs_ref[PAD_F + 8 * r:PAD_F + 8 * r + 8, :], 1, 0)
        xs_ref[tile(j), :] = jnp.where(seq_start, 0.0, prev_chunk)
    next_chunk = pltpu.roll(xs_ref[PAD_F:PAD_F + 8, :], 7, 0)
    xs_ref[PAD_F + UNIT:PAD_F + UNIT + PAD_B, :] = jnp.where(seq_end, 0.0, next_chunk)

    pv = pv_ref[...]
    cwts = cw_ref[...]
    conv_b = pv[6:7, :]

    def softplus_neg(lam):
        z = -lam
        return jnp.maximum(z, 0.0) + jnp.log1p(jnp.exp(-jnp.abs(z)))

    c4s = tuple((0.5 * RG_C) * softplus_neg(pv[4 + d:5 + d, :]) for d in range(2))
    pv_h = 0.5 * pv
    for n in range(nblk):
        wgh_ref[n] = (0.5 * wg_ref[n]).astype(BF16)
    a_refs = (af_ref, ab_ref)
    b_refs = (bf_ref, bb_ref)

    def gates(g, carry):
        base = pl.multiple_of(g * GATE_ROWS, GATE_ROWS)

        def tap(d):
            return xs_ref[pl.ds(pl.multiple_of(base + PAD_F + 8 * d, 8), GATE_ROWS), :]

        xc = conv_b + tap(-2) * cwts[0:1, :]
        xc = xc + tap(-1) * cwts[1:2, :]
        xc = xc + tap(0) * cwts[2:3, :]
        xc = xc + tap(1) * cwts[3:4, :]
        for n in range(nblk):
            ls = lanes(n)
            xn = xc[:, ls]
            hx = 0.5 * xn
            pre_h = _dot(xn.astype(BF16), wgh_ref[n])
            for d in range(2):
                th_r = jnp.tanh(pre_h[:, (2 * d) * RG_BLK:(2 * d + 1) * RG_BLK] + pv_h[2 * d:2 * d + 1, ls])
                th_i = jnp.tanh(pre_h[:, (2 * d + 1) * RG_BLK:(2 * d + 2) * RG_BLK] + pv_h[2 * d + 1:2 * d + 2, ls])
                c4 = c4s[d][:, ls]
                nla = c4 * th_r + c4
                a = jnp.exp2(nla * NEG_LOG2E)
                s = jnp.tanh(nla) * (a * a + 1.0)
                inp = (s * lax.rsqrt(jnp.maximum(s, TINY))) * (hx * th_i + hx)
                a_refs[d][pl.ds(base, GATE_ROWS), ls] = a
                b_refs[d][pl.ds(base, GATE_ROWS), ls] = inp
        return carry

    lax.fori_loop(0, UNIT // GATE_ROWS, gates, 0)

    hf = hb = jnp.zeros((8, REC_CW), F32)
    pf = pb = jnp.ones((8, REC_CW), F32)
    for r in range(CHUNK):
        rf, rb = tile(r), tile(CHUNK - 1 - r)
        a = af_ref[rf, :]
        hf = a * hf + bf_ref[rf, :]
        pf = a * pf
        bf_ref[rf, :] = hf
        af_ref[rf, :] = pf
        a = ab_ref[rb, :]
        hb = a * hb + bb_ref[rb, :]
        pb = a * pb
        bb_ref[rb, :] = hb
        ab_ref[rb, :] = pb

    h0f = h0_ref[0:1, :]
    h0b = h0_ref[1:2, :]
    cf = [h0f]
    for c in range(1, 8):
        chain = hf[c - 1:c, :] + pf[c - 1:c, :] * cf[c - 1]
        cf.append(jnp.where(jnp.bitwise_and(c, cps - 1) == 0, h0f, chain))
    cb = [None] * 8
    cb[7] = h0b
    for c in range(6, -1, -1):
        chain = hb[c + 1:c + 2, :] + pb[c + 1:c + 2, :] * cb[c + 1]
        cb[c] = jnp.where(jnp.bitwise_and(c, cps - 1) == cps - 1, h0b, chain)
    carry_f = jnp.concatenate(cf, axis=0)
    carry_b = jnp.concatenate(cb, axis=0)
    stf_ref[...] = hf + pf * carry_f
    stb_ref[...] = hb + pb * carry_b

    for r in range(CHUNK):
        h = (bf_ref[tile(r), :] + af_ref[tile(r), :] * carry_f) + (bb_ref[tile(r), :] + ab_ref[tile(r), :] * carry_b)
        for n in range(nblk):
            nat_ref[n, pl.ds(r, 8, stride=CSTRIDE), :] = h[:, lanes(n)]

    for c in range(8):
        rows = slice(c * CHUNK, (c + 1) * CHUNK)
        for n in range(nblk):
            g = gate_ref[rows, lanes(n)].astype(F32)
            h = nat_ref[n, c * CSTRIDE:c * CSTRIDE + CHUNK, :]
            y_ref[rows, lanes(n)] = (h * jax.nn.gelu(g, approximate=True)).astype(BF16)


def _rec(proj, conv_w, pvec, wg, h0, l):
    ncb = D // REC_CW
    return pl.pallas_call(
        _rec_kernel,
        grid=(N_UNIT, ncb),
        in_specs=[
            pl.BlockSpec((UNIT, REC_CW), lambda u, c: (u, c)),
            pl.BlockSpec((UNIT, REC_CW), lambda u, c: (u, ncb + c)),
            pl.BlockSpec((None, 4, REC_CW), lambda u, c: (l, 0, c)),
            pl.BlockSpec((None, 8, REC_CW), lambda u, c: (l, 0, c)),
            pl.BlockSpec((None, REC_CW // RG_BLK, RG_BLK, 4 * RG_BLK), lambda u, c: (l, c, 0, 0)),
            pl.BlockSpec((None, 2, REC_CW), lambda u, c: (u, 0, c)),
        ],
        out_specs=[
            pl.BlockSpec((UNIT, REC_CW), lambda u, c: (u, c)),
            pl.BlockSpec((None, 8, REC_CW), lambda u, c: (u, 0, c)),
            pl.BlockSpec((None, 8, REC_CW), lambda u, c: (u, 0, c)),
        ],
        out_shape=[
            jax.ShapeDtypeStruct((T, D), BF16),
            jax.ShapeDtypeStruct((N_UNIT, 8, D), F32),
            jax.ShapeDtypeStruct((N_UNIT, 8, D), F32),
        ],
        scratch_shapes=[pltpu.VMEM((PAD_F + UNIT + PAD_B, REC_CW), F32)]
        + [pltpu.VMEM((UNIT, REC_CW), F32)] * 4
        + [pltpu.VMEM((REC_CW // RG_BLK, 8 * CSTRIDE, RG_BLK), F32),
           pltpu.VMEM((REC_CW // RG_BLK, RG_BLK, 4 * RG_BLK), BF16)],
        compiler_params=_cp(("arbitrary", "arbitrary")),
        name="rec",
    )(proj, proj, conv_w, pvec, wg, h0)


def _head_norm(x, g128, bd):
    hi, lo = _split(x * x)
    ms = _dot(hi, bd) + _dot(lo, bd)
    return x * lax.rsqrt(ms + EPS) * g128


def _rope(x, cos, sin_signed):
    lane = lax.broadcasted_iota(jnp.int32, x.shape, 1)
    first_half = jnp.bitwise_and(lane, HD - 1) < HD // 2
    partner = jnp.where(first_half, pltpu.roll(x, 2 * HD - HD // 2, 1), pltpu.roll(x, HD // 2, 1))
    return x * cos + partner * sin_signed


def _with_ones(v):
    return jnp.concatenate([v, jnp.ones_like(v)], axis=-1)


def _softmax_pv(q, k, v_ext):
    s = _dot_nt(q, k)
    m = jnp.max(s, axis=-1, keepdims=True)
    p = jnp.exp2(s - m).astype(BF16)
    r = _dot(p, v_ext)
    return r[:, :HD] / r[:, HD:HD + 1]


def _attend_heads(q_ref, k, v_ext):
    return jnp.concatenate([_softmax_pv(q_ref[h], k, v_ext) for h in range(N_HEADS // N_KV)], axis=-1)


def _attn_ctx_kernel(q_ref, k_ref, v_ref, qg_ref, kg_ref, bd_ref, *rest, slab):
    o_ref, ko_ref, vo_ref = rest[-3:]
    for p in range(slab):
        ko_ref[p] = rest[0][p]
        vo_ref[p] = rest[1][p]
    g = N_HEADS // N_KV
    bd = bd_ref[...]
    scale = HD ** -0.5 * float(np.log2(np.e))

    kx = _head_norm(k_ref[...].astype(F32), kg_ref[...], bd)
    vx = v_ref[...].astype(F32)
    outs = []
    for kv in range(2):
        k_new, v_new = kx[:, kv * HD:(kv + 1) * HD], vx[:, kv * HD:(kv + 1) * HD]
        ko_ref[slab, kv] = k_new
        vo_ref[slab, kv] = v_new
        heads = []
        for j in range(g // 2):
            c0 = (kv * g + 2 * j) * HD
            x = _head_norm(q_ref[:, c0:c0 + 2 * HD].astype(F32), qg_ref[...], bd) * scale
            heads += [x[:, :HD].astype(BF16), x[:, HD:].astype(BF16)]
        o = _softmax_pv(jnp.concatenate(heads, axis=0), k_new.astype(BF16), _with_ones(v_new.astype(BF16)))
        outs += [o[h * SEQ:(h + 1) * SEQ] for h in range(g)]
    o_ref[...] = jnp.concatenate(outs, axis=-1).astype(BF16)


def _attn_ctx(proj, qg128, kg128, bd, l, prev_caches=()):
    g = 2 * (N_HEADS // N_KV)
    qcol = 2 * D // (g * HD)
    kcol = 3 * D // (2 * HD)
    vcol = kcol + N_KV // 2
    cache_spec = pl.BlockSpec((None, l + 1, 2, SEQ, HD), lambda b, h: (b, 0, h, 0, 0))
    prev_specs = [pl.BlockSpec((None, l, 2, SEQ, HD), lambda b, h: (b, 0, h, 0, 0))] * 2 if l else []
    cache_shape = jax.ShapeDtypeStruct((BATCH, l + 1, N_KV, SEQ, HD), F32)
    return pl.pallas_call(
        functools.partial(_attn_ctx_kernel, slab=l),
        grid=(BATCH, N_KV // 2),
        in_specs=[
            pl.BlockSpec((SEQ, g * HD), lambda b, h: (b, qcol + h)),
            pl.BlockSpec((SEQ, 2 * HD), lambda b, h: (b, kcol + h)),
            pl.BlockSpec((SEQ, 2 * HD), lambda b, h: (b, vcol + h)),
            pl.BlockSpec((None, 1, 2 * HD), lambda b, h: (l, 0, 0)),
            pl.BlockSpec((None, 1, 2 * HD), lambda b, h: (l, 0, 0)),
            pl.BlockSpec((2 * HD, 2 * HD), lambda b, h: (0, 0)),
        ] + prev_specs,
        out_specs=[pl.BlockSpec((SEQ, g * HD), lambda b, h: (b, h)), cache_spec, cache_spec],
        out_shape=[jax.ShapeDtypeStruct((T_CTX, D), BF16), cache_shape, cache_shape],
        compiler_params=_cp(("arbitrary", "arbitrary")),
        name="attn_ctx",
    )(proj, proj, proj, qg128, kg128, bd, *prev_caches)


def _attn_lat_kernel(q_ref, k_ref, v_ref, qg_ref, kg_ref, cos_ref, sin_ref, bd_ref, pk_ref, pv_ref,
                     o_ref, q_s, k_s, v_s):
    g = N_HEADS // N_KV
    bd, cos, sin = bd_ref[...], cos_ref[...], sin_ref[...]
    odd = lax.rem(pl.program_id(1), 2) == 1
    scale = HD ** -0.5 * float(np.log2(np.e))

    kx = _rope(_head_norm(k_ref[...].astype(F32), kg_ref[...], bd), cos, sin)
    vx = v_ref[...]
    k_s[0:PAST, :] = pk_ref[...].astype(BF16)
    k_s[PAST:, :] = jnp.where(odd, kx[:, HD:], kx[:, :HD]).astype(BF16)
    v_s[0:PAST, :] = _with_ones(pv_ref[...].astype(BF16))
    v_s[PAST:, :] = _with_ones(jnp.where(odd, vx[:, HD:], vx[:, :HD]))
    for j in range(g // 2):
        x = _head_norm(q_ref[:, 2 * HD * j:2 * HD * (j + 1)].astype(F32), qg_ref[...], bd)
        x = _rope(x, cos, sin) * scale
        q_s[2 * j] = x[:, :HD].astype(BF16)
        q_s[2 * j + 1] = x[:, HD:].astype(BF16)

    o_ref[...] = _attend_heads(q_s, k_s[...], v_s[...]).astype(BF16)


def _attn_lat(proj, cache_k, cache_v, qg128, kg128, cos128, sin128, bd, l):
    g = N_HEADS // N_KV
    row0 = T_CTX // DEC_SEQ
    qcol = 2 * D // (g * HD)
    kcol = 3 * D // (2 * HD)
    vcol = kcol + N_KV // 2
    return pl.pallas_call(
        _attn_lat_kernel,
        grid=(DEC_BATCH, N_KV),
        in_specs=[
            pl.BlockSpec((DEC_SEQ, g * HD), lambda b, h: (row0 + b, qcol + h)),
            pl.BlockSpec((DEC_SEQ, 2 * HD), lambda b, h: (row0 + b, kcol + h // 2)),
            pl.BlockSpec((DEC_SEQ, 2 * HD), lambda b, h: (row0 + b, vcol + h // 2)),
            pl.BlockSpec((None, 1, 2 * HD), lambda b, h: (l, 0, 0)),
            pl.BlockSpec((None, 1, 2 * HD), lambda b, h: (l, 0, 0)),
            pl.BlockSpec((DEC_SEQ, 2 * HD), lambda b, h: (0, 0)),
            pl.BlockSpec((DEC_SEQ, 2 * HD), lambda b, h: (0, 0)),
            pl.BlockSpec((2 * HD, 2 * HD), lambda b, h: (0, 0)),
            pl.BlockSpec((None, None, None, PAST, HD), lambda b, h: (b, l, h, 0, 0)),
            pl.BlockSpec((None, None, None, PAST, HD), lambda b, h: (b, l, h, 0, 0)),
        ],
        out_specs=pl.BlockSpec((DEC_SEQ, g * HD), lambda b, h: (b, h)),
        out_shape=jax.ShapeDtypeStruct((T_LAT, D), BF16),
        scratch_shapes=[pltpu.VMEM((g, DEC_SEQ, HD), BF16), pltpu.VMEM((PAST + DEC_SEQ, HD), BF16),
                        pltpu.VMEM((PAST + DEC_SEQ, 2 * HD), BF16)],
        compiler_params=_cp(("arbitrary", "arbitrary")),
        name="attn_lat",
    )(proj, proj, proj, qg128, kg128, cos128, sin128, bd, cache_k, cache_v)


MERGE_TM = 512


def _route(lt, bias):
    rows = [lt[e:e + 1, :] for e in range(N_EXP)]
    m = rows[0]
    for e in range(1, N_EXP):
        m = jnp.maximum(m, rows[e])
    ex = [jnp.exp(r - m) for r in rows]
    z = ex[0]
    for e in range(1, N_EXP):
        z = z + ex[e]
    probs = [x / z for x in ex]
    sel = [probs[e] + bias[e:e + 1, :] for e in range(N_EXP)]

    def top2_sum(v):
        a, b = jnp.maximum(v[0], v[1]), jnp.minimum(v[0], v[1])
        c, d = jnp.maximum(v[2], v[3]), jnp.minimum(v[2], v[3])
        return jnp.maximum(a, c) + jnp.maximum(jnp.minimum(a, c), jnp.maximum(b, d))

    scores = [top2_sum(sel[4 * g:4 * g + 4]) for g in range(4)]
    best = jnp.zeros_like(scores[0], dtype=jnp.int32)
    best_s = scores[0]
    for g in range(1, 4):
        take = scores[g] > best_s
        best = jnp.where(take, g, best)
        best_s = jnp.where(take, scores[g], best_s)
    cs, cp = [], []
    for j in range(4):
        s_j, p_j = sel[j], probs[j]
        for g in range(1, 4):
            s_j = jnp.where(best == g, sel[4 * g + j], s_j)
            p_j = jnp.where(best == g, probs[4 * g + j], p_j)
        cs.append(s_j)
        cp.append(p_j)
    neg = jnp.full_like(cs[0], -jnp.inf)

    def argmax4(v):
        bi = jnp.zeros_like(best)
        bv = v[0]
        for j in range(1, 4):
            take = v[j] > bv
            bi = jnp.where(take, j, bi)
            bv = jnp.where(take, v[j], bv)
        return bi

    def pick(v, idx):
        out = v[0]
        for j in range(1, 4):
            out = jnp.where(idx == j, v[j], out)
        return out

    i1 = argmax4(cs)
    cs2 = [jnp.where(i1 == j, neg, cs[j]) for j in range(4)]
    i2 = argmax4(cs2)
    i2 = jnp.where((i2 == 0) & (i1 == 0), 1, i2)
    w1, w2 = pick(cp, i1), pick(cp, i2)
    den = w1 + w2
    return best * 4 + i1, best * 4 + i2, w1 / den, w2 / den


def _merge_kernel(yrec_ref, oa_ref, ob_ref, gr0_ref, gr1_ref, ga0_ref, ga1_ref, xa_ref, xb_ref, mod_ref, g2_ref,
                  wrec_ref, watt_ref, wout_ref, wrt_ref, rb_ref,
                  x1_ref, h2_ref, idx_ref, wts_ref,
                  wrec_s, watt_s, wout_s):
    @pl.when(pl.program_id(0) == 0)
    def _():
        wrec_s[...] = wrec_ref[...].astype(BF16)
        watt_s[...] = watt_ref[...].astype(BF16)
        wout_s[...] = wout_ref[...].astype(BF16)

    is_ctx = pl.program_id(0) < T_CTX // MERGE_TM
    args = (yrec_ref, gr0_ref, gr1_ref, ga0_ref, ga1_ref, mod_ref, g2_ref, wrt_ref, rb_ref,
            x1_ref, h2_ref, idx_ref, wts_ref, wrec_s, watt_s, wout_s)

    @pl.when(is_ctx)
    def _():
        _merge_body(oa_ref, xa_ref, *args)

    @pl.when(jnp.logical_not(is_ctx))
    def _():
        _merge_body(ob_ref, xb_ref, *args)


def _merge_body(oatt_ref, x_ref, yrec_ref, gr0_ref, gr1_ref, ga0_ref, ga1_ref, mod_ref, g2_ref, wrt_ref, rb_ref,
                x1_ref, h2_ref, idx_ref, wts_ref, wrec_s, watt_s, wout_s):
    half = D // 2
    b_rec = _dot(yrec_ref[...], wrec_s[...])
    b_att = _dot(oatt_ref[...], watt_s[...])
    m0 = _sigmoid(gr0_ref[...].astype(F32)) * b_rec[:, :half] + _sigmoid(ga0_ref[...].astype(F32)) * b_att[:, :half]
    m1 = _sigmoid(gr1_ref[...].astype(F32)) * b_rec[:, half:] + _sigmoid(ga1_ref[...].astype(F32)) * b_att[:, half:]
    merged = jnp.concatenate([m0, m1], axis=-1).astype(BF16)
    out = _dot(merged, wout_s[...])

    hs = []
    for s in range(MERGE_TM // SEG):
        rows = slice(s * SEG, (s + 1) * SEG)
        m = mod_ref[s]
        x1 = x_ref[rows, :] + m[2:3, :] * out[rows, :]
        x1_ref[rows, :] = x1
        h2 = _norm_mod(x1, g2_ref[...], m[3:4, :], m[4:5, :])
        hs.append(h2)
    h2 = jnp.concatenate(hs, axis=0)
    _store_row_tiles(h2_ref, h2)

    h_hi, h_lo = _split(h2)
    w_hi, w_lo = _split(wrt_ref[...])
    lt = _dot_nt(w_hi, h_hi) + _dot_nt(w_hi, h_lo) + _dot_nt(w_lo, h_hi)
    e1, e2, w1, w2 = _route(lt, rb_ref[...])
    idx_ref[...] = jnp.concatenate([e1, e2], axis=0)
    wts_ref[...] = jnp.concatenate([w1, w2], axis=0)


def _merge(yrec, o_ctx, o_lat, proj, xa, xb, modseg, norm2_g, w_rec_out, w_att_out, w_out, wrt, rbias, l):
    tm = MERGE_TM
    half = D // 2
    gcol = (3 * D + 2 * N_KV * HD) // half
    wspec = pl.BlockSpec((None, D, D), lambda i: (l, 0, 0))
    return pl.pallas_call(
        _merge_kernel,
        grid=(T // tm,),
        in_specs=[pl.BlockSpec((tm, D), lambda i: (i, 0))] + _two_part_specs(tm, T_CTX // tm) + [
            pl.BlockSpec((tm, half), lambda i: (i, gcol)),
            pl.BlockSpec((tm, half), lambda i: (i, gcol + 1)),
            pl.BlockSpec((tm, half), lambda i: (i, gcol + 2)),
            pl.BlockSpec((tm, half), lambda i: (i, gcol + 3)),
        ] + _two_part_specs(tm, T_CTX // tm) + [
            pl.BlockSpec((None, tm // SEG, 8, D), lambda i: (l, i, 0, 0)),
            pl.BlockSpec((None, 1, D), lambda i: (l, 0, 0)),
            wspec, wspec, wspec,
            pl.BlockSpec((N_EXP, D), lambda i: (0, 0)),
            pl.BlockSpec((N_EXP, 1), lambda i: (0, 0)),
        ],
        out_specs=[
            pl.BlockSpec((tm, D), lambda i: (i, 0)),
            pl.BlockSpec((tm * NCH, 128), lambda i: (i, 0)),
            pl.BlockSpec((2, tm), lambda i: (0, i)),
            pl.BlockSpec((2, tm), lambda i: (0, i)),
        ],
        out_shape=[
            jax.ShapeDtypeStruct((T, D), F32),
            jax.ShapeDtypeStruct((T * NCH, 128), F32),
            jax.ShapeDtypeStruct((2, T), jnp.int32),
            jax.ShapeDtypeStruct((2, T), F32),
        ],
        scratch_shapes=[pltpu.VMEM((D, D), BF16)] * 3,
        compiler_params=_cp(("arbitrary",)),
        name="merge",
    )(yrec, o_ctx, o_lat, proj, proj, proj, proj, xa, xb, modseg, norm2_g.reshape(DEPTH, 1, D),
      w_rec_out, w_att_out, w_out, wrt, rbias)


MOE_TM = 512
MOE_NT = 2 * T // MOE_TM + N_EXP
MOE_ROWS = MOE_NT * MOE_TM
META_TILE_E, META_CNT, META_OFF, META_END, META_NT, META_NEXT_E = 0, 1, 2, 3, 4, 5


def _pos_kernel(idx_ref, pos_ref, meta_ref):
    shift = MOE_TM.bit_length() - 1
    idx = idx_ref[...]
    eid = lax.broadcasted_iota(jnp.int32, (N_EXP, T), 0)
    m0 = eid == idx[0:1, :]
    m1 = eid == idx[1:2, :]
    member = jnp.where(m0 | m1, 1.0, 0.0)
    cnt = jnp.sum(member, axis=1, keepdims=True).astype(jnp.int32)
    ntile = jnp.right_shift(cnt + (MOE_TM - 1), shift)
    offs, acc = [], jnp.zeros((1, 1), jnp.int32)
    for e in range(N_EXP):
        offs.append(acc)
        acc = acc + ntile[e:e + 1, :]
    off_t = jnp.concatenate(offs, axis=0)
    end_t = off_t + ntile

    blk = 256
    r_i = lax.broadcasted_iota(jnp.int32, (blk, blk), 0)
    c_i = lax.broadcasted_iota(jnp.int32, (blk, blk), 1)
    upper = jnp.where(r_i <= c_i, 1.0, 0.0).astype(BF16)
    run = (off_t * MOE_TM).astype(F32)
    for j in range(T // blk):
        ls = slice(j * blk, (j + 1) * blk)
        mb = member[:, ls]
        inc = _dot(mb.astype(BF16), upper)
        dest = run + inc - mb
        pos_ref[0:1, ls] = jnp.sum(jnp.where(m0[:, ls], dest, 0.0), axis=0, keepdims=True).astype(jnp.int32)
        pos_ref[1:2, ls] = jnp.sum(jnp.where(m1[:, ls], dest, 0.0), axis=0, keepdims=True).astype(jnp.int32)
        run = run + inc[:, blk - 1:blk]

    lane = lax.broadcasted_iota(jnp.int32, (1, 128), 1)
    zero = jnp.zeros((1, 128), jnp.int32)
    tile_e, cnt_row, off_row, end_row = zero, zero, zero, zero
    for e in range(N_EXP):
        tile_e = tile_e + jnp.where(lane >= end_t[e:e + 1, :], 1, 0)
        here = lane == e
        cnt_row = jnp.where(here, cnt[e:e + 1, :], cnt_row)
        off_row = jnp.where(here, off_t[e:e + 1, :] * MOE_TM, off_row)
        end_row = jnp.where(here, end_t[e:e + 1, :] * MOE_TM, end_row)
    tile_e = jnp.minimum(tile_e, N_EXP - 1)
    nt_row = zero + acc
    next_row = zero
    nxt = jnp.full((1, 1), -1, jnp.int32)
    for e in reversed(range(N_EXP)):
        next_row = jnp.where(lane == e, nxt, next_row)
        nxt = jnp.where(cnt[e:e + 1, :] > 0, e, nxt)
    meta_ref[...] = jnp.concatenate([tile_e, cnt_row, off_row, end_row, nt_row, next_row, zero, zero], axis=0)


def _route_pos(idx):
    return pl.pallas_call(
        _pos_kernel,
        grid=(1,),
        in_specs=[pl.BlockSpec((2, T), lambda i: (0, 0))],
        out_specs=[pl.BlockSpec((2, T), lambda i: (0, 0)), pl.BlockSpec((8, 128), lambda i: (0, 0))],
        out_shape=[jax.ShapeDtypeStruct((2, T), jnp.int32), jax.ShapeDtypeStruct((8, 128), jnp.int32)],
        compiler_params=_cp(("arbitrary",)),
        name="route_pos",
    )(idx)


DISP_TM = 256


def _dispatch_kernel(meta_ref, pos_ref, h_ref, z_hbm, xs_hbm, sem):
    i = pl.program_id(0)

    def row_copy(src, src_row, dst_row):
        return pltpu.make_async_copy(src.at[pl.ds(src_row * NCH, NCH), :],
                                     xs_hbm.at[pl.ds(pl.multiple_of(dst_row * NCH, NCH), NCH), :], sem)

    for r in range(DISP_TM):
        row_copy(h_ref, r, pos_ref[0, r]).start(priority=0)
        row_copy(h_ref, r, pos_ref[1, r]).start(priority=1)

    e = jnp.minimum(i, N_EXP - 1)
    pad0 = meta_ref[META_OFF, e] + meta_ref[META_CNT, e]
    npad = jnp.where(i < N_EXP, meta_ref[META_END, e] - pad0, 0)

    def pad_copies(act):
        s = pad0
        for bit in reversed(range(MOE_TM.bit_length() - 1)):
            size = 1 << bit
            part = jnp.bitwise_and(npad, size)

            @pl.when(part != 0)
            def _():
                dst = pl.ds(pl.multiple_of(s * NCH, NCH), size * NCH)
                act(pltpu.make_async_copy(z_hbm.at[pl.ds(0, size * NCH), :], xs_hbm.at[dst, :], sem))

            s = s + part

    pad_copies(lambda c: c.start())

    for _ in range(2):
        pltpu.make_async_copy(h_ref, xs_hbm.at[pl.ds(0, DISP_TM * NCH), :], sem).wait()

    tail = meta_ref[META_NT, 0] + i
    has_tail = (i < N_EXP) & (tail < MOE_NT)

    def tail_copy():
        rows = pl.ds(pl.multiple_of(tail * (MOE_TM * NCH), MOE_TM * NCH), MOE_TM * NCH)
        return pltpu.make_async_copy(z_hbm, xs_hbm.at[rows, :], sem)

    @pl.when(has_tail)
    def _():
        tail_copy().start()

    pad_copies(lambda c: c.wait())

    @pl.when(has_tail)
    def _():
        tail_copy().wait()


def _dispatch(meta, pos, h2, zrow):
    return pl.pallas_call(
        _dispatch_kernel,
        grid_spec=pltpu.PrefetchScalarGridSpec(
            num_scalar_prefetch=1,
            grid=(T // DISP_TM,),
            in_specs=[
                pl.BlockSpec((2, DISP_TM), lambda i, meta: (0, i), memory_space=pltpu.SMEM),
                pl.BlockSpec((DISP_TM * NCH, 128), lambda i, meta: (i, 0)),
                pl.BlockSpec((MOE_TM * NCH, 128), lambda i, meta: (0, 0)),
            ],
            out_specs=pl.BlockSpec(memory_space=pl.ANY),
            scratch_shapes=[pltpu.SemaphoreType.DMA],
        ),
        out_shape=jax.ShapeDtypeStruct((MOE_ROWS * NCH, 128), F32),
        compiler_params=_cp(("arbitrary",)),
        name="dispatch",
    )(meta, pos, h2, zrow)


def _experts_kernel(meta_ref, xs_ref, wg_hbm, wu_hbm, wd_hbm, ys_ref,
                    wg_f, wu_f, wd_f, wg_s, wu_s, wd_s, sem, *, l):
    j = pl.program_id(0)
    live = j < meta_ref[META_NT, 0]
    e = meta_ref[META_TILE_E, j]
    e_prev = meta_ref[META_TILE_E, jnp.maximum(j - 1, 0)]

    def fetch(ex):
        return (pltpu.make_async_copy(wg_hbm.at[l, ex], wg_f, sem.at[0]),
                pltpu.make_async_copy(wu_hbm.at[l, ex], wu_f, sem.at[1]),
                pltpu.make_async_copy(wd_hbm.at[l, ex], wd_f, sem.at[2]))

    @pl.when(j == 0)
    def _():
        for c in fetch(e):
            c.start()

    @pl.when(live & ((j == 0) | (e != e_prev)))
    def _():
        for c, dst, src in zip(fetch(e), (wg_s, wu_s, wd_s), (wg_f, wu_f, wd_f)):
            c.wait()
            dst[...] = src[...].astype(BF16)
        nxt = meta_ref[META_NEXT_E, e]

        @pl.when(nxt >= 0)
        def _():
            for c in fetch(nxt):
                c.start()

    @pl.when(live)
    def _():
        x = _load_row_tiles(xs_ref, MOE_TM).astype(BF16)
        g = _dot(x, wg_s[...])
        u = _dot(x, wu_s[...])
        act = (g * _sigmoid(g)) * u
        _store_row_tiles(ys_ref, _dot(act.astype(BF16), wd_s[...]))

    @pl.when(jnp.logical_not(live))
    def _():
        ys_ref[...] = jnp.zeros_like(ys_ref)


def _experts(meta, xs, w_gate_e, w_up_e, w_down_e, l):
    def tile(j, meta):
        return jnp.minimum(j, meta[META_NT, 0] - 1)

    return pl.pallas_call(
        functools.partial(_experts_kernel, l=l),
        grid_spec=pltpu.PrefetchScalarGridSpec(
            num_scalar_prefetch=1,
            grid=(MOE_NT,),
            in_specs=[
                pl.BlockSpec((MOE_TM * NCH, 128), lambda j, meta: (tile(j, meta), 0)),
                pl.BlockSpec(memory_space=pl.ANY),
                pl.BlockSpec(memory_space=pl.ANY),
                pl.BlockSpec(memory_space=pl.ANY),
            ],
            out_specs=pl.BlockSpec((MOE_TM * NCH, 128), lambda j, meta: (j, 0)),
            scratch_shapes=[pltpu.VMEM((D, D_EXP), F32), pltpu.VMEM((D, D_EXP), F32), pltpu.VMEM((D_EXP, D), F32),
                            pltpu.VMEM((D, D_EXP), BF16), pltpu.VMEM((D, D_EXP), BF16), pltpu.VMEM((D_EXP, D), BF16),
                            pltpu.SemaphoreType.DMA((3,))],
        ),
        out_shape=jax.ShapeDtypeStruct((MOE_ROWS * NCH, 128), F32),
        compiler_params=_cp(("arbitrary",)),
        name="experts",
    )(meta, xs, w_gate_e, w_up_e, w_down_e)


COMB_TM = SEG


def _combine_kernel(pos_ref, w_ref, x1_ref, mod_ref, fg_ref, ys_hbm, oa_ref, ob_ref, buf, y_s, sem, *, final):
    i = pl.program_id(0)
    n = pl.num_programs(0) - 1
    n_ctx = T_CTX // COMB_TM

    for s in range(2):
        @pl.when((i < n) & (lax.rem(i, 2) == s))
        def _():
            for r in range(COMB_TM):
                for k in range(2):
                    src = pl.ds(pl.multiple_of(pos_ref[k, r] * NCH, NCH), NCH)
                    pltpu.make_async_copy(ys_hbm.at[src, :], buf.at[s, k, pl.ds(r * NCH, NCH), :],
                                          sem.at[s]).start(priority=k)

    for slot in range(2):
        @pl.when((i > 0) & (lax.rem(i - 1, 2) == slot))
        def _():
            for k in range(2):
                pltpu.make_async_copy(ys_hbm.at[pl.ds(0, COMB_TM * NCH), :], buf.at[slot, k], sem.at[slot]).wait()
            w = w_ref[...]
            y = (w[:, 0:1] * _load_row_tiles(buf.at[slot, 0], COMB_TM)
                 + w[:, 1:2] * _load_row_tiles(buf.at[slot, 1], COMB_TM))
            y_s[...] = y

    @pl.when(i > 0)
    def _():
        x = x1_ref[...] + mod_ref[5:6, :] * y_s[...]
        if final:
            ms = jnp.mean(x * x, axis=-1, keepdims=True)
            x = x * lax.rsqrt(ms + EPS) * fg_ref[...]

        @pl.when(i - 1 < n_ctx)
        def _():
            oa_ref[...] = x

        @pl.when(i - 1 >= n_ctx)
        def _():
            ob_ref[...] = x


def _combine(pos, wts_t, x1, modseg, final_g, ys, l, final):
    n = T // COMB_TM
    n_ctx = T_CTX // COMB_TM

    def done(i):
        return jnp.maximum(i - 1, 0)

    return pl.pallas_call(
        functools.partial(_combine_kernel, final=final),
        grid=(n + 1,),
        in_specs=[
            pl.BlockSpec((2, COMB_TM), lambda i: (0, jnp.minimum(i, n - 1)), memory_space=pltpu.SMEM),
            pl.BlockSpec((COMB_TM, 2), lambda i: (done(i), 0)),
            pl.BlockSpec((COMB_TM, D), lambda i: (done(i), 0)),
            pl.BlockSpec((None, None, 8, D), lambda i: (l, done(i), 0, 0)),
            pl.BlockSpec((1, D), lambda i: (0, 0)),
            pl.BlockSpec(memory_space=pl.ANY),
        ],
        out_specs=[pl.BlockSpec((COMB_TM, D), lambda i: (jnp.minimum(done(i), n_ctx - 1), 0)),
                   pl.BlockSpec((COMB_TM, D), lambda i: (jnp.maximum(done(i) - n_ctx, 0), 0))],
        out_shape=[jax.ShapeDtypeStruct((T_CTX, D), F32), jax.ShapeDtypeStruct((T_LAT, D), F32)],
        scratch_shapes=[pltpu.VMEM((2, 2, COMB_TM * NCH, 128), F32), pltpu.VMEM((COMB_TM, D), F32),
                        pltpu.SemaphoreType.DMA((2,))],
        compiler_params=_cp(("arbitrary",)),
        name="combine",
    )(pos, wts_t, x1, modseg, final_g.reshape(1, D), ys)


def _rope_tables():
    n = DEC_SEQ
    pos_row = np.repeat(np.arange(n // GRID_W, dtype=np.float32), GRID_W)
    pos_col = np.tile(np.arange(GRID_W, dtype=np.float32), n // GRID_W)
    half = HD // 2
    inv_freq = jnp.asarray(ROPE_THETA, F32) ** (-jnp.arange(0, half, 2, dtype=F32) / half)
    ang = jnp.concatenate([jnp.asarray(pos_row)[:, None] * inv_freq,
                           jnp.asarray(pos_col)[:, None] * inv_freq], axis=-1)
    cos, sin = jnp.cos(ang), jnp.sin(ang)
    cos128 = jnp.tile(cos, (1, 4))
    sin128 = jnp.tile(jnp.concatenate([-sin, sin], axis=-1), (1, 2))
    return cos128, sin128


def _head_mean_matrix():
    idx = np.arange(2 * HD)
    same = (idx[:, None] // HD) == (idx[None, :] // HD)
    return jnp.asarray(same.astype(np.float32) / HD, BF16)


_SEG_ROWS = np.array([0] * (T_CTX // SEG) + [1 + b for b in range(DEC_BATCH) for _ in range(DEC_SEQ // SEG)])


def kernel(x_prompt, x_sample, cache_k, cache_v, state_rec, c, c_ctx, w_mod, b_mod, norm1_g, norm2_g, w_in, conv_w, conv_b, rg_wa, rg_ba, rg_wx, rg_bx, rg_lambda, q_norm_g, k_norm_g, w_rec_out, w_att_out, w_out, w_router, router_bias, w_gate_e, w_up_e, w_down_e, final_g):
    xa, xb = x_prompt.reshape(T_CTX, D), x_sample.reshape(T_LAT, D)

    cvecs = jnp.concatenate([c_ctx[None, :], c, jnp.zeros((3, D), F32)], axis=0)
    mods = _mods(cvecs, w_mod, b_mod).reshape(DEPTH, 8, 6, D)
    modseg = jnp.pad(mods[:, _SEG_ROWS], ((0, 0), (0, 0), (0, 2), (0, 0)))

    cos128, sin128 = _rope_tables()
    bd = _head_mean_matrix()
    qg128 = jnp.tile(q_norm_g, (1, 2)).reshape(DEPTH, 1, 2 * HD)
    kg128 = jnp.tile(k_norm_g, (1, 2)).reshape(DEPTH, 1, 2 * HD)
    wg = jnp.concatenate([rg_wa[:, 0], rg_wx[:, 0], rg_wa[:, 1], rg_wx[:, 1]], axis=-1)
    pvec = jnp.stack([rg_ba[:, 0], rg_bx[:, 0], rg_ba[:, 1], rg_bx[:, 1],
                      rg_lambda[:, 0], rg_lambda[:, 1], conv_b, jnp.zeros_like(conv_b)], axis=1)
    wrt = w_router.T
    rbias = router_bias.reshape(N_EXP, 1)
    zrow = jnp.zeros((MOE_TM * NCH, 128), F32)

    caches, new_s = (), []
    for l in range(DEPTH):
        proj = _inproj(xa, xb, modseg, norm1_g, w_in, l)
        h0 = jnp.concatenate([jnp.zeros((T_CTX // UNIT, 2, D), F32), state_rec[:, l]], axis=0)
        yrec, stf, stb = _rec(proj, conv_w, pvec, wg, h0, l)
        o_ctx, kc, vc = _attn_ctx(proj, qg128, kg128, bd, l, prev_caches=caches)
        caches = (kc, vc)
        o_lat = _attn_lat(proj, cache_k, cache_v, qg128, kg128, cos128, sin128, bd, l)
        x1, h2, idx, wts = _merge(yrec, o_ctx, o_lat, proj, xa, xb, modseg, norm2_g,
                                  w_rec_out, w_att_out, w_out, wrt, rbias, l)
        pos, meta = _route_pos(idx)
        xs = _dispatch(meta, pos, h2, zrow)
        ys = _experts(meta, xs, w_gate_e, w_up_e, w_down_e, l)
        xa, xb = _combine(pos, wts.T, x1, modseg, final_g, ys, l, final=(l == DEPTH - 1))
        n_cu = T_CTX // UNIT
        spu = UNIT // SEQ
        hf_last = stf[:n_cu].reshape(n_cu, spu, 2, D)[:, :, 1].reshape(BATCH, D)
        hb_first = stb[:n_cu].reshape(n_cu, spu, 2, D)[:, :, 0].reshape(BATCH, D)
        new_s.append(jnp.stack([hf_last, hb_first], axis=1))

    y_prompt = xa.reshape(BATCH, SEQ, D)
    y_sample = xb.reshape(DEC_BATCH, DEC_SEQ, D)
    return (y_prompt, y_sample, caches[0], caches[1], jnp.stack(new_s, axis=1))
```

```python
import functools

import numpy as np
import jax
import jax.numpy as jnp
from jax import lax
from jax.experimental import pallas as pl
from jax.experimental.pallas import tpu as pltpu

F32 = jnp.float32
BF16 = jnp.bfloat16

D = 1024
BATCH = 16
SEQ = 256
DEPTH = 2
DEC_BATCH = 4
DEC_SEQ = 1024
PAST = 256
GRID_W = 64
N_HEADS = 16
N_KV = 4
HD = 64
RG_BLK = 128
RG_C = 8.0
N_EXP = 16
D_EXP = 512
ROPE_THETA = 10000.0
EPS = 1e-6
P_IN = 5632
TINY = float(np.finfo(np.float32).tiny)
NEG_LOG2E = -float(np.log2(np.e))

T_CTX = BATCH * SEQ
T_LAT = DEC_BATCH * DEC_SEQ
T = T_CTX + T_LAT
SEG = 256
UNIT = 1024
N_UNIT = T // UNIT
LANES = 128
SUBLANES = 8
CHUNK = UNIT // SUBLANES
CSTRIDE = CHUNK + SUBLANES

VMEM_LIMIT = 56 * 1024 * 1024


def _cp(sem):
    return pltpu.CompilerParams(dimension_semantics=sem, vmem_limit_bytes=VMEM_LIMIT)


def _split(x):
    hi = x.astype(BF16)
    lo = (x - hi.astype(F32)).astype(BF16)
    return hi, lo


def _sigmoid(x):
    return 0.5 * jnp.tanh(0.5 * x) + 0.5


NCH = D // LANES


def _store_row_tiles(ref, x):
    n = x.shape[0]
    for c in range(NCH):
        ref[pl.ds(c, n, stride=NCH), :] = x[:, c * 128:(c + 1) * 128]


def _load_row_tiles(ref, n):
    return jnp.concatenate([ref[pl.ds(c, n, stride=NCH), :] for c in range(NCH)], axis=-1)


def _dot(a, b):
    return jnp.dot(a, b, preferred_element_type=F32)


def _dot_nt(a, b):
    return lax.dot_general(a, b, (((1,), (1,)), ((), ())), preferred_element_type=F32)


def _mods_kernel(c_ref, w_ref, b_ref, o_ref):
    c = c_ref[...]
    s = c * jax.nn.sigmoid(c)
    s_hi, s_lo = _split(s)
    w_hi, w_lo = _split(w_ref[...])
    o_ref[...] = _dot(s_hi, w_hi) + _dot(s_hi, w_lo) + _dot(s_lo, w_hi) + b_ref[...]


def _mods(cvecs, w_mod, b_mod):
    tn = 1536
    return pl.pallas_call(
        _mods_kernel,
        grid=(DEPTH, 6 * D // tn),
        in_specs=[
            pl.BlockSpec((8, D), lambda l, j: (0, 0)),
            pl.BlockSpec((None, D, tn), lambda l, j: (l, 0, j)),
            pl.BlockSpec((None, 1, tn), lambda l, j: (l, 0, j)),
        ],
        out_specs=pl.BlockSpec((None, 8, tn), lambda l, j: (l, 0, j)),
        out_shape=jax.ShapeDtypeStruct((DEPTH, 8, 6 * D), F32),
        compiler_params=_cp(("arbitrary", "arbitrary")),
        name="mods",
    )(cvecs, w_mod, b_mod.reshape(DEPTH, 1, 6 * D))


def _norm_mod(x, g, shift, scale):
    ms = jnp.mean(x * x, axis=-1, keepdims=True)
    return x * lax.rsqrt(ms + EPS) * g * (1.0 + scale) + shift


def _two_part_specs(tm, n_ctx):
    return [pl.BlockSpec((tm, D), lambda i, *_: (jnp.minimum(i, n_ctx - 1), 0)),
            pl.BlockSpec((tm, D), lambda i, *_: (jnp.maximum(i - n_ctx, 0), 0))]


def _inproj_kernel(xa_ref, xb_ref, mod_ref, g_ref, w_ref, o_ref, h_ref, *, tm):
    def prologue(x_ref):
        def seg(s, carry):
            r0 = pl.multiple_of(s * SEG, SEG)
            m = mod_ref[s]
            h = _norm_mod(x_ref[pl.ds(r0, SEG), :], g_ref[...], m[0:1, :], m[1:2, :])
            h_ref[pl.ds(r0, SEG), :] = h.astype(BF16)
            return carry
        lax.fori_loop(0, tm // SEG, seg, 0)

    first = pl.program_id(1) == 0
    is_ctx = pl.program_id(0) < T_CTX // tm

    @pl.when(first & is_ctx)
    def _():
        prologue(xa_ref)

    @pl.when(first & jnp.logical_not(is_ctx))
    def _():
        prologue(xb_ref)

    o_ref[...] = _dot(h_ref[...], w_ref[...].astype(BF16)).astype(BF16)


def _inproj(xa, xb, modseg, norm_g, w_in, l):
    tm, tn = 2048, 512
    return pl.pallas_call(
        functools.partial(_inproj_kernel, tm=tm),
        grid=(T // tm, P_IN // tn),
        in_specs=_two_part_specs(tm, T_CTX // tm) + [
            pl.BlockSpec((None, tm // SEG, 8, D), lambda i, j: (l, i, 0, 0)),
            pl.BlockSpec((None, 1, D), lambda i, j: (l, 0, 0)),
            pl.BlockSpec((None, D, tn), lambda i, j: (l, 0, j)),
        ],
        out_specs=pl.BlockSpec((tm, tn), lambda i, j: (i, j)),
        out_shape=jax.ShapeDtypeStruct((T, P_IN), BF16),
        scratch_shapes=[pltpu.VMEM((tm, D), BF16)],
        compiler_params=_cp(("arbitrary", "arbitrary")),
        name="inproj",
    )(xa, xb, modseg, norm_g.reshape(DEPTH, 1, D), w_in)


REC_CW = 1024
PAD_F = 16
PAD_B = 8
GATE_ROWS = 512


def _rec_kernel(xr_ref, gate_ref, cw_ref, pv_ref, wg_ref, h0_ref,
                y_ref, stf_ref, stb_ref,
                xs_ref, af_ref, bf_ref, ab_ref, bb_ref, nat_ref, wgh_ref):
    u = pl.program_id(0)
    is_ctx = u < (T_CTX // UNIT)
    cps = jnp.where(is_ctx, SEQ // CHUNK, DEC_SEQ // CHUNK)
    nblk = REC_CW // RG_BLK

    def lanes(n):
        return slice(n * RG_BLK, (n + 1) * RG_BLK)

    def tile(r):
        return slice(8 * r, 8 * r + 8)

    for c in range(8):
        for n in range(nblk):
            nat_ref[n, c * CSTRIDE:c * CSTRIDE + CHUNK, :] = xr_ref[c * CHUNK:(c + 1) * CHUNK, lanes(n)].astype(F32)
    for r in range(CHUNK):
        for n in range(nblk):
            xs_ref[PAD_F + 8 * r:PAD_F + 8 * r + 8, lanes(n)] = nat_ref[n, pl.ds(r, 8, stride=CSTRIDE), :]
    chunk_id = lax.broadcasted_iota(jnp.int32, (8, 1), 0)
    seq_start = jnp.bitwise_and(chunk_id, cps - 1) == 0
    seq_end = jnp.bitwise_and(chunk_id, cps - 1) == cps - 1
    for j, r in ((0, CHUNK - 2), (1, CHUNK - 1)):
        prev_chunk = pltpu.roll(xs_ref[PAD_F + 8 * r:PAD_F + 8 * r + 8, :], 1, 0)
        xs_ref[tile(j), :] = jnp.where(seq_start, 0.0, prev_chunk)
    next_chunk = pltpu.roll(xs_ref[PAD_F:PAD_F + 8, :], 7, 0)
    xs_ref[PAD_F + UNIT:PAD_F + UNIT + PAD_B, :] = jnp.where(seq_end, 0.0, next_chunk)

    pv = pv_ref[...]
    cwts = cw_ref[...]
    conv_b = pv[6:7, :]

    def softplus_neg(lam):
        z = -lam
        return jnp.maximum(z, 0.0) + jnp.log1p(jnp.exp(-jnp.abs(z)))

    c4s = tuple((0.5 * RG_C) * softplus_neg(pv[4 + d:5 + d, :]) for d in range(2))
    pv_h = 0.5 * pv
    for n in range(nblk):
        wgh_ref[n] = (0.5 * wg_ref[n]).astype(BF16)
    a_refs = (af_ref, ab_ref)
    b_refs = (bf_ref, bb_ref)

    def gates(g, carry):
        base = pl.multiple_of(g * GATE_ROWS, GATE_ROWS)

        def tap(d):
            return xs_ref[pl.ds(pl.multiple_of(base + PAD_F + 8 * d, 8), GATE_ROWS), :]

        xc = conv_b + tap(-2) * cwts[0:1, :]
        xc = xc + tap(-1) * cwts[1:2, :]
        xc = xc + tap(0) * cwts[2:3, :]
        xc = xc + tap(1) * cwts[3:4, :]
        for n in range(nblk):
            ls = lanes(n)
            xn = xc[:, ls]
            hx = 0.5 * xn
            pre_h = _dot(xn.astype(BF16), wgh_ref[n])
            for d in range(2):
                th_r = jnp.tanh(pre_h[:, (2 * d) * RG_BLK:(2 * d + 1) * RG_BLK] + pv_h[2 * d:2 * d + 1, ls])
                th_i = jnp.tanh(pre_h[:, (2 * d + 1) * RG_BLK:(2 * d + 2) * RG_BLK] + pv_h[2 * d + 1:2 * d + 2, ls])
                c4 = c4s[d][:, ls]
                nla = c4 * th_r + c4
                a = jnp.exp2(nla * NEG_LOG2E)
                s = jnp.tanh(nla) * (a * a + 1.0)
                inp = (s * lax.rsqrt(jnp.maximum(s, TINY))) * (hx * th_i + hx)
                a_refs[d][pl.ds(base, GATE_ROWS), ls] = a
                b_refs[d][pl.ds(base, GATE_ROWS), ls] = inp
        return carry

    lax.fori_loop(0, UNIT // GATE_ROWS, gates, 0)

    hf = hb = jnp.zeros((8, REC_CW), F32)
    pf = pb = jnp.ones((8, REC_CW), F32)
    for r in range(CHUNK):
        rf, rb = tile(r), tile(CHUNK - 1 - r)
        a = af_ref[rf, :]
        hf = a * hf + bf_ref[rf, :]
        pf = a * pf
        bf_ref[rf, :] = hf
        af_ref[rf, :] = pf
        a = ab_ref[rb, :]
        hb = a * hb + bb_ref[rb, :]
        pb = a * pb
        bb_ref[rb, :] = hb
        ab_ref[rb, :] = pb

    h0f = h0_ref[0:1, :]
    h0b = h0_ref[1:2, :]
    cf = [h0f]
    for c in range(1, 8):
        chain = hf[c - 1:c, :] + pf[c - 1:c, :] * cf[c - 1]
        cf.append(jnp.where(jnp.bitwise_and(c, cps - 1) == 0, h0f, chain))
    cb = [None] * 8
    cb[7] = h0b
    for c in range(6, -1, -1):
        chain = hb[c + 1:c + 2, :] + pb[c + 1:c + 2, :] * cb[c + 1]
        cb[c] = jnp.where(jnp.bitwise_and(c, cps - 1) == cps - 1, h0b, chain)
    carry_f = jnp.concatenate(cf, axis=0)
    carry_b = jnp.concatenate(cb, axis=0)
    stf_ref[...] = hf + pf * carry_f
    stb_ref[...] = hb + pb * carry_b

    for r in range(CHUNK):
        h = (bf_ref[tile(r), :] + af_ref[tile(r), :] * carry_f) + (bb_ref[tile(r), :] + ab_ref[tile(r), :] * carry_b)
        for n in range(nblk):
            nat_ref[n, pl.ds(r, 8, stride=CSTRIDE), :] = h[:, lanes(n)]

    for c in range(8):
        rows = slice(c * CHUNK, (c + 1) * CHUNK)
        for n in range(nblk):
            g = gate_ref[rows, lanes(n)].astype(F32)
            h = nat_ref[n, c * CSTRIDE:c * CSTRIDE + CHUNK, :]
            y_ref[rows, lanes(n)] = (h * jax.nn.gelu(g, approximate=True)).astype(BF16)


def _rec(proj, conv_w, pvec, wg, h0, l):
    ncb = D // REC_CW
    return pl.pallas_call(
        _rec_kernel,
        grid=(N_UNIT, ncb),
        in_specs=[
            pl.BlockSpec((UNIT, REC_CW), lambda u, c: (u, c)),
            pl.BlockSpec((UNIT, REC_CW), lambda u, c: (u, ncb + c)),
            pl.BlockSpec((None, 4, REC_CW), lambda u, c: (l, 0, c)),
            pl.BlockSpec((None, 8, REC_CW), lambda u, c: (l, 0, c)),
            pl.BlockSpec((None, REC_CW // RG_BLK, RG_BLK, 4 * RG_BLK), lambda u, c: (l, c, 0, 0)),
            pl.BlockSpec((None, 2, REC_CW), lambda u, c: (u, 0, c)),
        ],
        out_specs=[
            pl.BlockSpec((UNIT, REC_CW), lambda u, c: (u, c)),
            pl.BlockSpec((None, 8, REC_CW), lambda u, c: (u, 0, c)),
            pl.BlockSpec((None, 8, REC_CW), lambda u, c: (u, 0, c)),
        ],
        out_shape=[
            jax.ShapeDtypeStruct((T, D), BF16),
            jax.ShapeDtypeStruct((N_UNIT, 8, D), F32),
            jax.ShapeDtypeStruct((N_UNIT, 8, D), F32),
        ],
        scratch_shapes=[pltpu.VMEM((PAD_F + UNIT + PAD_B, REC_CW), F32)]
        + [pltpu.VMEM((UNIT, REC_CW), F32)] * 4
        + [pltpu.VMEM((REC_CW // RG_BLK, 8 * CSTRIDE, RG_BLK), F32),
           pltpu.VMEM((REC_CW // RG_BLK, RG_BLK, 4 * RG_BLK), BF16)],
        compiler_params=_cp(("arbitrary", "arbitrary")),
        name="rec",
    )(proj, proj, conv_w, pvec, wg, h0)


def _head_norm(x, g128, bd):
    hi, lo = _split(x * x)
    ms = _dot(hi, bd) + _dot(lo, bd)
    return x * lax.rsqrt(ms + EPS) * g128


def _rope(x, cos, sin_signed):
    lane = lax.broadcasted_iota(jnp.int32, x.shape, 1)
    first_half = jnp.bitwise_and(lane, HD - 1) < HD // 2
    partner = jnp.where(first_half, pltpu.roll(x, 2 * HD - HD // 2, 1), pltpu.roll(x, HD // 2, 1))
    return x * cos + partner * sin_signed


def _with_ones(v):
    return jnp.concatenate([v, jnp.ones_like(v)], axis=-1)


def _softmax_pv(q, k, v_ext):
    s = _dot_nt(q, k)
    m = jnp.max(s, axis=-1, keepdims=True)
    p = jnp.exp2(s - m).astype(BF16)
    r = _dot(p, v_ext)
    return r[:, :HD] / r[:, HD:HD + 1]


def _attend_heads(q_ref, k, v_ext):
    return jnp.concatenate([_softmax_pv(q_ref[h], k, v_ext) for h in range(N_HEADS // N_KV)], axis=-1)


def _attn_ctx_kernel(q_ref, k_ref, v_ref, qg_ref, kg_ref, bd_ref, *rest, slab):
    o_ref, ko_ref, vo_ref = rest[-3:]
    for p in range(slab):
        ko_ref[p] = rest[0][p]
        vo_ref[p] = rest[1][p]
    g = N_HEADS // N_KV
    bd = bd_ref[...]
    odd = lax.rem(pl.program_id(1), 2) == 1
    scale = HD ** -0.5 * float(np.log2(np.e))

    kx = _head_norm(k_ref[...].astype(F32), kg_ref[...], bd)
    vx = v_ref[...].astype(F32)
    k_new = jnp.where(odd, kx[:, HD:], kx[:, :HD])
    v_new = jnp.where(odd, vx[:, HD:], vx[:, :HD])
    ko_ref[slab] = k_new
    vo_ref[slab] = v_new
    heads = []
    for j in range(g // 2):
        x = _head_norm(q_ref[:, 2 * HD * j:2 * HD * (j + 1)].astype(F32), qg_ref[...], bd) * scale
        heads += [x[:, :HD].astype(BF16), x[:, HD:].astype(BF16)]
    o = _softmax_pv(jnp.concatenate(heads, axis=0), k_new.astype(BF16), _with_ones(v_new.astype(BF16)))
    o_ref[...] = jnp.concatenate([o[h * SEQ:(h + 1) * SEQ] for h in range(g)], axis=-1).astype(BF16)


def _attn_ctx(proj, qg128, kg128, bd, l, prev_caches=()):
    g = N_HEADS // N_KV
    qcol = 2 * D // (g * HD)
    kcol = 3 * D // (2 * HD)
    vcol = kcol + N_KV // 2
    cache_spec = pl.BlockSpec((None, l + 1, None, SEQ, HD), lambda b, h: (b, 0, h, 0, 0))
    prev_specs = [pl.BlockSpec((None, l, None, SEQ, HD), lambda b, h: (b, 0, h, 0, 0))] * 2 if l else []
    cache_shape = jax.ShapeDtypeStruct((BATCH, l + 1, N_KV, SEQ, HD), F32)
    return pl.pallas_call(
        functools.partial(_attn_ctx_kernel, slab=l),
        grid=(BATCH, N_KV),
        in_specs=[
            pl.BlockSpec((SEQ, g * HD), lambda b, h: (b, qcol + h)),
            pl.BlockSpec((SEQ, 2 * HD), lambda b, h: (b, kcol + h // 2)),
            pl.BlockSpec((SEQ, 2 * HD), lambda b, h: (b, vcol + h // 2)),
            pl.BlockSpec((None, 1, 2 * HD), lambda b, h: (l, 0, 0)),
            pl.BlockSpec((None, 1, 2 * HD), lambda b, h: (l, 0, 0)),
            pl.BlockSpec((2 * HD, 2 * HD), lambda b, h: (0, 0)),
        ] + prev_specs,
        out_specs=[pl.BlockSpec((SEQ, g * HD), lambda b, h: (b, h)), cache_spec, cache_spec],
        out_shape=[jax.ShapeDtypeStruct((T_CTX, D), BF16), cache_shape, cache_shape],
        compiler_params=_cp(("arbitrary", "arbitrary")),
        name="attn_ctx",
    )(proj, proj, proj, qg128, kg128, bd, *prev_caches)


def _attn_lat_kernel(q_ref, k_ref, v_ref, qg_ref, kg_ref, cos_ref, sin_ref, bd_ref, pk_ref, pv_ref,
                     o_ref, q_s, k_s, v_s):
    g = N_HEADS // N_KV
    bd, cos, sin = bd_ref[...], cos_ref[...], sin_ref[...]
    odd = lax.rem(pl.program_id(1), 2) == 1
    scale = HD ** -0.5 * float(np.log2(np.e))

    kx = _rope(_head_norm(k_ref[...].astype(F32), kg_ref[...], bd), cos, sin)
    vx = v_ref[...]
    k_s[0:PAST, :] = pk_ref[...].astype(BF16)
    k_s[PAST:, :] = jnp.where(odd, kx[:, HD:], kx[:, :HD]).astype(BF16)
    v_s[0:PAST, :] = _with_ones(pv_ref[...].astype(BF16))
    v_s[PAST:, :] = _with_ones(jnp.where(odd, vx[:, HD:], vx[:, :HD]))
    for j in range(g // 2):
        x = _head_norm(q_ref[:, 2 * HD * j:2 * HD * (j + 1)].astype(F32), qg_ref[...], bd)
        x = _rope(x, cos, sin) * scale
        q_s[2 * j] = x[:, :HD].astype(BF16)
        q_s[2 * j + 1] = x[:, HD:].astype(BF16)

    o_ref[...] = _attend_heads(q_s, k_s[...], v_s[...]).astype(BF16)


def _attn_lat(proj, cache_k, cache_v, qg128, kg128, cos128, sin128, bd, l):
    g = N_HEADS // N_KV
    row0 = T_CTX // DEC_SEQ
    qcol = 2 * D // (g * HD)
    kcol = 3 * D // (2 * HD)
    vcol = kcol + N_KV // 2
    return pl.pallas_call(
        _attn_lat_kernel,
        grid=(DEC_BATCH, N_KV),
        in_specs=[
            pl.BlockSpec((DEC_SEQ, g * HD), lambda b, h: (row0 + b, qcol + h)),
            pl.BlockSpec((DEC_SEQ, 2 * HD), lambda b, h: (row0 + b, kcol + h // 2)),
            pl.BlockSpec((DEC_SEQ, 2 * HD), lambda b, h: (row0 + b, vcol + h // 2)),
            pl.BlockSpec((None, 1, 2 * HD), lambda b, h: (l, 0, 0)),
            pl.BlockSpec((None, 1, 2 * HD), lambda b, h: (l, 0, 0)),
            pl.BlockSpec((DEC_SEQ, 2 * HD), lambda b, h: (0, 0)),
            pl.BlockSpec((DEC_SEQ, 2 * HD), lambda b, h: (0, 0)),
            pl.BlockSpec((2 * HD, 2 * HD), lambda b, h: (0, 0)),
            pl.BlockSpec((None, None, None, PAST, HD), lambda b, h: (b, l, h, 0, 0)),
            pl.BlockSpec((None, None, None, PAST, HD), lambda b, h: (b, l, h, 0, 0)),
        ],
        out_specs=pl.BlockSpec((DEC_SEQ, g * HD), lambda b, h: (b, h)),
        out_shape=jax.ShapeDtypeStruct((T_LAT, D), BF16),
        scratch_shapes=[pltpu.VMEM((g, DEC_SEQ, HD), BF16), pltpu.VMEM((PAST + DEC_SEQ, HD), BF16),
                        pltpu.VMEM((PAST + DEC_SEQ, 2 * HD), BF16)],
        compiler_params=_cp(("arbitrary", "arbitrary")),
        name="attn_lat",
    )(proj, proj, proj, qg128, kg128, cos128, sin128, bd, cache_k, cache_v)


MERGE_TM = 512


def _route(lt, bias):
    rows = [lt[e:e + 1, :] for e in range(N_EXP)]
    m = rows[0]
    for e in range(1, N_EXP):
        m = jnp.maximum(m, rows[e])
    ex = [jnp.exp(r - m) for r in rows]
    z = ex[0]
    for e in range(1, N_EXP):
        z = z + ex[e]
    probs = [x / z for x in ex]
    sel = [probs[e] + bias[e:e + 1, :] for e in range(N_EXP)]

    def top2_sum(v):
        a, b = jnp.maximum(v[0], v[1]), jnp.minimum(v[0], v[1])
        c, d = jnp.maximum(v[2], v[3]), jnp.minimum(v[2], v[3])
        return jnp.maximum(a, c) + jnp.maximum(jnp.minimum(a, c), jnp.maximum(b, d))

    scores = [top2_sum(sel[4 * g:4 * g + 4]) for g in range(4)]
    best = jnp.zeros_like(scores[0], dtype=jnp.int32)
    best_s = scores[0]
    for g in range(1, 4):
        take = scores[g] > best_s
        best = jnp.where(take, g, best)
        best_s = jnp.where(take, scores[g], best_s)
    cs, cp = [], []
    for j in range(4):
        s_j, p_j = sel[j], probs[j]
        for g in range(1, 4):
            s_j = jnp.where(best == g, sel[4 * g + j], s_j)
            p_j = jnp.where(best == g, probs[4 * g + j], p_j)
        cs.append(s_j)
        cp.append(p_j)
    neg = jnp.full_like(cs[0], -jnp.inf)

    def argmax4(v):
        bi = jnp.zeros_like(best)
        bv = v[0]
        for j in range(1, 4):
            take = v[j] > bv
            bi = jnp.where(take, j, bi)
            bv = jnp.where(take, v[j], bv)
        return bi

    def pick(v, idx):
        out = v[0]
        for j in range(1, 4):
            out = jnp.where(idx == j, v[j], out)
        return out

    i1 = argmax4(cs)
    cs2 = [jnp.where(i1 == j, neg, cs[j]) for j in range(4)]
    i2 = argmax4(cs2)
    i2 = jnp.where((i2 == 0) & (i1 == 0), 1, i2)
    w1, w2 = pick(cp, i1), pick(cp, i2)
    den = w1 + w2
    return best * 4 + i1, best * 4 + i2, w1 / den, w2 / den


def _merge_kernel(yrec_ref, oa_ref, ob_ref, gr0_ref, gr1_ref, ga0_ref, ga1_ref, xa_ref, xb_ref, mod_ref, g2_ref,
                  wrec_ref, watt_ref, wout_ref, wrt_ref, rb_ref,
                  x1_ref, h2_ref, idx_ref, wts_ref,
                  wrec_s, watt_s, wout_s):
    @pl.when(pl.program_id(0) == 0)
    def _():
        wrec_s[...] = wrec_ref[...].astype(BF16)
        watt_s[...] = watt_ref[...].astype(BF16)
        wout_s[...] = wout_ref[...].astype(BF16)

    is_ctx = pl.program_id(0) < T_CTX // MERGE_TM
    args = (yrec_ref, gr0_ref, gr1_ref, ga0_ref, ga1_ref, mod_ref, g2_ref, wrt_ref, rb_ref,
            x1_ref, h2_ref, idx_ref, wts_ref, wrec_s, watt_s, wout_s)

    @pl.when(is_ctx)
    def _():
        _merge_body(oa_ref, xa_ref, *args)

    @pl.when(jnp.logical_not(is_ctx))
    def _():
        _merge_body(ob_ref, xb_ref, *args)


def _merge_body(oatt_ref, x_ref, yrec_ref, gr0_ref, gr1_ref, ga0_ref, ga1_ref, mod_ref, g2_ref, wrt_ref, rb_ref,
                x1_ref, h2_ref, idx_ref, wts_ref, wrec_s, watt_s, wout_s):
    half = D // 2
    b_rec = _dot(yrec_ref[...], wrec_s[...])
    b_att = _dot(oatt_ref[...], watt_s[...])
    m0 = _sigmoid(gr0_ref[...].astype(F32)) * b_rec[:, :half] + _sigmoid(ga0_ref[...].astype(F32)) * b_att[:, :half]
    m1 = _sigmoid(gr1_ref[...].astype(F32)) * b_rec[:, half:] + _sigmoid(ga1_ref[...].astype(F32)) * b_att[:, half:]
    merged = jnp.concatenate([m0, m1], axis=-1).astype(BF16)
    out = _dot(merged, wout_s[...])

    hs = []
    for s in range(MERGE_TM // SEG):
        rows = slice(s * SEG, (s + 1) * SEG)
        m = mod_ref[s]
        x1 = x_ref[rows, :] + m[2:3, :] * out[rows, :]
        x1_ref[rows, :] = x1
        h2 = _norm_mod(x1, g2_ref[...], m[3:4, :], m[4:5, :])
        hs.append(h2)
    h2 = jnp.concatenate(hs, axis=0)
    _store_row_tiles(h2_ref, h2)

    h_hi, h_lo = _split(h2)
    w_hi, w_lo = _split(wrt_ref[...])
    lt = _dot_nt(w_hi, h_hi) + _dot_nt(w_hi, h_lo) + _dot_nt(w_lo, h_hi)
    e1, e2, w1, w2 = _route(lt, rb_ref[...])
    idx_ref[...] = jnp.concatenate([e1, e2], axis=0)
    wts_ref[...] = jnp.concatenate([w1, w2], axis=0)


def _merge(yrec, o_ctx, o_lat, proj, xa, xb, modseg, norm2_g, w_rec_out, w_att_out, w_out, wrt, rbias, l):
    tm = MERGE_TM
    half = D // 2
    gcol = (3 * D + 2 * N_KV * HD) // half
    wspec = pl.BlockSpec((None, D, D), lambda i: (l, 0, 0))
    return pl.pallas_call(
        _merge_kernel,
        grid=(T // tm,),
        in_specs=[pl.BlockSpec((tm, D), lambda i: (i, 0))] + _two_part_specs(tm, T_CTX // tm) + [
            pl.BlockSpec((tm, half), lambda i: (i, gcol)),
            pl.BlockSpec((tm, half), lambda i: (i, gcol + 1)),
            pl.BlockSpec((tm, half), lambda i: (i, gcol + 2)),
            pl.BlockSpec((tm, half), lambda i: (i, gcol + 3)),
        ] + _two_part_specs(tm, T_CTX // tm) + [
            pl.BlockSpec((None, tm // SEG, 8, D), lambda i: (l, i, 0, 0)),
            pl.BlockSpec((None, 1, D), lambda i: (l, 0, 0)),
            wspec, wspec, wspec,
            pl.BlockSpec((N_EXP, D), lambda i: (0, 0)),
            pl.BlockSpec((N_EXP, 1), lambda i: (0, 0)),
        ],
        out_specs=[
            pl.BlockSpec((tm, D), lambda i: (i, 0)),
            pl.BlockSpec((tm * NCH, 128), lambda i: (i, 0)),
            pl.BlockSpec((2, tm), lambda i: (0, i)),
            pl.BlockSpec((2, tm), lambda i: (0, i)),
        ],
        out_shape=[
            jax.ShapeDtypeStruct((T, D), F32),
            jax.ShapeDtypeStruct((T * NCH, 128), F32),
            jax.ShapeDtypeStruct((2, T), jnp.int32),
            jax.ShapeDtypeStruct((2, T), F32),
        ],
        scratch_shapes=[pltpu.VMEM((D, D), BF16)] * 3,
        compiler_params=_cp(("arbitrary",)),
        name="merge",
    )(yrec, o_ctx, o_lat, proj, proj, proj, proj, xa, xb, modseg, norm2_g.reshape(DEPTH, 1, D),
      w_rec_out, w_att_out, w_out, wrt, rbias)


MOE_TM = 512
MOE_NT = 2 * T // MOE_TM + N_EXP
MOE_ROWS = MOE_NT * MOE_TM
META_TILE_E, META_CNT, META_OFF, META_END, META_NT, META_NEXT_E = 0, 1, 2, 3, 4, 5


def _pos_kernel(idx_ref, pos_ref, meta_ref):
    shift = MOE_TM.bit_length() - 1
    idx = idx_ref[...]
    eid = lax.broadcasted_iota(jnp.int32, (N_EXP, T), 0)
    m0 = eid == idx[0:1, :]
    m1 = eid == idx[1:2, :]
    member = jnp.where(m0 | m1, 1.0, 0.0)
    cnt = jnp.sum(member, axis=1, keepdims=True).astype(jnp.int32)
    ntile = jnp.right_shift(cnt + (MOE_TM - 1), shift)
    offs, acc = [], jnp.zeros((1, 1), jnp.int32)
    for e in range(N_EXP):
        offs.append(acc)
        acc = acc + ntile[e:e + 1, :]
    off_t = jnp.concatenate(offs, axis=0)
    end_t = off_t + ntile

    blk = 256
    r_i = lax.broadcasted_iota(jnp.int32, (blk, blk), 0)
    c_i = lax.broadcasted_iota(jnp.int32, (blk, blk), 1)
    upper = jnp.where(r_i <= c_i, 1.0, 0.0).astype(BF16)
    run = (off_t * MOE_TM).astype(F32)
    for j in range(T // blk):
        ls = slice(j * blk, (j + 1) * blk)
        mb = member[:, ls]
        inc = _dot(mb.astype(BF16), upper)
        dest = run + inc - mb
        pos_ref[0:1, ls] = jnp.sum(jnp.where(m0[:, ls], dest, 0.0), axis=0, keepdims=True).astype(jnp.int32)
        pos_ref[1:2, ls] = jnp.sum(jnp.where(m1[:, ls], dest, 0.0), axis=0, keepdims=True).astype(jnp.int32)
        run = run + inc[:, blk - 1:blk]

    lane = lax.broadcasted_iota(jnp.int32, (1, 128), 1)
    zero = jnp.zeros((1, 128), jnp.int32)
    tile_e, cnt_row, off_row, end_row = zero, zero, zero, zero
    for e in range(N_EXP):
        tile_e = tile_e + jnp.where(lane >= end_t[e:e + 1, :], 1, 0)
        here = lane == e
        cnt_row = jnp.where(here, cnt[e:e + 1, :], cnt_row)
        off_row = jnp.where(here, off_t[e:e + 1, :] * MOE_TM, off_row)
        end_row = jnp.where(here, end_t[e:e + 1, :] * MOE_TM, end_row)
    tile_e = jnp.minimum(tile_e, N_EXP - 1)
    nt_row = zero + acc
    next_row = zero
    nxt = jnp.full((1, 1), -1, jnp.int32)
    for e in reversed(range(N_EXP)):
        next_row = jnp.where(lane == e, nxt, next_row)
        nxt = jnp.where(cnt[e:e + 1, :] > 0, e, nxt)
    meta_ref[...] = jnp.concatenate([tile_e, cnt_row, off_row, end_row, nt_row, next_row, zero, zero], axis=0)


def _route_pos(idx):
    return pl.pallas_call(
        _pos_kernel,
        grid=(1,),
        in_specs=[pl.BlockSpec((2, T), lambda i: (0, 0))],
        out_specs=[pl.BlockSpec((2, T), lambda i: (0, 0)), pl.BlockSpec((8, 128), lambda i: (0, 0))],
        out_shape=[jax.ShapeDtypeStruct((2, T), jnp.int32), jax.ShapeDtypeStruct((8, 128), jnp.int32)],
        compiler_params=_cp(("arbitrary",)),
        name="route_pos",
    )(idx)


DISP_TM = 256


def _dispatch_kernel(meta_ref, pos_ref, h_ref, z_hbm, xs_hbm, sem):
    i = pl.program_id(0)

    def row_copy(src, src_row, dst_row):
        return pltpu.make_async_copy(src.at[pl.ds(src_row * NCH, NCH), :],
                                     xs_hbm.at[pl.ds(pl.multiple_of(dst_row * NCH, NCH), NCH), :], sem)

    for r in range(DISP_TM):
        row_copy(h_ref, r, pos_ref[0, r]).start(priority=0)
        row_copy(h_ref, r, pos_ref[1, r]).start(priority=1)

    e = jnp.minimum(i, N_EXP - 1)
    pad0 = meta_ref[META_OFF, e] + meta_ref[META_CNT, e]
    npad = jnp.where(i < N_EXP, meta_ref[META_END, e] - pad0, 0)

    def pad_copies(act):
        s = pad0
        for bit in reversed(range(MOE_TM.bit_length() - 1)):
            size = 1 << bit
            part = jnp.bitwise_and(npad, size)

            @pl.when(part != 0)
            def _():
                dst = pl.ds(pl.multiple_of(s * NCH, NCH), size * NCH)
                act(pltpu.make_async_copy(z_hbm.at[pl.ds(0, size * NCH), :], xs_hbm.at[dst, :], sem))

            s = s + part

    pad_copies(lambda c: c.start())

    for _ in range(2):
        pltpu.make_async_copy(h_ref, xs_hbm.at[pl.ds(0, DISP_TM * NCH), :], sem).wait()

    tail = meta_ref[META_NT, 0] + i
    has_tail = (i < N_EXP) & (tail < MOE_NT)

    def tail_copy():
        rows = pl.ds(pl.multiple_of(tail * (MOE_TM * NCH), MOE_TM * NCH), MOE_TM * NCH)
        return pltpu.make_async_copy(z_hbm, xs_hbm.at[rows, :], sem)

    @pl.when(has_tail)
    def _():
        tail_copy().start()

    pad_copies(lambda c: c.wait())

    @pl.when(has_tail)
    def _():
        tail_copy().wait()


def _dispatch(meta, pos, h2, zrow):
    return pl.pallas_call(
        _dispatch_kernel,
        grid_spec=pltpu.PrefetchScalarGridSpec(
            num_scalar_prefetch=1,
            grid=(T // DISP_TM,),
            in_specs=[
                pl.BlockSpec((2, DISP_TM), lambda i, meta: (0, i), memory_space=pltpu.SMEM),
                pl.BlockSpec((DISP_TM * NCH, 128), lambda i, meta: (i, 0)),
                pl.BlockSpec((MOE_TM * NCH, 128), lambda i, meta: (0, 0)),
            ],
            out_specs=pl.BlockSpec(memory_space=pl.ANY),
            scratch_shapes=[pltpu.SemaphoreType.DMA],
        ),
        out_shape=jax.ShapeDtypeStruct((MOE_ROWS * NCH, 128), F32),
        compiler_params=_cp(("arbitrary",)),
        name="dispatch",
    )(meta, pos, h2, zrow)


def _experts_kernel(meta_ref, xs_ref, wg_hbm, wu_hbm, wd_hbm, ys_ref,
                    wg_f, wu_f, wd_f, wg_s, wu_s, wd_s, sem, *, l):
    j = pl.program_id(0)
    live = j < meta_ref[META_NT, 0]
    e = meta_ref[META_TILE_E, j]
    e_prev = meta_ref[META_TILE_E, jnp.maximum(j - 1, 0)]

    def fetch(ex):
        return (pltpu.make_async_copy(wg_hbm.at[l, ex], wg_f, sem.at[0]),
                pltpu.make_async_copy(wu_hbm.at[l, ex], wu_f, sem.at[1]),
                pltpu.make_async_copy(wd_hbm.at[l, ex], wd_f, sem.at[2]))

    @pl.when(j == 0)
    def _():
        for c in fetch(e):
            c.start()

    @pl.when(live & ((j == 0) | (e != e_prev)))
    def _():
        for c, dst, src in zip(fetch(e), (wg_s, wu_s, wd_s), (wg_f, wu_f, wd_f)):
            c.wait()
            dst[...] = src[...].astype(BF16)
        nxt = meta_ref[META_NEXT_E, e]

        @pl.when(nxt >= 0)
        def _():
            for c in fetch(nxt):
                c.start()

    @pl.when(live)
    def _():
        x = _load_row_tiles(xs_ref, MOE_TM).astype(BF16)
        g = _dot(x, wg_s[...])
        u = _dot(x, wu_s[...])
        act = (g * _sigmoid(g)) * u
        _store_row_tiles(ys_ref, _dot(act.astype(BF16), wd_s[...]))

    @pl.when(jnp.logical_not(live))
    def _():
        ys_ref[...] = jnp.zeros_like(ys_ref)


def _experts(meta, xs, w_gate_e, w_up_e, w_down_e, l):
    def tile(j, meta):
        return jnp.minimum(j, meta[META_NT, 0] - 1)

    return pl.pallas_call(
        functools.partial(_experts_kernel, l=l),
        grid_spec=pltpu.PrefetchScalarGridSpec(
            num_scalar_prefetch=1,
            grid=(MOE_NT,),
            in_specs=[
                pl.BlockSpec((MOE_TM * NCH, 128), lambda j, meta: (tile(j, meta), 0)),
                pl.BlockSpec(memory_space=pl.ANY),
                pl.BlockSpec(memory_space=pl.ANY),
                pl.BlockSpec(memory_space=pl.ANY),
            ],
            out_specs=pl.BlockSpec((MOE_TM * NCH, 128), lambda j, meta: (j, 0)),
            scratch_shapes=[pltpu.VMEM((D, D_EXP), F32), pltpu.VMEM((D, D_EXP), F32), pltpu.VMEM((D_EXP, D), F32),
                            pltpu.VMEM((D, D_EXP), BF16), pltpu.VMEM((D, D_EXP), BF16), pltpu.VMEM((D_EXP, D), BF16),
                            pltpu.SemaphoreType.DMA((3,))],
        ),
        out_shape=jax.ShapeDtypeStruct((MOE_ROWS * NCH, 128), F32),
        compiler_params=_cp(("arbitrary",)),
        name="experts",
    )(meta, xs, w_gate_e, w_up_e, w_down_e)


COMB_TM = SEG


def _combine_kernel(pos_ref, w_ref, x1_ref, mod_ref, fg_ref, ys_hbm, oa_ref, ob_ref, buf, y_s, sem, *, final):
    i = pl.program_id(0)
    n = pl.num_programs(0) - 1
    n_ctx = T_CTX // COMB_TM

    for s in range(2):
        @pl.when((i < n) & (lax.rem(i, 2) == s))
        def _():
            for r in range(COMB_TM):
                for k in range(2):
                    src = pl.ds(pl.multiple_of(pos_ref[k, r] * NCH, NCH), NCH)
                    pltpu.make_async_copy(ys_hbm.at[src, :], buf.at[s, k, pl.ds(r * NCH, NCH), :],
                                          sem.at[s]).start(priority=k)

    for slot in range(2):
        @pl.when((i > 0) & (lax.rem(i - 1, 2) == slot))
        def _():
            for k in range(2):
                pltpu.make_async_copy(ys_hbm.at[pl.ds(0, COMB_TM * NCH), :], buf.at[slot, k], sem.at[slot]).wait()
            w = w_ref[...]
            y = (w[:, 0:1] * _load_row_tiles(buf.at[slot, 0], COMB_TM)
                 + w[:, 1:2] * _load_row_tiles(buf.at[slot, 1], COMB_TM))
            y_s[...] = y

    @pl.when(i > 0)
    def _():
        x = x1_ref[...] + mod_ref[5:6, :] * y_s[...]
        if final:
            ms = jnp.mean(x * x, axis=-1, keepdims=True)
            x = x * lax.rsqrt(ms + EPS) * fg_ref[...]

        @pl.when(i - 1 < n_ctx)
        def _():
            oa_ref[...] = x

        @pl.when(i - 1 >= n_ctx)
        def _():
            ob_ref[...] = x


def _combine(pos, wts_t, x1, modseg, final_g, ys, l, final):
    n = T // COMB_TM
    n_ctx = T_CTX // COMB_TM

    def done(i):
        return jnp.maximum(i - 1, 0)

    return pl.pallas_call(
        functools.partial(_combine_kernel, final=final),
        grid=(n + 1,),
        in_specs=[
            pl.BlockSpec((2, COMB_TM), lambda i: (0, jnp.minimum(i, n - 1)), memory_space=pltpu.SMEM),
            pl.BlockSpec((COMB_TM, 2), lambda i: (done(i), 0)),
            pl.BlockSpec((COMB_TM, D), lambda i: (done(i), 0)),
            pl.BlockSpec((None, None, 8, D), lambda i: (l, done(i), 0, 0)),
            pl.BlockSpec((1, D), lambda i: (0, 0)),
            pl.BlockSpec(memory_space=pl.ANY),
        ],
        out_specs=[pl.BlockSpec((COMB_TM, D), lambda i: (jnp.minimum(done(i), n_ctx - 1), 0)),
                   pl.BlockSpec((COMB_TM, D), lambda i: (jnp.maximum(done(i) - n_ctx, 0), 0))],
        out_shape=[jax.ShapeDtypeStruct((T_CTX, D), F32), jax.ShapeDtypeStruct((T_LAT, D), F32)],
        scratch_shapes=[pltpu.VMEM((2, 2, COMB_TM * NCH, 128), F32), pltpu.VMEM((COMB_TM, D), F32),
                        pltpu.SemaphoreType.DMA((2,))],
        compiler_params=_cp(("arbitrary",)),
        name="combine",
    )(pos, wts_t, x1, modseg, final_g.reshape(1, D), ys)


def _rope_tables():
    n = DEC_SEQ
    pos_row = np.repeat(np.arange(n // GRID_W, dtype=np.float32), GRID_W)
    pos_col = np.tile(np.arange(GRID_W, dtype=np.float32), n // GRID_W)
    half = HD // 2
    inv_freq = jnp.asarray(ROPE_THETA, F32) ** (-jnp.arange(0, half, 2, dtype=F32) / half)
    ang = jnp.concatenate([jnp.asarray(pos_row)[:, None] * inv_freq,
                           jnp.asarray(pos_col)[:, None] * inv_freq], axis=-1)
    cos, sin = jnp.cos(ang), jnp.sin(ang)
    cos128 = jnp.tile(cos, (1, 4))
    sin128 = jnp.tile(jnp.concatenate([-sin, sin], axis=-1), (1, 2))
    return cos128, sin128


def _head_mean_matrix():
    idx = np.arange(2 * HD)
    same = (idx[:, None] // HD) == (idx[None, :] // HD)
    return jnp.asarray(same.astype(np.float32) / HD, BF16)


_SEG_ROWS = np.array([0] * (T_CTX // SEG) + [1 + b for b in range(DEC_BATCH) for _ in range(DEC_SEQ // SEG)])


def kernel(x_prompt, x_sample, cache_k, cache_v, state_rec, c, c_ctx, w_mod, b_mod, norm1_g, norm2_g, w_in, conv_w, conv_b, rg_wa, rg_ba, rg_wx, rg_bx, rg_lambda, q_norm_g, k_norm_g, w_rec_out, w_att_out, w_out, w_router, router_bias, w_gate_e, w_up_e, w_down_e, final_g):
    xa, xb = x_prompt.reshape(T_CTX, D), x_sample.reshape(T_LAT, D)

    cvecs = jnp.concatenate([c_ctx[None, :], c, jnp.zeros((3, D), F32)], axis=0)
    mods = _mods(cvecs, w_mod, b_mod).reshape(DEPTH, 8, 6, D)
    modseg = jnp.pad(mods[:, _SEG_ROWS], ((0, 0), (0, 0), (0, 2), (0, 0)))

    cos128, sin128 = _rope_tables()
    bd = _head_mean_matrix()
    qg128 = jnp.tile(q_norm_g, (1, 2)).reshape(DEPTH, 1, 2 * HD)
    kg128 = jnp.tile(k_norm_g, (1, 2)).reshape(DEPTH, 1, 2 * HD)
    wg = jnp.concatenate([rg_wa[:, 0], rg_wx[:, 0], rg_wa[:, 1], rg_wx[:, 1]], axis=-1)
    pvec = jnp.stack([rg_ba[:, 0], rg_bx[:, 0], rg_ba[:, 1], rg_bx[:, 1],
                      rg_lambda[:, 0], rg_lambda[:, 1], conv_b, jnp.zeros_like(conv_b)], axis=1)
    wrt = w_router.T
    rbias = router_bias.reshape(N_EXP, 1)
    zrow = jnp.zeros((MOE_TM * NCH, 128), F32)

    caches, new_s = (), []
    for l in range(DEPTH):
        proj = _inproj(xa, xb, modseg, norm1_g, w_in, l)
        h0 = jnp.concatenate([jnp.zeros((T_CTX // UNIT, 2, D), F32), state_rec[:, l]], axis=0)
        yrec, stf, stb = _rec(proj, conv_w, pvec, wg, h0, l)
        o_ctx, kc, vc = _attn_ctx(proj, qg128, kg128, bd, l, prev_caches=caches)
        caches = (kc, vc)
        o_lat = _attn_lat(proj, cache_k, cache_v, qg128, kg128, cos128, sin128, bd, l)
        x1, h2, idx, wts = _merge(yrec, o_ctx, o_lat, proj, xa, xb, modseg, norm2_g,
                                  w_rec_out, w_att_out, w_out, wrt, rbias, l)
        pos, meta = _route_pos(idx)
        xs = _dispatch(meta, pos, h2, zrow)
        ys = _experts(meta, xs, w_gate_e, w_up_e, w_down_e, l)
        xa, xb = _combine(pos, wts.T, x1, modseg, final_g, ys, l, final=(l == DEPTH - 1))
        n_cu = T_CTX // UNIT
        spu = UNIT // SEQ
        hf_last = stf[:n_cu].reshape(n_cu, spu, 2, D)[:, :, 1].reshape(BATCH, D)
        hb_first = stb[:n_cu].reshape(n_cu, spu, 2, D)[:, :, 0].reshape(BATCH, D)
        new_s.append(jnp.stack([hf_last, hb_first], axis=1))

    y_prompt = xa.reshape(BATCH, SEQ, D)
    y_sample = xb.reshape(DEC_BATCH, DEC_SEQ, D)
    return (y_prompt, y_sample, caches[0], caches[1], jnp.stack(new_s, axis=1))
```

```python
import functools

import numpy as np
import jax
import jax.numpy as jnp
from jax import lax
from jax.experimental import pallas as pl
from jax.experimental.pallas import tpu as pltpu

F32 = jnp.float32
BF16 = jnp.bfloat16

D = 1024
BATCH = 16
SEQ = 256
DEPTH = 2
DEC_BATCH = 4
DEC_SEQ = 1024
PAST = 256
GRID_W = 64
N_HEADS = 16
N_KV = 4
HD = 64
RG_BLK = 128
RG_C = 8.0
N_EXP = 16
D_EXP = 512
ROPE_THETA = 10000.0
EPS = 1e-6
P_IN = 5632
TINY = float(np.finfo(np.float32).tiny)
NEG_LOG2E = -float(np.log2(np.e))

T_CTX = BATCH * SEQ
T_LAT = DEC_BATCH * DEC_SEQ
T = T_CTX + T_LAT
SEG = 256
UNIT = 1024
N_UNIT = T // UNIT
LANES = 128
SUBLANES = 8
CHUNK = UNIT // SUBLANES
CSTRIDE = CHUNK + SUBLANES

VMEM_LIMIT = 56 * 1024 * 1024


def _cp(sem):
    return pltpu.CompilerParams(dimension_semantics=sem, vmem_limit_bytes=VMEM_LIMIT)


def _split(x):
    hi = x.astype(BF16)
    lo = (x - hi.astype(F32)).astype(BF16)
    return hi, lo


def _sigmoid(x):
    return 0.5 * jnp.tanh(0.5 * x) + 0.5


NCH = D // LANES


def _store_row_tiles(ref, x):
    n = x.shape[0]
    for c in range(NCH):
        ref[pl.ds(c, n, stride=NCH), :] = x[:, c * 128:(c + 1) * 128]


def _load_row_tiles(ref, n):
    return jnp.concatenate([ref[pl.ds(c, n, stride=NCH), :] for c in range(NCH)], axis=-1)


def _dot(a, b):
    return jnp.dot(a, b, preferred_element_type=F32)


def _dot_nt(a, b):
    return lax.dot_general(a, b, (((1,), (1,)), ((), ())), preferred_element_type=F32)


def _mods_kernel(c_ref, w_ref, b_ref, o_ref):
    c = c_ref[...]
    s = c * jax.nn.sigmoid(c)
    s_hi, s_lo = _split(s)
    w_hi, w_lo = _split(w_ref[...])
    o_ref[...] = _dot(s_hi, w_hi) + _dot(s_hi, w_lo) + _dot(s_lo, w_hi) + b_ref[...]


def _mods(cvecs, w_mod, b_mod):
    tn = 1536
    return pl.pallas_call(
        _mods_kernel,
        grid=(DEPTH, 6 * D // tn),
        in_specs=[
            pl.BlockSpec((8, D), lambda l, j: (0, 0)),
            pl.BlockSpec((None, D, tn), lambda l, j: (l, 0, j)),
            pl.BlockSpec((None, 1, tn), lambda l, j: (l, 0, j)),
        ],
        out_specs=pl.BlockSpec((None, 8, tn), lambda l, j: (l, 0, j)),
        out_shape=jax.ShapeDtypeStruct((DEPTH, 8, 6 * D), F32),
        compiler_params=_cp(("arbitrary", "arbitrary")),
        name="mods",
    )(cvecs, w_mod, b_mod.reshape(DEPTH, 1, 6 * D))


def _norm_mod(x, g, shift, scale):
    ms = jnp.mean(x * x, axis=-1, keepdims=True)
    return x * lax.rsqrt(ms + EPS) * g * (1.0 + scale) + shift


def _two_part_specs(tm, n_ctx):
    return [pl.BlockSpec((tm, D), lambda i, *_: (jnp.minimum(i, n_ctx - 1), 0)),
            pl.BlockSpec((tm, D), lambda i, *_: (jnp.maximum(i - n_ctx, 0), 0))]


def _inproj_kernel(xa_ref, xb_ref, mod_ref, g_ref, w_ref, o_ref, h_ref, *, tm):
    def prologue(x_ref):
        def seg(s, carry):
            r0 = pl.multiple_of(s * SEG, SEG)
            m = mod_ref[s]
            h = _norm_mod(x_ref[pl.ds(r0, SEG), :], g_ref[...], m[0:1, :], m[1:2, :])
            h_ref[pl.ds(r0, SEG), :] = h.astype(BF16)
            return carry
        lax.fori_loop(0, tm // SEG, seg, 0)

    first = pl.program_id(1) == 0
    is_ctx = pl.program_id(0) < T_CTX // tm

    @pl.when(first & is_ctx)
    def _():
        prologue(xa_ref)

    @pl.when(first & jnp.logical_not(is_ctx))
    def _():
        prologue(xb_ref)

    o_ref[...] = _dot(h_ref[...], w_ref[...].astype(BF16)).astype(BF16)


def _inproj(xa, xb, modseg, norm_g, w_in, l):
    tm, tn = 2048, 512
    return pl.pallas_call(
        functools.partial(_inproj_kernel, tm=tm),
        grid=(T // tm, P_IN // tn),
        in_specs=_two_part_specs(tm, T_CTX // tm) + [
            pl.BlockSpec((None, tm // SEG, 8, D), lambda i, j: (l, i, 0, 0)),
            pl.BlockSpec((None, 1, D), lambda i, j: (l, 0, 0)),
            pl.BlockSpec((None, D, tn), lambda i, j: (l, 0, j)),
        ],
        out_specs=pl.BlockSpec((tm, tn), lambda i, j: (i, j)),
        out_shape=jax.ShapeDtypeStruct((T, P_IN), BF16),
        scratch_shapes=[pltpu.VMEM((tm, D), BF16)],
        compiler_params=_cp(("arbitrary", "arbitrary")),
        name="inproj",
    )(xa, xb, modseg, norm_g.reshape(DEPTH, 1, D), w_in)


REC_CW = 512
PAD_F = 16
PAD_B = 8
GATE_ROWS = 512


def _rec_kernel(xr_ref, gate_ref, cw_ref, pv_ref, wg_ref, h0_ref,
                y_ref, stf_ref, stb_ref,
                xs_ref, af_ref, bf_ref, ab_ref, bb_ref, nat_ref, wgh_ref):
    u = pl.program_id(0)
    is_ctx = u < (T_CTX // UNIT)
    cps = jnp.where(is_ctx, SEQ // CHUNK, DEC_SEQ // CHUNK)
    nblk = REC_CW // RG_BLK

    def lanes(n):
        return slice(n * RG_BLK, (n + 1) * RG_BLK)

    def tile(r):
        return slice(8 * r, 8 * r + 8)

    for c in range(8):
        for n in range(nblk):
            nat_ref[n, c * CSTRIDE:c * CSTRIDE + CHUNK, :] = xr_ref[c * CHUNK:(c + 1) * CHUNK, lanes(n)].astype(F32)
    for r in range(CHUNK):
        for n in range(nblk):
            xs_ref[PAD_F + 8 * r:PAD_F + 8 * r + 8, lanes(n)] = nat_ref[n, pl.ds(r, 8, stride=CSTRIDE), :]
    chunk_id = lax.broadcasted_iota(jnp.int32, (8, 1), 0)
    seq_start = jnp.bitwise_and(chunk_id, cps - 1) == 0
    seq_end = jnp.bitwise_and(chunk_id, cps - 1) == cps - 1
    for j, r in ((0, CHUNK - 2), (1, CHUNK - 1)):
        prev_chunk = pltpu.roll(xs_ref[PAD_F + 8 * r:PAD_F + 8 * r + 8, :], 1, 0)
        xs_ref[tile(j), :] = jnp.where(seq_start, 0.0, prev_chunk)
    next_chunk = pltpu.roll(xs_ref[PAD_F:PAD_F + 8, :], 7, 0)
    xs_ref[PAD_F + UNIT:PAD_F + UNIT + PAD_B, :] = jnp.where(seq_end, 0.0, next_chunk)

    pv = pv_ref[...]
    cwts = cw_ref[...]
    conv_b = pv[6:7, :]

    def softplus_neg(lam):
        z = -lam
        return jnp.maximum(z, 0.0) + jnp.log1p(jnp.exp(-jnp.abs(z)))

    c4s = tuple((0.5 * RG_C) * softplus_neg(pv[4 + d:5 + d, :]) for d in range(2))
    pv_h = 0.5 * pv
    for n in range(nblk):
        wgh_ref[n] = (0.5 * wg_ref[n]).astype(BF16)
    a_refs = (af_ref, ab_ref)
    b_refs = (bf_ref, bb_ref)

    def gates(g, carry):
        base = pl.multiple_of(g * GATE_ROWS, GATE_ROWS)

        def tap(d):
            return xs_ref[pl.ds(pl.multiple_of(base + PAD_F + 8 * d, 8), GATE_ROWS), :]

        xc = conv_b + tap(-2) * cwts[0:1, :]
        xc = xc + tap(-1) * cwts[1:2, :]
        xc = xc + tap(0) * cwts[2:3, :]
        xc = xc + tap(1) * cwts[3:4, :]
        for n in range(nblk):
            ls = lanes(n)
            xn = xc[:, ls]
            hx = 0.5 * xn
            pre_h = _dot(xn.astype(BF16), wgh_ref[n])
            for d in range(2):
                th_r = jnp.tanh(pre_h[:, (2 * d) * RG_BLK:(2 * d + 1) * RG_BLK] + pv_h[2 * d:2 * d + 1, ls])
                th_i = jnp.tanh(pre_h[:, (2 * d + 1) * RG_BLK:(2 * d + 2) * RG_BLK] + pv_h[2 * d + 1:2 * d + 2, ls])
                c4 = c4s[d][:, ls]
                nla = c4 * th_r + c4
                a = jnp.exp2(nla * NEG_LOG2E)
                s = jnp.tanh(nla) * (a * a + 1.0)
                inp = (s * lax.rsqrt(jnp.maximum(s, TINY))) * (hx * th_i + hx)
                a_refs[d][pl.ds(base, GATE_ROWS), ls] = a
                b_refs[d][pl.ds(base, GATE_ROWS), ls] = inp
        return carry

    lax.fori_loop(0, UNIT // GATE_ROWS, gates, 0)

    hf = hb = jnp.zeros((8, REC_CW), F32)
    pf = pb = jnp.ones((8, REC_CW), F32)
    for r in range(CHUNK):
        rf, rb = tile(r), tile(CHUNK - 1 - r)
        a = af_ref[rf, :]
        hf = a * hf + bf_ref[rf, :]
        pf = a * pf
        bf_ref[rf, :] = hf
        af_ref[rf, :] = pf
        a = ab_ref[rb, :]
        hb = a * hb + bb_ref[rb, :]
        pb = a * pb
        bb_ref[rb, :] = hb
        ab_ref[rb, :] = pb

    h0f = h0_ref[0:1, :]
    h0b = h0_ref[1:2, :]
    cf = [h0f]
    for c in range(1, 8):
        chain = hf[c - 1:c, :] + pf[c - 1:c, :] * cf[c - 1]
        cf.append(jnp.where(jnp.bitwise_and(c, cps - 1) == 0, h0f, chain))
    cb = [None] * 8
    cb[7] = h0b
    for c in range(6, -1, -1):
        chain = hb[c + 1:c + 2, :] + pb[c + 1:c + 2, :] * cb[c + 1]
        cb[c] = jnp.where(jnp.bitwise_and(c, cps - 1) == cps - 1, h0b, chain)
    carry_f = jnp.concatenate(cf, axis=0)
    carry_b = jnp.concatenate(cb, axis=0)
    stf_ref[...] = hf + pf * carry_f
    stb_ref[...] = hb + pb * carry_b

    for r in range(CHUNK):
        h = (bf_ref[tile(r), :] + af_ref[tile(r), :] * carry_f) + (bb_ref[tile(r), :] + ab_ref[tile(r), :] * carry_b)
        for n in range(nblk):
            nat_ref[n, pl.ds(r, 8, stride=CSTRIDE), :] = h[:, lanes(n)]

    for c in range(8):
        rows = slice(c * CHUNK, (c + 1) * CHUNK)
        for n in range(nblk):
            g = gate_ref[rows, lanes(n)].astype(F32)
            h = nat_ref[n, c * CSTRIDE:c * CSTRIDE + CHUNK, :]
            y_ref[rows, lanes(n)] = (h * jax.nn.gelu(g, approximate=True)).astype(BF16)


def _rec(proj, conv_w, pvec, wg, h0, l):
    ncb = D // REC_CW
    return pl.pallas_call(
        _rec_kernel,
        grid=(N_UNIT, ncb),
        in_specs=[
            pl.BlockSpec((UNIT, REC_CW), lambda u, c: (u, c)),
            pl.BlockSpec((UNIT, REC_CW), lambda u, c: (u, ncb + c)),
            pl.BlockSpec((None, 4, REC_CW), lambda u, c: (l, 0, c)),
            pl.BlockSpec((None, 8, REC_CW), lambda u, c: (l, 0, c)),
            pl.BlockSpec((None, REC_CW // RG_BLK, RG_BLK, 4 * RG_BLK), lambda u, c: (l, c, 0, 0)),
            pl.BlockSpec((None, 2, REC_CW), lambda u, c: (u, 0, c)),
        ],
        out_specs=[
            pl.BlockSpec((UNIT, REC_CW), lambda u, c: (u, c)),
            pl.BlockSpec((None, 8, REC_CW), lambda u, c: (u, 0, c)),
            pl.BlockSpec((None, 8, REC_CW), lambda u, c: (u, 0, c)),
        ],
        out_shape=[
            jax.ShapeDtypeStruct((T, D), BF16),
            jax.ShapeDtypeStruct((N_UNIT, 8, D), F32),
            jax.ShapeDtypeStruct((N_UNIT, 8, D), F32),
        ],
        scratch_shapes=[pltpu.VMEM((PAD_F + UNIT + PAD_B, REC_CW), F32)]
        + [pltpu.VMEM((UNIT, REC_CW), F32)] * 4
        + [pltpu.VMEM((REC_CW // RG_BLK, 8 * CSTRIDE, RG_BLK), F32),
           pltpu.VMEM((REC_CW // RG_BLK, RG_BLK, 4 * RG_BLK), BF16)],
        compiler_params=_cp(("arbitrary", "arbitrary")),
        name="rec",
    )(proj, proj, conv_w, pvec, wg, h0)


def _head_norm(x, g128, bd):
    hi, lo = _split(x * x)
    ms = _dot(hi, bd) + _dot(lo, bd)
    return x * lax.rsqrt(ms + EPS) * g128


def _rope(x, cos, sin_signed):
    lane = lax.broadcasted_iota(jnp.int32, x.shape, 1)
    first_half = jnp.bitwise_and(lane, HD - 1) < HD // 2
    partner = jnp.where(first_half, pltpu.roll(x, 2 * HD - HD // 2, 1), pltpu.roll(x, HD // 2, 1))
    return x * cos + partner * sin_signed


def _with_ones(v):
    return jnp.concatenate([v, jnp.ones_like(v)], axis=-1)


def _softmax_pv(q, k, v_ext):
    s = _dot_nt(q, k)
    m = jnp.max(s, axis=-1, keepdims=True)
    p = jnp.exp2(s - m).astype(BF16)
    r = _dot(p, v_ext)
    return r[:, :HD] / r[:, HD:HD + 1]


def _attend_heads(q_ref, k, v_ext):
    return jnp.concatenate([_softmax_pv(q_ref[h], k, v_ext) for h in range(N_HEADS // N_KV)], axis=-1)


def _attn_ctx_kernel(q_ref, k_ref, v_ref, qg_ref, kg_ref, bd_ref, *rest, slab):
    o_ref, ko_ref, vo_ref = rest[-3:]
    for p in range(slab):
        ko_ref[p] = rest[0][p]
        vo_ref[p] = rest[1][p]
    g = N_HEADS // N_KV
    bd = bd_ref[...]
    odd = lax.rem(pl.program_id(1), 2) == 1
    scale = HD ** -0.5 * float(np.log2(np.e))

    kx = _head_norm(k_ref[...].astype(F32), kg_ref[...], bd)
    vx = v_ref[...].astype(F32)
    k_new = jnp.where(odd, kx[:, HD:], kx[:, :HD])
    v_new = jnp.where(odd, vx[:, HD:], vx[:, :HD])
    ko_ref[slab] = k_new
    vo_ref[slab] = v_new
    heads = []
    for j in range(g // 2):
        x = _head_norm(q_ref[:, 2 * HD * j:2 * HD * (j + 1)].astype(F32), qg_ref[...], bd) * scale
        heads += [x[:, :HD].astype(BF16), x[:, HD:].astype(BF16)]
    o = _softmax_pv(jnp.concatenate(heads, axis=0), k_new.astype(BF16), _with_ones(v_new.astype(BF16)))
    o_ref[...] = jnp.concatenate([o[h * SEQ:(h + 1) * SEQ] for h in range(g)], axis=-1).astype(BF16)


def _attn_ctx(proj, qg128, kg128, bd, l, prev_caches=()):
    g = N_HEADS // N_KV
    qcol = 2 * D // (g * HD)
    kcol = 3 * D // (2 * HD)
    vcol = kcol + N_KV // 2
    cache_spec = pl.BlockSpec((None, l + 1, None, SEQ, HD), lambda b, h: (b, 0, h, 0, 0))
    prev_specs = [pl.BlockSpec((None, l, None, SEQ, HD), lambda b, h: (b, 0, h, 0, 0))] * 2 if l else []
    cache_shape = jax.ShapeDtypeStruct((BATCH, l + 1, N_KV, SEQ, HD), F32)
    return pl.pallas_call(
        functools.partial(_attn_ctx_kernel, slab=l),
        grid=(BATCH, N_KV),
        in_specs=[
            pl.BlockSpec((SEQ, g * HD), lambda b, h: (b, qcol + h)),
            pl.BlockSpec((SEQ, 2 * HD), lambda b, h: (b, kcol + h // 2)),
            pl.BlockSpec((SEQ, 2 * HD), lambda b, h: (b, vcol + h // 2)),
            pl.BlockSpec((None, 1, 2 * HD), lambda b, h: (l, 0, 0)),
            pl.BlockSpec((None, 1, 2 * HD), lambda b, h: (l, 0, 0)),
            pl.BlockSpec((2 * HD, 2 * HD), lambda b, h: (0, 0)),
        ] + prev_specs,
        out_specs=[pl.BlockSpec((SEQ, g * HD), lambda b, h: (b, h)), cache_spec, cache_spec],
        out_shape=[jax.ShapeDtypeStruct((T_CTX, D), BF16), cache_shape, cache_shape],
        compiler_params=_cp(("arbitrary", "arbitrary")),
        name="attn_ctx",
    )(proj, proj, proj, qg128, kg128, bd, *prev_caches)


def _attn_lat_kernel(q_ref, k_ref, v_ref, qg_ref, kg_ref, cos_ref, sin_ref, bd_ref, pk_ref, pv_ref,
                     o_ref, q_s, k_s, v_s):
    g = N_HEADS // N_KV
    bd, cos, sin = bd_ref[...], cos_ref[...], sin_ref[...]
    odd = lax.rem(pl.program_id(1), 2) == 1
    scale = HD ** -0.5 * float(np.log2(np.e))

    kx = _rope(_head_norm(k_ref[...].astype(F32), kg_ref[...], bd), cos, sin)
    vx = v_ref[...]
    k_s[0:PAST, :] = pk_ref[...].astype(BF16)
    k_s[PAST:, :] = jnp.where(odd, kx[:, HD:], kx[:, :HD]).astype(BF16)
    v_s[0:PAST, :] = _with_ones(pv_ref[...].astype(BF16))
    v_s[PAST:, :] = _with_ones(jnp.where(odd, vx[:, HD:], vx[:, :HD]))
    for j in range(g // 2):
        x = _head_norm(q_ref[:, 2 * HD * j:2 * HD * (j + 1)].astype(F32), qg_ref[...], bd)
        x = _rope(x, cos, sin) * scale
        q_s[2 * j] = x[:, :HD].astype(BF16)
        q_s[2 * j + 1] = x[:, HD:].astype(BF16)

    o_ref[...] = _attend_heads(q_s, k_s[...], v_s[...]).astype(BF16)


def _attn_lat(proj, cache_k, cache_v, qg128, kg128, cos128, sin128, bd, l):
    g = N_HEADS // N_KV
    row0 = T_CTX // DEC_SEQ
    qcol = 2 * D // (g * HD)
    kcol = 3 * D // (2 * HD)
    vcol = kcol + N_KV // 2
    return pl.pallas_call(
        _attn_lat_kernel,
        grid=(DEC_BATCH, N_KV),
        in_specs=[
            pl.BlockSpec((DEC_SEQ, g * HD), lambda b, h: (row0 + b, qcol + h)),
            pl.BlockSpec((DEC_SEQ, 2 * HD), lambda b, h: (row0 + b, kcol + h // 2)),
            pl.BlockSpec((DEC_SEQ, 2 * HD), lambda b, h: (row0 + b, vcol + h // 2)),
            pl.BlockSpec((None, 1, 2 * HD), lambda b, h: (l, 0, 0)),
            pl.BlockSpec((None, 1, 2 * HD), lambda b, h: (l, 0, 0)),
            pl.BlockSpec((DEC_SEQ, 2 * HD), lambda b, h: (0, 0)),
            pl.BlockSpec((DEC_SEQ, 2 * HD), lambda b, h: (0, 0)),
            pl.BlockSpec((2 * HD, 2 * HD), lambda b, h: (0, 0)),
            pl.BlockSpec((None, None, None, PAST, HD), lambda b, h: (b, l, h, 0, 0)),
            pl.BlockSpec((None, None, None, PAST, HD), lambda b, h: (b, l, h, 0, 0)),
        ],
        out_specs=pl.BlockSpec((DEC_SEQ, g * HD), lambda b, h: (b, h)),
        out_shape=jax.ShapeDtypeStruct((T_LAT, D), BF16),
        scratch_shapes=[pltpu.VMEM((g, DEC_SEQ, HD), BF16), pltpu.VMEM((PAST + DEC_SEQ, HD), BF16),
                        pltpu.VMEM((PAST + DEC_SEQ, 2 * HD), BF16)],
        compiler_params=_cp(("arbitrary", "arbitrary")),
        name="attn_lat",
    )(proj, proj, proj, qg128, kg128, cos128, sin128, bd, cache_k, cache_v)


MERGE_TM = 512


def _route(lt, bias):
    rows = [lt[e:e + 1, :] for e in range(N_EXP)]
    m = rows[0]
    for e in range(1, N_EXP):
        m = jnp.maximum(m, rows[e])
    ex = [jnp.exp(r - m) for r in rows]
    z = ex[0]
    for e in range(1, N_EXP):
        z = z + ex[e]
    probs = [x / z for x in ex]
    sel = [probs[e] + bias[e:e + 1, :] for e in range(N_EXP)]

    def top2_sum(v):
        a, b = jnp.maximum(v[0], v[1]), jnp.minimum(v[0], v[1])
        c, d = jnp.maximum(v[2], v[3]), jnp.minimum(v[2], v[3])
        return jnp.maximum(a, c) + jnp.maximum(jnp.minimum(a, c), jnp.maximum(b, d))

    scores = [top2_sum(sel[4 * g:4 * g + 4]) for g in range(4)]
    best = jnp.zeros_like(scores[0], dtype=jnp.int32)
    best_s = scores[0]
    for g in range(1, 4):
        take = scores[g] > best_s
        best = jnp.where(take, g, best)
        best_s = jnp.where(take, scores[g], best_s)
    cs, cp = [], []
    for j in range(4):
        s_j, p_j = sel[j], probs[j]
        for g in range(1, 4):
            s_j = jnp.where(best == g, sel[4 * g + j], s_j)
            p_j = jnp.where(best == g, probs[4 * g + j], p_j)
        cs.append(s_j)
        cp.append(p_j)
    neg = jnp.full_like(cs[0], -jnp.inf)

    def argmax4(v):
        bi = jnp.zeros_like(best)
        bv = v[0]
        for j in range(1, 4):
            take = v[j] > bv
            bi = jnp.where(take, j, bi)
            bv = jnp.where(take, v[j], bv)
        return bi

    def pick(v, idx):
        out = v[0]
        for j in range(1, 4):
            out = jnp.where(idx == j, v[j], out)
        return out

    i1 = argmax4(cs)
    cs2 = [jnp.where(i1 == j, neg, cs[j]) for j in range(4)]
    i2 = argmax4(cs2)
    i2 = jnp.where((i2 == 0) & (i1 == 0), 1, i2)
    w1, w2 = pick(cp, i1), pick(cp, i2)
    den = w1 + w2
    return best * 4 + i1, best * 4 + i2, w1 / den, w2 / den


def _merge_kernel(yrec_ref, oa_ref, ob_ref, gr0_ref, gr1_ref, ga0_ref, ga1_ref, xa_ref, xb_ref, mod_ref, g2_ref,
                  wrec_ref, watt_ref, wout_ref, wrt_ref, rb_ref,
                  x1_ref, h2_ref, idx_ref, wts_ref,
                  wrec_s, watt_s, wout_s):
    @pl.when(pl.program_id(0) == 0)
    def _():
        wrec_s[...] = wrec_ref[...].astype(BF16)
        watt_s[...] = watt_ref[...].astype(BF16)
        wout_s[...] = wout_ref[...].astype(BF16)

    is_ctx = pl.program_id(0) < T_CTX // MERGE_TM
    args = (yrec_ref, gr0_ref, gr1_ref, ga0_ref, ga1_ref, mod_ref, g2_ref, wrt_ref, rb_ref,
            x1_ref, h2_ref, idx_ref, wts_ref, wrec_s, watt_s, wout_s)

    @pl.when(is_ctx)
    def _():
        _merge_body(oa_ref, xa_ref, *args)

    @pl.when(jnp.logical_not(is_ctx))
    def _():
        _merge_body(ob_ref, xb_ref, *args)


def _merge_body(oatt_ref, x_ref, yrec_ref, gr0_ref, gr1_ref, ga0_ref, ga1_ref, mod_ref, g2_ref, wrt_ref, rb_ref,
                x1_ref, h2_ref, idx_ref, wts_ref, wrec_s, watt_s, wout_s):
    half = D // 2
    b_rec = _dot(yrec_ref[...], wrec_s[...])
    b_att = _dot(oatt_ref[...], watt_s[...])
    m0 = _sigmoid(gr0_ref[...].astype(F32)) * b_rec[:, :half] + _sigmoid(ga0_ref[...].astype(F32)) * b_att[:, :half]
    m1 = _sigmoid(gr1_ref[...].astype(F32)) * b_rec[:, half:] + _sigmoid(ga1_ref[...].astype(F32)) * b_att[:, half:]
    merged = jnp.concatenate([m0, m1], axis=-1).astype(BF16)
    out = _dot(merged, wout_s[...])

    hs = []
    for s in range(MERGE_TM // SEG):
        rows = slice(s * SEG, (s + 1) * SEG)
        m = mod_ref[s]
        x1 = x_ref[rows, :] + m[2:3, :] * out[rows, :]
        x1_ref[rows, :] = x1
        h2 = _norm_mod(x1, g2_ref[...], m[3:4, :], m[4:5, :])
        hs.append(h2)
    h2 = jnp.concatenate(hs, axis=0)
    _store_row_tiles(h2_ref, h2)

    h_hi, h_lo = _split(h2)
    w_hi, w_lo = _split(wrt_ref[...])
    lt = _dot_nt(w_hi, h_hi) + _dot_nt(w_hi, h_lo) + _dot_nt(w_lo, h_hi)
    e1, e2, w1, w2 = _route(lt, rb_ref[...])
    idx_ref[...] = jnp.concatenate([e1, e2], axis=0)
    wts_ref[...] = jnp.concatenate([w1, w2], axis=0)


def _merge(yrec, o_ctx, o_lat, proj, xa, xb, modseg, norm2_g, w_rec_out, w_att_out, w_out, wrt, rbias, l):
    tm = MERGE_TM
    half = D // 2
    gcol = (3 * D + 2 * N_KV * HD) // half
    wspec = pl.BlockSpec((None, D, D), lambda i: (l, 0, 0))
    return pl.pallas_call(
        _merge_kernel,
        grid=(T // tm,),
        in_specs=[pl.BlockSpec((tm, D), lambda i: (i, 0))] + _two_part_specs(tm, T_CTX // tm) + [
            pl.BlockSpec((tm, half), lambda i: (i, gcol)),
            pl.BlockSpec((tm, half), lambda i: (i, gcol + 1)),
            pl.BlockSpec((tm, half), lambda i: (i, gcol + 2)),
            pl.BlockSpec((tm, half), lambda i: (i, gcol + 3)),
        ] + _two_part_specs(tm, T_CTX // tm) + [
            pl.BlockSpec((None, tm // SEG, 8, D), lambda i: (l, i, 0, 0)),
            pl.BlockSpec((None, 1, D), lambda i: (l, 0, 0)),
            wspec, wspec, wspec,
            pl.BlockSpec((N_EXP, D), lambda i: (0, 0)),
            pl.BlockSpec((N_EXP, 1), lambda i: (0, 0)),
        ],
        out_specs=[
            pl.BlockSpec((tm, D), lambda i: (i, 0)),
            pl.BlockSpec((tm * NCH, 128), lambda i: (i, 0)),
            pl.BlockSpec((2, tm), lambda i: (0, i)),
            pl.BlockSpec((2, tm), lambda i: (0, i)),
        ],
        out_shape=[
            jax.ShapeDtypeStruct((T, D), F32),
            jax.ShapeDtypeStruct((T * NCH, 128), F32),
            jax.ShapeDtypeStruct((2, T), jnp.int32),
            jax.ShapeDtypeStruct((2, T), F32),
        ],
        scratch_shapes=[pltpu.VMEM((D, D), BF16)] * 3,
        compiler_params=_cp(("arbitrary",)),
        name="merge",
    )(yrec, o_ctx, o_lat, proj, proj, proj, proj, xa, xb, modseg, norm2_g.reshape(DEPTH, 1, D),
      w_rec_out, w_att_out, w_out, wrt, rbias)


MOE_TM = 512
MOE_NT = 2 * T // MOE_TM + N_EXP
MOE_ROWS = MOE_NT * MOE_TM
META_TILE_E, META_CNT, META_OFF, META_END, META_NT, META_NEXT_E = 0, 1, 2, 3, 4, 5


def _pos_kernel(idx_ref, pos_ref, meta_ref):
    shift = MOE_TM.bit_length() - 1
    idx = idx_ref[...]
    eid = lax.broadcasted_iota(jnp.int32, (N_EXP, T), 0)
    m0 = eid == idx[0:1, :]
    m1 = eid == idx[1:2, :]
    member = jnp.where(m0 | m1, 1.0, 0.0)
    cnt = jnp.sum(member, axis=1, keepdims=True).astype(jnp.int32)
    ntile = jnp.right_shift(cnt + (MOE_TM - 1), shift)
    offs, acc = [], jnp.zeros((1, 1), jnp.int32)
    for e in range(N_EXP):
        offs.append(acc)
        acc = acc + ntile[e:e + 1, :]
    off_t = jnp.concatenate(offs, axis=0)
    end_t = off_t + ntile

    blk = 256
    r_i = lax.broadcasted_iota(jnp.int32, (blk, blk), 0)
    c_i = lax.broadcasted_iota(jnp.int32, (blk, blk), 1)
    upper = jnp.where(r_i <= c_i, 1.0, 0.0).astype(BF16)
    run = (off_t * MOE_TM).astype(F32)
    for j in range(T // blk):
        ls = slice(j * blk, (j + 1) * blk)
        mb = member[:, ls]
        inc = _dot(mb.astype(BF16), upper)
        dest = run + inc - mb
        pos_ref[0:1, ls] = jnp.sum(jnp.where(m0[:, ls], dest, 0.0), axis=0, keepdims=True).astype(jnp.int32)
        pos_ref[1:2, ls] = jnp.sum(jnp.where(m1[:, ls], dest, 0.0), axis=0, keepdims=True).astype(jnp.int32)
        run = run + inc[:, blk - 1:blk]

    lane = lax.broadcasted_iota(jnp.int32, (1, 128), 1)
    zero = jnp.zeros((1, 128), jnp.int32)
    tile_e, cnt_row, off_row, end_row = zero, zero, zero, zero
    for e in range(N_EXP):
        tile_e = tile_e + jnp.where(lane >= end_t[e:e + 1, :], 1, 0)
        here = lane == e
        cnt_row = jnp.where(here, cnt[e:e + 1, :], cnt_row)
        off_row = jnp.where(here, off_t[e:e + 1, :] * MOE_TM, off_row)
        end_row = jnp.where(here, end_t[e:e + 1, :] * MOE_TM, end_row)
    tile_e = jnp.minimum(tile_e, N_EXP - 1)
    nt_row = zero + acc
    next_row = zero
    nxt = jnp.full((1, 1), -1, jnp.int32)
    for e in reversed(range(N_EXP)):
        next_row = jnp.where(lane == e, nxt, next_row)
        nxt = jnp.where(cnt[e:e + 1, :] > 0, e, nxt)
    meta_ref[...] = jnp.concatenate([tile_e, cnt_row, off_row, end_row, nt_row, next_row, zero, zero], axis=0)


def _route_pos(idx):
    return pl.pallas_call(
        _pos_kernel,
        grid=(1,),
        in_specs=[pl.BlockSpec((2, T), lambda i: (0, 0))],
        out_specs=[pl.BlockSpec((2, T), lambda i: (0, 0)), pl.BlockSpec((8, 128), lambda i: (0, 0))],
        out_shape=[jax.ShapeDtypeStruct((2, T), jnp.int32), jax.ShapeDtypeStruct((8, 128), jnp.int32)],
        compiler_params=_cp(("arbitrary",)),
        name="route_pos",
    )(idx)


DISP_TM = 512


def _dispatch_kernel(meta_ref, pos_ref, h_ref, z_hbm, xs_hbm, sem):
    i = pl.program_id(0)

    def row_copy(src, src_row, dst_row):
        return pltpu.make_async_copy(src.at[pl.ds(src_row * NCH, NCH), :],
                                     xs_hbm.at[pl.ds(pl.multiple_of(dst_row * NCH, NCH), NCH), :], sem)

    for r in range(DISP_TM):
        row_copy(h_ref, r, pos_ref[0, r]).start(priority=0)
        row_copy(h_ref, r, pos_ref[1, r]).start(priority=1)

    e = jnp.minimum(i, N_EXP - 1)
    pad0 = meta_ref[META_OFF, e] + meta_ref[META_CNT, e]
    npad = jnp.where(i < N_EXP, meta_ref[META_END, e] - pad0, 0)

    def pad_copies(act):
        s = pad0
        for bit in reversed(range(MOE_TM.bit_length() - 1)):
            size = 1 << bit
            part = jnp.bitwise_and(npad, size)

            @pl.when(part != 0)
            def _():
                dst = pl.ds(pl.multiple_of(s * NCH, NCH), size * NCH)
                act(pltpu.make_async_copy(z_hbm.at[pl.ds(0, size * NCH), :], xs_hbm.at[dst, :], sem))

            s = s + part

    pad_copies(lambda c: c.start())

    for _ in range(2):
        pltpu.make_async_copy(h_ref, xs_hbm.at[pl.ds(0, DISP_TM * NCH), :], sem).wait()

    tail = meta_ref[META_NT, 0] + i
    has_tail = (i < N_EXP) & (tail < MOE_NT)

    def tail_copy():
        rows = pl.ds(pl.multiple_of(tail * (MOE_TM * NCH), MOE_TM * NCH), MOE_TM * NCH)
        return pltpu.make_async_copy(z_hbm, xs_hbm.at[rows, :], sem)

    @pl.when(has_tail)
    def _():
        tail_copy().start()

    pad_copies(lambda c: c.wait())

    @pl.when(has_tail)
    def _():
        tail_copy().wait()


def _dispatch(meta, pos, h2, zrow):
    return pl.pallas_call(
        _dispatch_kernel,
        grid_spec=pltpu.PrefetchScalarGridSpec(
            num_scalar_prefetch=1,
            grid=(T // DISP_TM,),
            in_specs=[
                pl.BlockSpec((2, DISP_TM), lambda i, meta: (0, i), memory_space=pltpu.SMEM),
                pl.BlockSpec((DISP_TM * NCH, 128), lambda i, meta: (i, 0)),
                pl.BlockSpec((MOE_TM * NCH, 128), lambda i, meta: (0, 0)),
            ],
            out_specs=pl.BlockSpec(memory_space=pl.ANY),
            scratch_shapes=[pltpu.SemaphoreType.DMA],
        ),
        out_shape=jax.ShapeDtypeStruct((MOE_ROWS * NCH, 128), F32),
        compiler_params=_cp(("arbitrary",)),
        name="dispatch",
    )(meta, pos, h2, zrow)


def _experts_kernel(meta_ref, xs_ref, wg_hbm, wu_hbm, wd_hbm, ys_ref,
                    wg_f, wu_f, wd_f, wg_s, wu_s, wd_s, sem, *, l):
    j = pl.program_id(0)
    live = j < meta_ref[META_NT, 0]
    e = meta_ref[META_TILE_E, j]
    e_prev = meta_ref[META_TILE_E, jnp.maximum(j - 1, 0)]

    def fetch(ex):
        return (pltpu.make_async_copy(wg_hbm.at[l, ex], wg_f, sem.at[0]),
                pltpu.make_async_copy(wu_hbm.at[l, ex], wu_f, sem.at[1]),
                pltpu.make_async_copy(wd_hbm.at[l, ex], wd_f, sem.at[2]))

    @pl.when(j == 0)
    def _():
        for c in fetch(e):
            c.start()

    @pl.when(live & ((j == 0) | (e != e_prev)))
    def _():
        for c, dst, src in zip(fetch(e), (wg_s, wu_s, wd_s), (wg_f, wu_f, wd_f)):
            c.wait()
            dst[...] = src[...].astype(BF16)
        nxt = meta_ref[META_NEXT_E, e]

        @pl.when(nxt >= 0)
        def _():
            for c in fetch(nxt):
                c.start()

    @pl.when(live)
    def _():
        x = _load_row_tiles(xs_ref, MOE_TM).astype(BF16)
        g = _dot(x, wg_s[...])
        u = _dot(x, wu_s[...])
        act = (g * _sigmoid(g)) * u
        _store_row_tiles(ys_ref, _dot(act.astype(BF16), wd_s[...]))

    @pl.when(jnp.logical_not(live))
    def _():
        ys_ref[...] = jnp.zeros_like(ys_ref)


def _experts(meta, xs, w_gate_e, w_up_e, w_down_e, l):
    def tile(j, meta):
        return jnp.minimum(j, meta[META_NT, 0] - 1)

    return pl.pallas_call(
        functools.partial(_experts_kernel, l=l),
        grid_spec=pltpu.PrefetchScalarGridSpec(
            num_scalar_prefetch=1,
            grid=(MOE_NT,),
            in_specs=[
                pl.BlockSpec((MOE_TM * NCH, 128), lambda j, meta: (tile(j, meta), 0)),
                pl.BlockSpec(memory_space=pl.ANY),
                pl.BlockSpec(memory_space=pl.ANY),
                pl.BlockSpec(memory_space=pl.ANY),
            ],
            out_specs=pl.BlockSpec((MOE_TM * NCH, 128), lambda j, meta: (j, 0)),
            scratch_shapes=[pltpu.VMEM((D, D_EXP), F32), pltpu.VMEM((D, D_EXP), F32), pltpu.VMEM((D_EXP, D), F32),
                            pltpu.VMEM((D, D_EXP), BF16), pltpu.VMEM((D, D_EXP), BF16), pltpu.VMEM((D_EXP, D), BF16),
                            pltpu.SemaphoreType.DMA((3,))],
        ),
        out_shape=jax.ShapeDtypeStruct((MOE_ROWS * NCH, 128), F32),
        compiler_params=_cp(("arbitrary",)),
        name="experts",
    )(meta, xs, w_gate_e, w_up_e, w_down_e)


COMB_TM = SEG


def _combine_kernel(pos_ref, w_ref, x1_ref, mod_ref, fg_ref, ys_hbm, oa_ref, ob_ref, buf, y_s, sem, *, final):
    i = pl.program_id(0)
    n = pl.num_programs(0) - 1
    n_ctx = T_CTX // COMB_TM

    for s in range(2):
        @pl.when((i < n) & (lax.rem(i, 2) == s))
        def _():
            for r in range(COMB_TM):
                for k in range(2):
                    src = pl.ds(pl.multiple_of(pos_ref[k, r] * NCH, NCH), NCH)
                    pltpu.make_async_copy(ys_hbm.at[src, :], buf.at[s, k, pl.ds(r * NCH, NCH), :],
                                          sem.at[s]).start(priority=k)

    for slot in range(2):
        @pl.when((i > 0) & (lax.rem(i - 1, 2) == slot))
        def _():
            for k in range(2):
                pltpu.make_async_copy(ys_hbm.at[pl.ds(0, COMB_TM * NCH), :], buf.at[slot, k], sem.at[slot]).wait()
            w = w_ref[...]
            y = (w[:, 0:1] * _load_row_tiles(buf.at[slot, 0], COMB_TM)
                 + w[:, 1:2] * _load_row_tiles(buf.at[slot, 1], COMB_TM))
            y_s[...] = y

    @pl.when(i > 0)
    def _():
        x = x1_ref[...] + mod_ref[5:6, :] * y_s[...]
        if final:
            ms = jnp.mean(x * x, axis=-1, keepdims=True)
            x = x * lax.rsqrt(ms + EPS) * fg_ref[...]

        @pl.when(i - 1 < n_ctx)
        def _():
            oa_ref[...] = x

        @pl.when(i - 1 >= n_ctx)
        def _():
            ob_ref[...] = x


def _combine(pos, wts_t, x1, modseg, final_g, ys, l, final):
    n = T // COMB_TM
    n_ctx = T_CTX // COMB_TM

    def done(i):
        return jnp.maximum(i - 1, 0)

    return pl.pallas_call(
        functools.partial(_combine_kernel, final=final),
        grid=(n + 1,),
        in_specs=[
            pl.BlockSpec((2, COMB_TM), lambda i: (0, jnp.minimum(i, n - 1)), memory_space=pltpu.SMEM),
            pl.BlockSpec((COMB_TM, 2), lambda i: (done(i), 0)),
            pl.BlockSpec((COMB_TM, D), lambda i: (done(i), 0)),
            pl.BlockSpec((None, None, 8, D), lambda i: (l, done(i), 0, 0)),
            pl.BlockSpec((1, D), lambda i: (0, 0)),
            pl.BlockSpec(memory_space=pl.ANY),
        ],
        out_specs=[pl.BlockSpec((COMB_TM, D), lambda i: (jnp.minimum(done(i), n_ctx - 1), 0)),
                   pl.BlockSpec((COMB_TM, D), lambda i: (jnp.maximum(done(i) - n_ctx, 0), 0))],
        out_shape=[jax.ShapeDtypeStruct((T_CTX, D), F32), jax.ShapeDtypeStruct((T_LAT, D), F32)],
        scratch_shapes=[pltpu.VMEM((2, 2, COMB_TM * NCH, 128), F32), pltpu.VMEM((COMB_TM, D), F32),
                        pltpu.SemaphoreType.DMA((2,))],
        compiler_params=_cp(("arbitrary",)),
        name="combine",
    )(pos, wts_t, x1, modseg, final_g.reshape(1, D), ys)


def _rope_tables():
    n = DEC_SEQ
    pos_row = np.repeat(np.arange(n // GRID_W, dtype=np.float32), GRID_W)
    pos_col = np.tile(np.arange(GRID_W, dtype=np.float32), n // GRID_W)
    half = HD // 2
    inv_freq = jnp.asarray(ROPE_THETA, F32) ** (-jnp.arange(0, half, 2, dtype=F32) / half)
    ang = jnp.concatenate([jnp.asarray(pos_row)[:, None] * inv_freq,
                           jnp.asarray(pos_col)[:, None] * inv_freq], axis=-1)
    cos, sin = jnp.cos(ang), jnp.sin(ang)
    cos128 = jnp.tile(cos, (1, 4))
    sin128 = jnp.tile(jnp.concatenate([-sin, sin], axis=-1), (1, 2))
    return cos128, sin128


def _head_mean_matrix():
    idx = np.arange(2 * HD)
    same = (idx[:, None] // HD) == (idx[None, :] // HD)
    return jnp.asarray(same.astype(np.float32) / HD, BF16)


_SEG_ROWS = np.array([0] * (T_CTX // SEG) + [1 + b for b in range(DEC_BATCH) for _ in range(DEC_SEQ // SEG)])


def kernel(x_prompt, x_sample, cache_k, cache_v, state_rec, c, c_ctx, w_mod, b_mod, norm1_g, norm2_g, w_in, conv_w, conv_b, rg_wa, rg_ba, rg_wx, rg_bx, rg_lambda, q_norm_g, k_norm_g, w_rec_out, w_att_out, w_out, w_router, router_bias, w_gate_e, w_up_e, w_down_e, final_g):
    xa, xb = x_prompt.reshape(T_CTX, D), x_sample.reshape(T_LAT, D)

    cvecs = jnp.concatenate([c_ctx[None, :], c, jnp.zeros((3, D), F32)], axis=0)
    mods = _mods(cvecs, w_mod, b_mod).reshape(DEPTH, 8, 6, D)
    modseg = jnp.pad(mods[:, _SEG_ROWS], ((0, 0), (0, 0), (0, 2), (0, 0)))

    cos128, sin128 = _rope_tables()
    bd = _head_mean_matrix()
    qg128 = jnp.tile(q_norm_g, (1, 2)).reshape(DEPTH, 1, 2 * HD)
    kg128 = jnp.tile(k_norm_g, (1, 2)).reshape(DEPTH, 1, 2 * HD)
    wg = jnp.concatenate([rg_wa[:, 0], rg_wx[:, 0], rg_wa[:, 1], rg_wx[:, 1]], axis=-1)
    pvec = jnp.stack([rg_ba[:, 0], rg_bx[:, 0], rg_ba[:, 1], rg_bx[:, 1],
                      rg_lambda[:, 0], rg_lambda[:, 1], conv_b, jnp.zeros_like(conv_b)], axis=1)
    wrt = w_router.T
    rbias = router_bias.reshape(N_EXP, 1)
    zrow = jnp.zeros((MOE_TM * NCH, 128), F32)

    caches, new_s = (), []
    for l in range(DEPTH):
        proj = _inproj(xa, xb, modseg, norm1_g, w_in, l)
        h0 = jnp.concatenate([jnp.zeros((T_CTX // UNIT, 2, D), F32), state_rec[:, l]], axis=0)
        yrec, stf, stb = _rec(proj, conv_w, pvec, wg, h0, l)
        o_ctx, kc, vc = _attn_ctx(proj, qg128, kg128, bd, l, prev_caches=caches)
        caches = (kc, vc)
        o_lat = _attn_lat(proj, cache_k, cache_v, qg128, kg128, cos128, sin128, bd, l)
        x1, h2, idx, wts = _merge(yrec, o_ctx, o_lat, proj, xa, xb, modseg, norm2_g,
                                  w_rec_out, w_att_out, w_out, wrt, rbias, l)
        pos, meta = _route_pos(idx)
        xs = _dispatch(meta, pos, h2, zrow)
        ys = _experts(meta, xs, w_gate_e, w_up_e, w_down_e, l)
        xa, xb = _combine(pos, wts.T, x1, modseg, final_g, ys, l, final=(l == DEPTH - 1))
        n_cu = T_CTX // UNIT
        spu = UNIT // SEQ
        hf_last = stf[:n_cu].reshape(n_cu, spu, 2, D)[:, :, 1].reshape(BATCH, D)
        hb_first = stb[:n_cu].reshape(n_cu, spu, 2, D)[:, :, 0].reshape(BATCH, D)
        new_s.append(jnp.stack([hf_last, hb_first], axis=1))

    y_prompt = xa.reshape(BATCH, SEQ, D)
    y_sample = xb.reshape(DEC_BATCH, DEC_SEQ, D)
    return (y_prompt, y_sample, caches[0], caches[1], jnp.stack(new_s, axis=1))
```

```python
import functools

import numpy as np
import jax
import jax.numpy as jnp
from jax import lax
from jax.experimental import pallas as pl
from jax.experimental.pallas import tpu as pltpu

F32 = jnp.float32
BF16 = jnp.bfloat16

D = 1024
BATCH = 16
SEQ = 256
DEPTH = 2
DEC_BATCH = 4
DEC_SEQ = 1024
PAST = 256
GRID_W = 64
N_HEADS = 16
N_KV = 4
HD = 64
RG_BLK = 128
RG_C = 8.0
N_EXP = 16
D_EXP = 512
ROPE_THETA = 10000.0
EPS = 1e-6
P_IN = 5632
TINY = float(np.finfo(np.float32).tiny)
NEG_LOG2E = -float(np.log2(np.e))

T_CTX = BATCH * SEQ
T_LAT = DEC_BATCH * DEC_SEQ
T = T_CTX + T_LAT
SEG = 256
UNIT = 1024
N_UNIT = T // UNIT
LANES = 128
SUBLANES = 8
CHUNK = UNIT // SUBLANES
CSTRIDE = CHUNK + SUBLANES

VMEM_LIMIT = 56 * 1024 * 1024


def _cp(sem):
    return pltpu.CompilerParams(dimension_semantics=sem, vmem_limit_bytes=VMEM_LIMIT)


def _split(x):
    hi = x.astype(BF16)
    lo = (x - hi.astype(F32)).astype(BF16)
    return hi, lo


def _sigmoid(x):
    return 0.5 * jnp.tanh(0.5 * x) + 0.5


NCH = D // LANES


def _store_row_tiles(ref, x):
    n = x.shape[0]
    for c in range(NCH):
        ref[pl.ds(c, n, stride=NCH), :] = x[:, c * 128:(c + 1) * 128]


def _load_row_tiles(ref, n):
    return jnp.concatenate([ref[pl.ds(c, n, stride=NCH), :] for c in range(NCH)], axis=-1)


def _dot(a, b):
    return jnp.dot(a, b, preferred_element_type=F32)


def _dot_nt(a, b):
    return lax.dot_general(a, b, (((1,), (1,)), ((), ())), preferred_element_type=F32)


def _mods_kernel(c_ref, w_ref, b_ref, o_ref):
    c = c_ref[...]
    s = c * jax.nn.sigmoid(c)
    s_hi, s_lo = _split(s)
    w_hi, w_lo = _split(w_ref[...])
    o_ref[...] = _dot(s_hi, w_hi) + _dot(s_hi, w_lo) + _dot(s_lo, w_hi) + b_ref[...]


def _mods(cvecs, w_mod, b_mod):
    tn = 1536
    return pl.pallas_call(
        _mods_kernel,
        grid=(DEPTH, 6 * D // tn),
        in_specs=[
            pl.BlockSpec((8, D), lambda l, j: (0, 0)),
            pl.BlockSpec((None, D, tn), lambda l, j: (l, 0, j)),
            pl.BlockSpec((None, 1, tn), lambda l, j: (l, 0, j)),
        ],
        out_specs=pl.BlockSpec((None, 8, tn), lambda l, j: (l, 0, j)),
        out_shape=jax.ShapeDtypeStruct((DEPTH, 8, 6 * D), F32),
        compiler_params=_cp(("arbitrary", "arbitrary")),
        name="mods",
    )(cvecs, w_mod, b_mod.reshape(DEPTH, 1, 6 * D))


def _norm_mod(x, g, shift, scale):
    ms = jnp.mean(x * x, axis=-1, keepdims=True)
    return x * lax.rsqrt(ms + EPS) * g * (1.0 + scale) + shift


def _two_part_specs(tm, n_ctx):
    return [pl.BlockSpec((tm, D), lambda i, *_: (jnp.minimum(i, n_ctx - 1), 0)),
            pl.BlockSpec((tm, D), lambda i, *_: (jnp.maximum(i - n_ctx, 0), 0))]


def _inproj_kernel(xa_ref, xb_ref, mod_ref, g_ref, w_ref, o_ref, h_ref, *, tm):
    def prologue(x_ref):
        def seg(s, carry):
            r0 = pl.multiple_of(s * SEG, SEG)
            m = mod_ref[s]
            h = _norm_mod(x_ref[pl.ds(r0, SEG), :], g_ref[...], m[0:1, :], m[1:2, :])
            h_ref[pl.ds(r0, SEG), :] = h.astype(BF16)
            return carry
        lax.fori_loop(0, tm // SEG, seg, 0)

    first = pl.program_id(1) == 0
    is_ctx = pl.program_id(0) < T_CTX // tm

    @pl.when(first & is_ctx)
    def _():
        prologue(xa_ref)

    @pl.when(first & jnp.logical_not(is_ctx))
    def _():
        prologue(xb_ref)

    o_ref[...] = _dot(h_ref[...], w_ref[...].astype(BF16)).astype(BF16)


def _inproj(xa, xb, modseg, norm_g, w_in, l):
    tm, tn = 2048, 512
    return pl.pallas_call(
        functools.partial(_inproj_kernel, tm=tm),
        grid=(T // tm, P_IN // tn),
        in_specs=_two_part_specs(tm, T_CTX // tm) + [
            pl.BlockSpec((None, tm // SEG, 8, D), lambda i, j: (l, i, 0, 0)),
            pl.BlockSpec((None, 1, D), lambda i, j: (l, 0, 0)),
            pl.BlockSpec((None, D, tn), lambda i, j: (l, 0, j)),
        ],
        out_specs=pl.BlockSpec((tm, tn), lambda i, j: (i, j)),
        out_shape=jax.ShapeDtypeStruct((T, P_IN), BF16),
        scratch_shapes=[pltpu.VMEM((tm, D), BF16)],
        compiler_params=_cp(("arbitrary", "arbitrary")),
        name="inproj",
    )(xa, xb, modseg, norm_g.reshape(DEPTH, 1, D), w_in)


REC_CW = 512
PAD_F = 16
PAD_B = 8
GATE_ROWS = 512


def _rec_kernel(xr_ref, gate_ref, cw_ref, pv_ref, wg_ref, h0_ref,
                y_ref, stf_ref, stb_ref,
                xs_ref, af_ref, bf_ref, ab_ref, bb_ref, nat_ref, wgh_ref):
    u = pl.program_id(0)
    is_ctx = u < (T_CTX // UNIT)
    cps = jnp.where(is_ctx, SEQ // CHUNK, DEC_SEQ // CHUNK)
    nblk = REC_CW // RG_BLK

    def lanes(n):
        return slice(n * RG_BLK, (n + 1) * RG_BLK)

    def tile(r):
        return slice(8 * r, 8 * r + 8)

    for c in range(8):
        for n in range(nblk):
            nat_ref[n, c * CSTRIDE:c * CSTRIDE + CHUNK, :] = xr_ref[c * CHUNK:(c + 1) * CHUNK, lanes(n)].astype(F32)
    for r in range(CHUNK):
        for n in range(nblk):
            xs_ref[PAD_F + 8 * r:PAD_F + 8 * r + 8, lanes(n)] = nat_ref[n, pl.ds(r, 8, stride=CSTRIDE), :]
    chunk_id = lax.broadcasted_iota(jnp.int32, (8, 1), 0)
    seq_start = jnp.bitwise_and(chunk_id, cps - 1) == 0
    seq_end = jnp.bitwise_and(chunk_id, cps - 1) == cps - 1
    for j, r in ((0, CHUNK - 2), (1, CHUNK - 1)):
        prev_chunk = pltpu.roll(xs_ref[PAD_F + 8 * r:PAD_F + 8 * r + 8, :], 1, 0)
        xs_ref[tile(j), :] = jnp.where(seq_start, 0.0, prev_chunk)
    next_chunk = pltpu.roll(xs_ref[PAD_F:PAD_F + 8, :], 7, 0)
    xs_ref[PAD_F + UNIT:PAD_F + UNIT + PAD_B, :] = jnp.where(seq_end, 0.0, next_chunk)

    pv = pv_ref[...]
    cwts = cw_ref[...]
    conv_b = pv[6:7, :]

    def softplus_neg(lam):
        z = -lam
        return jnp.maximum(z, 0.0) + jnp.log1p(jnp.exp(-jnp.abs(z)))

    c4s = tuple((0.5 * RG_C) * softplus_neg(pv[4 + d:5 + d, :]) for d in range(2))
    pv_h = 0.5 * pv
    for n in range(nblk):
        wgh_ref[n] = (0.5 * wg_ref[n]).astype(BF16)
    a_refs = (af_ref, ab_ref)
    b_refs = (bf_ref, bb_ref)

    def gates(g, carry):
        base = pl.multiple_of(g * GATE_ROWS, GATE_ROWS)

        def tap(d):
            return xs_ref[pl.ds(pl.multiple_of(base + PAD_F + 8 * d, 8), GATE_ROWS), :]

        xc = conv_b + tap(-2) * cwts[0:1, :]
        xc = xc + tap(-1) * cwts[1:2, :]
        xc = xc + tap(0) * cwts[2:3, :]
        xc = xc + tap(1) * cwts[3:4, :]
        for n in range(nblk):
            ls = lanes(n)
            xn = xc[:, ls]
            hx = 0.5 * xn
            pre_h = _dot(xn.astype(BF16), wgh_ref[n])
            for d in range(2):
                th_r = jnp.tanh(pre_h[:, (2 * d) * RG_BLK:(2 * d + 1) * RG_BLK] + pv_h[2 * d:2 * d + 1, ls])
                th_i = jnp.tanh(pre_h[:, (2 * d + 1) * RG_BLK:(2 * d + 2) * RG_BLK] + pv_h[2 * d + 1:2 * d + 2, ls])
                c4 = c4s[d][:, ls]
                nla = c4 * th_r + c4
                a = jnp.exp2(nla * NEG_LOG2E)
                s = jnp.tanh(nla) * (a * a + 1.0)
                inp = (s * lax.rsqrt(jnp.maximum(s, TINY))) * (hx * th_i + hx)
                a_refs[d][pl.ds(base, GATE_ROWS), ls] = a
                b_refs[d][pl.ds(base, GATE_ROWS), ls] = inp
        return carry

    lax.fori_loop(0, UNIT // GATE_ROWS, gates, 0)

    hf = hb = jnp.zeros((8, REC_CW), F32)
    pf = pb = jnp.ones((8, REC_CW), F32)
    for r in range(CHUNK):
        rf, rb = tile(r), tile(CHUNK - 1 - r)
        a = af_ref[rf, :]
        hf = a * hf + bf_ref[rf, :]
        pf = a * pf
        bf_ref[rf, :] = hf
        af_ref[rf, :] = pf
        a = ab_ref[rb, :]
        hb = a * hb + bb_ref[rb, :]
        pb = a * pb
        bb_ref[rb, :] = hb
        ab_ref[rb, :] = pb

    h0f = h0_ref[0:1, :]
    h0b = h0_ref[1:2, :]
    cf = [h0f]
    for c in range(1, 8):
        chain = hf[c - 1:c, :] + pf[c - 1:c, :] * cf[c - 1]
        cf.append(jnp.where(jnp.bitwise_and(c, cps - 1) == 0, h0f, chain))
    cb = [None] * 8
    cb[7] = h0b
    for c in range(6, -1, -1):
        chain = hb[c + 1:c + 2, :] + pb[c + 1:c + 2, :] * cb[c + 1]
        cb[c] = jnp.where(jnp.bitwise_and(c, cps - 1) == cps - 1, h0b, chain)
    carry_f = jnp.concatenate(cf, axis=0)
    carry_b = jnp.concatenate(cb, axis=0)
    stf_ref[...] = hf + pf * carry_f
    stb_ref[...] = hb + pb * carry_b

    for r in range(CHUNK):
        h = (bf_ref[tile(r), :] + af_ref[tile(r), :] * carry_f) + (bb_ref[tile(r), :] + ab_ref[tile(r), :] * carry_b)
        for n in range(nblk):
            nat_ref[n, pl.ds(r, 8, stride=CSTRIDE), :] = h[:, lanes(n)]

    for c in range(8):
        rows = slice(c * CHUNK, (c + 1) * CHUNK)
        for n in range(nblk):
            g = gate_ref[rows, lanes(n)].astype(F32)
            h = nat_ref[n, c * CSTRIDE:c * CSTRIDE + CHUNK, :]
            y_ref[rows, lanes(n)] = (h * jax.nn.gelu(g, approximate=True)).astype(BF16)


def _rec(proj, conv_w, pvec, wg, h0, l):
    ncb = D // REC_CW
    return pl.pallas_call(
        _rec_kernel,
        grid=(N_UNIT, ncb),
        in_specs=[
            pl.BlockSpec((UNIT, REC_CW), lambda u, c: (u, c)),
            pl.BlockSpec((UNIT, REC_CW), lambda u, c: (u, ncb + c)),
            pl.BlockSpec((None, 4, REC_CW), lambda u, c: (l, 0, c)),
            pl.BlockSpec((None, 8, REC_CW), lambda u, c: (l, 0, c)),
            pl.BlockSpec((None, REC_CW // RG_BLK, RG_BLK, 4 * RG_BLK), lambda u, c: (l, c, 0, 0)),
            pl.BlockSpec((None, 2, REC_CW), lambda u, c: (u, 0, c)),
        ],
        out_specs=[
            pl.BlockSpec((UNIT, REC_CW), lambda u, c: (u, c)),
            pl.BlockSpec((None, 8, REC_CW), lambda u, c: (u, 0, c)),
            pl.BlockSpec((None, 8, REC_CW), lambda u, c: (u, 0, c)),
        ],
        out_shape=[
            jax.ShapeDtypeStruct((T, D), BF16),
            jax.ShapeDtypeStruct((N_UNIT, 8, D), F32),
            jax.ShapeDtypeStruct((N_UNIT, 8, D), F32),
        ],
        scratch_shapes=[pltpu.VMEM((PAD_F + UNIT + PAD_B, REC_CW), F32)]
        + [pltpu.VMEM((UNIT, REC_CW), F32)] * 4
        + [pltpu.VMEM((REC_CW // RG_BLK, 8 * CSTRIDE, RG_BLK), F32),
           pltpu.VMEM((REC_CW // RG_BLK, RG_BLK, 4 * RG_BLK), BF16)],
        compiler_params=_cp(("arbitrary", "arbitrary")),
        name="rec",
    )(proj, proj, conv_w, pvec, wg, h0)


def _head_norm(x, g128, bd):
    hi, lo = _split(x * x)
    ms = _dot(hi, bd) + _dot(lo, bd)
    return x * lax.rsqrt(ms + EPS) * g128


def _rope(x, cos, sin_signed):
    lane = lax.broadcasted_iota(jnp.int32, x.shape, 1)
    first_half = jnp.bitwise_and(lane, HD - 1) < HD // 2
    partner = jnp.where(first_half, pltpu.roll(x, 2 * HD - HD // 2, 1), pltpu.roll(x, HD // 2, 1))
    return x * cos + partner * sin_signed


def _with_ones(v):
    return jnp.concatenate([v, jnp.ones_like(v)], axis=-1)


def _softmax_pv(q, k, v_ext):
    s = _dot_nt(q, k)
    m = jnp.max(s, axis=-1, keepdims=True)
    p = jnp.exp2(s - m).astype(BF16)
    r = _dot(p, v_ext)
    return r[:, :HD] / r[:, HD:HD + 1]


def _attend_heads(q_ref, k, v_ext):
    return jnp.concatenate([_softmax_pv(q_ref[h], k, v_ext) for h in range(N_HEADS // N_KV)], axis=-1)


def _attn_ctx_kernel(q_ref, k_ref, v_ref, qg_ref, kg_ref, bd_ref, *rest, slab):
    o_ref, ko_ref, vo_ref = rest[-3:]
    for p in range(slab):
        ko_ref[p] = rest[0][p]
        vo_ref[p] = rest[1][p]
    g = N_HEADS // N_KV
    bd = bd_ref[...]
    odd = lax.rem(pl.program_id(1), 2) == 1
    scale = HD ** -0.5 * float(np.log2(np.e))

    kx = _head_norm(k_ref[...].astype(F32), kg_ref[...], bd)
    vx = v_ref[...].astype(F32)
    k_new = jnp.where(odd, kx[:, HD:], kx[:, :HD])
    v_new = jnp.where(odd, vx[:, HD:], vx[:, :HD])
    ko_ref[slab] = k_new
    vo_ref[slab] = v_new
    heads = []
    for j in range(g // 2):
        x = _head_norm(q_ref[:, 2 * HD * j:2 * HD * (j + 1)].astype(F32), qg_ref[...], bd) * scale
        heads += [x[:, :HD].astype(BF16), x[:, HD:].astype(BF16)]
    o = _softmax_pv(jnp.concatenate(heads, axis=0), k_new.astype(BF16), _with_ones(v_new.astype(BF16)))
    o_ref[...] = jnp.concatenate([o[h * SEQ:(h + 1) * SEQ] for h in range(g)], axis=-1).astype(BF16)


def _attn_ctx(proj, qg128, kg128, bd, l, prev_caches=()):
    g = N_HEADS // N_KV
    qcol = 2 * D // (g * HD)
    kcol = 3 * D // (2 * HD)
    vcol = kcol + N_KV // 2
    cache_spec = pl.BlockSpec((None, l + 1, None, SEQ, HD), lambda b, h: (b, 0, h, 0, 0))
    prev_specs = [pl.BlockSpec((None, l, None, SEQ, HD), lambda b, h: (b, 0, h, 0, 0))] * 2 if l else []
    cache_shape = jax.ShapeDtypeStruct((BATCH, l + 1, N_KV, SEQ, HD), F32)
    return pl.pallas_call(
        functools.partial(_attn_ctx_kernel, slab=l),
        grid=(BATCH, N_KV),
        in_specs=[
            pl.BlockSpec((SEQ, g * HD), lambda b, h: (b, qcol + h)),
            pl.BlockSpec((SEQ, 2 * HD), lambda b, h: (b, kcol + h // 2)),
            pl.BlockSpec((SEQ, 2 * HD), lambda b, h: (b, vcol + h // 2)),
            pl.BlockSpec((None, 1, 2 * HD), lambda b, h: (l, 0, 0)),
            pl.BlockSpec((None, 1, 2 * HD), lambda b, h: (l, 0, 0)),
            pl.BlockSpec((2 * HD, 2 * HD), lambda b, h: (0, 0)),
        ] + prev_specs,
        out_specs=[pl.BlockSpec((SEQ, g * HD), lambda b, h: (b, h)), cache_spec, cache_spec],
        out_shape=[jax.ShapeDtypeStruct((T_CTX, D), BF16), cache_shape, cache_shape],
        compiler_params=_cp(("arbitrary", "arbitrary")),
        name="attn_ctx",
    )(proj, proj, proj, qg128, kg128, bd, *prev_caches)


def _attn_lat_kernel(q_ref, k_ref, v_ref, qg_ref, kg_ref, cos_ref, sin_ref, bd_ref, pk_ref, pv_ref,
                     o_ref, q_s, k_s, v_s):
    g = N_HEADS // N_KV
    bd, cos, sin = bd_ref[...], cos_ref[...], sin_ref[...]
    odd = lax.rem(pl.program_id(1), 2) == 1
    scale = HD ** -0.5 * float(np.log2(np.e))

    kx = _rope(_head_norm(k_ref[...].astype(F32), kg_ref[...], bd), cos, sin)
    vx = v_ref[...]
    k_s[0:PAST, :] = pk_ref[...].astype(BF16)
    k_s[PAST:, :] = jnp.where(odd, kx[:, HD:], kx[:, :HD]).astype(BF16)
    v_s[0:PAST, :] = _with_ones(pv_ref[...].astype(BF16))
    v_s[PAST:, :] = _with_ones(jnp.where(odd, vx[:, HD:], vx[:, :HD]))
    for j in range(g // 2):
        x = _head_norm(q_ref[:, 2 * HD * j:2 * HD * (j + 1)].astype(F32), qg_ref[...], bd)
        x = _rope(x, cos, sin) * scale
        q_s[2 * j] = x[:, :HD].astype(BF16)
        q_s[2 * j + 1] = x[:, HD:].astype(BF16)

    o_ref[...] = _attend_heads(q_s, k_s[...], v_s[...]).astype(BF16)


def _attn_lat(proj, cache_k, cache_v, qg128, kg128, cos128, sin128, bd, l):
    g = N_HEADS // N_KV
    row0 = T_CTX // DEC_SEQ
    qcol = 2 * D // (g * HD)
    kcol = 3 * D // (2 * HD)
    vcol = kcol + N_KV // 2
    return pl.pallas_call(
        _attn_lat_kernel,
        grid=(DEC_BATCH, N_KV),
        in_specs=[
            pl.BlockSpec((DEC_SEQ, g * HD), lambda b, h: (row0 + b, qcol + h)),
            pl.BlockSpec((DEC_SEQ, 2 * HD), lambda b, h: (row0 + b, kcol + h // 2)),
            pl.BlockSpec((DEC_SEQ, 2 * HD), lambda b, h: (row0 + b, vcol + h // 2)),
            pl.BlockSpec((None, 1, 2 * HD), lambda b, h: (l, 0, 0)),
            pl.BlockSpec((None, 1, 2 * HD), lambda b, h: (l, 0, 0)),
            pl.BlockSpec((DEC_SEQ, 2 * HD), lambda b, h: (0, 0)),
            pl.BlockSpec((DEC_SEQ, 2 * HD), lambda b, h: (0, 0)),
            pl.BlockSpec((2 * HD, 2 * HD), lambda b, h: (0, 0)),
            pl.BlockSpec((None, None, None, PAST, HD), lambda b, h: (b, l, h, 0, 0)),
            pl.BlockSpec((None, None, None, PAST, HD), lambda b, h: (b, l, h, 0, 0)),
        ],
        out_specs=pl.BlockSpec((DEC_SEQ, g * HD), lambda b, h: (b, h)),
        out_shape=jax.ShapeDtypeStruct((T_LAT, D), BF16),
        scratch_shapes=[pltpu.VMEM((g, DEC_SEQ, HD), BF16), pltpu.VMEM((PAST + DEC_SEQ, HD), BF16),
                        pltpu.VMEM((PAST + DEC_SEQ, 2 * HD), BF16)],
        compiler_params=_cp(("arbitrary", "arbitrary")),
        name="attn_lat",
    )(proj, proj, proj, qg128, kg128, cos128, sin128, bd, cache_k, cache_v)


MERGE_TM = 512


def _route(lt, bias):
    rows = [lt[e:e + 1, :] for e in range(N_EXP)]
    m = rows[0]
    for e in range(1, N_EXP):
        m = jnp.maximum(m, rows[e])
    ex = [jnp.exp(r - m) for r in rows]
    z = ex[0]
    for e in range(1, N_EXP):
        z = z + ex[e]
    probs = [x / z for x in ex]
    sel = [probs[e] + bias[e:e + 1, :] for e in range(N_EXP)]

    def top2_sum(v):
        a, b = jnp.maximum(v[0], v[1]), jnp.minimum(v[0], v[1])
        c, d = jnp.maximum(v[2], v[3]), jnp.minimum(v[2], v[3])
        return jnp.maximum(a, c) + jnp.maximum(jnp.minimum(a, c), jnp.maximum(b, d))

    scores = [top2_sum(sel[4 * g:4 * g + 4]) for g in range(4)]
    best = jnp.zeros_like(scores[0], dtype=jnp.int32)
    best_s = scores[0]
    for g in range(1, 4):
        take = scores[g] > best_s
        best = jnp.where(take, g, best)
        best_s = jnp.where(take, scores[g], best_s)
    cs, cp = [], []
    for j in range(4):
        s_j, p_j = sel[j], probs[j]
        for g in range(1, 4):
            s_j = jnp.where(best == g, sel[4 * g + j], s_j)
            p_j = jnp.where(best == g, probs[4 * g + j], p_j)
        cs.append(s_j)
        cp.append(p_j)
    neg = jnp.full_like(cs[0], -jnp.inf)

    def argmax4(v):
        bi = jnp.zeros_like(best)
        bv = v[0]
        for j in range(1, 4):
            take = v[j] > bv
            bi = jnp.where(take, j, bi)
            bv = jnp.where(take, v[j], bv)
        return bi

    def pick(v, idx):
        out = v[0]
        for j in range(1, 4):
            out = jnp.where(idx == j, v[j], out)
        return out

    i1 = argmax4(cs)
    cs2 = [jnp.where(i1 == j, neg, cs[j]) for j in range(4)]
    i2 = argmax4(cs2)
    i2 = jnp.where((i2 == 0) & (i1 == 0), 1, i2)
    w1, w2 = pick(cp, i1), pick(cp, i2)
    den = w1 + w2
    return best * 4 + i1, best * 4 + i2, w1 / den, w2 / den


def _merge_kernel(yrec_ref, oa_ref, ob_ref, gr0_ref, gr1_ref, ga0_ref, ga1_ref, xa_ref, xb_ref, mod_ref, g2_ref,
                  wrec_ref, watt_ref, wout_ref, wrt_ref, rb_ref,
                  x1_ref, h2_ref, idx_ref, wts_ref,
                  wrec_s, watt_s, wout_s):
    @pl.when(pl.program_id(0) == 0)
    def _():
        wrec_s[...] = wrec_ref[...].astype(BF16)
        watt_s[...] = watt_ref[...].astype(BF16)
        wout_s[...] = wout_ref[...].astype(BF16)

    is_ctx = pl.program_id(0) < T_CTX // MERGE_TM
    args = (yrec_ref, gr0_ref, gr1_ref, ga0_ref, ga1_ref, mod_ref, g2_ref, wrt_ref, rb_ref,
            x1_ref, h2_ref, idx_ref, wts_ref, wrec_s, watt_s, wout_s)

    @pl.when(is_ctx)
    def _():
        _merge_body(oa_ref, xa_ref, *args)

    @pl.when(jnp.logical_not(is_ctx))
    def _():
        _merge_body(ob_ref, xb_ref, *args)


def _merge_body(oatt_ref, x_ref, yrec_ref, gr0_ref, gr1_ref, ga0_ref, ga1_ref, mod_ref, g2_ref, wrt_ref, rb_ref,
                x1_ref, h2_ref, idx_ref, wts_ref, wrec_s, watt_s, wout_s):
    half = D // 2
    b_rec = _dot(yrec_ref[...], wrec_s[...])
    b_att = _dot(oatt_ref[...], watt_s[...])
    m0 = _sigmoid(gr0_ref[...].astype(F32)) * b_rec[:, :half] + _sigmoid(ga0_ref[...].astype(F32)) * b_att[:, :half]
    m1 = _sigmoid(gr1_ref[...].astype(F32)) * b_rec[:, half:] + _sigmoid(ga1_ref[...].astype(F32)) * b_att[:, half:]
    merged = jnp.concatenate([m0, m1], axis=-1).astype(BF16)
    out = _dot(merged, wout_s[...])

    hs = []
    for s in range(MERGE_TM // SEG):
        rows = slice(s * SEG, (s + 1) * SEG)
        m = mod_ref[s]
        x1 = x_ref[rows, :] + m[2:3, :] * out[rows, :]
        x1_ref[rows, :] = x1
        h2 = _norm_mod(x1, g2_ref[...], m[3:4, :], m[4:5, :])
        hs.append(h2)
    h2 = jnp.concatenate(hs, axis=0)
    _store_row_tiles(h2_ref, h2)

    h_hi, h_lo = _split(h2)
    w_hi, w_lo = _split(wrt_ref[...])
    lt = _dot_nt(w_hi, h_hi) + _dot_nt(w_hi, h_lo) + _dot_nt(w_lo, h_hi)
    e1, e2, w1, w2 = _route(lt, rb_ref[...])
    idx_ref[...] = jnp.concatenate([e1, e2], axis=0)
    wts_ref[...] = jnp.concatenate([w1, w2], axis=0)


def _merge(yrec, o_ctx, o_lat, proj, xa, xb, modseg, norm2_g, w_rec_out, w_att_out, w_out, wrt, rbias, l):
    tm = MERGE_TM
    half = D // 2
    gcol = (3 * D + 2 * N_KV * HD) // half
    wspec = pl.BlockSpec((None, D, D), lambda i: (l, 0, 0))
    return pl.pallas_call(
        _merge_kernel,
        grid=(T // tm,),
        in_specs=[pl.BlockSpec((tm, D), lambda i: (i, 0))] + _two_part_specs(tm, T_CTX // tm) + [
            pl.BlockSpec((tm, half), lambda i: (i, gcol)),
            pl.BlockSpec((tm, half), lambda i: (i, gcol + 1)),
            pl.BlockSpec((tm, half), lambda i: (i, gcol + 2)),
            pl.BlockSpec((tm, half), lambda i: (i, gcol + 3)),
        ] + _two_part_specs(tm, T_CTX // tm) + [
            pl.BlockSpec((None, tm // SEG, 8, D), lambda i: (l, i, 0, 0)),
            pl.BlockSpec((None, 1, D), lambda i: (l, 0, 0)),
            wspec, wspec, wspec,
            pl.BlockSpec((N_EXP, D), lambda i: (0, 0)),
            pl.BlockSpec((N_EXP, 1), lambda i: (0, 0)),
        ],
        out_specs=[
            pl.BlockSpec((tm, D), lambda i: (i, 0)),
            pl.BlockSpec((tm * NCH, 128), lambda i: (i, 0)),
            pl.BlockSpec((2, tm), lambda i: (0, i)),
            pl.BlockSpec((2, tm), lambda i: (0, i)),
        ],
        out_shape=[
            jax.ShapeDtypeStruct((T, D), F32),
            jax.ShapeDtypeStruct((T * NCH, 128), F32),
            jax.ShapeDtypeStruct((2, T), jnp.int32),
            jax.ShapeDtypeStruct((2, T), F32),
        ],
        scratch_shapes=[pltpu.VMEM((D, D), BF16)] * 3,
        compiler_params=_cp(("arbitrary",)),
        name="merge",
    )(yrec, o_ctx, o_lat, proj, proj, proj, proj, xa, xb, modseg, norm2_g.reshape(DEPTH, 1, D),
      w_rec_out, w_att_out, w_out, wrt, rbias)


MOE_TM = 512
MOE_NT = 2 * T // MOE_TM + N_EXP
MOE_ROWS = MOE_NT * MOE_TM
META_TILE_E, META_CNT, META_OFF, META_END, META_NT, META_NEXT_E = 0, 1, 2, 3, 4, 5


def _pos_kernel(idx_ref, pos_ref, meta_ref):
    shift = MOE_TM.bit_length() - 1
    idx = idx_ref[...]
    eid = lax.broadcasted_iota(jnp.int32, (N_EXP, T), 0)
    m0 = eid == idx[0:1, :]
    m1 = eid == idx[1:2, :]
    member = jnp.where(m0 | m1, 1.0, 0.0)
    cnt = jnp.sum(member, axis=1, keepdims=True).astype(jnp.int32)
    ntile = jnp.right_shift(cnt + (MOE_TM - 1), shift)
    offs, acc = [], jnp.zeros((1, 1), jnp.int32)
    for e in range(N_EXP):
        offs.append(acc)
        acc = acc + ntile[e:e + 1, :]
    off_t = jnp.concatenate(offs, axis=0)
    end_t = off_t + ntile

    blk = 256
    r_i = lax.broadcasted_iota(jnp.int32, (blk, blk), 0)
    c_i = lax.broadcasted_iota(jnp.int32, (blk, blk), 1)
    upper = jnp.where(r_i <= c_i, 1.0, 0.0).astype(BF16)
    run = (off_t * MOE_TM).astype(F32)
    for j in range(T // blk):
        ls = slice(j * blk, (j + 1) * blk)
        mb = member[:, ls]
        inc = _dot(mb.astype(BF16), upper)
        dest = run + inc - mb
        pos_ref[0:1, ls] = jnp.sum(jnp.where(m0[:, ls], dest, 0.0), axis=0, keepdims=True).astype(jnp.int32)
        pos_ref[1:2, ls] = jnp.sum(jnp.where(m1[:, ls], dest, 0.0), axis=0, keepdims=True).astype(jnp.int32)
        run = run + inc[:, blk - 1:blk]

    lane = lax.broadcasted_iota(jnp.int32, (1, 128), 1)
    zero = jnp.zeros((1, 128), jnp.int32)
    tile_e, cnt_row, off_row, end_row = zero, zero, zero, zero
    for e in range(N_EXP):
        tile_e = tile_e + jnp.where(lane >= end_t[e:e + 1, :], 1, 0)
        here = lane == e
        cnt_row = jnp.where(here, cnt[e:e + 1, :], cnt_row)
        off_row = jnp.where(here, off_t[e:e + 1, :] * MOE_TM, off_row)
        end_row = jnp.where(here, end_t[e:e + 1, :] * MOE_TM, end_row)
    tile_e = jnp.minimum(tile_e, N_EXP - 1)
    nt_row = zero + acc
    next_row = zero
    nxt = jnp.full((1, 1), -1, jnp.int32)
    for e in reversed(range(N_EXP)):
        next_row = jnp.where(lane == e, nxt, next_row)
        nxt = jnp.where(cnt[e:e + 1, :] > 0, e, nxt)
    meta_ref[...] = jnp.concatenate([tile_e, cnt_row, off_row, end_row, nt_row, next_row, zero, zero], axis=0)


def _route_pos(idx):
    return pl.pallas_call(
        _pos_kernel,
        grid=(1,),
        in_specs=[pl.BlockSpec((2, T), lambda i: (0, 0))],
        out_specs=[pl.BlockSpec((2, T), lambda i: (0, 0)), pl.BlockSpec((8, 128), lambda i: (0, 0))],
        out_shape=[jax.ShapeDtypeStruct((2, T), jnp.int32), jax.ShapeDtypeStruct((8, 128), jnp.int32)],
        compiler_params=_cp(("arbitrary",)),
        name="route_pos",
    )(idx)


DISP_TM = 1024
DISP_EXPERTS = N_EXP // (T // DISP_TM)


def _dispatch_kernel(meta_ref, pos_ref, h_ref, z_hbm, xs_hbm, sem):
    i = pl.program_id(0)

    def row_copy(src, src_row, dst_row):
        return pltpu.make_async_copy(src.at[pl.ds(src_row * NCH, NCH), :],
                                     xs_hbm.at[pl.ds(pl.multiple_of(dst_row * NCH, NCH), NCH), :], sem)

    for r in range(DISP_TM):
        row_copy(h_ref, r, pos_ref[0, r]).start(priority=0)
        row_copy(h_ref, r, pos_ref[1, r]).start(priority=1)

    def zero_copies(act):
        for q in range(DISP_EXPERTS):
            e = i * DISP_EXPERTS + q
            s = meta_ref[META_OFF, e] + meta_ref[META_CNT, e]
            npad = meta_ref[META_END, e] - s
            for bit in reversed(range(MOE_TM.bit_length() - 1)):
                size = 1 << bit
                part = jnp.bitwise_and(npad, size)

                @pl.when(part != 0)
                def _():
                    dst = pl.ds(pl.multiple_of(s * NCH, NCH), size * NCH)
                    act(pltpu.make_async_copy(z_hbm.at[pl.ds(0, size * NCH), :], xs_hbm.at[dst, :], sem))

                s = s + part
            tail = meta_ref[META_NT, 0] + e

            @pl.when(tail < MOE_NT)
            def _():
                rows = pl.ds(pl.multiple_of(tail * (MOE_TM * NCH), MOE_TM * NCH), MOE_TM * NCH)
                act(pltpu.make_async_copy(z_hbm, xs_hbm.at[rows, :], sem))

    zero_copies(lambda c: c.start())
    for _ in range(2):
        pltpu.make_async_copy(h_ref, xs_hbm.at[pl.ds(0, DISP_TM * NCH), :], sem).wait()
    zero_copies(lambda c: c.wait())


def _dispatch(meta, pos, h2, zrow):
    return pl.pallas_call(
        _dispatch_kernel,
        grid_spec=pltpu.PrefetchScalarGridSpec(
            num_scalar_prefetch=1,
            grid=(T // DISP_TM,),
            in_specs=[
                pl.BlockSpec((2, DISP_TM), lambda i, meta: (0, i), memory_space=pltpu.SMEM),
                pl.BlockSpec((DISP_TM * NCH, 128), lambda i, meta: (i, 0)),
                pl.BlockSpec((MOE_TM * NCH, 128), lambda i, meta: (0, 0)),
            ],
            out_specs=pl.BlockSpec(memory_space=pl.ANY),
            scratch_shapes=[pltpu.SemaphoreType.DMA],
        ),
        out_shape=jax.ShapeDtypeStruct((MOE_ROWS * NCH, 128), F32),
        compiler_params=_cp(("arbitrary",)),
        name="dispatch",
    )(meta, pos, h2, zrow)


def _experts_kernel(meta_ref, xs_ref, wg_hbm, wu_hbm, wd_hbm, ys_ref,
                    wg_f, wu_f, wd_f, wg_s, wu_s, wd_s, sem, *, l):
    j = pl.program_id(0)
    live = j < meta_ref[META_NT, 0]
    e = meta_ref[META_TILE_E, j]
    e_prev = meta_ref[META_TILE_E, jnp.maximum(j - 1, 0)]

    def fetch(ex):
        return (pltpu.make_async_copy(wg_hbm.at[l, ex], wg_f, sem.at[0]),
                pltpu.make_async_copy(wu_hbm.at[l, ex], wu_f, sem.at[1]),
                pltpu.make_async_copy(wd_hbm.at[l, ex], wd_f, sem.at[2]))

    @pl.when(j == 0)
    def _():
        for c in fetch(e):
            c.start()

    @pl.when(live & ((j == 0) | (e != e_prev)))
    def _():
        for c, dst, src in zip(fetch(e), (wg_s, wu_s, wd_s), (wg_f, wu_f, wd_f)):
            c.wait()
            dst[...] = src[...].astype(BF16)
        nxt = meta_ref[META_NEXT_E, e]

        @pl.when(nxt >= 0)
        def _():
            for c in fetch(nxt):
                c.start()

    @pl.when(live)
    def _():
        x = _load_row_tiles(xs_ref, MOE_TM).astype(BF16)
        g = _dot(x, wg_s[...])
        u = _dot(x, wu_s[...])
        act = (g * _sigmoid(g)) * u
        _store_row_tiles(ys_ref, _dot(act.astype(BF16), wd_s[...]))

    @pl.when(jnp.logical_not(live))
    def _():
        ys_ref[...] = jnp.zeros_like(ys_ref)


def _experts(meta, xs, w_gate_e, w_up_e, w_down_e, l):
    def tile(j, meta):
        return jnp.minimum(j, meta[META_NT, 0] - 1)

    return pl.pallas_call(
        functools.partial(_experts_kernel, l=l),
        grid_spec=pltpu.PrefetchScalarGridSpec(
            num_scalar_prefetch=1,
            grid=(MOE_NT,),
            in_specs=[
                pl.BlockSpec((MOE_TM * NCH, 128), lambda j, meta: (tile(j, meta), 0)),
                pl.BlockSpec(memory_space=pl.ANY),
                pl.BlockSpec(memory_space=pl.ANY),
                pl.BlockSpec(memory_space=pl.ANY),
            ],
            out_specs=pl.BlockSpec((MOE_TM * NCH, 128), lambda j, meta: (j, 0)),
            scratch_shapes=[pltpu.VMEM((D, D_EXP), F32), pltpu.VMEM((D, D_EXP), F32), pltpu.VMEM((D_EXP, D), F32),
                            pltpu.VMEM((D, D_EXP), BF16), pltpu.VMEM((D, D_EXP), BF16), pltpu.VMEM((D_EXP, D), BF16),
                            pltpu.SemaphoreType.DMA((3,))],
        ),
        out_shape=jax.ShapeDtypeStruct((MOE_ROWS * NCH, 128), F32),
        compiler_params=_cp(("arbitrary",)),
        name="experts",
    )(meta, xs, w_gate_e, w_up_e, w_down_e)


COMB_TM = SEG


def _combine_kernel(pos_ref, w_ref, x1_ref, mod_ref, fg_ref, ys_hbm, oa_ref, ob_ref, buf, y_s, sem, *, final):
    i = pl.program_id(0)
    n = pl.num_programs(0) - 1
    n_ctx = T_CTX // COMB_TM

    for s in range(2):
        @pl.when((i < n) & (lax.rem(i, 2) == s))
        def _():
            for r in range(COMB_TM):
                for k in range(2):
                    src = pl.ds(pl.multiple_of(pos_ref[k, r] * NCH, NCH), NCH)
                    pltpu.make_async_copy(ys_hbm.at[src, :], buf.at[s, k, pl.ds(r * NCH, NCH), :],
                                          sem.at[s]).start(priority=k)

    for slot in range(2):
        @pl.when((i > 0) & (lax.rem(i - 1, 2) == slot))
        def _():
            for k in range(2):
                pltpu.make_async_copy(ys_hbm.at[pl.ds(0, COMB_TM * NCH), :], buf.at[slot, k], sem.at[slot]).wait()
            w = w_ref[...]
            y = (w[:, 0:1] * _load_row_tiles(buf.at[slot, 0], COMB_TM)
                 + w[:, 1:2] * _load_row_tiles(buf.at[slot, 1], COMB_TM))
            y_s[...] = y

    @pl.when(i > 0)
    def _():
        x = x1_ref[...] + mod_ref[5:6, :] * y_s[...]
        if final:
            ms = jnp.mean(x * x, axis=-1, keepdims=True)
            x = x * lax.rsqrt(ms + EPS) * fg_ref[...]

        @pl.when(i - 1 < n_ctx)
        def _():
            oa_ref[...] = x

        @pl.when(i - 1 >= n_ctx)
        def _():
            ob_ref[...] = x


def _combine(pos, wts_t, x1, modseg, final_g, ys, l, final):
    n = T // COMB_TM
    n_ctx = T_CTX // COMB_TM

    def done(i):
        return jnp.maximum(i - 1, 0)

    return pl.pallas_call(
        functools.partial(_combine_kernel, final=final),
        grid=(n + 1,),
        in_specs=[
            pl.BlockSpec((2, COMB_TM), lambda i: (0, jnp.minimum(i, n - 1)), memory_space=pltpu.SMEM),
            pl.BlockSpec((COMB_TM, 2), lambda i: (done(i), 0)),
            pl.BlockSpec((COMB_TM, D), lambda i: (done(i), 0)),
            pl.BlockSpec((None, None, 8, D), lambda i: (l, done(i), 0, 0)),
            pl.BlockSpec((1, D), lambda i: (0, 0)),
            pl.BlockSpec(memory_space=pl.ANY),
        ],
        out_specs=[pl.BlockSpec((COMB_TM, D), lambda i: (jnp.minimum(done(i), n_ctx - 1), 0)),
                   pl.BlockSpec((COMB_TM, D), lambda i: (jnp.maximum(done(i) - n_ctx, 0), 0))],
        out_shape=[jax.ShapeDtypeStruct((T_CTX, D), F32), jax.ShapeDtypeStruct((T_LAT, D), F32)],
        scratch_shapes=[pltpu.VMEM((2, 2, COMB_TM * NCH, 128), F32), pltpu.VMEM((COMB_TM, D), F32),
                        pltpu.SemaphoreType.DMA((2,))],
        compiler_params=_cp(("arbitrary",)),
        name="combine",
    )(pos, wts_t, x1, modseg, final_g.reshape(1, D), ys)


def _rope_tables():
    n = DEC_SEQ
    pos_row = np.repeat(np.arange(n // GRID_W, dtype=np.float32), GRID_W)
    pos_col = np.tile(np.arange(GRID_W, dtype=np.float32), n // GRID_W)
    half = HD // 2
    inv_freq = jnp.asarray(ROPE_THETA, F32) ** (-jnp.arange(0, half, 2, dtype=F32) / half)
    ang = jnp.concatenate([jnp.asarray(pos_row)[:, None] * inv_freq,
                           jnp.asarray(pos_col)[:, None] * inv_freq], axis=-1)
    cos, sin = jnp.cos(ang), jnp.sin(ang)
    cos128 = jnp.tile(cos, (1, 4))
    sin128 = jnp.tile(jnp.concatenate([-sin, sin], axis=-1), (1, 2))
    return cos128, sin128


def _head_mean_matrix():
    idx = np.arange(2 * HD)
    same = (idx[:, None] // HD) == (idx[None, :] // HD)
    return jnp.asarray(same.astype(np.float32) / HD, BF16)


_SEG_ROWS = np.array([0] * (T_CTX // SEG) + [1 + b for b in range(DEC_BATCH) for _ in range(DEC_SEQ // SEG)])


def kernel(x_prompt, x_sample, cache_k, cache_v, state_rec, c, c_ctx, w_mod, b_mod, norm1_g, norm2_g, w_in, conv_w, conv_b, rg_wa, rg_ba, rg_wx, rg_bx, rg_lambda, q_norm_g, k_norm_g, w_rec_out, w_att_out, w_out, w_router, router_bias, w_gate_e, w_up_e, w_down_e, final_g):
    xa, xb = x_prompt.reshape(T_CTX, D), x_sample.reshape(T_LAT, D)

    cvecs = jnp.concatenate([c_ctx[None, :], c, jnp.zeros((3, D), F32)], axis=0)
    mods = _mods(cvecs, w_mod, b_mod).reshape(DEPTH, 8, 6, D)
    modseg = jnp.pad(mods[:, _SEG_ROWS], ((0, 0), (0, 0), (0, 2), (0, 0)))

    cos128, sin128 = _rope_tables()
    bd = _head_mean_matrix()
    qg128 = jnp.tile(q_norm_g, (1, 2)).reshape(DEPTH, 1, 2 * HD)
    kg128 = jnp.tile(k_norm_g, (1, 2)).reshape(DEPTH, 1, 2 * HD)
    wg = jnp.concatenate([rg_wa[:, 0], rg_wx[:, 0], rg_wa[:, 1], rg_wx[:, 1]], axis=-1)
    pvec = jnp.stack([rg_ba[:, 0], rg_bx[:, 0], rg_ba[:, 1], rg_bx[:, 1],
                      rg_lambda[:, 0], rg_lambda[:, 1], conv_b, jnp.zeros_like(conv_b)], axis=1)
    wrt = w_router.T
    rbias = router_bias.reshape(N_EXP, 1)
    zrow = jnp.zeros((MOE_TM * NCH, 128), F32)

    caches, new_s = (), []
    for l in range(DEPTH):
        proj = _inproj(xa, xb, modseg, norm1_g, w_in, l)
        h0 = jnp.concatenate([jnp.zeros((T_CTX // UNIT, 2, D), F32), state_rec[:, l]], axis=0)
        yrec, stf, stb = _rec(proj, conv_w, pvec, wg, h0, l)
        o_ctx, kc, vc = _attn_ctx(proj, qg128, kg128, bd, l, prev_caches=caches)
        caches = (kc, vc)
        o_lat = _attn_lat(proj, cache_k, cache_v, qg128, kg128, cos128, sin128, bd, l)
        x1, h2, idx, wts = _merge(yrec, o_ctx, o_lat, proj, xa, xb, modseg, norm2_g,
                                  w_rec_out, w_att_out, w_out, wrt, rbias, l)
        pos, meta = _route_pos(idx)
        xs = _dispatch(meta, pos, h2, zrow)
        ys = _experts(meta, xs, w_gate_e, w_up_e, w_down_e, l)
        xa, xb = _combine(pos, wts.T, x1, modseg, final_g, ys, l, final=(l == DEPTH - 1))
        n_cu = T_CTX // UNIT
        spu = UNIT // SEQ
        hf_last = stf[:n_cu].reshape(n_cu, spu, 2, D)[:, :, 1].reshape(BATCH, D)
        hb_first = stb[:n_cu].reshape(n_cu, spu, 2, D)[:, :, 0].reshape(BATCH, D)
        new_s.append(jnp.stack([hf_last, hb_first], axis=1))

    y_prompt = xa.reshape(BATCH, SEQ, D)
    y_sample = xb.reshape(DEC_BATCH, DEC_SEQ, D)
    return (y_prompt, y_sample, caches[0], caches[1], jnp.stack(new_s, axis=1))
```

```python
import functools

import numpy as np
import jax
import jax.numpy as jnp
from jax import lax
from jax.experimental import pallas as pl
from jax.experimental.pallas import tpu as pltpu

F32 = jnp.float32
BF16 = jnp.bfloat16

D = 1024
BATCH = 16
SEQ = 256
DEPTH = 2
DEC_BATCH = 4
DEC_SEQ = 1024
PAST = 256
GRID_W = 64
N_HEADS = 16
N_KV = 4
HD = 64
RG_BLK = 128
RG_C = 8.0
N_EXP = 16
D_EXP = 512
ROPE_THETA = 10000.0
EPS = 1e-6
P_IN = 5632
TINY = float(np.finfo(np.float32).tiny)
NEG_LOG2E = -float(np.log2(np.e))

T_CTX = BATCH * SEQ
T_LAT = DEC_BATCH * DEC_SEQ
T = T_CTX + T_LAT
SEG = 256
UNIT = 1024
N_UNIT = T // UNIT
LANES = 128
SUBLANES = 8
CHUNK = UNIT // SUBLANES
CSTRIDE = CHUNK + SUBLANES

VMEM_LIMIT = 56 * 1024 * 1024


def _cp(sem):
    return pltpu.CompilerParams(dimension_semantics=sem, vmem_limit_bytes=VMEM_LIMIT)


def _split(x):
    hi = x.astype(BF16)
    lo = (x - hi.astype(F32)).astype(BF16)
    return hi, lo


def _sigmoid(x):
    return 0.5 * jnp.tanh(0.5 * x) + 0.5


NCH = D // LANES


def _store_row_tiles(ref, x):
    n = x.shape[0]
    for c in range(NCH):
        ref[pl.ds(c, n, stride=NCH), :] = x[:, c * 128:(c + 1) * 128]


def _load_row_tiles(ref, n):
    return jnp.concatenate([ref[pl.ds(c, n, stride=NCH), :] for c in range(NCH)], axis=-1)


def _dot(a, b):
    return jnp.dot(a, b, preferred_element_type=F32)


def _dot_nt(a, b):
    return lax.dot_general(a, b, (((1,), (1,)), ((), ())), preferred_element_type=F32)


def _mods_kernel(c_ref, w_ref, b_ref, o_ref):
    c = c_ref[...]
    s = c * jax.nn.sigmoid(c)
    s_hi, s_lo = _split(s)
    w_hi, w_lo = _split(w_ref[...])
    o_ref[...] = _dot(s_hi, w_hi) + _dot(s_hi, w_lo) + _dot(s_lo, w_hi) + b_ref[...]


def _mods(cvecs, w_mod, b_mod):
    tn = 1536
    return pl.pallas_call(
        _mods_kernel,
        grid=(DEPTH, 6 * D // tn),
        in_specs=[
            pl.BlockSpec((8, D), lambda l, j: (0, 0)),
            pl.BlockSpec((None, D, tn), lambda l, j: (l, 0, j)),
            pl.BlockSpec((None, 1, tn), lambda l, j: (l, 0, j)),
        ],
        out_specs=pl.BlockSpec((None, 8, tn), lambda l, j: (l, 0, j)),
        out_shape=jax.ShapeDtypeStruct((DEPTH, 8, 6 * D), F32),
        compiler_params=_cp(("arbitrary", "arbitrary")),
        name="mods",
    )(cvecs, w_mod, b_mod.reshape(DEPTH, 1, 6 * D))


def _norm_mod(x, g, shift, scale):
    ms = jnp.mean(x * x, axis=-1, keepdims=True)
    return x * lax.rsqrt(ms + EPS) * g * (1.0 + scale) + shift


def _two_part_specs(tm, n_ctx):
    return [pl.BlockSpec((tm, D), lambda i, *_: (jnp.minimum(i, n_ctx - 1), 0)),
            pl.BlockSpec((tm, D), lambda i, *_: (jnp.maximum(i - n_ctx, 0), 0))]


def _inproj_kernel(xa_ref, xb_ref, mod_ref, g_ref, w_ref, o_ref, h_ref, *, tm):
    def prologue(x_ref):
        def seg(s, carry):
            r0 = pl.multiple_of(s * SEG, SEG)
            m = mod_ref[s]
            h = _norm_mod(x_ref[pl.ds(r0, SEG), :], g_ref[...], m[0:1, :], m[1:2, :])
            h_ref[pl.ds(r0, SEG), :] = h.astype(BF16)
            return carry
        lax.fori_loop(0, tm // SEG, seg, 0)

    first = pl.program_id(1) == 0
    is_ctx = pl.program_id(0) < T_CTX // tm

    @pl.when(first & is_ctx)
    def _():
        prologue(xa_ref)

    @pl.when(first & jnp.logical_not(is_ctx))
    def _():
        prologue(xb_ref)

    o_ref[...] = _dot(h_ref[...], w_ref[...].astype(BF16)).astype(BF16)


def _inproj(xa, xb, modseg, norm_g, w_in, l):
    tm, tn = 2048, 512
    return pl.pallas_call(
        functools.partial(_inproj_kernel, tm=tm),
        grid=(T // tm, P_IN // tn),
        in_specs=_two_part_specs(tm, T_CTX // tm) + [
            pl.BlockSpec((None, tm // SEG, 8, D), lambda i, j: (l, i, 0, 0)),
            pl.BlockSpec((None, 1, D), lambda i, j: (l, 0, 0)),
            pl.BlockSpec((None, D, tn), lambda i, j: (l, 0, j)),
        ],
        out_specs=pl.BlockSpec((tm, tn), lambda i, j: (i, j)),
        out_shape=jax.ShapeDtypeStruct((T, P_IN), BF16),
        scratch_shapes=[pltpu.VMEM((tm, D), BF16)],
        compiler_params=_cp(("arbitrary", "arbitrary")),
        name="inproj",
    )(xa, xb, modseg, norm_g.reshape(DEPTH, 1, D), w_in)


REC_CW = 512
PAD_F = 16
PAD_B = 8
GATE_ROWS = 512


def _rec_kernel(xr_ref, gate_ref, cw_ref, pv_ref, wg_ref, h0_ref,
                y_ref, stf_ref, stb_ref,
                xs_ref, af_ref, bf_ref, ab_ref, bb_ref, nat_ref, wgh_ref):
    u = pl.program_id(0)
    is_ctx = u < (T_CTX // UNIT)
    cps = jnp.where(is_ctx, SEQ // CHUNK, DEC_SEQ // CHUNK)
    nblk = REC_CW // RG_BLK

    def lanes(n):
        return slice(n * RG_BLK, (n + 1) * RG_BLK)

    def tile(r):
        return slice(8 * r, 8 * r + 8)

    for c in range(8):
        for n in range(nblk):
            nat_ref[n, c * CSTRIDE:c * CSTRIDE + CHUNK, :] = xr_ref[c * CHUNK:(c + 1) * CHUNK, lanes(n)].astype(F32)
    for r in range(CHUNK):
        for n in range(nblk):
            xs_ref[PAD_F + 8 * r:PAD_F + 8 * r + 8, lanes(n)] = nat_ref[n, pl.ds(r, 8, stride=CSTRIDE), :]
    chunk_id = lax.broadcasted_iota(jnp.int32, (8, 1), 0)
    seq_start = jnp.bitwise_and(chunk_id, cps - 1) == 0
    seq_end = jnp.bitwise_and(chunk_id, cps - 1) == cps - 1
    for j, r in ((0, CHUNK - 2), (1, CHUNK - 1)):
        prev_chunk = pltpu.roll(xs_ref[PAD_F + 8 * r:PAD_F + 8 * r + 8, :], 1, 0)
        xs_ref[tile(j), :] = jnp.where(seq_start, 0.0, prev_chunk)
    next_chunk = pltpu.roll(xs_ref[PAD_F:PAD_F + 8, :], 7, 0)
    xs_ref[PAD_F + UNIT:PAD_F + UNIT + PAD_B, :] = jnp.where(seq_end, 0.0, next_chunk)

    pv = pv_ref[...]
    cwts = cw_ref[...]
    conv_b = pv[6:7, :]

    def softplus_neg(lam):
        z = -lam
        return jnp.maximum(z, 0.0) + jnp.log1p(jnp.exp(-jnp.abs(z)))

    c4s = tuple((0.5 * RG_C) * softplus_neg(pv[4 + d:5 + d, :]) for d in range(2))
    pv_h = 0.5 * pv
    for n in range(nblk):
        wgh_ref[n] = (0.5 * wg_ref[n]).astype(BF16)
    a_refs = (af_ref, ab_ref)
    b_refs = (bf_ref, bb_ref)

    def gates(g, carry):
        base = pl.multiple_of(g * GATE_ROWS, GATE_ROWS)

        def tap(d):
            return xs_ref[pl.ds(pl.multiple_of(base + PAD_F + 8 * d, 8), GATE_ROWS), :]

        xc = conv_b + tap(-2) * cwts[0:1, :]
        xc = xc + tap(-1) * cwts[1:2, :]
        xc = xc + tap(0) * cwts[2:3, :]
        xc = xc + tap(1) * cwts[3:4, :]
        for n in range(nblk):
            ls = lanes(n)
            xn = xc[:, ls]
            hx = 0.5 * xn
            pre_h = _dot(xn.astype(BF16), wgh_ref[n])
            for d in range(2):
                th_r = jnp.tanh(pre_h[:, (2 * d) * RG_BLK:(2 * d + 1) * RG_BLK] + pv_h[2 * d:2 * d + 1, ls])
                th_i = jnp.tanh(pre_h[:, (2 * d + 1) * RG_BLK:(2 * d + 2) * RG_BLK] + pv_h[2 * d + 1:2 * d + 2, ls])
                c4 = c4s[d][:, ls]
                nla = c4 * th_r + c4
                a = jnp.exp2(nla * NEG_LOG2E)
                s = jnp.tanh(nla) * (a * a + 1.0)
                inp = (s * lax.rsqrt(jnp.maximum(s, TINY))) * (hx * th_i + hx)
                a_refs[d][pl.ds(base, GATE_ROWS), ls] = a
                b_refs[d][pl.ds(base, GATE_ROWS), ls] = inp
        return carry

    lax.fori_loop(0, UNIT // GATE_ROWS, gates, 0)

    hf = hb = jnp.zeros((8, REC_CW), F32)
    pf = pb = jnp.ones((8, REC_CW), F32)
    for r in range(CHUNK):
        rf, rb = tile(r), tile(CHUNK - 1 - r)
        a = af_ref[rf, :]
        hf = a * hf + bf_ref[rf, :]
        pf = a * pf
        bf_ref[rf, :] = hf
        af_ref[rf, :] = pf
        a = ab_ref[rb, :]
        hb = a * hb + bb_ref[rb, :]
        pb = a * pb
        bb_ref[rb, :] = hb
        ab_ref[rb, :] = pb

    h0f = h0_ref[0:1, :]
    h0b = h0_ref[1:2, :]
    cf = [h0f]
    for c in range(1, 8):
        chain = hf[c - 1:c, :] + pf[c - 1:c, :] * cf[c - 1]
        cf.append(jnp.where(jnp.bitwise_and(c, cps - 1) == 0, h0f, chain))
    cb = [None] * 8
    cb[7] = h0b
    for c in range(6, -1, -1):
        chain = hb[c + 1:c + 2, :] + pb[c + 1:c + 2, :] * cb[c + 1]
        cb[c] = jnp.where(jnp.bitwise_and(c, cps - 1) == cps - 1, h0b, chain)
    carry_f = jnp.concatenate(cf, axis=0)
    carry_b = jnp.concatenate(cb, axis=0)
    stf_ref[...] = hf + pf * carry_f
    stb_ref[...] = hb + pb * carry_b

    for r in range(CHUNK):
        h = (bf_ref[tile(r), :] + af_ref[tile(r), :] * carry_f) + (bb_ref[tile(r), :] + ab_ref[tile(r), :] * carry_b)
        for n in range(nblk):
            nat_ref[n, pl.ds(r, 8, stride=CSTRIDE), :] = h[:, lanes(n)]

    for c in range(8):
        rows = slice(c * CHUNK, (c + 1) * CHUNK)
        for n in range(nblk):
            g = gate_ref[rows, lanes(n)].astype(F32)
            h = nat_ref[n, c * CSTRIDE:c * CSTRIDE + CHUNK, :]
            y_ref[rows, lanes(n)] = (h * jax.nn.gelu(g, approximate=True)).astype(BF16)


def _rec(proj, conv_w, pvec, wg, h0, l):
    ncb = D // REC_CW
    return pl.pallas_call(
        _rec_kernel,
        grid=(N_UNIT, ncb),
        in_specs=[
            pl.BlockSpec((UNIT, REC_CW), lambda u, c: (u, c)),
            pl.BlockSpec((UNIT, REC_CW), lambda u, c: (u, ncb + c)),
            pl.BlockSpec((None, 4, REC_CW), lambda u, c: (l, 0, c)),
            pl.BlockSpec((None, 8, REC_CW), lambda u, c: (l, 0, c)),
            pl.BlockSpec((None, REC_CW // RG_BLK, RG_BLK, 4 * RG_BLK), lambda u, c: (l, c, 0, 0)),
            pl.BlockSpec((None, 2, REC_CW), lambda u, c: (u, 0, c)),
        ],
        out_specs=[
            pl.BlockSpec((UNIT, REC_CW), lambda u, c: (u, c)),
            pl.BlockSpec((None, 8, REC_CW), lambda u, c: (u, 0, c)),
            pl.BlockSpec((None, 8, REC_CW), lambda u, c: (u, 0, c)),
        ],
        out_shape=[
            jax.ShapeDtypeStruct((T, D), BF16),
            jax.ShapeDtypeStruct((N_UNIT, 8, D), F32),
            jax.ShapeDtypeStruct((N_UNIT, 8, D), F32),
        ],
        scratch_shapes=[pltpu.VMEM((PAD_F + UNIT + PAD_B, REC_CW), F32)]
        + [pltpu.VMEM((UNIT, REC_CW), F32)] * 4
        + [pltpu.VMEM((REC_CW // RG_BLK, 8 * CSTRIDE, RG_BLK), F32),
           pltpu.VMEM((REC_CW // RG_BLK, RG_BLK, 4 * RG_BLK), BF16)],
        compiler_params=_cp(("arbitrary", "arbitrary")),
        name="rec",
    )(proj, proj, conv_w, pvec, wg, h0)


def _head_norm(x, g128, bd):
    hi, lo = _split(x * x)
    ms = _dot(hi, bd) + _dot(lo, bd)
    return x * lax.rsqrt(ms + EPS) * g128


def _rope(x, cos, sin_signed):
    lane = lax.broadcasted_iota(jnp.int32, x.shape, 1)
    first_half = jnp.bitwise_and(lane, HD - 1) < HD // 2
    partner = jnp.where(first_half, pltpu.roll(x, 2 * HD - HD // 2, 1), pltpu.roll(x, HD // 2, 1))
    return x * cos + partner * sin_signed


def _with_ones(v):
    return jnp.concatenate([v, jnp.ones_like(v)], axis=-1)


def _softmax_pv(q, k, v_ext):
    s = _dot_nt(q, k)
    m = jnp.max(s, axis=-1, keepdims=True)
    p = jnp.exp2(s - m).astype(BF16)
    r = _dot(p, v_ext)
    return r[:, :HD] / r[:, HD:HD + 1]


def _attend_heads(q_ref, k, v_ext):
    return jnp.concatenate([_softmax_pv(q_ref[h], k, v_ext) for h in range(N_HEADS // N_KV)], axis=-1)


def _attn_ctx_kernel(q_ref, k_ref, v_ref, qg_ref, kg_ref, bd_ref, *rest, slab):
    o_ref, ko_ref, vo_ref = rest[-3:]
    for p in range(slab):
        ko_ref[p] = rest[0][p]
        vo_ref[p] = rest[1][p]
    g = N_HEADS // N_KV
    bd = bd_ref[...]
    odd = lax.rem(pl.program_id(1), 2) == 1
    scale = HD ** -0.5 * float(np.log2(np.e))

    kx = _head_norm(k_ref[...].astype(F32), kg_ref[...], bd)
    vx = v_ref[...].astype(F32)
    k_new = jnp.where(odd, kx[:, HD:], kx[:, :HD])
    v_new = jnp.where(odd, vx[:, HD:], vx[:, :HD])
    ko_ref[slab] = k_new
    vo_ref[slab] = v_new
    heads = []
    for j in range(g // 2):
        x = _head_norm(q_ref[:, 2 * HD * j:2 * HD * (j + 1)].astype(F32), qg_ref[...], bd) * scale
        heads += [x[:, :HD].astype(BF16), x[:, HD:].astype(BF16)]
    o = _softmax_pv(jnp.concatenate(heads, axis=0), k_new.astype(BF16), _with_ones(v_new.astype(BF16)))
    o_ref[...] = jnp.concatenate([o[h * SEQ:(h + 1) * SEQ] for h in range(g)], axis=-1).astype(BF16)


def _attn_ctx(proj, qg128, kg128, bd, l, prev_caches=()):
    g = N_HEADS // N_KV
    qcol = 2 * D // (g * HD)
    kcol = 3 * D // (2 * HD)
    vcol = kcol + N_KV // 2
    cache_spec = pl.BlockSpec((None, l + 1, None, SEQ, HD), lambda b, h: (b, 0, h, 0, 0))
    prev_specs = [pl.BlockSpec((None, l, None, SEQ, HD), lambda b, h: (b, 0, h, 0, 0))] * 2 if l else []
    cache_shape = jax.ShapeDtypeStruct((BATCH, l + 1, N_KV, SEQ, HD), F32)
    return pl.pallas_call(
        functools.partial(_attn_ctx_kernel, slab=l),
        grid=(BATCH, N_KV),
        in_specs=[
            pl.BlockSpec((SEQ, g * HD), lambda b, h: (b, qcol + h)),
            pl.BlockSpec((SEQ, 2 * HD), lambda b, h: (b, kcol + h // 2)),
            pl.BlockSpec((SEQ, 2 * HD), lambda b, h: (b, vcol + h // 2)),
            pl.BlockSpec((None, 1, 2 * HD), lambda b, h: (l, 0, 0)),
            pl.BlockSpec((None, 1, 2 * HD), lambda b, h: (l, 0, 0)),
            pl.BlockSpec((2 * HD, 2 * HD), lambda b, h: (0, 0)),
        ] + prev_specs,
        out_specs=[pl.BlockSpec((SEQ, g * HD), lambda b, h: (b, h)), cache_spec, cache_spec],
        out_shape=[jax.ShapeDtypeStruct((T_CTX, D), BF16), cache_shape, cache_shape],
        compiler_params=_cp(("arbitrary", "arbitrary")),
        name="attn_ctx",
    )(proj, proj, proj, qg128, kg128, bd, *prev_caches)


def _attn_lat_kernel(q_ref, k_ref, v_ref, qg_ref, kg_ref, cos_ref, sin_ref, bd_ref, pk_ref, pv_ref,
                     o_ref, q_s, k_s, v_s):
    g = N_HEADS // N_KV
    bd, cos, sin = bd_ref[...], cos_ref[...], sin_ref[...]
    odd = lax.rem(pl.program_id(1), 2) == 1
    scale = HD ** -0.5 * float(np.log2(np.e))

    kx = _rope(_head_norm(k_ref[...].astype(F32), kg_ref[...], bd), cos, sin)
    vx = v_ref[...]
    k_s[0:PAST, :] = pk_ref[...].astype(BF16)
    k_s[PAST:, :] = jnp.where(odd, kx[:, HD:], kx[:, :HD]).astype(BF16)
    v_s[0:PAST, :] = _with_ones(pv_ref[...].astype(BF16))
    v_s[PAST:, :] = _with_ones(jnp.where(odd, vx[:, HD:], vx[:, :HD]))
    for j in range(g // 2):
        x = _head_norm(q_ref[:, 2 * HD * j:2 * HD * (j + 1)].astype(F32), qg_ref[...], bd)
        x = _rope(x, cos, sin) * scale
        q_s[2 * j] = x[:, :HD].astype(BF16)
        q_s[2 * j + 1] = x[:, HD:].astype(BF16)

    o_ref[...] = _attend_heads(q_s, k_s[...], v_s[...]).astype(BF16)


def _attn_lat(proj, cache_k, cache_v, qg128, kg128, cos128, sin128, bd, l):
    g = N_HEADS // N_KV
    row0 = T_CTX // DEC_SEQ
    qcol = 2 * D // (g * HD)
    kcol = 3 * D // (2 * HD)
    vcol = kcol + N_KV // 2
    return pl.pallas_call(
        _attn_lat_kernel,
        grid=(DEC_BATCH, N_KV),
        in_specs=[
            pl.BlockSpec((DEC_SEQ, g * HD), lambda b, h: (row0 + b, qcol + h)),
            pl.BlockSpec((DEC_SEQ, 2 * HD), lambda b, h: (row0 + b, kcol + h // 2)),
            pl.BlockSpec((DEC_SEQ, 2 * HD), lambda b, h: (row0 + b, vcol + h // 2)),
            pl.BlockSpec((None, 1, 2 * HD), lambda b, h: (l, 0, 0)),
            pl.BlockSpec((None, 1, 2 * HD), lambda b, h: (l, 0, 0)),
            pl.BlockSpec((DEC_SEQ, 2 * HD), lambda b, h: (0, 0)),
            pl.BlockSpec((DEC_SEQ, 2 * HD), lambda b, h: (0, 0)),
            pl.BlockSpec((2 * HD, 2 * HD), lambda b, h: (0, 0)),
            pl.BlockSpec((None, None, None, PAST, HD), lambda b, h: (b, l, h, 0, 0)),
            pl.BlockSpec((None, None, None, PAST, HD), lambda b, h: (b, l, h, 0, 0)),
        ],
        out_specs=pl.BlockSpec((DEC_SEQ, g * HD), lambda b, h: (b, h)),
        out_shape=jax.ShapeDtypeStruct((T_LAT, D), BF16),
        scratch_shapes=[pltpu.VMEM((g, DEC_SEQ, HD), BF16), pltpu.VMEM((PAST + DEC_SEQ, HD), BF16),
                        pltpu.VMEM((PAST + DEC_SEQ, 2 * HD), BF16)],
        compiler_params=_cp(("arbitrary", "arbitrary")),
        name="attn_lat",
    )(proj, proj, proj, qg128, kg128, cos128, sin128, bd, cache_k, cache_v)


MERGE_TM = 512


def _route(lt, bias):
    rows = [lt[e:e + 1, :] for e in range(N_EXP)]
    m = rows[0]
    for e in range(1, N_EXP):
        m = jnp.maximum(m, rows[e])
    ex = [jnp.exp(r - m) for r in rows]
    z = ex[0]
    for e in range(1, N_EXP):
        z = z + ex[e]
    probs = [x / z for x in ex]
    sel = [probs[e] + bias[e:e + 1, :] for e in range(N_EXP)]

    def top2_sum(v):
        a, b = jnp.maximum(v[0], v[1]), jnp.minimum(v[0], v[1])
        c, d = jnp.maximum(v[2], v[3]), jnp.minimum(v[2], v[3])
        return jnp.maximum(a, c) + jnp.maximum(jnp.minimum(a, c), jnp.maximum(b, d))

    scores = [top2_sum(sel[4 * g:4 * g + 4]) for g in range(4)]
    best = jnp.zeros_like(scores[0], dtype=jnp.int32)
    best_s = scores[0]
    for g in range(1, 4):
        take = scores[g] > best_s
        best = jnp.where(take, g, best)
        best_s = jnp.where(take, scores[g], best_s)
    cs, cp = [], []
    for j in range(4):
        s_j, p_j = sel[j], probs[j]
        for g in range(1, 4):
            s_j = jnp.where(best == g, sel[4 * g + j], s_j)
            p_j = jnp.where(best == g, probs[4 * g + j], p_j)
        cs.append(s_j)
        cp.append(p_j)
    neg = jnp.full_like(cs[0], -jnp.inf)

    def argmax4(v):
        bi = jnp.zeros_like(best)
        bv = v[0]
        for j in range(1, 4):
            take = v[j] > bv
            bi = jnp.where(take, j, bi)
            bv = jnp.where(take, v[j], bv)
        return bi

    def pick(v, idx):
        out = v[0]
        for j in range(1, 4):
            out = jnp.where(idx == j, v[j], out)
        return out

    i1 = argmax4(cs)
    cs2 = [jnp.where(i1 == j, neg, cs[j]) for j in range(4)]
    i2 = argmax4(cs2)
    i2 = jnp.where((i2 == 0) & (i1 == 0), 1, i2)
    w1, w2 = pick(cp, i1), pick(cp, i2)
    den = w1 + w2
    return best * 4 + i1, best * 4 + i2, w1 / den, w2 / den


def _merge_kernel(yrec_ref, oa_ref, ob_ref, gr0_ref, gr1_ref, ga0_ref, ga1_ref, xa_ref, xb_ref, mod_ref, g2_ref,
                  wrec_ref, watt_ref, wout_ref, wrt_ref, rb_ref,
                  x1_ref, h2_ref, idx_ref, wts_ref,
                  wrec_s, watt_s, wout_s):
    @pl.when(pl.program_id(0) == 0)
    def _():
        wrec_s[...] = wrec_ref[...].astype(BF16)
        watt_s[...] = watt_ref[...].astype(BF16)
        wout_s[...] = wout_ref[...].astype(BF16)

    is_ctx = pl.program_id(0) < T_CTX // MERGE_TM
    args = (yrec_ref, gr0_ref, gr1_ref, ga0_ref, ga1_ref, mod_ref, g2_ref, wrt_ref, rb_ref,
            x1_ref, h2_ref, idx_ref, wts_ref, wrec_s, watt_s, wout_s)

    @pl.when(is_ctx)
    def _():
        _merge_body(oa_ref, xa_ref, *args)

    @pl.when(jnp.logical_not(is_ctx))
    def _():
        _merge_body(ob_ref, xb_ref, *args)


def _merge_body(oatt_ref, x_ref, yrec_ref, gr0_ref, gr1_ref, ga0_ref, ga1_ref, mod_ref, g2_ref, wrt_ref, rb_ref,
                x1_ref, h2_ref, idx_ref, wts_ref, wrec_s, watt_s, wout_s):
    half = D // 2
    b_rec = _dot(yrec_ref[...], wrec_s[...])
    b_att = _dot(oatt_ref[...], watt_s[...])
    m0 = _sigmoid(gr0_ref[...].astype(F32)) * b_rec[:, :half] + _sigmoid(ga0_ref[...].astype(F32)) * b_att[:, :half]
    m1 = _sigmoid(gr1_ref[...].astype(F32)) * b_rec[:, half:] + _sigmoid(ga1_ref[...].astype(F32)) * b_att[:, half:]
    merged = jnp.concatenate([m0, m1], axis=-1).astype(BF16)
    out = _dot(merged, wout_s[...])

    hs = []
    for s in range(MERGE_TM // SEG):
        rows = slice(s * SEG, (s + 1) * SEG)
        m = mod_ref[s]
        x1 = x_ref[rows, :] + m[2:3, :] * out[rows, :]
        x1_ref[rows, :] = x1
        h2 = _norm_mod(x1, g2_ref[...], m[3:4, :], m[4:5, :])
        hs.append(h2)
    h2 = jnp.concatenate(hs, axis=0)
    _store_row_tiles(h2_ref, h2)

    h_hi, h_lo = _split(h2)
    w_hi, w_lo = _split(wrt_ref[...])
    lt = _dot_nt(w_hi, h_hi) + _dot_nt(w_hi, h_lo) + _dot_nt(w_lo, h_hi)
    e1, e2, w1, w2 = _route(lt, rb_ref[...])
    idx_ref[...] = jnp.concatenate([e1, e2], axis=0)
    wts_ref[...] = jnp.concatenate([w1, w2], axis=0)


def _merge(yrec, o_ctx, o_lat, proj, xa, xb, modseg, norm2_g, w_rec_out, w_att_out, w_out, wrt, rbias, l):
    tm = MERGE_TM
    half = D // 2
    gcol = (3 * D + 2 * N_KV * HD) // half
    wspec = pl.BlockSpec((None, D, D), lambda i: (l, 0, 0))
    return pl.pallas_call(
        _merge_kernel,
        grid=(T // tm,),
        in_specs=[pl.BlockSpec((tm, D), lambda i: (i, 0))] + _two_part_specs(tm, T_CTX // tm) + [
            pl.BlockSpec((tm, half), lambda i: (i, gcol)),
            pl.BlockSpec((tm, half), lambda i: (i, gcol + 1)),
            pl.BlockSpec((tm, half), lambda i: (i, gcol + 2)),
            pl.BlockSpec((tm, half), lambda i: (i, gcol + 3)),
        ] + _two_part_specs(tm, T_CTX // tm) + [
            pl.BlockSpec((None, tm // SEG, 8, D), lambda i: (l, i, 0, 0)),
            pl.BlockSpec((None, 1, D), lambda i: (l, 0, 0)),
            wspec, wspec, wspec,
            pl.BlockSpec((N_EXP, D), lambda i: (0, 0)),
            pl.BlockSpec((N_EXP, 1), lambda i: (0, 0)),
        ],
        out_specs=[
            pl.BlockSpec((tm, D), lambda i: (i, 0)),
            pl.BlockSpec((tm * NCH, 128), lambda i: (i, 0)),
            pl.BlockSpec((2, tm), lambda i: (0, i)),
            pl.BlockSpec((2, tm), lambda i: (0, i)),
        ],
        out_shape=[
            jax.ShapeDtypeStruct((T, D), F32),
            jax.ShapeDtypeStruct((T * NCH, 128), F32),
            jax.ShapeDtypeStruct((2, T), jnp.int32),
            jax.ShapeDtypeStruct((2, T), F32),
        ],
        scratch_shapes=[pltpu.VMEM((D, D), BF16)] * 3,
        compiler_params=_cp(("arbitrary",)),
        name="merge",
    )(yrec, o_ctx, o_lat, proj, proj, proj, proj, xa, xb, modseg, norm2_g.reshape(DEPTH, 1, D),
      w_rec_out, w_att_out, w_out, wrt, rbias)


MOE_TM = 512
MOE_NT = 2 * T // MOE_TM + N_EXP
MOE_ROWS = MOE_NT * MOE_TM
META_TILE_E, META_CNT, META_OFF, META_END, META_NT, META_NEXT_E = 0, 1, 2, 3, 4, 5


def _pos_kernel(idx_ref, pos_ref, meta_ref):
    shift = MOE_TM.bit_length() - 1
    idx = idx_ref[...]
    eid = lax.broadcasted_iota(jnp.int32, (N_EXP, T), 0)
    m0 = eid == idx[0:1, :]
    m1 = eid == idx[1:2, :]
    member = jnp.where(m0 | m1, 1.0, 0.0)
    cnt = jnp.sum(member, axis=1, keepdims=True).astype(jnp.int32)
    ntile = jnp.right_shift(cnt + (MOE_TM - 1), shift)
    offs, acc = [], jnp.zeros((1, 1), jnp.int32)
    for e in range(N_EXP):
        offs.append(acc)
        acc = acc + ntile[e:e + 1, :]
    off_t = jnp.concatenate(offs, axis=0)
    end_t = off_t + ntile

    blk = 256
    r_i = lax.broadcasted_iota(jnp.int32, (blk, blk), 0)
    c_i = lax.broadcasted_iota(jnp.int32, (blk, blk), 1)
    upper = jnp.where(r_i <= c_i, 1.0, 0.0).astype(BF16)
    run = (off_t * MOE_TM).astype(F32)
    for j in range(T // blk):
        ls = slice(j * blk, (j + 1) * blk)
        mb = member[:, ls]
        inc = _dot(mb.astype(BF16), upper)
        dest = run + inc - mb
        pos_ref[0:1, ls] = jnp.sum(jnp.where(m0[:, ls], dest, 0.0), axis=0, keepdims=True).astype(jnp.int32)
        pos_ref[1:2, ls] = jnp.sum(jnp.where(m1[:, ls], dest, 0.0), axis=0, keepdims=True).astype(jnp.int32)
        run = run + inc[:, blk - 1:blk]

    lane = lax.broadcasted_iota(jnp.int32, (1, 128), 1)
    zero = jnp.zeros((1, 128), jnp.int32)
    tile_e, cnt_row, off_row, end_row = zero, zero, zero, zero
    for e in range(N_EXP):
        tile_e = tile_e + jnp.where(lane >= end_t[e:e + 1, :], 1, 0)
        here = lane == e
        cnt_row = jnp.where(here, cnt[e:e + 1, :], cnt_row)
        off_row = jnp.where(here, off_t[e:e + 1, :] * MOE_TM, off_row)
        end_row = jnp.where(here, end_t[e:e + 1, :] * MOE_TM, end_row)
    tile_e = jnp.minimum(tile_e, N_EXP - 1)
    nt_row = zero + acc
    next_row = zero
    nxt = jnp.full((1, 1), -1, jnp.int32)
    for e in reversed(range(N_EXP)):
        next_row = jnp.where(lane == e, nxt, next_row)
        nxt = jnp.where(cnt[e:e + 1, :] > 0, e, nxt)
    meta_ref[...] = jnp.concatenate([tile_e, cnt_row, off_row, end_row, nt_row, next_row, zero, zero], axis=0)


def _route_pos(idx):
    return pl.pallas_call(
        _pos_kernel,
        grid=(1,),
        in_specs=[pl.BlockSpec((2, T), lambda i: (0, 0))],
        out_specs=[pl.BlockSpec((2, T), lambda i: (0, 0)), pl.BlockSpec((8, 128), lambda i: (0, 0))],
        out_shape=[jax.ShapeDtypeStruct((2, T), jnp.int32), jax.ShapeDtypeStruct((8, 128), jnp.int32)],
        compiler_params=_cp(("arbitrary",)),
        name="route_pos",
    )(idx)


DISP_TM = 2048
DISP_EXPERTS = N_EXP // (T // DISP_TM)


def _dispatch_kernel(meta_ref, pos_ref, h_ref, z_hbm, xs_hbm, sem):
    i = pl.program_id(0)

    def row_copy(src, src_row, dst_row):
        return pltpu.make_async_copy(src.at[pl.ds(src_row * NCH, NCH), :],
                                     xs_hbm.at[pl.ds(pl.multiple_of(dst_row * NCH, NCH), NCH), :], sem)

    for r in range(DISP_TM):
        row_copy(h_ref, r, pos_ref[0, r]).start(priority=0)
        row_copy(h_ref, r, pos_ref[1, r]).start(priority=1)

    def zero_copies(act):
        for q in range(DISP_EXPERTS):
            e = i * DISP_EXPERTS + q
            s = meta_ref[META_OFF, e] + meta_ref[META_CNT, e]
            npad = meta_ref[META_END, e] - s
            for bit in reversed(range(MOE_TM.bit_length() - 1)):
                size = 1 << bit
                part = jnp.bitwise_and(npad, size)

                @pl.when(part != 0)
                def _():
                    dst = pl.ds(pl.multiple_of(s * NCH, NCH), size * NCH)
                    act(pltpu.make_async_copy(z_hbm.at[pl.ds(0, size * NCH), :], xs_hbm.at[dst, :], sem))

                s = s + part
            tail = meta_ref[META_NT, 0] + e

            @pl.when(tail < MOE_NT)
            def _():
                rows = pl.ds(pl.multiple_of(tail * (MOE_TM * NCH), MOE_TM * NCH), MOE_TM * NCH)
                act(pltpu.make_async_copy(z_hbm, xs_hbm.at[rows, :], sem))

    zero_copies(lambda c: c.start())
    for _ in range(2):
        pltpu.make_async_copy(h_ref, xs_hbm.at[pl.ds(0, DISP_TM * NCH), :], sem).wait()
    zero_copies(lambda c: c.wait())


def _dispatch(meta, pos, h2, zrow):
    return pl.pallas_call(
        _dispatch_kernel,
        grid_spec=pltpu.PrefetchScalarGridSpec(
            num_scalar_prefetch=1,
            grid=(T // DISP_TM,),
            in_specs=[
                pl.BlockSpec((2, DISP_TM), lambda i, meta: (0, i), memory_space=pltpu.SMEM),
                pl.BlockSpec((DISP_TM * NCH, 128), lambda i, meta: (i, 0)),
                pl.BlockSpec((MOE_TM * NCH, 128), lambda i, meta: (0, 0)),
            ],
            out_specs=pl.BlockSpec(memory_space=pl.ANY),
            scratch_shapes=[pltpu.SemaphoreType.DMA],
        ),
        out_shape=jax.ShapeDtypeStruct((MOE_ROWS * NCH, 128), F32),
        compiler_params=_cp(("arbitrary",)),
        name="dispatch",
    )(meta, pos, h2, zrow)


def _experts_kernel(meta_ref, xs_ref, wg_hbm, wu_hbm, wd_hbm, ys_ref,
                    wg_f, wu_f, wd_f, wg_s, wu_s, wd_s, sem, *, l):
    j = pl.program_id(0)
    live = j < meta_ref[META_NT, 0]
    e = meta_ref[META_TILE_E, j]
    e_prev = meta_ref[META_TILE_E, jnp.maximum(j - 1, 0)]

    def fetch(ex):
        return (pltpu.make_async_copy(wg_hbm.at[l, ex], wg_f, sem.at[0]),
                pltpu.make_async_copy(wu_hbm.at[l, ex], wu_f, sem.at[1]),
                pltpu.make_async_copy(wd_hbm.at[l, ex], wd_f, sem.at[2]))

    @pl.when(j == 0)
    def _():
        for c in fetch(e):
            c.start()

    @pl.when(live & ((j == 0) | (e != e_prev)))
    def _():
        for c, dst, src in zip(fetch(e), (wg_s, wu_s, wd_s), (wg_f, wu_f, wd_f)):
            c.wait()
            dst[...] = src[...].astype(BF16)
        nxt = meta_ref[META_NEXT_E, e]

        @pl.when(nxt >= 0)
        def _():
            for c in fetch(nxt):
                c.start()

    @pl.when(live)
    def _():
        x = _load_row_tiles(xs_ref, MOE_TM).astype(BF16)
        g = _dot(x, wg_s[...])
        u = _dot(x, wu_s[...])
        act = (g * _sigmoid(g)) * u
        _store_row_tiles(ys_ref, _dot(act.astype(BF16), wd_s[...]))

    @pl.when(jnp.logical_not(live))
    def _():
        ys_ref[...] = jnp.zeros_like(ys_ref)


def _experts(meta, xs, w_gate_e, w_up_e, w_down_e, l):
    def tile(j, meta):
        return jnp.minimum(j, meta[META_NT, 0] - 1)

    return pl.pallas_call(
        functools.partial(_experts_kernel, l=l),
        grid_spec=pltpu.PrefetchScalarGridSpec(
            num_scalar_prefetch=1,
            grid=(MOE_NT,),
            in_specs=[
                pl.BlockSpec((MOE_TM * NCH, 128), lambda j, meta: (tile(j, meta), 0)),
                pl.BlockSpec(memory_space=pl.ANY),
                pl.BlockSpec(memory_space=pl.ANY),
                pl.BlockSpec(memory_space=pl.ANY),
            ],
            out_specs=pl.BlockSpec((MOE_TM * NCH, 128), lambda j, meta: (j, 0)),
            scratch_shapes=[pltpu.VMEM((D, D_EXP), F32), pltpu.VMEM((D, D_EXP), F32), pltpu.VMEM((D_EXP, D), F32),
                            pltpu.VMEM((D, D_EXP), BF16), pltpu.VMEM((D, D_EXP), BF16), pltpu.VMEM((D_EXP, D), BF16),
                            pltpu.SemaphoreType.DMA((3,))],
        ),
        out_shape=jax.ShapeDtypeStruct((MOE_ROWS * NCH, 128), F32),
        compiler_params=_cp(("arbitrary",)),
        name="experts",
    )(meta, xs, w_gate_e, w_up_e, w_down_e)


COMB_TM = SEG


def _combine_kernel(pos_ref, w_ref, x1_ref, mod_ref, fg_ref, ys_hbm, oa_ref, ob_ref, buf, y_s, sem, *, final):
    i = pl.program_id(0)
    n = pl.num_programs(0) - 1
    n_ctx = T_CTX // COMB_TM

    for s in range(2):
        @pl.when((i < n) & (lax.rem(i, 2) == s))
        def _():
            for r in range(COMB_TM):
                for k in range(2):
                    src = pl.ds(pl.multiple_of(pos_ref[k, r] * NCH, NCH), NCH)
                    pltpu.make_async_copy(ys_hbm.at[src, :], buf.at[s, k, pl.ds(r * NCH, NCH), :],
                                          sem.at[s]).start(priority=k)

    for slot in range(2):
        @pl.when((i > 0) & (lax.rem(i - 1, 2) == slot))
        def _():
            for k in range(2):
                pltpu.make_async_copy(ys_hbm.at[pl.ds(0, COMB_TM * NCH), :], buf.at[slot, k], sem.at[slot]).wait()
            w = w_ref[...]
            y = (w[:, 0:1] * _load_row_tiles(buf.at[slot, 0], COMB_TM)
                 + w[:, 1:2] * _load_row_tiles(buf.at[slot, 1], COMB_TM))
            y_s[...] = y

    @pl.when(i > 0)
    def _():
        x = x1_ref[...] + mod_ref[5:6, :] * y_s[...]
        if final:
            ms = jnp.mean(x * x, axis=-1, keepdims=True)
            x = x * lax.rsqrt(ms + EPS) * fg_ref[...]

        @pl.when(i - 1 < n_ctx)
        def _():
            oa_ref[...] = x

        @pl.when(i - 1 >= n_ctx)
        def _():
            ob_ref[...] = x


def _combine(pos, wts_t, x1, modseg, final_g, ys, l, final):
    n = T // COMB_TM
    n_ctx = T_CTX // COMB_TM

    def done(i):
        return jnp.maximum(i - 1, 0)

    return pl.pallas_call(
        functools.partial(_combine_kernel, final=final),
        grid=(n + 1,),
        in_specs=[
            pl.BlockSpec((2, COMB_TM), lambda i: (0, jnp.minimum(i, n - 1)), memory_space=pltpu.SMEM),
            pl.BlockSpec((COMB_TM, 2), lambda i: (done(i), 0)),
            pl.BlockSpec((COMB_TM, D), lambda i: (done(i), 0)),
            pl.BlockSpec((None, None, 8, D), lambda i: (l, done(i), 0, 0)),
            pl.BlockSpec((1, D), lambda i: (0, 0)),
            pl.BlockSpec(memory_space=pl.ANY),
        ],
        out_specs=[pl.BlockSpec((COMB_TM, D), lambda i: (jnp.minimum(done(i), n_ctx - 1), 0)),
                   pl.BlockSpec((COMB_TM, D), lambda i: (jnp.maximum(done(i) - n_ctx, 0), 0))],
        out_shape=[jax.ShapeDtypeStruct((T_CTX, D), F32), jax.ShapeDtypeStruct((T_LAT, D), F32)],
        scratch_shapes=[pltpu.VMEM((2, 2, COMB_TM * NCH, 128), F32), pltpu.VMEM((COMB_TM, D), F32),
                        pltpu.SemaphoreType.DMA((2,))],
        compiler_params=_cp(("arbitrary",)),
        name="combine",
    )(pos, wts_t, x1, modseg, final_g.reshape(1, D), ys)


def _rope_tables():
    n = DEC_SEQ
    pos_row = np.repeat(np.arange(n // GRID_W, dtype=np.float32), GRID_W)
    pos_col = np.tile(np.arange(GRID_W, dtype=np.float32), n // GRID_W)
    half = HD // 2
    inv_freq = jnp.asarray(ROPE_THETA, F32) ** (-jnp.arange(0, half, 2, dtype=F32) / half)
    ang = jnp.concatenate([jnp.asarray(pos_row)[:, None] * inv_freq,
                           jnp.asarray(pos_col)[:, None] * inv_freq], axis=-1)
    cos, sin = jnp.cos(ang), jnp.sin(ang)
    cos128 = jnp.tile(cos, (1, 4))
    sin128 = jnp.tile(jnp.concatenate([-sin, sin], axis=-1), (1, 2))
    return cos128, sin128


def _head_mean_matrix():
    idx = np.arange(2 * HD)
    same = (idx[:, None] // HD) == (idx[None, :] // HD)
    return jnp.asarray(same.astype(np.float32) / HD, BF16)


_SEG_ROWS = np.array([0] * (T_CTX // SEG) + [1 + b for b in range(DEC_BATCH) for _ in range(DEC_SEQ // SEG)])


def kernel(x_prompt, x_sample, cache_k, cache_v, state_rec, c, c_ctx, w_mod, b_mod, norm1_g, norm2_g, w_in, conv_w, conv_b, rg_wa, rg_ba, rg_wx, rg_bx, rg_lambda, q_norm_g, k_norm_g, w_rec_out, w_att_out, w_out, w_router, router_bias, w_gate_e, w_up_e, w_down_e, final_g):
    xa, xb = x_prompt.reshape(T_CTX, D), x_sample.reshape(T_LAT, D)

    cvecs = jnp.concatenate([c_ctx[None, :], c, jnp.zeros((3, D), F32)], axis=0)
    mods = _mods(cvecs, w_mod, b_mod).reshape(DEPTH, 8, 6, D)
    modseg = jnp.pad(mods[:, _SEG_ROWS], ((0, 0), (0, 0), (0, 2), (0, 0)))

    cos128, sin128 = _rope_tables()
    bd = _head_mean_matrix()
    qg128 = jnp.tile(q_norm_g, (1, 2)).reshape(DEPTH, 1, 2 * HD)
    kg128 = jnp.tile(k_norm_g, (1, 2)).reshape(DEPTH, 1, 2 * HD)
    wg = jnp.concatenate([rg_wa[:, 0], rg_wx[:, 0], rg_wa[:, 1], rg_wx[:, 1]], axis=-1)
    pvec = jnp.stack([rg_ba[:, 0], rg_bx[:, 0], rg_ba[:, 1], rg_bx[:, 1],
                      rg_lambda[:, 0], rg_lambda[:, 1], conv_b, jnp.zeros_like(conv_b)], axis=1)
    wrt = w_router.T
    rbias = router_bias.reshape(N_EXP, 1)
    zrow = jnp.zeros((MOE_TM * NCH, 128), F32)

    caches, new_s = (), []
    for l in range(DEPTH):
        proj = _inproj(xa, xb, modseg, norm1_g, w_in, l)
        h0 = jnp.concatenate([jnp.zeros((T_CTX // UNIT, 2, D), F32), state_rec[:, l]], axis=0)
        yrec, stf, stb = _rec(proj, conv_w, pvec, wg, h0, l)
        o_ctx, kc, vc = _attn_ctx(proj, qg128, kg128, bd, l, prev_caches=caches)
        caches = (kc, vc)
        o_lat = _attn_lat(proj, cache_k, cache_v, qg128, kg128, cos128, sin128, bd, l)
        x1, h2, idx, wts = _merge(yrec, o_ctx, o_lat, proj, xa, xb, modseg, norm2_g,
                                  w_rec_out, w_att_out, w_out, wrt, rbias, l)
        pos, meta = _route_pos(idx)
        xs = _dispatch(meta, pos, h2, zrow)
        ys = _experts(meta, xs, w_gate_e, w_up_e, w_down_e, l)
        xa, xb = _combine(pos, wts.T, x1, modseg, final_g, ys, l, final=(l == DEPTH - 1))
        n_cu = T_CTX // UNIT
        spu = UNIT // SEQ
        hf_last = stf[:n_cu].reshape(n_cu, spu, 2, D)[:, :, 1].reshape(BATCH, D)
        hb_first = stb[:n_cu].reshape(n_cu, spu, 2, D)[:, :, 0].reshape(BATCH, D)
        new_s.append(jnp.stack([hf_last, hb_first], axis=1))

    y_prompt = xa.reshape(BATCH, SEQ, D)
    y_sample = xb.reshape(DEC_BATCH, DEC_SEQ, D)
    return (y_prompt, y_sample, caches[0], caches[1], jnp.stack(new_s, axis=1))
```

```python
import functools

import numpy as np
import jax
import jax.numpy as jnp
from jax import lax
from jax.experimental import pallas as pl
from jax.experimental.pallas import tpu as pltpu

F32 = jnp.float32
BF16 = jnp.bfloat16

D = 1024
BATCH = 16
SEQ = 256
DEPTH = 2
DEC_BATCH = 4
DEC_SEQ = 1024
PAST = 256
GRID_W = 64
N_HEADS = 16
N_KV = 4
HD = 64
RG_BLK = 128
RG_C = 8.0
N_EXP = 16
D_EXP = 512
ROPE_THETA = 10000.0
EPS = 1e-6
P_IN = 5632
TINY = float(np.finfo(np.float32).tiny)
NEG_LOG2E = -float(np.log2(np.e))

T_CTX = BATCH * SEQ
T_LAT = DEC_BATCH * DEC_SEQ
T = T_CTX + T_LAT
SEG = 256
UNIT = 1024
N_UNIT = T // UNIT
LANES = 128
SUBLANES = 8
CHUNK = UNIT // SUBLANES
CSTRIDE = CHUNK + SUBLANES

VMEM_LIMIT = 56 * 1024 * 1024


def _cp(sem):
    return pltpu.CompilerParams(dimension_semantics=sem, vmem_limit_bytes=VMEM_LIMIT)


def _split(x):
    hi = x.astype(BF16)
    lo = (x - hi.astype(F32)).astype(BF16)
    return hi, lo


def _sigmoid(x):
    return 0.5 * jnp.tanh(0.5 * x) + 0.5


NCH = D // LANES


def _store_row_tiles(ref, x):
    n = x.shape[0]
    for c in range(NCH):
        ref[pl.ds(c, n, stride=NCH), :] = x[:, c * 128:(c + 1) * 128]


def _load_row_tiles(ref, n):
    return jnp.concatenate([ref[pl.ds(c, n, stride=NCH), :] for c in range(NCH)], axis=-1)


def _dot(a, b):
    return jnp.dot(a, b, preferred_element_type=F32)


def _dot_nt(a, b):
    return lax.dot_general(a, b, (((1,), (1,)), ((), ())), preferred_element_type=F32)


def _mods_kernel(c_ref, w_ref, b_ref, o_ref):
    c = c_ref[...]
    s = c * jax.nn.sigmoid(c)
    s_hi, s_lo = _split(s)
    w_hi, w_lo = _split(w_ref[...])
    o_ref[...] = _dot(s_hi, w_hi) + _dot(s_hi, w_lo) + _dot(s_lo, w_hi) + b_ref[...]


def _mods(cvecs, w_mod, b_mod):
    tn = 1536
    return pl.pallas_call(
        _mods_kernel,
        grid=(DEPTH, 6 * D // tn),
        in_specs=[
            pl.BlockSpec((8, D), lambda l, j: (0, 0)),
            pl.BlockSpec((None, D, tn), lambda l, j: (l, 0, j)),
            pl.BlockSpec((None, 1, tn), lambda l, j: (l, 0, j)),
        ],
        out_specs=pl.BlockSpec((None, 8, tn), lambda l, j: (l, 0, j)),
        out_shape=jax.ShapeDtypeStruct((DEPTH, 8, 6 * D), F32),
        compiler_params=_cp(("arbitrary", "arbitrary")),
        name="mods",
    )(cvecs, w_mod, b_mod.reshape(DEPTH, 1, 6 * D))


def _norm_mod(x, g, shift, scale):
    ms = jnp.mean(x * x, axis=-1, keepdims=True)
    return x * lax.rsqrt(ms + EPS) * g * (1.0 + scale) + shift


def _two_part_specs(tm, n_ctx):
    return [pl.BlockSpec((tm, D), lambda i, *_: (jnp.minimum(i, n_ctx - 1), 0)),
            pl.BlockSpec((tm, D), lambda i, *_: (jnp.maximum(i - n_ctx, 0), 0))]


def _inproj_kernel(xa_ref, xb_ref, mod_ref, g_ref, w_ref, o_ref, h_ref, *, tm):
    def prologue(x_ref):
        def seg(s, carry):
            r0 = pl.multiple_of(s * SEG, SEG)
            m = mod_ref[s]
            h = _norm_mod(x_ref[pl.ds(r0, SEG), :], g_ref[...], m[0:1, :], m[1:2, :])
            h_ref[pl.ds(r0, SEG), :] = h.astype(BF16)
            return carry
        lax.fori_loop(0, tm // SEG, seg, 0)

    first = pl.program_id(1) == 0
    is_ctx = pl.program_id(0) < T_CTX // tm

    @pl.when(first & is_ctx)
    def _():
        prologue(xa_ref)

    @pl.when(first & jnp.logical_not(is_ctx))
    def _():
        prologue(xb_ref)

    o_ref[...] = _dot(h_ref[...], w_ref[...].astype(BF16)).astype(BF16)


def _inproj(xa, xb, modseg, norm_g, w_in, l):
    tm, tn = 2048, 512
    return pl.pallas_call(
        functools.partial(_inproj_kernel, tm=tm),
        grid=(T // tm, P_IN // tn),
        in_specs=_two_part_specs(tm, T_CTX // tm) + [
            pl.BlockSpec((None, tm // SEG, 8, D), lambda i, j: (l, i, 0, 0)),
            pl.BlockSpec((None, 1, D), lambda i, j: (l, 0, 0)),
            pl.BlockSpec((None, D, tn), lambda i, j: (l, 0, j)),
        ],
        out_specs=pl.BlockSpec((tm, tn), lambda i, j: (i, j)),
        out_shape=jax.ShapeDtypeStruct((T, P_IN), BF16),
        scratch_shapes=[pltpu.VMEM((tm, D), BF16)],
        compiler_params=_cp(("arbitrary", "arbitrary")),
        name="inproj",
    )(xa, xb, modseg, norm_g.reshape(DEPTH, 1, D), w_in)


REC_CW = 512
PAD_F = 16
PAD_B = 8
GATE_ROWS = 512


def _rec_kernel(xr_ref, gate_ref, cw_ref, pv_ref, wg_ref, h0_ref,
                y_ref, stf_ref, stb_ref,
                xs_ref, af_ref, bf_ref, ab_ref, bb_ref, nat_ref, wgh_ref):
    u = pl.program_id(0)
    is_ctx = u < (T_CTX // UNIT)
    cps = jnp.where(is_ctx, SEQ // CHUNK, DEC_SEQ // CHUNK)
    nblk = REC_CW // RG_BLK

    def lanes(n):
        return slice(n * RG_BLK, (n + 1) * RG_BLK)

    def tile(r):
        return slice(8 * r, 8 * r + 8)

    for c in range(8):
        for n in range(nblk):
            nat_ref[n, c * CSTRIDE:c * CSTRIDE + CHUNK, :] = xr_ref[c * CHUNK:(c + 1) * CHUNK, lanes(n)].astype(F32)
    for r in range(CHUNK):
        for n in range(nblk):
            xs_ref[PAD_F + 8 * r:PAD_F + 8 * r + 8, lanes(n)] = nat_ref[n, pl.ds(r, 8, stride=CSTRIDE), :]
    chunk_id = lax.broadcasted_iota(jnp.int32, (8, 1), 0)
    seq_start = jnp.bitwise_and(chunk_id, cps - 1) == 0
    seq_end = jnp.bitwise_and(chunk_id, cps - 1) == cps - 1
    for j, r in ((0, CHUNK - 2), (1, CHUNK - 1)):
        prev_chunk = pltpu.roll(xs_ref[PAD_F + 8 * r:PAD_F + 8 * r + 8, :], 1, 0)
        xs_ref[tile(j), :] = jnp.where(seq_start, 0.0, prev_chunk)
    next_chunk = pltpu.roll(xs_ref[PAD_F:PAD_F + 8, :], 7, 0)
    xs_ref[PAD_F + UNIT:PAD_F + UNIT + PAD_B, :] = jnp.where(seq_end, 0.0, next_chunk)

    pv = pv_ref[...]
    cwts = cw_ref[...]
    conv_b = pv[6:7, :]

    def softplus_neg(lam):
        z = -lam
        return jnp.maximum(z, 0.0) + jnp.log1p(jnp.exp(-jnp.abs(z)))

    c4s = tuple((0.5 * RG_C) * softplus_neg(pv[4 + d:5 + d, :]) for d in range(2))
    pv_h = 0.5 * pv
    for n in range(nblk):
        wgh_ref[n] = (0.5 * wg_ref[n]).astype(BF16)
    a_refs = (af_ref, ab_ref)
    b_refs = (bf_ref, bb_ref)

    def gates(g, carry):
        base = pl.multiple_of(g * GATE_ROWS, GATE_ROWS)

        def tap(d):
            return xs_ref[pl.ds(pl.multiple_of(base + PAD_F + 8 * d, 8), GATE_ROWS), :]

        xc = conv_b + tap(-2) * cwts[0:1, :]
        xc = xc + tap(-1) * cwts[1:2, :]
        xc = xc + tap(0) * cwts[2:3, :]
        xc = xc + tap(1) * cwts[3:4, :]
        for n in range(nblk):
            ls = lanes(n)
            xn = xc[:, ls]
            hx = 0.5 * xn
            pre_h = _dot(xn.astype(BF16), wgh_ref[n])
            for d in range(2):
                th_r = jnp.tanh(pre_h[:, (2 * d) * RG_BLK:(2 * d + 1) * RG_BLK] + pv_h[2 * d:2 * d + 1, ls])
                th_i = jnp.tanh(pre_h[:, (2 * d + 1) * RG_BLK:(2 * d + 2) * RG_BLK] + pv_h[2 * d + 1:2 * d + 2, ls])
                c4 = c4s[d][:, ls]
                nla = c4 * th_r + c4
                a = jnp.exp2(nla * NEG_LOG2E)
                s = jnp.tanh(nla) * (a * a + 1.0)
                inp = (s * lax.rsqrt(jnp.maximum(s, TINY))) * (hx * th_i + hx)
                a_refs[d][pl.ds(base, GATE_ROWS), ls] = a
                b_refs[d][pl.ds(base, GATE_ROWS), ls] = inp
        return carry

    lax.fori_loop(0, UNIT // GATE_ROWS, gates, 0)

    hf = hb = jnp.zeros((8, REC_CW), F32)
    pf = pb = jnp.ones((8, REC_CW), F32)
    for r in range(CHUNK):
        rf, rb = tile(r), tile(CHUNK - 1 - r)
        a = af_ref[rf, :]
        hf = a * hf + bf_ref[rf, :]
        pf = a * pf
        bf_ref[rf, :] = hf
        af_ref[rf, :] = pf
        a = ab_ref[rb, :]
        hb = a * hb + bb_ref[rb, :]
        pb = a * pb
        bb_ref[rb, :] = hb
        ab_ref[rb, :] = pb

    h0f = h0_ref[0:1, :]
    h0b = h0_ref[1:2, :]
    cf = [h0f]
    for c in range(1, 8):
        chain = hf[c - 1:c, :] + pf[c - 1:c, :] * cf[c - 1]
        cf.append(jnp.where(jnp.bitwise_and(c, cps - 1) == 0, h0f, chain))
    cb = [None] * 8
    cb[7] = h0b
    for c in range(6, -1, -1):
        chain = hb[c + 1:c + 2, :] + pb[c + 1:c + 2, :] * cb[c + 1]
        cb[c] = jnp.where(jnp.bitwise_and(c, cps - 1) == cps - 1, h0b, chain)
    carry_f = jnp.concatenate(cf, axis=0)
    carry_b = jnp.concatenate(cb, axis=0)
    stf_ref[...] = hf + pf * carry_f
    stb_ref[...] = hb + pb * carry_b

    for r in range(CHUNK):
        h = (bf_ref[tile(r), :] + af_ref[tile(r), :] * carry_f) + (bb_ref[tile(r), :] + ab_ref[tile(r), :] * carry_b)
        for n in range(nblk):
            nat_ref[n, pl.ds(r, 8, stride=CSTRIDE), :] = h[:, lanes(n)]

    for c in range(8):
        rows = slice(c * CHUNK, (c + 1) * CHUNK)
        for n in range(nblk):
            g = gate_ref[rows, lanes(n)].astype(F32)
            h = nat_ref[n, c * CSTRIDE:c * CSTRIDE + CHUNK, :]
            y_ref[rows, lanes(n)] = (h * jax.nn.gelu(g, approximate=True)).astype(BF16)


def _rec(proj, conv_w, pvec, wg, h0, l):
    ncb = D // REC_CW
    return pl.pallas_call(
        _rec_kernel,
        grid=(N_UNIT, ncb),
        in_specs=[
            pl.BlockSpec((UNIT, REC_CW), lambda u, c: (u, c)),
            pl.BlockSpec((UNIT, REC_CW), lambda u, c: (u, ncb + c)),
            pl.BlockSpec((None, 4, REC_CW), lambda u, c: (l, 0, c)),
            pl.BlockSpec((None, 8, REC_CW), lambda u, c: (l, 0, c)),
            pl.BlockSpec((None, REC_CW // RG_BLK, RG_BLK, 4 * RG_BLK), lambda u, c: (l, c, 0, 0)),
            pl.BlockSpec((None, 2, REC_CW), lambda u, c: (u, 0, c)),
        ],
        out_specs=[
            pl.BlockSpec((UNIT, REC_CW), lambda u, c: (u, c)),
            pl.BlockSpec((None, 8, REC_CW), lambda u, c: (u, 0, c)),
            pl.BlockSpec((None, 8, REC_CW), lambda u, c: (u, 0, c)),
        ],
        out_shape=[
            jax.ShapeDtypeStruct((T, D), BF16),
            jax.ShapeDtypeStruct((N_UNIT, 8, D), F32),
            jax.ShapeDtypeStruct((N_UNIT, 8, D), F32),
        ],
        scratch_shapes=[pltpu.VMEM((PAD_F + UNIT + PAD_B, REC_CW), F32)]
        + [pltpu.VMEM((UNIT, REC_CW), F32)] * 4
        + [pltpu.VMEM((REC_CW // RG_BLK, 8 * CSTRIDE, RG_BLK), F32),
           pltpu.VMEM((REC_CW // RG_BLK, RG_BLK, 4 * RG_BLK), BF16)],
        compiler_params=_cp(("arbitrary", "arbitrary")),
        name="rec",
    )(proj, proj, conv_w, pvec, wg, h0)


def _head_norm(x, g128, bd):
    hi, lo = _split(x * x)
    ms = _dot(hi, bd) + _dot(lo, bd)
    return x * lax.rsqrt(ms + EPS) * g128


def _rope(x, cos, sin_signed):
    lane = lax.broadcasted_iota(jnp.int32, x.shape, 1)
    first_half = jnp.bitwise_and(lane, HD - 1) < HD // 2
    partner = jnp.where(first_half, pltpu.roll(x, 2 * HD - HD // 2, 1), pltpu.roll(x, HD // 2, 1))
    return x * cos + partner * sin_signed


def _with_ones(v):
    return jnp.concatenate([v, jnp.ones_like(v)], axis=-1)


def _softmax_pv(q, k, v_ext):
    s = _dot_nt(q, k)
    m = jnp.max(s, axis=-1, keepdims=True)
    p = jnp.exp2(s - m).astype(BF16)
    r = _dot(p, v_ext)
    return r[:, :HD] / r[:, HD:HD + 1]


def _attend_heads(q_ref, k, v_ext):
    return jnp.concatenate([_softmax_pv(q_ref[h], k, v_ext) for h in range(N_HEADS // N_KV)], axis=-1)


def _attn_ctx_kernel(q_ref, k_ref, v_ref, qg_ref, kg_ref, bd_ref, *rest, slab):
    o_ref, ko_ref, vo_ref = rest[-3:]
    for p in range(slab):
        ko_ref[p] = rest[0][p]
        vo_ref[p] = rest[1][p]
    g = N_HEADS // N_KV
    bd = bd_ref[...]
    odd = lax.rem(pl.program_id(1), 2) == 1
    scale = HD ** -0.5 * float(np.log2(np.e))

    kx = _head_norm(k_ref[...].astype(F32), kg_ref[...], bd)
    vx = v_ref[...].astype(F32)
    k_new = jnp.where(odd, kx[:, HD:], kx[:, :HD])
    v_new = jnp.where(odd, vx[:, HD:], vx[:, :HD])
    ko_ref[slab] = k_new
    vo_ref[slab] = v_new
    heads = []
    for j in range(g // 2):
        x = _head_norm(q_ref[:, 2 * HD * j:2 * HD * (j + 1)].astype(F32), qg_ref[...], bd) * scale
        heads += [x[:, :HD].astype(BF16), x[:, HD:].astype(BF16)]
    o = _softmax_pv(jnp.concatenate(heads, axis=0), k_new.astype(BF16), _with_ones(v_new.astype(BF16)))
    o_ref[...] = jnp.concatenate([o[h * SEQ:(h + 1) * SEQ] for h in range(g)], axis=-1).astype(BF16)


def _attn_ctx(proj, qg128, kg128, bd, l, prev_caches=()):
    g = N_HEADS // N_KV
    qcol = 2 * D // (g * HD)
    kcol = 3 * D // (2 * HD)
    vcol = kcol + N_KV // 2
    cache_spec = pl.BlockSpec((None, l + 1, None, SEQ, HD), lambda b, h: (b, 0, h, 0, 0))
    prev_specs = [pl.BlockSpec((None, l, None, SEQ, HD), lambda b, h: (b, 0, h, 0, 0))] * 2 if l else []
    cache_shape = jax.ShapeDtypeStruct((BATCH, l + 1, N_KV, SEQ, HD), F32)
    return pl.pallas_call(
        functools.partial(_attn_ctx_kernel, slab=l),
        grid=(BATCH, N_KV),
        in_specs=[
            pl.BlockSpec((SEQ, g * HD), lambda b, h: (b, qcol + h)),
            pl.BlockSpec((SEQ, 2 * HD), lambda b, h: (b, kcol + h // 2)),
            pl.BlockSpec((SEQ, 2 * HD), lambda b, h: (b, vcol + h // 2)),
            pl.BlockSpec((None, 1, 2 * HD), lambda b, h: (l, 0, 0)),
            pl.BlockSpec((None, 1, 2 * HD), lambda b, h: (l, 0, 0)),
            pl.BlockSpec((2 * HD, 2 * HD), lambda b, h: (0, 0)),
        ] + prev_specs,
        out_specs=[pl.BlockSpec((SEQ, g * HD), lambda b, h: (b, h)), cache_spec, cache_spec],
        out_shape=[jax.ShapeDtypeStruct((T_CTX, D), BF16), cache_shape, cache_shape],
        compiler_params=_cp(("arbitrary", "arbitrary")),
        name="attn_ctx",
    )(proj, proj, proj, qg128, kg128, bd, *prev_caches)


def _attn_lat_kernel(q_ref, k_ref, v_ref, qg_ref, kg_ref, cos_ref, sin_ref, bd_ref, pk_ref, pv_ref,
                     o_ref, q_s, k_s, v_s):
    g = N_HEADS // N_KV
    bd, cos, sin = bd_ref[...], cos_ref[...], sin_ref[...]
    odd = lax.rem(pl.program_id(1), 2) == 1
    scale = HD ** -0.5 * float(np.log2(np.e))

    kx = _rope(_head_norm(k_ref[...].astype(F32), kg_ref[...], bd), cos, sin)
    vx = v_ref[...]
    k_s[0:PAST, :] = pk_ref[...].astype(BF16)
    k_s[PAST:, :] = jnp.where(odd, kx[:, HD:], kx[:, :HD]).astype(BF16)
    v_s[0:PAST, :] = _with_ones(pv_ref[...].astype(BF16))
    v_s[PAST:, :] = _with_ones(jnp.where(odd, vx[:, HD:], vx[:, :HD]))
    for j in range(g // 2):
        x = _head_norm(q_ref[:, 2 * HD * j:2 * HD * (j + 1)].astype(F32), qg_ref[...], bd)
        x = _rope(x, cos, sin) * scale
        q_s[2 * j] = x[:, :HD].astype(BF16)
        q_s[2 * j + 1] = x[:, HD:].astype(BF16)

    o_ref[...] = _attend_heads(q_s, k_s[...], v_s[...]).astype(BF16)


def _attn_lat(proj, cache_k, cache_v, qg128, kg128, cos128, sin128, bd, l):
    g = N_HEADS // N_KV
    row0 = T_CTX // DEC_SEQ
    qcol = 2 * D // (g * HD)
    kcol = 3 * D // (2 * HD)
    vcol = kcol + N_KV // 2
    return pl.pallas_call(
        _attn_lat_kernel,
        grid=(DEC_BATCH, N_KV),
        in_specs=[
            pl.BlockSpec((DEC_SEQ, g * HD), lambda b, h: (row0 + b, qcol + h)),
            pl.BlockSpec((DEC_SEQ, 2 * HD), lambda b, h: (row0 + b, kcol + h // 2)),
            pl.BlockSpec((DEC_SEQ, 2 * HD), lambda b, h: (row0 + b, vcol + h // 2)),
            pl.BlockSpec((None, 1, 2 * HD), lambda b, h: (l, 0, 0)),
            pl.BlockSpec((None, 1, 2 * HD), lambda b, h: (l, 0, 0)),
            pl.BlockSpec((DEC_SEQ, 2 * HD), lambda b, h: (0, 0)),
            pl.BlockSpec((DEC_SEQ, 2 * HD), lambda b, h: (0, 0)),
            pl.BlockSpec((2 * HD, 2 * HD), lambda b, h: (0, 0)),
            pl.BlockSpec((None, None, None, PAST, HD), lambda b, h: (b, l, h, 0, 0)),
            pl.BlockSpec((None, None, None, PAST, HD), lambda b, h: (b, l, h, 0, 0)),
        ],
        out_specs=pl.BlockSpec((DEC_SEQ, g * HD), lambda b, h: (b, h)),
        out_shape=jax.ShapeDtypeStruct((T_LAT, D), BF16),
        scratch_shapes=[pltpu.VMEM((g, DEC_SEQ, HD), BF16), pltpu.VMEM((PAST + DEC_SEQ, HD), BF16),
                        pltpu.VMEM((PAST + DEC_SEQ, 2 * HD), BF16)],
        compiler_params=_cp(("arbitrary", "arbitrary")),
        name="attn_lat",
    )(proj, proj, proj, qg128, kg128, cos128, sin128, bd, cache_k, cache_v)


MERGE_TM = 512


def _route(lt, bias):
    rows = [lt[e:e + 1, :] for e in range(N_EXP)]
    m = rows[0]
    for e in range(1, N_EXP):
        m = jnp.maximum(m, rows[e])
    ex = [jnp.exp(r - m) for r in rows]
    z = ex[0]
    for e in range(1, N_EXP):
        z = z + ex[e]
    probs = [x / z for x in ex]
    sel = [probs[e] + bias[e:e + 1, :] for e in range(N_EXP)]

    def top2_sum(v):
        a, b = jnp.maximum(v[0], v[1]), jnp.minimum(v[0], v[1])
        c, d = jnp.maximum(v[2], v[3]), jnp.minimum(v[2], v[3])
        return jnp.maximum(a, c) + jnp.maximum(jnp.minimum(a, c), jnp.maximum(b, d))

    scores = [top2_sum(sel[4 * g:4 * g + 4]) for g in range(4)]
    best = jnp.zeros_like(scores[0], dtype=jnp.int32)
    best_s = scores[0]
    for g in range(1, 4):
        take = scores[g] > best_s
        best = jnp.where(take, g, best)
        best_s = jnp.where(take, scores[g], best_s)
    cs, cp = [], []
    for j in range(4):
        s_j, p_j = sel[j], probs[j]
        for g in range(1, 4):
            s_j = jnp.where(best == g, sel[4 * g + j], s_j)
            p_j = jnp.where(best == g, probs[4 * g + j], p_j)
        cs.append(s_j)
        cp.append(p_j)
    neg = jnp.full_like(cs[0], -jnp.inf)

    def argmax4(v):
        bi = jnp.zeros_like(best)
        bv = v[0]
        for j in range(1, 4):
            take = v[j] > bv
            bi = jnp.where(take, j, bi)
            bv = jnp.where(take, v[j], bv)
        return bi

    def pick(v, idx):
        out = v[0]
        for j in range(1, 4):
            out = jnp.where(idx == j, v[j], out)
        return out

    i1 = argmax4(cs)
    cs2 = [jnp.where(i1 == j, neg, cs[j]) for j in range(4)]
    i2 = argmax4(cs2)
    i2 = jnp.where((i2 == 0) & (i1 == 0), 1, i2)
    w1, w2 = pick(cp, i1), pick(cp, i2)
    den = w1 + w2
    return best * 4 + i1, best * 4 + i2, w1 / den, w2 / den


def _merge_kernel(yrec_ref, oa_ref, ob_ref, gr0_ref, gr1_ref, ga0_ref, ga1_ref, xa_ref, xb_ref, mod_ref, g2_ref,
                  wrec_ref, watt_ref, wout_ref, wrt_ref, rb_ref,
                  x1_ref, h2_ref, idx_ref, wts_ref,
                  wrec_s, watt_s, wout_s):
    @pl.when(pl.program_id(0) == 0)
    def _():
        wrec_s[...] = wrec_ref[...].astype(BF16)
        watt_s[...] = watt_ref[...].astype(BF16)
        wout_s[...] = wout_ref[...].astype(BF16)

    is_ctx = pl.program_id(0) < T_CTX // MERGE_TM
    args = (yrec_ref, gr0_ref, gr1_ref, ga0_ref, ga1_ref, mod_ref, g2_ref, wrt_ref, rb_ref,
            x1_ref, h2_ref, idx_ref, wts_ref, wrec_s, watt_s, wout_s)

    @pl.when(is_ctx)
    def _():
        _merge_body(oa_ref, xa_ref, *args)

    @pl.when(jnp.logical_not(is_ctx))
    def _():
        _merge_body(ob_ref, xb_ref, *args)


def _merge_body(oatt_ref, x_ref, yrec_ref, gr0_ref, gr1_ref, ga0_ref, ga1_ref, mod_ref, g2_ref, wrt_ref, rb_ref,
                x1_ref, h2_ref, idx_ref, wts_ref, wrec_s, watt_s, wout_s):
    half = D // 2
    b_rec = _dot(yrec_ref[...], wrec_s[...])
    b_att = _dot(oatt_ref[...], watt_s[...])
    m0 = _sigmoid(gr0_ref[...].astype(F32)) * b_rec[:, :half] + _sigmoid(ga0_ref[...].astype(F32)) * b_att[:, :half]
    m1 = _sigmoid(gr1_ref[...].astype(F32)) * b_rec[:, half:] + _sigmoid(ga1_ref[...].astype(F32)) * b_att[:, half:]
    merged = jnp.concatenate([m0, m1], axis=-1).astype(BF16)
    out = _dot(merged, wout_s[...])

    hs = []
    for s in range(MERGE_TM // SEG):
        rows = slice(s * SEG, (s + 1) * SEG)
        m = mod_ref[s]
        x1 = x_ref[rows, :] + m[2:3, :] * out[rows, :]
        x1_ref[rows, :] = x1
        h2 = _norm_mod(x1, g2_ref[...], m[3:4, :], m[4:5, :])
        hs.append(h2)
    h2 = jnp.concatenate(hs, axis=0)
    _store_row_tiles(h2_ref, h2)

    h_hi, h_lo = _split(h2)
    w_hi, w_lo = _split(wrt_ref[...])
    lt = _dot_nt(w_hi, h_hi) + _dot_nt(w_hi, h_lo) + _dot_nt(w_lo, h_hi)
    e1, e2, w1, w2 = _route(lt, rb_ref[...])
    idx_ref[...] = jnp.concatenate([e1, e2], axis=0)
    wts_ref[...] = jnp.concatenate([w1, w2], axis=0)


def _merge(yrec, o_ctx, o_lat, proj, xa, xb, modseg, norm2_g, w_rec_out, w_att_out, w_out, wrt, rbias, l):
    tm = MERGE_TM
    half = D // 2
    gcol = (3 * D + 2 * N_KV * HD) // half
    wspec = pl.BlockSpec((None, D, D), lambda i: (l, 0, 0))
    return pl.pallas_call(
        _merge_kernel,
        grid=(T // tm,),
        in_specs=[pl.BlockSpec((tm, D), lambda i: (i, 0))] + _two_part_specs(tm, T_CTX // tm) + [
            pl.BlockSpec((tm, half), lambda i: (i, gcol)),
            pl.BlockSpec((tm, half), lambda i: (i, gcol + 1)),
            pl.BlockSpec((tm, half), lambda i: (i, gcol + 2)),
            pl.BlockSpec((tm, half), lambda i: (i, gcol + 3)),
        ] + _two_part_specs(tm, T_CTX // tm) + [
            pl.BlockSpec((None, tm // SEG, 8, D), lambda i: (l, i, 0, 0)),
            pl.BlockSpec((None, 1, D), lambda i: (l, 0, 0)),
            wspec, wspec, wspec,
            pl.BlockSpec((N_EXP, D), lambda i: (0, 0)),
            pl.BlockSpec((N_EXP, 1), lambda i: (0, 0)),
        ],
        out_specs=[
            pl.BlockSpec((tm, D), lambda i: (i, 0)),
            pl.BlockSpec((tm * NCH, 128), lambda i: (i, 0)),
            pl.BlockSpec((2, tm), lambda i: (0, i)),
            pl.BlockSpec((2, tm), lambda i: (0, i)),
        ],
        out_shape=[
            jax.ShapeDtypeStruct((T, D), F32),
            jax.ShapeDtypeStruct((T * NCH, 128), F32),
            jax.ShapeDtypeStruct((2, T), jnp.int32),
            jax.ShapeDtypeStruct((2, T), F32),
        ],
        scratch_shapes=[pltpu.VMEM((D, D), BF16)] * 3,
        compiler_params=_cp(("arbitrary",)),
        name="merge",
    )(yrec, o_ctx, o_lat, proj, proj, proj, proj, xa, xb, modseg, norm2_g.reshape(DEPTH, 1, D),
      w_rec_out, w_att_out, w_out, wrt, rbias)


MOE_TM = 512
MOE_NT = 2 * T // MOE_TM + N_EXP
MOE_ROWS = MOE_NT * MOE_TM
META_TILE_E, META_CNT, META_OFF, META_END, META_NT, META_NEXT_E = 0, 1, 2, 3, 4, 5


def _pos_kernel(idx_ref, pos_ref, meta_ref):
    shift = MOE_TM.bit_length() - 1
    idx = idx_ref[...]
    eid = lax.broadcasted_iota(jnp.int32, (N_EXP, T), 0)
    m0 = eid == idx[0:1, :]
    m1 = eid == idx[1:2, :]
    member = jnp.where(m0 | m1, 1.0, 0.0)
    cnt = jnp.sum(member, axis=1, keepdims=True).astype(jnp.int32)
    ntile = jnp.right_shift(cnt + (MOE_TM - 1), shift)
    offs, acc = [], jnp.zeros((1, 1), jnp.int32)
    for e in range(N_EXP):
        offs.append(acc)
        acc = acc + ntile[e:e + 1, :]
    off_t = jnp.concatenate(offs, axis=0)
    end_t = off_t + ntile

    blk = 256
    r_i = lax.broadcasted_iota(jnp.int32, (blk, blk), 0)
    c_i = lax.broadcasted_iota(jnp.int32, (blk, blk), 1)
    upper = jnp.where(r_i <= c_i, 1.0, 0.0).astype(BF16)
    run = (off_t * MOE_TM).astype(F32)
    for j in range(T // blk):
        ls = slice(j * blk, (j + 1) * blk)
        mb = member[:, ls]
        inc = _dot(mb.astype(BF16), upper)
        dest = run + inc - mb
        pos_ref[0:1, ls] = jnp.sum(jnp.where(m0[:, ls], dest, 0.0), axis=0, keepdims=True).astype(jnp.int32)
        pos_ref[1:2, ls] = jnp.sum(jnp.where(m1[:, ls], dest, 0.0), axis=0, keepdims=True).astype(jnp.int32)
        run = run + inc[:, blk - 1:blk]

    lane = lax.broadcasted_iota(jnp.int32, (1, 128), 1)
    zero = jnp.zeros((1, 128), jnp.int32)
    tile_e, cnt_row, off_row, end_row = zero, zero, zero, zero
    for e in range(N_EXP):
        tile_e = tile_e + jnp.where(lane >= end_t[e:e + 1, :], 1, 0)
        here = lane == e
        cnt_row = jnp.where(here, cnt[e:e + 1, :], cnt_row)
        off_row = jnp.where(here, off_t[e:e + 1, :] * MOE_TM, off_row)
        end_row = jnp.where(here, end_t[e:e + 1, :] * MOE_TM, end_row)
    tile_e = jnp.minimum(tile_e, N_EXP - 1)
    nt_row = zero + acc
    next_row = zero
    nxt = jnp.full((1, 1), -1, jnp.int32)
    for e in reversed(range(N_EXP)):
        next_row = jnp.where(lane == e, nxt, next_row)
        nxt = jnp.where(cnt[e:e + 1, :] > 0, e, nxt)
    meta_ref[...] = jnp.concatenate([tile_e, cnt_row, off_row, end_row, nt_row, next_row, zero, zero], axis=0)


def _route_pos(idx):
    return pl.pallas_call(
        _pos_kernel,
        grid=(1,),
        in_specs=[pl.BlockSpec((2, T), lambda i: (0, 0))],
        out_specs=[pl.BlockSpec((2, T), lambda i: (0, 0)), pl.BlockSpec((8, 128), lambda i: (0, 0))],
        out_shape=[jax.ShapeDtypeStruct((2, T), jnp.int32), jax.ShapeDtypeStruct((8, 128), jnp.int32)],
        compiler_params=_cp(("arbitrary",)),
        name="route_pos",
    )(idx)


DISP_TM = 1024
DISP_EXPERTS = N_EXP // (T // DISP_TM)


def _dispatch_kernel(meta_ref, pos_ref, h_ref, z_hbm, xs_hbm, sem):
    i = pl.program_id(0)

    def row_copy(src, src_row, dst_row):
        return pltpu.make_async_copy(src.at[pl.ds(src_row * NCH, NCH), :],
                                     xs_hbm.at[pl.ds(pl.multiple_of(dst_row * NCH, NCH), NCH), :], sem)

    for r in range(DISP_TM):
        row_copy(h_ref, r, pos_ref[0, r]).start(priority=0)
        row_copy(h_ref, r, pos_ref[1, r]).start(priority=1)

    def zero_copies(act):
        for q in range(DISP_EXPERTS):
            e = i * DISP_EXPERTS + q
            s = meta_ref[META_OFF, e] + meta_ref[META_CNT, e]
            npad = meta_ref[META_END, e] - s
            for bit in reversed(range(MOE_TM.bit_length() - 1)):
                size = 1 << bit
                part = jnp.bitwise_and(npad, size)

                @pl.when(part != 0)
                def _():
                    dst = pl.ds(pl.multiple_of(s * NCH, NCH), size * NCH)
                    act(pltpu.make_async_copy(z_hbm.at[pl.ds(0, size * NCH), :], xs_hbm.at[dst, :], sem))

                s = s + part
            tail = meta_ref[META_NT, 0] + e

            @pl.when(tail < MOE_NT)
            def _():
                rows = pl.ds(pl.multiple_of(tail * (MOE_TM * NCH), MOE_TM * NCH), MOE_TM * NCH)
                act(pltpu.make_async_copy(z_hbm, xs_hbm.at[rows, :], sem))

    zero_copies(lambda c: c.start())
    for _ in range(2):
        pltpu.make_async_copy(h_ref, xs_hbm.at[pl.ds(0, DISP_TM * NCH), :], sem).wait()
    zero_copies(lambda c: c.wait())


def _dispatch(meta, pos, h2, zrow):
    return pl.pallas_call(
        _dispatch_kernel,
        grid_spec=pltpu.PrefetchScalarGridSpec(
            num_scalar_prefetch=1,
            grid=(T // DISP_TM,),
            in_specs=[
                pl.BlockSpec((2, DISP_TM), lambda i, meta: (0, i), memory_space=pltpu.SMEM),
                pl.BlockSpec((DISP_TM * NCH, 128), lambda i, meta: (i, 0)),
                pl.BlockSpec((MOE_TM * NCH, 128), lambda i, meta: (0, 0)),
            ],
            out_specs=pl.BlockSpec(memory_space=pl.ANY),
            scratch_shapes=[pltpu.SemaphoreType.DMA],
        ),
        out_shape=jax.ShapeDtypeStruct((MOE_ROWS * NCH, 128), F32),
        compiler_params=_cp(("arbitrary",)),
        name="dispatch",
    )(meta, pos, h2, zrow)


XS_SLOTS = 3


def _experts_kernel(meta_ref, xs_hbm, wg_hbm, wu_hbm, wd_hbm, ys_ref,
                    wg_f, wu_f, wd_f, wg_s, wu_s, wd_s, sem, xring, x_s, xsem, *, l):
    j = pl.program_id(0)
    live = j < meta_ref[META_NT, 0]
    e = meta_ref[META_TILE_E, j]
    e_prev = meta_ref[META_TILE_E, jnp.maximum(j - 1, 0)]

    def fetch(ex):
        return (pltpu.make_async_copy(wg_hbm.at[l, ex], wg_f, sem.at[0]),
                pltpu.make_async_copy(wu_hbm.at[l, ex], wu_f, sem.at[1]),
                pltpu.make_async_copy(wd_hbm.at[l, ex], wd_f, sem.at[2]))

    @pl.when(j == 0)
    def _():
        for c in fetch(e):
            c.start()

    @pl.when(live & ((j == 0) | (e != e_prev)))
    def _():
        for c, dst, src in zip(fetch(e), (wg_s, wu_s, wd_s), (wg_f, wu_f, wd_f)):
            c.wait()
            dst[...] = src[...].astype(BF16)
        nxt = meta_ref[META_NEXT_E, e]

        @pl.when(nxt >= 0)
        def _():
            for c in fetch(nxt):
                c.start()

    nt = meta_ref[META_NT, 0]

    def xs_copy(t, s):
        rows = pl.ds(pl.multiple_of(t * (MOE_TM * NCH), MOE_TM * NCH), MOE_TM * NCH)
        return pltpu.make_async_copy(xs_hbm.at[rows, :], xring.at[s], xsem.at[s])

    for s in range(XS_SLOTS):
        if s < XS_SLOTS - 1:
            @pl.when((j == 0) & (s < nt))
            def _():
                xs_copy(s, s).start()

        @pl.when((j + XS_SLOTS - 1 < nt) & (lax.rem(j + XS_SLOTS - 1, XS_SLOTS) == s))
        def _():
            xs_copy(j + XS_SLOTS - 1, s).start()

        @pl.when(live & (lax.rem(j, XS_SLOTS) == s))
        def _():
            xs_copy(j, s).wait()
            x_s[...] = _load_row_tiles(xring.at[s], MOE_TM).astype(BF16)

    @pl.when(live)
    def _():
        x = x_s[...]
        g = _dot(x, wg_s[...])
        u = _dot(x, wu_s[...])
        act = (g * _sigmoid(g)) * u
        _store_row_tiles(ys_ref, _dot(act.astype(BF16), wd_s[...]))

    @pl.when(jnp.logical_not(live))
    def _():
        ys_ref[...] = jnp.zeros_like(ys_ref)


def _experts(meta, xs, w_gate_e, w_up_e, w_down_e, l):
    return pl.pallas_call(
        functools.partial(_experts_kernel, l=l),
        grid_spec=pltpu.PrefetchScalarGridSpec(
            num_scalar_prefetch=1,
            grid=(MOE_NT,),
            in_specs=[
                pl.BlockSpec(memory_space=pl.ANY),
                pl.BlockSpec(memory_space=pl.ANY),
                pl.BlockSpec(memory_space=pl.ANY),
                pl.BlockSpec(memory_space=pl.ANY),
            ],
            out_specs=pl.BlockSpec((MOE_TM * NCH, 128), lambda j, meta: (j, 0)),
            scratch_shapes=[pltpu.VMEM((D, D_EXP), F32), pltpu.VMEM((D, D_EXP), F32), pltpu.VMEM((D_EXP, D), F32),
                            pltpu.VMEM((D, D_EXP), BF16), pltpu.VMEM((D, D_EXP), BF16), pltpu.VMEM((D_EXP, D), BF16),
                            pltpu.SemaphoreType.DMA((3,)),
                            pltpu.VMEM((XS_SLOTS, MOE_TM * NCH, 128), F32), pltpu.VMEM((MOE_TM, D), BF16),
                            pltpu.SemaphoreType.DMA((XS_SLOTS,))],
        ),
        out_shape=jax.ShapeDtypeStruct((MOE_ROWS * NCH, 128), F32),
        compiler_params=_cp(("arbitrary",)),
        name="experts",
    )(meta, xs, w_gate_e, w_up_e, w_down_e)


COMB_TM = SEG


def _combine_kernel(pos_ref, w_ref, x1_ref, mod_ref, fg_ref, ys_hbm, oa_ref, ob_ref, buf, y_s, sem, *, final):
    i = pl.program_id(0)
    n = pl.num_programs(0) - 1
    n_ctx = T_CTX // COMB_TM

    for s in range(2):
        @pl.when((i < n) & (lax.rem(i, 2) == s))
        def _():
            for r in range(COMB_TM):
                for k in range(2):
                    src = pl.ds(pl.multiple_of(pos_ref[k, r] * NCH, NCH), NCH)
                    pltpu.make_async_copy(ys_hbm.at[src, :], buf.at[s, k, pl.ds(r * NCH, NCH), :],
                                          sem.at[s]).start(priority=k)

    for slot in range(2):
        @pl.when((i > 0) & (lax.rem(i - 1, 2) == slot))
        def _():
            for k in range(2):
                pltpu.make_async_copy(ys_hbm.at[pl.ds(0, COMB_TM * NCH), :], buf.at[slot, k], sem.at[slot]).wait()
            w = w_ref[...]
            y = (w[:, 0:1] * _load_row_tiles(buf.at[slot, 0], COMB_TM)
                 + w[:, 1:2] * _load_row_tiles(buf.at[slot, 1], COMB_TM))
            y_s[...] = y

    @pl.when(i > 0)
    def _():
        x = x1_ref[...] + mod_ref[5:6, :] * y_s[...]
        if final:
            ms = jnp.mean(x * x, axis=-1, keepdims=True)
            x = x * lax.rsqrt(ms + EPS) * fg_ref[...]

        @pl.when(i - 1 < n_ctx)
        def _():
            oa_ref[...] = x

        @pl.when(i - 1 >= n_ctx)
        def _():
            ob_ref[...] = x


def _combine(pos, wts_t, x1, modseg, final_g, ys, l, final):
    n = T // COMB_TM
    n_ctx = T_CTX // COMB_TM

    def done(i):
        return jnp.maximum(i - 1, 0)

    return pl.pallas_call(
        functools.partial(_combine_kernel, final=final),
        grid=(n + 1,),
        in_specs=[
            pl.BlockSpec((2, COMB_TM), lambda i: (0, jnp.minimum(i, n - 1)), memory_space=pltpu.SMEM),
            pl.BlockSpec((COMB_TM, 2), lambda i: (done(i), 0)),
            pl.BlockSpec((COMB_TM, D), lambda i: (done(i), 0)),
            pl.BlockSpec((None, None, 8, D), lambda i: (l, done(i), 0, 0)),
            pl.BlockSpec((1, D), lambda i: (0, 0)),
            pl.BlockSpec(memory_space=pl.ANY),
        ],
        out_specs=[pl.BlockSpec((COMB_TM, D), lambda i: (jnp.minimum(done(i), n_ctx - 1), 0)),
                   pl.BlockSpec((COMB_TM, D), lambda i: (jnp.maximum(done(i) - n_ctx, 0), 0))],
        out_shape=[jax.ShapeDtypeStruct((T_CTX, D), F32), jax.ShapeDtypeStruct((T_LAT, D), F32)],
        scratch_shapes=[pltpu.VMEM((2, 2, COMB_TM * NCH, 128), F32), pltpu.VMEM((COMB_TM, D), F32),
                        pltpu.SemaphoreType.DMA((2,))],
        compiler_params=_cp(("arbitrary",)),
        name="combine",
    )(pos, wts_t, x1, modseg, final_g.reshape(1, D), ys)


def _rope_tables():
    n = DEC_SEQ
    pos_row = np.repeat(np.arange(n // GRID_W, dtype=np.float32), GRID_W)
    pos_col = np.tile(np.arange(GRID_W, dtype=np.float32), n // GRID_W)
    half = HD // 2
    inv_freq = jnp.asarray(ROPE_THETA, F32) ** (-jnp.arange(0, half, 2, dtype=F32) / half)
    ang = jnp.concatenate([jnp.asarray(pos_row)[:, None] * inv_freq,
                           jnp.asarray(pos_col)[:, None] * inv_freq], axis=-1)
    cos, sin = jnp.cos(ang), jnp.sin(ang)
    cos128 = jnp.tile(cos, (1, 4))
    sin128 = jnp.tile(jnp.concatenate([-sin, sin], axis=-1), (1, 2))
    return cos128, sin128


def _head_mean_matrix():
    idx = np.arange(2 * HD)
    same = (idx[:, None] // HD) == (idx[None, :] // HD)
    return jnp.asarray(same.astype(np.float32) / HD, BF16)


_SEG_ROWS = np.array([0] * (T_CTX // SEG) + [1 + b for b in range(DEC_BATCH) for _ in range(DEC_SEQ // SEG)])


def kernel(x_prompt, x_sample, cache_k, cache_v, state_rec, c, c_ctx, w_mod, b_mod, norm1_g, norm2_g, w_in, conv_w, conv_b, rg_wa, rg_ba, rg_wx, rg_bx, rg_lambda, q_norm_g, k_norm_g, w_rec_out, w_att_out, w_out, w_router, router_bias, w_gate_e, w_up_e, w_down_e, final_g):
    xa, xb = x_prompt.reshape(T_CTX, D), x_sample.reshape(T_LAT, D)

    cvecs = jnp.concatenate([c_ctx[None, :], c, jnp.zeros((3, D), F32)], axis=0)
    mods = _mods(cvecs, w_mod, b_mod).reshape(DEPTH, 8, 6, D)
    modseg = jnp.pad(mods[:, _SEG_ROWS], ((0, 0), (0, 0), (0, 2), (0, 0)))

    cos128, sin128 = _rope_tables()
    bd = _head_mean_matrix()
    qg128 = jnp.tile(q_norm_g, (1, 2)).reshape(DEPTH, 1, 2 * HD)
    kg128 = jnp.tile(k_norm_g, (1, 2)).reshape(DEPTH, 1, 2 * HD)
    wg = jnp.concatenate([rg_wa[:, 0], rg_wx[:, 0], rg_wa[:, 1], rg_wx[:, 1]], axis=-1)
    pvec = jnp.stack([rg_ba[:, 0], rg_bx[:, 0], rg_ba[:, 1], rg_bx[:, 1],
                      rg_lambda[:, 0], rg_lambda[:, 1], conv_b, jnp.zeros_like(conv_b)], axis=1)
    wrt = w_router.T
    rbias = router_bias.reshape(N_EXP, 1)
    zrow = jnp.zeros((MOE_TM * NCH, 128), F32)

    caches, new_s = (), []
    for l in range(DEPTH):
        proj = _inproj(xa, xb, modseg, norm1_g, w_in, l)
        h0 = jnp.concatenate([jnp.zeros((T_CTX // UNIT, 2, D), F32), state_rec[:, l]], axis=0)
        yrec, stf, stb = _rec(proj, conv_w, pvec, wg, h0, l)
        o_ctx, kc, vc = _attn_ctx(proj, qg128, kg128, bd, l, prev_caches=caches)
        caches = (kc, vc)
        o_lat = _attn_lat(proj, cache_k, cache_v, qg128, kg128, cos128, sin128, bd, l)
        x1, h2, idx, wts = _merge(yrec, o_ctx, o_lat, proj, xa, xb, modseg, norm2_g,
                                  w_rec_out, w_att_out, w_out, wrt, rbias, l)
        pos, meta = _route_pos(idx)
        xs = _dispatch(meta, pos, h2, zrow)
        ys = _experts(meta, xs, w_gate_e, w_up_e, w_down_e, l)
        xa, xb = _combine(pos, wts.T, x1, modseg, final_g, ys, l, final=(l == DEPTH - 1))
        n_cu = T_CTX // UNIT
        spu = UNIT // SEQ
        hf_last = stf[:n_cu].reshape(n_cu, spu, 2, D)[:, :, 1].reshape(BATCH, D)
        hb_first = stb[:n_cu].reshape(n_cu, spu, 2, D)[:, :, 0].reshape(BATCH, D)
        new_s.append(jnp.stack([hf_last, hb_first], axis=1))

    y_prompt = xa.reshape(BATCH, SEQ, D)
    y_sample = xb.reshape(DEC_BATCH, DEC_SEQ, D)
    return (y_prompt, y_sample, caches[0], caches[1], jnp.stack(new_s, axis=1))
```

```python
import functools

import numpy as np
import jax
import jax.numpy as jnp
from jax import lax
from jax.experimental import pallas as pl
from jax.experimental.pallas import tpu as pltpu

F32 = jnp.float32
BF16 = jnp.bfloat16

D = 1024
BATCH = 16
SEQ = 256
DEPTH = 2
DEC_BATCH = 4
DEC_SEQ = 1024
PAST = 256
GRID_W = 64
N_HEADS = 16
N_KV = 4
HD = 64
RG_BLK = 128
RG_C = 8.0
N_EXP = 16
D_EXP = 512
ROPE_THETA = 10000.0
EPS = 1e-6
P_IN = 5632
TINY = float(np.finfo(np.float32).tiny)
NEG_LOG2E = -float(np.log2(np.e))

T_CTX = BATCH * SEQ
T_LAT = DEC_BATCH * DEC_SEQ
T = T_CTX + T_LAT
SEG = 256
UNIT = 1024
N_UNIT = T // UNIT
LANES = 128
SUBLANES = 8
CHUNK = UNIT // SUBLANES
CSTRIDE = CHUNK + SUBLANES

VMEM_LIMIT = 56 * 1024 * 1024


def _cp(sem):
    return pltpu.CompilerParams(dimension_semantics=sem, vmem_limit_bytes=VMEM_LIMIT)


def _split(x):
    hi = x.astype(BF16)
    lo = (x - hi.astype(F32)).astype(BF16)
    return hi, lo


def _sigmoid(x):
    return 0.5 * jnp.tanh(0.5 * x) + 0.5


NCH = D // LANES


def _store_row_tiles(ref, x):
    n = x.shape[0]
    for c in range(NCH):
        ref[pl.ds(c, n, stride=NCH), :] = x[:, c * 128:(c + 1) * 128]


def _load_row_tiles(ref, n):
    return jnp.concatenate([ref[pl.ds(c, n, stride=NCH), :] for c in range(NCH)], axis=-1)


def _dot(a, b):
    return jnp.dot(a, b, preferred_element_type=F32)


def _dot_nt(a, b):
    return lax.dot_general(a, b, (((1,), (1,)), ((), ())), preferred_element_type=F32)


def _mods_kernel(c_ref, w_ref, b_ref, o_ref):
    c = c_ref[...]
    s = c * jax.nn.sigmoid(c)
    s_hi, s_lo = _split(s)
    w_hi, w_lo = _split(w_ref[...])
    o_ref[...] = _dot(s_hi, w_hi) + _dot(s_hi, w_lo) + _dot(s_lo, w_hi) + b_ref[...]


def _mods(cvecs, w_mod, b_mod):
    tn = 1536
    return pl.pallas_call(
        _mods_kernel,
        grid=(DEPTH, 6 * D // tn),
        in_specs=[
            pl.BlockSpec((8, D), lambda l, j: (0, 0)),
            pl.BlockSpec((None, D, tn), lambda l, j: (l, 0, j)),
            pl.BlockSpec((None, 1, tn), lambda l, j: (l, 0, j)),
        ],
        out_specs=pl.BlockSpec((None, 8, tn), lambda l, j: (l, 0, j)),
        out_shape=jax.ShapeDtypeStruct((DEPTH, 8, 6 * D), F32),
        compiler_params=_cp(("arbitrary", "arbitrary")),
        name="mods",
    )(cvecs, w_mod, b_mod.reshape(DEPTH, 1, 6 * D))


def _norm_mod(x, g, shift, scale):
    ms = jnp.mean(x * x, axis=-1, keepdims=True)
    return x * lax.rsqrt(ms + EPS) * g * (1.0 + scale) + shift


def _two_part_specs(tm, n_ctx):
    return [pl.BlockSpec((tm, D), lambda i, *_: (jnp.minimum(i, n_ctx - 1), 0)),
            pl.BlockSpec((tm, D), lambda i, *_: (jnp.maximum(i - n_ctx, 0), 0))]


def _inproj_kernel(xa_ref, xb_ref, mod_ref, g_ref, w_ref, o_ref, h_ref, *, tm):
    def prologue(x_ref):
        def seg(s, carry):
            r0 = pl.multiple_of(s * SEG, SEG)
            m = mod_ref[s]
            h = _norm_mod(x_ref[pl.ds(r0, SEG), :], g_ref[...], m[0:1, :], m[1:2, :])
            h_ref[pl.ds(r0, SEG), :] = h.astype(BF16)
            return carry
        lax.fori_loop(0, tm // SEG, seg, 0)

    first = pl.program_id(1) == 0
    is_ctx = pl.program_id(0) < T_CTX // tm

    @pl.when(first & is_ctx)
    def _():
        prologue(xa_ref)

    @pl.when(first & jnp.logical_not(is_ctx))
    def _():
        prologue(xb_ref)

    o_ref[...] = _dot(h_ref[...], w_ref[...].astype(BF16)).astype(BF16)


def _inproj(xa, xb, modseg, norm_g, w_in, l):
    tm, tn = 2048, 512
    return pl.pallas_call(
        functools.partial(_inproj_kernel, tm=tm),
        grid=(T // tm, P_IN // tn),
        in_specs=_two_part_specs(tm, T_CTX // tm) + [
            pl.BlockSpec((None, tm // SEG, 8, D), lambda i, j: (l, i, 0, 0)),
            pl.BlockSpec((None, 1, D), lambda i, j: (l, 0, 0)),
            pl.BlockSpec((None, D, tn), lambda i, j: (l, 0, j)),
        ],
        out_specs=pl.BlockSpec((tm, tn), lambda i, j: (i, j)),
        out_shape=jax.ShapeDtypeStruct((T, P_IN), BF16),
        scratch_shapes=[pltpu.VMEM((tm, D), BF16)],
        compiler_params=_cp(("arbitrary", "arbitrary")),
        name="inproj",
    )(xa, xb, modseg, norm_g.reshape(DEPTH, 1, D), w_in)


REC_CW = 512
PAD_F = 16
PAD_B = 8
GATE_ROWS = 512


def _rec_kernel(xr_ref, gate_ref, cw_ref, pv_ref, wg_ref, h0_ref,
                y_ref, stf_ref, stb_ref,
                xs_ref, af_ref, bf_ref, ab_ref, bb_ref, nat_ref, wgh_ref):
    u = pl.program_id(0)
    is_ctx = u < (T_CTX // UNIT)
    cps = jnp.where(is_ctx, SEQ // CHUNK, DEC_SEQ // CHUNK)
    nblk = REC_CW // RG_BLK

    def lanes(n):
        return slice(n * RG_BLK, (n + 1) * RG_BLK)

    def tile(r):
        return slice(8 * r, 8 * r + 8)

    for c in range(8):
        for n in range(nblk):
            nat_ref[n, c * CSTRIDE:c * CSTRIDE + CHUNK, :] = xr_ref[c * CHUNK:(c + 1) * CHUNK, lanes(n)].astype(F32)
    for r in range(CHUNK):
        for n in range(nblk):
            xs_ref[PAD_F + 8 * r:PAD_F + 8 * r + 8, lanes(n)] = nat_ref[n, pl.ds(r, 8, stride=CSTRIDE), :]
    chunk_id = lax.broadcasted_iota(jnp.int32, (8, 1), 0)
    seq_start = jnp.bitwise_and(chunk_id, cps - 1) == 0
    seq_end = jnp.bitwise_and(chunk_id, cps - 1) == cps - 1
    for j, r in ((0, CHUNK - 2), (1, CHUNK - 1)):
        prev_chunk = pltpu.roll(xs_ref[PAD_F + 8 * r:PAD_F + 8 * r + 8, :], 1, 0)
        xs_ref[tile(j), :] = jnp.where(seq_start, 0.0, prev_chunk)
    next_chunk = pltpu.roll(xs_ref[PAD_F:PAD_F + 8, :], 7, 0)
    xs_ref[PAD_F + UNIT:PAD_F + UNIT + PAD_B, :] = jnp.where(seq_end, 0.0, next_chunk)

    pv = pv_ref[...]
    cwts = cw_ref[...]
    conv_b = pv[6:7, :]

    def softplus_neg(lam):
        z = -lam
        return jnp.maximum(z, 0.0) + jnp.log1p(jnp.exp(-jnp.abs(z)))

    c4s = tuple((0.5 * RG_C) * softplus_neg(pv[4 + d:5 + d, :]) for d in range(2))
    pv_h = 0.5 * pv
    for n in range(nblk):
        wgh_ref[n] = (0.5 * wg_ref[n]).astype(BF16)
    a_refs = (af_ref, ab_ref)
    b_refs = (bf_ref, bb_ref)

    def gates(g, carry):
        base = pl.multiple_of(g * GATE_ROWS, GATE_ROWS)

        def tap(d):
            return xs_ref[pl.ds(pl.multiple_of(base + PAD_F + 8 * d, 8), GATE_ROWS), :]

        xc = conv_b + tap(-2) * cwts[0:1, :]
        xc = xc + tap(-1) * cwts[1:2, :]
        xc = xc + tap(0) * cwts[2:3, :]
        xc = xc + tap(1) * cwts[3:4, :]
        for n in range(nblk):
            ls = lanes(n)
            xn = xc[:, ls]
            hx = 0.5 * xn
            pre_h = _dot(xn.astype(BF16), wgh_ref[n])
            for d in range(2):
                th_r = jnp.tanh(pre_h[:, (2 * d) * RG_BLK:(2 * d + 1) * RG_BLK] + pv_h[2 * d:2 * d + 1, ls])
                th_i = jnp.tanh(pre_h[:, (2 * d + 1) * RG_BLK:(2 * d + 2) * RG_BLK] + pv_h[2 * d + 1:2 * d + 2, ls])
                c4 = c4s[d][:, ls]
                nla = c4 * th_r + c4
                a = jnp.exp2(nla * NEG_LOG2E)
                s = jnp.tanh(nla) * (a * a + 1.0)
                inp = (s * lax.rsqrt(jnp.maximum(s, TINY))) * (hx * th_i + hx)
                a_refs[d][pl.ds(base, GATE_ROWS), ls] = a
                b_refs[d][pl.ds(base, GATE_ROWS), ls] = inp
        return carry

    lax.fori_loop(0, UNIT // GATE_ROWS, gates, 0)

    hf = hb = jnp.zeros((8, REC_CW), F32)
    pf = pb = jnp.ones((8, REC_CW), F32)
    for r in range(CHUNK):
        rf, rb = tile(r), tile(CHUNK - 1 - r)
        a = af_ref[rf, :]
        hf = a * hf + bf_ref[rf, :]
        pf = a * pf
        bf_ref[rf, :] = hf
        af_ref[rf, :] = pf
        a = ab_ref[rb, :]
        hb = a * hb + bb_ref[rb, :]
        pb = a * pb
        bb_ref[rb, :] = hb
        ab_ref[rb, :] = pb

    h0f = h0_ref[0:1, :]
    h0b = h0_ref[1:2, :]
    cf = [h0f]
    for c in range(1, 8):
        chain = hf[c - 1:c, :] + pf[c - 1:c, :] * cf[c - 1]
        cf.append(jnp.where(jnp.bitwise_and(c, cps - 1) == 0, h0f, chain))
    cb = [None] * 8
    cb[7] = h0b
    for c in range(6, -1, -1):
        chain = hb[c + 1:c + 2, :] + pb[c + 1:c + 2, :] * cb[c + 1]
        cb[c] = jnp.where(jnp.bitwise_and(c, cps - 1) == cps - 1, h0b, chain)
    carry_f = jnp.concatenate(cf, axis=0)
    carry_b = jnp.concatenate(cb, axis=0)
    stf_ref[...] = hf + pf * carry_f
    stb_ref[...] = hb + pb * carry_b

    for r in range(CHUNK):
        h = (bf_ref[tile(r), :] + af_ref[tile(r), :] * carry_f) + (bb_ref[tile(r), :] + ab_ref[tile(r), :] * carry_b)
        for n in range(nblk):
            nat_ref[n, pl.ds(r, 8, stride=CSTRIDE), :] = h[:, lanes(n)]

    for c in range(8):
        rows = slice(c * CHUNK, (c + 1) * CHUNK)
        for n in range(nblk):
            g = gate_ref[rows, lanes(n)].astype(F32)
            h = nat_ref[n, c * CSTRIDE:c * CSTRIDE + CHUNK, :]
            y_ref[rows, lanes(n)] = (h * jax.nn.gelu(g, approximate=True)).astype(BF16)


def _rec(proj, conv_w, pvec, wg, h0, l):
    ncb = D // REC_CW
    return pl.pallas_call(
        _rec_kernel,
        grid=(N_UNIT, ncb),
        in_specs=[
            pl.BlockSpec((UNIT, REC_CW), lambda u, c: (u, c)),
            pl.BlockSpec((UNIT, REC_CW), lambda u, c: (u, ncb + c)),
            pl.BlockSpec((None, 4, REC_CW), lambda u, c: (l, 0, c)),
            pl.BlockSpec((None, 8, REC_CW), lambda u, c: (l, 0, c)),
            pl.BlockSpec((None, REC_CW // RG_BLK, RG_BLK, 4 * RG_BLK), lambda u, c: (l, c, 0, 0)),
            pl.BlockSpec((None, 2, REC_CW), lambda u, c: (u, 0, c)),
        ],
        out_specs=[
            pl.BlockSpec((UNIT, REC_CW), lambda u, c: (u, c)),
            pl.BlockSpec((None, 8, REC_CW), lambda u, c: (u, 0, c)),
            pl.BlockSpec((None, 8, REC_CW), lambda u, c: (u, 0, c)),
        ],
        out_shape=[
            jax.ShapeDtypeStruct((T, D), BF16),
            jax.ShapeDtypeStruct((N_UNIT, 8, D), F32),
            jax.ShapeDtypeStruct((N_UNIT, 8, D), F32),
        ],
        scratch_shapes=[pltpu.VMEM((PAD_F + UNIT + PAD_B, REC_CW), F32)]
        + [pltpu.VMEM((UNIT, REC_CW), F32)] * 4
        + [pltpu.VMEM((REC_CW // RG_BLK, 8 * CSTRIDE, RG_BLK), F32),
           pltpu.VMEM((REC_CW // RG_BLK, RG_BLK, 4 * RG_BLK), BF16)],
        compiler_params=_cp(("arbitrary", "arbitrary")),
        name="rec",
    )(proj, proj, conv_w, pvec, wg, h0)


def _head_norm(x, g128, bd):
    hi, lo = _split(x * x)
    ms = _dot(hi, bd) + _dot(lo, bd)
    return x * lax.rsqrt(ms + EPS) * g128


def _rope(x, cos, sin_signed):
    lane = lax.broadcasted_iota(jnp.int32, x.shape, 1)
    first_half = jnp.bitwise_and(lane, HD - 1) < HD // 2
    partner = jnp.where(first_half, pltpu.roll(x, 2 * HD - HD // 2, 1), pltpu.roll(x, HD // 2, 1))
    return x * cos + partner * sin_signed


def _with_ones(v):
    return jnp.concatenate([v, jnp.ones_like(v)], axis=-1)


def _softmax_pv(q, k, v_ext):
    s = _dot_nt(q, k)
    m = jnp.max(s, axis=-1, keepdims=True)
    p = jnp.exp2(s - m).astype(BF16)
    r = _dot(p, v_ext)
    return r[:, :HD] / r[:, HD:HD + 1]


def _attend_heads(q_ref, k, v_ext):
    return jnp.concatenate([_softmax_pv(q_ref[h], k, v_ext) for h in range(N_HEADS // N_KV)], axis=-1)


def _attn_ctx_kernel(q_ref, k_ref, v_ref, qg_ref, kg_ref, bd_ref, *rest, slab):
    o_ref, ko_ref, vo_ref = rest[-3:]
    for p in range(slab):
        ko_ref[p] = rest[0][p]
        vo_ref[p] = rest[1][p]
    g = N_HEADS // N_KV
    bd = bd_ref[...]
    odd = lax.rem(pl.program_id(1), 2) == 1
    scale = HD ** -0.5 * float(np.log2(np.e))

    kx = _head_norm(k_ref[...].astype(F32), kg_ref[...], bd)
    vx = v_ref[...].astype(F32)
    k_new = jnp.where(odd, kx[:, HD:], kx[:, :HD])
    v_new = jnp.where(odd, vx[:, HD:], vx[:, :HD])
    ko_ref[slab] = k_new
    vo_ref[slab] = v_new
    heads = []
    for j in range(g // 2):
        x = _head_norm(q_ref[:, 2 * HD * j:2 * HD * (j + 1)].astype(F32), qg_ref[...], bd) * scale
        heads += [x[:, :HD].astype(BF16), x[:, HD:].astype(BF16)]
    o = _softmax_pv(jnp.concatenate(heads, axis=0), k_new.astype(BF16), _with_ones(v_new.astype(BF16)))
    o_ref[...] = jnp.concatenate([o[h * SEQ:(h + 1) * SEQ] for h in range(g)], axis=-1).astype(BF16)


def _attn_ctx(proj, qg128, kg128, bd, l, prev_caches=()):
    g = N_HEADS // N_KV
    qcol = 2 * D // (g * HD)
    kcol = 3 * D // (2 * HD)
    vcol = kcol + N_KV // 2
    cache_spec = pl.BlockSpec((None, l + 1, None, SEQ, HD), lambda b, h: (b, 0, h, 0, 0))
    prev_specs = [pl.BlockSpec((None, l, None, SEQ, HD), lambda b, h: (b, 0, h, 0, 0))] * 2 if l else []
    cache_shape = jax.ShapeDtypeStruct((BATCH, l + 1, N_KV, SEQ, HD), F32)
    return pl.pallas_call(
        functools.partial(_attn_ctx_kernel, slab=l),
        grid=(BATCH, N_KV),
        in_specs=[
            pl.BlockSpec((SEQ, g * HD), lambda b, h: (b, qcol + h)),
            pl.BlockSpec((SEQ, 2 * HD), lambda b, h: (b, kcol + h // 2)),
            pl.BlockSpec((SEQ, 2 * HD), lambda b, h: (b, vcol + h // 2)),
            pl.BlockSpec((None, 1, 2 * HD), lambda b, h: (l, 0, 0)),
            pl.BlockSpec((None, 1, 2 * HD), lambda b, h: (l, 0, 0)),
            pl.BlockSpec((2 * HD, 2 * HD), lambda b, h: (0, 0)),
        ] + prev_specs,
        out_specs=[pl.BlockSpec((SEQ, g * HD), lambda b, h: (b, h)), cache_spec, cache_spec],
        out_shape=[jax.ShapeDtypeStruct((T_CTX, D), BF16), cache_shape, cache_shape],
        compiler_params=_cp(("arbitrary", "arbitrary")),
        name="attn_ctx",
    )(proj, proj, proj, qg128, kg128, bd, *prev_caches)


def _attn_lat_kernel(q_ref, k_ref, v_ref, qg_ref, kg_ref, cos_ref, sin_ref, bd_ref, pk_ref, pv_ref,
                     o_ref, q_s, k_s, v_s):
    g = N_HEADS // N_KV
    bd, cos, sin = bd_ref[...], cos_ref[...], sin_ref[...]
    odd = lax.rem(pl.program_id(1), 2) == 1
    scale = HD ** -0.5 * float(np.log2(np.e))

    kx = _rope(_head_norm(k_ref[...].astype(F32), kg_ref[...], bd), cos, sin)
    vx = v_ref[...]
    k_s[0:PAST, :] = pk_ref[...].astype(BF16)
    k_s[PAST:, :] = jnp.where(odd, kx[:, HD:], kx[:, :HD]).astype(BF16)
    v_s[0:PAST, :] = _with_ones(pv_ref[...].astype(BF16))
    v_s[PAST:, :] = _with_ones(jnp.where(odd, vx[:, HD:], vx[:, :HD]))
    for j in range(g // 2):
        x = _head_norm(q_ref[:, 2 * HD * j:2 * HD * (j + 1)].astype(F32), qg_ref[...], bd)
        x = _rope(x, cos, sin) * scale
        q_s[2 * j] = x[:, :HD].astype(BF16)
        q_s[2 * j + 1] = x[:, HD:].astype(BF16)

    o_ref[...] = _attend_heads(q_s, k_s[...], v_s[...]).astype(BF16)


def _attn_lat(proj, cache_k, cache_v, qg128, kg128, cos128, sin128, bd, l):
    g = N_HEADS // N_KV
    row0 = T_CTX // DEC_SEQ
    qcol = 2 * D // (g * HD)
    kcol = 3 * D // (2 * HD)
    vcol = kcol + N_KV // 2
    return pl.pallas_call(
        _attn_lat_kernel,
        grid=(DEC_BATCH, N_KV),
        in_specs=[
            pl.BlockSpec((DEC_SEQ, g * HD), lambda b, h: (row0 + b, qcol + h)),
            pl.BlockSpec((DEC_SEQ, 2 * HD), lambda b, h: (row0 + b, kcol + h // 2)),
            pl.BlockSpec((DEC_SEQ, 2 * HD), lambda b, h: (row0 + b, vcol + h // 2)),
            pl.BlockSpec((None, 1, 2 * HD), lambda b, h: (l, 0, 0)),
            pl.BlockSpec((None, 1, 2 * HD), lambda b, h: (l, 0, 0)),
            pl.BlockSpec((DEC_SEQ, 2 * HD), lambda b, h: (0, 0)),
            pl.BlockSpec((DEC_SEQ, 2 * HD), lambda b, h: (0, 0)),
            pl.BlockSpec((2 * HD, 2 * HD), lambda b, h: (0, 0)),
            pl.BlockSpec((None, None, None, PAST, HD), lambda b, h: (b, l, h, 0, 0)),
            pl.BlockSpec((None, None, None, PAST, HD), lambda b, h: (b, l, h, 0, 0)),
        ],
        out_specs=pl.BlockSpec((DEC_SEQ, g * HD), lambda b, h: (b, h)),
        out_shape=jax.ShapeDtypeStruct((T_LAT, D), BF16),
        scratch_shapes=[pltpu.VMEM((g, DEC_SEQ, HD), BF16), pltpu.VMEM((PAST + DEC_SEQ, HD), BF16),
                        pltpu.VMEM((PAST + DEC_SEQ, 2 * HD), BF16)],
        compiler_params=_cp(("arbitrary", "arbitrary")),
        name="attn_lat",
    )(proj, proj, proj, qg128, kg128, cos128, sin128, bd, cache_k, cache_v)


MERGE_TM = 512


def _route(lt, bias):
    rows = [lt[e:e + 1, :] for e in range(N_EXP)]
    m = rows[0]
    for e in range(1, N_EXP):
        m = jnp.maximum(m, rows[e])
    ex = [jnp.exp(r - m) for r in rows]
    z = ex[0]
    for e in range(1, N_EXP):
        z = z + ex[e]
    probs = [x / z for x in ex]
    sel = [probs[e] + bias[e:e + 1, :] for e in range(N_EXP)]

    def top2_sum(v):
        a, b = jnp.maximum(v[0], v[1]), jnp.minimum(v[0], v[1])
        c, d = jnp.maximum(v[2], v[3]), jnp.minimum(v[2], v[3])
        return jnp.maximum(a, c) + jnp.maximum(jnp.minimum(a, c), jnp.maximum(b, d))

    scores = [top2_sum(sel[4 * g:4 * g + 4]) for g in range(4)]
    best = jnp.zeros_like(scores[0], dtype=jnp.int32)
    best_s = scores[0]
    for g in range(1, 4):
        take = scores[g] > best_s
        best = jnp.where(take, g, best)
        best_s = jnp.where(take, scores[g], best_s)
    cs, cp = [], []
    for j in range(4):
        s_j, p_j = sel[j], probs[j]
        for g in range(1, 4):
            s_j = jnp.where(best == g, sel[4 * g + j], s_j)
            p_j = jnp.where(best == g, probs[4 * g + j], p_j)
        cs.append(s_j)
        cp.append(p_j)
    neg = jnp.full_like(cs[0], -jnp.inf)

    def argmax4(v):
        bi = jnp.zeros_like(best)
        bv = v[0]
        for j in range(1, 4):
            take = v[j] > bv
            bi = jnp.where(take, j, bi)
            bv = jnp.where(take, v[j], bv)
        return bi

    def pick(v, idx):
        out = v[0]
        for j in range(1, 4):
            out = jnp.where(idx == j, v[j], out)
        return out

    i1 = argmax4(cs)
    cs2 = [jnp.where(i1 == j, neg, cs[j]) for j in range(4)]
    i2 = argmax4(cs2)
    i2 = jnp.where((i2 == 0) & (i1 == 0), 1, i2)
    w1, w2 = pick(cp, i1), pick(cp, i2)
    den = w1 + w2
    return best * 4 + i1, best * 4 + i2, w1 / den, w2 / den


def _merge_kernel(yrec_ref, oa_ref, ob_ref, gr0_ref, gr1_ref, ga0_ref, ga1_ref, xa_ref, xb_ref, mod_ref, g2_ref,
                  wrec_ref, watt_ref, wout_ref, wrt_ref, rb_ref,
                  x1_ref, h2_ref, idx_ref, wts_ref,
                  wrec_s, watt_s, wout_s):
    @pl.when(pl.program_id(0) == 0)
    def _():
        wrec_s[...] = wrec_ref[...].astype(BF16)
        watt_s[...] = watt_ref[...].astype(BF16)
        wout_s[...] = wout_ref[...].astype(BF16)

    is_ctx = pl.program_id(0) < T_CTX // MERGE_TM
    args = (yrec_ref, gr0_ref, gr1_ref, ga0_ref, ga1_ref, mod_ref, g2_ref, wrt_ref, rb_ref,
            x1_ref, h2_ref, idx_ref, wts_ref, wrec_s, watt_s, wout_s)

    @pl.when(is_ctx)
    def _():
        _merge_body(oa_ref, xa_ref, *args)

    @pl.when(jnp.logical_not(is_ctx))
    def _():
        _merge_body(ob_ref, xb_ref, *args)


def _merge_body(oatt_ref, x_ref, yrec_ref, gr0_ref, gr1_ref, ga0_ref, ga1_ref, mod_ref, g2_ref, wrt_ref, rb_ref,
                x1_ref, h2_ref, idx_ref, wts_ref, wrec_s, watt_s, wout_s):
    half = D // 2
    b_rec = _dot(yrec_ref[...], wrec_s[...])
    b_att = _dot(oatt_ref[...], watt_s[...])
    m0 = _sigmoid(gr0_ref[...].astype(F32)) * b_rec[:, :half] + _sigmoid(ga0_ref[...].astype(F32)) * b_att[:, :half]
    m1 = _sigmoid(gr1_ref[...].astype(F32)) * b_rec[:, half:] + _sigmoid(ga1_ref[...].astype(F32)) * b_att[:, half:]
    merged = jnp.concatenate([m0, m1], axis=-1).astype(BF16)
    out = _dot(merged, wout_s[...])

    hs = []
    for s in range(MERGE_TM // SEG):
        rows = slice(s * SEG, (s + 1) * SEG)
        m = mod_ref[s]
        x1 = x_ref[rows, :] + m[2:3, :] * out[rows, :]
        x1_ref[rows, :] = x1
        h2 = _norm_mod(x1, g2_ref[...], m[3:4, :], m[4:5, :])
        hs.append(h2)
    h2 = jnp.concatenate(hs, axis=0)
    _store_row_tiles(h2_ref, h2)

    h_hi, h_lo = _split(h2)
    w_hi, w_lo = _split(wrt_ref[...])
    lt = _dot_nt(w_hi, h_hi) + _dot_nt(w_hi, h_lo) + _dot_nt(w_lo, h_hi)
    e1, e2, w1, w2 = _route(lt, rb_ref[...])
    idx_ref[...] = jnp.concatenate([e1, e2], axis=0)
    wts_ref[...] = jnp.concatenate([w1, w2], axis=0)


def _merge(yrec, o_ctx, o_lat, proj, xa, xb, modseg, norm2_g, w_rec_out, w_att_out, w_out, wrt, rbias, l):
    tm = MERGE_TM
    half = D // 2
    gcol = (3 * D + 2 * N_KV * HD) // half
    wspec = pl.BlockSpec((None, D, D), lambda i: (l, 0, 0))
    return pl.pallas_call(
        _merge_kernel,
        grid=(T // tm,),
        in_specs=[pl.BlockSpec((tm, D), lambda i: (i, 0))] + _two_part_specs(tm, T_CTX // tm) + [
            pl.BlockSpec((tm, half), lambda i: (i, gcol)),
            pl.BlockSpec((tm, half), lambda i: (i, gcol + 1)),
            pl.BlockSpec((tm, half), lambda i: (i, gcol + 2)),
            pl.BlockSpec((tm, half), lambda i: (i, gcol + 3)),
        ] + _two_part_specs(tm, T_CTX // tm) + [
            pl.BlockSpec((None, tm // SEG, 8, D), lambda i: (l, i, 0, 0)),
            pl.BlockSpec((None, 1, D), lambda i: (l, 0, 0)),
            wspec, wspec, wspec,
            pl.BlockSpec((N_EXP, D), lambda i: (0, 0)),
            pl.BlockSpec((N_EXP, 1), lambda i: (0, 0)),
        ],
        out_specs=[
            pl.BlockSpec((tm, D), lambda i: (i, 0)),
            pl.BlockSpec((tm * NCH, 128), lambda i: (i, 0)),
            pl.BlockSpec((2, tm), lambda i: (0, i)),
            pl.BlockSpec((2, tm), lambda i: (0, i)),
        ],
        out_shape=[
            jax.ShapeDtypeStruct((T, D), F32),
            jax.ShapeDtypeStruct((T * NCH, 128), F32),
            jax.ShapeDtypeStruct((2, T), jnp.int32),
            jax.ShapeDtypeStruct((2, T), F32),
        ],
        scratch_shapes=[pltpu.VMEM((D, D), BF16)] * 3,
        compiler_params=_cp(("arbitrary",)),
        name="merge",
    )(yrec, o_ctx, o_lat, proj, proj, proj, proj, xa, xb, modseg, norm2_g.reshape(DEPTH, 1, D),
      w_rec_out, w_att_out, w_out, wrt, rbias)


MOE_TM = 512
MOE_NT = 2 * T // MOE_TM + N_EXP
MOE_ROWS = MOE_NT * MOE_TM
META_TILE_E, META_CNT, META_OFF, META_END, META_NT, META_NEXT_E = 0, 1, 2, 3, 4, 5


def _pos_kernel(idx_ref, pos_ref, meta_ref):
    shift = MOE_TM.bit_length() - 1
    idx = idx_ref[...]
    eid = lax.broadcasted_iota(jnp.int32, (N_EXP, T), 0)
    m0 = eid == idx[0:1, :]
    m1 = eid == idx[1:2, :]
    member = jnp.where(m0 | m1, 1.0, 0.0)
    cnt = jnp.sum(member, axis=1, keepdims=True).astype(jnp.int32)
    ntile = jnp.right_shift(cnt + (MOE_TM - 1), shift)
    offs, acc = [], jnp.zeros((1, 1), jnp.int32)
    for e in range(N_EXP):
        offs.append(acc)
        acc = acc + ntile[e:e + 1, :]
    off_t = jnp.concatenate(offs, axis=0)
    end_t = off_t + ntile

    blk = 256
    r_i = lax.broadcasted_iota(jnp.int32, (blk, blk), 0)
    c_i = lax.broadcasted_iota(jnp.int32, (blk, blk), 1)
    upper = jnp.where(r_i <= c_i, 1.0, 0.0).astype(BF16)
    run = (off_t * MOE_TM).astype(F32)
    for j in range(T // blk):
        ls = slice(j * blk, (j + 1) * blk)
        mb = member[:, ls]
        inc = _dot(mb.astype(BF16), upper)
        dest = run + inc - mb
        pos_ref[0:1, ls] = jnp.sum(jnp.where(m0[:, ls], dest, 0.0), axis=0, keepdims=True).astype(jnp.int32)
        pos_ref[1:2, ls] = jnp.sum(jnp.where(m1[:, ls], dest, 0.0), axis=0, keepdims=True).astype(jnp.int32)
        run = run + inc[:, blk - 1:blk]

    lane = lax.broadcasted_iota(jnp.int32, (1, 128), 1)
    zero = jnp.zeros((1, 128), jnp.int32)
    tile_e, cnt_row, off_row, end_row = zero, zero, zero, zero
    for e in range(N_EXP):
        tile_e = tile_e + jnp.where(lane >= end_t[e:e + 1, :], 1, 0)
        here = lane == e
        cnt_row = jnp.where(here, cnt[e:e + 1, :], cnt_row)
        off_row = jnp.where(here, off_t[e:e + 1, :] * MOE_TM, off_row)
        end_row = jnp.where(here, end_t[e:e + 1, :] * MOE_TM, end_row)
    tile_e = jnp.minimum(tile_e, N_EXP - 1)
    nt_row = zero + acc
    next_row = zero
    nxt = jnp.full((1, 1), -1, jnp.int32)
    for e in reversed(range(N_EXP)):
        next_row = jnp.where(lane == e, nxt, next_row)
        nxt = jnp.where(cnt[e:e + 1, :] > 0, e, nxt)
    meta_ref[...] = jnp.concatenate([tile_e, cnt_row, off_row, end_row, nt_row, next_row, zero, zero], axis=0)


def _route_pos(idx):
    return pl.pallas_call(
        _pos_kernel,
        grid=(1,),
        in_specs=[pl.BlockSpec((2, T), lambda i: (0, 0))],
        out_specs=[pl.BlockSpec((2, T), lambda i: (0, 0)), pl.BlockSpec((8, 128), lambda i: (0, 0))],
        out_shape=[jax.ShapeDtypeStruct((2, T), jnp.int32), jax.ShapeDtypeStruct((8, 128), jnp.int32)],
        compiler_params=_cp(("arbitrary",)),
        name="route_pos",
    )(idx)


DISP_TM = 1024
DISP_EXPERTS = N_EXP // (T // DISP_TM)


def _dispatch_kernel(meta_ref, pos_ref, h_ref, z_hbm, xs_hbm, sem):
    i = pl.program_id(0)

    def row_copy(src, src_row, dst_row):
        return pltpu.make_async_copy(src.at[pl.ds(src_row * NCH, NCH), :],
                                     xs_hbm.at[pl.ds(pl.multiple_of(dst_row * NCH, NCH), NCH), :], sem)

    for r in range(DISP_TM):
        row_copy(h_ref, r, pos_ref[0, r]).start(priority=0)
        row_copy(h_ref, r, pos_ref[1, r]).start(priority=1)

    def zero_copies(act):
        for q in range(DISP_EXPERTS):
            e = i * DISP_EXPERTS + q
            s = meta_ref[META_OFF, e] + meta_ref[META_CNT, e]
            npad = meta_ref[META_END, e] - s
            for bit in reversed(range(MOE_TM.bit_length() - 1)):
                size = 1 << bit
                part = jnp.bitwise_and(npad, size)

                @pl.when(part != 0)
                def _():
                    dst = pl.ds(pl.multiple_of(s * NCH, NCH), size * NCH)
                    act(pltpu.make_async_copy(z_hbm.at[pl.ds(0, size * NCH), :], xs_hbm.at[dst, :], sem))

                s = s + part
            tail = meta_ref[META_NT, 0] + e

            @pl.when(tail < MOE_NT)
            def _():
                rows = pl.ds(pl.multiple_of(tail * (MOE_TM * NCH), MOE_TM * NCH), MOE_TM * NCH)
                act(pltpu.make_async_copy(z_hbm, xs_hbm.at[rows, :], sem))

    zero_copies(lambda c: c.start())
    for _ in range(2):
        pltpu.make_async_copy(h_ref, xs_hbm.at[pl.ds(0, DISP_TM * NCH), :], sem).wait()
    zero_copies(lambda c: c.wait())


def _dispatch(meta, pos, h2, zrow):
    return pl.pallas_call(
        _dispatch_kernel,
        grid_spec=pltpu.PrefetchScalarGridSpec(
            num_scalar_prefetch=1,
            grid=(T // DISP_TM,),
            in_specs=[
                pl.BlockSpec((2, DISP_TM), lambda i, meta: (0, i), memory_space=pltpu.SMEM),
                pl.BlockSpec((DISP_TM * NCH, 128), lambda i, meta: (i, 0)),
                pl.BlockSpec((MOE_TM * NCH, 128), lambda i, meta: (0, 0)),
            ],
            out_specs=pl.BlockSpec(memory_space=pl.ANY),
            scratch_shapes=[pltpu.SemaphoreType.DMA],
        ),
        out_shape=jax.ShapeDtypeStruct((MOE_ROWS * NCH, 128), F32),
        compiler_params=_cp(("arbitrary",)),
        name="dispatch",
    )(meta, pos, h2, zrow)


def _experts_kernel(meta_ref, xs_ref, wg_hbm, wu_hbm, wd_hbm, ys_ref,
                    wg_f, wu_f, wd_f, wg_s, wu_s, wd_s, sem, *, l):
    j = pl.program_id(0)
    live = j < meta_ref[META_NT, 0]
    e = meta_ref[META_TILE_E, j]
    e_prev = meta_ref[META_TILE_E, jnp.maximum(j - 1, 0)]

    def fetch(ex):
        return (pltpu.make_async_copy(wg_hbm.at[l, ex], wg_f, sem.at[0]),
                pltpu.make_async_copy(wu_hbm.at[l, ex], wu_f, sem.at[1]),
                pltpu.make_async_copy(wd_hbm.at[l, ex], wd_f, sem.at[2]))

    @pl.when(j == 0)
    def _():
        for c in fetch(e):
            c.start()

    @pl.when(live & ((j == 0) | (e != e_prev)))
    def _():
        for c, dst, src in zip(fetch(e), (wg_s, wu_s, wd_s), (wg_f, wu_f, wd_f)):
            c.wait()
            dst[...] = src[...].astype(BF16)
        nxt = meta_ref[META_NEXT_E, e]

        @pl.when(nxt >= 0)
        def _():
            for c in fetch(nxt):
                c.start()

    @pl.when(live)
    def _():
        x = _load_row_tiles(xs_ref, MOE_TM).astype(BF16)
        y = None
        for hh in range(2):
            cols = slice(hh * (D_EXP // 2), (hh + 1) * (D_EXP // 2))
            g = _dot(x, wg_s[:, cols])
            u = _dot(x, wu_s[:, cols])
            act = (g * _sigmoid(g)) * u
            part = _dot(act.astype(BF16), wd_s[cols, :])
            y = part if y is None else y + part
        _store_row_tiles(ys_ref, y)

    @pl.when(jnp.logical_not(live))
    def _():
        ys_ref[...] = jnp.zeros_like(ys_ref)


def _experts(meta, xs, w_gate_e, w_up_e, w_down_e, l):
    def tile(j, meta):
        return jnp.minimum(j, meta[META_NT, 0] - 1)

    return pl.pallas_call(
        functools.partial(_experts_kernel, l=l),
        grid_spec=pltpu.PrefetchScalarGridSpec(
            num_scalar_prefetch=1,
            grid=(MOE_NT,),
            in_specs=[
                pl.BlockSpec((MOE_TM * NCH, 128), lambda j, meta: (tile(j, meta), 0)),
                pl.BlockSpec(memory_space=pl.ANY),
                pl.BlockSpec(memory_space=pl.ANY),
                pl.BlockSpec(memory_space=pl.ANY),
            ],
            out_specs=pl.BlockSpec((MOE_TM * NCH, 128), lambda j, meta: (j, 0)),
            scratch_shapes=[pltpu.VMEM((D, D_EXP), F32), pltpu.VMEM((D, D_EXP), F32), pltpu.VMEM((D_EXP, D), F32),
                            pltpu.VMEM((D, D_EXP), BF16), pltpu.VMEM((D, D_EXP), BF16), pltpu.VMEM((D_EXP, D), BF16),
                            pltpu.SemaphoreType.DMA((3,))],
        ),
        out_shape=jax.ShapeDtypeStruct((MOE_ROWS * NCH, 128), F32),
        compiler_params=_cp(("arbitrary",)),
        name="experts",
    )(meta, xs, w_gate_e, w_up_e, w_down_e)


COMB_TM = SEG


def _combine_kernel(pos_ref, w_ref, x1_ref, mod_ref, fg_ref, ys_hbm, oa_ref, ob_ref, buf, y_s, sem, *, final):
    i = pl.program_id(0)
    n = pl.num_programs(0) - 1
    n_ctx = T_CTX // COMB_TM

    for s in range(2):
        @pl.when((i < n) & (lax.rem(i, 2) == s))
        def _():
            for r in range(COMB_TM):
                for k in range(2):
                    src = pl.ds(pl.multiple_of(pos_ref[k, r] * NCH, NCH), NCH)
                    pltpu.make_async_copy(ys_hbm.at[src, :], buf.at[s, k, pl.ds(r * NCH, NCH), :],
                                          sem.at[s]).start(priority=k)

    for slot in range(2):
        @pl.when((i > 0) & (lax.rem(i - 1, 2) == slot))
        def _():
            for k in range(2):
                pltpu.make_async_copy(ys_hbm.at[pl.ds(0, COMB_TM * NCH), :], buf.at[slot, k], sem.at[slot]).wait()
            w = w_ref[...]
            y = (w[:, 0:1] * _load_row_tiles(buf.at[slot, 0], COMB_TM)
                 + w[:, 1:2] * _load_row_tiles(buf.at[slot, 1], COMB_TM))
            y_s[...] = y

    @pl.when(i > 0)
    def _():
        x = x1_ref[...] + mod_ref[5:6, :] * y_s[...]
        if final:
            ms = jnp.mean(x * x, axis=-1, keepdims=True)
            x = x * lax.rsqrt(ms + EPS) * fg_ref[...]

        @pl.when(i - 1 < n_ctx)
        def _():
            oa_ref[...] = x

        @pl.when(i - 1 >= n_ctx)
        def _():
            ob_ref[...] = x


def _combine(pos, wts_t, x1, modseg, final_g, ys, l, final):
    n = T // COMB_TM
    n_ctx = T_CTX // COMB_TM

    def done(i):
        return jnp.maximum(i - 1, 0)

    return pl.pallas_call(
        functools.partial(_combine_kernel, final=final),
        grid=(n + 1,),
        in_specs=[
            pl.BlockSpec((2, COMB_TM), lambda i: (0, jnp.minimum(i, n - 1)), memory_space=pltpu.SMEM),
            pl.BlockSpec((COMB_TM, 2), lambda i: (done(i), 0)),
            pl.BlockSpec((COMB_TM, D), lambda i: (done(i), 0)),
            pl.BlockSpec((None, None, 8, D), lambda i: (l, done(i), 0, 0)),
            pl.BlockSpec((1, D), lambda i: (0, 0)),
            pl.BlockSpec(memory_space=pl.ANY),
        ],
        out_specs=[pl.BlockSpec((COMB_TM, D), lambda i: (jnp.minimum(done(i), n_ctx - 1), 0)),
                   pl.BlockSpec((COMB_TM, D), lambda i: (jnp.maximum(done(i) - n_ctx, 0), 0))],
        out_shape=[jax.ShapeDtypeStruct((T_CTX, D), F32), jax.ShapeDtypeStruct((T_LAT, D), F32)],
        scratch_shapes=[pltpu.VMEM((2, 2, COMB_TM * NCH, 128), F32), pltpu.VMEM((COMB_TM, D), F32),
                        pltpu.SemaphoreType.DMA((2,))],
        compiler_params=_cp(("arbitrary",)),
        name="combine",
    )(pos, wts_t, x1, modseg, final_g.reshape(1, D), ys)


def _rope_tables():
    n = DEC_SEQ
    pos_row = np.repeat(np.arange(n // GRID_W, dtype=np.float32), GRID_W)
    pos_col = np.tile(np.arange(GRID_W, dtype=np.float32), n // GRID_W)
    half = HD // 2
    inv_freq = jnp.asarray(ROPE_THETA, F32) ** (-jnp.arange(0, half, 2, dtype=F32) / half)
    ang = jnp.concatenate([jnp.asarray(pos_row)[:, None] * inv_freq,
                           jnp.asarray(pos_col)[:, None] * inv_freq], axis=-1)
    cos, sin = jnp.cos(ang), jnp.sin(ang)
    cos128 = jnp.tile(cos, (1, 4))
    sin128 = jnp.tile(jnp.concatenate([-sin, sin], axis=-1), (1, 2))
    return cos128, sin128


def _head_mean_matrix():
    idx = np.arange(2 * HD)
    same = (idx[:, None] // HD) == (idx[None, :] // HD)
    return jnp.asarray(same.astype(np.float32) / HD, BF16)


_SEG_ROWS = np.array([0] * (T_CTX // SEG) + [1 + b for b in range(DEC_BATCH) for _ in range(DEC_SEQ // SEG)])


def kernel(x_prompt, x_sample, cache_k, cache_v, state_rec, c, c_ctx, w_mod, b_mod, norm1_g, norm2_g, w_in, conv_w, conv_b, rg_wa, rg_ba, rg_wx, rg_bx, rg_lambda, q_norm_g, k_norm_g, w_rec_out, w_att_out, w_out, w_router, router_bias, w_gate_e, w_up_e, w_down_e, final_g):
    xa, xb = x_prompt.reshape(T_CTX, D), x_sample.reshape(T_LAT, D)

    cvecs = jnp.concatenate([c_ctx[None, :], c, jnp.zeros((3, D), F32)], axis=0)
    mods = _mods(cvecs, w_mod, b_mod).reshape(DEPTH, 8, 6, D)
    modseg = jnp.pad(mods[:, _SEG_ROWS], ((0, 0), (0, 0), (0, 2), (0, 0)))

    cos128, sin128 = _rope_tables()
    bd = _head_mean_matrix()
    qg128 = jnp.tile(q_norm_g, (1, 2)).reshape(DEPTH, 1, 2 * HD)
    kg128 = jnp.tile(k_norm_g, (1, 2)).reshape(DEPTH, 1, 2 * HD)
    wg = jnp.concatenate([rg_wa[:, 0], rg_wx[:, 0], rg_wa[:, 1], rg_wx[:, 1]], axis=-1)
    pvec = jnp.stack([rg_ba[:, 0], rg_bx[:, 0], rg_ba[:, 1], rg_bx[:, 1],
                      rg_lambda[:, 0], rg_lambda[:, 1], conv_b, jnp.zeros_like(conv_b)], axis=1)
    wrt = w_router.T
    rbias = router_bias.reshape(N_EXP, 1)
    zrow = jnp.zeros((MOE_TM * NCH, 128), F32)

    caches, new_s = (), []
    for l in range(DEPTH):
        proj = _inproj(xa, xb, modseg, norm1_g, w_in, l)
        h0 = jnp.concatenate([jnp.zeros((T_CTX // UNIT, 2, D), F32), state_rec[:, l]], axis=0)
        yrec, stf, stb = _rec(proj, conv_w, pvec, wg, h0, l)
        o_ctx, kc, vc = _attn_ctx(proj, qg128, kg128, bd, l, prev_caches=caches)
        caches = (kc, vc)
        o_lat = _attn_lat(proj, cache_k, cache_v, qg128, kg128, cos128, sin128, bd, l)
        x1, h2, idx, wts = _merge(yrec, o_ctx, o_lat, proj, xa, xb, modseg, norm2_g,
                                  w_rec_out, w_att_out, w_out, wrt, rbias, l)
        pos, meta = _route_pos(idx)
        xs = _dispatch(meta, pos, h2, zrow)
        ys = _experts(meta, xs, w_gate_e, w_up_e, w_down_e, l)
        xa, xb = _combine(pos, wts.T, x1, modseg, final_g, ys, l, final=(l == DEPTH - 1))
        n_cu = T_CTX // UNIT
        spu = UNIT // SEQ
        hf_last = stf[:n_cu].reshape(n_cu, spu, 2, D)[:, :, 1].reshape(BATCH, D)
        hb_first = stb[:n_cu].reshape(n_cu, spu, 2, D)[:, :, 0].reshape(BATCH, D)
        new_s.append(jnp.stack([hf_last, hb_first], axis=1))

    y_prompt = xa.reshape(BATCH, SEQ, D)
    y_sample = xb.reshape(DEC_BATCH, DEC_SEQ, D)
    return (y_prompt, y_sample, caches[0], caches[1], jnp.stack(new_s, axis=1))
```
